```python
import math
import jax
import jax.numpy as jnp
from jax import lax
import numpy as np

D_MODEL = 2048
BATCH = 8
SEQ = 2048
DEPTH = 1

HEAD_DIM = 128
FOX_HEADS = 8
RET_HEADS = 8
FOX_WIDTH = FOX_HEADS * HEAD_DIM
RET_WIDTH = RET_HEADS * HEAD_DIM
N_BRANCH = 2
Q_BLOCK = 128
RET_CHUNK = 128
N_GROUPS = 4
EXPERTS_PER_GROUP = 8
N_EXPERTS = N_GROUPS * EXPERTS_PER_GROUP
TOP_K = 2
D_EXPERT = 512
EXPERT_BLOCK = 128
ROPE_BASE = 10000.0
EPS = 1e-6
IN_WIDTH = 3 * FOX_WIDTH + FOX_HEADS + 4 * RET_WIDTH + N_BRANCH * D_MODEL

kernel_name = 'fox_retnet_hier_moe_block'


def rmsnorm(x, w):
    xf = x.astype(jnp.float32)
    y = xf * lax.rsqrt(jnp.mean(xf * xf, axis=-1, keepdims=True) + EPS)
    return (y * w.astype(jnp.float32)).astype(x.dtype)


def to_heads(t, n_heads):
    b, s, _ = t.shape
    return t.reshape(b, s, n_heads, HEAD_DIM).transpose(0, 2, 1, 3)


def from_heads(t):
    b, h, s, d = t.shape
    return t.transpose(0, 2, 1, 3).reshape(b, s, h * d)


def rotary(t, positions):
    half = HEAD_DIM // 2
    inv_freq = ROPE_BASE ** (-jnp.arange(half, dtype=jnp.float32) / half)
    ang = positions.astype(jnp.float32)[:, None, :, None] * inv_freq
    cos, sin = jnp.cos(ang), jnp.sin(ang)
    tf = t.astype(jnp.float32)
    t1, t2 = tf[..., :half], tf[..., half:]
    return jnp.concatenate([t1 * cos - t2 * sin, t1 * sin + t2 * cos], axis=-1).astype(t.dtype)


def forgetting_attention(q, k, v, log_f):
    b, h, s, d = q.shape
    nq = s // Q_BLOCK
    scale = 1.0 / math.sqrt(d)
    c = jnp.cumsum(log_f, axis=-1)
    qb = q.reshape(b, h, nq, Q_BLOCK, d).transpose(2, 0, 1, 3, 4)
    cb = c.reshape(b, h, nq, Q_BLOCK).transpose(2, 0, 1, 3)
    key_pos = jnp.arange(s)

    def block(args):
        qi, ci, i = args
        logits = jnp.einsum('bhqd,bhkd->bhqk', qi, k).astype(jnp.float32) * scale
        logits = logits + ci[..., None] - c[:, :, None, :]
        q_pos = i * Q_BLOCK + jnp.arange(Q_BLOCK)
        mask = key_pos[None, :] <= q_pos[:, None]
        logits = jnp.where(mask[None, None], logits, jnp.float32(-1e30))
        p = jax.nn.softmax(logits, axis=-1)
        return jnp.einsum('bhqk,bhkd->bhqd', p.astype(v.dtype), v)

    out = lax.map(block, (qb, cb, jnp.arange(nq)))
    return out.transpose(1, 2, 0, 3, 4).reshape(b, h, s, d)


def retention_chunkwise(q, k, v):
    b, h, s, d = q.shape
    nc = s // RET_CHUNK
    log_gamma = jnp.log1p(-(2.0 ** (-5.0 - jnp.arange(h, dtype=jnp.float32))))
    n = jnp.arange(RET_CHUNK, dtype=jnp.float32)
    diff = n[:, None] - n[None, :]
    decay_in = jnp.where(diff[None] >= 0, jnp.exp(diff[None] * log_gamma[:, None, None]), 0.0)
    xi = jnp.exp((n[None, :] + 1.0) * log_gamma[:, None])
    zeta = jnp.exp((RET_CHUNK - 1.0 - n[None, :]) * log_gamma[:, None])
    g_chunk = jnp.exp(RET_CHUNK * log_gamma)
    qf = q.astype(jnp.float32) / math.sqrt(d)
    kf = k.astype(jnp.float32)
    vf = v.astype(jnp.float32)
    chunks = lambda t: t.reshape(b, h, nc, RET_CHUNK, d).transpose(2, 0, 1, 3, 4)

    def step(state, inp):
        qc, kc, vc = inp
        inner = jnp.einsum('bhnd,bhmd->bhnm', qc, kc) * decay_in[None]
        o = jnp.einsum('bhnm,bhme->bhne', inner, vc)
        o = o + jnp.einsum('bhnd,bhde->bhne', qc, state) * xi[None, :, :, None]
        state = state * g_chunk[None, :, None, None] + jnp.einsum(
            'bhmd,bhme->bhde', kc * zeta[None, :, :, None], vc)
        return state, o

    state0 = jnp.zeros((b, h, d, d), jnp.float32)
    _, out = lax.scan(step, state0, (chunks(qf), chunks(kf), chunks(vf)))
    return out.transpose(1, 2, 0, 3, 4).reshape(b, h, s, d)


def hierarchical_moe(h, w_rg, b_rg, w_re, b_re, w1, w3, w2):
    t_tok, d = h.shape
    g_logits = (h @ w_rg + b_rg).astype(jnp.float32)
    g_prob = jax.nn.softmax(g_logits, axis=-1)
    g_sel = jnp.argmax(g_prob, axis=-1)
    p_group = jnp.take_along_axis(g_prob, g_sel[:, None], axis=-1)[:, 0]
    e_logits = (h @ w_re + b_re).astype(jnp.float32).reshape(t_tok, N_GROUPS, EXPERTS_PER_GROUP)
    e_logits = jnp.take_along_axis(e_logits, g_sel[:, None, None], axis=1)[:, 0]
    e_prob = jax.nn.softmax(e_logits, axis=-1)
    top_p, top_i = lax.top_k(e_prob, TOP_K)
    top_p = top_p / jnp.sum(top_p, axis=-1, keepdims=True)
    weights = p_group[:, None] * top_p
    expert_id = g_sel[:, None] * EXPERTS_PER_GROUP + top_i

    m = t_tok * TOP_K
    e_flat = expert_id.reshape(m).astype(jnp.int32)
    w_flat = weights.reshape(m)
    tok_flat = jnp.repeat(jnp.arange(t_tok, dtype=jnp.int32), TOP_K)
    order = jnp.argsort(e_flat)
    e_s, tok_s, w_s = e_flat[order], tok_flat[order], w_flat[order]
    counts = jnp.zeros((N_EXPERTS,), jnp.int32).at[e_flat].add(1)
    padded = (counts + EXPERT_BLOCK - 1) // EXPERT_BLOCK * EXPERT_BLOCK
    starts = jnp.cumsum(counts) - counts
    pends = jnp.cumsum(padded)
    pstarts = pends - padded
    dest = pstarts[e_s] + (jnp.arange(m, dtype=jnp.int32) - starts[e_s])
    cap = m + N_EXPERTS * EXPERT_BLOCK
    n_blk = cap // EXPERT_BLOCK
    buf_tok = jnp.full((cap,), t_tok, jnp.int32).at[dest].set(tok_s)
    buf_w = jnp.zeros((cap,), h.dtype).at[dest].set(w_s.astype(h.dtype))
    blk_expert = jnp.clip(jnp.searchsorted(pends, jnp.arange(n_blk, dtype=jnp.int32) * EXPERT_BLOCK,
                                           side='right'), 0, N_EXPERTS - 1)
    h_pad = jnp.concatenate([h, jnp.zeros((1, d), h.dtype)], axis=0)
    xb = h_pad[buf_tok].reshape(n_blk, EXPERT_BLOCK, d)

    def expert_block(args):
        xi, e = args
        return (jax.nn.silu(xi @ w1[e]) * (xi @ w3[e])) @ w2[e]

    yb = lax.map(expert_block, (xb, blk_expert)).reshape(cap, d) * buf_w[:, None]
    return jnp.zeros((t_tok + 1, d), h.dtype).at[buf_tok].add(yb)[:t_tok]


def setup_inputs(seed: int = 0) -> dict:
    key = jax.random.key(seed)
    ks = jax.random.split(key, 20)
    f32 = jnp.float32
    nrm = lambda k, shape, scale: jax.random.normal(k, shape, f32) * scale
    x = jax.random.normal(ks[0], (BATCH, SEQ, D_MODEL), f32)
    offsets = jax.random.randint(ks[1], (BATCH, 1), 0, 1024, dtype=jnp.int32)
    positions = offsets + jnp.arange(SEQ, dtype=jnp.int32)[None, :]
    return {
        'x': x,
        'positions': positions,
        'norm_mix_w': 1.0 + nrm(ks[2], (DEPTH, D_MODEL), 0.02),
        'w_in': nrm(ks[3], (DEPTH, D_MODEL, IN_WIDTH), D_MODEL ** -0.5),
        'fox_b_f': 2.0 + nrm(ks[4], (DEPTH, FOX_HEADS), 0.1),
        'ret_gn_w': 1.0 + nrm(ks[5], (DEPTH, RET_WIDTH), 0.02),
        'w_branch': nrm(ks[6], (DEPTH, N_BRANCH, FOX_WIDTH, D_MODEL), FOX_WIDTH ** -0.5),
        'b_gate': nrm(ks[7], (DEPTH, N_BRANCH, D_MODEL), 0.01),
        'w_out': nrm(ks[8], (DEPTH, D_MODEL, D_MODEL), D_MODEL ** -0.5),
        'norm_ffn_w': 1.0 + nrm(ks[9], (DEPTH, D_MODEL), 0.02),
        'w_router_group': nrm(ks[10], (DEPTH, D_MODEL, N_GROUPS), D_MODEL ** -0.5),
        'b_router_group': nrm(ks[11], (DEPTH, N_GROUPS), 0.01),
        'w_router_expert': nrm(ks[12], (DEPTH, D_MODEL, N_EXPERTS), D_MODEL ** -0.5),
        'b_router_expert': nrm(ks[13], (DEPTH, N_EXPERTS), 0.01),
        'w1': nrm(ks[14], (DEPTH, N_EXPERTS, D_MODEL, D_EXPERT), D_MODEL ** -0.5),
        'w3': nrm(ks[15], (DEPTH, N_EXPERTS, D_MODEL, D_EXPERT), D_MODEL ** -0.5),
        'w2': nrm(ks[16], (DEPTH, N_EXPERTS, D_EXPERT, D_MODEL), D_EXPERT ** -0.5),
        'norm_final_w': 1.0 + nrm(ks[17], (D_MODEL,), 0.02),
    }


def reference(x, positions, norm_mix_w, w_in, fox_b_f, ret_gn_w, w_branch, b_gate, w_out,
              norm_ffn_w, w_router_group, b_router_group, w_router_expert, b_router_expert,
              w1, w3, w2, norm_final_w):
    b, s, d = x.shape
    split_at = [FOX_WIDTH, 2 * FOX_WIDTH, 3 * FOX_WIDTH, 3 * FOX_WIDTH + FOX_HEADS,
                3 * FOX_WIDTH + FOX_HEADS + RET_WIDTH, 3 * FOX_WIDTH + FOX_HEADS + 2 * RET_WIDTH,
                3 * FOX_WIDTH + FOX_HEADS + 3 * RET_WIDTH, 3 * FOX_WIDTH + FOX_HEADS + 4 * RET_WIDTH]
    h = x
    for l in range(DEPTH):
        xn = rmsnorm(h, norm_mix_w[l])
        proj = xn @ w_in[l]
        fq, fk, fv, f_logit, rq, rk, rv, rg, gate_logit = jnp.split(proj, split_at, axis=-1)
        log_f = jax.nn.log_sigmoid((f_logit + fox_b_f[l]).astype(jnp.float32)).transpose(0, 2, 1)
        fox_o = from_heads(forgetting_attention(to_heads(fq, FOX_HEADS), to_heads(fk, FOX_HEADS),
                                                to_heads(fv, FOX_HEADS), log_f))
        ret = retention_chunkwise(rotary(to_heads(rq, RET_HEADS), positions),
                                  rotary(to_heads(rk, RET_HEADS), positions),
                                  to_heads(rv, RET_HEADS))
        mu = jnp.mean(ret, axis=-1, keepdims=True)
        var = jnp.mean(jnp.square(ret - mu), axis=-1, keepdims=True)
        ret = from_heads((ret - mu) * lax.rsqrt(var + EPS)) * ret_gn_w[l].astype(jnp.float32)
        ret_o = (jax.nn.silu(rg.astype(jnp.float32)) * ret).astype(h.dtype)
        branches = jnp.stack([fox_o, ret_o], axis=2)
        branch_d = jnp.einsum('bsnc,ncd->bsnd', branches, w_branch[l])
        gates = jax.nn.sigmoid(gate_logit.reshape(b, s, N_BRANCH, d) + b_gate[l])
        merged = jnp.sum(gates * branch_d, axis=2)
        h = h + merged @ w_out[l]
        hn = rmsnorm(h, norm_ffn_w[l]).reshape(b * s, d)
        moe = hierarchical_moe(hn, w_router_group[l], b_router_group[l], w_router_expert[l],
                               b_router_expert[l], w1[l], w3[l], w2[l])
        h = h + moe.reshape(b, s, d)
    return rmsnorm(h, norm_final_w)
```

```python
import functools
import math

import numpy as np
import jax
import jax.numpy as jnp
from jax import lax
from jax.experimental import pallas as pl
from jax.experimental.pallas import tpu as pltpu

D_MODEL = 2048
HEAD_DIM = 128
FOX_HEADS = 8
RET_HEADS = 8
FOX_WIDTH = FOX_HEADS * HEAD_DIM
RET_WIDTH = RET_HEADS * HEAD_DIM
N_BRANCH = 2
N_GROUPS = 4
EXPERTS_PER_GROUP = 8
N_EXPERTS = N_GROUPS * EXPERTS_PER_GROUP
TOP_K = 2
D_EXPERT = 512
ROPE_BASE = 10000.0
EPS = 1e-6

LANES = 128
NEG_BIG = -1e30
MAIN_WIDTH = 3 * FOX_WIDTH + 4 * RET_WIDTH + N_BRANCH * D_MODEL
COL_GATE = 0
COL_FQ = N_BRANCH * D_MODEL // HEAD_DIM
COL_FK = COL_FQ + FOX_HEADS
COL_FV = COL_FK + FOX_HEADS
COL_RQ = COL_FV + FOX_HEADS
COL_RK = COL_RQ + RET_HEADS
COL_RV = COL_RK + RET_HEADS
COL_RG = COL_RV + RET_HEADS

ATTN_BLOCK = 256
RET_CHUNK = 256
MOE_BLOCK = 256
VMEM_LIMIT = 56 * 1024 * 1024

f32 = jnp.float32
bf16 = jnp.bfloat16


def _cparams(sem, vmem=VMEM_LIMIT):
    return pltpu.CompilerParams(dimension_semantics=sem, vmem_limit_bytes=vmem)


def _rope_kernel(pos_ref, freq_ref, sgn_ref, cos_ref, sin_ref):
    ang = pos_ref[0].astype(f32) * freq_ref[...]
    cos_ref[0] = jnp.cos(ang)
    sin_ref[0] = jnp.sin(ang) * sgn_ref[...]


def _rope_tables(positions):
    b, s = positions.shape
    half = HEAD_DIM // 2
    inv = (np.float32(ROPE_BASE) ** (-np.arange(half, dtype=np.float32) / np.float32(half))).astype(np.float32)
    freq = jnp.asarray(np.concatenate([inv, inv])[None, :])
    sgn = jnp.asarray(np.concatenate([-np.ones(half, np.float32), np.ones(half, np.float32)])[None, :])
    out = jax.ShapeDtypeStruct((b, s, HEAD_DIM), f32)
    return pl.pallas_call(
        _rope_kernel,
        grid=(b,),
        in_specs=[pl.BlockSpec((1, s, 1), lambda i: (i, 0, 0)),
                  pl.BlockSpec((1, HEAD_DIM), lambda i: (0, 0)),
                  pl.BlockSpec((1, HEAD_DIM), lambda i: (0, 0))],
        out_specs=[pl.BlockSpec((1, s, HEAD_DIM), lambda i: (i, 0, 0)),
                   pl.BlockSpec((1, s, HEAD_DIM), lambda i: (i, 0, 0))],
        out_shape=[out, out],
        compiler_params=_cparams(("parallel",)),
        name="rope_tables",
    )(positions.reshape(b, s, 1), freq, sgn)


def _inproj_kernel(x_ref, nw_ref, w_ref, wf_ref, o_ref, f_ref, xn_ref):
    @pl.when(pl.program_id(1) == 0)
    def _():
        x = x_ref[...]
        ms = jnp.mean(x * x, axis=-1, keepdims=True)
        xn = (x * lax.rsqrt(ms + EPS) * nw_ref[...]).astype(bf16)
        xn_ref[...] = xn
        f_ref[...] = jnp.dot(xn, wf_ref[...], preferred_element_type=f32)

    o_ref[...] = jnp.dot(xn_ref[...], w_ref[...], preferred_element_type=f32).astype(o_ref.dtype)


def _in_projection(x2, norm_w, w_main, w_f):
    t, d = x2.shape
    n = w_main.shape[1]
    tm = min(1024, t)
    tn = 1024
    return pl.pallas_call(
        _inproj_kernel,
        grid=(t // tm, n // tn),
        in_specs=[pl.BlockSpec((tm, d), lambda i, j: (i, 0)),
                  pl.BlockSpec((1, d), lambda i, j: (0, 0)),
                  pl.BlockSpec((d, tn), lambda i, j: (0, j)),
                  pl.BlockSpec((d, LANES), lambda i, j: (0, 0))],
        out_specs=[pl.BlockSpec((tm, tn), lambda i, j: (i, j)),
                   pl.BlockSpec((tm, LANES), lambda i, j: (i, 0))],
        out_shape=[jax.ShapeDtypeStruct((t, n), bf16), jax.ShapeDtypeStruct((t, LANES), f32)],
        scratch_shapes=[pltpu.VMEM((tm, d), bf16)],
        compiler_params=_cparams(("parallel", "arbitrary")),
        name="in_projection",
    )(x2, norm_w, w_main, w_f)


def _fcum_kernel(f_ref, b_ref, c_ref):
    z = f_ref[...] + b_ref[...]
    ls = jnp.minimum(z, 0.0) - jnp.log1p(jnp.exp(-jnp.abs(z)))
    c = ls.T[:FOX_HEADS, :]
    s = c.shape[1]
    lane = lax.broadcasted_iota(jnp.int32, c.shape, 1)
    sh = 1
    while sh < s:
        c = c + jnp.where(lane >= sh, pltpu.roll(c, sh, axis=1), 0.0)
        sh *= 2
    c_ref[0] = c


def _forget_cumsum(f_logit, bias, b, s):
    return pl.pallas_call(
        _fcum_kernel,
        grid=(b,),
        in_specs=[pl.BlockSpec((s, LANES), lambda i: (i, 0)),
                  pl.BlockSpec((1, LANES), lambda i: (0, 0))],
        out_specs=pl.BlockSpec((1, FOX_HEADS, s), lambda i: (i, 0, 0)),
        out_shape=jax.ShapeDtypeStruct((b, FOX_HEADS, s), f32),
        compiler_params=_cparams(("parallel",)),
        name="forget_cumsum",
    )(f_logit, bias)


def _fox_kernel(q_ref, k_ref, v_ref, c_ref, o_ref, *, tq):
    s_len = q_ref.shape[0]
    nq = s_len // tq
    row = lax.broadcasted_iota(jnp.int32, (tq, tq), 0)
    col = lax.broadcasted_iota(jnp.int32, (tq, tq), 1)
    causal = row >= col

    def kv_step(q, kb, carry, masked):
        m, l, acc = carry
        c0 = pl.multiple_of(kb * tq, tq)
        k = k_ref[pl.ds(c0, tq), :]
        v = v_ref[pl.ds(c0, tq), :]
        s = lax.dot_general(q, k, (((1,), (1,)), ((), ())), preferred_element_type=f32)
        s = s - c_ref[0, kb]
        if masked:
            s = jnp.where(causal, s, NEG_BIG)
        m_new = jnp.maximum(m, jnp.max(s, axis=-1, keepdims=True))
        alpha = jnp.exp(m - m_new)
        p = jnp.exp(s - m_new)
        l = alpha * l + jnp.sum(p, axis=-1, keepdims=True)
        acc = alpha * acc + jnp.dot(p.astype(bf16), v, preferred_element_type=f32)
        return m_new, l, acc

    def q_block(qi, _):
        r0 = pl.multiple_of(qi * tq, tq)
        q = q_ref[pl.ds(r0, tq), :]
        init = (jnp.full((tq, 1), NEG_BIG, f32), jnp.zeros((tq, 1), f32), jnp.zeros((tq, HEAD_DIM), f32))
        carry = lax.fori_loop(0, qi, lambda kb, c: kv_step(q, kb, c, False), init)
        m, l, acc = kv_step(q, qi, carry, True)
        o_ref[pl.ds(r0, tq), :] = (acc / l).astype(o_ref.dtype)
        return 0

    lax.fori_loop(0, nq, q_block, 0)


def _fox_attention(proj, c_blk, b, s):
    tq = min(ATTN_BLOCK, s)
    nq = s // tq
    h = FOX_HEADS
    return pl.pallas_call(
        functools.partial(_fox_kernel, tq=tq),
        grid=(b, h),
        in_specs=[pl.BlockSpec((s, HEAD_DIM), lambda i, j: (i, COL_FQ + j)),
                  pl.BlockSpec((s, HEAD_DIM), lambda i, j: (i, COL_FK + j)),
                  pl.BlockSpec((s, HEAD_DIM), lambda i, j: (i, COL_FV + j)),
                  pl.BlockSpec((1, nq, 1, tq), lambda i, j: (i * FOX_HEADS + j, 0, 0, 0))],
        out_specs=pl.BlockSpec((s, HEAD_DIM), lambda i, j: (i, j)),
        out_shape=jax.ShapeDtypeStruct((b * s, FOX_WIDTH), bf16),
        compiler_params=_cparams(("parallel", "parallel")),
        name="fox_attention",
    )(proj, proj, proj, c_blk)


def _ret_kernel(q_ref, k_ref, v_ref, g_ref, cos_ref, sin_ref, dec_ref, xi_ref, zeta_ref, gc_ref, gnw_ref,
                o_ref, st_ref, *, chunk):
    s_len = q_ref.shape[0]
    nc = s_len // chunk
    st_ref[...] = jnp.zeros_like(st_ref)

    def body(ci, _):
        r0 = pl.multiple_of(ci * chunk, chunk)
        rows = pl.ds(r0, chunk)
        cos = cos_ref[0, rows, :]
        sin = sin_ref[0, rows, :]

        def rot(t):
            t = t.astype(f32)
            return t * cos + pltpu.roll(t, HEAD_DIM // 2, axis=1) * sin

        q = rot(q_ref[rows, :])
        k = rot(k_ref[rows, :])
        v = v_ref[rows, :]
        qb = q.astype(bf16)
        inner = lax.dot_general(qb, k.astype(bf16), (((1,), (1,)), ((), ())), preferred_element_type=f32)
        inner = inner * dec_ref[0]
        state = st_ref[...]
        o = jnp.dot(inner.astype(bf16), v, preferred_element_type=f32)
        o = o + jnp.dot(qb, state.astype(bf16), preferred_element_type=f32) * xi_ref[0]
        kz = (k * zeta_ref[0]).astype(bf16)
        st_ref[...] = state * gc_ref[0] + lax.dot_general(kz, v, (((0,), (0,)), ((), ())),
                                                           preferred_element_type=f32)
        mu = jnp.mean(o, axis=-1, keepdims=True)
        oc = o - mu
        var = jnp.mean(oc * oc, axis=-1, keepdims=True)
        y = oc * lax.rsqrt(var + EPS) * gnw_ref[...]
        g = g_ref[rows, :].astype(f32)
        o_ref[rows, :] = (g * jax.nn.sigmoid(g) * y).astype(o_ref.dtype)
        return 0

    lax.fori_loop(0, nc, body, 0)


def _retention_constants(chunk):
    hh = np.arange(RET_HEADS, dtype=np.float64)
    log_gamma = np.log1p(-(2.0 ** (-5.0 - hh)))
    n = np.arange(chunk, dtype=np.float64)
    diff = n[:, None] - n[None, :]
    decay = np.where(diff[None] >= 0, np.exp(diff[None] * log_gamma[:, None, None]), 0.0)
    xi = np.exp((n[None, :] + 1.0) * log_gamma[:, None])
    zeta = np.exp((chunk - 1.0 - n[None, :]) * log_gamma[:, None])
    g_chunk = np.exp(chunk * log_gamma)
    bc = lambda a: np.broadcast_to(a[:, :, None], (RET_HEADS, chunk, HEAD_DIM))
    gc = np.broadcast_to(g_chunk[:, None, None], (RET_HEADS, 1, HEAD_DIM))
    to = lambda a: jnp.asarray(np.ascontiguousarray(a), dtype=f32)
    return to(decay), to(bc(xi)), to(bc(zeta)), to(gc)


def _retention(proj, cos, sin, gn_w, b, s):
    chunk = min(RET_CHUNK, s)
    decay, xi, zeta, gc = _retention_constants(chunk)
    h = RET_HEADS
    head = lambda col: pl.BlockSpec((s, HEAD_DIM), lambda i, j: (i, col + j))
    per_head = lambda shape: pl.BlockSpec(shape, lambda i, j: (j, 0, 0))
    return pl.pallas_call(
        functools.partial(_ret_kernel, chunk=chunk),
        grid=(b, h),
        in_specs=[head(COL_RQ), head(COL_RK), head(COL_RV), head(COL_RG),
                  pl.BlockSpec((1, s, HEAD_DIM), lambda i, j: (i, 0, 0)),
                  pl.BlockSpec((1, s, HEAD_DIM), lambda i, j: (i, 0, 0)),
                  per_head((1, chunk, chunk)), per_head((1, chunk, HEAD_DIM)), per_head((1, chunk, HEAD_DIM)),
                  per_head((1, 1, HEAD_DIM)),
                  pl.BlockSpec((1, HEAD_DIM), lambda i, j: (0, j))],
        out_specs=pl.BlockSpec((s, HEAD_DIM), lambda i, j: (i, j)),
        out_shape=jax.ShapeDtypeStruct((b * s, RET_WIDTH), bf16),
        scratch_shapes=[pltpu.VMEM((HEAD_DIM, HEAD_DIM), f32)],
        compiler_params=_cparams(("parallel", "parallel")),
        name="retention",
    )(proj, proj, proj, proj, cos, sin, decay, xi, zeta, gc, gn_w)


def _route(logits):
    lane = lax.broadcasted_iota(jnp.int32, logits.shape, 1).astype(f32)
    far = jnp.float32(1e9)
    gmask = lane < N_GROUPS
    gl = jnp.where(gmask, logits, NEG_BIG)
    gmax = jnp.max(gl, axis=-1, keepdims=True)
    gsel = jnp.min(jnp.where(gl == gmax, lane, far), axis=-1, keepdims=True)
    p_group = 1.0 / jnp.sum(jnp.where(gmask, jnp.exp(gl - gmax), 0.0), axis=-1, keepdims=True)
    lo = N_GROUPS + EXPERTS_PER_GROUP * gsel
    emask = (lane >= lo) & (lane < lo + EXPERTS_PER_GROUP)
    el = jnp.where(emask, logits, NEG_BIG)
    m1 = jnp.max(el, axis=-1, keepdims=True)
    i1 = jnp.min(jnp.where(el == m1, lane, far), axis=-1, keepdims=True)
    el2 = jnp.where(lane == i1, NEG_BIG, el)
    m2 = jnp.max(el2, axis=-1, keepdims=True)
    i2 = jnp.min(jnp.where(el2 == m2, lane, far), axis=-1, keepdims=True)
    t = jnp.exp(m2 - m1)
    w1 = p_group / (1.0 + t)
    w2 = p_group * t / (1.0 + t)
    out = jnp.where(lane == 0, i1 - N_GROUPS, 0.0)
    out = jnp.where(lane == 1, i2 - N_GROUPS, out)
    out = jnp.where(lane == 2, w1, out)
    out = jnp.where(lane == 3, w2, out)
    return out


def _merge_kernel(fo_ref, ro_ref, g0_ref, g1_ref, bg_ref, x_ref, wb_ref, wo_ref, nfw_ref, wrh_ref, wrl_ref, br_ref,
                  h_ref, route_ref, mg_ref, *, ncol):
    d = x_ref.shape[1]
    cw = d // ncol
    fo = fo_ref[...]
    ro = ro_ref[...]
    for c in range(ncol):
        cs = slice(c * cw, (c + 1) * cw)
        bd0 = jnp.dot(fo, wb_ref[0, :, cs], preferred_element_type=f32)
        bd1 = jnp.dot(ro, wb_ref[1, :, cs], preferred_element_type=f32)
        ga = jax.nn.sigmoid(g0_ref[:, cs].astype(f32) + bg_ref[0:1, cs])
        gb = jax.nn.sigmoid(g1_ref[:, cs].astype(f32) + bg_ref[1:2, cs])
        mg_ref[:, cs] = (ga * bd0 + gb * bd1).astype(bf16)
    h = x_ref[...] + jnp.dot(mg_ref[...], wo_ref[...], preferred_element_type=f32)
    h_ref[...] = h
    hn = h * lax.rsqrt(jnp.mean(h * h, axis=-1, keepdims=True) + EPS) * nfw_ref[...]
    hi = hn.astype(bf16)
    lo = (hn - hi.astype(f32)).astype(bf16)
    logits = (jnp.dot(hi, wrh_ref[...], preferred_element_type=f32)
              + jnp.dot(lo, wrh_ref[...], preferred_element_type=f32)
              + jnp.dot(hi, wrl_ref[...], preferred_element_type=f32)) + br_ref[...]
    route_ref[...] = _route(logits)


def _merge(fox_o, ret_o, proj, b_gate, x2, w_branch, w_out, nfw, wr_hi, wr_lo, b_router):
    t, d = x2.shape
    tm = min(256, t)
    gate0 = COL_GATE * HEAD_DIM // d
    const = lambda shape: pl.BlockSpec(shape, lambda i: tuple(0 for _ in shape), pipeline_mode=pl.Buffered(1))
    return pl.pallas_call(
        functools.partial(_merge_kernel, ncol=4),
        grid=(t // tm,),
        in_specs=[pl.BlockSpec((tm, FOX_WIDTH), lambda i: (i, 0)),
                  pl.BlockSpec((tm, RET_WIDTH), lambda i: (i, 0)),
                  pl.BlockSpec((tm, d), lambda i: (i, gate0)),
                  pl.BlockSpec((tm, d), lambda i: (i, gate0 + 1)),
                  const((N_BRANCH, d)),
                  pl.BlockSpec((tm, d), lambda i: (i, 0)),
                  const((N_BRANCH, FOX_WIDTH, d)),
                  const((d, d)),
                  const((1, d)),
                  const((d, LANES)), const((d, LANES)), const((1, LANES))],
        out_specs=[pl.BlockSpec((tm, d), lambda i: (i, 0)),
                   pl.BlockSpec((tm, LANES), lambda i: (i, 0))],
        out_shape=[jax.ShapeDtypeStruct((t, d), f32), jax.ShapeDtypeStruct((t, LANES), f32)],
        scratch_shapes=[pltpu.VMEM((tm, d), bf16)],
        compiler_params=_cparams(("parallel",)),
        name="merge_outproj_router",
    )(fox_o, ret_o, proj, proj, b_gate, x2, w_branch, w_out, nfw, wr_hi, wr_lo, b_router)


def _rank_kernel(route_ref, rank_ref, cnt_ref, carry_ref):
    @pl.when(pl.program_id(0) == 0)
    def _():
        carry_ref[...] = jnp.zeros_like(carry_ref)

    r = route_ref[...]
    tb = r.shape[0]
    lane = lax.broadcasted_iota(jnp.int32, r.shape, 1).astype(f32)
    o0 = (lane == r[:, 0:1]).astype(f32)
    o1 = (lane == r[:, 1:2]).astype(f32)
    o = o0 + o1
    row = lax.broadcasted_iota(jnp.int32, (tb, tb), 0)
    col = lax.broadcasted_iota(jnp.int32, (tb, tb), 1)
    lower = jnp.where(col < row, 1.0, 0.0).astype(bf16)
    before = jnp.dot(lower, o.astype(bf16), preferred_element_type=f32) + carry_ref[...]
    r0 = jnp.sum(before * o0, axis=-1, keepdims=True)
    r1 = jnp.sum(before * o1, axis=-1, keepdims=True)
    rank_ref[...] = jnp.where(lane == 0, r0, jnp.where(lane == 1, r1, 0.0)).astype(jnp.int32)
    carry_ref[...] += jnp.sum(o, axis=0, keepdims=True)
    cnt_ref[...] = carry_ref[...]


def _dispatch_ranks(route):
    t = route.shape[0]
    tb = min(512, t)
    return pl.pallas_call(
        _rank_kernel,
        grid=(t // tb,),
        in_specs=[pl.BlockSpec((tb, LANES), lambda i: (i, 0))],
        out_specs=[pl.BlockSpec((tb, LANES), lambda i: (i, 0)),
                   pl.BlockSpec((1, LANES), lambda i: (0, 0))],
        out_shape=[jax.ShapeDtypeStruct((t, LANES), jnp.int32), jax.ShapeDtypeStruct((1, LANES), f32)],
        scratch_shapes=[pltpu.VMEM((1, LANES), f32)],
        compiler_params=_cparams(("arbitrary",)),
        name="dispatch_ranks",
    )(route)


def _row_gather_start(idx_ref, base, n_rows, src_hbm, dst, sem, unroll=8):
    def body(g, _):
        for u in range(unroll):
            r = g * unroll + u
            t = idx_ref[base + r]
            pltpu.make_async_copy(src_hbm.at[pl.ds(t, 1)], dst.at[pl.ds(r, 1)], sem).start()
        return 0
    lax.fori_loop(0, n_rows // unroll, body, 0)


def _moe_kernel(be_ref, nu_ref, tok_ref, h_hbm, nfw_ref, w1_ref, w3_ref, w2_ref, y_ref, xbuf, sem, *, bm):
    i = pl.program_id(0)
    n_used = nu_ref[0]
    slot = i % 2

    @pl.when(i == 0)
    def _():
        _row_gather_start(tok_ref, 0, bm, h_hbm, xbuf.at[0], sem.at[0])

    @pl.when(i + 1 < n_used)
    def _():
        _row_gather_start(tok_ref, (i + 1) * bm, bm, h_hbm, xbuf.at[1 - slot], sem.at[1 - slot])

    @pl.when(i < n_used)
    def _():
        pltpu.make_async_copy(h_hbm.at[pl.ds(0, bm)], xbuf.at[slot], sem.at[slot]).wait()
        h = xbuf[slot]
        hn = (h * lax.rsqrt(jnp.mean(h * h, axis=-1, keepdims=True) + EPS) * nfw_ref[...]).astype(bf16)
        a = jnp.dot(hn, w1_ref[0], preferred_element_type=f32)
        b = jnp.dot(hn, w3_ref[0], preferred_element_type=f32)
        mid = (a * jax.nn.sigmoid(a) * b).astype(bf16)
        y_ref[...] = jnp.dot(mid, w2_ref[0], preferred_element_type=f32)

    @pl.when(i >= n_used)
    def _():
        y_ref[...] = jnp.zeros_like(y_ref)


def _expert_mlp(blk_expert, n_used, buf_tok, h, nfw, w1, w3, w2, cap):
    t, d = h.shape
    bm = MOE_BLOCK
    n_blk = cap // bm
    grid_spec = pltpu.PrefetchScalarGridSpec(
        num_scalar_prefetch=3,
        grid=(n_blk,),
        in_specs=[pl.BlockSpec(memory_space=pl.ANY),
                  pl.BlockSpec((1, d), lambda i, be, nu, tok: (0, 0)),
                  pl.BlockSpec((1, d, D_EXPERT), lambda i, be, nu, tok: (be[i], 0, 0)),
                  pl.BlockSpec((1, d, D_EXPERT), lambda i, be, nu, tok: (be[i], 0, 0)),
                  pl.BlockSpec((1, D_EXPERT, d), lambda i, be, nu, tok: (be[i], 0, 0))],
        out_specs=pl.BlockSpec((bm, d), lambda i, be, nu, tok: (i, 0)),
        scratch_shapes=[pltpu.VMEM((2, bm, d), f32), pltpu.SemaphoreType.DMA((2,))],
    )
    return pl.pallas_call(
        functools.partial(_moe_kernel, bm=bm),
        grid_spec=grid_spec,
        out_shape=jax.ShapeDtypeStruct((cap, d), f32),
        compiler_params=_cparams(("arbitrary",)),
        name="expert_mlp",
    )(blk_expert, n_used, buf_tok, h, nfw, w1, w3, w2)


def _combine_kernel(d0_ref, d1_ref, h_ref, route_ref, y_hbm, nw_ref, o_ref, ybuf, sem, *, tb):
    i = pl.program_id(0)
    n = pl.num_programs(0)
    slot = i % 2

    def start(blk, s):
        _row_gather_start(d0_ref, blk * tb, tb, y_hbm, ybuf.at[s, 0], sem.at[s])
        _row_gather_start(d1_ref, blk * tb, tb, y_hbm, ybuf.at[s, 1], sem.at[s])

    @pl.when(i == 0)
    def _():
        start(0, 0)

    @pl.when(i + 1 < n)
    def _():
        start(i + 1, 1 - slot)

    for k in range(TOP_K):
        pltpu.make_async_copy(y_hbm.at[pl.ds(0, tb)], ybuf.at[slot, k], sem.at[slot]).wait()
    r = route_ref[...]
    h = h_ref[...] + (r[:, 2:3] * ybuf[slot, 0] + r[:, 3:4] * ybuf[slot, 1])
    o_ref[...] = h * lax.rsqrt(jnp.mean(h * h, axis=-1, keepdims=True) + EPS) * nw_ref[...]


def _combine(dest0, dest1, h, route, y, norm_w):
    t, d = h.shape
    tb = min(256, t)
    grid_spec = pltpu.PrefetchScalarGridSpec(
        num_scalar_prefetch=2,
        grid=(t // tb,),
        in_specs=[pl.BlockSpec((tb, d), lambda i, a, b: (i, 0)),
                  pl.BlockSpec((tb, LANES), lambda i, a, b: (i, 0)),
                  pl.BlockSpec(memory_space=pl.ANY),
                  pl.BlockSpec((1, d), lambda i, a, b: (0, 0))],
        out_specs=pl.BlockSpec((tb, d), lambda i, a, b: (i, 0)),
        scratch_shapes=[pltpu.VMEM((2, TOP_K, tb, d), f32), pltpu.SemaphoreType.DMA((2,))],
    )
    return pl.pallas_call(
        functools.partial(_combine_kernel, tb=tb),
        grid_spec=grid_spec,
        out_shape=jax.ShapeDtypeStruct((t, d), f32),
        compiler_params=_cparams(("arbitrary",)),
        name="combine_final_norm",
    )(dest0, dest1, h, route, y, norm_w)


def kernel(x, positions, norm_mix_w, w_in, fox_b_f, ret_gn_w, w_branch, b_gate, w_out, norm_ffn_w, w_router_group,
           b_router_group, w_router_expert, b_router_expert, w1, w3, w2, norm_final_w):
    b, s, d = x.shape
    assert d == D_MODEL and norm_mix_w.shape[0] == 1
    t = b * s
    x2 = x.reshape(t, d)
    scale = 1.0 / math.sqrt(HEAD_DIM)

    wi = w_in[0]
    o_f = 3 * FOX_WIDTH
    o_r = o_f + FOX_HEADS
    o_g = o_r + 4 * RET_WIDTH
    w_main = jnp.concatenate([wi[:, o_g:], wi[:, :FOX_WIDTH] * scale, wi[:, FOX_WIDTH:o_f],
                              wi[:, o_r:o_r + RET_WIDTH] * scale, wi[:, o_r + RET_WIDTH:o_g]], axis=1).astype(bf16)
    w_f = jnp.pad(wi[:, o_f:o_r], ((0, 0), (0, LANES - FOX_HEADS))).astype(bf16)
    bias_f = jnp.pad(fox_b_f[0], (0, LANES - FOX_HEADS)).reshape(1, LANES)
    w_r = jnp.concatenate([w_router_group[0], w_router_expert[0]], axis=1)
    w_r = jnp.pad(w_r, ((0, 0), (0, LANES - w_r.shape[1])))
    wr_hi = w_r.astype(bf16)
    wr_lo = (w_r - wr_hi.astype(f32)).astype(bf16)
    b_r = jnp.pad(jnp.concatenate([b_router_group[0], b_router_expert[0]]), (0, LANES - N_GROUPS - N_EXPERTS))
    b_r = b_r.reshape(1, LANES)

    cos, sin = _rope_tables(positions)
    proj, f_logit = _in_projection(x2, norm_mix_w, w_main, w_f)
    c = _forget_cumsum(f_logit, bias_f, b, s)
    tq = min(ATTN_BLOCK, s)
    fox_o = _fox_attention(proj, c.reshape(b * FOX_HEADS, s // tq, 1, tq), b, s)
    ret_o = _retention(proj, cos, sin, ret_gn_w, b, s)
    h, route = _merge(fox_o, ret_o, proj, b_gate[0], x2, w_branch[0].astype(bf16), w_out[0].astype(bf16),
                      norm_ffn_w, wr_hi, wr_lo, b_r)

    rank, counts = _dispatch_ranks(route)
    bm = MOE_BLOCK
    m = t * TOP_K
    cap = m + N_EXPERTS * bm
    n_blk = cap // bm
    counts = counts[0, :N_EXPERTS].astype(jnp.int32)
    padded = (counts + bm - 1) // bm * bm
    pends = jnp.cumsum(padded)
    pstarts = pends - padded
    e_sel = route[:, :TOP_K].astype(jnp.int32)
    dest = pstarts[e_sel] + rank[:, :TOP_K]
    tok = jnp.broadcast_to(jnp.arange(t, dtype=jnp.int32)[:, None], (t, TOP_K))
    buf_tok = jnp.zeros((cap,), jnp.int32).at[dest.reshape(m)].set(tok.reshape(m))
    blk_expert = jnp.clip(jnp.searchsorted(pends, jnp.arange(n_blk, dtype=jnp.int32) * bm, side='right'),
                          0, N_EXPERTS - 1).astype(jnp.int32)
    n_used = (pends[-1:] // bm).astype(jnp.int32)

    y = _expert_mlp(blk_expert, n_used, buf_tok, h, norm_ffn_w, w1[0].astype(bf16), w3[0].astype(bf16),
                    w2[0].astype(bf16), cap)
    out = _combine(dest[:, 0], dest[:, 1], h, route, y, norm_final_w.reshape(1, d))
    return out.reshape(b, s, d)
```

```python
import functools
import math

import numpy as np
import jax
import jax.numpy as jnp
from jax import lax
from jax.experimental import pallas as pl
from jax.experimental.pallas import tpu as pltpu

D_MODEL = 2048
HEAD_DIM = 128
FOX_HEADS = 8
RET_HEADS = 8
FOX_WIDTH = FOX_HEADS * HEAD_DIM
RET_WIDTH = RET_HEADS * HEAD_DIM
N_BRANCH = 2
N_GROUPS = 4
EXPERTS_PER_GROUP = 8
N_EXPERTS = N_GROUPS * EXPERTS_PER_GROUP
TOP_K = 2
D_EXPERT = 512
ROPE_BASE = 10000.0
EPS = 1e-6

LANES = 128
NEG_BIG = -1e30
MAIN_WIDTH = 3 * FOX_WIDTH + 4 * RET_WIDTH + N_BRANCH * D_MODEL
COL_GATE = 0
COL_FQ = N_BRANCH * D_MODEL // HEAD_DIM
COL_FK = COL_FQ + FOX_HEADS
COL_FV = COL_FK + FOX_HEADS
COL_RQ = COL_FV + FOX_HEADS
COL_RK = COL_RQ + RET_HEADS
COL_RV = COL_RK + RET_HEADS
COL_RG = COL_RV + RET_HEADS

ATTN_BLOCK = 256
RET_CHUNK = 256
MOE_BLOCK = 256
VMEM_LIMIT = 56 * 1024 * 1024

f32 = jnp.float32
bf16 = jnp.bfloat16


def _cparams(sem, vmem=VMEM_LIMIT):
    return pltpu.CompilerParams(dimension_semantics=sem, vmem_limit_bytes=vmem)


def _rope_kernel(pos_ref, freq_ref, sgn_ref, cos_ref, sin_ref):
    ang = pos_ref[0].astype(f32) * freq_ref[...]
    cos_ref[0] = jnp.cos(ang)
    sin_ref[0] = jnp.sin(ang) * sgn_ref[...]


def _rope_tables(positions):
    b, s = positions.shape
    half = HEAD_DIM // 2
    inv = (np.float32(ROPE_BASE) ** (-np.arange(half, dtype=np.float32) / np.float32(half))).astype(np.float32)
    freq = jnp.asarray(np.concatenate([inv, inv])[None, :])
    sgn = jnp.asarray(np.concatenate([-np.ones(half, np.float32), np.ones(half, np.float32)])[None, :])
    out = jax.ShapeDtypeStruct((b, s, HEAD_DIM), f32)
    return pl.pallas_call(
        _rope_kernel,
        grid=(b,),
        in_specs=[pl.BlockSpec((1, s, 1), lambda i: (i, 0, 0)),
                  pl.BlockSpec((1, HEAD_DIM), lambda i: (0, 0)),
                  pl.BlockSpec((1, HEAD_DIM), lambda i: (0, 0))],
        out_specs=[pl.BlockSpec((1, s, HEAD_DIM), lambda i: (i, 0, 0)),
                   pl.BlockSpec((1, s, HEAD_DIM), lambda i: (i, 0, 0))],
        out_shape=[out, out],
        compiler_params=_cparams(("parallel",)),
        name="rope_tables",
    )(positions.reshape(b, s, 1), freq, sgn)


def _inproj_kernel(x_ref, nw_ref, w_ref, wf_ref, o_ref, f_ref, xn_ref):
    @pl.when(pl.program_id(1) == 0)
    def _():
        x = x_ref[...]
        ms = jnp.mean(x * x, axis=-1, keepdims=True)
        xn = (x * lax.rsqrt(ms + EPS) * nw_ref[...]).astype(bf16)
        xn_ref[...] = xn
        f_ref[...] = jnp.dot(xn, wf_ref[...], preferred_element_type=f32)

    o_ref[...] = jnp.dot(xn_ref[...], w_ref[...], preferred_element_type=f32).astype(o_ref.dtype)


def _in_projection(x2, norm_w, w_main, w_f):
    t, d = x2.shape
    n = w_main.shape[1]
    tm = min(1024, t)
    tn = 1024
    return pl.pallas_call(
        _inproj_kernel,
        grid=(t // tm, n // tn),
        in_specs=[pl.BlockSpec((tm, d), lambda i, j: (i, 0)),
                  pl.BlockSpec((1, d), lambda i, j: (0, 0)),
                  pl.BlockSpec((d, tn), lambda i, j: (0, j)),
                  pl.BlockSpec((d, LANES), lambda i, j: (0, 0))],
        out_specs=[pl.BlockSpec((tm, tn), lambda i, j: (i, j)),
                   pl.BlockSpec((tm, LANES), lambda i, j: (i, 0))],
        out_shape=[jax.ShapeDtypeStruct((t, n), bf16), jax.ShapeDtypeStruct((t, LANES), f32)],
        scratch_shapes=[pltpu.VMEM((tm, d), bf16)],
        compiler_params=_cparams(("parallel", "arbitrary")),
        name="in_projection",
    )(x2, norm_w, w_main, w_f)


def _fcum_kernel(f_ref, b_ref, c_ref):
    z = f_ref[...] + b_ref[...]
    ls = jnp.minimum(z, 0.0) - jnp.log1p(jnp.exp(-jnp.abs(z)))
    c = ls.T[:FOX_HEADS, :]
    s = c.shape[1]
    lane = lax.broadcasted_iota(jnp.int32, c.shape, 1)
    sh = 1
    while sh < s:
        c = c + jnp.where(lane >= sh, pltpu.roll(c, sh, axis=1), 0.0)
        sh *= 2
    c_ref[0] = c


def _forget_cumsum(f_logit, bias, b, s):
    return pl.pallas_call(
        _fcum_kernel,
        grid=(b,),
        in_specs=[pl.BlockSpec((s, LANES), lambda i: (i, 0)),
                  pl.BlockSpec((1, LANES), lambda i: (0, 0))],
        out_specs=pl.BlockSpec((1, FOX_HEADS, s), lambda i: (i, 0, 0)),
        out_shape=jax.ShapeDtypeStruct((b, FOX_HEADS, s), f32),
        compiler_params=_cparams(("parallel",)),
        name="forget_cumsum",
    )(f_logit, bias)


def _fox_kernel(q_ref, k_ref, v_ref, c_ref, o_ref, *, tq):
    s_len = q_ref.shape[0]
    nq = s_len // tq
    nt = (((1,), (1,)), ((), ()))
    row = lax.broadcasted_iota(jnp.int32, (tq, tq), 0)
    col = lax.broadcasted_iota(jnp.int32, (tq, tq), 1)
    causal = row >= col
    for qi in range(nq):
        r0 = qi * tq
        q = q_ref[r0:r0 + tq, :]
        sd = lax.dot_general(q, k_ref[r0:r0 + tq, :], nt, preferred_element_type=f32) - c_ref[0, :, r0:r0 + tq]
        sd = jnp.where(causal, sd, NEG_BIG)
        m = jnp.max(sd, axis=-1, keepdims=True)
        if qi:
            so = lax.dot_general(q, k_ref[0:r0, :], nt, preferred_element_type=f32) - c_ref[0, :, 0:r0]
            m = jnp.maximum(m, jnp.max(so, axis=-1, keepdims=True))
        pd = jnp.exp(sd - m)
        l = jnp.sum(pd, axis=-1, keepdims=True)
        acc = jnp.dot(pd.astype(bf16), v_ref[r0:r0 + tq, :], preferred_element_type=f32)
        if qi:
            po = jnp.exp(so - m)
            l = l + jnp.sum(po, axis=-1, keepdims=True)
            acc = acc + jnp.dot(po.astype(bf16), v_ref[0:r0, :], preferred_element_type=f32)
        o_ref[r0:r0 + tq, :] = (acc / l).astype(o_ref.dtype)


def _fox_attention(proj, c_blk, b, s):
    tq = min(ATTN_BLOCK, s)
    h = FOX_HEADS
    return pl.pallas_call(
        functools.partial(_fox_kernel, tq=tq),
        grid=(b, h),
        in_specs=[pl.BlockSpec((s, HEAD_DIM), lambda i, j: (i, COL_FQ + j)),
                  pl.BlockSpec((s, HEAD_DIM), lambda i, j: (i, COL_FK + j)),
                  pl.BlockSpec((s, HEAD_DIM), lambda i, j: (i, COL_FV + j)),
                  pl.BlockSpec((1, 1, s), lambda i, j: (i * FOX_HEADS + j, 0, 0))],
        out_specs=pl.BlockSpec((s, HEAD_DIM), lambda i, j: (i, j)),
        out_shape=jax.ShapeDtypeStruct((b * s, FOX_WIDTH), bf16),
        compiler_params=_cparams(("parallel", "parallel")),
        name="fox_attention",
    )(proj, proj, proj, c_blk)


def _ret_kernel(q_ref, k_ref, v_ref, g_ref, cos_ref, sin_ref, dec_ref, xi_ref, zeta_ref, gc_ref, gnw_ref,
                o_ref, st_ref, *, chunk):
    s_len = q_ref.shape[0]
    nc = s_len // chunk
    st_ref[...] = jnp.zeros_like(st_ref)

    def body(ci, _):
        r0 = pl.multiple_of(ci * chunk, chunk)
        rows = pl.ds(r0, chunk)
        cos = cos_ref[0, rows, :]
        sin = sin_ref[0, rows, :]

        def rot(t):
            t = t.astype(f32)
            return t * cos + pltpu.roll(t, HEAD_DIM // 2, axis=1) * sin

        q = rot(q_ref[rows, :])
        k = rot(k_ref[rows, :])
        v = v_ref[rows, :]
        qb = q.astype(bf16)
        inner = lax.dot_general(qb, k.astype(bf16), (((1,), (1,)), ((), ())), preferred_element_type=f32)
        inner = inner * dec_ref[0]
        state = st_ref[...]
        o = jnp.dot(inner.astype(bf16), v, preferred_element_type=f32)
        o = o + jnp.dot(qb, state.astype(bf16), preferred_element_type=f32) * xi_ref[0]
        kz = (k * zeta_ref[0]).astype(bf16)
        st_ref[...] = state * gc_ref[0] + lax.dot_general(kz, v, (((0,), (0,)), ((), ())),
                                                           preferred_element_type=f32)
        mu = jnp.mean(o, axis=-1, keepdims=True)
        oc = o - mu
        var = jnp.mean(oc * oc, axis=-1, keepdims=True)
        y = oc * lax.rsqrt(var + EPS) * gnw_ref[...]
        g = g_ref[rows, :].astype(f32)
        o_ref[rows, :] = (g * jax.nn.sigmoid(g) * y).astype(o_ref.dtype)
        return 0

    lax.fori_loop(0, nc, body, 0)


def _retention_constants(chunk):
    hh = np.arange(RET_HEADS, dtype=np.float64)
    log_gamma = np.log1p(-(2.0 ** (-5.0 - hh)))
    n = np.arange(chunk, dtype=np.float64)
    diff = n[:, None] - n[None, :]
    decay = np.where(diff[None] >= 0, np.exp(diff[None] * log_gamma[:, None, None]), 0.0)
    xi = np.exp((n[None, :] + 1.0) * log_gamma[:, None])
    zeta = np.exp((chunk - 1.0 - n[None, :]) * log_gamma[:, None])
    g_chunk = np.exp(chunk * log_gamma)
    bc = lambda a: np.broadcast_to(a[:, :, None], (RET_HEADS, chunk, HEAD_DIM))
    gc = np.broadcast_to(g_chunk[:, None, None], (RET_HEADS, 1, HEAD_DIM))
    to = lambda a: jnp.asarray(np.ascontiguousarray(a), dtype=f32)
    return to(decay), to(bc(xi)), to(bc(zeta)), to(gc)


def _retention(proj, cos, sin, gn_w, b, s):
    chunk = min(RET_CHUNK, s)
    decay, xi, zeta, gc = _retention_constants(chunk)
    h = RET_HEADS
    head = lambda col: pl.BlockSpec((s, HEAD_DIM), lambda i, j: (i, col + j))
    per_head = lambda shape: pl.BlockSpec(shape, lambda i, j: (j, 0, 0))
    return pl.pallas_call(
        functools.partial(_ret_kernel, chunk=chunk),
        grid=(b, h),
        in_specs=[head(COL_RQ), head(COL_RK), head(COL_RV), head(COL_RG),
                  pl.BlockSpec((1, s, HEAD_DIM), lambda i, j: (i, 0, 0)),
                  pl.BlockSpec((1, s, HEAD_DIM), lambda i, j: (i, 0, 0)),
                  per_head((1, chunk, chunk)), per_head((1, chunk, HEAD_DIM)), per_head((1, chunk, HEAD_DIM)),
                  per_head((1, 1, HEAD_DIM)),
                  pl.BlockSpec((1, HEAD_DIM), lambda i, j: (0, j))],
        out_specs=pl.BlockSpec((s, HEAD_DIM), lambda i, j: (i, j)),
        out_shape=jax.ShapeDtypeStruct((b * s, RET_WIDTH), bf16),
        scratch_shapes=[pltpu.VMEM((HEAD_DIM, HEAD_DIM), f32)],
        compiler_params=_cparams(("parallel", "parallel")),
        name="retention",
    )(proj, proj, proj, proj, cos, sin, decay, xi, zeta, gc, gn_w)


def _route(logits):
    lane = lax.broadcasted_iota(jnp.int32, logits.shape, 1).astype(f32)
    far = jnp.float32(1e9)
    gmask = lane < N_GROUPS
    gl = jnp.where(gmask, logits, NEG_BIG)
    gmax = jnp.max(gl, axis=-1, keepdims=True)
    gsel = jnp.min(jnp.where(gl == gmax, lane, far), axis=-1, keepdims=True)
    p_group = 1.0 / jnp.sum(jnp.where(gmask, jnp.exp(gl - gmax), 0.0), axis=-1, keepdims=True)
    lo = N_GROUPS + EXPERTS_PER_GROUP * gsel
    emask = (lane >= lo) & (lane < lo + EXPERTS_PER_GROUP)
    el = jnp.where(emask, logits, NEG_BIG)
    m1 = jnp.max(el, axis=-1, keepdims=True)
    i1 = jnp.min(jnp.where(el == m1, lane, far), axis=-1, keepdims=True)
    el2 = jnp.where(lane == i1, NEG_BIG, el)
    m2 = jnp.max(el2, axis=-1, keepdims=True)
    i2 = jnp.min(jnp.where(el2 == m2, lane, far), axis=-1, keepdims=True)
    t = jnp.exp(m2 - m1)
    w1 = p_group / (1.0 + t)
    w2 = p_group * t / (1.0 + t)
    out = jnp.where(lane == 0, i1 - N_GROUPS, 0.0)
    out = jnp.where(lane == 1, i2 - N_GROUPS, out)
    out = jnp.where(lane == 2, w1, out)
    out = jnp.where(lane == 3, w2, out)
    return out


def _merge_kernel(fo_ref, ro_ref, g0_ref, g1_ref, bg_ref, x_ref, wb_ref, wo_ref, nfw_ref, wrh_ref, wrl_ref, br_ref,
                  h_ref, route_ref, mg_ref, *, ncol):
    d = x_ref.shape[1]
    cw = d // ncol
    fo = fo_ref[...]
    ro = ro_ref[...]
    for c in range(ncol):
        cs = slice(c * cw, (c + 1) * cw)
        bd0 = jnp.dot(fo, wb_ref[0, :, cs], preferred_element_type=f32)
        bd1 = jnp.dot(ro, wb_ref[1, :, cs], preferred_element_type=f32)
        ga = jax.nn.sigmoid(g0_ref[:, cs].astype(f32) + bg_ref[0:1, cs])
        gb = jax.nn.sigmoid(g1_ref[:, cs].astype(f32) + bg_ref[1:2, cs])
        mg_ref[:, cs] = (ga * bd0 + gb * bd1).astype(bf16)
    h = x_ref[...] + jnp.dot(mg_ref[...], wo_ref[...], preferred_element_type=f32)
    h_ref[...] = h
    hn = h * lax.rsqrt(jnp.mean(h * h, axis=-1, keepdims=True) + EPS) * nfw_ref[...]
    hi = hn.astype(bf16)
    lo = (hn - hi.astype(f32)).astype(bf16)
    logits = (jnp.dot(hi, wrh_ref[...], preferred_element_type=f32)
              + jnp.dot(lo, wrh_ref[...], preferred_element_type=f32)
              + jnp.dot(hi, wrl_ref[...], preferred_element_type=f32)) + br_ref[...]
    route_ref[...] = _route(logits)


def _merge(fox_o, ret_o, proj, b_gate, x2, w_branch, w_out, nfw, wr_hi, wr_lo, b_router):
    t, d = x2.shape
    tm = min(256, t)
    gate0 = COL_GATE * HEAD_DIM // d
    const = lambda shape: pl.BlockSpec(shape, lambda i: tuple(0 for _ in shape), pipeline_mode=pl.Buffered(1))
    return pl.pallas_call(
        functools.partial(_merge_kernel, ncol=4),
        grid=(t // tm,),
        in_specs=[pl.BlockSpec((tm, FOX_WIDTH), lambda i: (i, 0)),
                  pl.BlockSpec((tm, RET_WIDTH), lambda i: (i, 0)),
                  pl.BlockSpec((tm, d), lambda i: (i, gate0)),
                  pl.BlockSpec((tm, d), lambda i: (i, gate0 + 1)),
                  const((N_BRANCH, d)),
                  pl.BlockSpec((tm, d), lambda i: (i, 0)),
                  const((N_BRANCH, FOX_WIDTH, d)),
                  const((d, d)),
                  const((1, d)),
                  const((d, LANES)), const((d, LANES)), const((1, LANES))],
        out_specs=[pl.BlockSpec((tm, d), lambda i: (i, 0)),
                   pl.BlockSpec((tm, LANES), lambda i: (i, 0))],
        out_shape=[jax.ShapeDtypeStruct((t, d), f32), jax.ShapeDtypeStruct((t, LANES), f32)],
        scratch_shapes=[pltpu.VMEM((tm, d), bf16)],
        compiler_params=_cparams(("parallel",)),
        name="merge_outproj_router",
    )(fox_o, ret_o, proj, proj, b_gate, x2, w_branch, w_out, nfw, wr_hi, wr_lo, b_router)


def _rank_kernel(route_ref, rank_ref, cnt_ref, carry_ref):
    @pl.when(pl.program_id(0) == 0)
    def _():
        carry_ref[...] = jnp.zeros_like(carry_ref)

    r = route_ref[...]
    tb = r.shape[0]
    lane = lax.broadcasted_iota(jnp.int32, r.shape, 1).astype(f32)
    o0 = (lane == r[:, 0:1]).astype(f32)
    o1 = (lane == r[:, 1:2]).astype(f32)
    o = o0 + o1
    row = lax.broadcasted_iota(jnp.int32, (tb, tb), 0)
    col = lax.broadcasted_iota(jnp.int32, (tb, tb), 1)
    lower = jnp.where(col < row, 1.0, 0.0).astype(bf16)
    before = jnp.dot(lower, o.astype(bf16), preferred_element_type=f32) + carry_ref[...]
    r0 = jnp.sum(before * o0, axis=-1, keepdims=True)
    r1 = jnp.sum(before * o1, axis=-1, keepdims=True)
    rank_ref[...] = jnp.where(lane == 0, r0, jnp.where(lane == 1, r1, 0.0)).astype(jnp.int32)
    carry_ref[...] += jnp.sum(o, axis=0, keepdims=True)
    cnt_ref[...] = carry_ref[...]


def _dispatch_ranks(route):
    t = route.shape[0]
    tb = min(512, t)
    return pl.pallas_call(
        _rank_kernel,
        grid=(t // tb,),
        in_specs=[pl.BlockSpec((tb, LANES), lambda i: (i, 0))],
        out_specs=[pl.BlockSpec((tb, LANES), lambda i: (i, 0)),
                   pl.BlockSpec((1, LANES), lambda i: (0, 0))],
        out_shape=[jax.ShapeDtypeStruct((t, LANES), jnp.int32), jax.ShapeDtypeStruct((1, LANES), f32)],
        scratch_shapes=[pltpu.VMEM((1, LANES), f32)],
        compiler_params=_cparams(("arbitrary",)),
        name="dispatch_ranks",
    )(route)


def _row_gather_start(idx_ref, base, n_rows, src_hbm, dst, sem, unroll=8):
    def body(g, _):
        for u in range(unroll):
            r = g * unroll + u
            t = idx_ref[base + r]
            pltpu.make_async_copy(src_hbm.at[pl.ds(t, 1)], dst.at[pl.ds(r, 1)], sem).start()
        return 0
    lax.fori_loop(0, n_rows // unroll, body, 0)


def _moe_kernel(be_ref, nu_ref, tok_ref, h_hbm, nfw_ref, w1_ref, w3_ref, w2_ref, y_ref, xbuf, sem, *, bm):
    i = pl.program_id(0)
    n_used = nu_ref[0]
    slot = i % 2

    @pl.when(i == 0)
    def _():
        _row_gather_start(tok_ref, 0, bm, h_hbm, xbuf.at[0], sem.at[0])

    @pl.when(i + 1 < n_used)
    def _():
        _row_gather_start(tok_ref, (i + 1) * bm, bm, h_hbm, xbuf.at[1 - slot], sem.at[1 - slot])

    @pl.when(i < n_used)
    def _():
        pltpu.make_async_copy(h_hbm.at[pl.ds(0, bm)], xbuf.at[slot], sem.at[slot]).wait()
        h = xbuf[slot]
        hn = (h * lax.rsqrt(jnp.mean(h * h, axis=-1, keepdims=True) + EPS) * nfw_ref[...]).astype(bf16)
        a = jnp.dot(hn, w1_ref[0], preferred_element_type=f32)
        b = jnp.dot(hn, w3_ref[0], preferred_element_type=f32)
        mid = (a * jax.nn.sigmoid(a) * b).astype(bf16)
        y_ref[...] = jnp.dot(mid, w2_ref[0], preferred_element_type=f32)

    @pl.when(i >= n_used)
    def _():
        y_ref[...] = jnp.zeros_like(y_ref)


def _expert_mlp(blk_expert, n_used, buf_tok, h, nfw, w1, w3, w2, cap):
    t, d = h.shape
    bm = MOE_BLOCK
    n_blk = cap // bm
    grid_spec = pltpu.PrefetchScalarGridSpec(
        num_scalar_prefetch=3,
        grid=(n_blk,),
        in_specs=[pl.BlockSpec(memory_space=pl.ANY),
                  pl.BlockSpec((1, d), lambda i, be, nu, tok: (0, 0)),
                  pl.BlockSpec((1, d, D_EXPERT), lambda i, be, nu, tok: (be[i], 0, 0)),
                  pl.BlockSpec((1, d, D_EXPERT), lambda i, be, nu, tok: (be[i], 0, 0)),
                  pl.BlockSpec((1, D_EXPERT, d), lambda i, be, nu, tok: (be[i], 0, 0))],
        out_specs=pl.BlockSpec((bm, d), lambda i, be, nu, tok: (i, 0)),
        scratch_shapes=[pltpu.VMEM((2, bm, d), f32), pltpu.SemaphoreType.DMA((2,))],
    )
    return pl.pallas_call(
        functools.partial(_moe_kernel, bm=bm),
        grid_spec=grid_spec,
        out_shape=jax.ShapeDtypeStruct((cap, d), f32),
        compiler_params=_cparams(("arbitrary",)),
        name="expert_mlp",
    )(blk_expert, n_used, buf_tok, h, nfw, w1, w3, w2)


def _combine_kernel(d0_ref, d1_ref, h_ref, route_ref, y_hbm, nw_ref, o_ref, ybuf, sem, *, tb):
    i = pl.program_id(0)
    n = pl.num_programs(0)
    slot = i % 2

    def start(blk, s):
        _row_gather_start(d0_ref, blk * tb, tb, y_hbm, ybuf.at[s, 0], sem.at[s])
        _row_gather_start(d1_ref, blk * tb, tb, y_hbm, ybuf.at[s, 1], sem.at[s])

    @pl.when(i == 0)
    def _():
        start(0, 0)

    @pl.when(i + 1 < n)
    def _():
        start(i + 1, 1 - slot)

    for k in range(TOP_K):
        pltpu.make_async_copy(y_hbm.at[pl.ds(0, tb)], ybuf.at[slot, k], sem.at[slot]).wait()
    r = route_ref[...]
    h = h_ref[...] + (r[:, 2:3] * ybuf[slot, 0] + r[:, 3:4] * ybuf[slot, 1])
    o_ref[...] = h * lax.rsqrt(jnp.mean(h * h, axis=-1, keepdims=True) + EPS) * nw_ref[...]


def _combine(dest0, dest1, h, route, y, norm_w):
    t, d = h.shape
    tb = min(256, t)
    grid_spec = pltpu.PrefetchScalarGridSpec(
        num_scalar_prefetch=2,
        grid=(t // tb,),
        in_specs=[pl.BlockSpec((tb, d), lambda i, a, b: (i, 0)),
                  pl.BlockSpec((tb, LANES), lambda i, a, b: (i, 0)),
                  pl.BlockSpec(memory_space=pl.ANY),
                  pl.BlockSpec((1, d), lambda i, a, b: (0, 0))],
        out_specs=pl.BlockSpec((tb, d), lambda i, a, b: (i, 0)),
        scratch_shapes=[pltpu.VMEM((2, TOP_K, tb, d), f32), pltpu.SemaphoreType.DMA((2,))],
    )
    return pl.pallas_call(
        functools.partial(_combine_kernel, tb=tb),
        grid_spec=grid_spec,
        out_shape=jax.ShapeDtypeStruct((t, d), f32),
        compiler_params=_cparams(("arbitrary",)),
        name="combine_final_norm",
    )(dest0, dest1, h, route, y, norm_w)


def kernel(x, positions, norm_mix_w, w_in, fox_b_f, ret_gn_w, w_branch, b_gate, w_out, norm_ffn_w, w_router_group,
           b_router_group, w_router_expert, b_router_expert, w1, w3, w2, norm_final_w):
    b, s, d = x.shape
    assert d == D_MODEL and norm_mix_w.shape[0] == 1
    t = b * s
    x2 = x.reshape(t, d)
    scale = 1.0 / math.sqrt(HEAD_DIM)

    wi = w_in[0]
    o_f = 3 * FOX_WIDTH
    o_r = o_f + FOX_HEADS
    o_g = o_r + 4 * RET_WIDTH
    w_main = jnp.concatenate([wi[:, o_g:], wi[:, :FOX_WIDTH] * scale, wi[:, FOX_WIDTH:o_f],
                              wi[:, o_r:o_r + RET_WIDTH] * scale, wi[:, o_r + RET_WIDTH:o_g]], axis=1).astype(bf16)
    w_f = jnp.pad(wi[:, o_f:o_r], ((0, 0), (0, LANES - FOX_HEADS))).astype(bf16)
    bias_f = jnp.pad(fox_b_f[0], (0, LANES - FOX_HEADS)).reshape(1, LANES)
    w_r = jnp.concatenate([w_router_group[0], w_router_expert[0]], axis=1)
    w_r = jnp.pad(w_r, ((0, 0), (0, LANES - w_r.shape[1])))
    wr_hi = w_r.astype(bf16)
    wr_lo = (w_r - wr_hi.astype(f32)).astype(bf16)
    b_r = jnp.pad(jnp.concatenate([b_router_group[0], b_router_expert[0]]), (0, LANES - N_GROUPS - N_EXPERTS))
    b_r = b_r.reshape(1, LANES)

    cos, sin = _rope_tables(positions)
    proj, f_logit = _in_projection(x2, norm_mix_w, w_main, w_f)
    c = _forget_cumsum(f_logit, bias_f, b, s)
    fox_o = _fox_attention(proj, c.reshape(b * FOX_HEADS, 1, s), b, s)
    ret_o = _retention(proj, cos, sin, ret_gn_w, b, s)
    h, route = _merge(fox_o, ret_o, proj, b_gate[0], x2, w_branch[0].astype(bf16), w_out[0].astype(bf16),
                      norm_ffn_w, wr_hi, wr_lo, b_r)

    rank, counts = _dispatch_ranks(route)
    bm = MOE_BLOCK
    m = t * TOP_K
    cap = m + N_EXPERTS * bm
    n_blk = cap // bm
    counts = counts[0, :N_EXPERTS].astype(jnp.int32)
    padded = (counts + bm - 1) // bm * bm
    pends = jnp.cumsum(padded)
    pstarts = pends - padded
    e_sel = route[:, :TOP_K].astype(jnp.int32)
    dest = pstarts[e_sel] + rank[:, :TOP_K]
    tok = jnp.broadcast_to(jnp.arange(t, dtype=jnp.int32)[:, None], (t, TOP_K))
    buf_tok = jnp.zeros((cap,), jnp.int32).at[dest.reshape(m)].set(tok.reshape(m))
    blk_start = jnp.arange(n_blk, dtype=jnp.int32) * bm
    blk_expert = jnp.minimum(jnp.sum((pends[None, :] <= blk_start[:, None]).astype(jnp.int32), axis=1),
                             N_EXPERTS - 1)
    n_used = (pends[-1:] // bm).astype(jnp.int32)

    y = _expert_mlp(blk_expert, n_used, buf_tok, h, norm_ffn_w, w1[0].astype(bf16), w3[0].astype(bf16),
                    w2[0].astype(bf16), cap)
    out = _combine(dest[:, 0], dest[:, 1], h, route, y, norm_final_w.reshape(1, d))
    return out.reshape(b, s, d)
```

```python
import functools
import math

import numpy as np
import jax
import jax.numpy as jnp
from jax import lax
from jax.experimental import pallas as pl
from jax.experimental.pallas import tpu as pltpu

D_MODEL = 2048
HEAD_DIM = 128
FOX_HEADS = 8
RET_HEADS = 8
FOX_WIDTH = FOX_HEADS * HEAD_DIM
RET_WIDTH = RET_HEADS * HEAD_DIM
N_BRANCH = 2
N_GROUPS = 4
EXPERTS_PER_GROUP = 8
N_EXPERTS = N_GROUPS * EXPERTS_PER_GROUP
TOP_K = 2
D_EXPERT = 512
ROPE_BASE = 10000.0
EPS = 1e-6

LANES = 128
NEG_BIG = -1e30
MAIN_WIDTH = 3 * FOX_WIDTH + 4 * RET_WIDTH + N_BRANCH * D_MODEL
COL_GATE = 0
COL_FQ = N_BRANCH * D_MODEL // HEAD_DIM
COL_FK = COL_FQ + FOX_HEADS
COL_FV = COL_FK + FOX_HEADS
COL_RQ = COL_FV + FOX_HEADS
COL_RK = COL_RQ + RET_HEADS
COL_RV = COL_RK + RET_HEADS
COL_RG = COL_RV + RET_HEADS

ATTN_BLOCK = 256
RET_CHUNK = 256
MOE_BLOCK = 256
VMEM_LIMIT = 56 * 1024 * 1024

f32 = jnp.float32
bf16 = jnp.bfloat16
_NT = (((1,), (1,)), ((), ()))
SLAB = D_MODEL // LANES
PITCH = SLAB + 1


def _rows_to_slabs(ref2, val, n_rows):
    for j in range(SLAB):
        ref2[pl.ds(j, n_rows, stride=PITCH), :] = val[:, j * LANES:(j + 1) * LANES]
    ref2[pl.ds(SLAB, n_rows, stride=PITCH), :] = jnp.zeros((n_rows, LANES), val.dtype)


def _slabs_to_rows(ref2, n_rows):
    return jnp.concatenate([ref2[pl.ds(j, n_rows, stride=PITCH), :] for j in range(SLAB)], axis=-1)


def _slab_gather_start(idx_ref, base, n_rows, src_hbm, dst, sem, unroll=8):
    def body(g, _):
        for u in range(unroll):
            r = g * unroll + u
            t = idx_ref[base + r]
            pltpu.make_async_copy(src_hbm.at[pl.ds(t * PITCH, SLAB), :], dst.at[pl.ds(r * PITCH, SLAB), :], sem).start()
        return 0
    lax.fori_loop(0, n_rows // unroll, body, 0)


def _slab_gather_wait(n_rows, src_hbm, dst, sem):
    pltpu.make_async_copy(src_hbm.at[pl.ds(0, n_rows * SLAB), :], dst.at[pl.ds(0, n_rows * SLAB), :], sem).wait()


def _cparams(sem, vmem=VMEM_LIMIT):
    return pltpu.CompilerParams(dimension_semantics=sem, vmem_limit_bytes=vmem)


def _rope_kernel(pos_ref, freq_ref, sgn_ref, cos_ref, sin_ref):
    ang = pos_ref[0].astype(f32) * freq_ref[...]
    cos_ref[0] = jnp.cos(ang)
    sin_ref[0] = jnp.sin(ang) * sgn_ref[...]


def _rope_tables(positions):
    b, s = positions.shape
    half = HEAD_DIM // 2
    inv = (np.float32(ROPE_BASE) ** (-np.arange(half, dtype=np.float32) / np.float32(half))).astype(np.float32)
    freq = jnp.asarray(np.concatenate([inv, inv])[None, :])
    sgn = jnp.asarray(np.concatenate([-np.ones(half, np.float32), np.ones(half, np.float32)])[None, :])
    out = jax.ShapeDtypeStruct((b, s, HEAD_DIM), f32)
    return pl.pallas_call(
        _rope_kernel,
        grid=(b,),
        in_specs=[pl.BlockSpec((1, s, 1), lambda i: (i, 0, 0)),
                  pl.BlockSpec((1, HEAD_DIM), lambda i: (0, 0)),
                  pl.BlockSpec((1, HEAD_DIM), lambda i: (0, 0))],
        out_specs=[pl.BlockSpec((1, s, HEAD_DIM), lambda i: (i, 0, 0)),
                   pl.BlockSpec((1, s, HEAD_DIM), lambda i: (i, 0, 0))],
        out_shape=[out, out],
        compiler_params=_cparams(("parallel",)),
        name="rope_tables",
    )(positions.reshape(b, s, 1), freq, sgn)


def _inproj_kernel(x_ref, nw_ref, w_ref, wf_ref, o_ref, f_ref, xn_ref):
    @pl.when(pl.program_id(1) == 0)
    def _():
        x = x_ref[...]
        ms = jnp.mean(x * x, axis=-1, keepdims=True)
        xn = (x * lax.rsqrt(ms + EPS) * nw_ref[...]).astype(bf16)
        xn_ref[...] = xn
        f_ref[...] = lax.dot_general(xn, wf_ref[...], _NT, preferred_element_type=f32)

    o_ref[...] = lax.dot_general(xn_ref[...], w_ref[...], _NT, preferred_element_type=f32).astype(o_ref.dtype)


def _in_projection(x2, norm_w, w_main_t, w_f_t):
    t, d = x2.shape
    n = w_main_t.shape[0]
    tm = min(1024, t)
    tn = 1024
    return pl.pallas_call(
        _inproj_kernel,
        grid=(t // tm, n // tn),
        in_specs=[pl.BlockSpec((tm, d), lambda i, j: (i, 0)),
                  pl.BlockSpec((1, d), lambda i, j: (0, 0)),
                  pl.BlockSpec((tn, d), lambda i, j: (j, 0)),
                  pl.BlockSpec((LANES, d), lambda i, j: (0, 0))],
        out_specs=[pl.BlockSpec((tm, tn), lambda i, j: (i, j)),
                   pl.BlockSpec((tm, LANES), lambda i, j: (i, 0))],
        out_shape=[jax.ShapeDtypeStruct((t, n), bf16), jax.ShapeDtypeStruct((t, LANES), f32)],
        scratch_shapes=[pltpu.VMEM((tm, d), bf16)],
        compiler_params=_cparams(("parallel", "arbitrary")),
        name="in_projection",
    )(x2, norm_w, w_main_t, w_f_t)


def _fcum_kernel(f_ref, b_ref, c_ref):
    z = f_ref[...] + b_ref[...]
    ls = jnp.minimum(z, 0.0) - jnp.log1p(jnp.exp(-jnp.abs(z)))
    c = ls.T[:FOX_HEADS, :]
    s = c.shape[1]
    lane = lax.broadcasted_iota(jnp.int32, c.shape, 1)
    sh = 1
    while sh < s:
        c = c + jnp.where(lane >= sh, pltpu.roll(c, sh, axis=1), 0.0)
        sh *= 2
    c_ref[0] = c


def _forget_cumsum(f_logit, bias, b, s):
    return pl.pallas_call(
        _fcum_kernel,
        grid=(b,),
        in_specs=[pl.BlockSpec((s, LANES), lambda i: (i, 0)),
                  pl.BlockSpec((1, LANES), lambda i: (0, 0))],
        out_specs=pl.BlockSpec((1, FOX_HEADS, s), lambda i: (i, 0, 0)),
        out_shape=jax.ShapeDtypeStruct((b, FOX_HEADS, s), f32),
        compiler_params=_cparams(("parallel",)),
        name="forget_cumsum",
    )(f_logit, bias)


def _fox_kernel(q_ref, k_ref, v_ref, c_ref, o_ref, *, tq):
    s_len = q_ref.shape[0]
    nq = s_len // tq
    nt = (((1,), (1,)), ((), ()))
    row = lax.broadcasted_iota(jnp.int32, (tq, tq), 0)
    col = lax.broadcasted_iota(jnp.int32, (tq, tq), 1)
    causal = row >= col
    for qi in range(nq):
        r0 = qi * tq
        q = q_ref[r0:r0 + tq, :]
        sd = lax.dot_general(q, k_ref[r0:r0 + tq, :], nt, preferred_element_type=f32) - c_ref[0, :, r0:r0 + tq]
        sd = jnp.where(causal, sd, NEG_BIG)
        m = jnp.max(sd, axis=-1, keepdims=True)
        if qi:
            so = lax.dot_general(q, k_ref[0:r0, :], nt, preferred_element_type=f32) - c_ref[0, :, 0:r0]
            m = jnp.maximum(m, jnp.max(so, axis=-1, keepdims=True))
        pd = jnp.exp(sd - m)
        l = jnp.sum(pd, axis=-1, keepdims=True)
        acc = jnp.dot(pd.astype(bf16), v_ref[r0:r0 + tq, :], preferred_element_type=f32)
        if qi:
            po = jnp.exp(so - m)
            l = l + jnp.sum(po, axis=-1, keepdims=True)
            acc = acc + jnp.dot(po.astype(bf16), v_ref[0:r0, :], preferred_element_type=f32)
        o_ref[r0:r0 + tq, :] = (acc / l).astype(o_ref.dtype)


def _fox_attention(proj, c_blk, b, s):
    tq = min(ATTN_BLOCK, s)
    h = FOX_HEADS
    return pl.pallas_call(
        functools.partial(_fox_kernel, tq=tq),
        grid=(b, h),
        in_specs=[pl.BlockSpec((s, HEAD_DIM), lambda i, j: (i, COL_FQ + j)),
                  pl.BlockSpec((s, HEAD_DIM), lambda i, j: (i, COL_FK + j)),
                  pl.BlockSpec((s, HEAD_DIM), lambda i, j: (i, COL_FV + j)),
                  pl.BlockSpec((1, 1, s), lambda i, j: (i * FOX_HEADS + j, 0, 0))],
        out_specs=pl.BlockSpec((s, HEAD_DIM), lambda i, j: (i, j)),
        out_shape=jax.ShapeDtypeStruct((b * s, FOX_WIDTH), bf16),
        compiler_params=_cparams(("parallel", "parallel")),
        name="fox_attention",
    )(proj, proj, proj, c_blk)


def _ret_kernel(q_ref, k_ref, v_ref, g_ref, cos_ref, sin_ref, dec_ref, xi_ref, zeta_ref, gc_ref, gnw_ref,
                o_ref, st_ref, *, chunk):
    s_len = q_ref.shape[0]
    nc = s_len // chunk
    st_ref[...] = jnp.zeros_like(st_ref)

    def body(ci, _):
        r0 = pl.multiple_of(ci * chunk, chunk)
        rows = pl.ds(r0, chunk)
        cos = cos_ref[0, rows, :]
        sin = sin_ref[0, rows, :]

        def rot(t):
            t = t.astype(f32)
            return t * cos + pltpu.roll(t, HEAD_DIM // 2, axis=1) * sin

        q = rot(q_ref[rows, :])
        k = rot(k_ref[rows, :])
        v = v_ref[rows, :]
        qb = q.astype(bf16)
        inner = lax.dot_general(qb, k.astype(bf16), (((1,), (1,)), ((), ())), preferred_element_type=f32)
        inner = inner * dec_ref[0]
        state = st_ref[...]
        o = jnp.dot(inner.astype(bf16), v, preferred_element_type=f32)
        o = o + jnp.dot(qb, state.astype(bf16), preferred_element_type=f32) * xi_ref[0]
        kz = (k * zeta_ref[0]).astype(bf16)
        st_ref[...] = state * gc_ref[0] + lax.dot_general(kz, v, (((0,), (0,)), ((), ())),
                                                           preferred_element_type=f32)
        mu = jnp.mean(o, axis=-1, keepdims=True)
        oc = o - mu
        var = jnp.mean(oc * oc, axis=-1, keepdims=True)
        y = oc * lax.rsqrt(var + EPS) * gnw_ref[...]
        g = g_ref[rows, :].astype(f32)
        o_ref[rows, :] = (g * jax.nn.sigmoid(g) * y).astype(o_ref.dtype)
        return 0

    lax.fori_loop(0, nc, body, 0)


def _retention_constants(chunk):
    hh = np.arange(RET_HEADS, dtype=np.float64)
    log_gamma = np.log1p(-(2.0 ** (-5.0 - hh)))
    n = np.arange(chunk, dtype=np.float64)
    diff = n[:, None] - n[None, :]
    decay = np.where(diff[None] >= 0, np.exp(diff[None] * log_gamma[:, None, None]), 0.0)
    xi = np.exp((n[None, :] + 1.0) * log_gamma[:, None])
    zeta = np.exp((chunk - 1.0 - n[None, :]) * log_gamma[:, None])
    g_chunk = np.exp(chunk * log_gamma)
    bc = lambda a: np.broadcast_to(a[:, :, None], (RET_HEADS, chunk, HEAD_DIM))
    gc = np.broadcast_to(g_chunk[:, None, None], (RET_HEADS, 1, HEAD_DIM))
    to = lambda a: jnp.asarray(np.ascontiguousarray(a), dtype=f32)
    return to(decay), to(bc(xi)), to(bc(zeta)), to(gc)


def _retention(proj, cos, sin, gn_w, b, s):
    chunk = min(RET_CHUNK, s)
    decay, xi, zeta, gc = _retention_constants(chunk)
    h = RET_HEADS
    head = lambda col: pl.BlockSpec((s, HEAD_DIM), lambda i, j: (i, col + j))
    per_head = lambda shape: pl.BlockSpec(shape, lambda i, j: (j, 0, 0))
    return pl.pallas_call(
        functools.partial(_ret_kernel, chunk=chunk),
        grid=(b, h),
        in_specs=[head(COL_RQ), head(COL_RK), head(COL_RV), head(COL_RG),
                  pl.BlockSpec((1, s, HEAD_DIM), lambda i, j: (i, 0, 0)),
                  pl.BlockSpec((1, s, HEAD_DIM), lambda i, j: (i, 0, 0)),
                  per_head((1, chunk, chunk)), per_head((1, chunk, HEAD_DIM)), per_head((1, chunk, HEAD_DIM)),
                  per_head((1, 1, HEAD_DIM)),
                  pl.BlockSpec((1, HEAD_DIM), lambda i, j: (0, j))],
        out_specs=pl.BlockSpec((s, HEAD_DIM), lambda i, j: (i, j)),
        out_shape=jax.ShapeDtypeStruct((b * s, RET_WIDTH), bf16),
        scratch_shapes=[pltpu.VMEM((HEAD_DIM, HEAD_DIM), f32)],
        compiler_params=_cparams(("parallel", "parallel")),
        name="retention",
    )(proj, proj, proj, proj, cos, sin, decay, xi, zeta, gc, gn_w)


def _route(logits):
    lane = lax.broadcasted_iota(jnp.int32, logits.shape, 1).astype(f32)
    far = jnp.float32(1e9)
    gmask = lane < N_GROUPS
    gl = jnp.where(gmask, logits, NEG_BIG)
    gmax = jnp.max(gl, axis=-1, keepdims=True)
    gsel = jnp.min(jnp.where(gl == gmax, lane, far), axis=-1, keepdims=True)
    p_group = 1.0 / jnp.sum(jnp.where(gmask, jnp.exp(gl - gmax), 0.0), axis=-1, keepdims=True)
    lo = N_GROUPS + EXPERTS_PER_GROUP * gsel
    emask = (lane >= lo) & (lane < lo + EXPERTS_PER_GROUP)
    el = jnp.where(emask, logits, NEG_BIG)
    m1 = jnp.max(el, axis=-1, keepdims=True)
    i1 = jnp.min(jnp.where(el == m1, lane, far), axis=-1, keepdims=True)
    el2 = jnp.where(lane == i1, NEG_BIG, el)
    m2 = jnp.max(el2, axis=-1, keepdims=True)
    i2 = jnp.min(jnp.where(el2 == m2, lane, far), axis=-1, keepdims=True)
    t = jnp.exp(m2 - m1)
    w1 = p_group / (1.0 + t)
    w2 = p_group * t / (1.0 + t)
    out = jnp.where(lane == 0, i1 - N_GROUPS, 0.0)
    out = jnp.where(lane == 1, i2 - N_GROUPS, out)
    out = jnp.where(lane == 2, w1, out)
    out = jnp.where(lane == 3, w2, out)
    return out


def _merge_kernel(fo_ref, ro_ref, g0_ref, g1_ref, bg_ref, x_ref, wb_ref, wo_ref, nfw_ref, wrh_ref, wrl_ref, br_ref,
                  h_ref, hs_ref, route_ref, mg_ref, *, ncol):
    d = x_ref.shape[1]
    cw = d // ncol
    fo = fo_ref[...]
    ro = ro_ref[...]
    for c in range(ncol):
        cs = slice(c * cw, (c + 1) * cw)
        bd0 = jnp.dot(fo, wb_ref[0, :, cs], preferred_element_type=f32)
        bd1 = jnp.dot(ro, wb_ref[1, :, cs], preferred_element_type=f32)
        ga = jax.nn.sigmoid(g0_ref[:, cs].astype(f32) + bg_ref[0:1, cs])
        gb = jax.nn.sigmoid(g1_ref[:, cs].astype(f32) + bg_ref[1:2, cs])
        mg_ref[:, cs] = (ga * bd0 + gb * bd1).astype(bf16)
    h = x_ref[...] + jnp.dot(mg_ref[...], wo_ref[...], preferred_element_type=f32)
    h_ref[...] = h
    _rows_to_slabs(hs_ref, h, h.shape[0])
    hn = h * lax.rsqrt(jnp.mean(h * h, axis=-1, keepdims=True) + EPS) * nfw_ref[...]
    hi = hn.astype(bf16)
    lo = (hn - hi.astype(f32)).astype(bf16)
    logits = (jnp.dot(hi, wrh_ref[...], preferred_element_type=f32)
              + jnp.dot(lo, wrh_ref[...], preferred_element_type=f32)
              + jnp.dot(hi, wrl_ref[...], preferred_element_type=f32)) + br_ref[...]
    route_ref[...] = _route(logits)


def _merge(fox_o, ret_o, proj, b_gate, x2, w_branch, w_out, nfw, wr_hi, wr_lo, b_router):
    t, d = x2.shape
    tm = min(256, t)
    gate0 = COL_GATE * HEAD_DIM // d
    const = lambda shape: pl.BlockSpec(shape, lambda i: tuple(0 for _ in shape), pipeline_mode=pl.Buffered(1))
    return pl.pallas_call(
        functools.partial(_merge_kernel, ncol=4),
        grid=(t // tm,),
        in_specs=[pl.BlockSpec((tm, FOX_WIDTH), lambda i: (i, 0)),
                  pl.BlockSpec((tm, RET_WIDTH), lambda i: (i, 0)),
                  pl.BlockSpec((tm, d), lambda i: (i, gate0)),
                  pl.BlockSpec((tm, d), lambda i: (i, gate0 + 1)),
                  const((N_BRANCH, d)),
                  pl.BlockSpec((tm, d), lambda i: (i, 0)),
                  const((N_BRANCH, FOX_WIDTH, d)),
                  const((d, d)),
                  const((1, d)),
                  const((d, LANES)), const((d, LANES)), const((1, LANES))],
        out_specs=[pl.BlockSpec((tm, d), lambda i: (i, 0)),
                   pl.BlockSpec((tm * PITCH, LANES), lambda i: (i, 0)),
                   pl.BlockSpec((tm, LANES), lambda i: (i, 0))],
        out_shape=[jax.ShapeDtypeStruct((t, d), f32), jax.ShapeDtypeStruct((t * PITCH, LANES), f32),
                   jax.ShapeDtypeStruct((t, LANES), f32)],
        scratch_shapes=[pltpu.VMEM((tm, d), bf16)],
        compiler_params=_cparams(("parallel",)),
        name="merge_outproj_router",
    )(fox_o, ret_o, proj, proj, b_gate, x2, w_branch, w_out, nfw, wr_hi, wr_lo, b_router)


def _rank_kernel(route_ref, rank_ref, cnt_ref, carry_ref):
    @pl.when(pl.program_id(0) == 0)
    def _():
        carry_ref[...] = jnp.zeros_like(carry_ref)

    r = route_ref[...]
    tb = r.shape[0]
    lane = lax.broadcasted_iota(jnp.int32, r.shape, 1).astype(f32)
    o0 = (lane == r[:, 0:1]).astype(f32)
    o1 = (lane == r[:, 1:2]).astype(f32)
    o = o0 + o1
    row = lax.broadcasted_iota(jnp.int32, (tb, tb), 0)
    col = lax.broadcasted_iota(jnp.int32, (tb, tb), 1)
    lower = jnp.where(col < row, 1.0, 0.0).astype(bf16)
    before = jnp.dot(lower, o.astype(bf16), preferred_element_type=f32) + carry_ref[...]
    r0 = jnp.sum(before * o0, axis=-1, keepdims=True)
    r1 = jnp.sum(before * o1, axis=-1, keepdims=True)
    rank_ref[...] = jnp.where(lane == 0, r0, jnp.where(lane == 1, r1, 0.0)).astype(jnp.int32)
    carry_ref[...] += jnp.sum(o, axis=0, keepdims=True)
    cnt_ref[...] = carry_ref[...]


def _dispatch_ranks(route):
    t = route.shape[0]
    tb = min(512, t)
    return pl.pallas_call(
        _rank_kernel,
        grid=(t // tb,),
        in_specs=[pl.BlockSpec((tb, LANES), lambda i: (i, 0))],
        out_specs=[pl.BlockSpec((tb, LANES), lambda i: (i, 0)),
                   pl.BlockSpec((1, LANES), lambda i: (0, 0))],
        out_shape=[jax.ShapeDtypeStruct((t, LANES), jnp.int32), jax.ShapeDtypeStruct((1, LANES), f32)],
        scratch_shapes=[pltpu.VMEM((1, LANES), f32)],
        compiler_params=_cparams(("arbitrary",)),
        name="dispatch_ranks",
    )(route)


def _moe_kernel(be_ref, nu_ref, tok_ref, h_hbm, nfw_ref, w1_ref, w3_ref, w2_ref, y_ref,
                xbuf, w1b, w3b, w2b, sem, *, bm):
    i = pl.program_id(0)
    n_used = nu_ref[0]
    slot = i % 2

    @pl.when(i == 0)
    def _():
        _slab_gather_start(tok_ref, 0, bm, h_hbm, xbuf.at[0], sem.at[0])

    @pl.when(i + 1 < n_used)
    def _():
        _slab_gather_start(tok_ref, (i + 1) * bm, bm, h_hbm, xbuf.at[1 - slot], sem.at[1 - slot])

    @pl.when((i == 0) | (be_ref[i] != be_ref[jnp.maximum(i - 1, 0)]))
    def _():
        w1b[...] = w1_ref[0].astype(bf16)
        w3b[...] = w3_ref[0].astype(bf16)
        w2b[...] = w2_ref[0].astype(bf16)

    @pl.when(i < n_used)
    def _():
        _slab_gather_wait(bm, h_hbm, xbuf.at[slot], sem.at[slot])
        h = _slabs_to_rows(xbuf.at[slot], bm)
        hn = (h * lax.rsqrt(jnp.mean(h * h, axis=-1, keepdims=True) + EPS) * nfw_ref[...]).astype(bf16)
        a = jnp.dot(hn, w1b[...], preferred_element_type=f32)
        b = jnp.dot(hn, w3b[...], preferred_element_type=f32)
        mid = (a * jax.nn.sigmoid(a) * b).astype(bf16)
        _rows_to_slabs(y_ref, jnp.dot(mid, w2b[...], preferred_element_type=f32), bm)

    @pl.when(i >= n_used)
    def _():
        y_ref[...] = jnp.zeros_like(y_ref)


def _expert_mlp(blk_expert, n_used, buf_tok, h3, nfw, w1, w3, w2, cap):
    d = D_MODEL
    bm = MOE_BLOCK
    n_blk = cap // bm
    grid_spec = pltpu.PrefetchScalarGridSpec(
        num_scalar_prefetch=3,
        grid=(n_blk,),
        in_specs=[pl.BlockSpec(memory_space=pl.ANY),
                  pl.BlockSpec((1, d), lambda i, be, nu, tok: (0, 0)),
                  pl.BlockSpec((1, d, D_EXPERT), lambda i, be, nu, tok: (be[i], 0, 0)),
                  pl.BlockSpec((1, d, D_EXPERT), lambda i, be, nu, tok: (be[i], 0, 0)),
                  pl.BlockSpec((1, D_EXPERT, d), lambda i, be, nu, tok: (be[i], 0, 0))],
        out_specs=pl.BlockSpec((bm * PITCH, LANES), lambda i, be, nu, tok: (i, 0)),
        scratch_shapes=[pltpu.VMEM((2, bm * PITCH, LANES), f32),
                        pltpu.VMEM((d, D_EXPERT), bf16), pltpu.VMEM((d, D_EXPERT), bf16),
                        pltpu.VMEM((D_EXPERT, d), bf16),
                        pltpu.SemaphoreType.DMA((2,))],
    )
    return pl.pallas_call(
        functools.partial(_moe_kernel, bm=bm),
        grid_spec=grid_spec,
        out_shape=jax.ShapeDtypeStruct((cap * PITCH, LANES), f32),
        compiler_params=_cparams(("arbitrary",)),
        name="expert_mlp",
    )(blk_expert, n_used, buf_tok, h3, nfw, w1, w3, w2)


def _combine_kernel(d0_ref, d1_ref, h_ref, route_ref, y_hbm, nw_ref, o_ref, ybuf, sem, *, tb):
    i = pl.program_id(0)
    n = pl.num_programs(0)
    slot = i % 2

    def start(blk, s):
        _slab_gather_start(d0_ref, blk * tb, tb, y_hbm, ybuf.at[s, 0], sem.at[s])
        _slab_gather_start(d1_ref, blk * tb, tb, y_hbm, ybuf.at[s, 1], sem.at[s])

    @pl.when(i == 0)
    def _():
        start(0, 0)

    @pl.when(i + 1 < n)
    def _():
        start(i + 1, 1 - slot)

    for k in range(TOP_K):
        _slab_gather_wait(tb, y_hbm, ybuf.at[slot, k], sem.at[slot])
    r = route_ref[...]
    h = h_ref[...] + (r[:, 2:3] * _slabs_to_rows(ybuf.at[slot, 0], tb)
                      + r[:, 3:4] * _slabs_to_rows(ybuf.at[slot, 1], tb))
    o_ref[...] = h * lax.rsqrt(jnp.mean(h * h, axis=-1, keepdims=True) + EPS) * nw_ref[...]


def _combine(dest0, dest1, h, route, ys, norm_w):
    t, d = h.shape
    tb = min(256, t)
    grid_spec = pltpu.PrefetchScalarGridSpec(
        num_scalar_prefetch=2,
        grid=(t // tb,),
        in_specs=[pl.BlockSpec((tb, d), lambda i, a, b: (i, 0)),
                  pl.BlockSpec((tb, LANES), lambda i, a, b: (i, 0)),
                  pl.BlockSpec(memory_space=pl.ANY),
                  pl.BlockSpec((1, d), lambda i, a, b: (0, 0))],
        out_specs=pl.BlockSpec((tb, d), lambda i, a, b: (i, 0)),
        scratch_shapes=[pltpu.VMEM((2, TOP_K, tb * PITCH, LANES), f32), pltpu.SemaphoreType.DMA((2,))],
    )
    return pl.pallas_call(
        functools.partial(_combine_kernel, tb=tb),
        grid_spec=grid_spec,
        out_shape=jax.ShapeDtypeStruct((t, d), f32),
        compiler_params=_cparams(("arbitrary",)),
        name="combine_final_norm",
    )(dest0, dest1, h, route, ys, norm_w)


def kernel(x, positions, norm_mix_w, w_in, fox_b_f, ret_gn_w, w_branch, b_gate, w_out, norm_ffn_w, w_router_group,
           b_router_group, w_router_expert, b_router_expert, w1, w3, w2, norm_final_w):
    b, s, d = x.shape
    assert d == D_MODEL and norm_mix_w.shape[0] == 1
    t = b * s
    x2 = x.reshape(t, d)
    scale = 1.0 / math.sqrt(HEAD_DIM)

    wt = w_in[0].T
    o_f = 3 * FOX_WIDTH
    o_r = o_f + FOX_HEADS
    o_g = o_r + 4 * RET_WIDTH
    w_main = jnp.concatenate([wt[o_g:], wt[:FOX_WIDTH] * scale, wt[FOX_WIDTH:o_f],
                              wt[o_r:o_r + RET_WIDTH] * scale, wt[o_r + RET_WIDTH:o_g]], axis=0).astype(bf16)
    w_f = jnp.pad(wt[o_f:o_r], ((0, LANES - FOX_HEADS), (0, 0))).astype(bf16)
    bias_f = jnp.pad(fox_b_f[0], (0, LANES - FOX_HEADS)).reshape(1, LANES)
    w_r = jnp.concatenate([w_router_group[0], w_router_expert[0]], axis=1)
    w_r = jnp.pad(w_r, ((0, 0), (0, LANES - w_r.shape[1])))
    wr_hi = w_r.astype(bf16)
    wr_lo = (w_r - wr_hi.astype(f32)).astype(bf16)
    b_r = jnp.pad(jnp.concatenate([b_router_group[0], b_router_expert[0]]), (0, LANES - N_GROUPS - N_EXPERTS))
    b_r = b_r.reshape(1, LANES)

    cos, sin = _rope_tables(positions)
    proj, f_logit = _in_projection(x2, norm_mix_w, w_main, w_f)
    c = _forget_cumsum(f_logit, bias_f, b, s)
    fox_o = _fox_attention(proj, c.reshape(b * FOX_HEADS, 1, s), b, s)
    ret_o = _retention(proj, cos, sin, ret_gn_w, b, s)
    h, h_slabs, route = _merge(fox_o, ret_o, proj, b_gate[0], x2, w_branch[0].astype(bf16), w_out[0].astype(bf16),
                      norm_ffn_w, wr_hi, wr_lo, b_r)

    rank, counts = _dispatch_ranks(route)
    bm = MOE_BLOCK
    m = t * TOP_K
    cap = m + N_EXPERTS * bm
    n_blk = cap // bm
    counts = counts[0, :N_EXPERTS].astype(jnp.int32)
    padded = (counts + bm - 1) // bm * bm
    pends = jnp.cumsum(padded)
    pstarts = pends - padded
    e_sel = route[:, :TOP_K].astype(jnp.int32)
    dest = pstarts[e_sel] + rank[:, :TOP_K]
    tok = jnp.broadcast_to(jnp.arange(t, dtype=jnp.int32)[:, None], (t, TOP_K))
    buf_tok = jnp.zeros((cap,), jnp.int32).at[dest.reshape(m)].set(tok.reshape(m))
    blk_start = jnp.arange(n_blk, dtype=jnp.int32) * bm
    blk_expert = jnp.minimum(jnp.sum((pends[None, :] <= blk_start[:, None]).astype(jnp.int32), axis=1),
                             N_EXPERTS - 1)
    n_used = (pends[-1:] // bm).astype(jnp.int32)

    y = _expert_mlp(blk_expert, n_used, buf_tok, h_slabs, norm_ffn_w, w1[0], w3[0], w2[0], cap)
    out = _combine(dest[:, 0], dest[:, 1], h, route, y, norm_final_w.reshape(1, d))
    return out.reshape(b, s, d)
```

```python
import functools
import math

import numpy as np
import jax
import jax.numpy as jnp
from jax import lax
from jax.experimental import pallas as pl
from jax.experimental.pallas import tpu as pltpu

D_MODEL = 2048
HEAD_DIM = 128
FOX_HEADS = 8
RET_HEADS = 8
FOX_WIDTH = FOX_HEADS * HEAD_DIM
RET_WIDTH = RET_HEADS * HEAD_DIM
N_BRANCH = 2
N_GROUPS = 4
EXPERTS_PER_GROUP = 8
N_EXPERTS = N_GROUPS * EXPERTS_PER_GROUP
TOP_K = 2
D_EXPERT = 512
ROPE_BASE = 10000.0
EPS = 1e-6

LANES = 128
NEG_BIG = -1e30
MAIN_WIDTH = 3 * FOX_WIDTH + 4 * RET_WIDTH + N_BRANCH * D_MODEL
COL_GATE = 0
COL_FQ = N_BRANCH * D_MODEL // HEAD_DIM
COL_FK = COL_FQ + FOX_HEADS
COL_FV = COL_FK + FOX_HEADS
COL_RQ = COL_FV + FOX_HEADS
COL_RK = COL_RQ + RET_HEADS
COL_RV = COL_RK + RET_HEADS
COL_RG = COL_RV + RET_HEADS

ATTN_BLOCK = 256
RET_CHUNK = 256
MOE_BLOCK = 256
VMEM_LIMIT = 56 * 1024 * 1024

f32 = jnp.float32
bf16 = jnp.bfloat16
_NT = (((1,), (1,)), ((), ()))
SLAB = D_MODEL // LANES
PITCH = SLAB + 1


def _rows_to_slabs(ref2, val, n_rows):
    for j in range(SLAB):
        ref2[pl.ds(j, n_rows, stride=PITCH), :] = val[:, j * LANES:(j + 1) * LANES]
    ref2[pl.ds(SLAB, n_rows, stride=PITCH), :] = jnp.zeros((n_rows, LANES), val.dtype)


def _slabs_to_rows(ref2, n_rows):
    return jnp.concatenate([ref2[pl.ds(j, n_rows, stride=PITCH), :] for j in range(SLAB)], axis=-1)


def _slab_gather_start(idx_ref, base, n_rows, src_hbm, dst, sem, unroll=8):
    def body(g, _):
        for u in range(unroll):
            r = g * unroll + u
            t = idx_ref[base + r]
            pltpu.make_async_copy(src_hbm.at[pl.ds(t * PITCH, SLAB), :], dst.at[pl.ds(r * PITCH, SLAB), :], sem).start()
        return 0
    lax.fori_loop(0, n_rows // unroll, body, 0)


def _slab_gather_wait(n_rows, src_hbm, dst, sem):
    pltpu.make_async_copy(src_hbm.at[pl.ds(0, n_rows * SLAB), :], dst.at[pl.ds(0, n_rows * SLAB), :], sem).wait()


def _cparams(sem, vmem=VMEM_LIMIT):
    return pltpu.CompilerParams(dimension_semantics=sem, vmem_limit_bytes=vmem)


def _rope_kernel(pos_ref, freq_ref, sgn_ref, cos_ref, sin_ref):
    ang = pos_ref[0].astype(f32) * freq_ref[...]
    cos_ref[0] = jnp.cos(ang)
    sin_ref[0] = jnp.sin(ang) * sgn_ref[...]


def _rope_tables(positions):
    b, s = positions.shape
    half = HEAD_DIM // 2
    inv = (np.float32(ROPE_BASE) ** (-np.arange(half, dtype=np.float32) / np.float32(half))).astype(np.float32)
    freq = jnp.asarray(np.concatenate([inv, inv])[None, :])
    sgn = jnp.asarray(np.concatenate([-np.ones(half, np.float32), np.ones(half, np.float32)])[None, :])
    out = jax.ShapeDtypeStruct((b, s, HEAD_DIM), f32)
    return pl.pallas_call(
        _rope_kernel,
        grid=(b,),
        in_specs=[pl.BlockSpec((1, s, 1), lambda i: (i, 0, 0)),
                  pl.BlockSpec((1, HEAD_DIM), lambda i: (0, 0)),
                  pl.BlockSpec((1, HEAD_DIM), lambda i: (0, 0))],
        out_specs=[pl.BlockSpec((1, s, HEAD_DIM), lambda i: (i, 0, 0)),
                   pl.BlockSpec((1, s, HEAD_DIM), lambda i: (i, 0, 0))],
        out_shape=[out, out],
        compiler_params=_cparams(("parallel",)),
        name="rope_tables",
    )(positions.reshape(b, s, 1), freq, sgn)


def _inproj_kernel(x_ref, nw_ref, w_ref, wf_ref, o_ref, f_ref, xn_ref):
    @pl.when(pl.program_id(1) == 0)
    def _():
        x = x_ref[...]
        ms = jnp.mean(x * x, axis=-1, keepdims=True)
        xn = (x * lax.rsqrt(ms + EPS) * nw_ref[...]).astype(bf16)
        xn_ref[...] = xn
        f_ref[...] = lax.dot_general(xn, wf_ref[...], _NT, preferred_element_type=f32)

    o_ref[...] = lax.dot_general(xn_ref[...], w_ref[...], _NT, preferred_element_type=f32).astype(o_ref.dtype)


def _in_projection(x2, norm_w, w_main_t, w_f_t):
    t, d = x2.shape
    n = w_main_t.shape[0]
    tm = min(1024, t)
    tn = 1024
    return pl.pallas_call(
        _inproj_kernel,
        grid=(t // tm, n // tn),
        in_specs=[pl.BlockSpec((tm, d), lambda i, j: (i, 0)),
                  pl.BlockSpec((1, d), lambda i, j: (0, 0)),
                  pl.BlockSpec((tn, d), lambda i, j: (j, 0)),
                  pl.BlockSpec((LANES, d), lambda i, j: (0, 0))],
        out_specs=[pl.BlockSpec((tm, tn), lambda i, j: (i, j)),
                   pl.BlockSpec((tm, LANES), lambda i, j: (i, 0))],
        out_shape=[jax.ShapeDtypeStruct((t, n), bf16), jax.ShapeDtypeStruct((t, LANES), f32)],
        scratch_shapes=[pltpu.VMEM((tm, d), bf16)],
        compiler_params=_cparams(("parallel", "arbitrary")),
        name="in_projection",
    )(x2, norm_w, w_main_t, w_f_t)


def _fcum_kernel(f_ref, b_ref, c_ref):
    z = f_ref[...] + b_ref[...]
    ls = jnp.minimum(z, 0.0) - jnp.log1p(jnp.exp(-jnp.abs(z)))
    c = ls.T[:FOX_HEADS, :]
    s = c.shape[1]
    lane = lax.broadcasted_iota(jnp.int32, c.shape, 1)
    sh = 1
    while sh < s:
        c = c + jnp.where(lane >= sh, pltpu.roll(c, sh, axis=1), 0.0)
        sh *= 2
    c_ref[0] = c


def _forget_cumsum(f_logit, bias, b, s):
    return pl.pallas_call(
        _fcum_kernel,
        grid=(b,),
        in_specs=[pl.BlockSpec((s, LANES), lambda i: (i, 0)),
                  pl.BlockSpec((1, LANES), lambda i: (0, 0))],
        out_specs=pl.BlockSpec((1, FOX_HEADS, s), lambda i: (i, 0, 0)),
        out_shape=jax.ShapeDtypeStruct((b, FOX_HEADS, s), f32),
        compiler_params=_cparams(("parallel",)),
        name="forget_cumsum",
    )(f_logit, bias)


def _fox_kernel(q_ref, k_ref, v_ref, c_ref, o_ref, *, tq):
    s_len = q_ref.shape[0]
    nq = s_len // tq
    nt = (((1,), (1,)), ((), ()))
    row = lax.broadcasted_iota(jnp.int32, (tq, tq), 0)
    col = lax.broadcasted_iota(jnp.int32, (tq, tq), 1)
    causal = row >= col
    for qi in range(nq):
        r0 = qi * tq
        q = q_ref[r0:r0 + tq, :]
        sd = lax.dot_general(q, k_ref[r0:r0 + tq, :], nt, preferred_element_type=f32) - c_ref[0, :, r0:r0 + tq]
        sd = jnp.where(causal, sd, NEG_BIG)
        m = jnp.max(sd, axis=-1, keepdims=True)
        if qi:
            so = lax.dot_general(q, k_ref[0:r0, :], nt, preferred_element_type=f32) - c_ref[0, :, 0:r0]
            m = jnp.maximum(m, jnp.max(so, axis=-1, keepdims=True))
        pd = jnp.exp(sd - m)
        l = jnp.sum(pd, axis=-1, keepdims=True)
        acc = jnp.dot(pd.astype(bf16), v_ref[r0:r0 + tq, :], preferred_element_type=f32)
        if qi:
            po = jnp.exp(so - m)
            l = l + jnp.sum(po, axis=-1, keepdims=True)
            acc = acc + jnp.dot(po.astype(bf16), v_ref[0:r0, :], preferred_element_type=f32)
        o_ref[r0:r0 + tq, :] = (acc / l).astype(o_ref.dtype)


def _fox_attention(proj, c_blk, b, s):
    tq = min(ATTN_BLOCK, s)
    h = FOX_HEADS
    return pl.pallas_call(
        functools.partial(_fox_kernel, tq=tq),
        grid=(b, h),
        in_specs=[pl.BlockSpec((s, HEAD_DIM), lambda i, j: (i, COL_FQ + j)),
                  pl.BlockSpec((s, HEAD_DIM), lambda i, j: (i, COL_FK + j)),
                  pl.BlockSpec((s, HEAD_DIM), lambda i, j: (i, COL_FV + j)),
                  pl.BlockSpec((1, 1, s), lambda i, j: (i * FOX_HEADS + j, 0, 0))],
        out_specs=pl.BlockSpec((s, HEAD_DIM), lambda i, j: (i, j)),
        out_shape=jax.ShapeDtypeStruct((b * s, FOX_WIDTH), bf16),
        compiler_params=_cparams(("parallel", "parallel")),
        name="fox_attention",
    )(proj, proj, proj, c_blk)


def _ret_kernel(q_ref, k_ref, v_ref, g_ref, cos_ref, sin_ref, dec_ref, xi_ref, zeta_ref, gc_ref, gnw_ref,
                o_ref, *, chunk):
    s_len = q_ref.shape[0]
    nc = s_len // chunk
    state = jnp.zeros((HEAD_DIM, HEAD_DIM), f32)
    for ci in range(nc):
        rows = slice(ci * chunk, (ci + 1) * chunk)
        cos = cos_ref[0, rows, :]
        sin = sin_ref[0, rows, :]

        def rot(t):
            t = t.astype(f32)
            return t * cos + pltpu.roll(t, HEAD_DIM // 2, axis=1) * sin

        q = rot(q_ref[rows, :])
        k = rot(k_ref[rows, :])
        v = v_ref[rows, :]
        qb = q.astype(bf16)
        inner = lax.dot_general(qb, k.astype(bf16), _NT, preferred_element_type=f32) * dec_ref[0]
        o = jnp.dot(inner.astype(bf16), v, preferred_element_type=f32)
        if ci:
            o = o + jnp.dot(qb, state.astype(bf16), preferred_element_type=f32) * xi_ref[0]
        if ci + 1 < nc:
            kz = (k * zeta_ref[0]).astype(bf16)
            kv = lax.dot_general(kz, v, (((0,), (0,)), ((), ())), preferred_element_type=f32)
            state = state * gc_ref[0] + kv if ci else kv
        mu = jnp.mean(o, axis=-1, keepdims=True)
        oc = o - mu
        var = jnp.mean(oc * oc, axis=-1, keepdims=True)
        y = oc * lax.rsqrt(var + EPS) * gnw_ref[...]
        g = g_ref[rows, :].astype(f32)
        o_ref[rows, :] = (g * jax.nn.sigmoid(g) * y).astype(o_ref.dtype)


def _retention_constants(chunk):
    hh = np.arange(RET_HEADS, dtype=np.float64)
    log_gamma = np.log1p(-(2.0 ** (-5.0 - hh)))
    n = np.arange(chunk, dtype=np.float64)
    diff = n[:, None] - n[None, :]
    decay = np.where(diff[None] >= 0, np.exp(diff[None] * log_gamma[:, None, None]), 0.0)
    xi = np.exp((n[None, :] + 1.0) * log_gamma[:, None])
    zeta = np.exp((chunk - 1.0 - n[None, :]) * log_gamma[:, None])
    g_chunk = np.exp(chunk * log_gamma)
    bc = lambda a: np.broadcast_to(a[:, :, None], (RET_HEADS, chunk, HEAD_DIM))
    gc = np.broadcast_to(g_chunk[:, None, None], (RET_HEADS, 1, HEAD_DIM))
    to = lambda a: jnp.asarray(np.ascontiguousarray(a), dtype=f32)
    return to(decay), to(bc(xi)), to(bc(zeta)), to(gc)


def _retention(proj, cos, sin, gn_w, b, s):
    chunk = min(RET_CHUNK, s)
    decay, xi, zeta, gc = _retention_constants(chunk)
    h = RET_HEADS
    head = lambda col: pl.BlockSpec((s, HEAD_DIM), lambda i, j: (i, col + j))
    per_head = lambda shape: pl.BlockSpec(shape, lambda i, j: (j, 0, 0))
    return pl.pallas_call(
        functools.partial(_ret_kernel, chunk=chunk),
        grid=(b, h),
        in_specs=[head(COL_RQ), head(COL_RK), head(COL_RV), head(COL_RG),
                  pl.BlockSpec((1, s, HEAD_DIM), lambda i, j: (i, 0, 0)),
                  pl.BlockSpec((1, s, HEAD_DIM), lambda i, j: (i, 0, 0)),
                  per_head((1, chunk, chunk)), per_head((1, chunk, HEAD_DIM)), per_head((1, chunk, HEAD_DIM)),
                  per_head((1, 1, HEAD_DIM)),
                  pl.BlockSpec((1, HEAD_DIM), lambda i, j: (0, j))],
        out_specs=pl.BlockSpec((s, HEAD_DIM), lambda i, j: (i, j)),
        out_shape=jax.ShapeDtypeStruct((b * s, RET_WIDTH), bf16),
        compiler_params=_cparams(("parallel", "parallel")),
        name="retention",
    )(proj, proj, proj, proj, cos, sin, decay, xi, zeta, gc, gn_w)


def _route(logits):
    lane = lax.broadcasted_iota(jnp.int32, logits.shape, 1).astype(f32)
    far = jnp.float32(1e9)
    gmask = lane < N_GROUPS
    gl = jnp.where(gmask, logits, NEG_BIG)
    gmax = jnp.max(gl, axis=-1, keepdims=True)
    gsel = jnp.min(jnp.where(gl == gmax, lane, far), axis=-1, keepdims=True)
    p_group = 1.0 / jnp.sum(jnp.where(gmask, jnp.exp(gl - gmax), 0.0), axis=-1, keepdims=True)
    lo = N_GROUPS + EXPERTS_PER_GROUP * gsel
    emask = (lane >= lo) & (lane < lo + EXPERTS_PER_GROUP)
    el = jnp.where(emask, logits, NEG_BIG)
    m1 = jnp.max(el, axis=-1, keepdims=True)
    i1 = jnp.min(jnp.where(el == m1, lane, far), axis=-1, keepdims=True)
    el2 = jnp.where(lane == i1, NEG_BIG, el)
    m2 = jnp.max(el2, axis=-1, keepdims=True)
    i2 = jnp.min(jnp.where(el2 == m2, lane, far), axis=-1, keepdims=True)
    t = jnp.exp(m2 - m1)
    w1 = p_group / (1.0 + t)
    w2 = p_group * t / (1.0 + t)
    out = jnp.where(lane == 0, i1 - N_GROUPS, 0.0)
    out = jnp.where(lane == 1, i2 - N_GROUPS, out)
    out = jnp.where(lane == 2, w1, out)
    out = jnp.where(lane == 3, w2, out)
    return out


def _merge_kernel(fo_ref, ro_ref, g0_ref, g1_ref, bg_ref, x_ref, wb_ref, wo_ref, nfw_ref, wrh_ref, wrl_ref, br_ref,
                  h_ref, hs_ref, route_ref, mg_ref, *, ncol):
    d = x_ref.shape[1]
    cw = d // ncol
    fo = fo_ref[...]
    ro = ro_ref[...]
    for c in range(ncol):
        cs = slice(c * cw, (c + 1) * cw)
        bd0 = jnp.dot(fo, wb_ref[0, :, cs], preferred_element_type=f32)
        bd1 = jnp.dot(ro, wb_ref[1, :, cs], preferred_element_type=f32)
        ga = jax.nn.sigmoid(g0_ref[:, cs].astype(f32) + bg_ref[0:1, cs])
        gb = jax.nn.sigmoid(g1_ref[:, cs].astype(f32) + bg_ref[1:2, cs])
        mg_ref[:, cs] = (ga * bd0 + gb * bd1).astype(bf16)
    h = x_ref[...] + jnp.dot(mg_ref[...], wo_ref[...], preferred_element_type=f32)
    h_ref[...] = h
    _rows_to_slabs(hs_ref, h, h.shape[0])
    hn = h * lax.rsqrt(jnp.mean(h * h, axis=-1, keepdims=True) + EPS) * nfw_ref[...]
    hi = hn.astype(bf16)
    lo = (hn - hi.astype(f32)).astype(bf16)
    logits = (jnp.dot(hi, wrh_ref[...], preferred_element_type=f32)
              + jnp.dot(lo, wrh_ref[...], preferred_element_type=f32)
              + jnp.dot(hi, wrl_ref[...], preferred_element_type=f32)) + br_ref[...]
    route_ref[...] = _route(logits)


def _merge(fox_o, ret_o, proj, b_gate, x2, w_branch, w_out, nfw, wr_hi, wr_lo, b_router):
    t, d = x2.shape
    tm = min(256, t)
    gate0 = COL_GATE * HEAD_DIM // d
    const = lambda shape: pl.BlockSpec(shape, lambda i: tuple(0 for _ in shape), pipeline_mode=pl.Buffered(1))
    return pl.pallas_call(
        functools.partial(_merge_kernel, ncol=4),
        grid=(t // tm,),
        in_specs=[pl.BlockSpec((tm, FOX_WIDTH), lambda i: (i, 0)),
                  pl.BlockSpec((tm, RET_WIDTH), lambda i: (i, 0)),
                  pl.BlockSpec((tm, d), lambda i: (i, gate0)),
                  pl.BlockSpec((tm, d), lambda i: (i, gate0 + 1)),
                  const((N_BRANCH, d)),
                  pl.BlockSpec((tm, d), lambda i: (i, 0)),
                  const((N_BRANCH, FOX_WIDTH, d)),
                  const((d, d)),
                  const((1, d)),
                  const((d, LANES)), const((d, LANES)), const((1, LANES))],
        out_specs=[pl.BlockSpec((tm, d), lambda i: (i, 0)),
                   pl.BlockSpec((tm * PITCH, LANES), lambda i: (i, 0)),
                   pl.BlockSpec((tm, LANES), lambda i: (i, 0))],
        out_shape=[jax.ShapeDtypeStruct((t, d), f32), jax.ShapeDtypeStruct((t * PITCH, LANES), f32),
                   jax.ShapeDtypeStruct((t, LANES), f32)],
        scratch_shapes=[pltpu.VMEM((tm, d), bf16)],
        compiler_params=_cparams(("parallel",)),
        name="merge_outproj_router",
    )(fox_o, ret_o, proj, proj, b_gate, x2, w_branch, w_out, nfw, wr_hi, wr_lo, b_router)


def _rank_kernel(route_ref, rank_ref, cnt_ref, carry_ref):
    @pl.when(pl.program_id(0) == 0)
    def _():
        carry_ref[...] = jnp.zeros_like(carry_ref)

    r = route_ref[...]
    tb = r.shape[0]
    lane = lax.broadcasted_iota(jnp.int32, r.shape, 1).astype(f32)
    o0 = (lane == r[:, 0:1]).astype(f32)
    o1 = (lane == r[:, 1:2]).astype(f32)
    o = o0 + o1
    row = lax.broadcasted_iota(jnp.int32, (tb, tb), 0)
    col = lax.broadcasted_iota(jnp.int32, (tb, tb), 1)
    lower = jnp.where(col < row, 1.0, 0.0).astype(bf16)
    before = jnp.dot(lower, o.astype(bf16), preferred_element_type=f32) + carry_ref[...]
    r0 = jnp.sum(before * o0, axis=-1, keepdims=True)
    r1 = jnp.sum(before * o1, axis=-1, keepdims=True)
    rank_ref[...] = jnp.where(lane == 0, r0, jnp.where(lane == 1, r1, 0.0)).astype(jnp.int32)
    carry_ref[...] += jnp.sum(o, axis=0, keepdims=True)
    cnt_ref[...] = carry_ref[...]


def _dispatch_ranks(route):
    t = route.shape[0]
    tb = min(512, t)
    return pl.pallas_call(
        _rank_kernel,
        grid=(t // tb,),
        in_specs=[pl.BlockSpec((tb, LANES), lambda i: (i, 0))],
        out_specs=[pl.BlockSpec((tb, LANES), lambda i: (i, 0)),
                   pl.BlockSpec((1, LANES), lambda i: (0, 0))],
        out_shape=[jax.ShapeDtypeStruct((t, LANES), jnp.int32), jax.ShapeDtypeStruct((1, LANES), f32)],
        scratch_shapes=[pltpu.VMEM((1, LANES), f32)],
        compiler_params=_cparams(("arbitrary",)),
        name="dispatch_ranks",
    )(route)


def _moe_kernel(g0_ref, nb_ref, nu_ref, tok_ref, h_hbm, nfw_ref, w1_ref, w3_ref, w2_ref, y_hbm,
                xbuf, ybuf, w1b, w3b, w2b, gsem, ysem, *, bm, n_blk):
    e = pl.program_id(0)
    n_used = nu_ref[0]
    g0 = g0_ref[e]
    nb = nb_ref[e]
    blk_rows = bm * PITCH

    def y_copy(slot, g):
        dst = y_hbm.at[pl.ds(pl.multiple_of(g * blk_rows, 8), blk_rows), :]
        return pltpu.make_async_copy(ybuf.at[slot], dst, ysem.at[slot])

    @pl.when(e == 0)
    def _():
        _slab_gather_start(tok_ref, 0, bm, h_hbm, xbuf.at[0], gsem.at[0])

    @pl.when(nb > 0)
    def _():
        w1b[...] = w1_ref[0].astype(bf16)
        w3b[...] = w3_ref[0].astype(bf16)
        w2b[...] = w2_ref[0].astype(bf16)

    def block(c, carry):
        g = g0 + c
        slot = g % 2

        @pl.when(g + 1 < n_used)
        def _():
            _slab_gather_start(tok_ref, (g + 1) * bm, bm, h_hbm, xbuf.at[1 - slot], gsem.at[1 - slot])

        _slab_gather_wait(bm, h_hbm, xbuf.at[slot], gsem.at[slot])
        h = _slabs_to_rows(xbuf.at[slot], bm)
        hn = (h * lax.rsqrt(jnp.mean(h * h, axis=-1, keepdims=True) + EPS) * nfw_ref[...]).astype(bf16)
        a = jnp.dot(hn, w1b[...], preferred_element_type=f32)
        b = jnp.dot(hn, w3b[...], preferred_element_type=f32)
        mid = (a * jax.nn.sigmoid(a) * b).astype(bf16)
        y = jnp.dot(mid, w2b[...], preferred_element_type=f32)

        @pl.when(g >= 2)
        def _():
            y_copy(slot, g - 2).wait()

        _rows_to_slabs(ybuf.at[slot], y, bm)
        y_copy(slot, g).start()
        return carry

    lax.fori_loop(0, nb, block, 0)

    @pl.when(e == pl.num_programs(0) - 1)
    def _():
        @pl.when(n_used >= 2)
        def _():
            y_copy(n_used % 2, n_used - 2).wait()

        y_copy((n_used - 1) % 2, n_used - 1).wait()
        ybuf[0] = jnp.zeros(ybuf.shape[1:], f32)

        def zstart(g, carry):
            y_copy(0, g).start()
            return carry

        def zwait(g, carry):
            y_copy(0, g).wait()
            return carry

        lax.fori_loop(n_used, n_blk, zstart, 0)
        lax.fori_loop(n_used, n_blk, zwait, 0)


def _expert_mlp(blk_start, blk_count, n_used, buf_tok, hs, nfw, w1, w3, w2, cap):
    d = D_MODEL
    bm = MOE_BLOCK
    n_blk = cap // bm
    by_expert = lambda shape: pl.BlockSpec(shape, lambda e, g0, nb, nu, tok: (e, 0, 0))
    grid_spec = pltpu.PrefetchScalarGridSpec(
        num_scalar_prefetch=4,
        grid=(N_EXPERTS,),
        in_specs=[pl.BlockSpec(memory_space=pl.ANY),
                  pl.BlockSpec((1, d), lambda e, g0, nb, nu, tok: (0, 0)),
                  by_expert((1, d, D_EXPERT)), by_expert((1, d, D_EXPERT)), by_expert((1, D_EXPERT, d))],
        out_specs=pl.BlockSpec(memory_space=pl.ANY),
        scratch_shapes=[pltpu.VMEM((2, bm * PITCH, LANES), f32), pltpu.VMEM((2, bm * PITCH, LANES), f32),
                        pltpu.VMEM((d, D_EXPERT), bf16), pltpu.VMEM((d, D_EXPERT), bf16),
                        pltpu.VMEM((D_EXPERT, d), bf16),
                        pltpu.SemaphoreType.DMA((2,)), pltpu.SemaphoreType.DMA((2,))],
    )
    return pl.pallas_call(
        functools.partial(_moe_kernel, bm=bm, n_blk=n_blk),
        grid_spec=grid_spec,
        out_shape=jax.ShapeDtypeStruct((cap * PITCH, LANES), f32),
        compiler_params=_cparams(("arbitrary",)),
        name="expert_mlp",
    )(blk_start, blk_count, n_used, buf_tok, hs, nfw, w1, w3, w2)


def _combine_kernel(d0_ref, d1_ref, h_ref, route_ref, y_hbm, nw_ref, o_ref, ybuf, sem, *, tb):
    i = pl.program_id(0)
    n = pl.num_programs(0)
    slot = i % 2

    def start(blk, s):
        _slab_gather_start(d0_ref, blk * tb, tb, y_hbm, ybuf.at[s, 0], sem.at[s])
        _slab_gather_start(d1_ref, blk * tb, tb, y_hbm, ybuf.at[s, 1], sem.at[s])

    @pl.when(i == 0)
    def _():
        start(0, 0)

    @pl.when(i + 1 < n)
    def _():
        start(i + 1, 1 - slot)

    for k in range(TOP_K):
        _slab_gather_wait(tb, y_hbm, ybuf.at[slot, k], sem.at[slot])
    r = route_ref[...]
    h = h_ref[...] + (r[:, 2:3] * _slabs_to_rows(ybuf.at[slot, 0], tb)
                      + r[:, 3:4] * _slabs_to_rows(ybuf.at[slot, 1], tb))
    o_ref[...] = h * lax.rsqrt(jnp.mean(h * h, axis=-1, keepdims=True) + EPS) * nw_ref[...]


def _combine(dest0, dest1, h, route, ys, norm_w):
    t, d = h.shape
    tb = min(256, t)
    grid_spec = pltpu.PrefetchScalarGridSpec(
        num_scalar_prefetch=2,
        grid=(t // tb,),
        in_specs=[pl.BlockSpec((tb, d), lambda i, a, b: (i, 0)),
                  pl.BlockSpec((tb, LANES), lambda i, a, b: (i, 0)),
                  pl.BlockSpec(memory_space=pl.ANY),
                  pl.BlockSpec((1, d), lambda i, a, b: (0, 0))],
        out_specs=pl.BlockSpec((tb, d), lambda i, a, b: (i, 0)),
        scratch_shapes=[pltpu.VMEM((2, TOP_K, tb * PITCH, LANES), f32), pltpu.SemaphoreType.DMA((2,))],
    )
    return pl.pallas_call(
        functools.partial(_combine_kernel, tb=tb),
        grid_spec=grid_spec,
        out_shape=jax.ShapeDtypeStruct((t, d), f32),
        compiler_params=_cparams(("arbitrary",)),
        name="combine_final_norm",
    )(dest0, dest1, h, route, ys, norm_w)


def kernel(x, positions, norm_mix_w, w_in, fox_b_f, ret_gn_w, w_branch, b_gate, w_out, norm_ffn_w, w_router_group,
           b_router_group, w_router_expert, b_router_expert, w1, w3, w2, norm_final_w):
    b, s, d = x.shape
    assert d == D_MODEL and norm_mix_w.shape[0] == 1
    t = b * s
    x2 = x.reshape(t, d)
    scale = 1.0 / math.sqrt(HEAD_DIM)

    wt = w_in[0].T
    o_f = 3 * FOX_WIDTH
    o_r = o_f + FOX_HEADS
    o_g = o_r + 4 * RET_WIDTH
    w_main = jnp.concatenate([wt[o_g:], wt[:FOX_WIDTH] * scale, wt[FOX_WIDTH:o_f],
                              wt[o_r:o_r + RET_WIDTH] * scale, wt[o_r + RET_WIDTH:o_g]], axis=0).astype(bf16)
    w_f = jnp.pad(wt[o_f:o_r], ((0, LANES - FOX_HEADS), (0, 0))).astype(bf16)
    bias_f = jnp.pad(fox_b_f[0], (0, LANES - FOX_HEADS)).reshape(1, LANES)
    w_r = jnp.concatenate([w_router_group[0], w_router_expert[0]], axis=1)
    w_r = jnp.pad(w_r, ((0, 0), (0, LANES - w_r.shape[1])))
    wr_hi = w_r.astype(bf16)
    wr_lo = (w_r - wr_hi.astype(f32)).astype(bf16)
    b_r = jnp.pad(jnp.concatenate([b_router_group[0], b_router_expert[0]]), (0, LANES - N_GROUPS - N_EXPERTS))
    b_r = b_r.reshape(1, LANES)

    cos, sin = _rope_tables(positions)
    proj, f_logit = _in_projection(x2, norm_mix_w, w_main, w_f)
    c = _forget_cumsum(f_logit, bias_f, b, s)
    fox_o = _fox_attention(proj, c.reshape(b * FOX_HEADS, 1, s), b, s)
    ret_o = _retention(proj, cos, sin, ret_gn_w, b, s)
    h, h_slabs, route = _merge(fox_o, ret_o, proj, b_gate[0], x2, w_branch[0].astype(bf16), w_out[0].astype(bf16),
                      norm_ffn_w, wr_hi, wr_lo, b_r)

    rank, counts = _dispatch_ranks(route)
    bm = MOE_BLOCK
    m = t * TOP_K
    cap = m + N_EXPERTS * bm
    n_blk = cap // bm
    counts = counts[0, :N_EXPERTS].astype(jnp.int32)
    padded = (counts + bm - 1) // bm * bm
    pends = jnp.cumsum(padded)
    pstarts = pends - padded
    e_sel = route[:, :TOP_K].astype(jnp.int32)
    dest = pstarts[e_sel] + rank[:, :TOP_K]
    tok = jnp.broadcast_to(jnp.arange(t, dtype=jnp.int32)[:, None], (t, TOP_K))
    buf_tok = jnp.zeros((cap,), jnp.int32).at[dest.reshape(m)].set(tok.reshape(m))
    n_used = (pends[-1:] // bm).astype(jnp.int32)

    y = _expert_mlp((pstarts // bm).astype(jnp.int32), (padded // bm).astype(jnp.int32), n_used, buf_tok,
                    h_slabs, norm_ffn_w, w1[0], w3[0], w2[0], cap)
    out = _combine(dest[:, 0], dest[:, 1], h, route, y, norm_final_w.reshape(1, d))
    return out.reshape(b, s, d)
```

```python
import functools
import math

import numpy as np
import jax
import jax.numpy as jnp
from jax import lax
from jax.experimental import pallas as pl
from jax.experimental.pallas import tpu as pltpu

D_MODEL = 2048
HEAD_DIM = 128
FOX_HEADS = 8
RET_HEADS = 8
FOX_WIDTH = FOX_HEADS * HEAD_DIM
RET_WIDTH = RET_HEADS * HEAD_DIM
N_BRANCH = 2
N_GROUPS = 4
EXPERTS_PER_GROUP = 8
N_EXPERTS = N_GROUPS * EXPERTS_PER_GROUP
TOP_K = 2
D_EXPERT = 512
ROPE_BASE = 10000.0
EPS = 1e-6

LANES = 128
NEG_BIG = -1e30
MAIN_WIDTH = 3 * FOX_WIDTH + 4 * RET_WIDTH + N_BRANCH * D_MODEL
COL_GATE = 0
COL_FQ = N_BRANCH * D_MODEL // HEAD_DIM
COL_FK = COL_FQ + FOX_HEADS
COL_FV = COL_FK + FOX_HEADS
COL_RQ = COL_FV + FOX_HEADS
COL_RK = COL_RQ + RET_HEADS
COL_RV = COL_RK + RET_HEADS
COL_RG = COL_RV + RET_HEADS

ATTN_BLOCK = 256
RET_CHUNK = 256
MOE_BLOCK = 256
VMEM_LIMIT = 56 * 1024 * 1024

f32 = jnp.float32
bf16 = jnp.bfloat16
_NT = (((1,), (1,)), ((), ()))
SLAB = D_MODEL // (2 * LANES)
PITCH = SLAB + 1
u32 = jnp.uint32
_HI_MASK = np.uint32(0xFFFF0000)


def _rows_to_slabs(ref2, val, n_rows):
    bits = lambda v: lax.bitcast_convert_type(v.astype(bf16).astype(f32), u32)
    for j in range(SLAB):
        lo = val[:, (2 * j) * LANES:(2 * j + 1) * LANES]
        hi = val[:, (2 * j + 1) * LANES:(2 * j + 2) * LANES]
        ref2[pl.ds(j, n_rows, stride=PITCH), :] = (bits(lo) >> 16) | (bits(hi) & _HI_MASK)
    ref2[pl.ds(SLAB, n_rows, stride=PITCH), :] = jnp.zeros((n_rows, LANES), u32)


def _slabs_to_rows(ref2, n_rows):
    chunks = []
    for j in range(SLAB):
        w = ref2[pl.ds(j, n_rows, stride=PITCH), :]
        chunks.append(lax.bitcast_convert_type(w << 16, f32))
        chunks.append(lax.bitcast_convert_type(w & _HI_MASK, f32))
    return jnp.concatenate(chunks, axis=-1)


def _slab_gather_start(idx_ref, base, n_rows, src_hbm, dst, sem, unroll=8):
    def body(g, _):
        for u in range(unroll):
            r = g * unroll + u
            t = idx_ref[base + r]
            pltpu.make_async_copy(src_hbm.at[pl.ds(t * PITCH, SLAB), :], dst.at[pl.ds(r * PITCH, SLAB), :], sem).start()
        return 0
    lax.fori_loop(0, n_rows // unroll, body, 0)


def _slab_gather_wait(n_rows, src_hbm, dst, sem):
    pltpu.make_async_copy(src_hbm.at[pl.ds(0, n_rows * SLAB), :], dst.at[pl.ds(0, n_rows * SLAB), :], sem).wait()


def _cparams(sem, vmem=VMEM_LIMIT):
    return pltpu.CompilerParams(dimension_semantics=sem, vmem_limit_bytes=vmem)


def _rope_kernel(pos_ref, freq_ref, sgn_ref, cos_ref, sin_ref):
    ang = pos_ref[0].astype(f32) * freq_ref[...]
    cos_ref[0] = jnp.cos(ang)
    sin_ref[0] = jnp.sin(ang) * sgn_ref[...]


def _rope_tables(positions):
    b, s = positions.shape
    half = HEAD_DIM // 2
    inv = (np.float32(ROPE_BASE) ** (-np.arange(half, dtype=np.float32) / np.float32(half))).astype(np.float32)
    freq = jnp.asarray(np.concatenate([inv, inv])[None, :])
    sgn = jnp.asarray(np.concatenate([-np.ones(half, np.float32), np.ones(half, np.float32)])[None, :])
    out = jax.ShapeDtypeStruct((b, s, HEAD_DIM), f32)
    return pl.pallas_call(
        _rope_kernel,
        grid=(b,),
        in_specs=[pl.BlockSpec((1, s, 1), lambda i: (i, 0, 0)),
                  pl.BlockSpec((1, HEAD_DIM), lambda i: (0, 0)),
                  pl.BlockSpec((1, HEAD_DIM), lambda i: (0, 0))],
        out_specs=[pl.BlockSpec((1, s, HEAD_DIM), lambda i: (i, 0, 0)),
                   pl.BlockSpec((1, s, HEAD_DIM), lambda i: (i, 0, 0))],
        out_shape=[out, out],
        compiler_params=_cparams(("parallel",)),
        name="rope_tables",
    )(positions.reshape(b, s, 1), freq, sgn)


def _inproj_kernel(x_ref, nw_ref, w_ref, wf_ref, o_ref, f_ref, xn_ref):
    @pl.when(pl.program_id(1) == 0)
    def _():
        x = x_ref[...]
        ms = jnp.mean(x * x, axis=-1, keepdims=True)
        xn = (x * lax.rsqrt(ms + EPS) * nw_ref[...]).astype(bf16)
        xn_ref[...] = xn
        f_ref[...] = lax.dot_general(xn, wf_ref[...], _NT, preferred_element_type=f32)

    o_ref[...] = lax.dot_general(xn_ref[...], w_ref[...], _NT, preferred_element_type=f32).astype(o_ref.dtype)


def _in_projection(x2, norm_w, w_main_t, w_f_t):
    t, d = x2.shape
    n = w_main_t.shape[0]
    tm = min(1024, t)
    tn = 1024
    return pl.pallas_call(
        _inproj_kernel,
        grid=(t // tm, n // tn),
        in_specs=[pl.BlockSpec((tm, d), lambda i, j: (i, 0)),
                  pl.BlockSpec((1, d), lambda i, j: (0, 0)),
                  pl.BlockSpec((tn, d), lambda i, j: (j, 0)),
                  pl.BlockSpec((LANES, d), lambda i, j: (0, 0))],
        out_specs=[pl.BlockSpec((tm, tn), lambda i, j: (i, j)),
                   pl.BlockSpec((tm, LANES), lambda i, j: (i, 0))],
        out_shape=[jax.ShapeDtypeStruct((t, n), bf16), jax.ShapeDtypeStruct((t, LANES), f32)],
        scratch_shapes=[pltpu.VMEM((tm, d), bf16)],
        compiler_params=_cparams(("parallel", "arbitrary")),
        name="in_projection",
    )(x2, norm_w, w_main_t, w_f_t)


def _fcum_kernel(f_ref, b_ref, c_ref):
    z = f_ref[...] + b_ref[...]
    ls = jnp.minimum(z, 0.0) - jnp.log1p(jnp.exp(-jnp.abs(z)))
    c = ls.T[:FOX_HEADS, :]
    s = c.shape[1]
    lane = lax.broadcasted_iota(jnp.int32, c.shape, 1)
    sh = 1
    while sh < s:
        c = c + jnp.where(lane >= sh, pltpu.roll(c, sh, axis=1), 0.0)
        sh *= 2
    c_ref[0] = c


def _forget_cumsum(f_logit, bias, b, s):
    return pl.pallas_call(
        _fcum_kernel,
        grid=(b,),
        in_specs=[pl.BlockSpec((s, LANES), lambda i: (i, 0)),
                  pl.BlockSpec((1, LANES), lambda i: (0, 0))],
        out_specs=pl.BlockSpec((1, FOX_HEADS, s), lambda i: (i, 0, 0)),
        out_shape=jax.ShapeDtypeStruct((b, FOX_HEADS, s), f32),
        compiler_params=_cparams(("parallel",)),
        name="forget_cumsum",
    )(f_logit, bias)


def _fox_kernel(q_ref, k_ref, v_ref, c_ref, o_ref, *, tq):
    s_len = q_ref.shape[0]
    nq = s_len // tq
    nt = (((1,), (1,)), ((), ()))
    row = lax.broadcasted_iota(jnp.int32, (tq, tq), 0)
    col = lax.broadcasted_iota(jnp.int32, (tq, tq), 1)
    causal = row >= col
    for qi in range(nq):
        r0 = qi * tq
        q = q_ref[r0:r0 + tq, :]
        sd = lax.dot_general(q, k_ref[r0:r0 + tq, :], nt, preferred_element_type=f32) - c_ref[0, :, r0:r0 + tq]
        sd = jnp.where(causal, sd, NEG_BIG)
        m = jnp.max(sd, axis=-1, keepdims=True)
        if qi:
            so = lax.dot_general(q, k_ref[0:r0, :], nt, preferred_element_type=f32) - c_ref[0, :, 0:r0]
            m = jnp.maximum(m, jnp.max(so, axis=-1, keepdims=True))
        pd = jnp.exp(sd - m)
        l = jnp.sum(pd, axis=-1, keepdims=True)
        acc = jnp.dot(pd.astype(bf16), v_ref[r0:r0 + tq, :], preferred_element_type=f32)
        if qi:
            po = jnp.exp(so - m)
            l = l + jnp.sum(po, axis=-1, keepdims=True)
            acc = acc + jnp.dot(po.astype(bf16), v_ref[0:r0, :], preferred_element_type=f32)
        o_ref[r0:r0 + tq, :] = (acc / l).astype(o_ref.dtype)


def _fox_attention(proj, c_blk, b, s):
    tq = min(ATTN_BLOCK, s)
    h = FOX_HEADS
    return pl.pallas_call(
        functools.partial(_fox_kernel, tq=tq),
        grid=(b, h),
        in_specs=[pl.BlockSpec((s, HEAD_DIM), lambda i, j: (i, COL_FQ + j)),
                  pl.BlockSpec((s, HEAD_DIM), lambda i, j: (i, COL_FK + j)),
                  pl.BlockSpec((s, HEAD_DIM), lambda i, j: (i, COL_FV + j)),
                  pl.BlockSpec((1, 1, s), lambda i, j: (i * FOX_HEADS + j, 0, 0))],
        out_specs=pl.BlockSpec((s, HEAD_DIM), lambda i, j: (i, j)),
        out_shape=jax.ShapeDtypeStruct((b * s, FOX_WIDTH), bf16),
        compiler_params=_cparams(("parallel", "parallel")),
        name="fox_attention",
    )(proj, proj, proj, c_blk)


def _ret_kernel(q_ref, k_ref, v_ref, g_ref, cos_ref, sin_ref, dec_ref, xi_ref, zeta_ref, gc_ref, gnw_ref,
                o_ref, *, chunk):
    s_len = q_ref.shape[0]
    nc = s_len // chunk
    state = jnp.zeros((HEAD_DIM, HEAD_DIM), f32)
    for ci in range(nc):
        rows = slice(ci * chunk, (ci + 1) * chunk)
        cos = cos_ref[0, rows, :]
        sin = sin_ref[0, rows, :]

        def rot(t):
            t = t.astype(f32)
            return t * cos + pltpu.roll(t, HEAD_DIM // 2, axis=1) * sin

        q = rot(q_ref[rows, :])
        k = rot(k_ref[rows, :])
        v = v_ref[rows, :]
        qb = q.astype(bf16)
        inner = lax.dot_general(qb, k.astype(bf16), _NT, preferred_element_type=f32) * dec_ref[0]
        o = jnp.dot(inner.astype(bf16), v, preferred_element_type=f32)
        if ci:
            o = o + jnp.dot(qb, state.astype(bf16), preferred_element_type=f32) * xi_ref[0]
        if ci + 1 < nc:
            kz = (k * zeta_ref[0]).astype(bf16)
            kv = lax.dot_general(kz, v, (((0,), (0,)), ((), ())), preferred_element_type=f32)
            state = state * gc_ref[0] + kv if ci else kv
        mu = jnp.mean(o, axis=-1, keepdims=True)
        oc = o - mu
        var = jnp.mean(oc * oc, axis=-1, keepdims=True)
        y = oc * lax.rsqrt(var + EPS) * gnw_ref[...]
        g = g_ref[rows, :].astype(f32)
        o_ref[rows, :] = (g * jax.nn.sigmoid(g) * y).astype(o_ref.dtype)


def _retention_constants(chunk):
    hh = np.arange(RET_HEADS, dtype=np.float64)
    log_gamma = np.log1p(-(2.0 ** (-5.0 - hh)))
    n = np.arange(chunk, dtype=np.float64)
    diff = n[:, None] - n[None, :]
    decay = np.where(diff[None] >= 0, np.exp(diff[None] * log_gamma[:, None, None]), 0.0)
    xi = np.exp((n[None, :] + 1.0) * log_gamma[:, None])
    zeta = np.exp((chunk - 1.0 - n[None, :]) * log_gamma[:, None])
    g_chunk = np.exp(chunk * log_gamma)
    bc = lambda a: np.broadcast_to(a[:, :, None], (RET_HEADS, chunk, HEAD_DIM))
    gc = np.broadcast_to(g_chunk[:, None, None], (RET_HEADS, 1, HEAD_DIM))
    to = lambda a: jnp.asarray(np.ascontiguousarray(a), dtype=f32)
    return to(decay), to(bc(xi)), to(bc(zeta)), to(gc)


def _retention(proj, cos, sin, gn_w, b, s):
    chunk = min(RET_CHUNK, s)
    decay, xi, zeta, gc = _retention_constants(chunk)
    h = RET_HEADS
    head = lambda col: pl.BlockSpec((s, HEAD_DIM), lambda i, j: (i, col + j))
    per_head = lambda shape: pl.BlockSpec(shape, lambda i, j: (j, 0, 0))
    return pl.pallas_call(
        functools.partial(_ret_kernel, chunk=chunk),
        grid=(b, h),
        in_specs=[head(COL_RQ), head(COL_RK), head(COL_RV), head(COL_RG),
                  pl.BlockSpec((1, s, HEAD_DIM), lambda i, j: (i, 0, 0)),
                  pl.BlockSpec((1, s, HEAD_DIM), lambda i, j: (i, 0, 0)),
                  per_head((1, chunk, chunk)), per_head((1, chunk, HEAD_DIM)), per_head((1, chunk, HEAD_DIM)),
                  per_head((1, 1, HEAD_DIM)),
                  pl.BlockSpec((1, HEAD_DIM), lambda i, j: (0, j))],
        out_specs=pl.BlockSpec((s, HEAD_DIM), lambda i, j: (i, j)),
        out_shape=jax.ShapeDtypeStruct((b * s, RET_WIDTH), bf16),
        compiler_params=_cparams(("parallel", "parallel")),
        name="retention",
    )(proj, proj, proj, proj, cos, sin, decay, xi, zeta, gc, gn_w)


def _route(logits):
    lane = lax.broadcasted_iota(jnp.int32, logits.shape, 1).astype(f32)
    far = jnp.float32(1e9)
    gmask = lane < N_GROUPS
    gl = jnp.where(gmask, logits, NEG_BIG)
    gmax = jnp.max(gl, axis=-1, keepdims=True)
    gsel = jnp.min(jnp.where(gl == gmax, lane, far), axis=-1, keepdims=True)
    p_group = 1.0 / jnp.sum(jnp.where(gmask, jnp.exp(gl - gmax), 0.0), axis=-1, keepdims=True)
    lo = N_GROUPS + EXPERTS_PER_GROUP * gsel
    emask = (lane >= lo) & (lane < lo + EXPERTS_PER_GROUP)
    el = jnp.where(emask, logits, NEG_BIG)
    m1 = jnp.max(el, axis=-1, keepdims=True)
    i1 = jnp.min(jnp.where(el == m1, lane, far), axis=-1, keepdims=True)
    el2 = jnp.where(lane == i1, NEG_BIG, el)
    m2 = jnp.max(el2, axis=-1, keepdims=True)
    i2 = jnp.min(jnp.where(el2 == m2, lane, far), axis=-1, keepdims=True)
    t = jnp.exp(m2 - m1)
    w1 = p_group / (1.0 + t)
    w2 = p_group * t / (1.0 + t)
    out = jnp.where(lane == 0, i1 - N_GROUPS, 0.0)
    out = jnp.where(lane == 1, i2 - N_GROUPS, out)
    out = jnp.where(lane == 2, w1, out)
    out = jnp.where(lane == 3, w2, out)
    return out


def _merge_kernel(fo_ref, ro_ref, g0_ref, g1_ref, bg_ref, x_ref, wb_ref, wo_ref, nfw_ref, wrh_ref, wrl_ref, br_ref,
                  h_ref, hs_ref, route_ref, mg_ref, *, ncol):
    d = x_ref.shape[1]
    cw = d // ncol
    fo = fo_ref[...]
    ro = ro_ref[...]
    for c in range(ncol):
        cs = slice(c * cw, (c + 1) * cw)
        bd0 = jnp.dot(fo, wb_ref[0, :, cs], preferred_element_type=f32)
        bd1 = jnp.dot(ro, wb_ref[1, :, cs], preferred_element_type=f32)
        ga = jax.nn.sigmoid(g0_ref[:, cs].astype(f32) + bg_ref[0:1, cs])
        gb = jax.nn.sigmoid(g1_ref[:, cs].astype(f32) + bg_ref[1:2, cs])
        mg_ref[:, cs] = (ga * bd0 + gb * bd1).astype(bf16)
    h = x_ref[...] + jnp.dot(mg_ref[...], wo_ref[...], preferred_element_type=f32)
    h_ref[...] = h
    hn = h * lax.rsqrt(jnp.mean(h * h, axis=-1, keepdims=True) + EPS) * nfw_ref[...]
    _rows_to_slabs(hs_ref, hn, hn.shape[0])
    hi = hn.astype(bf16)
    lo = (hn - hi.astype(f32)).astype(bf16)
    logits = (jnp.dot(hi, wrh_ref[...], preferred_element_type=f32)
              + jnp.dot(lo, wrh_ref[...], preferred_element_type=f32)
              + jnp.dot(hi, wrl_ref[...], preferred_element_type=f32)) + br_ref[...]
    route_ref[...] = _route(logits)


def _merge(fox_o, ret_o, proj, b_gate, x2, w_branch, w_out, nfw, wr_hi, wr_lo, b_router):
    t, d = x2.shape
    tm = min(256, t)
    gate0 = COL_GATE * HEAD_DIM // d
    const = lambda shape: pl.BlockSpec(shape, lambda i: tuple(0 for _ in shape), pipeline_mode=pl.Buffered(1))
    return pl.pallas_call(
        functools.partial(_merge_kernel, ncol=4),
        grid=(t // tm,),
        in_specs=[pl.BlockSpec((tm, FOX_WIDTH), lambda i: (i, 0)),
                  pl.BlockSpec((tm, RET_WIDTH), lambda i: (i, 0)),
                  pl.BlockSpec((tm, d), lambda i: (i, gate0)),
                  pl.BlockSpec((tm, d), lambda i: (i, gate0 + 1)),
                  const((N_BRANCH, d)),
                  pl.BlockSpec((tm, d), lambda i: (i, 0)),
                  const((N_BRANCH, FOX_WIDTH, d)),
                  const((d, d)),
                  const((1, d)),
                  const((d, LANES)), const((d, LANES)), const((1, LANES))],
        out_specs=[pl.BlockSpec((tm, d), lambda i: (i, 0)),
                   pl.BlockSpec((tm * PITCH, LANES), lambda i: (i, 0)),
                   pl.BlockSpec((tm, LANES), lambda i: (i, 0))],
        out_shape=[jax.ShapeDtypeStruct((t, d), f32), jax.ShapeDtypeStruct((t * PITCH, LANES), u32),
                   jax.ShapeDtypeStruct((t, LANES), f32)],
        scratch_shapes=[pltpu.VMEM((tm, d), bf16)],
        compiler_params=_cparams(("parallel",)),
        name="merge_outproj_router",
    )(fox_o, ret_o, proj, proj, b_gate, x2, w_branch, w_out, nfw, wr_hi, wr_lo, b_router)


def _plan_kernel(route_ref, dest_ref, ends_ref, carry_ref, *, bm):
    phase = pl.program_id(0)
    i = pl.program_id(1)
    r = route_ref[...]
    tb = r.shape[0]
    lane = lax.broadcasted_iota(jnp.int32, r.shape, 1).astype(f32)
    o0 = (lane == r[:, 0:1]).astype(f32)
    o1 = (lane == r[:, 1:2]).astype(f32)
    o = o0 + o1

    @pl.when((phase == 0) & (i == 0))
    def _():
        carry_ref[...] = jnp.zeros_like(carry_ref)

    @pl.when((phase == 1) & (i == 0))
    def _():
        cnt = carry_ref[...]
        padded = jnp.floor((cnt + (bm - 1)) * (1.0 / bm)) * bm
        lane8 = lax.broadcasted_iota(jnp.int32, cnt.shape, 1)
        ends = padded
        sh = 1
        while sh < LANES:
            ends = ends + jnp.where(lane8 >= sh, pltpu.roll(ends, sh, axis=1), 0.0)
            sh *= 2
        sub = lax.broadcasted_iota(jnp.int32, cnt.shape, 0)
        ends_ref[...] = jnp.where(sub == 0, ends, padded)
        carry_ref[...] = ends - padded

    @pl.when(phase == 1)
    def _():
        row = lax.broadcasted_iota(jnp.int32, (tb, tb), 0)
        col = lax.broadcasted_iota(jnp.int32, (tb, tb), 1)
        lower = jnp.where(col < row, 1.0, 0.0).astype(bf16)
        before = jnp.dot(lower, o.astype(bf16), preferred_element_type=f32) + carry_ref[0:1, :]
        d0 = jnp.sum(before * o0, axis=-1, keepdims=True)
        d1 = jnp.sum(before * o1, axis=-1, keepdims=True)
        dd = jnp.where(lane == 0, d0, jnp.where(lane == 1, d1, 0.0))
        dest_ref[0] = dd.T[:8, :].astype(jnp.int32)

    carry_ref[...] += jnp.sum(o, axis=0, keepdims=True)


def _dispatch_plan(route, bm):
    t = route.shape[0]
    tb = min(512, t)
    nb = t // tb
    return pl.pallas_call(
        functools.partial(_plan_kernel, bm=bm),
        grid=(2, nb),
        in_specs=[pl.BlockSpec((tb, LANES), lambda p, i: (i, 0))],
        out_specs=[pl.BlockSpec((1, 8, tb), lambda p, i: (i * p, 0, 0)),
                   pl.BlockSpec((8, LANES), lambda p, i: (0, 0))],
        out_shape=[jax.ShapeDtypeStruct((nb, 8, tb), jnp.int32), jax.ShapeDtypeStruct((8, LANES), f32)],
        scratch_shapes=[pltpu.VMEM((8, LANES), f32)],
        compiler_params=_cparams(("arbitrary", "arbitrary")),
        name="dispatch_plan",
    )(route)


def _slot_kernel(d0_ref, d1_ref, zeros_hbm, tok_ref, sem, *, t, unroll=8):
    init = pltpu.make_async_copy(zeros_hbm, tok_ref, sem)
    init.start()
    init.wait()

    def fill(i, c):
        for u in range(unroll):
            tt = i * unroll + u
            tok_ref[d0_ref[tt]] = tt
            tok_ref[d1_ref[tt]] = tt
        return c

    lax.fori_loop(0, t // unroll, fill, 0)


def _slot_tokens(dest0, dest1, cap):
    t = dest0.shape[0]
    smem = pl.BlockSpec(memory_space=pltpu.SMEM)
    return pl.pallas_call(
        functools.partial(_slot_kernel, t=t),
        in_specs=[smem, smem, pl.BlockSpec(memory_space=pl.ANY)],
        out_specs=smem,
        out_shape=jax.ShapeDtypeStruct((cap,), jnp.int32),
        scratch_shapes=[pltpu.SemaphoreType.DMA(())],
        name="slot_tokens",
    )(dest0, dest1, jnp.zeros((cap,), jnp.int32))


def _moe_kernel(g0_ref, nb_ref, nu_ref, tok_ref, h_hbm, w1_ref, w3_ref, w2_ref, y_hbm,
                xbuf, ybuf, w1b, w3b, w2b, gsem, ysem, *, bm, n_blk):
    e = pl.program_id(0)
    n_used = nu_ref[0]
    g0 = g0_ref[e]
    nb = nb_ref[e]
    blk_rows = bm * PITCH

    def y_copy(slot, g):
        dst = y_hbm.at[pl.ds(pl.multiple_of(g * blk_rows, 8), blk_rows), :]
        return pltpu.make_async_copy(ybuf.at[slot], dst, ysem.at[slot])

    @pl.when(e == 0)
    def _():
        _slab_gather_start(tok_ref, 0, bm, h_hbm, xbuf.at[0], gsem.at[0])

    @pl.when(nb > 0)
    def _():
        w1b[...] = w1_ref[0].astype(bf16)
        w3b[...] = w3_ref[0].astype(bf16)
        w2b[...] = w2_ref[0].astype(bf16)

    def block(c, carry):
        g = g0 + c
        slot = g % 2

        @pl.when(g + 1 < n_used)
        def _():
            _slab_gather_start(tok_ref, (g + 1) * bm, bm, h_hbm, xbuf.at[1 - slot], gsem.at[1 - slot])

        _slab_gather_wait(bm, h_hbm, xbuf.at[slot], gsem.at[slot])
        hn = _slabs_to_rows(xbuf.at[slot], bm).astype(bf16)
        a = jnp.dot(hn, w1b[...], preferred_element_type=f32)
        b = jnp.dot(hn, w3b[...], preferred_element_type=f32)
        mid = (a * jax.nn.sigmoid(a) * b).astype(bf16)
        y = jnp.dot(mid, w2b[...], preferred_element_type=f32)

        @pl.when(g >= 2)
        def _():
            y_copy(slot, g - 2).wait()

        _rows_to_slabs(ybuf.at[slot], y, bm)
        y_copy(slot, g).start()
        return carry

    lax.fori_loop(0, nb, block, 0)

    @pl.when(e == pl.num_programs(0) - 1)
    def _():
        @pl.when(n_used >= 2)
        def _():
            y_copy(n_used % 2, n_used - 2).wait()

        y_copy((n_used - 1) % 2, n_used - 1).wait()
        ybuf[0] = jnp.zeros(ybuf.shape[1:], u32)

        def zstart(g, carry):
            y_copy(0, g).start()
            return carry

        def zwait(g, carry):
            y_copy(0, g).wait()
            return carry

        lax.fori_loop(n_used, n_blk, zstart, 0)
        lax.fori_loop(n_used, n_blk, zwait, 0)


def _expert_mlp(blk_start, blk_count, n_used, buf_tok, hs, w1, w3, w2, cap):
    d = D_MODEL
    bm = MOE_BLOCK
    n_blk = cap // bm
    by_expert = lambda shape: pl.BlockSpec(shape, lambda e, g0, nb, nu, tok: (e, 0, 0))
    grid_spec = pltpu.PrefetchScalarGridSpec(
        num_scalar_prefetch=4,
        grid=(N_EXPERTS,),
        in_specs=[pl.BlockSpec(memory_space=pl.ANY),
                  by_expert((1, d, D_EXPERT)), by_expert((1, d, D_EXPERT)), by_expert((1, D_EXPERT, d))],
        out_specs=pl.BlockSpec(memory_space=pl.ANY),
        scratch_shapes=[pltpu.VMEM((2, bm * PITCH, LANES), u32), pltpu.VMEM((2, bm * PITCH, LANES), u32),
                        pltpu.VMEM((d, D_EXPERT), bf16), pltpu.VMEM((d, D_EXPERT), bf16),
                        pltpu.VMEM((D_EXPERT, d), bf16),
                        pltpu.SemaphoreType.DMA((2,)), pltpu.SemaphoreType.DMA((2,))],
    )
    return pl.pallas_call(
        functools.partial(_moe_kernel, bm=bm, n_blk=n_blk),
        grid_spec=grid_spec,
        out_shape=jax.ShapeDtypeStruct((cap * PITCH, LANES), u32),
        compiler_params=_cparams(("arbitrary",)),
        name="expert_mlp",
    )(blk_start, blk_count, n_used, buf_tok, hs, w1, w3, w2)


def _combine_kernel(d0_ref, d1_ref, h_ref, route_ref, y_hbm, nw_ref, o_ref, ybuf, sem, *, tb):
    i = pl.program_id(0)
    n = pl.num_programs(0)
    slot = i % 2

    def start(blk, s):
        _slab_gather_start(d0_ref, blk * tb, tb, y_hbm, ybuf.at[s, 0], sem.at[s])
        _slab_gather_start(d1_ref, blk * tb, tb, y_hbm, ybuf.at[s, 1], sem.at[s])

    @pl.when(i == 0)
    def _():
        start(0, 0)

    @pl.when(i + 1 < n)
    def _():
        start(i + 1, 1 - slot)

    for k in range(TOP_K):
        _slab_gather_wait(tb, y_hbm, ybuf.at[slot, k], sem.at[slot])
    r = route_ref[...]
    h = h_ref[...] + (r[:, 2:3] * _slabs_to_rows(ybuf.at[slot, 0], tb)
                      + r[:, 3:4] * _slabs_to_rows(ybuf.at[slot, 1], tb))
    o_ref[...] = h * lax.rsqrt(jnp.mean(h * h, axis=-1, keepdims=True) + EPS) * nw_ref[...]


def _combine(dest0, dest1, h, route, ys, norm_w):
    t, d = h.shape
    tb = min(256, t)
    grid_spec = pltpu.PrefetchScalarGridSpec(
        num_scalar_prefetch=2,
        grid=(t // tb,),
        in_specs=[pl.BlockSpec((tb, d), lambda i, a, b: (i, 0)),
                  pl.BlockSpec((tb, LANES), lambda i, a, b: (i, 0)),
                  pl.BlockSpec(memory_space=pl.ANY),
                  pl.BlockSpec((1, d), lambda i, a, b: (0, 0))],
        out_specs=pl.BlockSpec((tb, d), lambda i, a, b: (i, 0)),
        scratch_shapes=[pltpu.VMEM((2, TOP_K, tb * PITCH, LANES), u32), pltpu.SemaphoreType.DMA((2,))],
    )
    return pl.pallas_call(
        functools.partial(_combine_kernel, tb=tb),
        grid_spec=grid_spec,
        out_shape=jax.ShapeDtypeStruct((t, d), f32),
        compiler_params=_cparams(("arbitrary",)),
        name="combine_final_norm",
    )(dest0, dest1, h, route, ys, norm_w)


def kernel(x, positions, norm_mix_w, w_in, fox_b_f, ret_gn_w, w_branch, b_gate, w_out, norm_ffn_w, w_router_group,
           b_router_group, w_router_expert, b_router_expert, w1, w3, w2, norm_final_w):
    b, s, d = x.shape
    assert d == D_MODEL and norm_mix_w.shape[0] == 1
    t = b * s
    x2 = x.reshape(t, d)
    scale = 1.0 / math.sqrt(HEAD_DIM)

    wt = w_in[0].T
    o_f = 3 * FOX_WIDTH
    o_r = o_f + FOX_HEADS
    o_g = o_r + 4 * RET_WIDTH
    w_main = jnp.concatenate([wt[o_g:], wt[:FOX_WIDTH] * scale, wt[FOX_WIDTH:o_f],
                              wt[o_r:o_r + RET_WIDTH] * scale, wt[o_r + RET_WIDTH:o_g]], axis=0).astype(bf16)
    w_f = jnp.pad(wt[o_f:o_r], ((0, LANES - FOX_HEADS), (0, 0))).astype(bf16)
    bias_f = jnp.pad(fox_b_f[0], (0, LANES - FOX_HEADS)).reshape(1, LANES)
    w_r = jnp.concatenate([w_router_group[0], w_router_expert[0]], axis=1)
    w_r = jnp.pad(w_r, ((0, 0), (0, LANES - w_r.shape[1])))
    wr_hi = w_r.astype(bf16)
    wr_lo = (w_r - wr_hi.astype(f32)).astype(bf16)
    b_r = jnp.pad(jnp.concatenate([b_router_group[0], b_router_expert[0]]), (0, LANES - N_GROUPS - N_EXPERTS))
    b_r = b_r.reshape(1, LANES)

    cos, sin = _rope_tables(positions)
    proj, f_logit = _in_projection(x2, norm_mix_w, w_main, w_f)
    c = _forget_cumsum(f_logit, bias_f, b, s)
    fox_o = _fox_attention(proj, c.reshape(b * FOX_HEADS, 1, s), b, s)
    ret_o = _retention(proj, cos, sin, ret_gn_w, b, s)
    h, h_slabs, route = _merge(fox_o, ret_o, proj, b_gate[0], x2, w_branch[0].astype(bf16), w_out[0].astype(bf16),
                      norm_ffn_w, wr_hi, wr_lo, b_r)

    bm = MOE_BLOCK
    cap = t * TOP_K + N_EXPERTS * bm
    dest_rows, seg = _dispatch_plan(route, bm)
    dest0 = dest_rows[:, 0, :].reshape(t)
    dest1 = dest_rows[:, 1, :].reshape(t)
    pends = seg[0, :N_EXPERTS].astype(jnp.int32)
    padded = seg[1, :N_EXPERTS].astype(jnp.int32)
    buf_tok = _slot_tokens(dest0, dest1, cap)
    n_used = pends[-1:] // bm

    y = _expert_mlp((pends - padded) // bm, padded // bm, n_used, buf_tok, h_slabs, w1[0], w3[0], w2[0], cap)
    out = _combine(dest0, dest1, h, route, y, norm_final_w.reshape(1, d))
    return out.reshape(b, s, d)
```

```python
import functools
import math

import numpy as np
import jax
import jax.numpy as jnp
from jax import lax
from jax.experimental import pallas as pl
from jax.experimental.pallas import tpu as pltpu

D_MODEL = 2048
HEAD_DIM = 128
FOX_HEADS = 8
RET_HEADS = 8
FOX_WIDTH = FOX_HEADS * HEAD_DIM
RET_WIDTH = RET_HEADS * HEAD_DIM
N_BRANCH = 2
N_GROUPS = 4
EXPERTS_PER_GROUP = 8
N_EXPERTS = N_GROUPS * EXPERTS_PER_GROUP
TOP_K = 2
D_EXPERT = 512
ROPE_BASE = 10000.0
EPS = 1e-6

LANES = 128
NEG_BIG = -1e30
MAIN_WIDTH = 3 * FOX_WIDTH + 4 * RET_WIDTH + N_BRANCH * D_MODEL
COL_GATE = 0
COL_FQ = N_BRANCH * D_MODEL // HEAD_DIM
COL_FK = COL_FQ + FOX_HEADS
COL_FV = COL_FK + FOX_HEADS
COL_RQ = COL_FV + FOX_HEADS
COL_RK = COL_RQ + RET_HEADS
COL_RV = COL_RK + RET_HEADS
COL_RG = COL_RV + RET_HEADS

ATTN_BLOCK = 256
RET_CHUNK = 256
MOE_BLOCK = 256
VMEM_LIMIT = 56 * 1024 * 1024

f32 = jnp.float32
bf16 = jnp.bfloat16
_NT = (((1,), (1,)), ((), ()))
SLAB = D_MODEL // (2 * LANES)
PITCH = SLAB + 1
u32 = jnp.uint32
_HI_MASK = np.uint32(0xFFFF0000)


def _rows_to_slabs(ref2, val, n_rows):
    bits = lambda v: lax.bitcast_convert_type(v.astype(bf16).astype(f32), u32)
    for j in range(SLAB):
        lo = val[:, (2 * j) * LANES:(2 * j + 1) * LANES]
        hi = val[:, (2 * j + 1) * LANES:(2 * j + 2) * LANES]
        ref2[pl.ds(j, n_rows, stride=PITCH), :] = (bits(lo) >> 16) | (bits(hi) & _HI_MASK)
    ref2[pl.ds(SLAB, n_rows, stride=PITCH), :] = jnp.zeros((n_rows, LANES), u32)


def _slabs_to_rows(ref2, n_rows):
    chunks = []
    for j in range(SLAB):
        w = ref2[pl.ds(j, n_rows, stride=PITCH), :]
        chunks.append(lax.bitcast_convert_type(w << 16, f32))
        chunks.append(lax.bitcast_convert_type(w & _HI_MASK, f32))
    return jnp.concatenate(chunks, axis=-1)


def _slab_gather_start(idx_ref, base, n_rows, src_hbm, dst, sem, unroll=8):
    def body(g, _):
        for u in range(unroll):
            r = g * unroll + u
            t = idx_ref[base + r]
            pltpu.make_async_copy(src_hbm.at[pl.ds(t * PITCH, SLAB), :], dst.at[pl.ds(r * PITCH, SLAB), :],
                                  sem).start(priority=1)
        return 0
    lax.fori_loop(0, n_rows // unroll, body, 0)


def _slab_gather_wait(n_rows, src_hbm, dst, sem):
    pltpu.make_async_copy(src_hbm.at[pl.ds(0, n_rows * SLAB), :], dst.at[pl.ds(0, n_rows * SLAB), :], sem).wait()


def _cparams(sem, vmem=VMEM_LIMIT):
    return pltpu.CompilerParams(dimension_semantics=sem, vmem_limit_bytes=vmem)


def _rope_kernel(pos_ref, freq_ref, sgn_ref, cos_ref, sin_ref):
    ang = pos_ref[0].astype(f32) * freq_ref[...]
    cos_ref[0] = jnp.cos(ang)
    sin_ref[0] = jnp.sin(ang) * sgn_ref[...]


def _rope_tables(positions):
    b, s = positions.shape
    half = HEAD_DIM // 2
    inv = (np.float32(ROPE_BASE) ** (-np.arange(half, dtype=np.float32) / np.float32(half))).astype(np.float32)
    freq = jnp.asarray(np.concatenate([inv, inv])[None, :])
    sgn = jnp.asarray(np.concatenate([-np.ones(half, np.float32), np.ones(half, np.float32)])[None, :])
    out = jax.ShapeDtypeStruct((b, s, HEAD_DIM), f32)
    return pl.pallas_call(
        _rope_kernel,
        grid=(b,),
        in_specs=[pl.BlockSpec((1, s, 1), lambda i: (i, 0, 0)),
                  pl.BlockSpec((1, HEAD_DIM), lambda i: (0, 0)),
                  pl.BlockSpec((1, HEAD_DIM), lambda i: (0, 0))],
        out_specs=[pl.BlockSpec((1, s, HEAD_DIM), lambda i: (i, 0, 0)),
                   pl.BlockSpec((1, s, HEAD_DIM), lambda i: (i, 0, 0))],
        out_shape=[out, out],
        compiler_params=_cparams(("parallel",)),
        name="rope_tables",
    )(positions.reshape(b, s, 1), freq, sgn)


def _inproj_kernel(x_ref, nw_ref, w_ref, wf_ref, o_ref, f_ref, xn_ref):
    @pl.when(pl.program_id(1) == 0)
    def _():
        x = x_ref[...]
        ms = jnp.mean(x * x, axis=-1, keepdims=True)
        xn = (x * lax.rsqrt(ms + EPS) * nw_ref[...]).astype(bf16)
        xn_ref[...] = xn
        f_ref[...] = lax.dot_general(xn, wf_ref[...], _NT, preferred_element_type=f32)

    o_ref[...] = lax.dot_general(xn_ref[...], w_ref[...], _NT, preferred_element_type=f32).astype(o_ref.dtype)


def _in_projection(x2, norm_w, w_main_t, w_f_t):
    t, d = x2.shape
    n = w_main_t.shape[0]
    tm = min(1024, t)
    tn = 1024
    return pl.pallas_call(
        _inproj_kernel,
        grid=(t // tm, n // tn),
        in_specs=[pl.BlockSpec((tm, d), lambda i, j: (i, 0)),
                  pl.BlockSpec((1, d), lambda i, j: (0, 0)),
                  pl.BlockSpec((tn, d), lambda i, j: (j, 0)),
                  pl.BlockSpec((LANES, d), lambda i, j: (0, 0))],
        out_specs=[pl.BlockSpec((tm, tn), lambda i, j: (i, j)),
                   pl.BlockSpec((tm, LANES), lambda i, j: (i, 0))],
        out_shape=[jax.ShapeDtypeStruct((t, n), bf16), jax.ShapeDtypeStruct((t, LANES), f32)],
        scratch_shapes=[pltpu.VMEM((tm, d), bf16)],
        compiler_params=_cparams(("parallel", "arbitrary")),
        name="in_projection",
    )(x2, norm_w, w_main_t, w_f_t)


def _fcum_kernel(f_ref, b_ref, c_ref):
    z = f_ref[...] + b_ref[...]
    ls = jnp.minimum(z, 0.0) - jnp.log1p(jnp.exp(-jnp.abs(z)))
    c = ls.T[:FOX_HEADS, :]
    s = c.shape[1]
    lane = lax.broadcasted_iota(jnp.int32, c.shape, 1)
    sh = 1
    while sh < s:
        c = c + jnp.where(lane >= sh, pltpu.roll(c, sh, axis=1), 0.0)
        sh *= 2
    c_ref[0] = c


def _forget_cumsum(f_logit, bias, b, s):
    return pl.pallas_call(
        _fcum_kernel,
        grid=(b,),
        in_specs=[pl.BlockSpec((s, LANES), lambda i: (i, 0)),
                  pl.BlockSpec((1, LANES), lambda i: (0, 0))],
        out_specs=pl.BlockSpec((1, FOX_HEADS, s), lambda i: (i, 0, 0)),
        out_shape=jax.ShapeDtypeStruct((b, FOX_HEADS, s), f32),
        compiler_params=_cparams(("parallel",)),
        name="forget_cumsum",
    )(f_logit, bias)


def _fox_kernel(q_ref, k_ref, v_ref, c_ref, o_ref, *, tq):
    s_len = q_ref.shape[0]
    nq = s_len // tq
    nt = (((1,), (1,)), ((), ()))
    row = lax.broadcasted_iota(jnp.int32, (tq, tq), 0)
    col = lax.broadcasted_iota(jnp.int32, (tq, tq), 1)
    causal = row >= col
    for qi in range(nq):
        r0 = qi * tq
        q = q_ref[r0:r0 + tq, :]
        sd = lax.dot_general(q, k_ref[r0:r0 + tq, :], nt, preferred_element_type=f32) - c_ref[0, :, r0:r0 + tq]
        sd = jnp.where(causal, sd, NEG_BIG)
        m = jnp.max(sd, axis=-1, keepdims=True)
        if qi:
            so = lax.dot_general(q, k_ref[0:r0, :], nt, preferred_element_type=f32) - c_ref[0, :, 0:r0]
            m = jnp.maximum(m, jnp.max(so, axis=-1, keepdims=True))
        pd = jnp.exp(sd - m)
        l = jnp.sum(pd, axis=-1, keepdims=True)
        acc = jnp.dot(pd.astype(bf16), v_ref[r0:r0 + tq, :], preferred_element_type=f32)
        if qi:
            po = jnp.exp(so - m)
            l = l + jnp.sum(po, axis=-1, keepdims=True)
            acc = acc + jnp.dot(po.astype(bf16), v_ref[0:r0, :], preferred_element_type=f32)
        o_ref[r0:r0 + tq, :] = (acc / l).astype(o_ref.dtype)


def _fox_attention(proj, c_blk, b, s):
    tq = min(ATTN_BLOCK, s)
    h = FOX_HEADS
    return pl.pallas_call(
        functools.partial(_fox_kernel, tq=tq),
        grid=(b, h),
        in_specs=[pl.BlockSpec((s, HEAD_DIM), lambda i, j: (i, COL_FQ + j)),
                  pl.BlockSpec((s, HEAD_DIM), lambda i, j: (i, COL_FK + j)),
                  pl.BlockSpec((s, HEAD_DIM), lambda i, j: (i, COL_FV + j)),
                  pl.BlockSpec((1, 1, s), lambda i, j: (i * FOX_HEADS + j, 0, 0))],
        out_specs=pl.BlockSpec((s, HEAD_DIM), lambda i, j: (i, j)),
        out_shape=jax.ShapeDtypeStruct((b * s, FOX_WIDTH), bf16),
        compiler_params=_cparams(("parallel", "parallel")),
        name="fox_attention",
    )(proj, proj, proj, c_blk)


def _ret_kernel(q_ref, k_ref, v_ref, g_ref, cos_ref, sin_ref, dec_ref, xi_ref, zeta_ref, gc_ref, gnw_ref,
                o_ref, *, chunk):
    s_len = q_ref.shape[0]
    nc = s_len // chunk
    state = jnp.zeros((HEAD_DIM, HEAD_DIM), f32)
    for ci in range(nc):
        rows = slice(ci * chunk, (ci + 1) * chunk)
        cos = cos_ref[0, rows, :]
        sin = sin_ref[0, rows, :]

        def rot(t):
            t = t.astype(f32)
            return t * cos + pltpu.roll(t, HEAD_DIM // 2, axis=1) * sin

        q = rot(q_ref[rows, :])
        k = rot(k_ref[rows, :])
        v = v_ref[rows, :]
        qb = q.astype(bf16)
        inner = lax.dot_general(qb, k.astype(bf16), _NT, preferred_element_type=f32) * dec_ref[0]
        o = jnp.dot(inner.astype(bf16), v, preferred_element_type=f32)
        if ci:
            o = o + jnp.dot(qb, state.astype(bf16), preferred_element_type=f32) * xi_ref[0]
        if ci + 1 < nc:
            kz = (k * zeta_ref[0]).astype(bf16)
            kv = lax.dot_general(kz, v, (((0,), (0,)), ((), ())), preferred_element_type=f32)
            state = state * gc_ref[0] + kv if ci else kv
        mu = jnp.mean(o, axis=-1, keepdims=True)
        oc = o - mu
        var = jnp.mean(oc * oc, axis=-1, keepdims=True)
        y = oc * lax.rsqrt(var + EPS) * gnw_ref[...]
        g = g_ref[rows, :].astype(f32)
        o_ref[rows, :] = (g * jax.nn.sigmoid(g) * y).astype(o_ref.dtype)


def _retention_constants(chunk):
    hh = np.arange(RET_HEADS, dtype=np.float64)
    log_gamma = np.log1p(-(2.0 ** (-5.0 - hh)))
    n = np.arange(chunk, dtype=np.float64)
    diff = n[:, None] - n[None, :]
    decay = np.where(diff[None] >= 0, np.exp(diff[None] * log_gamma[:, None, None]), 0.0)
    xi = np.exp((n[None, :] + 1.0) * log_gamma[:, None])
    zeta = np.exp((chunk - 1.0 - n[None, :]) * log_gamma[:, None])
    g_chunk = np.exp(chunk * log_gamma)
    bc = lambda a: np.broadcast_to(a[:, :, None], (RET_HEADS, chunk, HEAD_DIM))
    gc = np.broadcast_to(g_chunk[:, None, None], (RET_HEADS, 1, HEAD_DIM))
    to = lambda a: jnp.asarray(np.ascontiguousarray(a), dtype=f32)
    return to(decay), to(bc(xi)), to(bc(zeta)), to(gc)


def _retention(proj, cos, sin, gn_w, b, s):
    chunk = min(RET_CHUNK, s)
    decay, xi, zeta, gc = _retention_constants(chunk)
    h = RET_HEADS
    head = lambda col: pl.BlockSpec((s, HEAD_DIM), lambda i, j: (i, col + j))
    per_head = lambda shape: pl.BlockSpec(shape, lambda i, j: (j, 0, 0))
    return pl.pallas_call(
        functools.partial(_ret_kernel, chunk=chunk),
        grid=(b, h),
        in_specs=[head(COL_RQ), head(COL_RK), head(COL_RV), head(COL_RG),
                  pl.BlockSpec((1, s, HEAD_DIM), lambda i, j: (i, 0, 0)),
                  pl.BlockSpec((1, s, HEAD_DIM), lambda i, j: (i, 0, 0)),
                  per_head((1, chunk, chunk)), per_head((1, chunk, HEAD_DIM)), per_head((1, chunk, HEAD_DIM)),
                  per_head((1, 1, HEAD_DIM)),
                  pl.BlockSpec((1, HEAD_DIM), lambda i, j: (0, j))],
        out_specs=pl.BlockSpec((s, HEAD_DIM), lambda i, j: (i, j)),
        out_shape=jax.ShapeDtypeStruct((b * s, RET_WIDTH), bf16),
        compiler_params=_cparams(("parallel", "parallel")),
        name="retention",
    )(proj, proj, proj, proj, cos, sin, decay, xi, zeta, gc, gn_w)


def _route(logits):
    lane = lax.broadcasted_iota(jnp.int32, logits.shape, 1).astype(f32)
    far = jnp.float32(1e9)
    gmask = lane < N_GROUPS
    gl = jnp.where(gmask, logits, NEG_BIG)
    gmax = jnp.max(gl, axis=-1, keepdims=True)
    gsel = jnp.min(jnp.where(gl == gmax, lane, far), axis=-1, keepdims=True)
    p_group = 1.0 / jnp.sum(jnp.where(gmask, jnp.exp(gl - gmax), 0.0), axis=-1, keepdims=True)
    lo = N_GROUPS + EXPERTS_PER_GROUP * gsel
    emask = (lane >= lo) & (lane < lo + EXPERTS_PER_GROUP)
    el = jnp.where(emask, logits, NEG_BIG)
    m1 = jnp.max(el, axis=-1, keepdims=True)
    i1 = jnp.min(jnp.where(el == m1, lane, far), axis=-1, keepdims=True)
    el2 = jnp.where(lane == i1, NEG_BIG, el)
    m2 = jnp.max(el2, axis=-1, keepdims=True)
    i2 = jnp.min(jnp.where(el2 == m2, lane, far), axis=-1, keepdims=True)
    t = jnp.exp(m2 - m1)
    w1 = p_group / (1.0 + t)
    w2 = p_group * t / (1.0 + t)
    out = jnp.where(lane == 0, i1 - N_GROUPS, 0.0)
    out = jnp.where(lane == 1, i2 - N_GROUPS, out)
    out = jnp.where(lane == 2, w1, out)
    out = jnp.where(lane == 3, w2, out)
    return out


def _merge_kernel(fo_ref, ro_ref, g0_ref, g1_ref, bg_ref, x_ref, wb_ref, wo_ref, nfw_ref, wrh_ref, wrl_ref, br_ref,
                  h_ref, hs_ref, route_ref, mg_ref, *, ncol):
    d = x_ref.shape[1]
    cw = d // ncol
    fo = fo_ref[...]
    ro = ro_ref[...]
    for c in range(ncol):
        cs = slice(c * cw, (c + 1) * cw)
        bd0 = jnp.dot(fo, wb_ref[0, :, cs], preferred_element_type=f32)
        bd1 = jnp.dot(ro, wb_ref[1, :, cs], preferred_element_type=f32)
        ga = jax.nn.sigmoid(g0_ref[:, cs].astype(f32) + bg_ref[0:1, cs])
        gb = jax.nn.sigmoid(g1_ref[:, cs].astype(f32) + bg_ref[1:2, cs])
        mg_ref[:, cs] = (ga * bd0 + gb * bd1).astype(bf16)
    h = x_ref[...] + jnp.dot(mg_ref[...], wo_ref[...], preferred_element_type=f32)
    h_ref[...] = h
    hn = h * lax.rsqrt(jnp.mean(h * h, axis=-1, keepdims=True) + EPS) * nfw_ref[...]
    _rows_to_slabs(hs_ref, hn, hn.shape[0])
    hi = hn.astype(bf16)
    lo = (hn - hi.astype(f32)).astype(bf16)
    logits = (jnp.dot(hi, wrh_ref[...], preferred_element_type=f32)
              + jnp.dot(lo, wrh_ref[...], preferred_element_type=f32)
              + jnp.dot(hi, wrl_ref[...], preferred_element_type=f32)) + br_ref[...]
    route_ref[...] = _route(logits)


def _merge(fox_o, ret_o, proj, b_gate, x2, w_branch, w_out, nfw, wr_hi, wr_lo, b_router):
    t, d = x2.shape
    tm = min(256, t)
    gate0 = COL_GATE * HEAD_DIM // d
    const = lambda shape: pl.BlockSpec(shape, lambda i: tuple(0 for _ in shape), pipeline_mode=pl.Buffered(1))
    return pl.pallas_call(
        functools.partial(_merge_kernel, ncol=4),
        grid=(t // tm,),
        in_specs=[pl.BlockSpec((tm, FOX_WIDTH), lambda i: (i, 0)),
                  pl.BlockSpec((tm, RET_WIDTH), lambda i: (i, 0)),
                  pl.BlockSpec((tm, d), lambda i: (i, gate0)),
                  pl.BlockSpec((tm, d), lambda i: (i, gate0 + 1)),
                  const((N_BRANCH, d)),
                  pl.BlockSpec((tm, d), lambda i: (i, 0)),
                  const((N_BRANCH, FOX_WIDTH, d)),
                  const((d, d)),
                  const((1, d)),
                  const((d, LANES)), const((d, LANES)), const((1, LANES))],
        out_specs=[pl.BlockSpec((tm, d), lambda i: (i, 0)),
                   pl.BlockSpec((tm * PITCH, LANES), lambda i: (i, 0)),
                   pl.BlockSpec((tm, LANES), lambda i: (i, 0))],
        out_shape=[jax.ShapeDtypeStruct((t, d), f32), jax.ShapeDtypeStruct((t * PITCH, LANES), u32),
                   jax.ShapeDtypeStruct((t, LANES), f32)],
        scratch_shapes=[pltpu.VMEM((tm, d), bf16)],
        compiler_params=_cparams(("parallel",)),
        name="merge_outproj_router",
    )(fox_o, ret_o, proj, proj, b_gate, x2, w_branch, w_out, nfw, wr_hi, wr_lo, b_router)


def _plan_kernel(route_ref, dest_ref, ends_ref, carry_ref, *, bm):
    phase = pl.program_id(0)
    i = pl.program_id(1)
    r = route_ref[...]
    tb = r.shape[0]
    lane = lax.broadcasted_iota(jnp.int32, r.shape, 1).astype(f32)
    o0 = (lane == r[:, 0:1]).astype(f32)
    o1 = (lane == r[:, 1:2]).astype(f32)
    o = o0 + o1

    @pl.when((phase == 0) & (i == 0))
    def _():
        carry_ref[...] = jnp.zeros_like(carry_ref)

    @pl.when((phase == 1) & (i == 0))
    def _():
        cnt = carry_ref[...]
        padded = jnp.floor((cnt + (bm - 1)) * (1.0 / bm)) * bm
        lane8 = lax.broadcasted_iota(jnp.int32, cnt.shape, 1)
        ends = padded
        sh = 1
        while sh < LANES:
            ends = ends + jnp.where(lane8 >= sh, pltpu.roll(ends, sh, axis=1), 0.0)
            sh *= 2
        sub = lax.broadcasted_iota(jnp.int32, cnt.shape, 0)
        ends_ref[...] = jnp.where(sub == 0, ends, padded)
        carry_ref[...] = ends - padded

    @pl.when(phase == 1)
    def _():
        row = lax.broadcasted_iota(jnp.int32, (tb, tb), 0)
        col = lax.broadcasted_iota(jnp.int32, (tb, tb), 1)
        lower = jnp.where(col < row, 1.0, 0.0).astype(bf16)
        before = jnp.dot(lower, o.astype(bf16), preferred_element_type=f32) + carry_ref[0:1, :]
        d0 = jnp.sum(before * o0, axis=-1, keepdims=True)
        d1 = jnp.sum(before * o1, axis=-1, keepdims=True)
        dd = jnp.where(lane == 0, d0, jnp.where(lane == 1, d1, 0.0))
        dest_ref[0] = dd.T[:8, :].astype(jnp.int32)

    carry_ref[...] += jnp.sum(o, axis=0, keepdims=True)


def _dispatch_plan(route, bm):
    t = route.shape[0]
    tb = min(512, t)
    nb = t // tb
    return pl.pallas_call(
        functools.partial(_plan_kernel, bm=bm),
        grid=(2, nb),
        in_specs=[pl.BlockSpec((tb, LANES), lambda p, i: (i, 0))],
        out_specs=[pl.BlockSpec((1, 8, tb), lambda p, i: (i * p, 0, 0)),
                   pl.BlockSpec((8, LANES), lambda p, i: (0, 0))],
        out_shape=[jax.ShapeDtypeStruct((nb, 8, tb), jnp.int32), jax.ShapeDtypeStruct((8, LANES), f32)],
        scratch_shapes=[pltpu.VMEM((8, LANES), f32)],
        compiler_params=_cparams(("arbitrary", "arbitrary")),
        name="dispatch_plan",
    )(route)


def _slot_kernel(d0_ref, d1_ref, zeros_hbm, tok_ref, sem, *, t, unroll=8):
    init = pltpu.make_async_copy(zeros_hbm, tok_ref, sem)
    init.start()
    init.wait()

    def fill(i, c):
        for u in range(unroll):
            tt = i * unroll + u
            tok_ref[d0_ref[tt]] = tt
            tok_ref[d1_ref[tt]] = tt
        return c

    lax.fori_loop(0, t // unroll, fill, 0)


def _slot_tokens(dest0, dest1, cap):
    t = dest0.shape[0]
    smem = pl.BlockSpec(memory_space=pltpu.SMEM)
    return pl.pallas_call(
        functools.partial(_slot_kernel, t=t),
        in_specs=[smem, smem, pl.BlockSpec(memory_space=pl.ANY)],
        out_specs=smem,
        out_shape=jax.ShapeDtypeStruct((cap,), jnp.int32),
        scratch_shapes=[pltpu.SemaphoreType.DMA(())],
        name="slot_tokens",
    )(dest0, dest1, jnp.zeros((cap,), jnp.int32))


def _moe_kernel(g0_ref, nb_ref, nu_ref, tok_ref, h_hbm, w1_ref, w3_ref, w2_ref, y_hbm,
                xbuf, ybuf, w1b, w3b, w2b, gsem, ysem, *, bm, n_blk):
    e = pl.program_id(0)
    n_used = nu_ref[0]
    g0 = g0_ref[e]
    nb = nb_ref[e]
    blk_rows = bm * PITCH

    def y_copy(slot, g):
        dst = y_hbm.at[pl.ds(pl.multiple_of(g * blk_rows, 8), blk_rows), :]
        return pltpu.make_async_copy(ybuf.at[slot], dst, ysem.at[slot])

    @pl.when(e == 0)
    def _():
        _slab_gather_start(tok_ref, 0, bm, h_hbm, xbuf.at[0], gsem.at[0])

    @pl.when(nb > 0)
    def _():
        w1b[...] = w1_ref[0].astype(bf16)
        w3b[...] = w3_ref[0].astype(bf16)
        w2b[...] = w2_ref[0].astype(bf16)

    def block(c, carry):
        g = g0 + c
        slot = g % 2

        @pl.when(g + 1 < n_used)
        def _():
            _slab_gather_start(tok_ref, (g + 1) * bm, bm, h_hbm, xbuf.at[1 - slot], gsem.at[1 - slot])

        _slab_gather_wait(bm, h_hbm, xbuf.at[slot], gsem.at[slot])
        hn = _slabs_to_rows(xbuf.at[slot], bm).astype(bf16)
        a = jnp.dot(hn, w1b[...], preferred_element_type=f32)
        b = jnp.dot(hn, w3b[...], preferred_element_type=f32)
        mid = (a * jax.nn.sigmoid(a) * b).astype(bf16)
        y = jnp.dot(mid, w2b[...], preferred_element_type=f32)

        @pl.when(g >= 2)
        def _():
            y_copy(slot, g - 2).wait()

        _rows_to_slabs(ybuf.at[slot], y, bm)
        y_copy(slot, g).start()
        return carry

    lax.fori_loop(0, nb, block, 0)

    @pl.when(e == pl.num_programs(0) - 1)
    def _():
        @pl.when(n_used >= 2)
        def _():
            y_copy(n_used % 2, n_used - 2).wait()

        y_copy((n_used - 1) % 2, n_used - 1).wait()
        ybuf[0] = jnp.zeros(ybuf.shape[1:], u32)

        def zstart(g, carry):
            y_copy(0, g).start()
            return carry

        def zwait(g, carry):
            y_copy(0, g).wait()
            return carry

        lax.fori_loop(n_used, n_blk, zstart, 0)
        lax.fori_loop(n_used, n_blk, zwait, 0)


def _expert_mlp(blk_start, blk_count, n_used, buf_tok, hs, w1, w3, w2, cap):
    d = D_MODEL
    bm = MOE_BLOCK
    n_blk = cap // bm
    by_expert = lambda shape: pl.BlockSpec(shape, lambda e, g0, nb, nu, tok: (e, 0, 0))
    grid_spec = pltpu.PrefetchScalarGridSpec(
        num_scalar_prefetch=4,
        grid=(N_EXPERTS,),
        in_specs=[pl.BlockSpec(memory_space=pl.ANY),
                  by_expert((1, d, D_EXPERT)), by_expert((1, d, D_EXPERT)), by_expert((1, D_EXPERT, d))],
        out_specs=pl.BlockSpec(memory_space=pl.ANY),
        scratch_shapes=[pltpu.VMEM((2, bm * PITCH, LANES), u32), pltpu.VMEM((2, bm * PITCH, LANES), u32),
                        pltpu.VMEM((d, D_EXPERT), bf16), pltpu.VMEM((d, D_EXPERT), bf16),
                        pltpu.VMEM((D_EXPERT, d), bf16),
                        pltpu.SemaphoreType.DMA((2,)), pltpu.SemaphoreType.DMA((2,))],
    )
    return pl.pallas_call(
        functools.partial(_moe_kernel, bm=bm, n_blk=n_blk),
        grid_spec=grid_spec,
        out_shape=jax.ShapeDtypeStruct((cap * PITCH, LANES), u32),
        compiler_params=_cparams(("arbitrary",)),
        name="expert_mlp",
    )(blk_start, blk_count, n_used, buf_tok, hs, w1, w3, w2)


def _combine_kernel(d0_ref, d1_ref, h_ref, route_ref, y_hbm, nw_ref, o_ref, ybuf, sem, *, tb):
    i = pl.program_id(0)
    n = pl.num_programs(0)
    slot = i % 2

    def start(blk, s):
        _slab_gather_start(d0_ref, blk * tb, tb, y_hbm, ybuf.at[s, 0], sem.at[s])
        _slab_gather_start(d1_ref, blk * tb, tb, y_hbm, ybuf.at[s, 1], sem.at[s])

    @pl.when(i == 0)
    def _():
        start(0, 0)

    @pl.when(i + 1 < n)
    def _():
        start(i + 1, 1 - slot)

    for k in range(TOP_K):
        _slab_gather_wait(tb, y_hbm, ybuf.at[slot, k], sem.at[slot])
    r = route_ref[...]
    h = h_ref[...] + (r[:, 2:3] * _slabs_to_rows(ybuf.at[slot, 0], tb)
                      + r[:, 3:4] * _slabs_to_rows(ybuf.at[slot, 1], tb))
    o_ref[...] = h * lax.rsqrt(jnp.mean(h * h, axis=-1, keepdims=True) + EPS) * nw_ref[...]


def _combine(dest0, dest1, h, route, ys, norm_w):
    t, d = h.shape
    tb = min(256, t)
    grid_spec = pltpu.PrefetchScalarGridSpec(
        num_scalar_prefetch=2,
        grid=(t // tb,),
        in_specs=[pl.BlockSpec((tb, d), lambda i, a, b: (i, 0)),
                  pl.BlockSpec((tb, LANES), lambda i, a, b: (i, 0)),
                  pl.BlockSpec(memory_space=pl.ANY),
                  pl.BlockSpec((1, d), lambda i, a, b: (0, 0))],
        out_specs=pl.BlockSpec((tb, d), lambda i, a, b: (i, 0)),
        scratch_shapes=[pltpu.VMEM((2, TOP_K, tb * PITCH, LANES), u32), pltpu.SemaphoreType.DMA((2,))],
    )
    return pl.pallas_call(
        functools.partial(_combine_kernel, tb=tb),
        grid_spec=grid_spec,
        out_shape=jax.ShapeDtypeStruct((t, d), f32),
        compiler_params=_cparams(("arbitrary",)),
        name="combine_final_norm",
    )(dest0, dest1, h, route, ys, norm_w)


def kernel(x, positions, norm_mix_w, w_in, fox_b_f, ret_gn_w, w_branch, b_gate, w_out, norm_ffn_w, w_router_group,
           b_router_group, w_router_expert, b_router_expert, w1, w3, w2, norm_final_w):
    b, s, d = x.shape
    assert d == D_MODEL and norm_mix_w.shape[0] == 1
    t = b * s
    x2 = x.reshape(t, d)
    scale = 1.0 / math.sqrt(HEAD_DIM)

    wt = w_in[0].T
    o_f = 3 * FOX_WIDTH
    o_r = o_f + FOX_HEADS
    o_g = o_r + 4 * RET_WIDTH
    w_main = jnp.concatenate([wt[o_g:], wt[:FOX_WIDTH] * scale, wt[FOX_WIDTH:o_f],
                              wt[o_r:o_r + RET_WIDTH] * scale, wt[o_r + RET_WIDTH:o_g]], axis=0).astype(bf16)
    w_f = jnp.pad(wt[o_f:o_r], ((0, LANES - FOX_HEADS), (0, 0))).astype(bf16)
    bias_f = jnp.pad(fox_b_f[0], (0, LANES - FOX_HEADS)).reshape(1, LANES)
    w_r = jnp.concatenate([w_router_group[0], w_router_expert[0]], axis=1)
    w_r = jnp.pad(w_r, ((0, 0), (0, LANES - w_r.shape[1])))
    wr_hi = w_r.astype(bf16)
    wr_lo = (w_r - wr_hi.astype(f32)).astype(bf16)
    b_r = jnp.pad(jnp.concatenate([b_router_group[0], b_router_expert[0]]), (0, LANES - N_GROUPS - N_EXPERTS))
    b_r = b_r.reshape(1, LANES)

    cos, sin = _rope_tables(positions)
    proj, f_logit = _in_projection(x2, norm_mix_w, w_main, w_f)
    c = _forget_cumsum(f_logit, bias_f, b, s)
    fox_o = _fox_attention(proj, c.reshape(b * FOX_HEADS, 1, s), b, s)
    ret_o = _retention(proj, cos, sin, ret_gn_w, b, s)
    h, h_slabs, route = _merge(fox_o, ret_o, proj, b_gate[0], x2, w_branch[0].astype(bf16), w_out[0].astype(bf16),
                      norm_ffn_w, wr_hi, wr_lo, b_r)

    bm = MOE_BLOCK
    cap = t * TOP_K + N_EXPERTS * bm
    dest_rows, seg = _dispatch_plan(route, bm)
    dest0 = dest_rows[:, 0, :].reshape(t)
    dest1 = dest_rows[:, 1, :].reshape(t)
    pends = seg[0, :N_EXPERTS].astype(jnp.int32)
    padded = seg[1, :N_EXPERTS].astype(jnp.int32)
    buf_tok = _slot_tokens(dest0, dest1, cap)
    n_used = pends[-1:] // bm

    y = _expert_mlp((pends - padded) // bm, padded // bm, n_used, buf_tok, h_slabs, w1[0], w3[0], w2[0], cap)
    out = _combine(dest0, dest1, h, route, y, norm_final_w.reshape(1, d))
    return out.reshape(b, s, d)
```

```python
import functools
import math

import numpy as np
import jax
import jax.numpy as jnp
from jax import lax
from jax.experimental import pallas as pl
from jax.experimental.pallas import tpu as pltpu

D_MODEL = 2048
HEAD_DIM = 128
FOX_HEADS = 8
RET_HEADS = 8
FOX_WIDTH = FOX_HEADS * HEAD_DIM
RET_WIDTH = RET_HEADS * HEAD_DIM
N_BRANCH = 2
N_GROUPS = 4
EXPERTS_PER_GROUP = 8
N_EXPERTS = N_GROUPS * EXPERTS_PER_GROUP
TOP_K = 2
D_EXPERT = 512
ROPE_BASE = 10000.0
EPS = 1e-6

LANES = 128
NEG_BIG = -1e30
MAIN_WIDTH = 3 * FOX_WIDTH + 4 * RET_WIDTH + N_BRANCH * D_MODEL
COL_GATE = 0
COL_FQ = N_BRANCH * D_MODEL // HEAD_DIM
COL_FK = COL_FQ + FOX_HEADS
COL_FV = COL_FK + FOX_HEADS
COL_RQ = COL_FV + FOX_HEADS
COL_RK = COL_RQ + RET_HEADS
COL_RV = COL_RK + RET_HEADS
COL_RG = COL_RV + RET_HEADS

ATTN_BLOCK = 256
RET_CHUNK = 256
MOE_BLOCK = 256
VMEM_LIMIT = 56 * 1024 * 1024

f32 = jnp.float32
bf16 = jnp.bfloat16
_NT = (((1,), (1,)), ((), ()))
SLAB = D_MODEL // (2 * LANES)
PITCH = SLAB + 1
u32 = jnp.uint32
_HI_MASK = np.uint32(0xFFFF0000)


def _rows_to_slabs(ref2, val, n_rows):
    bits = lambda v: lax.bitcast_convert_type(v.astype(bf16).astype(f32), u32)
    for j in range(SLAB):
        lo = val[:, (2 * j) * LANES:(2 * j + 1) * LANES]
        hi = val[:, (2 * j + 1) * LANES:(2 * j + 2) * LANES]
        ref2[pl.ds(j, n_rows, stride=PITCH), :] = (bits(lo) >> 16) | (bits(hi) & _HI_MASK)
    ref2[pl.ds(SLAB, n_rows, stride=PITCH), :] = jnp.zeros((n_rows, LANES), u32)


def _slabs_to_rows(ref2, n_rows):
    chunks = []
    for j in range(SLAB):
        w = ref2[pl.ds(j, n_rows, stride=PITCH), :]
        chunks.append(lax.bitcast_convert_type(w << 16, f32))
        chunks.append(lax.bitcast_convert_type(w & _HI_MASK, f32))
    return jnp.concatenate(chunks, axis=-1)


def _slab_gather_start(idx_ref, base, n_rows, src_hbm, dst, sem, unroll=8):
    def body(g, _):
        for u in range(unroll):
            r = g * unroll + u
            t = idx_ref[base + r]
            pltpu.make_async_copy(src_hbm.at[pl.ds(t * PITCH, SLAB), :], dst.at[pl.ds(r * PITCH, SLAB), :],
                                  sem).start(priority=u % 2)
        return 0
    lax.fori_loop(0, n_rows // unroll, body, 0)


def _slab_gather_wait(n_rows, src_hbm, dst, sem):
    pltpu.make_async_copy(src_hbm.at[pl.ds(0, n_rows * SLAB), :], dst.at[pl.ds(0, n_rows * SLAB), :], sem).wait()


def _cparams(sem, vmem=VMEM_LIMIT):
    return pltpu.CompilerParams(dimension_semantics=sem, vmem_limit_bytes=vmem)


def _rope_kernel(pos_ref, freq_ref, sgn_ref, cos_ref, sin_ref):
    ang = pos_ref[0].astype(f32) * freq_ref[...]
    cos_ref[0] = jnp.cos(ang)
    sin_ref[0] = jnp.sin(ang) * sgn_ref[...]


def _rope_tables(positions):
    b, s = positions.shape
    half = HEAD_DIM // 2
    inv = (np.float32(ROPE_BASE) ** (-np.arange(half, dtype=np.float32) / np.float32(half))).astype(np.float32)
    freq = jnp.asarray(np.concatenate([inv, inv])[None, :])
    sgn = jnp.asarray(np.concatenate([-np.ones(half, np.float32), np.ones(half, np.float32)])[None, :])
    out = jax.ShapeDtypeStruct((b, s, HEAD_DIM), f32)
    return pl.pallas_call(
        _rope_kernel,
        grid=(b,),
        in_specs=[pl.BlockSpec((1, s, 1), lambda i: (i, 0, 0)),
                  pl.BlockSpec((1, HEAD_DIM), lambda i: (0, 0)),
                  pl.BlockSpec((1, HEAD_DIM), lambda i: (0, 0))],
        out_specs=[pl.BlockSpec((1, s, HEAD_DIM), lambda i: (i, 0, 0)),
                   pl.BlockSpec((1, s, HEAD_DIM), lambda i: (i, 0, 0))],
        out_shape=[out, out],
        compiler_params=_cparams(("parallel",)),
        name="rope_tables",
    )(positions.reshape(b, s, 1), freq, sgn)


def _inproj_kernel(x_ref, nw_ref, w_ref, wf_ref, o_ref, f_ref, xn_ref):
    @pl.when(pl.program_id(1) == 0)
    def _():
        x = x_ref[...]
        ms = jnp.mean(x * x, axis=-1, keepdims=True)
        xn = (x * lax.rsqrt(ms + EPS) * nw_ref[...]).astype(bf16)
        xn_ref[...] = xn
        f_ref[...] = lax.dot_general(xn, wf_ref[...], _NT, preferred_element_type=f32)

    o_ref[...] = lax.dot_general(xn_ref[...], w_ref[...], _NT, preferred_element_type=f32).astype(o_ref.dtype)


def _in_projection(x2, norm_w, w_main_t, w_f_t):
    t, d = x2.shape
    n = w_main_t.shape[0]
    tm = min(1024, t)
    tn = 1024
    return pl.pallas_call(
        _inproj_kernel,
        grid=(t // tm, n // tn),
        in_specs=[pl.BlockSpec((tm, d), lambda i, j: (i, 0)),
                  pl.BlockSpec((1, d), lambda i, j: (0, 0)),
                  pl.BlockSpec((tn, d), lambda i, j: (j, 0)),
                  pl.BlockSpec((LANES, d), lambda i, j: (0, 0))],
        out_specs=[pl.BlockSpec((tm, tn), lambda i, j: (i, j)),
                   pl.BlockSpec((tm, LANES), lambda i, j: (i, 0))],
        out_shape=[jax.ShapeDtypeStruct((t, n), bf16), jax.ShapeDtypeStruct((t, LANES), f32)],
        scratch_shapes=[pltpu.VMEM((tm, d), bf16)],
        compiler_params=_cparams(("parallel", "arbitrary")),
        name="in_projection",
    )(x2, norm_w, w_main_t, w_f_t)


def _fcum_kernel(f_ref, b_ref, c_ref):
    z = f_ref[...] + b_ref[...]
    ls = jnp.minimum(z, 0.0) - jnp.log1p(jnp.exp(-jnp.abs(z)))
    c = ls.T[:FOX_HEADS, :]
    s = c.shape[1]
    lane = lax.broadcasted_iota(jnp.int32, c.shape, 1)
    sh = 1
    while sh < s:
        c = c + jnp.where(lane >= sh, pltpu.roll(c, sh, axis=1), 0.0)
        sh *= 2
    c_ref[0] = c


def _forget_cumsum(f_logit, bias, b, s):
    return pl.pallas_call(
        _fcum_kernel,
        grid=(b,),
        in_specs=[pl.BlockSpec((s, LANES), lambda i: (i, 0)),
                  pl.BlockSpec((1, LANES), lambda i: (0, 0))],
        out_specs=pl.BlockSpec((1, FOX_HEADS, s), lambda i: (i, 0, 0)),
        out_shape=jax.ShapeDtypeStruct((b, FOX_HEADS, s), f32),
        compiler_params=_cparams(("parallel",)),
        name="forget_cumsum",
    )(f_logit, bias)


def _fox_kernel(q_ref, k_ref, v_ref, c_ref, o_ref, *, tq):
    s_len = q_ref.shape[0]
    nq = s_len // tq
    nt = (((1,), (1,)), ((), ()))
    row = lax.broadcasted_iota(jnp.int32, (tq, tq), 0)
    col = lax.broadcasted_iota(jnp.int32, (tq, tq), 1)
    causal = row >= col
    for qi in range(nq):
        r0 = qi * tq
        q = q_ref[r0:r0 + tq, :]
        sd = lax.dot_general(q, k_ref[r0:r0 + tq, :], nt, preferred_element_type=f32) - c_ref[0, :, r0:r0 + tq]
        sd = jnp.where(causal, sd, NEG_BIG)
        m = jnp.max(sd, axis=-1, keepdims=True)
        if qi:
            so = lax.dot_general(q, k_ref[0:r0, :], nt, preferred_element_type=f32) - c_ref[0, :, 0:r0]
            m = jnp.maximum(m, jnp.max(so, axis=-1, keepdims=True))
        pd = jnp.exp(sd - m)
        l = jnp.sum(pd, axis=-1, keepdims=True)
        acc = jnp.dot(pd.astype(bf16), v_ref[r0:r0 + tq, :], preferred_element_type=f32)
        if qi:
            po = jnp.exp(so - m)
            l = l + jnp.sum(po, axis=-1, keepdims=True)
            acc = acc + jnp.dot(po.astype(bf16), v_ref[0:r0, :], preferred_element_type=f32)
        o_ref[r0:r0 + tq, :] = (acc / l).astype(o_ref.dtype)


def _fox_attention(proj, c_blk, b, s):
    tq = min(ATTN_BLOCK, s)
    h = FOX_HEADS
    return pl.pallas_call(
        functools.partial(_fox_kernel, tq=tq),
        grid=(b, h),
        in_specs=[pl.BlockSpec((s, HEAD_DIM), lambda i, j: (i, COL_FQ + j)),
                  pl.BlockSpec((s, HEAD_DIM), lambda i, j: (i, COL_FK + j)),
                  pl.BlockSpec((s, HEAD_DIM), lambda i, j: (i, COL_FV + j)),
                  pl.BlockSpec((1, 1, s), lambda i, j: (i * FOX_HEADS + j, 0, 0))],
        out_specs=pl.BlockSpec((s, HEAD_DIM), lambda i, j: (i, j)),
        out_shape=jax.ShapeDtypeStruct((b * s, FOX_WIDTH), bf16),
        compiler_params=_cparams(("parallel", "parallel")),
        name="fox_attention",
    )(proj, proj, proj, c_blk)


def _ret_kernel(q_ref, k_ref, v_ref, g_ref, cos_ref, sin_ref, dec_ref, xi_ref, zeta_ref, gc_ref, gnw_ref,
                o_ref, *, chunk):
    s_len = q_ref.shape[0]
    nc = s_len // chunk
    state = jnp.zeros((HEAD_DIM, HEAD_DIM), f32)
    for ci in range(nc):
        rows = slice(ci * chunk, (ci + 1) * chunk)
        cos = cos_ref[0, rows, :]
        sin = sin_ref[0, rows, :]

        def rot(t):
            t = t.astype(f32)
            return t * cos + pltpu.roll(t, HEAD_DIM // 2, axis=1) * sin

        q = rot(q_ref[rows, :])
        k = rot(k_ref[rows, :])
        v = v_ref[rows, :]
        qb = q.astype(bf16)
        inner = lax.dot_general(qb, k.astype(bf16), _NT, preferred_element_type=f32) * dec_ref[0]
        o = jnp.dot(inner.astype(bf16), v, preferred_element_type=f32)
        if ci:
            o = o + jnp.dot(qb, state.astype(bf16), preferred_element_type=f32) * xi_ref[0]
        if ci + 1 < nc:
            kz = (k * zeta_ref[0]).astype(bf16)
            kv = lax.dot_general(kz, v, (((0,), (0,)), ((), ())), preferred_element_type=f32)
            state = state * gc_ref[0] + kv if ci else kv
        mu = jnp.mean(o, axis=-1, keepdims=True)
        oc = o - mu
        var = jnp.mean(oc * oc, axis=-1, keepdims=True)
        y = oc * lax.rsqrt(var + EPS) * gnw_ref[...]
        g = g_ref[rows, :].astype(f32)
        o_ref[rows, :] = (g * jax.nn.sigmoid(g) * y).astype(o_ref.dtype)


def _retention_constants(chunk):
    hh = np.arange(RET_HEADS, dtype=np.float64)
    log_gamma = np.log1p(-(2.0 ** (-5.0 - hh)))
    n = np.arange(chunk, dtype=np.float64)
    diff = n[:, None] - n[None, :]
    decay = np.where(diff[None] >= 0, np.exp(diff[None] * log_gamma[:, None, None]), 0.0)
    xi = np.exp((n[None, :] + 1.0) * log_gamma[:, None])
    zeta = np.exp((chunk - 1.0 - n[None, :]) * log_gamma[:, None])
    g_chunk = np.exp(chunk * log_gamma)
    bc = lambda a: np.broadcast_to(a[:, :, None], (RET_HEADS, chunk, HEAD_DIM))
    gc = np.broadcast_to(g_chunk[:, None, None], (RET_HEADS, 1, HEAD_DIM))
    to = lambda a: jnp.asarray(np.ascontiguousarray(a), dtype=f32)
    return to(decay), to(bc(xi)), to(bc(zeta)), to(gc)


def _retention(proj, cos, sin, gn_w, b, s):
    chunk = min(RET_CHUNK, s)
    decay, xi, zeta, gc = _retention_constants(chunk)
    h = RET_HEADS
    head = lambda col: pl.BlockSpec((s, HEAD_DIM), lambda i, j: (i, col + j))
    per_head = lambda shape: pl.BlockSpec(shape, lambda i, j: (j, 0, 0))
    return pl.pallas_call(
        functools.partial(_ret_kernel, chunk=chunk),
        grid=(b, h),
        in_specs=[head(COL_RQ), head(COL_RK), head(COL_RV), head(COL_RG),
                  pl.BlockSpec((1, s, HEAD_DIM), lambda i, j: (i, 0, 0)),
                  pl.BlockSpec((1, s, HEAD_DIM), lambda i, j: (i, 0, 0)),
                  per_head((1, chunk, chunk)), per_head((1, chunk, HEAD_DIM)), per_head((1, chunk, HEAD_DIM)),
                  per_head((1, 1, HEAD_DIM)),
                  pl.BlockSpec((1, HEAD_DIM), lambda i, j: (0, j))],
        out_specs=pl.BlockSpec((s, HEAD_DIM), lambda i, j: (i, j)),
        out_shape=jax.ShapeDtypeStruct((b * s, RET_WIDTH), bf16),
        compiler_params=_cparams(("parallel", "parallel")),
        name="retention",
    )(proj, proj, proj, proj, cos, sin, decay, xi, zeta, gc, gn_w)


def _route(logits):
    lane = lax.broadcasted_iota(jnp.int32, logits.shape, 1).astype(f32)
    far = jnp.float32(1e9)
    gmask = lane < N_GROUPS
    gl = jnp.where(gmask, logits, NEG_BIG)
    gmax = jnp.max(gl, axis=-1, keepdims=True)
    gsel = jnp.min(jnp.where(gl == gmax, lane, far), axis=-1, keepdims=True)
    p_group = 1.0 / jnp.sum(jnp.where(gmask, jnp.exp(gl - gmax), 0.0), axis=-1, keepdims=True)
    lo = N_GROUPS + EXPERTS_PER_GROUP * gsel
    emask = (lane >= lo) & (lane < lo + EXPERTS_PER_GROUP)
    el = jnp.where(emask, logits, NEG_BIG)
    m1 = jnp.max(el, axis=-1, keepdims=True)
    i1 = jnp.min(jnp.where(el == m1, lane, far), axis=-1, keepdims=True)
    el2 = jnp.where(lane == i1, NEG_BIG, el)
    m2 = jnp.max(el2, axis=-1, keepdims=True)
    i2 = jnp.min(jnp.where(el2 == m2, lane, far), axis=-1, keepdims=True)
    t = jnp.exp(m2 - m1)
    w1 = p_group / (1.0 + t)
    w2 = p_group * t / (1.0 + t)
    out = jnp.where(lane == 0, i1 - N_GROUPS, 0.0)
    out = jnp.where(lane == 1, i2 - N_GROUPS, out)
    out = jnp.where(lane == 2, w1, out)
    out = jnp.where(lane == 3, w2, out)
    return out


def _merge_kernel(fo_ref, ro_ref, g0_ref, g1_ref, bg_ref, x_ref, wb_ref, wo_ref, nfw_ref, wrh_ref, wrl_ref, br_ref,
                  h_ref, hs_ref, route_ref, mg_ref, *, ncol):
    d = x_ref.shape[1]
    cw = d // ncol
    fo = fo_ref[...]
    ro = ro_ref[...]
    for c in range(ncol):
        cs = slice(c * cw, (c + 1) * cw)
        bd0 = jnp.dot(fo, wb_ref[0, :, cs], preferred_element_type=f32)
        bd1 = jnp.dot(ro, wb_ref[1, :, cs], preferred_element_type=f32)
        ga = jax.nn.sigmoid(g0_ref[:, cs].astype(f32) + bg_ref[0:1, cs])
        gb = jax.nn.sigmoid(g1_ref[:, cs].astype(f32) + bg_ref[1:2, cs])
        mg_ref[:, cs] = (ga * bd0 + gb * bd1).astype(bf16)
    h = x_ref[...] + jnp.dot(mg_ref[...], wo_ref[...], preferred_element_type=f32)
    h_ref[...] = h
    hn = h * lax.rsqrt(jnp.mean(h * h, axis=-1, keepdims=True) + EPS) * nfw_ref[...]
    _rows_to_slabs(hs_ref, hn, hn.shape[0])
    hi = hn.astype(bf16)
    lo = (hn - hi.astype(f32)).astype(bf16)
    logits = (jnp.dot(hi, wrh_ref[...], preferred_element_type=f32)
              + jnp.dot(lo, wrh_ref[...], preferred_element_type=f32)
              + jnp.dot(hi, wrl_ref[...], preferred_element_type=f32)) + br_ref[...]
    route_ref[...] = _route(logits)


def _merge(fox_o, ret_o, proj, b_gate, x2, w_branch, w_out, nfw, wr_hi, wr_lo, b_router):
    t, d = x2.shape
    tm = min(256, t)
    gate0 = COL_GATE * HEAD_DIM // d
    const = lambda shape: pl.BlockSpec(shape, lambda i: tuple(0 for _ in shape), pipeline_mode=pl.Buffered(1))
    return pl.pallas_call(
        functools.partial(_merge_kernel, ncol=4),
        grid=(t // tm,),
        in_specs=[pl.BlockSpec((tm, FOX_WIDTH), lambda i: (i, 0)),
                  pl.BlockSpec((tm, RET_WIDTH), lambda i: (i, 0)),
                  pl.BlockSpec((tm, d), lambda i: (i, gate0)),
                  pl.BlockSpec((tm, d), lambda i: (i, gate0 + 1)),
                  const((N_BRANCH, d)),
                  pl.BlockSpec((tm, d), lambda i: (i, 0)),
                  const((N_BRANCH, FOX_WIDTH, d)),
                  const((d, d)),
                  const((1, d)),
                  const((d, LANES)), const((d, LANES)), const((1, LANES))],
        out_specs=[pl.BlockSpec((tm, d), lambda i: (i, 0)),
                   pl.BlockSpec((tm * PITCH, LANES), lambda i: (i, 0)),
                   pl.BlockSpec((tm, LANES), lambda i: (i, 0))],
        out_shape=[jax.ShapeDtypeStruct((t, d), f32), jax.ShapeDtypeStruct((t * PITCH, LANES), u32),
                   jax.ShapeDtypeStruct((t, LANES), f32)],
        scratch_shapes=[pltpu.VMEM((tm, d), bf16)],
        compiler_params=_cparams(("parallel",)),
        name="merge_outproj_router",
    )(fox_o, ret_o, proj, proj, b_gate, x2, w_branch, w_out, nfw, wr_hi, wr_lo, b_router)


def _plan_kernel(route_ref, dest_ref, ends_ref, carry_ref, *, bm):
    phase = pl.program_id(0)
    i = pl.program_id(1)
    r = route_ref[...]
    tb = r.shape[0]
    lane = lax.broadcasted_iota(jnp.int32, r.shape, 1).astype(f32)
    o0 = (lane == r[:, 0:1]).astype(f32)
    o1 = (lane == r[:, 1:2]).astype(f32)
    o = o0 + o1

    @pl.when((phase == 0) & (i == 0))
    def _():
        carry_ref[...] = jnp.zeros_like(carry_ref)

    @pl.when((phase == 1) & (i == 0))
    def _():
        cnt = carry_ref[...]
        padded = jnp.floor((cnt + (bm - 1)) * (1.0 / bm)) * bm
        lane8 = lax.broadcasted_iota(jnp.int32, cnt.shape, 1)
        ends = padded
        sh = 1
        while sh < LANES:
            ends = ends + jnp.where(lane8 >= sh, pltpu.roll(ends, sh, axis=1), 0.0)
            sh *= 2
        sub = lax.broadcasted_iota(jnp.int32, cnt.shape, 0)
        ends_ref[...] = jnp.where(sub == 0, ends, padded)
        carry_ref[...] = ends - padded

    @pl.when(phase == 1)
    def _():
        row = lax.broadcasted_iota(jnp.int32, (tb, tb), 0)
        col = lax.broadcasted_iota(jnp.int32, (tb, tb), 1)
        lower = jnp.where(col < row, 1.0, 0.0).astype(bf16)
        before = jnp.dot(lower, o.astype(bf16), preferred_element_type=f32) + carry_ref[0:1, :]
        d0 = jnp.sum(before * o0, axis=-1, keepdims=True)
        d1 = jnp.sum(before * o1, axis=-1, keepdims=True)
        dd = jnp.where(lane == 0, d0, jnp.where(lane == 1, d1, 0.0))
        dest_ref[0] = dd.T[:8, :].astype(jnp.int32)

    carry_ref[...] += jnp.sum(o, axis=0, keepdims=True)


def _dispatch_plan(route, bm):
    t = route.shape[0]
    tb = min(512, t)
    nb = t // tb
    return pl.pallas_call(
        functools.partial(_plan_kernel, bm=bm),
        grid=(2, nb),
        in_specs=[pl.BlockSpec((tb, LANES), lambda p, i: (i, 0))],
        out_specs=[pl.BlockSpec((1, 8, tb), lambda p, i: (i * p, 0, 0)),
                   pl.BlockSpec((8, LANES), lambda p, i: (0, 0))],
        out_shape=[jax.ShapeDtypeStruct((nb, 8, tb), jnp.int32), jax.ShapeDtypeStruct((8, LANES), f32)],
        scratch_shapes=[pltpu.VMEM((8, LANES), f32)],
        compiler_params=_cparams(("arbitrary", "arbitrary")),
        name="dispatch_plan",
    )(route)


def _slot_kernel(d0_ref, d1_ref, zeros_hbm, tok_ref, sem, *, t, unroll=8):
    init = pltpu.make_async_copy(zeros_hbm, tok_ref, sem)
    init.start()
    init.wait()

    def fill(i, c):
        for u in range(unroll):
            tt = i * unroll + u
            tok_ref[d0_ref[tt]] = tt
            tok_ref[d1_ref[tt]] = tt
        return c

    lax.fori_loop(0, t // unroll, fill, 0)


def _slot_tokens(dest0, dest1, cap):
    t = dest0.shape[0]
    smem = pl.BlockSpec(memory_space=pltpu.SMEM)
    return pl.pallas_call(
        functools.partial(_slot_kernel, t=t),
        in_specs=[smem, smem, pl.BlockSpec(memory_space=pl.ANY)],
        out_specs=smem,
        out_shape=jax.ShapeDtypeStruct((cap,), jnp.int32),
        scratch_shapes=[pltpu.SemaphoreType.DMA(())],
        name="slot_tokens",
    )(dest0, dest1, jnp.zeros((cap,), jnp.int32))


def _moe_kernel(g0_ref, nb_ref, nu_ref, tok_ref, h_hbm, w1_ref, w3_ref, w2_ref, y_hbm,
                xbuf, ybuf, w1b, w3b, w2b, gsem, ysem, *, bm, n_blk):
    e = pl.program_id(0)
    n_used = nu_ref[0]
    g0 = g0_ref[e]
    nb = nb_ref[e]
    blk_rows = bm * PITCH

    def y_copy(slot, g):
        dst = y_hbm.at[pl.ds(pl.multiple_of(g * blk_rows, 8), blk_rows), :]
        return pltpu.make_async_copy(ybuf.at[slot], dst, ysem.at[slot])

    @pl.when(e == 0)
    def _():
        _slab_gather_start(tok_ref, 0, bm, h_hbm, xbuf.at[0], gsem.at[0])

    @pl.when(nb > 0)
    def _():
        w1b[...] = w1_ref[0].astype(bf16)
        w3b[...] = w3_ref[0].astype(bf16)
        w2b[...] = w2_ref[0].astype(bf16)

    def block(c, carry):
        g = g0 + c
        slot = g % 2

        @pl.when(g + 1 < n_used)
        def _():
            _slab_gather_start(tok_ref, (g + 1) * bm, bm, h_hbm, xbuf.at[1 - slot], gsem.at[1 - slot])

        _slab_gather_wait(bm, h_hbm, xbuf.at[slot], gsem.at[slot])
        hn = _slabs_to_rows(xbuf.at[slot], bm).astype(bf16)
        a = jnp.dot(hn, w1b[...], preferred_element_type=f32)
        b = jnp.dot(hn, w3b[...], preferred_element_type=f32)
        mid = (a * jax.nn.sigmoid(a) * b).astype(bf16)
        y = jnp.dot(mid, w2b[...], preferred_element_type=f32)

        @pl.when(g >= 2)
        def _():
            y_copy(slot, g - 2).wait()

        _rows_to_slabs(ybuf.at[slot], y, bm)
        y_copy(slot, g).start()
        return carry

    lax.fori_loop(0, nb, block, 0)

    @pl.when(e == pl.num_programs(0) - 1)
    def _():
        @pl.when(n_used >= 2)
        def _():
            y_copy(n_used % 2, n_used - 2).wait()

        y_copy((n_used - 1) % 2, n_used - 1).wait()
        ybuf[0] = jnp.zeros(ybuf.shape[1:], u32)

        def zstart(g, carry):
            y_copy(0, g).start()
            return carry

        def zwait(g, carry):
            y_copy(0, g).wait()
            return carry

        lax.fori_loop(n_used, n_blk, zstart, 0)
        lax.fori_loop(n_used, n_blk, zwait, 0)


def _expert_mlp(blk_start, blk_count, n_used, buf_tok, hs, w1, w3, w2, cap):
    d = D_MODEL
    bm = MOE_BLOCK
    n_blk = cap // bm
    by_expert = lambda shape: pl.BlockSpec(shape, lambda e, g0, nb, nu, tok: (e, 0, 0))
    grid_spec = pltpu.PrefetchScalarGridSpec(
        num_scalar_prefetch=4,
        grid=(N_EXPERTS,),
        in_specs=[pl.BlockSpec(memory_space=pl.ANY),
                  by_expert((1, d, D_EXPERT)), by_expert((1, d, D_EXPERT)), by_expert((1, D_EXPERT, d))],
        out_specs=pl.BlockSpec(memory_space=pl.ANY),
        scratch_shapes=[pltpu.VMEM((2, bm * PITCH, LANES), u32), pltpu.VMEM((2, bm * PITCH, LANES), u32),
                        pltpu.VMEM((d, D_EXPERT), bf16), pltpu.VMEM((d, D_EXPERT), bf16),
                        pltpu.VMEM((D_EXPERT, d), bf16),
                        pltpu.SemaphoreType.DMA((2,)), pltpu.SemaphoreType.DMA((2,))],
    )
    return pl.pallas_call(
        functools.partial(_moe_kernel, bm=bm, n_blk=n_blk),
        grid_spec=grid_spec,
        out_shape=jax.ShapeDtypeStruct((cap * PITCH, LANES), u32),
        compiler_params=_cparams(("arbitrary",)),
        name="expert_mlp",
    )(blk_start, blk_count, n_used, buf_tok, hs, w1, w3, w2)


def _combine_kernel(d0_ref, d1_ref, h_ref, route_ref, y_hbm, nw_ref, o_ref, ybuf, sem, *, tb):
    i = pl.program_id(0)
    n = pl.num_programs(0)
    slot = i % 2

    def start(blk, s):
        _slab_gather_start(d0_ref, blk * tb, tb, y_hbm, ybuf.at[s, 0], sem.at[s])
        _slab_gather_start(d1_ref, blk * tb, tb, y_hbm, ybuf.at[s, 1], sem.at[s])

    @pl.when(i == 0)
    def _():
        start(0, 0)

    @pl.when(i + 1 < n)
    def _():
        start(i + 1, 1 - slot)

    for k in range(TOP_K):
        _slab_gather_wait(tb, y_hbm, ybuf.at[slot, k], sem.at[slot])
    r = route_ref[...]
    h = h_ref[...] + (r[:, 2:3] * _slabs_to_rows(ybuf.at[slot, 0], tb)
                      + r[:, 3:4] * _slabs_to_rows(ybuf.at[slot, 1], tb))
    o_ref[...] = h * lax.rsqrt(jnp.mean(h * h, axis=-1, keepdims=True) + EPS) * nw_ref[...]


def _combine(dest0, dest1, h, route, ys, norm_w):
    t, d = h.shape
    tb = min(256, t)
    grid_spec = pltpu.PrefetchScalarGridSpec(
        num_scalar_prefetch=2,
        grid=(t // tb,),
        in_specs=[pl.BlockSpec((tb, d), lambda i, a, b: (i, 0)),
                  pl.BlockSpec((tb, LANES), lambda i, a, b: (i, 0)),
                  pl.BlockSpec(memory_space=pl.ANY),
                  pl.BlockSpec((1, d), lambda i, a, b: (0, 0))],
        out_specs=pl.BlockSpec((tb, d), lambda i, a, b: (i, 0)),
        scratch_shapes=[pltpu.VMEM((2, TOP_K, tb * PITCH, LANES), u32), pltpu.SemaphoreType.DMA((2,))],
    )
    return pl.pallas_call(
        functools.partial(_combine_kernel, tb=tb),
        grid_spec=grid_spec,
        out_shape=jax.ShapeDtypeStruct((t, d), f32),
        compiler_params=_cparams(("arbitrary",)),
        name="combine_final_norm",
    )(dest0, dest1, h, route, ys, norm_w)


def kernel(x, positions, norm_mix_w, w_in, fox_b_f, ret_gn_w, w_branch, b_gate, w_out, norm_ffn_w, w_router_group,
           b_router_group, w_router_expert, b_router_expert, w1, w3, w2, norm_final_w):
    b, s, d = x.shape
    assert d == D_MODEL and norm_mix_w.shape[0] == 1
    t = b * s
    x2 = x.reshape(t, d)
    scale = 1.0 / math.sqrt(HEAD_DIM)

    wt = w_in[0].T
    o_f = 3 * FOX_WIDTH
    o_r = o_f + FOX_HEADS
    o_g = o_r + 4 * RET_WIDTH
    w_main = jnp.concatenate([wt[o_g:], wt[:FOX_WIDTH] * scale, wt[FOX_WIDTH:o_f],
                              wt[o_r:o_r + RET_WIDTH] * scale, wt[o_r + RET_WIDTH:o_g]], axis=0).astype(bf16)
    w_f = jnp.pad(wt[o_f:o_r], ((0, LANES - FOX_HEADS), (0, 0))).astype(bf16)
    bias_f = jnp.pad(fox_b_f[0], (0, LANES - FOX_HEADS)).reshape(1, LANES)
    w_r = jnp.concatenate([w_router_group[0], w_router_expert[0]], axis=1)
    w_r = jnp.pad(w_r, ((0, 0), (0, LANES - w_r.shape[1])))
    wr_hi = w_r.astype(bf16)
    wr_lo = (w_r - wr_hi.astype(f32)).astype(bf16)
    b_r = jnp.pad(jnp.concatenate([b_router_group[0], b_router_expert[0]]), (0, LANES - N_GROUPS - N_EXPERTS))
    b_r = b_r.reshape(1, LANES)

    cos, sin = _rope_tables(positions)
    proj, f_logit = _in_projection(x2, norm_mix_w, w_main, w_f)
    c = _forget_cumsum(f_logit, bias_f, b, s)
    fox_o = _fox_attention(proj, c.reshape(b * FOX_HEADS, 1, s), b, s)
    ret_o = _retention(proj, cos, sin, ret_gn_w, b, s)
    h, h_slabs, route = _merge(fox_o, ret_o, proj, b_gate[0], x2, w_branch[0].astype(bf16), w_out[0].astype(bf16),
                      norm_ffn_w, wr_hi, wr_lo, b_r)

    bm = MOE_BLOCK
    cap = t * TOP_K + N_EXPERTS * bm
    dest_rows, seg = _dispatch_plan(route, bm)
    dest0 = dest_rows[:, 0, :].reshape(t)
    dest1 = dest_rows[:, 1, :].reshape(t)
    pends = seg[0, :N_EXPERTS].astype(jnp.int32)
    padded = seg[1, :N_EXPERTS].astype(jnp.int32)
    buf_tok = _slot_tokens(dest0, dest1, cap)
    n_used = pends[-1:] // bm

    y = _expert_mlp((pends - padded) // bm, padded // bm, n_used, buf_tok, h_slabs, w1[0], w3[0], w2[0], cap)
    out = _combine(dest0, dest1, h, route, y, norm_final_w.reshape(1, d))
    return out.reshape(b, s, d)
```

```python
import functools
import math

import numpy as np
import jax
import jax.numpy as jnp
from jax import lax
from jax.experimental import pallas as pl
from jax.experimental.pallas import tpu as pltpu

D_MODEL = 2048
HEAD_DIM = 128
FOX_HEADS = 8
RET_HEADS = 8
FOX_WIDTH = FOX_HEADS * HEAD_DIM
RET_WIDTH = RET_HEADS * HEAD_DIM
N_BRANCH = 2
N_GROUPS = 4
EXPERTS_PER_GROUP = 8
N_EXPERTS = N_GROUPS * EXPERTS_PER_GROUP
TOP_K = 2
D_EXPERT = 512
ROPE_BASE = 10000.0
EPS = 1e-6

LANES = 128
NEG_BIG = -1e30
MAIN_WIDTH = 3 * FOX_WIDTH + 4 * RET_WIDTH + N_BRANCH * D_MODEL
COL_GATE = 0
COL_FQ = N_BRANCH * D_MODEL // HEAD_DIM
COL_FK = COL_FQ + FOX_HEADS
COL_FV = COL_FK + FOX_HEADS
COL_RQ = COL_FV + FOX_HEADS
COL_RK = COL_RQ + RET_HEADS
COL_RV = COL_RK + RET_HEADS
COL_RG = COL_RV + RET_HEADS

ATTN_BLOCK = 256
RET_CHUNK = 256
MOE_BLOCK = 256
VMEM_LIMIT = 56 * 1024 * 1024

f32 = jnp.float32
bf16 = jnp.bfloat16
_NT = (((1,), (1,)), ((), ()))
SLAB = D_MODEL // (2 * LANES)
PITCH = SLAB + 1
u32 = jnp.uint32
_HI_MASK = np.uint32(0xFFFF0000)


def _rows_to_slabs(ref2, val, n_rows):
    bits = lambda v: lax.bitcast_convert_type(v.astype(bf16).astype(f32), u32)
    for j in range(SLAB):
        lo = val[:, (2 * j) * LANES:(2 * j + 1) * LANES]
        hi = val[:, (2 * j + 1) * LANES:(2 * j + 2) * LANES]
        ref2[pl.ds(j, n_rows, stride=PITCH), :] = (bits(lo) >> 16) | (bits(hi) & _HI_MASK)
    ref2[pl.ds(SLAB, n_rows, stride=PITCH), :] = jnp.zeros((n_rows, LANES), u32)


def _slabs_to_rows(ref2, n_rows):
    chunks = []
    for j in range(SLAB):
        w = ref2[pl.ds(j, n_rows, stride=PITCH), :]
        chunks.append(lax.bitcast_convert_type(w << 16, f32))
        chunks.append(lax.bitcast_convert_type(w & _HI_MASK, f32))
    return jnp.concatenate(chunks, axis=-1)


def _slab_gather_start(idx_ref, base, n_rows, src_hbm, dst, sem, unroll=8):
    def body(g, _):
        for u in range(unroll):
            r = g * unroll + u
            t = idx_ref[base + r]
            pltpu.make_async_copy(src_hbm.at[pl.ds(t * PITCH, SLAB), :], dst.at[pl.ds(r * PITCH, SLAB), :],
                                  sem).start(priority=u % 2)
        return 0
    lax.fori_loop(0, n_rows // unroll, body, 0)


def _slab_gather_wait(n_rows, src_hbm, dst, sem):
    pltpu.make_async_copy(src_hbm.at[pl.ds(0, n_rows * SLAB), :], dst.at[pl.ds(0, n_rows * SLAB), :], sem).wait()


def _cparams(sem, vmem=VMEM_LIMIT):
    return pltpu.CompilerParams(dimension_semantics=sem, vmem_limit_bytes=vmem)


def _rope_kernel(pos_ref, freq_ref, sgn_ref, cos_ref, sin_ref):
    ang = pos_ref[0].astype(f32) * freq_ref[...]
    cos_ref[0] = jnp.cos(ang)
    sin_ref[0] = jnp.sin(ang) * sgn_ref[...]


def _rope_tables(positions):
    b, s = positions.shape
    half = HEAD_DIM // 2
    inv = (np.float32(ROPE_BASE) ** (-np.arange(half, dtype=np.float32) / np.float32(half))).astype(np.float32)
    freq = jnp.asarray(np.concatenate([inv, inv])[None, :])
    sgn = jnp.asarray(np.concatenate([-np.ones(half, np.float32), np.ones(half, np.float32)])[None, :])
    out = jax.ShapeDtypeStruct((b, s, HEAD_DIM), f32)
    return pl.pallas_call(
        _rope_kernel,
        grid=(b,),
        in_specs=[pl.BlockSpec((1, s, 1), lambda i: (i, 0, 0)),
                  pl.BlockSpec((1, HEAD_DIM), lambda i: (0, 0)),
                  pl.BlockSpec((1, HEAD_DIM), lambda i: (0, 0))],
        out_specs=[pl.BlockSpec((1, s, HEAD_DIM), lambda i: (i, 0, 0)),
                   pl.BlockSpec((1, s, HEAD_DIM), lambda i: (i, 0, 0))],
        out_shape=[out, out],
        compiler_params=_cparams(("parallel",)),
        name="rope_tables",
    )(positions.reshape(b, s, 1), freq, sgn)


def _inproj_kernel(x_ref, nw_ref, w_ref, wf_ref, o_ref, f_ref, xn_ref):
    @pl.when(pl.program_id(1) == 0)
    def _():
        x = x_ref[...]
        ms = jnp.mean(x * x, axis=-1, keepdims=True)
        xn = (x * lax.rsqrt(ms + EPS) * nw_ref[...]).astype(bf16)
        xn_ref[...] = xn
        f_ref[...] = lax.dot_general(xn, wf_ref[...], _NT, preferred_element_type=f32)

    o_ref[...] = lax.dot_general(xn_ref[...], w_ref[...], _NT, preferred_element_type=f32).astype(o_ref.dtype)


def _in_projection(x2, norm_w, w_main_t, w_f_t):
    t, d = x2.shape
    n = w_main_t.shape[0]
    tm = min(1024, t)
    tn = 1024
    return pl.pallas_call(
        _inproj_kernel,
        grid=(t // tm, n // tn),
        in_specs=[pl.BlockSpec((tm, d), lambda i, j: (i, 0)),
                  pl.BlockSpec((1, d), lambda i, j: (0, 0)),
                  pl.BlockSpec((tn, d), lambda i, j: (j, 0)),
                  pl.BlockSpec((LANES, d), lambda i, j: (0, 0))],
        out_specs=[pl.BlockSpec((tm, tn), lambda i, j: (i, j)),
                   pl.BlockSpec((tm, LANES), lambda i, j: (i, 0))],
        out_shape=[jax.ShapeDtypeStruct((t, n), bf16), jax.ShapeDtypeStruct((t, LANES), f32)],
        scratch_shapes=[pltpu.VMEM((tm, d), bf16)],
        compiler_params=_cparams(("parallel", "arbitrary")),
        name="in_projection",
    )(x2, norm_w, w_main_t, w_f_t)


def _fcum_kernel(f_ref, b_ref, c_ref):
    z = f_ref[...] + b_ref[...]
    ls = jnp.minimum(z, 0.0) - jnp.log1p(jnp.exp(-jnp.abs(z)))
    c = ls.T[:FOX_HEADS, :]
    s = c.shape[1]
    lane = lax.broadcasted_iota(jnp.int32, c.shape, 1)
    sh = 1
    while sh < s:
        c = c + jnp.where(lane >= sh, pltpu.roll(c, sh, axis=1), 0.0)
        sh *= 2
    c_ref[0] = c


def _forget_cumsum(f_logit, bias, b, s):
    return pl.pallas_call(
        _fcum_kernel,
        grid=(b,),
        in_specs=[pl.BlockSpec((s, LANES), lambda i: (i, 0)),
                  pl.BlockSpec((1, LANES), lambda i: (0, 0))],
        out_specs=pl.BlockSpec((1, FOX_HEADS, s), lambda i: (i, 0, 0)),
        out_shape=jax.ShapeDtypeStruct((b, FOX_HEADS, s), f32),
        compiler_params=_cparams(("parallel",)),
        name="forget_cumsum",
    )(f_logit, bias)


def _fox_kernel(q_ref, k_ref, v_ref, c_ref, o_ref, *, tq):
    s_len = q_ref.shape[0]
    nq = s_len // tq
    nt = (((1,), (1,)), ((), ()))
    row = lax.broadcasted_iota(jnp.int32, (tq, tq), 0)
    col = lax.broadcasted_iota(jnp.int32, (tq, tq), 1)
    causal = row >= col
    for qi in range(nq):
        r0 = qi * tq
        q = q_ref[r0:r0 + tq, :]
        sd = lax.dot_general(q, k_ref[r0:r0 + tq, :], nt, preferred_element_type=f32) - c_ref[0, :, r0:r0 + tq]
        sd = jnp.where(causal, sd, NEG_BIG)
        m = jnp.max(sd, axis=-1, keepdims=True)
        if qi:
            so = lax.dot_general(q, k_ref[0:r0, :], nt, preferred_element_type=f32) - c_ref[0, :, 0:r0]
            m = jnp.maximum(m, jnp.max(so, axis=-1, keepdims=True))
        pd = jnp.exp(sd - m)
        l = jnp.sum(pd, axis=-1, keepdims=True)
        acc = jnp.dot(pd.astype(bf16), v_ref[r0:r0 + tq, :], preferred_element_type=f32)
        if qi:
            po = jnp.exp(so - m)
            l = l + jnp.sum(po, axis=-1, keepdims=True)
            acc = acc + jnp.dot(po.astype(bf16), v_ref[0:r0, :], preferred_element_type=f32)
        o_ref[r0:r0 + tq, :] = (acc / l).astype(o_ref.dtype)


def _fox_attention(proj, c_blk, b, s):
    tq = min(ATTN_BLOCK, s)
    h = FOX_HEADS
    return pl.pallas_call(
        functools.partial(_fox_kernel, tq=tq),
        grid=(b, h),
        in_specs=[pl.BlockSpec((s, HEAD_DIM), lambda i, j: (i, COL_FQ + j)),
                  pl.BlockSpec((s, HEAD_DIM), lambda i, j: (i, COL_FK + j)),
                  pl.BlockSpec((s, HEAD_DIM), lambda i, j: (i, COL_FV + j)),
                  pl.BlockSpec((1, 1, s), lambda i, j: (i * FOX_HEADS + j, 0, 0))],
        out_specs=pl.BlockSpec((s, HEAD_DIM), lambda i, j: (i, j)),
        out_shape=jax.ShapeDtypeStruct((b * s, FOX_WIDTH), bf16),
        compiler_params=_cparams(("parallel", "parallel")),
        name="fox_attention",
    )(proj, proj, proj, c_blk)


def _ret_kernel(q_ref, k_ref, v_ref, g_ref, cos_ref, sin_ref, dec_ref, xi_ref, zeta_ref, gc_ref, gnw_ref,
                o_ref, *, chunk):
    s_len = q_ref.shape[0]
    nc = s_len // chunk
    state = jnp.zeros((HEAD_DIM, HEAD_DIM), f32)
    for ci in range(nc):
        rows = slice(ci * chunk, (ci + 1) * chunk)
        cos = cos_ref[0, rows, :]
        sin = sin_ref[0, rows, :]

        def rot(t):
            t = t.astype(f32)
            return t * cos + pltpu.roll(t, HEAD_DIM // 2, axis=1) * sin

        q = rot(q_ref[rows, :])
        k = rot(k_ref[rows, :])
        v = v_ref[rows, :]
        qb = q.astype(bf16)
        inner = lax.dot_general(qb, k.astype(bf16), _NT, preferred_element_type=f32) * dec_ref[0]
        o = jnp.dot(inner.astype(bf16), v, preferred_element_type=f32)
        if ci:
            o = o + jnp.dot(qb, state.astype(bf16), preferred_element_type=f32) * xi_ref[0]
        if ci + 1 < nc:
            kz = (k * zeta_ref[0]).astype(bf16)
            kv = lax.dot_general(kz, v, (((0,), (0,)), ((), ())), preferred_element_type=f32)
            state = state * gc_ref[0] + kv if ci else kv
        mu = jnp.mean(o, axis=-1, keepdims=True)
        oc = o - mu
        var = jnp.mean(oc * oc, axis=-1, keepdims=True)
        y = oc * lax.rsqrt(var + EPS) * gnw_ref[...]
        g = g_ref[rows, :].astype(f32)
        o_ref[rows, :] = (g * jax.nn.sigmoid(g) * y).astype(o_ref.dtype)


def _retention_constants(chunk):
    hh = np.arange(RET_HEADS, dtype=np.float64)
    log_gamma = np.log1p(-(2.0 ** (-5.0 - hh)))
    n = np.arange(chunk, dtype=np.float64)
    diff = n[:, None] - n[None, :]
    decay = np.where(diff[None] >= 0, np.exp(diff[None] * log_gamma[:, None, None]), 0.0)
    xi = np.exp((n[None, :] + 1.0) * log_gamma[:, None])
    zeta = np.exp((chunk - 1.0 - n[None, :]) * log_gamma[:, None])
    g_chunk = np.exp(chunk * log_gamma)
    bc = lambda a: np.broadcast_to(a[:, :, None], (RET_HEADS, chunk, HEAD_DIM))
    gc = np.broadcast_to(g_chunk[:, None, None], (RET_HEADS, 1, HEAD_DIM))
    to = lambda a: jnp.asarray(np.ascontiguousarray(a), dtype=f32)
    return to(decay), to(bc(xi)), to(bc(zeta)), to(gc)


def _retention(proj, cos, sin, gn_w, b, s):
    chunk = min(RET_CHUNK, s)
    decay, xi, zeta, gc = _retention_constants(chunk)
    h = RET_HEADS
    head = lambda col: pl.BlockSpec((s, HEAD_DIM), lambda i, j: (i, col + j))
    per_head = lambda shape: pl.BlockSpec(shape, lambda i, j: (j, 0, 0))
    return pl.pallas_call(
        functools.partial(_ret_kernel, chunk=chunk),
        grid=(b, h),
        in_specs=[head(COL_RQ), head(COL_RK), head(COL_RV), head(COL_RG),
                  pl.BlockSpec((1, s, HEAD_DIM), lambda i, j: (i, 0, 0)),
                  pl.BlockSpec((1, s, HEAD_DIM), lambda i, j: (i, 0, 0)),
                  per_head((1, chunk, chunk)), per_head((1, chunk, HEAD_DIM)), per_head((1, chunk, HEAD_DIM)),
                  per_head((1, 1, HEAD_DIM)),
                  pl.BlockSpec((1, HEAD_DIM), lambda i, j: (0, j))],
        out_specs=pl.BlockSpec((s, HEAD_DIM), lambda i, j: (i, j)),
        out_shape=jax.ShapeDtypeStruct((b * s, RET_WIDTH), bf16),
        compiler_params=_cparams(("parallel", "parallel")),
        name="retention",
    )(proj, proj, proj, proj, cos, sin, decay, xi, zeta, gc, gn_w)


def _route(logits):
    lane = lax.broadcasted_iota(jnp.int32, logits.shape, 1).astype(f32)
    far = jnp.float32(1e9)
    gmask = lane < N_GROUPS
    gl = jnp.where(gmask, logits, NEG_BIG)
    gmax = jnp.max(gl, axis=-1, keepdims=True)
    gsel = jnp.min(jnp.where(gl == gmax, lane, far), axis=-1, keepdims=True)
    p_group = 1.0 / jnp.sum(jnp.where(gmask, jnp.exp(gl - gmax), 0.0), axis=-1, keepdims=True)
    lo = N_GROUPS + EXPERTS_PER_GROUP * gsel
    emask = (lane >= lo) & (lane < lo + EXPERTS_PER_GROUP)
    el = jnp.where(emask, logits, NEG_BIG)
    m1 = jnp.max(el, axis=-1, keepdims=True)
    i1 = jnp.min(jnp.where(el == m1, lane, far), axis=-1, keepdims=True)
    el2 = jnp.where(lane == i1, NEG_BIG, el)
    m2 = jnp.max(el2, axis=-1, keepdims=True)
    i2 = jnp.min(jnp.where(el2 == m2, lane, far), axis=-1, keepdims=True)
    t = jnp.exp(m2 - m1)
    w1 = p_group / (1.0 + t)
    w2 = p_group * t / (1.0 + t)
    out = jnp.where(lane == 0, i1 - N_GROUPS, 0.0)
    out = jnp.where(lane == 1, i2 - N_GROUPS, out)
    out = jnp.where(lane == 2, w1, out)
    out = jnp.where(lane == 3, w2, out)
    return out


def _merge_kernel(fo_ref, ro_ref, g0_ref, g1_ref, bg_ref, x_ref, wb_ref, wo_ref, nfw_ref, wrh_ref, wrl_ref, br_ref,
                  h_ref, hs_ref, route_ref, mg_ref, *, ncol):
    d = x_ref.shape[1]
    cw = d // ncol
    fo = fo_ref[...]
    ro = ro_ref[...]
    for c in range(ncol):
        cs = slice(c * cw, (c + 1) * cw)
        bd0 = jnp.dot(fo, wb_ref[0, :, cs], preferred_element_type=f32)
        bd1 = jnp.dot(ro, wb_ref[1, :, cs], preferred_element_type=f32)
        ga = jax.nn.sigmoid(g0_ref[:, cs].astype(f32) + bg_ref[0:1, cs])
        gb = jax.nn.sigmoid(g1_ref[:, cs].astype(f32) + bg_ref[1:2, cs])
        mg_ref[:, cs] = (ga * bd0 + gb * bd1).astype(bf16)
    h = x_ref[...] + jnp.dot(mg_ref[...], wo_ref[...], preferred_element_type=f32)
    h_ref[...] = h
    hn = h * lax.rsqrt(jnp.mean(h * h, axis=-1, keepdims=True) + EPS) * nfw_ref[...]
    _rows_to_slabs(hs_ref, hn, hn.shape[0])
    hi = hn.astype(bf16)
    lo = (hn - hi.astype(f32)).astype(bf16)
    logits = (jnp.dot(hi, wrh_ref[...], preferred_element_type=f32)
              + jnp.dot(lo, wrh_ref[...], preferred_element_type=f32)
              + jnp.dot(hi, wrl_ref[...], preferred_element_type=f32)) + br_ref[...]
    route_ref[...] = _route(logits)


def _merge(fox_o, ret_o, proj, b_gate, x2, w_branch, w_out, nfw, wr_hi, wr_lo, b_router):
    t, d = x2.shape
    tm = min(256, t)
    gate0 = COL_GATE * HEAD_DIM // d
    const = lambda shape: pl.BlockSpec(shape, lambda i: tuple(0 for _ in shape), pipeline_mode=pl.Buffered(1))
    return pl.pallas_call(
        functools.partial(_merge_kernel, ncol=4),
        grid=(t // tm,),
        in_specs=[pl.BlockSpec((tm, FOX_WIDTH), lambda i: (i, 0)),
                  pl.BlockSpec((tm, RET_WIDTH), lambda i: (i, 0)),
                  pl.BlockSpec((tm, d), lambda i: (i, gate0)),
                  pl.BlockSpec((tm, d), lambda i: (i, gate0 + 1)),
                  const((N_BRANCH, d)),
                  pl.BlockSpec((tm, d), lambda i: (i, 0)),
                  const((N_BRANCH, FOX_WIDTH, d)),
                  const((d, d)),
                  const((1, d)),
                  const((d, LANES)), const((d, LANES)), const((1, LANES))],
        out_specs=[pl.BlockSpec((tm, d), lambda i: (i, 0)),
                   pl.BlockSpec((tm * PITCH, LANES), lambda i: (i, 0)),
                   pl.BlockSpec((tm, LANES), lambda i: (i, 0))],
        out_shape=[jax.ShapeDtypeStruct((t, d), f32), jax.ShapeDtypeStruct((t * PITCH, LANES), u32),
                   jax.ShapeDtypeStruct((t, LANES), f32)],
        scratch_shapes=[pltpu.VMEM((tm, d), bf16)],
        compiler_params=_cparams(("parallel",)),
        name="merge_outproj_router",
    )(fox_o, ret_o, proj, proj, b_gate, x2, w_branch, w_out, nfw, wr_hi, wr_lo, b_router)


def _plan_kernel(route_ref, dest_ref, ends_ref, carry_ref, *, bm):
    phase = pl.program_id(0)
    i = pl.program_id(1)
    r = route_ref[...]
    tb = r.shape[0]
    lane = lax.broadcasted_iota(jnp.int32, r.shape, 1).astype(f32)
    o0 = (lane == r[:, 0:1]).astype(f32)
    o1 = (lane == r[:, 1:2]).astype(f32)
    o = o0 + o1

    @pl.when((phase == 0) & (i == 0))
    def _():
        carry_ref[...] = jnp.zeros_like(carry_ref)

    @pl.when((phase == 1) & (i == 0))
    def _():
        cnt = carry_ref[...]
        padded = jnp.floor((cnt + (bm - 1)) * (1.0 / bm)) * bm
        lane8 = lax.broadcasted_iota(jnp.int32, cnt.shape, 1)
        ends = padded
        sh = 1
        while sh < LANES:
            ends = ends + jnp.where(lane8 >= sh, pltpu.roll(ends, sh, axis=1), 0.0)
            sh *= 2
        sub = lax.broadcasted_iota(jnp.int32, cnt.shape, 0)
        ends_ref[...] = jnp.where(sub == 0, ends, jnp.where(sub == 1, padded, cnt))
        carry_ref[...] = ends - padded

    @pl.when(phase == 1)
    def _():
        row = lax.broadcasted_iota(jnp.int32, (tb, tb), 0)
        col = lax.broadcasted_iota(jnp.int32, (tb, tb), 1)
        lower = jnp.where(col < row, 1.0, 0.0).astype(bf16)
        before = jnp.dot(lower, o.astype(bf16), preferred_element_type=f32) + carry_ref[0:1, :]
        d0 = jnp.sum(before * o0, axis=-1, keepdims=True)
        d1 = jnp.sum(before * o1, axis=-1, keepdims=True)
        dd = jnp.where(lane == 0, d0, jnp.where(lane == 1, d1, 0.0))
        dest_ref[0] = dd.T[:8, :].astype(jnp.int32)

    carry_ref[...] += jnp.sum(o, axis=0, keepdims=True)


def _dispatch_plan(route, bm):
    t = route.shape[0]
    tb = min(512, t)
    nb = t // tb
    return pl.pallas_call(
        functools.partial(_plan_kernel, bm=bm),
        grid=(2, nb),
        in_specs=[pl.BlockSpec((tb, LANES), lambda p, i: (i, 0))],
        out_specs=[pl.BlockSpec((1, 8, tb), lambda p, i: (i * p, 0, 0)),
                   pl.BlockSpec((8, LANES), lambda p, i: (0, 0))],
        out_shape=[jax.ShapeDtypeStruct((nb, 8, tb), jnp.int32), jax.ShapeDtypeStruct((8, LANES), f32)],
        scratch_shapes=[pltpu.VMEM((8, LANES), f32)],
        compiler_params=_cparams(("arbitrary", "arbitrary")),
        name="dispatch_plan",
    )(route)


def _dispatch_kernel(d0_ref, d1_ref, fill_ref, end_ref, nu_ref, hs_ref, xs_hbm, zbuf, sem, zsem,
                     *, tb, bm, n_blk, unroll=8):
    i = pl.program_id(0)
    base = i * tb
    blk_rows = bm * PITCH

    def pad_copy(s):
        return pltpu.make_async_copy(zbuf.at[pl.ds(0, PITCH), :], xs_hbm.at[pl.ds(s * PITCH, PITCH), :], zsem)

    def tail_copy(g):
        return pltpu.make_async_copy(zbuf, xs_hbm.at[pl.ds(pl.multiple_of(g * blk_rows, 8), blk_rows), :], zsem)

    @pl.when(i == 0)
    def _():
        zbuf[...] = jnp.zeros(zbuf.shape, u32)
        for op in ("start", "wait"):
            def per_expert(e, c, op=op):
                def one(s, c2):
                    getattr(pad_copy(s), op)()
                    return c2
                return lax.fori_loop(fill_ref[e], end_ref[e], one, c)

            def per_tail(g, c, op=op):
                getattr(tail_copy(g), op)()
                return c

            lax.fori_loop(0, N_EXPERTS, per_expert, 0)
            lax.fori_loop(nu_ref[0], n_blk, per_tail, 0)

    def body(g, c):
        for u in range(unroll):
            r = g * unroll + u
            src = hs_ref.at[pl.ds(r * PITCH, PITCH), :]
            for k, dref in enumerate((d0_ref, d1_ref)):
                d = dref[base + r]
                pltpu.make_async_copy(src, xs_hbm.at[pl.ds(d * PITCH, PITCH), :], sem).start(priority=k)
        return c

    lax.fori_loop(0, tb // unroll, body, 0)
    for _ in range(TOP_K):
        pltpu.make_async_copy(xs_hbm.at[pl.ds(0, tb * PITCH), :], xs_hbm.at[pl.ds(0, tb * PITCH), :], sem).wait()


def _dispatch_rows(dest0, dest1, fill_start, seg_end, n_used, hs, cap):
    t = dest0.shape[0]
    tb = min(1024, t)
    bm = MOE_BLOCK
    grid_spec = pltpu.PrefetchScalarGridSpec(
        num_scalar_prefetch=5,
        grid=(t // tb,),
        in_specs=[pl.BlockSpec((tb * PITCH, LANES), lambda i, *_: (i, 0))],
        out_specs=pl.BlockSpec(memory_space=pl.ANY),
        scratch_shapes=[pltpu.VMEM((bm * PITCH, LANES), u32), pltpu.SemaphoreType.DMA(()),
                        pltpu.SemaphoreType.DMA(())],
    )
    return pl.pallas_call(
        functools.partial(_dispatch_kernel, tb=tb, bm=bm, n_blk=cap // bm),
        grid_spec=grid_spec,
        out_shape=jax.ShapeDtypeStruct((cap * PITCH, LANES), u32),
        compiler_params=_cparams(("arbitrary",)),
        name="dispatch_rows",
    )(dest0, dest1, fill_start, seg_end, n_used, hs)


def _moe_kernel(g0_ref, nb_ref, nu_ref, x_hbm, w1_ref, w3_ref, w2_ref, y_hbm,
                xbuf, ybuf, w1b, w3b, w2b, gsem, ysem, *, bm, n_blk):
    e = pl.program_id(0)
    n_used = nu_ref[0]
    g0 = g0_ref[e]
    nb = nb_ref[e]
    blk_rows = bm * PITCH

    def x_copy(slot, g):
        src = x_hbm.at[pl.ds(pl.multiple_of(g * blk_rows, 8), blk_rows), :]
        return pltpu.make_async_copy(src, xbuf.at[slot], gsem.at[slot])

    def y_copy(slot, g):
        dst = y_hbm.at[pl.ds(pl.multiple_of(g * blk_rows, 8), blk_rows), :]
        return pltpu.make_async_copy(ybuf.at[slot], dst, ysem.at[slot])

    @pl.when(e == 0)
    def _():
        x_copy(0, 0).start()

    @pl.when(nb > 0)
    def _():
        w1b[...] = w1_ref[0].astype(bf16)
        w3b[...] = w3_ref[0].astype(bf16)
        w2b[...] = w2_ref[0].astype(bf16)

    def block(c, carry):
        g = g0 + c
        slot = g % 2

        @pl.when(g + 1 < n_used)
        def _():
            x_copy(1 - slot, g + 1).start()

        x_copy(slot, g).wait()

        hn = _slabs_to_rows(xbuf.at[slot], bm).astype(bf16)
        a = jnp.dot(hn, w1b[...], preferred_element_type=f32)
        b = jnp.dot(hn, w3b[...], preferred_element_type=f32)
        mid = (a * jax.nn.sigmoid(a) * b).astype(bf16)
        y = jnp.dot(mid, w2b[...], preferred_element_type=f32)

        @pl.when(g >= 2)
        def _():
            y_copy(slot, g - 2).wait()

        _rows_to_slabs(ybuf.at[slot], y, bm)
        y_copy(slot, g).start()
        return carry

    lax.fori_loop(0, nb, block, 0)

    @pl.when(e == pl.num_programs(0) - 1)
    def _():
        @pl.when(n_used >= 2)
        def _():
            y_copy(n_used % 2, n_used - 2).wait()

        y_copy((n_used - 1) % 2, n_used - 1).wait()
        ybuf[0] = jnp.zeros(ybuf.shape[1:], u32)

        def zstart(g, carry):
            y_copy(0, g).start()
            return carry

        def zwait(g, carry):
            y_copy(0, g).wait()
            return carry

        lax.fori_loop(n_used, n_blk, zstart, 0)
        lax.fori_loop(n_used, n_blk, zwait, 0)


def _expert_mlp(blk_start, blk_count, n_used, xs, w1, w3, w2, cap):
    d = D_MODEL
    bm = MOE_BLOCK
    n_blk = cap // bm
    by_expert = lambda shape: pl.BlockSpec(shape, lambda e, g0, nb, nu: (e, 0, 0))
    grid_spec = pltpu.PrefetchScalarGridSpec(
        num_scalar_prefetch=3,
        grid=(N_EXPERTS,),
        in_specs=[pl.BlockSpec(memory_space=pl.ANY),
                  by_expert((1, d, D_EXPERT)), by_expert((1, d, D_EXPERT)), by_expert((1, D_EXPERT, d))],
        out_specs=pl.BlockSpec(memory_space=pl.ANY),
        scratch_shapes=[pltpu.VMEM((2, bm * PITCH, LANES), u32), pltpu.VMEM((2, bm * PITCH, LANES), u32),
                        pltpu.VMEM((d, D_EXPERT), bf16), pltpu.VMEM((d, D_EXPERT), bf16),
                        pltpu.VMEM((D_EXPERT, d), bf16),
                        pltpu.SemaphoreType.DMA((2,)), pltpu.SemaphoreType.DMA((2,))],
    )
    return pl.pallas_call(
        functools.partial(_moe_kernel, bm=bm, n_blk=n_blk),
        grid_spec=grid_spec,
        out_shape=jax.ShapeDtypeStruct((cap * PITCH, LANES), u32),
        compiler_params=_cparams(("arbitrary",)),
        name="expert_mlp",
    )(blk_start, blk_count, n_used, xs, w1, w3, w2)


def _combine_kernel(d0_ref, d1_ref, h_ref, route_ref, y_hbm, nw_ref, o_ref, ybuf, sem, *, tb):
    i = pl.program_id(0)
    n = pl.num_programs(0)
    slot = i % 2

    def start(blk, s):
        _slab_gather_start(d0_ref, blk * tb, tb, y_hbm, ybuf.at[s, 0], sem.at[s])
        _slab_gather_start(d1_ref, blk * tb, tb, y_hbm, ybuf.at[s, 1], sem.at[s])

    @pl.when(i == 0)
    def _():
        start(0, 0)

    @pl.when(i + 1 < n)
    def _():
        start(i + 1, 1 - slot)

    for k in range(TOP_K):
        _slab_gather_wait(tb, y_hbm, ybuf.at[slot, k], sem.at[slot])
    r = route_ref[...]
    h = h_ref[...] + (r[:, 2:3] * _slabs_to_rows(ybuf.at[slot, 0], tb)
                      + r[:, 3:4] * _slabs_to_rows(ybuf.at[slot, 1], tb))
    o_ref[...] = h * lax.rsqrt(jnp.mean(h * h, axis=-1, keepdims=True) + EPS) * nw_ref[...]


def _combine(dest0, dest1, h, route, ys, norm_w):
    t, d = h.shape
    tb = min(256, t)
    grid_spec = pltpu.PrefetchScalarGridSpec(
        num_scalar_prefetch=2,
        grid=(t // tb,),
        in_specs=[pl.BlockSpec((tb, d), lambda i, a, b: (i, 0)),
                  pl.BlockSpec((tb, LANES), lambda i, a, b: (i, 0)),
                  pl.BlockSpec(memory_space=pl.ANY),
                  pl.BlockSpec((1, d), lambda i, a, b: (0, 0))],
        out_specs=pl.BlockSpec((tb, d), lambda i, a, b: (i, 0)),
        scratch_shapes=[pltpu.VMEM((2, TOP_K, tb * PITCH, LANES), u32), pltpu.SemaphoreType.DMA((2,))],
    )
    return pl.pallas_call(
        functools.partial(_combine_kernel, tb=tb),
        grid_spec=grid_spec,
        out_shape=jax.ShapeDtypeStruct((t, d), f32),
        compiler_params=_cparams(("arbitrary",)),
        name="combine_final_norm",
    )(dest0, dest1, h, route, ys, norm_w)


def kernel(x, positions, norm_mix_w, w_in, fox_b_f, ret_gn_w, w_branch, b_gate, w_out, norm_ffn_w, w_router_group,
           b_router_group, w_router_expert, b_router_expert, w1, w3, w2, norm_final_w):
    b, s, d = x.shape
    assert d == D_MODEL and norm_mix_w.shape[0] == 1
    t = b * s
    x2 = x.reshape(t, d)
    scale = 1.0 / math.sqrt(HEAD_DIM)

    wt = w_in[0].T
    o_f = 3 * FOX_WIDTH
    o_r = o_f + FOX_HEADS
    o_g = o_r + 4 * RET_WIDTH
    w_main = jnp.concatenate([wt[o_g:], wt[:FOX_WIDTH] * scale, wt[FOX_WIDTH:o_f],
                              wt[o_r:o_r + RET_WIDTH] * scale, wt[o_r + RET_WIDTH:o_g]], axis=0).astype(bf16)
    w_f = jnp.pad(wt[o_f:o_r], ((0, LANES - FOX_HEADS), (0, 0))).astype(bf16)
    bias_f = jnp.pad(fox_b_f[0], (0, LANES - FOX_HEADS)).reshape(1, LANES)
    w_r = jnp.concatenate([w_router_group[0], w_router_expert[0]], axis=1)
    w_r = jnp.pad(w_r, ((0, 0), (0, LANES - w_r.shape[1])))
    wr_hi = w_r.astype(bf16)
    wr_lo = (w_r - wr_hi.astype(f32)).astype(bf16)
    b_r = jnp.pad(jnp.concatenate([b_router_group[0], b_router_expert[0]]), (0, LANES - N_GROUPS - N_EXPERTS))
    b_r = b_r.reshape(1, LANES)

    cos, sin = _rope_tables(positions)
    proj, f_logit = _in_projection(x2, norm_mix_w, w_main, w_f)
    c = _forget_cumsum(f_logit, bias_f, b, s)
    fox_o = _fox_attention(proj, c.reshape(b * FOX_HEADS, 1, s), b, s)
    ret_o = _retention(proj, cos, sin, ret_gn_w, b, s)
    h, h_slabs, route = _merge(fox_o, ret_o, proj, b_gate[0], x2, w_branch[0].astype(bf16), w_out[0].astype(bf16),
                      norm_ffn_w, wr_hi, wr_lo, b_r)

    bm = MOE_BLOCK
    cap = t * TOP_K + N_EXPERTS * bm
    dest_rows, seg = _dispatch_plan(route, bm)
    dest0 = dest_rows[:, 0, :].reshape(t)
    dest1 = dest_rows[:, 1, :].reshape(t)
    pends = seg[0, :N_EXPERTS].astype(jnp.int32)
    padded = seg[1, :N_EXPERTS].astype(jnp.int32)
    counts = seg[2, :N_EXPERTS].astype(jnp.int32)
    pstarts = pends - padded
    n_used = pends[-1:] // bm

    xs = _dispatch_rows(dest0, dest1, pstarts + counts, pends, n_used, h_slabs, cap)
    y = _expert_mlp(pstarts // bm, padded // bm, n_used, xs, w1[0], w3[0], w2[0], cap)
    out = _combine(dest0, dest1, h, route, y, norm_final_w.reshape(1, d))
    return out.reshape(b, s, d)
```

```python
import functools
import math

import numpy as np
import jax
import jax.numpy as jnp
from jax import lax
from jax.experimental import pallas as pl
from jax.experimental.pallas import tpu as pltpu

D_MODEL = 2048
HEAD_DIM = 128
FOX_HEADS = 8
RET_HEADS = 8
FOX_WIDTH = FOX_HEADS * HEAD_DIM
RET_WIDTH = RET_HEADS * HEAD_DIM
N_BRANCH = 2
N_GROUPS = 4
EXPERTS_PER_GROUP = 8
N_EXPERTS = N_GROUPS * EXPERTS_PER_GROUP
TOP_K = 2
D_EXPERT = 512
ROPE_BASE = 10000.0
EPS = 1e-6

LANES = 128
NEG_BIG = -1e30
MAIN_WIDTH = 3 * FOX_WIDTH + 4 * RET_WIDTH + N_BRANCH * D_MODEL
COL_FQ = 0
COL_FK = COL_FQ + FOX_HEADS
COL_FV = COL_FK + FOX_HEADS
COL_RQ = COL_FV + FOX_HEADS
COL_RK = COL_RQ + RET_HEADS
COL_RV = COL_RK + RET_HEADS
COL_RG = COL_RV + RET_HEADS
COL_GATE = COL_RG + RET_HEADS
PROJ_TILE = 1024
FOX_TILES = 3 * FOX_WIDTH // PROJ_TILE
Q_TILES = (COL_FQ * HEAD_DIM // PROJ_TILE, COL_RQ * HEAD_DIM // PROJ_TILE)

ATTN_BLOCK = 256
RET_CHUNK = 256
MOE_BLOCK = 256
VMEM_LIMIT = 56 * 1024 * 1024

f32 = jnp.float32
bf16 = jnp.bfloat16
_NT = (((1,), (1,)), ((), ()))
SLAB = D_MODEL // (2 * LANES)
PITCH = SLAB + 1
u32 = jnp.uint32
_HI_MASK = np.uint32(0xFFFF0000)


def _rows_to_slabs(ref2, val, n_rows):
    bits = lambda v: lax.bitcast_convert_type(v.astype(bf16).astype(f32), u32)
    for j in range(SLAB):
        lo = val[:, (2 * j) * LANES:(2 * j + 1) * LANES]
        hi = val[:, (2 * j + 1) * LANES:(2 * j + 2) * LANES]
        ref2[pl.ds(j, n_rows, stride=PITCH), :] = (bits(lo) >> 16) | (bits(hi) & _HI_MASK)
    ref2[pl.ds(SLAB, n_rows, stride=PITCH), :] = jnp.zeros((n_rows, LANES), u32)


def _slabs_to_rows(ref2, n_rows):
    chunks = []
    for j in range(SLAB):
        w = ref2[pl.ds(j, n_rows, stride=PITCH), :]
        chunks.append(lax.bitcast_convert_type(w << 16, f32))
        chunks.append(lax.bitcast_convert_type(w & _HI_MASK, f32))
    return jnp.concatenate(chunks, axis=-1)


def _slab_gather_start(idx_ref, base, n_rows, src_hbm, dst, sem, unroll=8):
    def body(g, _):
        for u in range(unroll):
            r = g * unroll + u
            t = idx_ref[base + r]
            pltpu.make_async_copy(src_hbm.at[pl.ds(t * PITCH, SLAB), :], dst.at[pl.ds(r * PITCH, SLAB), :],
                                  sem).start(priority=u % 2)
        return 0
    lax.fori_loop(0, n_rows // unroll, body, 0)


def _slab_gather_wait(n_rows, src_hbm, dst, sem):
    pltpu.make_async_copy(src_hbm.at[pl.ds(0, n_rows * SLAB), :], dst.at[pl.ds(0, n_rows * SLAB), :], sem).wait()


def _cparams(sem, vmem=VMEM_LIMIT):
    return pltpu.CompilerParams(dimension_semantics=sem, vmem_limit_bytes=vmem)


def _rope_kernel(pos_ref, freq_ref, sgn_ref, cos_ref, sin_ref):
    ang = pos_ref[0].astype(f32) * freq_ref[...]
    cos_ref[0] = jnp.cos(ang)
    sin_ref[0] = jnp.sin(ang) * sgn_ref[...]


def _rope_tables(positions):
    b, s = positions.shape
    half = HEAD_DIM // 2
    inv = (np.float32(ROPE_BASE) ** (-np.arange(half, dtype=np.float32) / np.float32(half))).astype(np.float32)
    freq = jnp.asarray(np.concatenate([inv, inv])[None, :])
    sgn = jnp.asarray(np.concatenate([-np.ones(half, np.float32), np.ones(half, np.float32)])[None, :])
    out = jax.ShapeDtypeStruct((b, s, HEAD_DIM), f32)
    return pl.pallas_call(
        _rope_kernel,
        grid=(b,),
        in_specs=[pl.BlockSpec((1, s, 1), lambda i: (i, 0, 0)),
                  pl.BlockSpec((1, HEAD_DIM), lambda i: (0, 0)),
                  pl.BlockSpec((1, HEAD_DIM), lambda i: (0, 0))],
        out_specs=[pl.BlockSpec((1, s, HEAD_DIM), lambda i: (i, 0, 0)),
                   pl.BlockSpec((1, s, HEAD_DIM), lambda i: (i, 0, 0))],
        out_shape=[out, out],
        compiler_params=_cparams(("parallel",)),
        name="rope_tables",
    )(positions.reshape(b, s, 1), freq, sgn)


def _inproj_kernel(x_ref, nw_ref, wa_ref, wb_ref, wf_ref, o_ref, f_ref, xn_ref):
    j = pl.program_id(1)

    @pl.when(j == 0)
    def _():
        x = x_ref[...]
        ms = jnp.mean(x * x, axis=-1, keepdims=True)
        xn = (x * lax.rsqrt(ms + EPS) * nw_ref[...]).astype(bf16)
        xn_ref[...] = xn
        f_ref[...] = lax.dot_general(xn, wf_ref[...], _NT, preferred_element_type=f32)

    factor = jnp.where((j == Q_TILES[0]) | (j == Q_TILES[1]), 1.0 / math.sqrt(HEAD_DIM), 1.0).astype(f32)

    def project(w_ref):
        acc = lax.dot_general(xn_ref[...], w_ref[...], _NT, preferred_element_type=f32)
        o_ref[...] = (acc * factor).astype(o_ref.dtype)

    pl.when(j < FOX_TILES)(lambda: project(wa_ref))
    pl.when(j >= FOX_TILES)(lambda: project(wb_ref))


def _in_projection(x2, norm_w, w_a_t, w_b_t, w_f_t):
    t, d = x2.shape
    tn = PROJ_TILE
    n = w_a_t.shape[0] + w_b_t.shape[0]
    tm = min(1024, t)
    return pl.pallas_call(
        _inproj_kernel,
        grid=(t // tm, n // tn),
        in_specs=[pl.BlockSpec((tm, d), lambda i, j: (i, 0)),
                  pl.BlockSpec((1, d), lambda i, j: (0, 0)),
                  pl.BlockSpec((tn, d), lambda i, j: (jnp.minimum(j, FOX_TILES - 1), 0)),
                  pl.BlockSpec((tn, d), lambda i, j: (jnp.maximum(j - FOX_TILES, 0), 0)),
                  pl.BlockSpec((LANES, d), lambda i, j: (0, 0))],
        out_specs=[pl.BlockSpec((tm, tn), lambda i, j: (i, j)),
                   pl.BlockSpec((tm, LANES), lambda i, j: (i, 0))],
        out_shape=[jax.ShapeDtypeStruct((t, n), bf16), jax.ShapeDtypeStruct((t, LANES), f32)],
        scratch_shapes=[pltpu.VMEM((tm, d), bf16)],
        compiler_params=_cparams(("parallel", "arbitrary")),
        name="in_projection",
    )(x2, norm_w, w_a_t, w_b_t, w_f_t)


def _fcum_kernel(f_ref, b_ref, c_ref):
    z = f_ref[...] + b_ref[...]
    ls = jnp.minimum(z, 0.0) - jnp.log1p(jnp.exp(-jnp.abs(z)))
    c = ls.T[:FOX_HEADS, :]
    s = c.shape[1]
    lane = lax.broadcasted_iota(jnp.int32, c.shape, 1)
    sh = 1
    while sh < s:
        c = c + jnp.where(lane >= sh, pltpu.roll(c, sh, axis=1), 0.0)
        sh *= 2
    c_ref[0] = c


def _forget_cumsum(f_logit, bias, b, s):
    return pl.pallas_call(
        _fcum_kernel,
        grid=(b,),
        in_specs=[pl.BlockSpec((s, LANES), lambda i: (i, 0)),
                  pl.BlockSpec((1, LANES), lambda i: (0, 0))],
        out_specs=pl.BlockSpec((1, FOX_HEADS, s), lambda i: (i, 0, 0)),
        out_shape=jax.ShapeDtypeStruct((b, FOX_HEADS, s), f32),
        compiler_params=_cparams(("parallel",)),
        name="forget_cumsum",
    )(f_logit, bias)


def _fox_kernel(q_ref, k_ref, v_ref, c_ref, o_ref, *, tq):
    s_len = q_ref.shape[0]
    nq = s_len // tq
    nt = (((1,), (1,)), ((), ()))
    row = lax.broadcasted_iota(jnp.int32, (tq, tq), 0)
    col = lax.broadcasted_iota(jnp.int32, (tq, tq), 1)
    causal = row >= col
    for qi in range(nq):
        r0 = qi * tq
        q = q_ref[r0:r0 + tq, :]
        sd = lax.dot_general(q, k_ref[r0:r0 + tq, :], nt, preferred_element_type=f32) - c_ref[0, :, r0:r0 + tq]
        sd = jnp.where(causal, sd, NEG_BIG)
        m = jnp.max(sd, axis=-1, keepdims=True)
        if qi:
            so = lax.dot_general(q, k_ref[0:r0, :], nt, preferred_element_type=f32) - c_ref[0, :, 0:r0]
            m = jnp.maximum(m, jnp.max(so, axis=-1, keepdims=True))
        pd = jnp.exp(sd - m)
        l = jnp.sum(pd, axis=-1, keepdims=True)
        acc = jnp.dot(pd.astype(bf16), v_ref[r0:r0 + tq, :], preferred_element_type=f32)
        if qi:
            po = jnp.exp(so - m)
            l = l + jnp.sum(po, axis=-1, keepdims=True)
            acc = acc + jnp.dot(po.astype(bf16), v_ref[0:r0, :], preferred_element_type=f32)
        o_ref[r0:r0 + tq, :] = (acc / l).astype(o_ref.dtype)


def _fox_attention(proj, c_blk, b, s):
    tq = min(ATTN_BLOCK, s)
    h = FOX_HEADS
    return pl.pallas_call(
        functools.partial(_fox_kernel, tq=tq),
        grid=(b, h),
        in_specs=[pl.BlockSpec((s, HEAD_DIM), lambda i, j: (i, COL_FQ + j)),
                  pl.BlockSpec((s, HEAD_DIM), lambda i, j: (i, COL_FK + j)),
                  pl.BlockSpec((s, HEAD_DIM), lambda i, j: (i, COL_FV + j)),
                  pl.BlockSpec((1, 1, s), lambda i, j: (i * FOX_HEADS + j, 0, 0))],
        out_specs=pl.BlockSpec((s, HEAD_DIM), lambda i, j: (i, j)),
        out_shape=jax.ShapeDtypeStruct((b * s, FOX_WIDTH), bf16),
        compiler_params=_cparams(("parallel", "parallel")),
        name="fox_attention",
    )(proj, proj, proj, c_blk)


def _ret_kernel(q_ref, k_ref, v_ref, g_ref, cos_ref, sin_ref, dec_ref, xi_ref, zeta_ref, gc_ref, gnw_ref,
                o_ref, *, chunk):
    s_len = q_ref.shape[0]
    nc = s_len // chunk
    state = jnp.zeros((HEAD_DIM, HEAD_DIM), f32)
    for ci in range(nc):
        rows = slice(ci * chunk, (ci + 1) * chunk)
        cos = cos_ref[0, rows, :]
        sin = sin_ref[0, rows, :]

        def rot(t):
            t = t.astype(f32)
            return t * cos + pltpu.roll(t, HEAD_DIM // 2, axis=1) * sin

        q = rot(q_ref[rows, :])
        k = rot(k_ref[rows, :])
        v = v_ref[rows, :]
        qb = q.astype(bf16)
        inner = lax.dot_general(qb, k.astype(bf16), _NT, preferred_element_type=f32) * dec_ref[0]
        o = jnp.dot(inner.astype(bf16), v, preferred_element_type=f32)
        if ci:
            o = o + jnp.dot(qb, state.astype(bf16), preferred_element_type=f32) * xi_ref[0]
        if ci + 1 < nc:
            kz = (k * zeta_ref[0]).astype(bf16)
            kv = lax.dot_general(kz, v, (((0,), (0,)), ((), ())), preferred_element_type=f32)
            state = state * gc_ref[0] + kv if ci else kv
        mu = jnp.mean(o, axis=-1, keepdims=True)
        oc = o - mu
        var = jnp.mean(oc * oc, axis=-1, keepdims=True)
        y = oc * lax.rsqrt(var + EPS) * gnw_ref[...]
        g = g_ref[rows, :].astype(f32)
        o_ref[rows, :] = (g * jax.nn.sigmoid(g) * y).astype(o_ref.dtype)


def _retention_constants(chunk):
    hh = np.arange(RET_HEADS, dtype=np.float64)
    log_gamma = np.log1p(-(2.0 ** (-5.0 - hh)))
    n = np.arange(chunk, dtype=np.float64)
    diff = n[:, None] - n[None, :]
    decay = np.where(diff[None] >= 0, np.exp(diff[None] * log_gamma[:, None, None]), 0.0)
    xi = np.exp((n[None, :] + 1.0) * log_gamma[:, None])
    zeta = np.exp((chunk - 1.0 - n[None, :]) * log_gamma[:, None])
    g_chunk = np.exp(chunk * log_gamma)
    bc = lambda a: np.broadcast_to(a[:, :, None], (RET_HEADS, chunk, HEAD_DIM))
    gc = np.broadcast_to(g_chunk[:, None, None], (RET_HEADS, 1, HEAD_DIM))
    to = lambda a: jnp.asarray(np.ascontiguousarray(a), dtype=f32)
    return to(decay), to(bc(xi)), to(bc(zeta)), to(gc)


def _retention(proj, cos, sin, gn_w, b, s):
    chunk = min(RET_CHUNK, s)
    decay, xi, zeta, gc = _retention_constants(chunk)
    h = RET_HEADS
    head = lambda col: pl.BlockSpec((s, HEAD_DIM), lambda i, j: (i, col + j))
    per_head = lambda shape: pl.BlockSpec(shape, lambda i, j: (j, 0, 0))
    return pl.pallas_call(
        functools.partial(_ret_kernel, chunk=chunk),
        grid=(b, h),
        in_specs=[head(COL_RQ), head(COL_RK), head(COL_RV), head(COL_RG),
                  pl.BlockSpec((1, s, HEAD_DIM), lambda i, j: (i, 0, 0)),
                  pl.BlockSpec((1, s, HEAD_DIM), lambda i, j: (i, 0, 0)),
                  per_head((1, chunk, chunk)), per_head((1, chunk, HEAD_DIM)), per_head((1, chunk, HEAD_DIM)),
                  per_head((1, 1, HEAD_DIM)),
                  pl.BlockSpec((1, HEAD_DIM), lambda i, j: (0, j))],
        out_specs=pl.BlockSpec((s, HEAD_DIM), lambda i, j: (i, j)),
        out_shape=jax.ShapeDtypeStruct((b * s, RET_WIDTH), bf16),
        compiler_params=_cparams(("parallel", "parallel")),
        name="retention",
    )(proj, proj, proj, proj, cos, sin, decay, xi, zeta, gc, gn_w)


def _route(logits):
    lane = lax.broadcasted_iota(jnp.int32, logits.shape, 1).astype(f32)
    far = jnp.float32(1e9)
    gmask = lane < N_GROUPS
    gl = jnp.where(gmask, logits, NEG_BIG)
    gmax = jnp.max(gl, axis=-1, keepdims=True)
    gsel = jnp.min(jnp.where(gl == gmax, lane, far), axis=-1, keepdims=True)
    p_group = 1.0 / jnp.sum(jnp.where(gmask, jnp.exp(gl - gmax), 0.0), axis=-1, keepdims=True)
    lo = N_GROUPS + EXPERTS_PER_GROUP * gsel
    emask = (lane >= lo) & (lane < lo + EXPERTS_PER_GROUP)
    el = jnp.where(emask, logits, NEG_BIG)
    m1 = jnp.max(el, axis=-1, keepdims=True)
    i1 = jnp.min(jnp.where(el == m1, lane, far), axis=-1, keepdims=True)
    el2 = jnp.where(lane == i1, NEG_BIG, el)
    m2 = jnp.max(el2, axis=-1, keepdims=True)
    i2 = jnp.min(jnp.where(el2 == m2, lane, far), axis=-1, keepdims=True)
    t = jnp.exp(m2 - m1)
    w1 = p_group / (1.0 + t)
    w2 = p_group * t / (1.0 + t)
    out = jnp.where(lane == 0, i1 - N_GROUPS, 0.0)
    out = jnp.where(lane == 1, i2 - N_GROUPS, out)
    out = jnp.where(lane == 2, w1, out)
    out = jnp.where(lane == 3, w2, out)
    return out


def _merge_kernel(fo_ref, ro_ref, g0a_ref, g0b_ref, g1a_ref, g1b_ref, bg_ref, x_ref, wb_ref, wo_ref, nfw_ref,
                  wrh_ref, wrl_ref, br_ref, h_ref, hs_ref, route_ref, mg_ref, *, ncol):
    d = x_ref.shape[1]
    cw = d // ncol
    half = ncol // 2
    fo = fo_ref[...]
    ro = ro_ref[...]
    for c in range(ncol):
        cs = slice(c * cw, (c + 1) * cw)
        gs = slice((c % half) * cw, (c % half + 1) * cw)
        g0_ref, g1_ref = (g0a_ref, g1a_ref) if c < half else (g0b_ref, g1b_ref)
        bd0 = jnp.dot(fo, wb_ref[0, :, cs], preferred_element_type=f32)
        bd1 = jnp.dot(ro, wb_ref[1, :, cs], preferred_element_type=f32)
        ga = jax.nn.sigmoid(g0_ref[:, gs].astype(f32) + bg_ref[0:1, cs])
        gb = jax.nn.sigmoid(g1_ref[:, gs].astype(f32) + bg_ref[1:2, cs])
        mg_ref[:, cs] = (ga * bd0 + gb * bd1).astype(bf16)
    h = x_ref[...] + jnp.dot(mg_ref[...], wo_ref[...], preferred_element_type=f32)
    h_ref[...] = h
    hn = h * lax.rsqrt(jnp.mean(h * h, axis=-1, keepdims=True) + EPS) * nfw_ref[...]
    _rows_to_slabs(hs_ref, hn, hn.shape[0])
    hi = hn.astype(bf16)
    lo = (hn - hi.astype(f32)).astype(bf16)
    logits = (jnp.dot(hi, wrh_ref[...], preferred_element_type=f32)
              + jnp.dot(lo, wrh_ref[...], preferred_element_type=f32)
              + jnp.dot(hi, wrl_ref[...], preferred_element_type=f32)) + br_ref[...]
    route_ref[...] = _route(logits)


def _merge(fox_o, ret_o, proj, b_gate, x2, w_branch, w_out, nfw, wr_hi, wr_lo, b_router):
    t, d = x2.shape
    tm = min(256, t)
    gw = d // 2
    gate0 = COL_GATE * HEAD_DIM // gw
    gate = lambda k: pl.BlockSpec((tm, gw), lambda i: (i, gate0 + k))
    const = lambda shape: pl.BlockSpec(shape, lambda i: tuple(0 for _ in shape), pipeline_mode=pl.Buffered(1))
    return pl.pallas_call(
        functools.partial(_merge_kernel, ncol=4),
        grid=(t // tm,),
        in_specs=[pl.BlockSpec((tm, FOX_WIDTH), lambda i: (i, 0)),
                  pl.BlockSpec((tm, RET_WIDTH), lambda i: (i, 0)),
                  gate(0), gate(1), gate(2), gate(3),
                  const((N_BRANCH, d)),
                  pl.BlockSpec((tm, d), lambda i: (i, 0)),
                  const((N_BRANCH, FOX_WIDTH, d)),
                  const((d, d)),
                  const((1, d)),
                  const((d, LANES)), const((d, LANES)), const((1, LANES))],
        out_specs=[pl.BlockSpec((tm, d), lambda i: (i, 0)),
                   pl.BlockSpec((tm * PITCH, LANES), lambda i: (i, 0)),
                   pl.BlockSpec((tm, LANES), lambda i: (i, 0))],
        out_shape=[jax.ShapeDtypeStruct((t, d), f32), jax.ShapeDtypeStruct((t * PITCH, LANES), u32),
                   jax.ShapeDtypeStruct((t, LANES), f32)],
        scratch_shapes=[pltpu.VMEM((tm, d), bf16)],
        compiler_params=_cparams(("parallel",)),
        name="merge_outproj_router",
    )(fox_o, ret_o, proj, proj, proj, proj, b_gate, x2, w_branch, w_out, nfw, wr_hi, wr_lo, b_router)


def _plan_kernel(route_ref, dest_ref, ends_ref, carry_ref, *, bm):
    phase = pl.program_id(0)
    i = pl.program_id(1)
    r = route_ref[...]
    tb = r.shape[0]
    lane = lax.broadcasted_iota(jnp.int32, r.shape, 1).astype(f32)
    o0 = (lane == r[:, 0:1]).astype(f32)
    o1 = (lane == r[:, 1:2]).astype(f32)
    o = o0 + o1

    @pl.when((phase == 0) & (i == 0))
    def _():
        carry_ref[...] = jnp.zeros_like(carry_ref)

    @pl.when((phase == 1) & (i == 0))
    def _():
        cnt = carry_ref[...]
        padded = jnp.floor((cnt + (bm - 1)) * (1.0 / bm)) * bm
        lane8 = lax.broadcasted_iota(jnp.int32, cnt.shape, 1)
        ends = padded
        sh = 1
        while sh < LANES:
            ends = ends + jnp.where(lane8 >= sh, pltpu.roll(ends, sh, axis=1), 0.0)
            sh *= 2
        sub = lax.broadcasted_iota(jnp.int32, cnt.shape, 0)
        ends_ref[...] = jnp.where(sub == 0, ends, jnp.where(sub == 1, padded, cnt))
        carry_ref[...] = ends - padded

    @pl.when(phase == 1)
    def _():
        row = lax.broadcasted_iota(jnp.int32, (tb, tb), 0)
        col = lax.broadcasted_iota(jnp.int32, (tb, tb), 1)
        lower = jnp.where(col < row, 1.0, 0.0).astype(bf16)
        before = jnp.dot(lower, o.astype(bf16), preferred_element_type=f32) + carry_ref[0:1, :]
        d0 = jnp.sum(before * o0, axis=-1, keepdims=True)
        d1 = jnp.sum(before * o1, axis=-1, keepdims=True)
        dd = jnp.where(lane == 0, d0, jnp.where(lane == 1, d1, 0.0))
        dest_ref[0] = dd.T[:8, :].astype(jnp.int32)

    carry_ref[...] += jnp.sum(o, axis=0, keepdims=True)


def _dispatch_plan(route, bm):
    t = route.shape[0]
    tb = min(512, t)
    nb = t // tb
    return pl.pallas_call(
        functools.partial(_plan_kernel, bm=bm),
        grid=(2, nb),
        in_specs=[pl.BlockSpec((tb, LANES), lambda p, i: (i, 0))],
        out_specs=[pl.BlockSpec((1, 8, tb), lambda p, i: (i * p, 0, 0)),
                   pl.BlockSpec((8, LANES), lambda p, i: (0, 0))],
        out_shape=[jax.ShapeDtypeStruct((nb, 8, tb), jnp.int32), jax.ShapeDtypeStruct((8, LANES), f32)],
        scratch_shapes=[pltpu.VMEM((8, LANES), f32)],
        compiler_params=_cparams(("arbitrary", "arbitrary")),
        name="dispatch_plan",
    )(route)


def _dispatch_kernel(d0_ref, d1_ref, fill_ref, end_ref, nu_ref, hs_ref, xs_hbm, zbuf, sem, zsem,
                     *, tb, bm, n_blk, unroll=8):
    i = pl.program_id(0)
    base = i * tb
    blk_rows = bm * PITCH

    def pad_copy(s):
        return pltpu.make_async_copy(zbuf.at[pl.ds(0, PITCH), :], xs_hbm.at[pl.ds(s * PITCH, PITCH), :], zsem)

    def tail_copy(g):
        return pltpu.make_async_copy(zbuf, xs_hbm.at[pl.ds(pl.multiple_of(g * blk_rows, 8), blk_rows), :], zsem)

    @pl.when(i == 0)
    def _():
        zbuf[...] = jnp.zeros(zbuf.shape, u32)
        for op in ("start", "wait"):
            def per_expert(e, c, op=op):
                def one(s, c2):
                    getattr(pad_copy(s), op)()
                    return c2
                return lax.fori_loop(fill_ref[e], end_ref[e], one, c)

            def per_tail(g, c, op=op):
                getattr(tail_copy(g), op)()
                return c

            lax.fori_loop(0, N_EXPERTS, per_expert, 0)
            lax.fori_loop(nu_ref[0], n_blk, per_tail, 0)

    def body(g, c):
        for u in range(unroll):
            r = g * unroll + u
            src = hs_ref.at[pl.ds(r * PITCH, PITCH), :]
            for k, dref in enumerate((d0_ref, d1_ref)):
                d = dref[base + r]
                pltpu.make_async_copy(src, xs_hbm.at[pl.ds(d * PITCH, PITCH), :], sem).start(priority=k)
        return c

    lax.fori_loop(0, tb // unroll, body, 0)
    for _ in range(TOP_K):
        pltpu.make_async_copy(xs_hbm.at[pl.ds(0, tb * PITCH), :], xs_hbm.at[pl.ds(0, tb * PITCH), :], sem).wait()


def _dispatch_rows(dest0, dest1, fill_start, seg_end, n_used, hs, cap):
    t = dest0.shape[0]
    tb = min(1024, t)
    bm = MOE_BLOCK
    grid_spec = pltpu.PrefetchScalarGridSpec(
        num_scalar_prefetch=5,
        grid=(t // tb,),
        in_specs=[pl.BlockSpec((tb * PITCH, LANES), lambda i, *_: (i, 0))],
        out_specs=pl.BlockSpec(memory_space=pl.ANY),
        scratch_shapes=[pltpu.VMEM((bm * PITCH, LANES), u32), pltpu.SemaphoreType.DMA(()),
                        pltpu.SemaphoreType.DMA(())],
    )
    return pl.pallas_call(
        functools.partial(_dispatch_kernel, tb=tb, bm=bm, n_blk=cap // bm),
        grid_spec=grid_spec,
        out_shape=jax.ShapeDtypeStruct((cap * PITCH, LANES), u32),
        compiler_params=_cparams(("arbitrary",)),
        name="dispatch_rows",
    )(dest0, dest1, fill_start, seg_end, n_used, hs)


def _moe_kernel(g0_ref, nb_ref, nu_ref, x_hbm, w1_ref, w3_ref, w2_ref, y_hbm,
                xbuf, ybuf, w1b, w3b, w2b, gsem, ysem, *, bm, n_blk):
    e = pl.program_id(0)
    n_used = nu_ref[0]
    g0 = g0_ref[e]
    nb = nb_ref[e]
    blk_rows = bm * PITCH

    def x_copy(slot, g):
        src = x_hbm.at[pl.ds(pl.multiple_of(g * blk_rows, 8), blk_rows), :]
        return pltpu.make_async_copy(src, xbuf.at[slot], gsem.at[slot])

    def y_copy(slot, g):
        dst = y_hbm.at[pl.ds(pl.multiple_of(g * blk_rows, 8), blk_rows), :]
        return pltpu.make_async_copy(ybuf.at[slot], dst, ysem.at[slot])

    @pl.when(e == 0)
    def _():
        x_copy(0, 0).start(priority=1)

    @pl.when(nb > 0)
    def _():
        w1b[...] = w1_ref[0].astype(bf16)
        w3b[...] = w3_ref[0].astype(bf16)
        w2b[...] = w2_ref[0].astype(bf16)

    def block(c, carry):
        g = g0 + c
        slot = g % 2

        @pl.when(g + 1 < n_used)
        def _():
            x_copy(1 - slot, g + 1).start(priority=1)

        x_copy(slot, g).wait()

        hn = _slabs_to_rows(xbuf.at[slot], bm).astype(bf16)
        a = jnp.dot(hn, w1b[...], preferred_element_type=f32)
        b = jnp.dot(hn, w3b[...], preferred_element_type=f32)
        mid = (a * jax.nn.sigmoid(a) * b).astype(bf16)
        y = jnp.dot(mid, w2b[...], preferred_element_type=f32)

        @pl.when(g >= 2)
        def _():
            y_copy(slot, g - 2).wait()

        _rows_to_slabs(ybuf.at[slot], y, bm)
        y_copy(slot, g).start()
        return carry

    lax.fori_loop(0, nb, block, 0)

    @pl.when(e == pl.num_programs(0) - 1)
    def _():
        @pl.when(n_used >= 2)
        def _():
            y_copy(n_used % 2, n_used - 2).wait()

        y_copy((n_used - 1) % 2, n_used - 1).wait()
        ybuf[0] = jnp.zeros(ybuf.shape[1:], u32)

        def zstart(g, carry):
            y_copy(0, g).start()
            return carry

        def zwait(g, carry):
            y_copy(0, g).wait()
            return carry

        lax.fori_loop(n_used, n_blk, zstart, 0)
        lax.fori_loop(n_used, n_blk, zwait, 0)


def _expert_mlp(blk_start, blk_count, n_used, xs, w1, w3, w2, cap):
    d = D_MODEL
    bm = MOE_BLOCK
    n_blk = cap // bm
    by_expert = lambda shape: pl.BlockSpec(shape, lambda e, g0, nb, nu: (e, 0, 0))
    grid_spec = pltpu.PrefetchScalarGridSpec(
        num_scalar_prefetch=3,
        grid=(N_EXPERTS,),
        in_specs=[pl.BlockSpec(memory_space=pl.ANY),
                  by_expert((1, d, D_EXPERT)), by_expert((1, d, D_EXPERT)), by_expert((1, D_EXPERT, d))],
        out_specs=pl.BlockSpec(memory_space=pl.ANY),
        scratch_shapes=[pltpu.VMEM((2, bm * PITCH, LANES), u32), pltpu.VMEM((2, bm * PITCH, LANES), u32),
                        pltpu.VMEM((d, D_EXPERT), bf16), pltpu.VMEM((d, D_EXPERT), bf16),
                        pltpu.VMEM((D_EXPERT, d), bf16),
                        pltpu.SemaphoreType.DMA((2,)), pltpu.SemaphoreType.DMA((2,))],
    )
    return pl.pallas_call(
        functools.partial(_moe_kernel, bm=bm, n_blk=n_blk),
        grid_spec=grid_spec,
        out_shape=jax.ShapeDtypeStruct((cap * PITCH, LANES), u32),
        compiler_params=_cparams(("arbitrary",)),
        name="expert_mlp",
    )(blk_start, blk_count, n_used, xs, w1, w3, w2)


def _combine_kernel(d0_ref, d1_ref, h_ref, route_ref, y_hbm, nw_ref, o_ref, ybuf, sem, *, tb):
    i = pl.program_id(0)
    n = pl.num_programs(0)
    slot = i % 2

    def start(blk, s):
        _slab_gather_start(d0_ref, blk * tb, tb, y_hbm, ybuf.at[s, 0], sem.at[s])
        _slab_gather_start(d1_ref, blk * tb, tb, y_hbm, ybuf.at[s, 1], sem.at[s])

    @pl.when(i == 0)
    def _():
        start(0, 0)

    @pl.when(i + 1 < n)
    def _():
        start(i + 1, 1 - slot)

    for k in range(TOP_K):
        _slab_gather_wait(tb, y_hbm, ybuf.at[slot, k], sem.at[slot])
    r = route_ref[...]
    h = h_ref[...] + (r[:, 2:3] * _slabs_to_rows(ybuf.at[slot, 0], tb)
                      + r[:, 3:4] * _slabs_to_rows(ybuf.at[slot, 1], tb))
    o_ref[...] = h * lax.rsqrt(jnp.mean(h * h, axis=-1, keepdims=True) + EPS) * nw_ref[...]


def _combine(dest0, dest1, h, route, ys, norm_w):
    t, d = h.shape
    tb = min(256, t)
    grid_spec = pltpu.PrefetchScalarGridSpec(
        num_scalar_prefetch=2,
        grid=(t // tb,),
        in_specs=[pl.BlockSpec((tb, d), lambda i, a, b: (i, 0)),
                  pl.BlockSpec((tb, LANES), lambda i, a, b: (i, 0)),
                  pl.BlockSpec(memory_space=pl.ANY),
                  pl.BlockSpec((1, d), lambda i, a, b: (0, 0))],
        out_specs=pl.BlockSpec((tb, d), lambda i, a, b: (i, 0)),
        scratch_shapes=[pltpu.VMEM((2, TOP_K, tb * PITCH, LANES), u32), pltpu.SemaphoreType.DMA((2,))],
    )
    return pl.pallas_call(
        functools.partial(_combine_kernel, tb=tb),
        grid_spec=grid_spec,
        out_shape=jax.ShapeDtypeStruct((t, d), f32),
        compiler_params=_cparams(("arbitrary",)),
        name="combine_final_norm",
    )(dest0, dest1, h, route, ys, norm_w)


def kernel(x, positions, norm_mix_w, w_in, fox_b_f, ret_gn_w, w_branch, b_gate, w_out, norm_ffn_w, w_router_group,
           b_router_group, w_router_expert, b_router_expert, w1, w3, w2, norm_final_w):
    b, s, d = x.shape
    assert d == D_MODEL and norm_mix_w.shape[0] == 1
    t = b * s
    x2 = x.reshape(t, d)

    wt = w_in[0].T
    o_f = 3 * FOX_WIDTH
    o_r = o_f + FOX_HEADS
    w_a = wt[:o_f].astype(bf16)
    w_b = wt[o_r:].astype(bf16)
    w_f = jnp.pad(wt[o_f:o_r], ((0, LANES - FOX_HEADS), (0, 0))).astype(bf16)
    bias_f = jnp.pad(fox_b_f[0], (0, LANES - FOX_HEADS)).reshape(1, LANES)
    w_r = jnp.concatenate([w_router_group[0], w_router_expert[0]], axis=1)
    w_r = jnp.pad(w_r, ((0, 0), (0, LANES - w_r.shape[1])))
    wr_hi = w_r.astype(bf16)
    wr_lo = (w_r - wr_hi.astype(f32)).astype(bf16)
    b_r = jnp.pad(jnp.concatenate([b_router_group[0], b_router_expert[0]]), (0, LANES - N_GROUPS - N_EXPERTS))
    b_r = b_r.reshape(1, LANES)

    cos, sin = _rope_tables(positions)
    proj, f_logit = _in_projection(x2, norm_mix_w, w_a, w_b, w_f)
    c = _forget_cumsum(f_logit, bias_f, b, s)
    fox_o = _fox_attention(proj, c.reshape(b * FOX_HEADS, 1, s), b, s)
    ret_o = _retention(proj, cos, sin, ret_gn_w, b, s)
    h, h_slabs, route = _merge(fox_o, ret_o, proj, b_gate[0], x2, w_branch[0].astype(bf16), w_out[0].astype(bf16),
                      norm_ffn_w, wr_hi, wr_lo, b_r)

    bm = MOE_BLOCK
    cap = t * TOP_K + N_EXPERTS * bm
    dest_rows, seg = _dispatch_plan(route, bm)
    dest0 = dest_rows[:, 0, :].reshape(t)
    dest1 = dest_rows[:, 1, :].reshape(t)
    pends = seg[0, :N_EXPERTS].astype(jnp.int32)
    padded = seg[1, :N_EXPERTS].astype(jnp.int32)
    counts = seg[2, :N_EXPERTS].astype(jnp.int32)
    pstarts = pends - padded
    n_used = pends[-1:] // bm

    xs = _dispatch_rows(dest0, dest1, pstarts + counts, pends, n_used, h_slabs, cap)
    y = _expert_mlp(pstarts // bm, padded // bm, n_used, xs, w1[0], w3[0], w2[0], cap)
    out = _combine(dest0, dest1, h, route, y, norm_final_w.reshape(1, d))
    return out.reshape(b, s, d)
```

```python
import functools
import math

import numpy as np
import jax
import jax.numpy as jnp
from jax import lax
from jax.experimental import pallas as pl
from jax.experimental.pallas import tpu as pltpu

D_MODEL = 2048
HEAD_DIM = 128
FOX_HEADS = 8
RET_HEADS = 8
FOX_WIDTH = FOX_HEADS * HEAD_DIM
RET_WIDTH = RET_HEADS * HEAD_DIM
N_BRANCH = 2
N_GROUPS = 4
EXPERTS_PER_GROUP = 8
N_EXPERTS = N_GROUPS * EXPERTS_PER_GROUP
TOP_K = 2
D_EXPERT = 512
ROPE_BASE = 10000.0
EPS = 1e-6

LANES = 128
NEG_BIG = -1e30
MAIN_WIDTH = 3 * FOX_WIDTH + 4 * RET_WIDTH + N_BRANCH * D_MODEL
COL_GATE = 0
COL_FQ = N_BRANCH * D_MODEL // HEAD_DIM
COL_FK = COL_FQ + FOX_HEADS
COL_FV = COL_FK + FOX_HEADS
COL_RQ = COL_FV + FOX_HEADS
COL_RK = COL_RQ + RET_HEADS
COL_RV = COL_RK + RET_HEADS
COL_RG = COL_RV + RET_HEADS

ATTN_BLOCK = 256
RET_CHUNK = 256
MOE_BLOCK = 256
VMEM_LIMIT = 56 * 1024 * 1024

f32 = jnp.float32
bf16 = jnp.bfloat16
_NT = (((1,), (1,)), ((), ()))
SLAB = D_MODEL // (2 * LANES)
PITCH = SLAB + 1
u32 = jnp.uint32
_HI_MASK = np.uint32(0xFFFF0000)


def _rows_to_slabs(ref2, val, n_rows):
    bits = lambda v: lax.bitcast_convert_type(v.astype(bf16).astype(f32), u32)
    for j in range(SLAB):
        lo = val[:, (2 * j) * LANES:(2 * j + 1) * LANES]
        hi = val[:, (2 * j + 1) * LANES:(2 * j + 2) * LANES]
        ref2[pl.ds(j, n_rows, stride=PITCH), :] = (bits(lo) >> 16) | (bits(hi) & _HI_MASK)
    ref2[pl.ds(SLAB, n_rows, stride=PITCH), :] = jnp.zeros((n_rows, LANES), u32)


def _slabs_to_rows(ref2, n_rows):
    chunks = []
    for j in range(SLAB):
        w = ref2[pl.ds(j, n_rows, stride=PITCH), :]
        chunks.append(lax.bitcast_convert_type(w << 16, f32))
        chunks.append(lax.bitcast_convert_type(w & _HI_MASK, f32))
    return jnp.concatenate(chunks, axis=-1)


def _slab_gather_start(idx_ref, base, n_rows, src_hbm, dst, sem, unroll=8):
    def body(g, _):
        for u in range(unroll):
            r = g * unroll + u
            t = idx_ref[base + r]
            pltpu.make_async_copy(src_hbm.at[pl.ds(t * PITCH, SLAB), :], dst.at[pl.ds(r * PITCH, SLAB), :],
                                  sem).start(priority=u % 2)
        return 0
    lax.fori_loop(0, n_rows // unroll, body, 0)


def _slab_gather_wait(n_rows, src_hbm, dst, sem):
    pltpu.make_async_copy(src_hbm.at[pl.ds(0, n_rows * SLAB), :], dst.at[pl.ds(0, n_rows * SLAB), :], sem).wait()


def _cparams(sem, vmem=VMEM_LIMIT):
    return pltpu.CompilerParams(dimension_semantics=sem, vmem_limit_bytes=vmem)


def _const_spec(shape):
    return pl.BlockSpec(shape, lambda i: tuple(0 for _ in shape), pipeline_mode=pl.Buffered(1))


def _rope_kernel(pos_ref, freq_ref, sgn_ref, cos_ref, sin_ref):
    ang = pos_ref[0].astype(f32) * freq_ref[...]
    cos_ref[0] = jnp.cos(ang)
    sin_ref[0] = jnp.sin(ang) * sgn_ref[...]


def _rope_tables(positions):
    b, s = positions.shape
    half = HEAD_DIM // 2
    inv = (np.float32(ROPE_BASE) ** (-np.arange(half, dtype=np.float32) / np.float32(half))).astype(np.float32)
    freq = jnp.asarray(np.concatenate([inv, inv])[None, :])
    sgn = jnp.asarray(np.concatenate([-np.ones(half, np.float32), np.ones(half, np.float32)])[None, :])
    out = jax.ShapeDtypeStruct((b, s, HEAD_DIM), f32)
    return pl.pallas_call(
        _rope_kernel,
        grid=(b,),
        in_specs=[pl.BlockSpec((1, s, 1), lambda i: (i, 0, 0)),
                  pl.BlockSpec((1, HEAD_DIM), lambda i: (0, 0)),
                  pl.BlockSpec((1, HEAD_DIM), lambda i: (0, 0))],
        out_specs=[pl.BlockSpec((1, s, HEAD_DIM), lambda i: (i, 0, 0)),
                   pl.BlockSpec((1, s, HEAD_DIM), lambda i: (i, 0, 0))],
        out_shape=[out, out],
        compiler_params=_cparams(("parallel",)),
        name="rope_tables",
    )(positions.reshape(b, s, 1), freq, sgn)


def _inproj_kernel(x_ref, nw_ref, w_ref, wf_ref, o_ref, f_ref, xn_ref):
    @pl.when(pl.program_id(1) == 0)
    def _():
        x = x_ref[...]
        ms = jnp.mean(x * x, axis=-1, keepdims=True)
        xn = (x * lax.rsqrt(ms + EPS) * nw_ref[...]).astype(bf16)
        xn_ref[...] = xn
        f_ref[...] = lax.dot_general(xn, wf_ref[...], _NT, preferred_element_type=f32)

    o_ref[...] = lax.dot_general(xn_ref[...], w_ref[...], _NT, preferred_element_type=f32).astype(o_ref.dtype)


def _in_projection(x2, norm_w, w_main_t, w_f_t):
    t, d = x2.shape
    n = w_main_t.shape[0]
    tm = min(1024, t)
    tn = 1024
    return pl.pallas_call(
        _inproj_kernel,
        grid=(t // tm, n // tn),
        in_specs=[pl.BlockSpec((tm, d), lambda i, j: (i, 0)),
                  pl.BlockSpec((1, d), lambda i, j: (0, 0)),
                  pl.BlockSpec((tn, d), lambda i, j: (j, 0)),
                  pl.BlockSpec((LANES, d), lambda i, j: (0, 0))],
        out_specs=[pl.BlockSpec((tm, tn), lambda i, j: (i, j)),
                   pl.BlockSpec((tm, LANES), lambda i, j: (i, 0))],
        out_shape=[jax.ShapeDtypeStruct((t, n), bf16), jax.ShapeDtypeStruct((t, LANES), f32)],
        scratch_shapes=[pltpu.VMEM((tm, d), bf16)],
        compiler_params=_cparams(("parallel", "arbitrary")),
        name="in_projection",
    )(x2, norm_w, w_main_t, w_f_t)


def _fcum_kernel(f_ref, b_ref, c_ref):
    z = f_ref[...] + b_ref[...]
    ls = jnp.minimum(z, 0.0) - jnp.log1p(jnp.exp(-jnp.abs(z)))
    c = ls.T[:FOX_HEADS, :]
    s = c.shape[1]
    lane = lax.broadcasted_iota(jnp.int32, c.shape, 1)
    sh = 1
    while sh < s:
        c = c + jnp.where(lane >= sh, pltpu.roll(c, sh, axis=1), 0.0)
        sh *= 2
    c_ref[0] = c


def _forget_cumsum(f_logit, bias, b, s):
    return pl.pallas_call(
        _fcum_kernel,
        grid=(b,),
        in_specs=[pl.BlockSpec((s, LANES), lambda i: (i, 0)),
                  pl.BlockSpec((1, LANES), lambda i: (0, 0))],
        out_specs=pl.BlockSpec((1, FOX_HEADS, s), lambda i: (i, 0, 0)),
        out_shape=jax.ShapeDtypeStruct((b, FOX_HEADS, s), f32),
        compiler_params=_cparams(("parallel",)),
        name="forget_cumsum",
    )(f_logit, bias)


def _fox_kernel(q_ref, k_ref, v_ref, c_ref, o_ref, *, tq):
    s_len = q_ref.shape[0]
    nq = s_len // tq
    row = lax.broadcasted_iota(jnp.int32, (tq, tq), 0)
    col = lax.broadcasted_iota(jnp.int32, (tq, tq), 1)
    causal = row >= col
    for qi in range(nq):
        r0 = qi * tq
        q = q_ref[r0:r0 + tq, :]
        sd = lax.dot_general(q, k_ref[r0:r0 + tq, :], _NT, preferred_element_type=f32) - c_ref[0, :, r0:r0 + tq]
        sd = jnp.where(causal, sd, NEG_BIG)
        m = jnp.max(sd, axis=-1, keepdims=True)
        if qi:
            so = lax.dot_general(q, k_ref[0:r0, :], _NT, preferred_element_type=f32) - c_ref[0, :, 0:r0]
            m = jnp.maximum(m, jnp.max(so, axis=-1, keepdims=True))
        pd = jnp.exp(sd - m)
        l = jnp.sum(pd, axis=-1, keepdims=True)
        acc = jnp.dot(pd.astype(bf16), v_ref[r0:r0 + tq, :], preferred_element_type=f32)
        if qi:
            po = jnp.exp(so - m)
            l = l + jnp.sum(po, axis=-1, keepdims=True)
            acc = acc + jnp.dot(po.astype(bf16), v_ref[0:r0, :], preferred_element_type=f32)
        o_ref[r0:r0 + tq, :] = (acc / l).astype(o_ref.dtype)


def _fox_attention(proj, c_blk, b, s):
    tq = min(ATTN_BLOCK, s)
    h = FOX_HEADS
    return pl.pallas_call(
        functools.partial(_fox_kernel, tq=tq),
        grid=(b, h),
        in_specs=[pl.BlockSpec((s, HEAD_DIM), lambda i, j: (i, COL_FQ + j)),
                  pl.BlockSpec((s, HEAD_DIM), lambda i, j: (i, COL_FK + j)),
                  pl.BlockSpec((s, HEAD_DIM), lambda i, j: (i, COL_FV + j)),
                  pl.BlockSpec((1, 1, s), lambda i, j: (i * FOX_HEADS + j, 0, 0))],
        out_specs=pl.BlockSpec((s, HEAD_DIM), lambda i, j: (i, j)),
        out_shape=jax.ShapeDtypeStruct((b * s, FOX_WIDTH), bf16),
        compiler_params=_cparams(("parallel", "parallel")),
        name="fox_attention",
    )(proj, proj, proj, c_blk)


def _ret_kernel(q_ref, k_ref, v_ref, g_ref, cos_ref, sin_ref, dec_ref, xi_ref, zeta_ref, gc_ref, gnw_ref,
                o_ref, *, chunk):
    s_len = q_ref.shape[0]
    nc = s_len // chunk
    state = jnp.zeros((HEAD_DIM, HEAD_DIM), f32)
    for ci in range(nc):
        rows = slice(ci * chunk, (ci + 1) * chunk)
        cos = cos_ref[0, rows, :]
        sin = sin_ref[0, rows, :]

        def rot(t):
            t = t.astype(f32)
            return t * cos + pltpu.roll(t, HEAD_DIM // 2, axis=1) * sin

        q = rot(q_ref[rows, :])
        k = rot(k_ref[rows, :])
        v = v_ref[rows, :]
        qb = q.astype(bf16)
        inner = lax.dot_general(qb, k.astype(bf16), _NT, preferred_element_type=f32) * dec_ref[0]
        o = jnp.dot(inner.astype(bf16), v, preferred_element_type=f32)
        if ci:
            o = o + jnp.dot(qb, state.astype(bf16), preferred_element_type=f32) * xi_ref[0]
        if ci + 1 < nc:
            kz = (k * zeta_ref[0]).astype(bf16)
            kv = lax.dot_general(kz, v, (((0,), (0,)), ((), ())), preferred_element_type=f32)
            state = state * gc_ref[0] + kv if ci else kv
        mu = jnp.mean(o, axis=-1, keepdims=True)
        oc = o - mu
        var = jnp.mean(oc * oc, axis=-1, keepdims=True)
        y = oc * lax.rsqrt(var + EPS) * gnw_ref[...]
        g = g_ref[rows, :].astype(f32)
        o_ref[rows, :] = (g * jax.nn.sigmoid(g) * y).astype(o_ref.dtype)


def _retention_constants(chunk):
    hh = np.arange(RET_HEADS, dtype=np.float64)
    log_gamma = np.log1p(-(2.0 ** (-5.0 - hh)))
    n = np.arange(chunk, dtype=np.float64)
    diff = n[:, None] - n[None, :]
    decay = np.where(diff[None] >= 0, np.exp(diff[None] * log_gamma[:, None, None]), 0.0)
    xi = np.exp((n[None, :] + 1.0) * log_gamma[:, None])
    zeta = np.exp((chunk - 1.0 - n[None, :]) * log_gamma[:, None])
    g_chunk = np.exp(chunk * log_gamma)
    bc = lambda a: np.broadcast_to(a[:, :, None], (RET_HEADS, chunk, HEAD_DIM))
    gc = np.broadcast_to(g_chunk[:, None, None], (RET_HEADS, 1, HEAD_DIM))
    to = lambda a: jnp.asarray(np.ascontiguousarray(a), dtype=f32)
    return to(decay), to(bc(xi)), to(bc(zeta)), to(gc)


def _retention(proj, cos, sin, gn_w, b, s):
    chunk = min(RET_CHUNK, s)
    decay, xi, zeta, gc = _retention_constants(chunk)
    h = RET_HEADS
    head = lambda col: pl.BlockSpec((s, HEAD_DIM), lambda i, j: (i, col + j))
    per_head = lambda shape: pl.BlockSpec(shape, lambda i, j: (j, 0, 0))
    return pl.pallas_call(
        functools.partial(_ret_kernel, chunk=chunk),
        grid=(b, h),
        in_specs=[head(COL_RQ), head(COL_RK), head(COL_RV), head(COL_RG),
                  pl.BlockSpec((1, s, HEAD_DIM), lambda i, j: (i, 0, 0)),
                  pl.BlockSpec((1, s, HEAD_DIM), lambda i, j: (i, 0, 0)),
                  per_head((1, chunk, chunk)), per_head((1, chunk, HEAD_DIM)), per_head((1, chunk, HEAD_DIM)),
                  per_head((1, 1, HEAD_DIM)),
                  pl.BlockSpec((1, HEAD_DIM), lambda i, j: (0, j))],
        out_specs=pl.BlockSpec((s, HEAD_DIM), lambda i, j: (i, j)),
        out_shape=jax.ShapeDtypeStruct((b * s, RET_WIDTH), bf16),
        compiler_params=_cparams(("parallel", "parallel")),
        name="retention",
    )(proj, proj, proj, proj, cos, sin, decay, xi, zeta, gc, gn_w)


def _branch_merge_kernel(fo_ref, ro_ref, g0_ref, g1_ref, bg_ref, wb_ref, mg_ref, *, ncol):
    d = mg_ref.shape[1]
    cw = d // ncol
    fo = fo_ref[...]
    ro = ro_ref[...]
    for c in range(ncol):
        cs = slice(c * cw, (c + 1) * cw)
        bd0 = jnp.dot(fo, wb_ref[0, :, cs], preferred_element_type=f32)
        bd1 = jnp.dot(ro, wb_ref[1, :, cs], preferred_element_type=f32)
        ga = jax.nn.sigmoid(g0_ref[:, cs].astype(f32) + bg_ref[0:1, cs])
        gb = jax.nn.sigmoid(g1_ref[:, cs].astype(f32) + bg_ref[1:2, cs])
        mg_ref[:, cs] = (ga * bd0 + gb * bd1).astype(mg_ref.dtype)


def _branch_merge(fox_o, ret_o, proj, b_gate, w_branch):
    t = fox_o.shape[0]
    d = D_MODEL
    tm = min(1024, t)
    gate0 = COL_GATE * HEAD_DIM // d
    return pl.pallas_call(
        functools.partial(_branch_merge_kernel, ncol=4),
        grid=(t // tm,),
        in_specs=[pl.BlockSpec((tm, FOX_WIDTH), lambda i: (i, 0)),
                  pl.BlockSpec((tm, RET_WIDTH), lambda i: (i, 0)),
                  pl.BlockSpec((tm, d), lambda i: (i, gate0)),
                  pl.BlockSpec((tm, d), lambda i: (i, gate0 + 1)),
                  _const_spec((N_BRANCH, d)),
                  _const_spec((N_BRANCH, FOX_WIDTH, d))],
        out_specs=pl.BlockSpec((tm, d), lambda i: (i, 0)),
        out_shape=jax.ShapeDtypeStruct((t, d), bf16),
        compiler_params=_cparams(("parallel",)),
        name="branch_merge",
    )(fox_o, ret_o, proj, proj, b_gate, w_branch)


def _route(logits):
    lane = lax.broadcasted_iota(jnp.int32, logits.shape, 1).astype(f32)
    far = jnp.float32(1e9)
    gmask = lane < N_GROUPS
    gl = jnp.where(gmask, logits, NEG_BIG)
    gmax = jnp.max(gl, axis=-1, keepdims=True)
    gsel = jnp.min(jnp.where(gl == gmax, lane, far), axis=-1, keepdims=True)
    p_group = 1.0 / jnp.sum(jnp.where(gmask, jnp.exp(gl - gmax), 0.0), axis=-1, keepdims=True)
    lo = N_GROUPS + EXPERTS_PER_GROUP * gsel
    emask = (lane >= lo) & (lane < lo + EXPERTS_PER_GROUP)
    el = jnp.where(emask, logits, NEG_BIG)
    m1 = jnp.max(el, axis=-1, keepdims=True)
    i1 = jnp.min(jnp.where(el == m1, lane, far), axis=-1, keepdims=True)
    el2 = jnp.where(lane == i1, NEG_BIG, el)
    m2 = jnp.max(el2, axis=-1, keepdims=True)
    i2 = jnp.min(jnp.where(el2 == m2, lane, far), axis=-1, keepdims=True)
    t = jnp.exp(m2 - m1)
    w1 = p_group / (1.0 + t)
    w2 = p_group * t / (1.0 + t)
    out = jnp.where(lane == 0, i1 - N_GROUPS, 0.0)
    out = jnp.where(lane == 1, i2 - N_GROUPS, out)
    out = jnp.where(lane == 2, w1, out)
    out = jnp.where(lane == 3, w2, out)
    return out


def _outproj_kernel(mg_ref, x_ref, wo_ref, nfw_ref, wr_ref, br_ref, h_ref, hs_ref, route_ref):
    h = x_ref[...] + jnp.dot(mg_ref[...], wo_ref[...], preferred_element_type=f32)
    h_ref[...] = h
    hn = h * lax.rsqrt(jnp.mean(h * h, axis=-1, keepdims=True) + EPS) * nfw_ref[...]
    _rows_to_slabs(hs_ref, hn, hn.shape[0])
    hi = hn.astype(bf16)
    lo = (hn - hi.astype(f32)).astype(bf16)
    hw = jnp.dot(hi, wr_ref[...], preferred_element_type=f32)
    logits = (hw[:, :LANES] + jnp.dot(lo, wr_ref[:, :LANES], preferred_element_type=f32) + hw[:, LANES:]) + br_ref[...]
    route_ref[...] = _route(logits)


def _outproj_router(merged, x2, w_out, nfw, wr_cat, b_router):
    t, d = x2.shape
    tm = min(512, t)
    return pl.pallas_call(
        _outproj_kernel,
        grid=(t // tm,),
        in_specs=[pl.BlockSpec((tm, d), lambda i: (i, 0)),
                  pl.BlockSpec((tm, d), lambda i: (i, 0)),
                  _const_spec((d, d)),
                  _const_spec((1, d)),
                  _const_spec((d, 2 * LANES)), _const_spec((1, LANES))],
        out_specs=[pl.BlockSpec((tm, d), lambda i: (i, 0)),
                   pl.BlockSpec((tm * PITCH, LANES), lambda i: (i, 0)),
                   pl.BlockSpec((tm, LANES), lambda i: (i, 0))],
        out_shape=[jax.ShapeDtypeStruct((t, d), f32), jax.ShapeDtypeStruct((t * PITCH, LANES), u32),
                   jax.ShapeDtypeStruct((t, LANES), f32)],
        compiler_params=_cparams(("parallel",)),
        name="outproj_router",
    )(merged, x2, w_out, nfw, wr_cat, b_router)


def _plan_kernel(route_ref, dest_ref, ends_ref, carry_ref, *, bm):
    phase = pl.program_id(0)
    i = pl.program_id(1)
    r = route_ref[...]
    tb = r.shape[0]
    lane = lax.broadcasted_iota(jnp.int32, r.shape, 1).astype(f32)
    o0 = (lane == r[:, 0:1]).astype(f32)
    o1 = (lane == r[:, 1:2]).astype(f32)
    o = o0 + o1

    @pl.when((phase == 0) & (i == 0))
    def _():
        carry_ref[...] = jnp.zeros_like(carry_ref)

    @pl.when((phase == 1) & (i == 0))
    def _():
        cnt = carry_ref[...]
        padded = jnp.floor((cnt + (bm - 1)) * (1.0 / bm)) * bm
        lane8 = lax.broadcasted_iota(jnp.int32, cnt.shape, 1)
        ends = padded
        sh = 1
        while sh < LANES:
            ends = ends + jnp.where(lane8 >= sh, pltpu.roll(ends, sh, axis=1), 0.0)
            sh *= 2
        sub = lax.broadcasted_iota(jnp.int32, cnt.shape, 0)
        ends_ref[...] = jnp.where(sub == 0, ends, jnp.where(sub == 1, padded, cnt))
        carry_ref[...] = ends - padded

    @pl.when(phase == 1)
    def _():
        row = lax.broadcasted_iota(jnp.int32, (tb, tb), 0)
        col = lax.broadcasted_iota(jnp.int32, (tb, tb), 1)
        lower = jnp.where(col < row, 1.0, 0.0).astype(bf16)
        before = jnp.dot(lower, o.astype(bf16), preferred_element_type=f32) + carry_ref[0:1, :]
        d0 = jnp.sum(before * o0, axis=-1, keepdims=True)
        d1 = jnp.sum(before * o1, axis=-1, keepdims=True)
        dd = jnp.where(lane == 0, d0, jnp.where(lane == 1, d1, 0.0))
        dest_ref[0] = dd.T[:8, :].astype(jnp.int32)

    carry_ref[...] += jnp.sum(o, axis=0, keepdims=True)


def _dispatch_plan(route, bm):
    t = route.shape[0]
    tb = min(1024, t)
    nb = t // tb
    return pl.pallas_call(
        functools.partial(_plan_kernel, bm=bm),
        grid=(2, nb),
        in_specs=[pl.BlockSpec((tb, LANES), lambda p, i: (i, 0))],
        out_specs=[pl.BlockSpec((1, 8, tb), lambda p, i: (i * p, 0, 0)),
                   pl.BlockSpec((8, LANES), lambda p, i: (0, 0))],
        out_shape=[jax.ShapeDtypeStruct((nb, 8, tb), jnp.int32), jax.ShapeDtypeStruct((8, LANES), f32)],
        scratch_shapes=[pltpu.VMEM((8, LANES), f32)],
        compiler_params=_cparams(("arbitrary", "arbitrary")),
        name="dispatch_plan",
    )(route)


def _dispatch_kernel(d0_ref, d1_ref, fill_ref, end_ref, nu_ref, hs_ref, xs_hbm, zbuf, sem, zsem,
                     *, tb, bm, n_blk, unroll=8):
    i = pl.program_id(0)
    base = i * tb
    blk_rows = bm * PITCH

    def pad_copy(s):
        return pltpu.make_async_copy(zbuf.at[pl.ds(0, PITCH), :], xs_hbm.at[pl.ds(s * PITCH, PITCH), :], zsem)

    def tail_copy(g):
        return pltpu.make_async_copy(zbuf, xs_hbm.at[pl.ds(pl.multiple_of(g * blk_rows, 8), blk_rows), :], zsem)

    @pl.when(i == 0)
    def _():
        zbuf[...] = jnp.zeros(zbuf.shape, u32)
        for op in ("start", "wait"):
            def per_expert(e, c, op=op):
                def one(s, c2):
                    getattr(pad_copy(s), op)()
                    return c2
                return lax.fori_loop(fill_ref[e], end_ref[e], one, c)

            def per_tail(g, c, op=op):
                getattr(tail_copy(g), op)()
                return c

            lax.fori_loop(0, N_EXPERTS, per_expert, 0)
            lax.fori_loop(nu_ref[0], n_blk, per_tail, 0)

    def body(g, c):
        for u in range(unroll):
            r = g * unroll + u
            src = hs_ref.at[pl.ds(r * PITCH, PITCH), :]
            for k, dref in enumerate((d0_ref, d1_ref)):
                d = dref[base + r]
                pltpu.make_async_copy(src, xs_hbm.at[pl.ds(d * PITCH, PITCH), :], sem).start(priority=k)
        return c

    lax.fori_loop(0, tb // unroll, body, 0)
    for _ in range(TOP_K):
        pltpu.make_async_copy(xs_hbm.at[pl.ds(0, tb * PITCH), :], xs_hbm.at[pl.ds(0, tb * PITCH), :], sem).wait()


def _dispatch_rows(dest0, dest1, fill_start, seg_end, n_used, hs, cap):
    t = dest0.shape[0]
    tb = min(1024, t)
    bm = MOE_BLOCK
    grid_spec = pltpu.PrefetchScalarGridSpec(
        num_scalar_prefetch=5,
        grid=(t // tb,),
        in_specs=[pl.BlockSpec((tb * PITCH, LANES), lambda i, *_: (i, 0))],
        out_specs=pl.BlockSpec(memory_space=pl.ANY),
        scratch_shapes=[pltpu.VMEM((bm * PITCH, LANES), u32), pltpu.SemaphoreType.DMA(()),
                        pltpu.SemaphoreType.DMA(())],
    )
    return pl.pallas_call(
        functools.partial(_dispatch_kernel, tb=tb, bm=bm, n_blk=cap // bm),
        grid_spec=grid_spec,
        out_shape=jax.ShapeDtypeStruct((cap * PITCH, LANES), u32),
        compiler_params=_cparams(("arbitrary",)),
        name="dispatch_rows",
    )(dest0, dest1, fill_start, seg_end, n_used, hs)


def _moe_kernel(g0_ref, nb_ref, nu_ref, x_hbm, w1_ref, w3_ref, w2_ref, y_hbm,
                xbuf, ybuf, w1b, w3b, w2b, gsem, ysem, *, bm, n_blk):
    e = pl.program_id(0)
    n_used = nu_ref[0]
    g0 = g0_ref[e]
    nb = nb_ref[e]
    blk_rows = bm * PITCH

    def x_copy(slot, g):
        src = x_hbm.at[pl.ds(pl.multiple_of(g * blk_rows, 8), blk_rows), :]
        return pltpu.make_async_copy(src, xbuf.at[slot], gsem.at[slot])

    def y_copy(slot, g):
        dst = y_hbm.at[pl.ds(pl.multiple_of(g * blk_rows, 8), blk_rows), :]
        return pltpu.make_async_copy(ybuf.at[slot], dst, ysem.at[slot])

    @pl.when(e == 0)
    def _():
        x_copy(0, 0).start()

    @pl.when(nb > 0)
    def _():
        w1b[...] = w1_ref[0].astype(bf16)
        w3b[...] = w3_ref[0].astype(bf16)
        w2b[...] = w2_ref[0].astype(bf16)

    def block(c, carry):
        g = g0 + c
        slot = g % 2

        @pl.when(g + 1 < n_used)
        def _():
            x_copy(1 - slot, g + 1).start()

        x_copy(slot, g).wait()

        hn = _slabs_to_rows(xbuf.at[slot], bm).astype(bf16)
        a = jnp.dot(hn, w1b[...], preferred_element_type=f32)
        b = jnp.dot(hn, w3b[...], preferred_element_type=f32)
        mid = (a * jax.nn.sigmoid(a) * b).astype(bf16)
        y = jnp.dot(mid, w2b[...], preferred_element_type=f32)

        @pl.when(g >= 2)
        def _():
            y_copy(slot, g - 2).wait()

        _rows_to_slabs(ybuf.at[slot], y, bm)
        y_copy(slot, g).start()
        return carry

    lax.fori_loop(0, nb, block, 0)

    @pl.when(e == pl.num_programs(0) - 1)
    def _():
        @pl.when(n_used >= 2)
        def _():
            y_copy(n_used % 2, n_used - 2).wait()

        y_copy((n_used - 1) % 2, n_used - 1).wait()
        ybuf[0] = jnp.zeros(ybuf.shape[1:], u32)

        def zstart(g, carry):
            y_copy(0, g).start()
            return carry

        def zwait(g, carry):
            y_copy(0, g).wait()
            return carry

        lax.fori_loop(n_used, n_blk, zstart, 0)
        lax.fori_loop(n_used, n_blk, zwait, 0)


def _expert_mlp(blk_start, blk_count, n_used, xs, w1, w3, w2, cap):
    d = D_MODEL
    bm = MOE_BLOCK
    n_blk = cap // bm
    by_expert = lambda shape: pl.BlockSpec(shape, lambda e, g0, nb, nu: (e, 0, 0))
    grid_spec = pltpu.PrefetchScalarGridSpec(
        num_scalar_prefetch=3,
        grid=(N_EXPERTS,),
        in_specs=[pl.BlockSpec(memory_space=pl.ANY),
                  by_expert((1, d, D_EXPERT)), by_expert((1, d, D_EXPERT)), by_expert((1, D_EXPERT, d))],
        out_specs=pl.BlockSpec(memory_space=pl.ANY),
        scratch_shapes=[pltpu.VMEM((2, bm * PITCH, LANES), u32), pltpu.VMEM((2, bm * PITCH, LANES), u32),
                        pltpu.VMEM((d, D_EXPERT), bf16), pltpu.VMEM((d, D_EXPERT), bf16),
                        pltpu.VMEM((D_EXPERT, d), bf16),
                        pltpu.SemaphoreType.DMA((2,)), pltpu.SemaphoreType.DMA((2,))],
    )
    return pl.pallas_call(
        functools.partial(_moe_kernel, bm=bm, n_blk=n_blk),
        grid_spec=grid_spec,
        out_shape=jax.ShapeDtypeStruct((cap * PITCH, LANES), u32),
        compiler_params=_cparams(("arbitrary",)),
        name="expert_mlp",
    )(blk_start, blk_count, n_used, xs, w1, w3, w2)


def _combine_kernel(d0_ref, d1_ref, h_ref, route_ref, y_hbm, nw_ref, o_ref, ybuf, sem, *, tb):
    i = pl.program_id(0)
    n = pl.num_programs(0)
    slot = i % 2

    def start(blk, s):
        _slab_gather_start(d0_ref, blk * tb, tb, y_hbm, ybuf.at[s, 0], sem.at[s])
        _slab_gather_start(d1_ref, blk * tb, tb, y_hbm, ybuf.at[s, 1], sem.at[s])

    @pl.when(i == 0)
    def _():
        start(0, 0)

    @pl.when(i + 1 < n)
    def _():
        start(i + 1, 1 - slot)

    for k in range(TOP_K):
        _slab_gather_wait(tb, y_hbm, ybuf.at[slot, k], sem.at[slot])
    r = route_ref[...]
    h = h_ref[...] + (r[:, 2:3] * _slabs_to_rows(ybuf.at[slot, 0], tb)
                      + r[:, 3:4] * _slabs_to_rows(ybuf.at[slot, 1], tb))
    o_ref[...] = h * lax.rsqrt(jnp.mean(h * h, axis=-1, keepdims=True) + EPS) * nw_ref[...]


def _combine(dest0, dest1, h, route, ys, norm_w):
    t, d = h.shape
    tb = min(256, t)
    grid_spec = pltpu.PrefetchScalarGridSpec(
        num_scalar_prefetch=2,
        grid=(t // tb,),
        in_specs=[pl.BlockSpec((tb, d), lambda i, a, b: (i, 0)),
                  pl.BlockSpec((tb, LANES), lambda i, a, b: (i, 0)),
                  pl.BlockSpec(memory_space=pl.ANY),
                  pl.BlockSpec((1, d), lambda i, a, b: (0, 0))],
        out_specs=pl.BlockSpec((tb, d), lambda i, a, b: (i, 0)),
        scratch_shapes=[pltpu.VMEM((2, TOP_K, tb * PITCH, LANES), u32), pltpu.SemaphoreType.DMA((2,))],
    )
    return pl.pallas_call(
        functools.partial(_combine_kernel, tb=tb),
        grid_spec=grid_spec,
        out_shape=jax.ShapeDtypeStruct((t, d), f32),
        compiler_params=_cparams(("arbitrary",)),
        name="combine_final_norm",
    )(dest0, dest1, h, route, ys, norm_w)


def kernel(x, positions, norm_mix_w, w_in, fox_b_f, ret_gn_w, w_branch, b_gate, w_out, norm_ffn_w, w_router_group,
           b_router_group, w_router_expert, b_router_expert, w1, w3, w2, norm_final_w):
    b, s, d = x.shape
    assert d == D_MODEL and norm_mix_w.shape[0] == 1
    t = b * s
    x2 = x.reshape(t, d)
    scale = 1.0 / math.sqrt(HEAD_DIM)

    wt = w_in[0].T
    o_f = 3 * FOX_WIDTH
    o_r = o_f + FOX_HEADS
    o_g = o_r + 4 * RET_WIDTH
    w_main = jnp.concatenate([wt[o_g:], wt[:FOX_WIDTH] * scale, wt[FOX_WIDTH:o_f],
                              wt[o_r:o_r + RET_WIDTH] * scale, wt[o_r + RET_WIDTH:o_g]], axis=0).astype(bf16)
    w_f = jnp.pad(wt[o_f:o_r], ((0, LANES - FOX_HEADS), (0, 0))).astype(bf16)
    bias_f = jnp.pad(fox_b_f[0], (0, LANES - FOX_HEADS)).reshape(1, LANES)
    w_r = jnp.concatenate([w_router_group[0], w_router_expert[0]], axis=1)
    w_r = jnp.pad(w_r, ((0, 0), (0, LANES - w_r.shape[1])))
    wr_hi = w_r.astype(bf16)
    wr_lo = (w_r - wr_hi.astype(f32)).astype(bf16)
    b_r = jnp.pad(jnp.concatenate([b_router_group[0], b_router_expert[0]]), (0, LANES - N_GROUPS - N_EXPERTS))
    b_r = b_r.reshape(1, LANES)

    cos, sin = _rope_tables(positions)
    proj, f_logit = _in_projection(x2, norm_mix_w, w_main, w_f)
    c = _forget_cumsum(f_logit, bias_f, b, s)
    fox_o = _fox_attention(proj, c.reshape(b * FOX_HEADS, 1, s), b, s)
    ret_o = _retention(proj, cos, sin, ret_gn_w, b, s)
    merged = _branch_merge(fox_o, ret_o, proj, b_gate[0], w_branch[0].astype(bf16))
    h, h_slabs, route = _outproj_router(merged, x2, w_out[0].astype(bf16), norm_ffn_w,
                                        jnp.concatenate([wr_hi, wr_lo], axis=1), b_r)

    bm = MOE_BLOCK
    cap = t * TOP_K + N_EXPERTS * bm
    dest_rows, seg = _dispatch_plan(route, bm)
    dest0 = dest_rows[:, 0, :].reshape(t)
    dest1 = dest_rows[:, 1, :].reshape(t)
    pends = seg[0, :N_EXPERTS].astype(jnp.int32)
    padded = seg[1, :N_EXPERTS].astype(jnp.int32)
    counts = seg[2, :N_EXPERTS].astype(jnp.int32)
    pstarts = pends - padded
    n_used = pends[-1:] // bm

    xs = _dispatch_rows(dest0, dest1, pstarts + counts, pends, n_used, h_slabs, cap)
    y = _expert_mlp(pstarts // bm, padded // bm, n_used, xs, w1[0], w3[0], w2[0], cap)
    out = _combine(dest0, dest1, h, route, y, norm_final_w.reshape(1, d))
    return out.reshape(b, s, d)
```

```python
import functools
import math

import numpy as np
import jax
import jax.numpy as jnp
from jax import lax
from jax.experimental import pallas as pl
from jax.experimental.pallas import tpu as pltpu

D_MODEL = 2048
HEAD_DIM = 128
FOX_HEADS = 8
RET_HEADS = 8
FOX_WIDTH = FOX_HEADS * HEAD_DIM
RET_WIDTH = RET_HEADS * HEAD_DIM
N_BRANCH = 2
N_GROUPS = 4
EXPERTS_PER_GROUP = 8
N_EXPERTS = N_GROUPS * EXPERTS_PER_GROUP
TOP_K = 2
D_EXPERT = 512
ROPE_BASE = 10000.0
EPS = 1e-6

LANES = 128
NEG_BIG = -1e30
LOG2E = math.log2(math.e)
MAIN_WIDTH = 3 * FOX_WIDTH + 4 * RET_WIDTH + N_BRANCH * D_MODEL
COL_GATE = 0
COL_FQ = N_BRANCH * D_MODEL // HEAD_DIM
COL_FK = COL_FQ + FOX_HEADS
COL_FV = COL_FK + FOX_HEADS
COL_RQ = COL_FV + FOX_HEADS
COL_RK = COL_RQ + RET_HEADS
COL_RV = COL_RK + RET_HEADS
COL_RG = COL_RV + RET_HEADS

ATTN_BLOCK = 256
FOX_HEADS_PER_STEP = 4
RET_CHUNK = 256
MOE_BLOCK = 256
VMEM_LIMIT = 56 * 1024 * 1024

f32 = jnp.float32
bf16 = jnp.bfloat16
_NT = (((1,), (1,)), ((), ()))
SLAB = D_MODEL // (2 * LANES)
PITCH = SLAB + 1
u32 = jnp.uint32
_HI_MASK = np.uint32(0xFFFF0000)


def _rows_to_slabs(ref2, val, n_rows):
    bits = lambda v: lax.bitcast_convert_type(v.astype(bf16).astype(f32), u32)
    for j in range(SLAB):
        lo = val[:, (2 * j) * LANES:(2 * j + 1) * LANES]
        hi = val[:, (2 * j + 1) * LANES:(2 * j + 2) * LANES]
        ref2[pl.ds(j, n_rows, stride=PITCH), :] = (bits(lo) >> 16) | (bits(hi) & _HI_MASK)
    ref2[pl.ds(SLAB, n_rows, stride=PITCH), :] = jnp.zeros((n_rows, LANES), u32)


def _slabs_to_rows(ref2, n_rows):
    chunks = []
    for j in range(SLAB):
        w = ref2[pl.ds(j, n_rows, stride=PITCH), :]
        chunks.append(lax.bitcast_convert_type(w << 16, f32))
        chunks.append(lax.bitcast_convert_type(w & _HI_MASK, f32))
    return jnp.concatenate(chunks, axis=-1)


def _slab_gather_start(idx_ref, base, n_rows, src_hbm, dst, sem, unroll=8):
    def body(g, _):
        for u in range(unroll):
            r = g * unroll + u
            t = idx_ref[base + r]
            pltpu.make_async_copy(src_hbm.at[pl.ds(t * PITCH, SLAB), :], dst.at[pl.ds(r * PITCH, SLAB), :],
                                  sem).start(priority=u % 2)
        return 0
    lax.fori_loop(0, n_rows // unroll, body, 0)


def _slab_gather_wait(n_rows, src_hbm, dst, sem):
    pltpu.make_async_copy(src_hbm.at[pl.ds(0, n_rows * SLAB), :], dst.at[pl.ds(0, n_rows * SLAB), :], sem).wait()


def _cparams(sem, vmem=VMEM_LIMIT):
    return pltpu.CompilerParams(dimension_semantics=sem, vmem_limit_bytes=vmem)


def _const_spec(shape):
    return pl.BlockSpec(shape, lambda i: tuple(0 for _ in shape), pipeline_mode=pl.Buffered(1))


def _rope_kernel(pos_ref, freq_ref, sgn_ref, cos_ref, sin_ref):
    ang = pos_ref[0].astype(f32) * freq_ref[...]
    cos_ref[0] = jnp.cos(ang)
    sin_ref[0] = jnp.sin(ang) * sgn_ref[...]


def _rope_tables(positions):
    b, s = positions.shape
    half = HEAD_DIM // 2
    inv = (np.float32(ROPE_BASE) ** (-np.arange(half, dtype=np.float32) / np.float32(half))).astype(np.float32)
    freq = jnp.asarray(np.concatenate([inv, inv])[None, :])
    sgn = jnp.asarray(np.concatenate([-np.ones(half, np.float32), np.ones(half, np.float32)])[None, :])
    out = jax.ShapeDtypeStruct((b, s, HEAD_DIM), f32)
    return pl.pallas_call(
        _rope_kernel,
        grid=(b,),
        in_specs=[pl.BlockSpec((1, s, 1), lambda i: (i, 0, 0)),
                  pl.BlockSpec((1, HEAD_DIM), lambda i: (0, 0)),
                  pl.BlockSpec((1, HEAD_DIM), lambda i: (0, 0))],
        out_specs=[pl.BlockSpec((1, s, HEAD_DIM), lambda i: (i, 0, 0)),
                   pl.BlockSpec((1, s, HEAD_DIM), lambda i: (i, 0, 0))],
        out_shape=[out, out],
        compiler_params=_cparams(("parallel",)),
        name="rope_tables",
    )(positions.reshape(b, s, 1), freq, sgn)


def _inproj_kernel(x_ref, nw_ref, w_ref, wf_ref, o_ref, f_ref, xn_ref):
    @pl.when(pl.program_id(1) == 0)
    def _():
        x = x_ref[...]
        ms = jnp.mean(x * x, axis=-1, keepdims=True)
        xn = (x * lax.rsqrt(ms + EPS) * nw_ref[...]).astype(bf16)
        xn_ref[...] = xn
        f_ref[...] = lax.dot_general(xn, wf_ref[...], _NT, preferred_element_type=f32)

    o_ref[...] = lax.dot_general(xn_ref[...], w_ref[...], _NT, preferred_element_type=f32).astype(o_ref.dtype)


def _in_projection(x2, norm_w, w_main_t, w_f_t):
    t, d = x2.shape
    n = w_main_t.shape[0]
    tm = min(1024, t)
    tn = 1024
    return pl.pallas_call(
        _inproj_kernel,
        grid=(t // tm, n // tn),
        in_specs=[pl.BlockSpec((tm, d), lambda i, j: (i, 0)),
                  pl.BlockSpec((1, d), lambda i, j: (0, 0)),
                  pl.BlockSpec((tn, d), lambda i, j: (j, 0)),
                  pl.BlockSpec((LANES, d), lambda i, j: (0, 0))],
        out_specs=[pl.BlockSpec((tm, tn), lambda i, j: (i, j)),
                   pl.BlockSpec((tm, LANES), lambda i, j: (i, 0))],
        out_shape=[jax.ShapeDtypeStruct((t, n), bf16), jax.ShapeDtypeStruct((t, LANES), f32)],
        scratch_shapes=[pltpu.VMEM((tm, d), bf16)],
        compiler_params=_cparams(("parallel", "arbitrary")),
        name="in_projection",
    )(x2, norm_w, w_main_t, w_f_t)


def _fcum_kernel(f_ref, b_ref, c_ref):
    z = f_ref[...] + b_ref[...]
    ls = jnp.minimum(z, 0.0) - jnp.log1p(jnp.exp(-jnp.abs(z)))
    c = ls.T[:FOX_HEADS, :]
    s = c.shape[1]
    lane = lax.broadcasted_iota(jnp.int32, c.shape, 1)
    sh = 1
    while sh < s:
        c = c + jnp.where(lane >= sh, pltpu.roll(c, sh, axis=1), 0.0)
        sh *= 2
    c_ref[0] = c * LOG2E


def _forget_cumsum(f_logit, bias, b, s):
    return pl.pallas_call(
        _fcum_kernel,
        grid=(b,),
        in_specs=[pl.BlockSpec((s, LANES), lambda i: (i, 0)),
                  pl.BlockSpec((1, LANES), lambda i: (0, 0))],
        out_specs=pl.BlockSpec((1, FOX_HEADS, s), lambda i: (i, 0, 0)),
        out_shape=jax.ShapeDtypeStruct((b, FOX_HEADS, s), f32),
        compiler_params=_cparams(("parallel",)),
        name="forget_cumsum",
    )(f_logit, bias)


def _fox_kernel(q_ref, k_ref, v_ref, c_ref, o_ref, va_ref, acc_ref, *, tq):
    s_len = q_ref.shape[0]
    nq = s_len // tq
    n_heads = q_ref.shape[1] // HEAD_DIM
    row = lax.broadcasted_iota(jnp.int32, (tq, tq), 0)
    col = lax.broadcasted_iota(jnp.int32, (tq, tq), 1)
    causal = row >= col
    for hh in range(n_heads):
        lanes = slice(hh * HEAD_DIM, (hh + 1) * HEAD_DIM)
        va_ref[hh, :, :HEAD_DIM] = v_ref[:, lanes]
        va_ref[hh, :, HEAD_DIM:] = jnp.ones((s_len, HEAD_DIM), va_ref.dtype)
        panels = []
        row_max = [None] * nq
        for kb in range(nq):
            r0 = kb * tq
            s = lax.dot_general(q_ref[r0:, lanes], k_ref[r0:r0 + tq, lanes], _NT, preferred_element_type=f32)
            s = s - c_ref[hh, :, r0:r0 + tq]
            parts = [jnp.where(causal, s[:tq], NEG_BIG)]
            parts += [s[(j - kb) * tq:(j - kb + 1) * tq] for j in range(kb + 1, nq)]
            panels.append(parts)
            for j, part in zip(range(kb, nq), parts):
                pm = jnp.max(part, axis=-1, keepdims=True)
                row_max[j] = pm if row_max[j] is None else jnp.maximum(row_max[j], pm)
        for kb in range(nq):
            r0 = kb * tq
            p = jnp.concatenate([jnp.exp2(part - row_max[j]).astype(bf16)
                                 for j, part in zip(range(kb, nq), panels[kb])], axis=0)
            contrib = jnp.dot(p, va_ref[hh, r0:r0 + tq, :], preferred_element_type=f32)
            if kb:
                acc_ref[hh, r0:, :] += contrib
            else:
                acc_ref[hh] = contrib
        acc = acc_ref[hh]
        o_ref[:, lanes] = (acc[:, :HEAD_DIM] / acc[:, HEAD_DIM:]).astype(o_ref.dtype)


def _fox_attention(proj, c_blk, b, s):
    tq = min(ATTN_BLOCK, s)
    hp = FOX_HEADS_PER_STEP
    w = hp * HEAD_DIM
    col = lambda c0: pl.BlockSpec((s, w), lambda i, j: (i, c0 // hp + j))
    return pl.pallas_call(
        functools.partial(_fox_kernel, tq=tq),
        grid=(b, FOX_HEADS // hp),
        in_specs=[col(COL_FQ), col(COL_FK), col(COL_FV),
                  pl.BlockSpec((hp, 1, s), lambda i, j: (i * (FOX_HEADS // hp) + j, 0, 0))],
        out_specs=pl.BlockSpec((s, w), lambda i, j: (i, j)),
        out_shape=jax.ShapeDtypeStruct((b * s, FOX_WIDTH), bf16),
        scratch_shapes=[pltpu.VMEM((hp, s, 2 * HEAD_DIM), bf16), pltpu.VMEM((hp, s, 2 * HEAD_DIM), f32)],
        compiler_params=_cparams(("parallel", "parallel")),
        name="fox_attention",
    )(proj, proj, proj, c_blk)


def _ret_kernel(q_ref, k_ref, v_ref, g_ref, cos_ref, sin_ref, dec_ref, xi_ref, zeta_ref, gc_ref, gnw_ref,
                o_ref, *, chunk):
    s_len = q_ref.shape[0]
    nc = s_len // chunk
    state = jnp.zeros((HEAD_DIM, HEAD_DIM), f32)
    for ci in range(nc):
        rows = slice(ci * chunk, (ci + 1) * chunk)
        cos = cos_ref[0, rows, :]
        sin = sin_ref[0, rows, :]

        def rot(t):
            t = t.astype(f32)
            return t * cos + pltpu.roll(t, HEAD_DIM // 2, axis=1) * sin

        q = rot(q_ref[rows, :])
        k = rot(k_ref[rows, :])
        v = v_ref[rows, :]
        qb = q.astype(bf16)
        inner = lax.dot_general(qb, k.astype(bf16), _NT, preferred_element_type=f32) * dec_ref[0]
        o = jnp.dot(inner.astype(bf16), v, preferred_element_type=f32)
        if ci:
            o = o + jnp.dot(qb, state.astype(bf16), preferred_element_type=f32) * xi_ref[0]
        if ci + 1 < nc:
            kz = (k * zeta_ref[0]).astype(bf16)
            kv = lax.dot_general(kz, v, (((0,), (0,)), ((), ())), preferred_element_type=f32)
            state = state * gc_ref[0] + kv if ci else kv
        mu = jnp.mean(o, axis=-1, keepdims=True)
        oc = o - mu
        var = jnp.mean(oc * oc, axis=-1, keepdims=True)
        y = oc * lax.rsqrt(var + EPS) * gnw_ref[...]
        g = g_ref[rows, :].astype(f32)
        o_ref[rows, :] = (g * jax.nn.sigmoid(g) * y).astype(o_ref.dtype)


def _retention_constants(chunk):
    hh = np.arange(RET_HEADS, dtype=np.float64)
    log_gamma = np.log1p(-(2.0 ** (-5.0 - hh)))
    n = np.arange(chunk, dtype=np.float64)
    diff = n[:, None] - n[None, :]
    decay = np.where(diff[None] >= 0, np.exp(diff[None] * log_gamma[:, None, None]), 0.0)
    xi = np.exp((n[None, :] + 1.0) * log_gamma[:, None])
    zeta = np.exp((chunk - 1.0 - n[None, :]) * log_gamma[:, None])
    g_chunk = np.exp(chunk * log_gamma)
    bc = lambda a: np.broadcast_to(a[:, :, None], (RET_HEADS, chunk, HEAD_DIM))
    gc = np.broadcast_to(g_chunk[:, None, None], (RET_HEADS, 1, HEAD_DIM))
    to = lambda a: jnp.asarray(np.ascontiguousarray(a), dtype=f32)
    return to(decay), to(bc(xi)), to(bc(zeta)), to(gc)


def _retention(proj, cos, sin, gn_w, b, s):
    chunk = min(RET_CHUNK, s)
    decay, xi, zeta, gc = _retention_constants(chunk)
    h = RET_HEADS
    head = lambda col: pl.BlockSpec((s, HEAD_DIM), lambda i, j: (i, col + j))
    per_head = lambda shape: pl.BlockSpec(shape, lambda i, j: (j, 0, 0))
    return pl.pallas_call(
        functools.partial(_ret_kernel, chunk=chunk),
        grid=(b, h),
        in_specs=[head(COL_RQ), head(COL_RK), head(COL_RV), head(COL_RG),
                  pl.BlockSpec((1, s, HEAD_DIM), lambda i, j: (i, 0, 0)),
                  pl.BlockSpec((1, s, HEAD_DIM), lambda i, j: (i, 0, 0)),
                  per_head((1, chunk, chunk)), per_head((1, chunk, HEAD_DIM)), per_head((1, chunk, HEAD_DIM)),
                  per_head((1, 1, HEAD_DIM)),
                  pl.BlockSpec((1, HEAD_DIM), lambda i, j: (0, j))],
        out_specs=pl.BlockSpec((s, HEAD_DIM), lambda i, j: (i, j)),
        out_shape=jax.ShapeDtypeStruct((b * s, RET_WIDTH), bf16),
        compiler_params=_cparams(("parallel", "parallel")),
        name="retention",
    )(proj, proj, proj, proj, cos, sin, decay, xi, zeta, gc, gn_w)


def _branch_merge_kernel(fo_ref, ro_ref, g0_ref, g1_ref, bg_ref, wb_ref, mg_ref, *, ncol):
    d = mg_ref.shape[1]
    cw = d // ncol
    fo = fo_ref[...]
    ro = ro_ref[...]
    for c in range(ncol):
        cs = slice(c * cw, (c + 1) * cw)
        bd0 = jnp.dot(fo, wb_ref[0, :, cs], preferred_element_type=f32)
        bd1 = jnp.dot(ro, wb_ref[1, :, cs], preferred_element_type=f32)
        ga = jax.nn.sigmoid(g0_ref[:, cs].astype(f32) + bg_ref[0:1, cs])
        gb = jax.nn.sigmoid(g1_ref[:, cs].astype(f32) + bg_ref[1:2, cs])
        mg_ref[:, cs] = (ga * bd0 + gb * bd1).astype(mg_ref.dtype)


def _branch_merge(fox_o, ret_o, proj, b_gate, w_branch):
    t = fox_o.shape[0]
    d = D_MODEL
    tm = min(1024, t)
    gate0 = COL_GATE * HEAD_DIM // d
    return pl.pallas_call(
        functools.partial(_branch_merge_kernel, ncol=4),
        grid=(t // tm,),
        in_specs=[pl.BlockSpec((tm, FOX_WIDTH), lambda i: (i, 0)),
                  pl.BlockSpec((tm, RET_WIDTH), lambda i: (i, 0)),
                  pl.BlockSpec((tm, d), lambda i: (i, gate0)),
                  pl.BlockSpec((tm, d), lambda i: (i, gate0 + 1)),
                  _const_spec((N_BRANCH, d)),
                  _const_spec((N_BRANCH, FOX_WIDTH, d))],
        out_specs=pl.BlockSpec((tm, d), lambda i: (i, 0)),
        out_shape=jax.ShapeDtypeStruct((t, d), bf16),
        compiler_params=_cparams(("parallel",)),
        name="branch_merge",
    )(fox_o, ret_o, proj, proj, b_gate, w_branch)


def _route(logits):
    lane = lax.broadcasted_iota(jnp.int32, logits.shape, 1).astype(f32)
    far = jnp.float32(1e9)
    gmask = lane < N_GROUPS
    gl = jnp.where(gmask, logits, NEG_BIG)
    gmax = jnp.max(gl, axis=-1, keepdims=True)
    gsel = jnp.min(jnp.where(gl == gmax, lane, far), axis=-1, keepdims=True)
    p_group = 1.0 / jnp.sum(jnp.where(gmask, jnp.exp(gl - gmax), 0.0), axis=-1, keepdims=True)
    lo = N_GROUPS + EXPERTS_PER_GROUP * gsel
    emask = (lane >= lo) & (lane < lo + EXPERTS_PER_GROUP)
    el = jnp.where(emask, logits, NEG_BIG)
    m1 = jnp.max(el, axis=-1, keepdims=True)
    i1 = jnp.min(jnp.where(el == m1, lane, far), axis=-1, keepdims=True)
    el2 = jnp.where(lane == i1, NEG_BIG, el)
    m2 = jnp.max(el2, axis=-1, keepdims=True)
    i2 = jnp.min(jnp.where(el2 == m2, lane, far), axis=-1, keepdims=True)
    t = jnp.exp(m2 - m1)
    w1 = p_group / (1.0 + t)
    w2 = p_group * t / (1.0 + t)
    out = jnp.where(lane == 0, i1 - N_GROUPS, 0.0)
    out = jnp.where(lane == 1, i2 - N_GROUPS, out)
    out = jnp.where(lane == 2, w1, out)
    out = jnp.where(lane == 3, w2, out)
    return out


def _outproj_kernel(mg_ref, x_ref, wo_ref, nfw_ref, wr_ref, br_ref, h_ref, hs_ref, route_ref):
    h = x_ref[...] + jnp.dot(mg_ref[...], wo_ref[...], preferred_element_type=f32)
    h_ref[...] = h
    hn = h * lax.rsqrt(jnp.mean(h * h, axis=-1, keepdims=True) + EPS) * nfw_ref[...]
    _rows_to_slabs(hs_ref, hn, hn.shape[0])
    hi = hn.astype(bf16)
    lo = (hn - hi.astype(f32)).astype(bf16)
    hw = jnp.dot(hi, wr_ref[...], preferred_element_type=f32)
    logits = (hw[:, :LANES] + jnp.dot(lo, wr_ref[:, :LANES], preferred_element_type=f32) + hw[:, LANES:]) + br_ref[...]
    route_ref[...] = _route(logits)


def _outproj_router(merged, x2, w_out, nfw, wr_cat, b_router):
    t, d = x2.shape
    tm = min(512, t)
    return pl.pallas_call(
        _outproj_kernel,
        grid=(t // tm,),
        in_specs=[pl.BlockSpec((tm, d), lambda i: (i, 0)),
                  pl.BlockSpec((tm, d), lambda i: (i, 0)),
                  _const_spec((d, d)),
                  _const_spec((1, d)),
                  _const_spec((d, 2 * LANES)), _const_spec((1, LANES))],
        out_specs=[pl.BlockSpec((tm, d), lambda i: (i, 0)),
                   pl.BlockSpec((tm * PITCH, LANES), lambda i: (i, 0)),
                   pl.BlockSpec((tm, LANES), lambda i: (i, 0))],
        out_shape=[jax.ShapeDtypeStruct((t, d), f32), jax.ShapeDtypeStruct((t * PITCH, LANES), u32),
                   jax.ShapeDtypeStruct((t, LANES), f32)],
        compiler_params=_cparams(("parallel",)),
        name="outproj_router",
    )(merged, x2, w_out, nfw, wr_cat, b_router)


def _plan_kernel(route_ref, dest_ref, ends_ref, carry_ref, *, bm):
    phase = pl.program_id(0)
    i = pl.program_id(1)
    r = route_ref[...]
    tb = r.shape[0]
    lane = lax.broadcasted_iota(jnp.int32, r.shape, 1).astype(f32)
    o0 = (lane == r[:, 0:1]).astype(f32)
    o1 = (lane == r[:, 1:2]).astype(f32)
    o = o0 + o1

    @pl.when((phase == 0) & (i == 0))
    def _():
        carry_ref[...] = jnp.zeros_like(carry_ref)

    @pl.when((phase == 1) & (i == 0))
    def _():
        cnt = carry_ref[...]
        padded = jnp.floor((cnt + (bm - 1)) * (1.0 / bm)) * bm
        lane8 = lax.broadcasted_iota(jnp.int32, cnt.shape, 1)
        ends = padded
        sh = 1
        while sh < LANES:
            ends = ends + jnp.where(lane8 >= sh, pltpu.roll(ends, sh, axis=1), 0.0)
            sh *= 2
        sub = lax.broadcasted_iota(jnp.int32, cnt.shape, 0)
        ends_ref[...] = jnp.where(sub == 0, ends, jnp.where(sub == 1, padded, cnt))
        carry_ref[...] = ends - padded

    @pl.when(phase == 1)
    def _():
        row = lax.broadcasted_iota(jnp.int32, (tb, tb), 0)
        col = lax.broadcasted_iota(jnp.int32, (tb, tb), 1)
        lower = jnp.where(col < row, 1.0, 0.0).astype(bf16)
        before = jnp.dot(lower, o.astype(bf16), preferred_element_type=f32) + carry_ref[0:1, :]
        d0 = jnp.sum(before * o0, axis=-1, keepdims=True)
        d1 = jnp.sum(before * o1, axis=-1, keepdims=True)
        dd = jnp.where(lane == 0, d0, jnp.where(lane == 1, d1, 0.0))
        dest_ref[0] = dd.T[:8, :].astype(jnp.int32)

    carry_ref[...] += jnp.sum(o, axis=0, keepdims=True)


def _dispatch_plan(route, bm):
    t = route.shape[0]
    tb = min(1024, t)
    nb = t // tb
    return pl.pallas_call(
        functools.partial(_plan_kernel, bm=bm),
        grid=(2, nb),
        in_specs=[pl.BlockSpec((tb, LANES), lambda p, i: (i, 0))],
        out_specs=[pl.BlockSpec((1, 8, tb), lambda p, i: (i * p, 0, 0)),
                   pl.BlockSpec((8, LANES), lambda p, i: (0, 0))],
        out_shape=[jax.ShapeDtypeStruct((nb, 8, tb), jnp.int32), jax.ShapeDtypeStruct((8, LANES), f32)],
        scratch_shapes=[pltpu.VMEM((8, LANES), f32)],
        compiler_params=_cparams(("arbitrary", "arbitrary")),
        name="dispatch_plan",
    )(route)


def _dispatch_kernel(d0_ref, d1_ref, fill_ref, end_ref, nu_ref, hs_ref, xs_hbm, zbuf, sem, zsem,
                     *, tb, bm, n_blk, unroll=8):
    i = pl.program_id(0)
    base = i * tb
    blk_rows = bm * PITCH

    def pad_copy(s):
        return pltpu.make_async_copy(zbuf.at[pl.ds(0, PITCH), :], xs_hbm.at[pl.ds(s * PITCH, PITCH), :], zsem)

    def tail_copy(g):
        return pltpu.make_async_copy(zbuf, xs_hbm.at[pl.ds(pl.multiple_of(g * blk_rows, 8), blk_rows), :], zsem)

    @pl.when(i == 0)
    def _():
        zbuf[...] = jnp.zeros(zbuf.shape, u32)
        for op in ("start", "wait"):
            def per_expert(e, c, op=op):
                def one(s, c2):
                    getattr(pad_copy(s), op)()
                    return c2
                return lax.fori_loop(fill_ref[e], end_ref[e], one, c)

            def per_tail(g, c, op=op):
                getattr(tail_copy(g), op)()
                return c

            lax.fori_loop(0, N_EXPERTS, per_expert, 0)
            lax.fori_loop(nu_ref[0], n_blk, per_tail, 0)

    def body(g, c):
        for u in range(unroll):
            r = g * unroll + u
            src = hs_ref.at[pl.ds(r * PITCH, PITCH), :]
            for k, dref in enumerate((d0_ref, d1_ref)):
                d = dref[base + r]
                pltpu.make_async_copy(src, xs_hbm.at[pl.ds(d * PITCH, PITCH), :], sem).start(priority=k)
        return c

    lax.fori_loop(0, tb // unroll, body, 0)
    for _ in range(TOP_K):
        pltpu.make_async_copy(xs_hbm.at[pl.ds(0, tb * PITCH), :], xs_hbm.at[pl.ds(0, tb * PITCH), :], sem).wait()


def _dispatch_rows(dest0, dest1, fill_start, seg_end, n_used, hs, cap):
    t = dest0.shape[0]
    tb = min(1024, t)
    bm = MOE_BLOCK
    grid_spec = pltpu.PrefetchScalarGridSpec(
        num_scalar_prefetch=5,
        grid=(t // tb,),
        in_specs=[pl.BlockSpec((tb * PITCH, LANES), lambda i, *_: (i, 0))],
        out_specs=pl.BlockSpec(memory_space=pl.ANY),
        scratch_shapes=[pltpu.VMEM((bm * PITCH, LANES), u32), pltpu.SemaphoreType.DMA(()),
                        pltpu.SemaphoreType.DMA(())],
    )
    return pl.pallas_call(
        functools.partial(_dispatch_kernel, tb=tb, bm=bm, n_blk=cap // bm),
        grid_spec=grid_spec,
        out_shape=jax.ShapeDtypeStruct((cap * PITCH, LANES), u32),
        compiler_params=_cparams(("arbitrary",)),
        name="dispatch_rows",
    )(dest0, dest1, fill_start, seg_end, n_used, hs)


def _moe_kernel(g0_ref, nb_ref, nu_ref, x_hbm, w1_ref, w3_ref, w2_ref, y_hbm,
                xbuf, ybuf, w1b, w3b, w2b, gsem, ysem, *, bm, n_blk):
    e = pl.program_id(0)
    n_used = nu_ref[0]
    g0 = g0_ref[e]
    nb = nb_ref[e]
    blk_rows = bm * PITCH

    def x_copy(slot, g):
        src = x_hbm.at[pl.ds(pl.multiple_of(g * blk_rows, 8), blk_rows), :]
        return pltpu.make_async_copy(src, xbuf.at[slot], gsem.at[slot])

    def y_copy(slot, g):
        dst = y_hbm.at[pl.ds(pl.multiple_of(g * blk_rows, 8), blk_rows), :]
        return pltpu.make_async_copy(ybuf.at[slot], dst, ysem.at[slot])

    @pl.when(e == 0)
    def _():
        x_copy(0, 0).start()

    @pl.when(nb > 0)
    def _():
        w1b[...] = w1_ref[0].astype(bf16)
        w3b[...] = w3_ref[0].astype(bf16)
        w2b[...] = w2_ref[0].astype(bf16)

    def block(c, carry):
        g = g0 + c
        slot = g % 2

        @pl.when(g + 1 < n_used)
        def _():
            x_copy(1 - slot, g + 1).start()

        x_copy(slot, g).wait()

        hn = _slabs_to_rows(xbuf.at[slot], bm).astype(bf16)
        a = jnp.dot(hn, w1b[...], preferred_element_type=f32)
        b = jnp.dot(hn, w3b[...], preferred_element_type=f32)
        mid = (a * jax.nn.sigmoid(a) * b).astype(bf16)
        y = jnp.dot(mid, w2b[...], preferred_element_type=f32)

        @pl.when(g >= 2)
        def _():
            y_copy(slot, g - 2).wait()

        _rows_to_slabs(ybuf.at[slot], y, bm)
        y_copy(slot, g).start()
        return carry

    lax.fori_loop(0, nb, block, 0)

    @pl.when(e == pl.num_programs(0) - 1)
    def _():
        @pl.when(n_used >= 2)
        def _():
            y_copy(n_used % 2, n_used - 2).wait()

        y_copy((n_used - 1) % 2, n_used - 1).wait()
        ybuf[0] = jnp.zeros(ybuf.shape[1:], u32)

        def zstart(g, carry):
            y_copy(0, g).start()
            return carry

        def zwait(g, carry):
            y_copy(0, g).wait()
            return carry

        lax.fori_loop(n_used, n_blk, zstart, 0)
        lax.fori_loop(n_used, n_blk, zwait, 0)


def _expert_mlp(blk_start, blk_count, n_used, xs, w1, w3, w2, cap):
    d = D_MODEL
    bm = MOE_BLOCK
    n_blk = cap // bm
    by_expert = lambda shape: pl.BlockSpec(shape, lambda e, g0, nb, nu: (e, 0, 0))
    grid_spec = pltpu.PrefetchScalarGridSpec(
        num_scalar_prefetch=3,
        grid=(N_EXPERTS,),
        in_specs=[pl.BlockSpec(memory_space=pl.ANY),
                  by_expert((1, d, D_EXPERT)), by_expert((1, d, D_EXPERT)), by_expert((1, D_EXPERT, d))],
        out_specs=pl.BlockSpec(memory_space=pl.ANY),
        scratch_shapes=[pltpu.VMEM((2, bm * PITCH, LANES), u32), pltpu.VMEM((2, bm * PITCH, LANES), u32),
                        pltpu.VMEM((d, D_EXPERT), bf16), pltpu.VMEM((d, D_EXPERT), bf16),
                        pltpu.VMEM((D_EXPERT, d), bf16),
                        pltpu.SemaphoreType.DMA((2,)), pltpu.SemaphoreType.DMA((2,))],
    )
    return pl.pallas_call(
        functools.partial(_moe_kernel, bm=bm, n_blk=n_blk),
        grid_spec=grid_spec,
        out_shape=jax.ShapeDtypeStruct((cap * PITCH, LANES), u32),
        compiler_params=_cparams(("arbitrary",)),
        name="expert_mlp",
    )(blk_start, blk_count, n_used, xs, w1, w3, w2)


def _combine_kernel(d0_ref, d1_ref, h_ref, route_ref, y_hbm, nw_ref, o_ref, ybuf, sem, *, tb):
    i = pl.program_id(0)
    n = pl.num_programs(0)
    slot = i % 2

    def start(blk, s):
        _slab_gather_start(d0_ref, blk * tb, tb, y_hbm, ybuf.at[s, 0], sem.at[s])
        _slab_gather_start(d1_ref, blk * tb, tb, y_hbm, ybuf.at[s, 1], sem.at[s])

    @pl.when(i == 0)
    def _():
        start(0, 0)

    @pl.when(i + 1 < n)
    def _():
        start(i + 1, 1 - slot)

    for k in range(TOP_K):
        _slab_gather_wait(tb, y_hbm, ybuf.at[slot, k], sem.at[slot])
    r = route_ref[...]
    h = h_ref[...] + (r[:, 2:3] * _slabs_to_rows(ybuf.at[slot, 0], tb)
                      + r[:, 3:4] * _slabs_to_rows(ybuf.at[slot, 1], tb))
    o_ref[...] = h * lax.rsqrt(jnp.mean(h * h, axis=-1, keepdims=True) + EPS) * nw_ref[...]


def _combine(dest0, dest1, h, route, ys, norm_w):
    t, d = h.shape
    tb = min(256, t)
    grid_spec = pltpu.PrefetchScalarGridSpec(
        num_scalar_prefetch=2,
        grid=(t // tb,),
        in_specs=[pl.BlockSpec((tb, d), lambda i, a, b: (i, 0)),
                  pl.BlockSpec((tb, LANES), lambda i, a, b: (i, 0)),
                  pl.BlockSpec(memory_space=pl.ANY),
                  pl.BlockSpec((1, d), lambda i, a, b: (0, 0))],
        out_specs=pl.BlockSpec((tb, d), lambda i, a, b: (i, 0)),
        scratch_shapes=[pltpu.VMEM((2, TOP_K, tb * PITCH, LANES), u32), pltpu.SemaphoreType.DMA((2,))],
    )
    return pl.pallas_call(
        functools.partial(_combine_kernel, tb=tb),
        grid_spec=grid_spec,
        out_shape=jax.ShapeDtypeStruct((t, d), f32),
        compiler_params=_cparams(("arbitrary",)),
        name="combine_final_norm",
    )(dest0, dest1, h, route, ys, norm_w)


def kernel(x, positions, norm_mix_w, w_in, fox_b_f, ret_gn_w, w_branch, b_gate, w_out, norm_ffn_w, w_router_group,
           b_router_group, w_router_expert, b_router_expert, w1, w3, w2, norm_final_w):
    b, s, d = x.shape
    assert d == D_MODEL and norm_mix_w.shape[0] == 1
    t = b * s
    x2 = x.reshape(t, d)
    scale = 1.0 / math.sqrt(HEAD_DIM)

    wt = w_in[0].T
    o_f = 3 * FOX_WIDTH
    o_r = o_f + FOX_HEADS
    o_g = o_r + 4 * RET_WIDTH
    w_main = jnp.concatenate([wt[o_g:], wt[:FOX_WIDTH] * (scale * LOG2E), wt[FOX_WIDTH:o_f],
                              wt[o_r:o_r + RET_WIDTH] * scale, wt[o_r + RET_WIDTH:o_g]], axis=0).astype(bf16)
    w_f = jnp.pad(wt[o_f:o_r], ((0, LANES - FOX_HEADS), (0, 0))).astype(bf16)
    bias_f = jnp.pad(fox_b_f[0], (0, LANES - FOX_HEADS)).reshape(1, LANES)
    w_r = jnp.concatenate([w_router_group[0], w_router_expert[0]], axis=1)
    w_r = jnp.pad(w_r, ((0, 0), (0, LANES - w_r.shape[1])))
    wr_hi = w_r.astype(bf16)
    wr_lo = (w_r - wr_hi.astype(f32)).astype(bf16)
    b_r = jnp.pad(jnp.concatenate([b_router_group[0], b_router_expert[0]]), (0, LANES - N_GROUPS - N_EXPERTS))
    b_r = b_r.reshape(1, LANES)

    cos, sin = _rope_tables(positions)
    proj, f_logit = _in_projection(x2, norm_mix_w, w_main, w_f)
    c = _forget_cumsum(f_logit, bias_f, b, s)
    fox_o = _fox_attention(proj, c.reshape(b * FOX_HEADS, 1, s), b, s)
    ret_o = _retention(proj, cos, sin, ret_gn_w, b, s)
    merged = _branch_merge(fox_o, ret_o, proj, b_gate[0], w_branch[0].astype(bf16))
    h, h_slabs, route = _outproj_router(merged, x2, w_out[0].astype(bf16), norm_ffn_w,
                                        jnp.concatenate([wr_hi, wr_lo], axis=1), b_r)

    bm = MOE_BLOCK
    cap = t * TOP_K + N_EXPERTS * bm
    dest_rows, seg = _dispatch_plan(route, bm)
    dest0 = dest_rows[:, 0, :].reshape(t)
    dest1 = dest_rows[:, 1, :].reshape(t)
    pends = seg[0, :N_EXPERTS].astype(jnp.int32)
    padded = seg[1, :N_EXPERTS].astype(jnp.int32)
    counts = seg[2, :N_EXPERTS].astype(jnp.int32)
    pstarts = pends - padded
    n_used = pends[-1:] // bm

    xs = _dispatch_rows(dest0, dest1, pstarts + counts, pends, n_used, h_slabs, cap)
    y = _expert_mlp(pstarts // bm, padded // bm, n_used, xs, w1[0], w3[0], w2[0], cap)
    out = _combine(dest0, dest1, h, route, y, norm_final_w.reshape(1, d))
    return out.reshape(b, s, d)
```

```python
import functools
import math

import numpy as np
import jax
import jax.numpy as jnp
from jax import lax
from jax.experimental import pallas as pl
from jax.experimental.pallas import tpu as pltpu

D_MODEL = 2048
HEAD_DIM = 128
FOX_HEADS = 8
RET_HEADS = 8
FOX_WIDTH = FOX_HEADS * HEAD_DIM
RET_WIDTH = RET_HEADS * HEAD_DIM
N_BRANCH = 2
N_GROUPS = 4
EXPERTS_PER_GROUP = 8
N_EXPERTS = N_GROUPS * EXPERTS_PER_GROUP
TOP_K = 2
D_EXPERT = 512
ROPE_BASE = 10000.0
EPS = 1e-6

LANES = 128
NEG_BIG = -1e30
LOG2E = math.log2(math.e)
MAIN_WIDTH = 3 * FOX_WIDTH + 4 * RET_WIDTH + N_BRANCH * D_MODEL
COL_GATE = 0
COL_FQ = N_BRANCH * D_MODEL // HEAD_DIM
COL_FK = COL_FQ + FOX_HEADS
COL_FV = COL_FK + FOX_HEADS
COL_RQ = COL_FV + FOX_HEADS
COL_RK = COL_RQ + RET_HEADS
COL_RV = COL_RK + RET_HEADS
COL_RG = COL_RV + RET_HEADS

ATTN_BLOCK = 256
FOX_HEADS_PER_STEP = 4
RET_HEADS_PER_STEP = 1
RET_CHUNK = 256
MOE_BLOCK = 256
VMEM_LIMIT = 56 * 1024 * 1024

f32 = jnp.float32
bf16 = jnp.bfloat16
_NT = (((1,), (1,)), ((), ()))
SLAB = D_MODEL // (2 * LANES)
PITCH = SLAB + 1
u32 = jnp.uint32
_HI_MASK = np.uint32(0xFFFF0000)


def _rows_to_slabs(ref2, val, n_rows):
    bits = lambda v: lax.bitcast_convert_type(v.astype(bf16).astype(f32), u32)
    for j in range(SLAB):
        lo = val[:, (2 * j) * LANES:(2 * j + 1) * LANES]
        hi = val[:, (2 * j + 1) * LANES:(2 * j + 2) * LANES]
        ref2[pl.ds(j, n_rows, stride=PITCH), :] = (bits(lo) >> 16) | (bits(hi) & _HI_MASK)
    ref2[pl.ds(SLAB, n_rows, stride=PITCH), :] = jnp.zeros((n_rows, LANES), u32)


def _slabs_to_rows(ref2, n_rows):
    chunks = []
    for j in range(SLAB):
        w = ref2[pl.ds(j, n_rows, stride=PITCH), :]
        chunks.append(lax.bitcast_convert_type(w << 16, f32))
        chunks.append(lax.bitcast_convert_type(w & _HI_MASK, f32))
    return jnp.concatenate(chunks, axis=-1)


def _slab_gather_start(idx_ref, base, n_rows, src_hbm, dst, sem, unroll=8):
    def body(g, _):
        for u in range(unroll):
            r = g * unroll + u
            t = idx_ref[base + r]
            pltpu.make_async_copy(src_hbm.at[pl.ds(t * PITCH, SLAB), :], dst.at[pl.ds(r * PITCH, SLAB), :],
                                  sem).start(priority=u % 2)
        return 0
    lax.fori_loop(0, n_rows // unroll, body, 0)


def _slab_gather_wait(n_rows, src_hbm, dst, sem):
    pltpu.make_async_copy(src_hbm.at[pl.ds(0, n_rows * SLAB), :], dst.at[pl.ds(0, n_rows * SLAB), :], sem).wait()


def _cparams(sem, vmem=VMEM_LIMIT):
    return pltpu.CompilerParams(dimension_semantics=sem, vmem_limit_bytes=vmem)


def _const_spec(shape):
    return pl.BlockSpec(shape, lambda i: tuple(0 for _ in shape), pipeline_mode=pl.Buffered(1))


def _rope_kernel(pos_ref, freq_ref, sgn_ref, cos_ref, sin_ref):
    pos = pos_ref[0].astype(f32)
    hs = pos.shape[0] // 2
    half = HEAD_DIM // 2
    lane = lax.broadcasted_iota(jnp.int32, (hs, HEAD_DIM), 1)
    low = lane < half
    ang = jnp.where(low, pos[:hs], pos[hs:]) * freq_ref[...]
    c = jnp.cos(ang)
    s = jnp.sin(ang) * sgn_ref[...]
    c_sw = pltpu.roll(c, half, axis=1)
    s_sw = pltpu.roll(s, half, axis=1)
    cos_ref[0, :hs, :] = jnp.where(low, c, c_sw)
    cos_ref[0, hs:, :] = jnp.where(low, c_sw, c)
    sin_ref[0, :hs, :] = jnp.where(low, s, -s_sw)
    sin_ref[0, hs:, :] = jnp.where(low, -s_sw, s)


def _rope_tables(positions):
    b, s = positions.shape
    half = HEAD_DIM // 2
    inv = (np.float32(ROPE_BASE) ** (-np.arange(half, dtype=np.float32) / np.float32(half))).astype(np.float32)
    freq = jnp.asarray(np.concatenate([inv, inv])[None, :])
    sgn = jnp.asarray(np.concatenate([-np.ones(half, np.float32), np.ones(half, np.float32)])[None, :])
    out = jax.ShapeDtypeStruct((b, s, HEAD_DIM), f32)
    return pl.pallas_call(
        _rope_kernel,
        grid=(b,),
        in_specs=[pl.BlockSpec((1, s, 1), lambda i: (i, 0, 0)),
                  pl.BlockSpec((1, HEAD_DIM), lambda i: (0, 0)),
                  pl.BlockSpec((1, HEAD_DIM), lambda i: (0, 0))],
        out_specs=[pl.BlockSpec((1, s, HEAD_DIM), lambda i: (i, 0, 0)),
                   pl.BlockSpec((1, s, HEAD_DIM), lambda i: (i, 0, 0))],
        out_shape=[out, out],
        compiler_params=_cparams(("parallel",)),
        name="rope_tables",
    )(positions.reshape(b, s, 1), freq, sgn)


_Q_FACTORS = ((COL_FQ * HEAD_DIM // 1024, LOG2E / math.sqrt(HEAD_DIM)),
              (COL_RQ * HEAD_DIM // 1024, 1.0 / math.sqrt(HEAD_DIM)))


def _wprep_kernel(off_ref, w_hbm, o_ref, buf, sem, *, tn):
    j = pl.program_id(0)
    n = pl.num_programs(0)
    slot = j % 2

    def copy(jj, s):
        src = w_hbm.at[pl.ds(pl.multiple_of(off_ref[jj], 8), tn), :]
        return pltpu.make_async_copy(src, buf.at[s], sem.at[s])

    @pl.when(j == 0)
    def _():
        copy(0, 0).start()

    @pl.when(j + 1 < n)
    def _():
        copy(j + 1, 1 - slot).start()

    copy(j, slot).wait()
    factor = jnp.float32(1.0)
    for tile, value in _Q_FACTORS:
        factor = jnp.where(j == tile, jnp.float32(value), factor)
    o_ref[...] = (buf[slot] * factor).astype(o_ref.dtype)


def _prep_in_weights(wt, row_offsets):
    tn = 1024
    d = wt.shape[1]
    n_tiles = len(row_offsets)
    grid_spec = pltpu.PrefetchScalarGridSpec(
        num_scalar_prefetch=1,
        grid=(n_tiles,),
        in_specs=[pl.BlockSpec(memory_space=pl.ANY)],
        out_specs=pl.BlockSpec((tn, d), lambda j, off: (j, 0)),
        scratch_shapes=[pltpu.VMEM((2, tn, d), f32), pltpu.SemaphoreType.DMA((2,))],
    )
    return pl.pallas_call(
        functools.partial(_wprep_kernel, tn=tn),
        grid_spec=grid_spec,
        out_shape=jax.ShapeDtypeStruct((n_tiles * tn, d), bf16),
        compiler_params=_cparams(("arbitrary",)),
        name="prep_in_weights",
    )(jnp.asarray(row_offsets, jnp.int32), wt)


def _inproj_kernel(x_ref, nw_ref, w_ref, wf_ref, o_ref, f_ref, xn_ref):
    @pl.when(pl.program_id(1) == 0)
    def _():
        x = x_ref[...]
        ms = jnp.mean(x * x, axis=-1, keepdims=True)
        xn = (x * lax.rsqrt(ms + EPS) * nw_ref[...]).astype(bf16)
        xn_ref[...] = xn
        f_ref[...] = lax.dot_general(xn, wf_ref[...], _NT, preferred_element_type=f32)

    o_ref[...] = lax.dot_general(xn_ref[...], w_ref[...], _NT, preferred_element_type=f32).astype(o_ref.dtype)


def _in_projection(x2, norm_w, w_main_t, w_f_t):
    t, d = x2.shape
    n = w_main_t.shape[0]
    tm = min(1024, t)
    tn = 1024
    return pl.pallas_call(
        _inproj_kernel,
        grid=(t // tm, n // tn),
        in_specs=[pl.BlockSpec((tm, d), lambda i, j: (i, 0)),
                  pl.BlockSpec((1, d), lambda i, j: (0, 0)),
                  pl.BlockSpec((tn, d), lambda i, j: (j, 0)),
                  pl.BlockSpec((LANES, d), lambda i, j: (0, 0))],
        out_specs=[pl.BlockSpec((tm, tn), lambda i, j: (i, j)),
                   pl.BlockSpec((tm, LANES), lambda i, j: (i, 0))],
        out_shape=[jax.ShapeDtypeStruct((t, n), bf16), jax.ShapeDtypeStruct((t, LANES), f32)],
        scratch_shapes=[pltpu.VMEM((tm, d), bf16)],
        compiler_params=_cparams(("parallel", "arbitrary")),
        name="in_projection",
    )(x2, norm_w, w_main_t, w_f_t)


def _fcum_kernel(f_ref, b_ref, c_ref):
    z = f_ref[...] + b_ref[...]
    ls = jnp.minimum(z, 0.0) - jnp.log1p(jnp.exp(-jnp.abs(z)))
    c = ls.T[:FOX_HEADS, :]
    s = c.shape[1]
    lane = lax.broadcasted_iota(jnp.int32, c.shape, 1)
    sh = 1
    while sh < s:
        c = c + jnp.where(lane >= sh, pltpu.roll(c, sh, axis=1), 0.0)
        sh *= 2
    c_ref[0] = c * LOG2E


def _forget_cumsum(f_logit, bias, b, s):
    return pl.pallas_call(
        _fcum_kernel,
        grid=(b,),
        in_specs=[pl.BlockSpec((s, LANES), lambda i: (i, 0)),
                  pl.BlockSpec((1, LANES), lambda i: (0, 0))],
        out_specs=pl.BlockSpec((1, FOX_HEADS, s), lambda i: (i, 0, 0)),
        out_shape=jax.ShapeDtypeStruct((b, FOX_HEADS, s), f32),
        compiler_params=_cparams(("parallel",)),
        name="forget_cumsum",
    )(f_logit, bias)


def _fox_kernel(q_ref, k_ref, v_ref, c_ref, o_ref, va_ref, acc_ref, *, tq):
    s_len = q_ref.shape[0]
    nq = s_len // tq
    n_heads = q_ref.shape[1] // HEAD_DIM
    row = lax.broadcasted_iota(jnp.int32, (tq, tq), 0)
    col = lax.broadcasted_iota(jnp.int32, (tq, tq), 1)
    causal = row >= col
    for hh in range(n_heads):
        lanes = slice(hh * HEAD_DIM, (hh + 1) * HEAD_DIM)
        va_ref[hh, :, :HEAD_DIM] = v_ref[:, lanes]
        va_ref[hh, :, HEAD_DIM:] = jnp.ones((s_len, HEAD_DIM), va_ref.dtype)
        panels = []
        row_max = [None] * nq
        for kb in range(nq):
            r0 = kb * tq
            s = lax.dot_general(q_ref[r0:, lanes], k_ref[r0:r0 + tq, lanes], _NT, preferred_element_type=f32)
            s = s - c_ref[hh, :, r0:r0 + tq]
            parts = [jnp.where(causal, s[:tq], NEG_BIG)]
            parts += [s[(j - kb) * tq:(j - kb + 1) * tq] for j in range(kb + 1, nq)]
            panels.append(parts)
            for j, part in zip(range(kb, nq), parts):
                pm = jnp.max(part, axis=-1, keepdims=True)
                row_max[j] = pm if row_max[j] is None else jnp.maximum(row_max[j], pm)
        for kb in range(nq):
            r0 = kb * tq
            p = jnp.concatenate([jnp.exp2(part - row_max[j]).astype(bf16)
                                 for j, part in zip(range(kb, nq), panels[kb])], axis=0)
            contrib = jnp.dot(p, va_ref[hh, r0:r0 + tq, :], preferred_element_type=f32)
            if kb:
                acc_ref[hh, r0:, :] += contrib
            else:
                acc_ref[hh] = contrib
        acc = acc_ref[hh]
        o_ref[:, lanes] = (acc[:, :HEAD_DIM] / acc[:, HEAD_DIM:]).astype(o_ref.dtype)


def _fox_attention(proj, c_blk, b, s):
    tq = min(ATTN_BLOCK, s)
    hp = FOX_HEADS_PER_STEP
    w = hp * HEAD_DIM
    col = lambda c0: pl.BlockSpec((s, w), lambda i, j: (i, c0 // hp + j))
    return pl.pallas_call(
        functools.partial(_fox_kernel, tq=tq),
        grid=(b, FOX_HEADS // hp),
        in_specs=[col(COL_FQ), col(COL_FK), col(COL_FV),
                  pl.BlockSpec((hp, 1, s), lambda i, j: (i * (FOX_HEADS // hp) + j, 0, 0))],
        out_specs=pl.BlockSpec((s, w), lambda i, j: (i, j)),
        out_shape=jax.ShapeDtypeStruct((b * s, FOX_WIDTH), bf16),
        scratch_shapes=[pltpu.VMEM((hp, s, 2 * HEAD_DIM), bf16), pltpu.VMEM((hp, s, 2 * HEAD_DIM), f32)],
        compiler_params=_cparams(("parallel", "parallel")),
        name="fox_attention",
    )(proj, proj, proj, c_blk)


def _ret_kernel(q_ref, k_ref, v_ref, g_ref, cos_ref, sin_ref, dec_ref, xi_ref, zeta_ref, gc_ref, gnw_ref,
                o_ref, *, chunk):
    s_len = q_ref.shape[0]
    nc = s_len // chunk
    n_heads = q_ref.shape[1] // HEAD_DIM
    for hh in range(n_heads):
        lanes = slice(hh * HEAD_DIM, (hh + 1) * HEAD_DIM)
        state = jnp.zeros((HEAD_DIM, HEAD_DIM), f32)
        for ci in range(nc):
            rows = slice(ci * chunk, (ci + 1) * chunk)
            cos = cos_ref[0, rows, :]
            sin = sin_ref[0, rows, :]

            def rot(t):
                t = t.astype(f32)
                return t * cos + pltpu.roll(t, HEAD_DIM // 2, axis=1) * sin

            q = rot(q_ref[rows, lanes])
            k = rot(k_ref[rows, lanes])
            v = v_ref[rows, lanes]
            qb = q.astype(bf16)
            inner = lax.dot_general(qb, k.astype(bf16), _NT, preferred_element_type=f32) * dec_ref[hh]
            o = jnp.dot(inner.astype(bf16), v, preferred_element_type=f32)
            if ci:
                o = o + jnp.dot(qb, state.astype(bf16), preferred_element_type=f32) * xi_ref[hh]
            if ci + 1 < nc:
                kz = (k * zeta_ref[hh]).astype(bf16)
                kv = lax.dot_general(kz, v, (((0,), (0,)), ((), ())), preferred_element_type=f32)
                state = state * gc_ref[hh] + kv if ci else kv
            mu = jnp.mean(o, axis=-1, keepdims=True)
            oc = o - mu
            var = jnp.mean(oc * oc, axis=-1, keepdims=True)
            y = oc * lax.rsqrt(var + EPS) * gnw_ref[:, lanes]
            g = g_ref[rows, lanes].astype(f32)
            o_ref[rows, lanes] = (g * jax.nn.sigmoid(g) * y).astype(o_ref.dtype)


def _retention_constants(chunk):
    hh = np.arange(RET_HEADS, dtype=np.float64)
    log_gamma = np.log1p(-(2.0 ** (-5.0 - hh)))
    n = np.arange(chunk, dtype=np.float64)
    diff = n[:, None] - n[None, :]
    decay = np.where(diff[None] >= 0, np.exp(diff[None] * log_gamma[:, None, None]), 0.0)
    xi = np.exp((n[None, :] + 1.0) * log_gamma[:, None])
    zeta = np.exp((chunk - 1.0 - n[None, :]) * log_gamma[:, None])
    g_chunk = np.exp(chunk * log_gamma)
    bc = lambda a: np.broadcast_to(a[:, :, None], (RET_HEADS, chunk, HEAD_DIM))
    gc = np.broadcast_to(g_chunk[:, None, None], (RET_HEADS, 1, HEAD_DIM))
    to = lambda a: jnp.asarray(np.ascontiguousarray(a), dtype=f32)
    return to(decay), to(bc(xi)), to(bc(zeta)), to(gc)


def _retention(proj, cos, sin, gn_w, b, s):
    chunk = min(RET_CHUNK, s)
    decay, xi, zeta, gc = _retention_constants(chunk)
    hp = RET_HEADS_PER_STEP
    w = hp * HEAD_DIM
    head = lambda col: pl.BlockSpec((s, w), lambda i, j: (i, col // hp + j))
    per_head = lambda shape: pl.BlockSpec((hp,) + shape, lambda i, j: (j, 0, 0))
    return pl.pallas_call(
        functools.partial(_ret_kernel, chunk=chunk),
        grid=(b, RET_HEADS // hp),
        in_specs=[head(COL_RQ), head(COL_RK), head(COL_RV), head(COL_RG),
                  pl.BlockSpec((1, s, HEAD_DIM), lambda i, j: (i, 0, 0)),
                  pl.BlockSpec((1, s, HEAD_DIM), lambda i, j: (i, 0, 0)),
                  per_head((chunk, chunk)), per_head((chunk, HEAD_DIM)), per_head((chunk, HEAD_DIM)),
                  per_head((1, HEAD_DIM)),
                  pl.BlockSpec((1, w), lambda i, j: (0, j))],
        out_specs=pl.BlockSpec((s, w), lambda i, j: (i, j)),
        out_shape=jax.ShapeDtypeStruct((b * s, RET_WIDTH), bf16),
        compiler_params=_cparams(("parallel", "parallel")),
        name="retention",
    )(proj, proj, proj, proj, cos, sin, decay, xi, zeta, gc, gn_w)


def _branch_merge_kernel(fo_ref, ro_ref, g0_ref, g1_ref, bg_ref, wb_ref, mg_ref, *, ncol):
    d = mg_ref.shape[1]
    cw = d // ncol
    fo = fo_ref[...]
    ro = ro_ref[...]
    for c in range(ncol):
        cs = slice(c * cw, (c + 1) * cw)
        bd0 = jnp.dot(fo, wb_ref[0, :, cs], preferred_element_type=f32)
        bd1 = jnp.dot(ro, wb_ref[1, :, cs], preferred_element_type=f32)
        ga = jax.nn.sigmoid(g0_ref[:, cs].astype(f32) + bg_ref[0:1, cs])
        gb = jax.nn.sigmoid(g1_ref[:, cs].astype(f32) + bg_ref[1:2, cs])
        mg_ref[:, cs] = (ga * bd0 + gb * bd1).astype(mg_ref.dtype)


def _branch_merge(fox_o, ret_o, proj, b_gate, w_branch):
    t = fox_o.shape[0]
    d = D_MODEL
    tm = min(1024, t)
    gate0 = COL_GATE * HEAD_DIM // d
    return pl.pallas_call(
        functools.partial(_branch_merge_kernel, ncol=4),
        grid=(t // tm,),
        in_specs=[pl.BlockSpec((tm, FOX_WIDTH), lambda i: (i, 0)),
                  pl.BlockSpec((tm, RET_WIDTH), lambda i: (i, 0)),
                  pl.BlockSpec((tm, d), lambda i: (i, gate0)),
                  pl.BlockSpec((tm, d), lambda i: (i, gate0 + 1)),
                  _const_spec((N_BRANCH, d)),
                  _const_spec((N_BRANCH, FOX_WIDTH, d))],
        out_specs=pl.BlockSpec((tm, d), lambda i: (i, 0)),
        out_shape=jax.ShapeDtypeStruct((t, d), bf16),
        compiler_params=_cparams(("parallel",)),
        name="branch_merge",
    )(fox_o, ret_o, proj, proj, b_gate, w_branch)


def _route(logits):
    lane = lax.broadcasted_iota(jnp.int32, logits.shape, 1).astype(f32)
    far = jnp.float32(1e9)
    gmask = lane < N_GROUPS
    gl = jnp.where(gmask, logits, NEG_BIG)
    gmax = jnp.max(gl, axis=-1, keepdims=True)
    gsel = jnp.min(jnp.where(gl == gmax, lane, far), axis=-1, keepdims=True)
    p_group = 1.0 / jnp.sum(jnp.where(gmask, jnp.exp(gl - gmax), 0.0), axis=-1, keepdims=True)
    lo = N_GROUPS + EXPERTS_PER_GROUP * gsel
    emask = (lane >= lo) & (lane < lo + EXPERTS_PER_GROUP)
    el = jnp.where(emask, logits, NEG_BIG)
    m1 = jnp.max(el, axis=-1, keepdims=True)
    i1 = jnp.min(jnp.where(el == m1, lane, far), axis=-1, keepdims=True)
    el2 = jnp.where(lane == i1, NEG_BIG, el)
    m2 = jnp.max(el2, axis=-1, keepdims=True)
    i2 = jnp.min(jnp.where(el2 == m2, lane, far), axis=-1, keepdims=True)
    t = jnp.exp(m2 - m1)
    w1 = p_group / (1.0 + t)
    w2 = p_group * t / (1.0 + t)
    out = jnp.where(lane == 0, i1 - N_GROUPS, 0.0)
    out = jnp.where(lane == 1, i2 - N_GROUPS, out)
    out = jnp.where(lane == 2, w1, out)
    out = jnp.where(lane == 3, w2, out)
    return out


def _outproj_kernel(mg_ref, x_ref, wo_ref, nfw_ref, wr_ref, br_ref, h_ref, hs_ref, route_ref):
    h = x_ref[...] + jnp.dot(mg_ref[...], wo_ref[...], preferred_element_type=f32)
    h_ref[...] = h
    hn = h * lax.rsqrt(jnp.mean(h * h, axis=-1, keepdims=True) + EPS) * nfw_ref[...]
    _rows_to_slabs(hs_ref, hn, hn.shape[0])
    hi = hn.astype(bf16)
    lo = (hn - hi.astype(f32)).astype(bf16)
    hw = jnp.dot(hi, wr_ref[...], preferred_element_type=f32)
    logits = (hw[:, :LANES] + jnp.dot(lo, wr_ref[:, :LANES], preferred_element_type=f32) + hw[:, LANES:]) + br_ref[...]
    route_ref[...] = _route(logits)


def _outproj_router(merged, x2, w_out, nfw, wr_cat, b_router):
    t, d = x2.shape
    tm = min(512, t)
    return pl.pallas_call(
        _outproj_kernel,
        grid=(t // tm,),
        in_specs=[pl.BlockSpec((tm, d), lambda i: (i, 0)),
                  pl.BlockSpec((tm, d), lambda i: (i, 0)),
                  _const_spec((d, d)),
                  _const_spec((1, d)),
                  _const_spec((d, 2 * LANES)), _const_spec((1, LANES))],
        out_specs=[pl.BlockSpec((tm, d), lambda i: (i, 0)),
                   pl.BlockSpec((tm * PITCH, LANES), lambda i: (i, 0)),
                   pl.BlockSpec((tm, LANES), lambda i: (i, 0))],
        out_shape=[jax.ShapeDtypeStruct((t, d), f32), jax.ShapeDtypeStruct((t * PITCH, LANES), u32),
                   jax.ShapeDtypeStruct((t, LANES), f32)],
        compiler_params=_cparams(("parallel",)),
        name="outproj_router",
    )(merged, x2, w_out, nfw, wr_cat, b_router)


def _plan_kernel(route_ref, dest_ref, ends_ref, carry_ref, *, bm):
    phase = pl.program_id(0)
    i = pl.program_id(1)
    r = route_ref[...]
    tb = r.shape[0]
    lane = lax.broadcasted_iota(jnp.int32, r.shape, 1).astype(f32)
    o0 = (lane == r[:, 0:1]).astype(f32)
    o1 = (lane == r[:, 1:2]).astype(f32)
    o = o0 + o1

    @pl.when((phase == 0) & (i == 0))
    def _():
        carry_ref[...] = jnp.zeros_like(carry_ref)

    @pl.when((phase == 1) & (i == 0))
    def _():
        cnt = carry_ref[...]
        padded = jnp.floor((cnt + (bm - 1)) * (1.0 / bm)) * bm
        lane8 = lax.broadcasted_iota(jnp.int32, cnt.shape, 1)
        ends = padded
        sh = 1
        while sh < LANES:
            ends = ends + jnp.where(lane8 >= sh, pltpu.roll(ends, sh, axis=1), 0.0)
            sh *= 2
        sub = lax.broadcasted_iota(jnp.int32, cnt.shape, 0)
        ends_ref[...] = jnp.where(sub == 0, ends, jnp.where(sub == 1, padded, cnt))
        carry_ref[...] = ends - padded

    @pl.when(phase == 1)
    def _():
        row = lax.broadcasted_iota(jnp.int32, (tb, tb), 0)
        col = lax.broadcasted_iota(jnp.int32, (tb, tb), 1)
        lower = jnp.where(col < row, 1.0, 0.0).astype(bf16)
        before = jnp.dot(lower, o.astype(bf16), preferred_element_type=f32) + carry_ref[0:1, :]
        d0 = jnp.sum(before * o0, axis=-1, keepdims=True)
        d1 = jnp.sum(before * o1, axis=-1, keepdims=True)
        dd = jnp.where(lane == 0, d0, jnp.where(lane == 1, d1, 0.0))
        dest_ref[0] = dd.T[:8, :].astype(jnp.int32)

    carry_ref[...] += jnp.sum(o, axis=0, keepdims=True)


def _dispatch_plan(route, bm):
    t = route.shape[0]
    tb = min(1024, t)
    nb = t // tb
    return pl.pallas_call(
        functools.partial(_plan_kernel, bm=bm),
        grid=(2, nb),
        in_specs=[pl.BlockSpec((tb, LANES), lambda p, i: (i, 0))],
        out_specs=[pl.BlockSpec((1, 8, tb), lambda p, i: (i * p, 0, 0)),
                   pl.BlockSpec((8, LANES), lambda p, i: (0, 0))],
        out_shape=[jax.ShapeDtypeStruct((nb, 8, tb), jnp.int32), jax.ShapeDtypeStruct((8, LANES), f32)],
        scratch_shapes=[pltpu.VMEM((8, LANES), f32)],
        compiler_params=_cparams(("arbitrary", "arbitrary")),
        name="dispatch_plan",
    )(route)


def _dispatch_kernel(d0_ref, d1_ref, fill_ref, end_ref, nu_ref, hs_ref, xs_hbm, zbuf, sem, zsem,
                     *, tb, bm, n_blk, unroll=8):
    i = pl.program_id(0)
    base = i * tb
    blk_rows = bm * PITCH

    def pad_copy(s):
        return pltpu.make_async_copy(zbuf.at[pl.ds(0, PITCH), :], xs_hbm.at[pl.ds(s * PITCH, PITCH), :], zsem)

    def tail_copy(g):
        return pltpu.make_async_copy(zbuf, xs_hbm.at[pl.ds(pl.multiple_of(g * blk_rows, 8), blk_rows), :], zsem)

    @pl.when(i == 0)
    def _():
        zbuf[...] = jnp.zeros(zbuf.shape, u32)
        for op in ("start", "wait"):
            def per_expert(e, c, op=op):
                def one(s, c2):
                    getattr(pad_copy(s), op)()
                    return c2
                return lax.fori_loop(fill_ref[e], end_ref[e], one, c)

            def per_tail(g, c, op=op):
                getattr(tail_copy(g), op)()
                return c

            lax.fori_loop(0, N_EXPERTS, per_expert, 0)
            lax.fori_loop(nu_ref[0], n_blk, per_tail, 0)

    def body(g, c):
        for u in range(unroll):
            r = g * unroll + u
            src = hs_ref.at[pl.ds(r * PITCH, PITCH), :]
            for k, dref in enumerate((d0_ref, d1_ref)):
                d = dref[base + r]
                pltpu.make_async_copy(src, xs_hbm.at[pl.ds(d * PITCH, PITCH), :], sem).start(priority=k)
        return c

    lax.fori_loop(0, tb // unroll, body, 0)
    for _ in range(TOP_K):
        pltpu.make_async_copy(xs_hbm.at[pl.ds(0, tb * PITCH), :], xs_hbm.at[pl.ds(0, tb * PITCH), :], sem).wait()


def _dispatch_rows(dest0, dest1, fill_start, seg_end, n_used, hs, cap):
    t = dest0.shape[0]
    tb = min(1024, t)
    bm = MOE_BLOCK
    grid_spec = pltpu.PrefetchScalarGridSpec(
        num_scalar_prefetch=5,
        grid=(t // tb,),
        in_specs=[pl.BlockSpec((tb * PITCH, LANES), lambda i, *_: (i, 0))],
        out_specs=pl.BlockSpec(memory_space=pl.ANY),
        scratch_shapes=[pltpu.VMEM((bm * PITCH, LANES), u32), pltpu.SemaphoreType.DMA(()),
                        pltpu.SemaphoreType.DMA(())],
    )
    return pl.pallas_call(
        functools.partial(_dispatch_kernel, tb=tb, bm=bm, n_blk=cap // bm),
        grid_spec=grid_spec,
        out_shape=jax.ShapeDtypeStruct((cap * PITCH, LANES), u32),
        compiler_params=_cparams(("arbitrary",)),
        name="dispatch_rows",
    )(dest0, dest1, fill_start, seg_end, n_used, hs)


def _moe_kernel(g0_ref, nb_ref, nu_ref, x_hbm, w1_ref, w3_ref, w2_ref, y_hbm,
                xbuf, ybuf, w1b, w3b, w2b, gsem, ysem, *, bm, n_blk):
    e = pl.program_id(0)
    n_used = nu_ref[0]
    g0 = g0_ref[e]
    nb = nb_ref[e]
    blk_rows = bm * PITCH

    def x_copy(slot, g):
        src = x_hbm.at[pl.ds(pl.multiple_of(g * blk_rows, 8), blk_rows), :]
        return pltpu.make_async_copy(src, xbuf.at[slot], gsem.at[slot])

    def y_copy(slot, g):
        dst = y_hbm.at[pl.ds(pl.multiple_of(g * blk_rows, 8), blk_rows), :]
        return pltpu.make_async_copy(ybuf.at[slot], dst, ysem.at[slot])

    @pl.when(e == 0)
    def _():
        x_copy(0, 0).start()

    @pl.when(nb > 0)
    def _():
        w1b[...] = w1_ref[0].astype(bf16)
        w3b[...] = w3_ref[0].astype(bf16)
        w2b[...] = w2_ref[0].astype(bf16)

    def block(c, carry):
        g = g0 + c
        slot = g % 2

        @pl.when(g + 1 < n_used)
        def _():
            x_copy(1 - slot, g + 1).start()

        x_copy(slot, g).wait()

        hn = _slabs_to_rows(xbuf.at[slot], bm).astype(bf16)
        a = jnp.dot(hn, w1b[...], preferred_element_type=f32)
        b = jnp.dot(hn, w3b[...], preferred_element_type=f32)
        mid = (a * jax.nn.sigmoid(a) * b).astype(bf16)
        y = jnp.dot(mid, w2b[...], preferred_element_type=f32)

        @pl.when(g >= 2)
        def _():
            y_copy(slot, g - 2).wait()

        _rows_to_slabs(ybuf.at[slot], y, bm)
        y_copy(slot, g).start()
        return carry

    lax.fori_loop(0, nb, block, 0)

    @pl.when(e == pl.num_programs(0) - 1)
    def _():
        @pl.when(n_used >= 2)
        def _():
            y_copy(n_used % 2, n_used - 2).wait()

        y_copy((n_used - 1) % 2, n_used - 1).wait()
        ybuf[0] = jnp.zeros(ybuf.shape[1:], u32)

        def zstart(g, carry):
            y_copy(0, g).start()
            return carry

        def zwait(g, carry):
            y_copy(0, g).wait()
            return carry

        lax.fori_loop(n_used, n_blk, zstart, 0)
        lax.fori_loop(n_used, n_blk, zwait, 0)


def _expert_mlp(blk_start, blk_count, n_used, xs, w1, w3, w2, cap):
    d = D_MODEL
    bm = MOE_BLOCK
    n_blk = cap // bm
    by_expert = lambda shape: pl.BlockSpec(shape, lambda e, g0, nb, nu: (e, 0, 0))
    grid_spec = pltpu.PrefetchScalarGridSpec(
        num_scalar_prefetch=3,
        grid=(N_EXPERTS,),
        in_specs=[pl.BlockSpec(memory_space=pl.ANY),
                  by_expert((1, d, D_EXPERT)), by_expert((1, d, D_EXPERT)), by_expert((1, D_EXPERT, d))],
        out_specs=pl.BlockSpec(memory_space=pl.ANY),
        scratch_shapes=[pltpu.VMEM((2, bm * PITCH, LANES), u32), pltpu.VMEM((2, bm * PITCH, LANES), u32),
                        pltpu.VMEM((d, D_EXPERT), bf16), pltpu.VMEM((d, D_EXPERT), bf16),
                        pltpu.VMEM((D_EXPERT, d), bf16),
                        pltpu.SemaphoreType.DMA((2,)), pltpu.SemaphoreType.DMA((2,))],
    )
    return pl.pallas_call(
        functools.partial(_moe_kernel, bm=bm, n_blk=n_blk),
        grid_spec=grid_spec,
        out_shape=jax.ShapeDtypeStruct((cap * PITCH, LANES), u32),
        compiler_params=_cparams(("arbitrary",)),
        name="expert_mlp",
    )(blk_start, blk_count, n_used, xs, w1, w3, w2)


def _combine_kernel(d0_ref, d1_ref, h_ref, route_ref, y_hbm, nw_ref, o_ref, ybuf, sem, *, tb):
    i = pl.program_id(0)
    n = pl.num_programs(0)
    slot = i % 2

    def start(blk, s):
        _slab_gather_start(d0_ref, blk * tb, tb, y_hbm, ybuf.at[s, 0], sem.at[s])
        _slab_gather_start(d1_ref, blk * tb, tb, y_hbm, ybuf.at[s, 1], sem.at[s])

    @pl.when(i == 0)
    def _():
        start(0, 0)

    @pl.when(i + 1 < n)
    def _():
        start(i + 1, 1 - slot)

    for k in range(TOP_K):
        _slab_gather_wait(tb, y_hbm, ybuf.at[slot, k], sem.at[slot])
    r = route_ref[...]
    h = h_ref[...] + (r[:, 2:3] * _slabs_to_rows(ybuf.at[slot, 0], tb)
                      + r[:, 3:4] * _slabs_to_rows(ybuf.at[slot, 1], tb))
    o_ref[...] = h * lax.rsqrt(jnp.mean(h * h, axis=-1, keepdims=True) + EPS) * nw_ref[...]


def _combine(dest0, dest1, h, route, ys, norm_w):
    t, d = h.shape
    tb = min(512, t)
    grid_spec = pltpu.PrefetchScalarGridSpec(
        num_scalar_prefetch=2,
        grid=(t // tb,),
        in_specs=[pl.BlockSpec((tb, d), lambda i, a, b: (i, 0)),
                  pl.BlockSpec((tb, LANES), lambda i, a, b: (i, 0)),
                  pl.BlockSpec(memory_space=pl.ANY),
                  pl.BlockSpec((1, d), lambda i, a, b: (0, 0))],
        out_specs=pl.BlockSpec((tb, d), lambda i, a, b: (i, 0)),
        scratch_shapes=[pltpu.VMEM((2, TOP_K, tb * PITCH, LANES), u32), pltpu.SemaphoreType.DMA((2,))],
    )
    return pl.pallas_call(
        functools.partial(_combine_kernel, tb=tb),
        grid_spec=grid_spec,
        out_shape=jax.ShapeDtypeStruct((t, d), f32),
        compiler_params=_cparams(("arbitrary",)),
        name="combine_final_norm",
    )(dest0, dest1, h, route, ys, norm_w)


def kernel(x, positions, norm_mix_w, w_in, fox_b_f, ret_gn_w, w_branch, b_gate, w_out, norm_ffn_w, w_router_group,
           b_router_group, w_router_expert, b_router_expert, w1, w3, w2, norm_final_w):
    b, s, d = x.shape
    assert d == D_MODEL and norm_mix_w.shape[0] == 1
    t = b * s
    x2 = x.reshape(t, d)

    wt = w_in[0].T
    o_f = 3 * FOX_WIDTH
    o_r = o_f + FOX_HEADS
    o_g = o_r + 4 * RET_WIDTH
    tile_rows = ([o_g + k * 1024 for k in range(N_BRANCH * D_MODEL // 1024)]
                 + [k * 1024 for k in range(o_f // 1024)]
                 + [o_r + k * 1024 for k in range(4 * RET_WIDTH // 1024)])
    w_main = _prep_in_weights(wt, tile_rows)
    w_f = jnp.pad(wt[o_f:o_r], ((0, LANES - FOX_HEADS), (0, 0))).astype(bf16)
    bias_f = jnp.pad(fox_b_f[0], (0, LANES - FOX_HEADS)).reshape(1, LANES)
    w_r = jnp.concatenate([w_router_group[0], w_router_expert[0]], axis=1)
    w_r = jnp.pad(w_r, ((0, 0), (0, LANES - w_r.shape[1])))
    wr_hi = w_r.astype(bf16)
    wr_lo = (w_r - wr_hi.astype(f32)).astype(bf16)
    b_r = jnp.pad(jnp.concatenate([b_router_group[0], b_router_expert[0]]), (0, LANES - N_GROUPS - N_EXPERTS))
    b_r = b_r.reshape(1, LANES)

    cos, sin = _rope_tables(positions)
    proj, f_logit = _in_projection(x2, norm_mix_w, w_main, w_f)
    c = _forget_cumsum(f_logit, bias_f, b, s)
    fox_o = _fox_attention(proj, c.reshape(b * FOX_HEADS, 1, s), b, s)
    ret_o = _retention(proj, cos, sin, ret_gn_w, b, s)
    merged = _branch_merge(fox_o, ret_o, proj, b_gate[0], w_branch[0].astype(bf16))
    h, h_slabs, route = _outproj_router(merged, x2, w_out[0].astype(bf16), norm_ffn_w,
                                        jnp.concatenate([wr_hi, wr_lo], axis=1), b_r)

    bm = MOE_BLOCK
    cap = t * TOP_K + N_EXPERTS * bm
    dest_rows, seg = _dispatch_plan(route, bm)
    dest0 = dest_rows[:, 0, :].reshape(t)
    dest1 = dest_rows[:, 1, :].reshape(t)
    pends = seg[0, :N_EXPERTS].astype(jnp.int32)
    padded = seg[1, :N_EXPERTS].astype(jnp.int32)
    counts = seg[2, :N_EXPERTS].astype(jnp.int32)
    pstarts = pends - padded
    n_used = pends[-1:] // bm

    xs = _dispatch_rows(dest0, dest1, pstarts + counts, pends, n_used, h_slabs, cap)
    y = _expert_mlp(pstarts // bm, padded // bm, n_used, xs, w1[0], w3[0], w2[0], cap)
    out = _combine(dest0, dest1, h, route, y, norm_final_w.reshape(1, d))
    return out.reshape(b, s, d)
```

```python
import functools
import math

import numpy as np
import jax
import jax.numpy as jnp
from jax import lax
from jax.experimental import pallas as pl
from jax.experimental.pallas import tpu as pltpu

D_MODEL = 2048
HEAD_DIM = 128
FOX_HEADS = 8
RET_HEADS = 8
FOX_WIDTH = FOX_HEADS * HEAD_DIM
RET_WIDTH = RET_HEADS * HEAD_DIM
N_BRANCH = 2
N_GROUPS = 4
EXPERTS_PER_GROUP = 8
N_EXPERTS = N_GROUPS * EXPERTS_PER_GROUP
TOP_K = 2
D_EXPERT = 512
ROPE_BASE = 10000.0
EPS = 1e-6

LANES = 128
NEG_BIG = -1e30
LOG2E = math.log2(math.e)
MAIN_WIDTH = 3 * FOX_WIDTH + 4 * RET_WIDTH + N_BRANCH * D_MODEL
COL_GATE = 0
COL_FQ = N_BRANCH * D_MODEL // HEAD_DIM
COL_FK = COL_FQ + FOX_HEADS
COL_FV = COL_FK + FOX_HEADS
COL_RQ = COL_FV + FOX_HEADS
COL_RK = COL_RQ + RET_HEADS
COL_RV = COL_RK + RET_HEADS
COL_RG = COL_RV + RET_HEADS

ATTN_BLOCK = 256
FOX_HEADS_PER_STEP = 4
RET_HEADS_PER_STEP = 1
RET_CHUNK = 256
MOE_BLOCK = 256
X_BUFS = 3
VMEM_LIMIT = 56 * 1024 * 1024

f32 = jnp.float32
bf16 = jnp.bfloat16
_NT = (((1,), (1,)), ((), ()))
SLAB = D_MODEL // (2 * LANES)
PITCH = SLAB + 1
u32 = jnp.uint32
_HI_MASK = np.uint32(0xFFFF0000)


def _rows_to_slabs(ref2, val, n_rows):
    bits = lambda v: lax.bitcast_convert_type(v.astype(bf16).astype(f32), u32)
    for j in range(SLAB):
        lo = val[:, (2 * j) * LANES:(2 * j + 1) * LANES]
        hi = val[:, (2 * j + 1) * LANES:(2 * j + 2) * LANES]
        ref2[pl.ds(j, n_rows, stride=PITCH), :] = (bits(lo) >> 16) | (bits(hi) & _HI_MASK)
    ref2[pl.ds(SLAB, n_rows, stride=PITCH), :] = jnp.zeros((n_rows, LANES), u32)


def _slabs_to_rows(ref2, n_rows):
    chunks = []
    for j in range(SLAB):
        w = ref2[pl.ds(j, n_rows, stride=PITCH), :]
        chunks.append(lax.bitcast_convert_type(w << 16, f32))
        chunks.append(lax.bitcast_convert_type(w & _HI_MASK, f32))
    return jnp.concatenate(chunks, axis=-1)


def _slab_gather_start(idx_ref, base, n_rows, src_hbm, dst, sem, unroll=8):
    def body(g, _):
        for u in range(unroll):
            r = g * unroll + u
            t = idx_ref[base + r]
            pltpu.make_async_copy(src_hbm.at[pl.ds(t * PITCH, SLAB), :], dst.at[pl.ds(r * PITCH, SLAB), :],
                                  sem).start(priority=u % 2)
        return 0
    lax.fori_loop(0, n_rows // unroll, body, 0)


def _slab_gather_wait(n_rows, src_hbm, dst, sem):
    pltpu.make_async_copy(src_hbm.at[pl.ds(0, n_rows * SLAB), :], dst.at[pl.ds(0, n_rows * SLAB), :], sem).wait()


def _cparams(sem, vmem=VMEM_LIMIT):
    return pltpu.CompilerParams(dimension_semantics=sem, vmem_limit_bytes=vmem)


def _const_spec(shape):
    return pl.BlockSpec(shape, lambda i: tuple(0 for _ in shape), pipeline_mode=pl.Buffered(1))


def _rope_kernel(pos_ref, freq_ref, sgn_ref, cos_ref, sin_ref):
    pos = pos_ref[0].astype(f32)
    hs = pos.shape[0] // 2
    half = HEAD_DIM // 2
    lane = lax.broadcasted_iota(jnp.int32, (hs, HEAD_DIM), 1)
    low = lane < half
    ang = jnp.where(low, pos[:hs], pos[hs:]) * freq_ref[...]
    c = jnp.cos(ang)
    s = jnp.sin(ang) * sgn_ref[...]
    c_sw = pltpu.roll(c, half, axis=1)
    s_sw = pltpu.roll(s, half, axis=1)
    cos_ref[0, :hs, :] = jnp.where(low, c, c_sw)
    cos_ref[0, hs:, :] = jnp.where(low, c_sw, c)
    sin_ref[0, :hs, :] = jnp.where(low, s, -s_sw)
    sin_ref[0, hs:, :] = jnp.where(low, -s_sw, s)


def _rope_tables(positions):
    b, s = positions.shape
    half = HEAD_DIM // 2
    inv = (np.float32(ROPE_BASE) ** (-np.arange(half, dtype=np.float32) / np.float32(half))).astype(np.float32)
    freq = jnp.asarray(np.concatenate([inv, inv])[None, :])
    sgn = jnp.asarray(np.concatenate([-np.ones(half, np.float32), np.ones(half, np.float32)])[None, :])
    out = jax.ShapeDtypeStruct((b, s, HEAD_DIM), f32)
    return pl.pallas_call(
        _rope_kernel,
        grid=(b,),
        in_specs=[pl.BlockSpec((1, s, 1), lambda i: (i, 0, 0)),
                  pl.BlockSpec((1, HEAD_DIM), lambda i: (0, 0)),
                  pl.BlockSpec((1, HEAD_DIM), lambda i: (0, 0))],
        out_specs=[pl.BlockSpec((1, s, HEAD_DIM), lambda i: (i, 0, 0)),
                   pl.BlockSpec((1, s, HEAD_DIM), lambda i: (i, 0, 0))],
        out_shape=[out, out],
        compiler_params=_cparams(("parallel",)),
        name="rope_tables",
    )(positions.reshape(b, s, 1), freq, sgn)


_Q_FACTORS = ((COL_FQ * HEAD_DIM // 1024, LOG2E / math.sqrt(HEAD_DIM)),
              (COL_RQ * HEAD_DIM // 1024, 1.0 / math.sqrt(HEAD_DIM)))


def _wprep_kernel(off_ref, w_hbm, o_ref, buf, sem, *, tn):
    j = pl.program_id(0)
    n = pl.num_programs(0)
    slot = j % 2

    def copy(jj, s):
        src = w_hbm.at[pl.ds(pl.multiple_of(off_ref[jj], 8), tn), :]
        return pltpu.make_async_copy(src, buf.at[s], sem.at[s])

    @pl.when(j == 0)
    def _():
        copy(0, 0).start()

    @pl.when(j + 1 < n)
    def _():
        copy(j + 1, 1 - slot).start()

    copy(j, slot).wait()
    factor = jnp.float32(1.0)
    for tile, value in _Q_FACTORS:
        factor = jnp.where(j == tile, jnp.float32(value), factor)
    o_ref[...] = (buf[slot] * factor).astype(o_ref.dtype)


def _prep_in_weights(wt, row_offsets):
    tn = 1024
    d = wt.shape[1]
    n_tiles = len(row_offsets)
    grid_spec = pltpu.PrefetchScalarGridSpec(
        num_scalar_prefetch=1,
        grid=(n_tiles,),
        in_specs=[pl.BlockSpec(memory_space=pl.ANY)],
        out_specs=pl.BlockSpec((tn, d), lambda j, off: (j, 0)),
        scratch_shapes=[pltpu.VMEM((2, tn, d), f32), pltpu.SemaphoreType.DMA((2,))],
    )
    return pl.pallas_call(
        functools.partial(_wprep_kernel, tn=tn),
        grid_spec=grid_spec,
        out_shape=jax.ShapeDtypeStruct((n_tiles * tn, d), bf16),
        compiler_params=_cparams(("arbitrary",)),
        name="prep_in_weights",
    )(jnp.asarray(row_offsets, jnp.int32), wt)


def _inproj_kernel(x_ref, nw_ref, w_ref, wf_ref, o_ref, f_ref, xn_ref):
    @pl.when(pl.program_id(1) == 0)
    def _():
        x = x_ref[...]
        ms = jnp.mean(x * x, axis=-1, keepdims=True)
        xn = (x * lax.rsqrt(ms + EPS) * nw_ref[...]).astype(bf16)
        xn_ref[...] = xn
        f_ref[...] = lax.dot_general(xn, wf_ref[...], _NT, preferred_element_type=f32)

    o_ref[...] = lax.dot_general(xn_ref[...], w_ref[...], _NT, preferred_element_type=f32).astype(o_ref.dtype)


def _in_projection(x2, norm_w, w_main_t, w_f_t):
    t, d = x2.shape
    n = w_main_t.shape[0]
    tm = min(1024, t)
    tn = 1024
    return pl.pallas_call(
        _inproj_kernel,
        grid=(t // tm, n // tn),
        in_specs=[pl.BlockSpec((tm, d), lambda i, j: (i, 0)),
                  pl.BlockSpec((1, d), lambda i, j: (0, 0)),
                  pl.BlockSpec((tn, d), lambda i, j: (j, 0)),
                  pl.BlockSpec((LANES, d), lambda i, j: (0, 0))],
        out_specs=[pl.BlockSpec((tm, tn), lambda i, j: (i, j)),
                   pl.BlockSpec((tm, LANES), lambda i, j: (i, 0))],
        out_shape=[jax.ShapeDtypeStruct((t, n), bf16), jax.ShapeDtypeStruct((t, LANES), f32)],
        scratch_shapes=[pltpu.VMEM((tm, d), bf16)],
        compiler_params=_cparams(("parallel", "arbitrary")),
        name="in_projection",
    )(x2, norm_w, w_main_t, w_f_t)


def _fcum_kernel(f_ref, b_ref, c_ref):
    z = f_ref[...] + b_ref[...]
    ls = jnp.minimum(z, 0.0) - jnp.log1p(jnp.exp(-jnp.abs(z)))
    c = ls.T[:FOX_HEADS, :]
    s = c.shape[1]
    lane = lax.broadcasted_iota(jnp.int32, c.shape, 1)
    sh = 1
    while sh < s:
        c = c + jnp.where(lane >= sh, pltpu.roll(c, sh, axis=1), 0.0)
        sh *= 2
    c_ref[0] = c * LOG2E


def _forget_cumsum(f_logit, bias, b, s):
    return pl.pallas_call(
        _fcum_kernel,
        grid=(b,),
        in_specs=[pl.BlockSpec((s, LANES), lambda i: (i, 0)),
                  pl.BlockSpec((1, LANES), lambda i: (0, 0))],
        out_specs=pl.BlockSpec((1, FOX_HEADS, s), lambda i: (i, 0, 0)),
        out_shape=jax.ShapeDtypeStruct((b, FOX_HEADS, s), f32),
        compiler_params=_cparams(("parallel",)),
        name="forget_cumsum",
    )(f_logit, bias)


def _fox_kernel(q_ref, k_ref, v_ref, c_ref, o_ref, va_ref, acc_ref, *, tq):
    s_len = q_ref.shape[0]
    nq = s_len // tq
    n_heads = q_ref.shape[1] // HEAD_DIM
    row = lax.broadcasted_iota(jnp.int32, (tq, tq), 0)
    col = lax.broadcasted_iota(jnp.int32, (tq, tq), 1)
    causal = row >= col
    for hh in range(n_heads):
        lanes = slice(hh * HEAD_DIM, (hh + 1) * HEAD_DIM)
        va_ref[hh, :, :HEAD_DIM] = v_ref[:, lanes]
        va_ref[hh, :, HEAD_DIM:] = jnp.ones((s_len, HEAD_DIM), va_ref.dtype)
        panels = []
        row_max = [None] * nq
        for kb in range(nq):
            r0 = kb * tq
            s = lax.dot_general(q_ref[r0:, lanes], k_ref[r0:r0 + tq, lanes], _NT, preferred_element_type=f32)
            s = s - c_ref[hh, :, r0:r0 + tq]
            parts = [jnp.where(causal, s[:tq], NEG_BIG)]
            parts += [s[(j - kb) * tq:(j - kb + 1) * tq] for j in range(kb + 1, nq)]
            panels.append(parts)
            for j, part in zip(range(kb, nq), parts):
                pm = jnp.max(part, axis=-1, keepdims=True)
                row_max[j] = pm if row_max[j] is None else jnp.maximum(row_max[j], pm)
        for kb in range(nq):
            r0 = kb * tq
            p = jnp.concatenate([jnp.exp2(part - row_max[j]).astype(bf16)
                                 for j, part in zip(range(kb, nq), panels[kb])], axis=0)
            contrib = jnp.dot(p, va_ref[hh, r0:r0 + tq, :], preferred_element_type=f32)
            if kb:
                acc_ref[hh, r0:, :] += contrib
            else:
                acc_ref[hh] = contrib
        acc = acc_ref[hh]
        o_ref[:, lanes] = (acc[:, :HEAD_DIM] / acc[:, HEAD_DIM:]).astype(o_ref.dtype)


def _fox_attention(proj, c_blk, b, s):
    tq = min(ATTN_BLOCK, s)
    hp = FOX_HEADS_PER_STEP
    w = hp * HEAD_DIM
    col = lambda c0: pl.BlockSpec((s, w), lambda i, j: (i, c0 // hp + j))
    return pl.pallas_call(
        functools.partial(_fox_kernel, tq=tq),
        grid=(b, FOX_HEADS // hp),
        in_specs=[col(COL_FQ), col(COL_FK), col(COL_FV),
                  pl.BlockSpec((hp, 1, s), lambda i, j: (i * (FOX_HEADS // hp) + j, 0, 0))],
        out_specs=pl.BlockSpec((s, w), lambda i, j: (i, j)),
        out_shape=jax.ShapeDtypeStruct((b * s, FOX_WIDTH), bf16),
        scratch_shapes=[pltpu.VMEM((hp, s, 2 * HEAD_DIM), bf16), pltpu.VMEM((hp, s, 2 * HEAD_DIM), f32)],
        compiler_params=_cparams(("parallel", "parallel")),
        name="fox_attention",
    )(proj, proj, proj, c_blk)


def _ret_kernel(q_ref, k_ref, v_ref, g_ref, cos_ref, sin_ref, dec_ref, xi_ref, zeta_ref, gc_ref, gnw_ref,
                o_ref, *, chunk):
    s_len = q_ref.shape[0]
    nc = s_len // chunk
    n_heads = q_ref.shape[1] // HEAD_DIM
    for hh in range(n_heads):
        lanes = slice(hh * HEAD_DIM, (hh + 1) * HEAD_DIM)
        state = jnp.zeros((HEAD_DIM, HEAD_DIM), f32)
        for ci in range(nc):
            rows = slice(ci * chunk, (ci + 1) * chunk)
            cos = cos_ref[0, rows, :]
            sin = sin_ref[0, rows, :]

            def rot(t):
                t = t.astype(f32)
                return t * cos + pltpu.roll(t, HEAD_DIM // 2, axis=1) * sin

            q = rot(q_ref[rows, lanes])
            k = rot(k_ref[rows, lanes])
            v = v_ref[rows, lanes]
            qb = q.astype(bf16)
            inner = lax.dot_general(qb, k.astype(bf16), _NT, preferred_element_type=f32) * dec_ref[hh]
            o = jnp.dot(inner.astype(bf16), v, preferred_element_type=f32)
            if ci:
                o = o + jnp.dot(qb, state.astype(bf16), preferred_element_type=f32) * xi_ref[hh]
            if ci + 1 < nc:
                kz = (k * zeta_ref[hh]).astype(bf16)
                kv = lax.dot_general(kz, v, (((0,), (0,)), ((), ())), preferred_element_type=f32)
                state = state * gc_ref[hh] + kv if ci else kv
            mu = jnp.mean(o, axis=-1, keepdims=True)
            oc = o - mu
            var = jnp.mean(oc * oc, axis=-1, keepdims=True)
            y = oc * lax.rsqrt(var + EPS) * gnw_ref[:, lanes]
            g = g_ref[rows, lanes].astype(f32)
            o_ref[rows, lanes] = (g * jax.nn.sigmoid(g) * y).astype(o_ref.dtype)


def _retention_constants(chunk):
    hh = np.arange(RET_HEADS, dtype=np.float64)
    log_gamma = np.log1p(-(2.0 ** (-5.0 - hh)))
    n = np.arange(chunk, dtype=np.float64)
    diff = n[:, None] - n[None, :]
    decay = np.where(diff[None] >= 0, np.exp(diff[None] * log_gamma[:, None, None]), 0.0)
    xi = np.exp((n[None, :] + 1.0) * log_gamma[:, None])
    zeta = np.exp((chunk - 1.0 - n[None, :]) * log_gamma[:, None])
    g_chunk = np.exp(chunk * log_gamma)
    bc = lambda a: np.broadcast_to(a[:, :, None], (RET_HEADS, chunk, HEAD_DIM))
    gc = np.broadcast_to(g_chunk[:, None, None], (RET_HEADS, 1, HEAD_DIM))
    to = lambda a: jnp.asarray(np.ascontiguousarray(a), dtype=f32)
    return to(decay), to(bc(xi)), to(bc(zeta)), to(gc)


def _retention(proj, cos, sin, gn_w, b, s):
    chunk = min(RET_CHUNK, s)
    decay, xi, zeta, gc = _retention_constants(chunk)
    hp = RET_HEADS_PER_STEP
    w = hp * HEAD_DIM
    head = lambda col: pl.BlockSpec((s, w), lambda i, j: (i, col // hp + j))
    per_head = lambda shape: pl.BlockSpec((hp,) + shape, lambda i, j: (j, 0, 0))
    return pl.pallas_call(
        functools.partial(_ret_kernel, chunk=chunk),
        grid=(b, RET_HEADS // hp),
        in_specs=[head(COL_RQ), head(COL_RK), head(COL_RV), head(COL_RG),
                  pl.BlockSpec((1, s, HEAD_DIM), lambda i, j: (i, 0, 0)),
                  pl.BlockSpec((1, s, HEAD_DIM), lambda i, j: (i, 0, 0)),
                  per_head((chunk, chunk)), per_head((chunk, HEAD_DIM)), per_head((chunk, HEAD_DIM)),
                  per_head((1, HEAD_DIM)),
                  pl.BlockSpec((1, w), lambda i, j: (0, j))],
        out_specs=pl.BlockSpec((s, w), lambda i, j: (i, j)),
        out_shape=jax.ShapeDtypeStruct((b * s, RET_WIDTH), bf16),
        compiler_params=_cparams(("parallel", "parallel")),
        name="retention",
    )(proj, proj, proj, proj, cos, sin, decay, xi, zeta, gc, gn_w)


def _branch_merge_kernel(fo_ref, ro_ref, g0_ref, g1_ref, bg_ref, wb_ref, mg_ref, *, ncol):
    d = mg_ref.shape[1]
    cw = d // ncol
    fo = fo_ref[...]
    ro = ro_ref[...]
    for c in range(ncol):
        cs = slice(c * cw, (c + 1) * cw)
        bd0 = jnp.dot(fo, wb_ref[0, :, cs], preferred_element_type=f32)
        bd1 = jnp.dot(ro, wb_ref[1, :, cs], preferred_element_type=f32)
        ga = jax.nn.sigmoid(g0_ref[:, cs].astype(f32) + bg_ref[0:1, cs])
        gb = jax.nn.sigmoid(g1_ref[:, cs].astype(f32) + bg_ref[1:2, cs])
        mg_ref[:, cs] = (ga * bd0 + gb * bd1).astype(mg_ref.dtype)


def _branch_merge(fox_o, ret_o, proj, b_gate, w_branch):
    t = fox_o.shape[0]
    d = D_MODEL
    tm = min(1024, t)
    gate0 = COL_GATE * HEAD_DIM // d
    return pl.pallas_call(
        functools.partial(_branch_merge_kernel, ncol=4),
        grid=(t // tm,),
        in_specs=[pl.BlockSpec((tm, FOX_WIDTH), lambda i: (i, 0)),
                  pl.BlockSpec((tm, RET_WIDTH), lambda i: (i, 0)),
                  pl.BlockSpec((tm, d), lambda i: (i, gate0)),
                  pl.BlockSpec((tm, d), lambda i: (i, gate0 + 1)),
                  _const_spec((N_BRANCH, d)),
                  _const_spec((N_BRANCH, FOX_WIDTH, d))],
        out_specs=pl.BlockSpec((tm, d), lambda i: (i, 0)),
        out_shape=jax.ShapeDtypeStruct((t, d), bf16),
        compiler_params=_cparams(("parallel",)),
        name="branch_merge",
    )(fox_o, ret_o, proj, proj, b_gate, w_branch)


def _route(logits):
    lane = lax.broadcasted_iota(jnp.int32, logits.shape, 1).astype(f32)
    far = jnp.float32(1e9)
    gmask = lane < N_GROUPS
    gl = jnp.where(gmask, logits, NEG_BIG)
    gmax = jnp.max(gl, axis=-1, keepdims=True)
    gsel = jnp.min(jnp.where(gl == gmax, lane, far), axis=-1, keepdims=True)
    p_group = 1.0 / jnp.sum(jnp.where(gmask, jnp.exp(gl - gmax), 0.0), axis=-1, keepdims=True)
    lo = N_GROUPS + EXPERTS_PER_GROUP * gsel
    emask = (lane >= lo) & (lane < lo + EXPERTS_PER_GROUP)
    el = jnp.where(emask, logits, NEG_BIG)
    m1 = jnp.max(el, axis=-1, keepdims=True)
    i1 = jnp.min(jnp.where(el == m1, lane, far), axis=-1, keepdims=True)
    el2 = jnp.where(lane == i1, NEG_BIG, el)
    m2 = jnp.max(el2, axis=-1, keepdims=True)
    i2 = jnp.min(jnp.where(el2 == m2, lane, far), axis=-1, keepdims=True)
    t = jnp.exp(m2 - m1)
    w1 = p_group / (1.0 + t)
    w2 = p_group * t / (1.0 + t)
    out = jnp.where(lane == 0, i1 - N_GROUPS, 0.0)
    out = jnp.where(lane == 1, i2 - N_GROUPS, out)
    out = jnp.where(lane == 2, w1, out)
    out = jnp.where(lane == 3, w2, out)
    return out


def _outproj_kernel(mg_ref, x_ref, wo_ref, nfw_ref, wr_ref, br_ref, h_ref, hs_ref, route_ref, cnt_ref):
    h = x_ref[...] + jnp.dot(mg_ref[...], wo_ref[...], preferred_element_type=f32)
    h_ref[...] = h
    hn = h * lax.rsqrt(jnp.mean(h * h, axis=-1, keepdims=True) + EPS) * nfw_ref[...]
    _rows_to_slabs(hs_ref, hn, hn.shape[0])
    hi = hn.astype(bf16)
    lo = (hn - hi.astype(f32)).astype(bf16)
    hw = jnp.dot(hi, wr_ref[...], preferred_element_type=f32)
    logits = (hw[:, :LANES] + jnp.dot(lo, wr_ref[:, :LANES], preferred_element_type=f32) + hw[:, LANES:]) + br_ref[...]
    route = _route(logits)
    route_ref[...] = route

    @pl.when(pl.program_id(0) == 0)
    def _():
        cnt_ref[...] = jnp.zeros_like(cnt_ref)

    lane = lax.broadcasted_iota(jnp.int32, route.shape, 1).astype(f32)
    chosen = (lane == route[:, 0:1]).astype(f32) + (lane == route[:, 1:2]).astype(f32)
    cnt_ref[...] += jnp.sum(chosen, axis=0, keepdims=True)


def _outproj_router(merged, x2, w_out, nfw, wr_cat, b_router):
    t, d = x2.shape
    tm = min(512, t)
    return pl.pallas_call(
        _outproj_kernel,
        grid=(t // tm,),
        in_specs=[pl.BlockSpec((tm, d), lambda i: (i, 0)),
                  pl.BlockSpec((tm, d), lambda i: (i, 0)),
                  _const_spec((d, d)),
                  _const_spec((1, d)),
                  _const_spec((d, 2 * LANES)), _const_spec((1, LANES))],
        out_specs=[pl.BlockSpec((tm, d), lambda i: (i, 0)),
                   pl.BlockSpec((tm * PITCH, LANES), lambda i: (i, 0)),
                   pl.BlockSpec((tm, LANES), lambda i: (i, 0)),
                   pl.BlockSpec((8, LANES), lambda i: (0, 0))],
        out_shape=[jax.ShapeDtypeStruct((t, d), f32), jax.ShapeDtypeStruct((t * PITCH, LANES), u32),
                   jax.ShapeDtypeStruct((t, LANES), f32), jax.ShapeDtypeStruct((8, LANES), f32)],
        compiler_params=_cparams(("arbitrary",)),
        name="outproj_router",
    )(merged, x2, w_out, nfw, wr_cat, b_router)


def _plan_kernel(route_ref, cnt_ref, dest_ref, ends_ref, carry_ref, *, bm):
    r = route_ref[...]
    tb = r.shape[0]
    lane = lax.broadcasted_iota(jnp.int32, r.shape, 1).astype(f32)
    o0 = (lane == r[:, 0:1]).astype(f32)
    o1 = (lane == r[:, 1:2]).astype(f32)
    o = o0 + o1

    @pl.when(pl.program_id(0) == 0)
    def _():
        cnt = cnt_ref[...]
        padded = jnp.floor((cnt + (bm - 1)) * (1.0 / bm)) * bm
        lane8 = lax.broadcasted_iota(jnp.int32, cnt.shape, 1)
        ends = padded
        sh = 1
        while sh < LANES:
            ends = ends + jnp.where(lane8 >= sh, pltpu.roll(ends, sh, axis=1), 0.0)
            sh *= 2
        sub = lax.broadcasted_iota(jnp.int32, cnt.shape, 0)
        ends_ref[...] = jnp.where(sub == 0, ends, jnp.where(sub == 1, padded, cnt))
        carry_ref[...] = ends - padded

    row = lax.broadcasted_iota(jnp.int32, (tb, tb), 0)
    col = lax.broadcasted_iota(jnp.int32, (tb, tb), 1)
    lower = jnp.where(col < row, 1.0, 0.0).astype(bf16)
    before = jnp.dot(lower, o.astype(bf16), preferred_element_type=f32) + carry_ref[0:1, :]
    d0 = jnp.sum(before * o0, axis=-1, keepdims=True)
    d1 = jnp.sum(before * o1, axis=-1, keepdims=True)
    dd = jnp.where(lane == 0, d0, jnp.where(lane == 1, d1, 0.0))
    dest_ref[0] = dd.T[:8, :].astype(jnp.int32)
    carry_ref[...] += jnp.sum(o, axis=0, keepdims=True)


def _dispatch_plan(route, counts, bm):
    t = route.shape[0]
    tb = min(1024, t)
    nb = t // tb
    return pl.pallas_call(
        functools.partial(_plan_kernel, bm=bm),
        grid=(nb,),
        in_specs=[pl.BlockSpec((tb, LANES), lambda i: (i, 0)),
                  pl.BlockSpec((8, LANES), lambda i: (0, 0))],
        out_specs=[pl.BlockSpec((1, 8, tb), lambda i: (i, 0, 0)),
                   pl.BlockSpec((8, LANES), lambda i: (0, 0))],
        out_shape=[jax.ShapeDtypeStruct((nb, 8, tb), jnp.int32), jax.ShapeDtypeStruct((8, LANES), f32)],
        scratch_shapes=[pltpu.VMEM((8, LANES), f32)],
        compiler_params=_cparams(("arbitrary",)),
        name="dispatch_plan",
    )(route, counts)


def _dispatch_kernel(d0_ref, d1_ref, fill_ref, end_ref, nu_ref, hs_ref, xs_hbm, zbuf, sem, zsem,
                     *, tb, bm, n_blk, unroll=8):
    i = pl.program_id(0)
    base = i * tb
    blk_rows = bm * PITCH

    def pad_copy(s):
        return pltpu.make_async_copy(zbuf.at[pl.ds(0, PITCH), :], xs_hbm.at[pl.ds(s * PITCH, PITCH), :], zsem)

    def tail_copy(g):
        return pltpu.make_async_copy(zbuf, xs_hbm.at[pl.ds(pl.multiple_of(g * blk_rows, 8), blk_rows), :], zsem)

    @pl.when(i == 0)
    def _():
        zbuf[...] = jnp.zeros(zbuf.shape, u32)
        for op in ("start", "wait"):
            def per_expert(e, c, op=op):
                def one(s, c2):
                    getattr(pad_copy(s), op)()
                    return c2
                return lax.fori_loop(fill_ref[e], end_ref[e], one, c)

            def per_tail(g, c, op=op):
                getattr(tail_copy(g), op)()
                return c

            lax.fori_loop(0, N_EXPERTS, per_expert, 0)
            lax.fori_loop(nu_ref[0], n_blk, per_tail, 0)

    def body(g, c):
        for u in range(unroll):
            r = g * unroll + u
            src = hs_ref.at[pl.ds(r * PITCH, PITCH), :]
            for k, dref in enumerate((d0_ref, d1_ref)):
                d = dref[base + r]
                pltpu.make_async_copy(src, xs_hbm.at[pl.ds(d * PITCH, PITCH), :], sem).start(priority=k)
        return c

    lax.fori_loop(0, tb // unroll, body, 0)
    for _ in range(TOP_K):
        pltpu.make_async_copy(xs_hbm.at[pl.ds(0, tb * PITCH), :], xs_hbm.at[pl.ds(0, tb * PITCH), :], sem).wait()


def _dispatch_rows(dest0, dest1, fill_start, seg_end, n_used, hs, cap):
    t = dest0.shape[0]
    tb = min(1024, t)
    bm = MOE_BLOCK
    grid_spec = pltpu.PrefetchScalarGridSpec(
        num_scalar_prefetch=5,
        grid=(t // tb,),
        in_specs=[pl.BlockSpec((tb * PITCH, LANES), lambda i, *_: (i, 0))],
        out_specs=pl.BlockSpec(memory_space=pl.ANY),
        scratch_shapes=[pltpu.VMEM((bm * PITCH, LANES), u32), pltpu.SemaphoreType.DMA(()),
                        pltpu.SemaphoreType.DMA(())],
    )
    return pl.pallas_call(
        functools.partial(_dispatch_kernel, tb=tb, bm=bm, n_blk=cap // bm),
        grid_spec=grid_spec,
        out_shape=jax.ShapeDtypeStruct((cap * PITCH, LANES), u32),
        compiler_params=_cparams(("arbitrary",)),
        name="dispatch_rows",
    )(dest0, dest1, fill_start, seg_end, n_used, hs)


def _moe_kernel(g0_ref, nb_ref, nu_ref, x_hbm, w1_ref, w3_ref, w2_ref, y_hbm,
                xbuf, ybuf, w1b, w3b, w2b, gsem, ysem, *, bm, n_blk):
    e = pl.program_id(0)
    n_used = nu_ref[0]
    g0 = g0_ref[e]
    nb = nb_ref[e]
    blk_rows = bm * PITCH

    def x_copy(slot, g):
        src = x_hbm.at[pl.ds(pl.multiple_of(g * blk_rows, 8), blk_rows), :]
        return pltpu.make_async_copy(src, xbuf.at[slot], gsem.at[slot])

    def y_copy(slot, g):
        dst = y_hbm.at[pl.ds(pl.multiple_of(g * blk_rows, 8), blk_rows), :]
        return pltpu.make_async_copy(ybuf.at[slot], dst, ysem.at[slot])

    @pl.when(e == 0)
    def _():
        x_copy(0, 0).start()

        @pl.when(n_used > 1)
        def _():
            x_copy(1, 1).start()

    @pl.when(nb > 0)
    def _():
        w1b[...] = w1_ref[0].astype(bf16)
        w3b[...] = w3_ref[0].astype(bf16)
        w2b[...] = w2_ref[0].astype(bf16)

    def block(c, carry):
        g = g0 + c
        slot = g % 2
        xslot = g % X_BUFS

        @pl.when(g + 2 < n_used)
        def _():
            x_copy((g + 2) % X_BUFS, g + 2).start()

        x_copy(xslot, g).wait()

        hn = _slabs_to_rows(xbuf.at[xslot], bm).astype(bf16)
        a = jnp.dot(hn, w1b[...], preferred_element_type=f32)
        b = jnp.dot(hn, w3b[...], preferred_element_type=f32)
        mid = (a * jax.nn.sigmoid(a) * b).astype(bf16)
        y = jnp.dot(mid, w2b[...], preferred_element_type=f32)

        @pl.when(g >= 2)
        def _():
            y_copy(slot, g - 2).wait()

        _rows_to_slabs(ybuf.at[slot], y, bm)
        y_copy(slot, g).start()
        return carry

    lax.fori_loop(0, nb, block, 0)

    @pl.when(e == pl.num_programs(0) - 1)
    def _():
        @pl.when(n_used >= 2)
        def _():
            y_copy(n_used % 2, n_used - 2).wait()

        y_copy((n_used - 1) % 2, n_used - 1).wait()
        ybuf[0] = jnp.zeros(ybuf.shape[1:], u32)

        def zstart(g, carry):
            y_copy(0, g).start()
            return carry

        def zwait(g, carry):
            y_copy(0, g).wait()
            return carry

        lax.fori_loop(n_used, n_blk, zstart, 0)
        lax.fori_loop(n_used, n_blk, zwait, 0)


def _expert_mlp(blk_start, blk_count, n_used, xs, w1, w3, w2, cap):
    d = D_MODEL
    bm = MOE_BLOCK
    n_blk = cap // bm
    by_expert = lambda shape: pl.BlockSpec(shape, lambda e, g0, nb, nu: (e, 0, 0))
    grid_spec = pltpu.PrefetchScalarGridSpec(
        num_scalar_prefetch=3,
        grid=(N_EXPERTS,),
        in_specs=[pl.BlockSpec(memory_space=pl.ANY),
                  by_expert((1, d, D_EXPERT)), by_expert((1, d, D_EXPERT)), by_expert((1, D_EXPERT, d))],
        out_specs=pl.BlockSpec(memory_space=pl.ANY),
        scratch_shapes=[pltpu.VMEM((X_BUFS, bm * PITCH, LANES), u32), pltpu.VMEM((2, bm * PITCH, LANES), u32),
                        pltpu.VMEM((d, D_EXPERT), bf16), pltpu.VMEM((d, D_EXPERT), bf16),
                        pltpu.VMEM((D_EXPERT, d), bf16),
                        pltpu.SemaphoreType.DMA((X_BUFS,)), pltpu.SemaphoreType.DMA((2,))],
    )
    return pl.pallas_call(
        functools.partial(_moe_kernel, bm=bm, n_blk=n_blk),
        grid_spec=grid_spec,
        out_shape=jax.ShapeDtypeStruct((cap * PITCH, LANES), u32),
        compiler_params=_cparams(("arbitrary",)),
        name="expert_mlp",
    )(blk_start, blk_count, n_used, xs, w1, w3, w2)


def _combine_kernel(d0_ref, d1_ref, h_ref, route_ref, y_hbm, nw_ref, o_ref, ybuf, sem, *, tb):
    i = pl.program_id(0)
    n = pl.num_programs(0)
    slot = i % 2

    def start(blk, s):
        _slab_gather_start(d0_ref, blk * tb, tb, y_hbm, ybuf.at[s, 0], sem.at[s])
        _slab_gather_start(d1_ref, blk * tb, tb, y_hbm, ybuf.at[s, 1], sem.at[s])

    @pl.when(i == 0)
    def _():
        start(0, 0)

    @pl.when(i + 1 < n)
    def _():
        start(i + 1, 1 - slot)

    for k in range(TOP_K):
        _slab_gather_wait(tb, y_hbm, ybuf.at[slot, k], sem.at[slot])
    r = route_ref[...]
    h = h_ref[...] + (r[:, 2:3] * _slabs_to_rows(ybuf.at[slot, 0], tb)
                      + r[:, 3:4] * _slabs_to_rows(ybuf.at[slot, 1], tb))
    o_ref[...] = h * lax.rsqrt(jnp.mean(h * h, axis=-1, keepdims=True) + EPS) * nw_ref[...]


def _combine(dest0, dest1, h, route, ys, norm_w):
    t, d = h.shape
    tb = min(256, t)
    grid_spec = pltpu.PrefetchScalarGridSpec(
        num_scalar_prefetch=2,
        grid=(t // tb,),
        in_specs=[pl.BlockSpec((tb, d), lambda i, a, b: (i, 0)),
                  pl.BlockSpec((tb, LANES), lambda i, a, b: (i, 0)),
                  pl.BlockSpec(memory_space=pl.ANY),
                  pl.BlockSpec((1, d), lambda i, a, b: (0, 0))],
        out_specs=pl.BlockSpec((tb, d), lambda i, a, b: (i, 0)),
        scratch_shapes=[pltpu.VMEM((2, TOP_K, tb * PITCH, LANES), u32), pltpu.SemaphoreType.DMA((2,))],
    )
    return pl.pallas_call(
        functools.partial(_combine_kernel, tb=tb),
        grid_spec=grid_spec,
        out_shape=jax.ShapeDtypeStruct((t, d), f32),
        compiler_params=_cparams(("arbitrary",)),
        name="combine_final_norm",
    )(dest0, dest1, h, route, ys, norm_w)


def kernel(x, positions, norm_mix_w, w_in, fox_b_f, ret_gn_w, w_branch, b_gate, w_out, norm_ffn_w, w_router_group,
           b_router_group, w_router_expert, b_router_expert, w1, w3, w2, norm_final_w):
    b, s, d = x.shape
    assert d == D_MODEL and norm_mix_w.shape[0] == 1
    t = b * s
    x2 = x.reshape(t, d)

    wt = w_in[0].T
    o_f = 3 * FOX_WIDTH
    o_r = o_f + FOX_HEADS
    o_g = o_r + 4 * RET_WIDTH
    tile_rows = ([o_g + k * 1024 for k in range(N_BRANCH * D_MODEL // 1024)]
                 + [k * 1024 for k in range(o_f // 1024)]
                 + [o_r + k * 1024 for k in range(4 * RET_WIDTH // 1024)])
    w_main = _prep_in_weights(wt, tile_rows)
    w_f = jnp.pad(wt[o_f:o_r], ((0, LANES - FOX_HEADS), (0, 0))).astype(bf16)
    bias_f = jnp.pad(fox_b_f[0], (0, LANES - FOX_HEADS)).reshape(1, LANES)
    w_r = jnp.concatenate([w_router_group[0], w_router_expert[0]], axis=1)
    w_r = jnp.pad(w_r, ((0, 0), (0, LANES - w_r.shape[1])))
    wr_hi = w_r.astype(bf16)
    wr_lo = (w_r - wr_hi.astype(f32)).astype(bf16)
    b_r = jnp.pad(jnp.concatenate([b_router_group[0], b_router_expert[0]]), (0, LANES - N_GROUPS - N_EXPERTS))
    b_r = b_r.reshape(1, LANES)

    cos, sin = _rope_tables(positions)
    proj, f_logit = _in_projection(x2, norm_mix_w, w_main, w_f)
    c = _forget_cumsum(f_logit, bias_f, b, s)
    fox_o = _fox_attention(proj, c.reshape(b * FOX_HEADS, 1, s), b, s)
    ret_o = _retention(proj, cos, sin, ret_gn_w, b, s)
    merged = _branch_merge(fox_o, ret_o, proj, b_gate[0], w_branch[0].astype(bf16))
    h, h_slabs, route, counts = _outproj_router(merged, x2, w_out[0].astype(bf16), norm_ffn_w,
                                        jnp.concatenate([wr_hi, wr_lo], axis=1), b_r)

    bm = MOE_BLOCK
    cap = t * TOP_K + N_EXPERTS * bm
    dest_rows, seg = _dispatch_plan(route, counts, bm)
    dest0 = dest_rows[:, 0, :].reshape(t)
    dest1 = dest_rows[:, 1, :].reshape(t)
    pends = seg[0, :N_EXPERTS].astype(jnp.int32)
    padded = seg[1, :N_EXPERTS].astype(jnp.int32)
    counts = seg[2, :N_EXPERTS].astype(jnp.int32)
    pstarts = pends - padded
    n_used = pends[-1:] // bm

    xs = _dispatch_rows(dest0, dest1, pstarts + counts, pends, n_used, h_slabs, cap)
    y = _expert_mlp(pstarts // bm, padded // bm, n_used, xs, w1[0], w3[0], w2[0], cap)
    out = _combine(dest0, dest1, h, route, y, norm_final_w.reshape(1, d))
    return out.reshape(b, s, d)
```

```python
import functools
import math

import numpy as np
import jax
import jax.numpy as jnp
from jax import lax
from jax.experimental import pallas as pl
from jax.experimental.pallas import tpu as pltpu

D_MODEL = 2048
HEAD_DIM = 128
FOX_HEADS = 8
RET_HEADS = 8
FOX_WIDTH = FOX_HEADS * HEAD_DIM
RET_WIDTH = RET_HEADS * HEAD_DIM
N_BRANCH = 2
N_GROUPS = 4
EXPERTS_PER_GROUP = 8
N_EXPERTS = N_GROUPS * EXPERTS_PER_GROUP
TOP_K = 2
D_EXPERT = 512
ROPE_BASE = 10000.0
EPS = 1e-6

LANES = 128
NEG_BIG = -1e30
LOG2E = math.log2(math.e)
MAIN_WIDTH = 3 * FOX_WIDTH + 4 * RET_WIDTH + N_BRANCH * D_MODEL
COL_GATE = 0
COL_FQ = N_BRANCH * D_MODEL // HEAD_DIM
COL_FK = COL_FQ + FOX_HEADS
COL_FV = COL_FK + FOX_HEADS
COL_RQ = COL_FV + FOX_HEADS
COL_RK = COL_RQ + RET_HEADS
COL_RV = COL_RK + RET_HEADS
COL_RG = COL_RV + RET_HEADS

ATTN_BLOCK = 256
FOX_HEADS_PER_STEP = 4
RET_HEADS_PER_STEP = 1
RET_CHUNK = 256
MOE_BLOCK = 256
X_BUFS = 4
VMEM_LIMIT = 56 * 1024 * 1024

f32 = jnp.float32
bf16 = jnp.bfloat16
_NT = (((1,), (1,)), ((), ()))
SLAB = D_MODEL // (2 * LANES)
PITCH = SLAB + 1
u32 = jnp.uint32
_HI_MASK = np.uint32(0xFFFF0000)


def _rows_to_slabs(ref2, val, n_rows):
    bits = lambda v: lax.bitcast_convert_type(v.astype(bf16).astype(f32), u32)
    for j in range(SLAB):
        lo = val[:, (2 * j) * LANES:(2 * j + 1) * LANES]
        hi = val[:, (2 * j + 1) * LANES:(2 * j + 2) * LANES]
        ref2[pl.ds(j, n_rows, stride=PITCH), :] = (bits(lo) >> 16) | (bits(hi) & _HI_MASK)
    ref2[pl.ds(SLAB, n_rows, stride=PITCH), :] = jnp.zeros((n_rows, LANES), u32)


def _slabs_to_rows(ref2, n_rows):
    chunks = []
    for j in range(SLAB):
        w = ref2[pl.ds(j, n_rows, stride=PITCH), :]
        chunks.append(lax.bitcast_convert_type(w << 16, f32))
        chunks.append(lax.bitcast_convert_type(w & _HI_MASK, f32))
    return jnp.concatenate(chunks, axis=-1)


def _slab_gather_start(idx_ref, base, n_rows, src_hbm, dst, sem, unroll=8):
    def body(g, _):
        for u in range(unroll):
            r = g * unroll + u
            t = idx_ref[base + r]
            pltpu.make_async_copy(src_hbm.at[pl.ds(t * PITCH, SLAB), :], dst.at[pl.ds(r * PITCH, SLAB), :],
                                  sem).start(priority=u % 2)
        return 0
    lax.fori_loop(0, n_rows // unroll, body, 0)


def _slab_gather_wait(n_rows, src_hbm, dst, sem):
    pltpu.make_async_copy(src_hbm.at[pl.ds(0, n_rows * SLAB), :], dst.at[pl.ds(0, n_rows * SLAB), :], sem).wait()


def _cparams(sem, vmem=VMEM_LIMIT):
    return pltpu.CompilerParams(dimension_semantics=sem, vmem_limit_bytes=vmem)


def _const_spec(shape):
    return pl.BlockSpec(shape, lambda i: tuple(0 for _ in shape), pipeline_mode=pl.Buffered(1))


def _rope_kernel(pos_ref, freq_ref, sgn_ref, cos_ref, sin_ref):
    pos = pos_ref[0].astype(f32)
    hs = pos.shape[0] // 2
    half = HEAD_DIM // 2
    lane = lax.broadcasted_iota(jnp.int32, (hs, HEAD_DIM), 1)
    low = lane < half
    ang = jnp.where(low, pos[:hs], pos[hs:]) * freq_ref[...]
    c = jnp.cos(ang)
    s = jnp.sin(ang) * sgn_ref[...]
    c_sw = pltpu.roll(c, half, axis=1)
    s_sw = pltpu.roll(s, half, axis=1)
    cos_ref[0, :hs, :] = jnp.where(low, c, c_sw)
    cos_ref[0, hs:, :] = jnp.where(low, c_sw, c)
    sin_ref[0, :hs, :] = jnp.where(low, s, -s_sw)
    sin_ref[0, hs:, :] = jnp.where(low, -s_sw, s)


def _rope_tables(positions):
    b, s = positions.shape
    half = HEAD_DIM // 2
    inv = (np.float32(ROPE_BASE) ** (-np.arange(half, dtype=np.float32) / np.float32(half))).astype(np.float32)
    freq = jnp.asarray(np.concatenate([inv, inv])[None, :])
    sgn = jnp.asarray(np.concatenate([-np.ones(half, np.float32), np.ones(half, np.float32)])[None, :])
    out = jax.ShapeDtypeStruct((b, s, HEAD_DIM), f32)
    return pl.pallas_call(
        _rope_kernel,
        grid=(b,),
        in_specs=[pl.BlockSpec((1, s, 1), lambda i: (i, 0, 0)),
                  pl.BlockSpec((1, HEAD_DIM), lambda i: (0, 0)),
                  pl.BlockSpec((1, HEAD_DIM), lambda i: (0, 0))],
        out_specs=[pl.BlockSpec((1, s, HEAD_DIM), lambda i: (i, 0, 0)),
                   pl.BlockSpec((1, s, HEAD_DIM), lambda i: (i, 0, 0))],
        out_shape=[out, out],
        compiler_params=_cparams(("parallel",)),
        name="rope_tables",
    )(positions.reshape(b, s, 1), freq, sgn)


_Q_FACTORS = ((COL_FQ * HEAD_DIM // 1024, LOG2E / math.sqrt(HEAD_DIM)),
              (COL_RQ * HEAD_DIM // 1024, 1.0 / math.sqrt(HEAD_DIM)))


def _wprep_kernel(off_ref, w_hbm, o_ref, buf, sem, *, tn):
    j = pl.program_id(0)
    n = pl.num_programs(0)
    slot = j % 2

    def copy(jj, s):
        src = w_hbm.at[pl.ds(pl.multiple_of(off_ref[jj], 8), tn), :]
        return pltpu.make_async_copy(src, buf.at[s], sem.at[s])

    @pl.when(j == 0)
    def _():
        copy(0, 0).start()

    @pl.when(j + 1 < n)
    def _():
        copy(j + 1, 1 - slot).start()

    copy(j, slot).wait()
    factor = jnp.float32(1.0)
    for tile, value in _Q_FACTORS:
        factor = jnp.where(j == tile, jnp.float32(value), factor)
    o_ref[...] = (buf[slot] * factor).astype(o_ref.dtype)


def _prep_in_weights(wt, row_offsets):
    tn = 1024
    d = wt.shape[1]
    n_tiles = len(row_offsets)
    grid_spec = pltpu.PrefetchScalarGridSpec(
        num_scalar_prefetch=1,
        grid=(n_tiles,),
        in_specs=[pl.BlockSpec(memory_space=pl.ANY)],
        out_specs=pl.BlockSpec((tn, d), lambda j, off: (j, 0)),
        scratch_shapes=[pltpu.VMEM((2, tn, d), f32), pltpu.SemaphoreType.DMA((2,))],
    )
    return pl.pallas_call(
        functools.partial(_wprep_kernel, tn=tn),
        grid_spec=grid_spec,
        out_shape=jax.ShapeDtypeStruct((n_tiles * tn, d), bf16),
        compiler_params=_cparams(("arbitrary",)),
        name="prep_in_weights",
    )(jnp.asarray(row_offsets, jnp.int32), wt)


def _inproj_kernel(x_ref, nw_ref, w_ref, wf_ref, o_ref, f_ref, xn_ref):
    @pl.when(pl.program_id(1) == 0)
    def _():
        x = x_ref[...]
        ms = jnp.mean(x * x, axis=-1, keepdims=True)
        xn = (x * lax.rsqrt(ms + EPS) * nw_ref[...]).astype(bf16)
        xn_ref[...] = xn
        f_ref[...] = lax.dot_general(xn, wf_ref[...], _NT, preferred_element_type=f32)

    o_ref[...] = lax.dot_general(xn_ref[...], w_ref[...], _NT, preferred_element_type=f32).astype(o_ref.dtype)


def _in_projection(x2, norm_w, w_main_t, w_f_t):
    t, d = x2.shape
    n = w_main_t.shape[0]
    tm = min(1024, t)
    tn = 1024
    return pl.pallas_call(
        _inproj_kernel,
        grid=(t // tm, n // tn),
        in_specs=[pl.BlockSpec((tm, d), lambda i, j: (i, 0)),
                  pl.BlockSpec((1, d), lambda i, j: (0, 0)),
                  pl.BlockSpec((tn, d), lambda i, j: (j, 0)),
                  pl.BlockSpec((LANES, d), lambda i, j: (0, 0))],
        out_specs=[pl.BlockSpec((tm, tn), lambda i, j: (i, j)),
                   pl.BlockSpec((tm, LANES), lambda i, j: (i, 0))],
        out_shape=[jax.ShapeDtypeStruct((t, n), bf16), jax.ShapeDtypeStruct((t, LANES), f32)],
        scratch_shapes=[pltpu.VMEM((tm, d), bf16)],
        compiler_params=_cparams(("parallel", "arbitrary")),
        name="in_projection",
    )(x2, norm_w, w_main_t, w_f_t)


def _fcum_kernel(f_ref, b_ref, c_ref):
    z = f_ref[...] + b_ref[...]
    ls = jnp.minimum(z, 0.0) - jnp.log1p(jnp.exp(-jnp.abs(z)))
    c = ls.T[:FOX_HEADS, :]
    s = c.shape[1]
    lane = lax.broadcasted_iota(jnp.int32, c.shape, 1)
    sh = 1
    while sh < s:
        c = c + jnp.where(lane >= sh, pltpu.roll(c, sh, axis=1), 0.0)
        sh *= 2
    c_ref[0] = c * LOG2E


def _forget_cumsum(f_logit, bias, b, s):
    return pl.pallas_call(
        _fcum_kernel,
        grid=(b,),
        in_specs=[pl.BlockSpec((s, LANES), lambda i: (i, 0)),
                  pl.BlockSpec((1, LANES), lambda i: (0, 0))],
        out_specs=pl.BlockSpec((1, FOX_HEADS, s), lambda i: (i, 0, 0)),
        out_shape=jax.ShapeDtypeStruct((b, FOX_HEADS, s), f32),
        compiler_params=_cparams(("parallel",)),
        name="forget_cumsum",
    )(f_logit, bias)


def _fox_kernel(q_ref, k_ref, v_ref, c_ref, o_ref, va_ref, acc_ref, *, tq):
    s_len = q_ref.shape[0]
    nq = s_len // tq
    n_heads = q_ref.shape[1] // HEAD_DIM
    row = lax.broadcasted_iota(jnp.int32, (tq, tq), 0)
    col = lax.broadcasted_iota(jnp.int32, (tq, tq), 1)
    causal = row >= col
    for hh in range(n_heads):
        lanes = slice(hh * HEAD_DIM, (hh + 1) * HEAD_DIM)
        va_ref[hh, :, :HEAD_DIM] = v_ref[:, lanes]
        va_ref[hh, :, HEAD_DIM:] = jnp.ones((s_len, HEAD_DIM), va_ref.dtype)
        panels = []
        row_max = [None] * nq
        for kb in range(nq):
            r0 = kb * tq
            s = lax.dot_general(q_ref[r0:, lanes], k_ref[r0:r0 + tq, lanes], _NT, preferred_element_type=f32)
            s = s - c_ref[hh, :, r0:r0 + tq]
            parts = [jnp.where(causal, s[:tq], NEG_BIG)]
            parts += [s[(j - kb) * tq:(j - kb + 1) * tq] for j in range(kb + 1, nq)]
            panels.append(parts)
            for j, part in zip(range(kb, nq), parts):
                pm = jnp.max(part, axis=-1, keepdims=True)
                row_max[j] = pm if row_max[j] is None else jnp.maximum(row_max[j], pm)
        for kb in range(nq):
            r0 = kb * tq
            p = jnp.concatenate([jnp.exp2(part - row_max[j]).astype(bf16)
                                 for j, part in zip(range(kb, nq), panels[kb])], axis=0)
            contrib = jnp.dot(p, va_ref[hh, r0:r0 + tq, :], preferred_element_type=f32)
            if kb:
                acc_ref[hh, r0:, :] += contrib
            else:
                acc_ref[hh] = contrib
        acc = acc_ref[hh]
        o_ref[:, lanes] = (acc[:, :HEAD_DIM] / acc[:, HEAD_DIM:]).astype(o_ref.dtype)


def _fox_attention(proj, c_blk, b, s):
    tq = min(ATTN_BLOCK, s)
    hp = FOX_HEADS_PER_STEP
    w = hp * HEAD_DIM
    col = lambda c0: pl.BlockSpec((s, w), lambda i, j: (i, c0 // hp + j))
    return pl.pallas_call(
        functools.partial(_fox_kernel, tq=tq),
        grid=(b, FOX_HEADS // hp),
        in_specs=[col(COL_FQ), col(COL_FK), col(COL_FV),
                  pl.BlockSpec((hp, 1, s), lambda i, j: (i * (FOX_HEADS // hp) + j, 0, 0))],
        out_specs=pl.BlockSpec((s, w), lambda i, j: (i, j)),
        out_shape=jax.ShapeDtypeStruct((b * s, FOX_WIDTH), bf16),
        scratch_shapes=[pltpu.VMEM((hp, s, 2 * HEAD_DIM), bf16), pltpu.VMEM((hp, s, 2 * HEAD_DIM), f32)],
        compiler_params=_cparams(("parallel", "parallel")),
        name="fox_attention",
    )(proj, proj, proj, c_blk)


def _ret_kernel(q_ref, k_ref, v_ref, g_ref, cos_ref, sin_ref, dec_ref, xi_ref, zeta_ref, gc_ref, gnw_ref,
                o_ref, *, chunk):
    s_len = q_ref.shape[0]
    nc = s_len // chunk
    n_heads = q_ref.shape[1] // HEAD_DIM
    for hh in range(n_heads):
        lanes = slice(hh * HEAD_DIM, (hh + 1) * HEAD_DIM)
        state = jnp.zeros((HEAD_DIM, HEAD_DIM), f32)
        for ci in range(nc):
            rows = slice(ci * chunk, (ci + 1) * chunk)
            cos = cos_ref[0, rows, :]
            sin = sin_ref[0, rows, :]

            def rot(t):
                t = t.astype(f32)
                return t * cos + pltpu.roll(t, HEAD_DIM // 2, axis=1) * sin

            q = rot(q_ref[rows, lanes])
            k = rot(k_ref[rows, lanes])
            v = v_ref[rows, lanes]
            qb = q.astype(bf16)
            inner = lax.dot_general(qb, k.astype(bf16), _NT, preferred_element_type=f32) * dec_ref[hh]
            o = jnp.dot(inner.astype(bf16), v, preferred_element_type=f32)
            if ci:
                o = o + jnp.dot(qb, state.astype(bf16), preferred_element_type=f32) * xi_ref[hh]
            if ci + 1 < nc:
                kz = (k * zeta_ref[hh]).astype(bf16)
                kv = lax.dot_general(kz, v, (((0,), (0,)), ((), ())), preferred_element_type=f32)
                state = state * gc_ref[hh] + kv if ci else kv
            mu = jnp.mean(o, axis=-1, keepdims=True)
            oc = o - mu
            var = jnp.mean(oc * oc, axis=-1, keepdims=True)
            y = oc * lax.rsqrt(var + EPS) * gnw_ref[:, lanes]
            g = g_ref[rows, lanes].astype(f32)
            o_ref[rows, lanes] = (g * jax.nn.sigmoid(g) * y).astype(o_ref.dtype)


def _retention_constants(chunk):
    hh = np.arange(RET_HEADS, dtype=np.float64)
    log_gamma = np.log1p(-(2.0 ** (-5.0 - hh)))
    n = np.arange(chunk, dtype=np.float64)
    diff = n[:, None] - n[None, :]
    decay = np.where(diff[None] >= 0, np.exp(diff[None] * log_gamma[:, None, None]), 0.0)
    xi = np.exp((n[None, :] + 1.0) * log_gamma[:, None])
    zeta = np.exp((chunk - 1.0 - n[None, :]) * log_gamma[:, None])
    g_chunk = np.exp(chunk * log_gamma)
    bc = lambda a: np.broadcast_to(a[:, :, None], (RET_HEADS, chunk, HEAD_DIM))
    gc = np.broadcast_to(g_chunk[:, None, None], (RET_HEADS, 1, HEAD_DIM))
    to = lambda a: jnp.asarray(np.ascontiguousarray(a), dtype=f32)
    return to(decay), to(bc(xi)), to(bc(zeta)), to(gc)


def _retention(proj, cos, sin, gn_w, b, s):
    chunk = min(RET_CHUNK, s)
    decay, xi, zeta, gc = _retention_constants(chunk)
    hp = RET_HEADS_PER_STEP
    w = hp * HEAD_DIM
    head = lambda col: pl.BlockSpec((s, w), lambda i, j: (i, col // hp + j))
    per_head = lambda shape: pl.BlockSpec((hp,) + shape, lambda i, j: (j, 0, 0))
    return pl.pallas_call(
        functools.partial(_ret_kernel, chunk=chunk),
        grid=(b, RET_HEADS // hp),
        in_specs=[head(COL_RQ), head(COL_RK), head(COL_RV), head(COL_RG),
                  pl.BlockSpec((1, s, HEAD_DIM), lambda i, j: (i, 0, 0)),
                  pl.BlockSpec((1, s, HEAD_DIM), lambda i, j: (i, 0, 0)),
                  per_head((chunk, chunk)), per_head((chunk, HEAD_DIM)), per_head((chunk, HEAD_DIM)),
                  per_head((1, HEAD_DIM)),
                  pl.BlockSpec((1, w), lambda i, j: (0, j))],
        out_specs=pl.BlockSpec((s, w), lambda i, j: (i, j)),
        out_shape=jax.ShapeDtypeStruct((b * s, RET_WIDTH), bf16),
        compiler_params=_cparams(("parallel", "parallel")),
        name="retention",
    )(proj, proj, proj, proj, cos, sin, decay, xi, zeta, gc, gn_w)


def _branch_merge_kernel(fo_ref, ro_ref, g0_ref, g1_ref, bg_ref, wb_ref, mg_ref, *, ncol):
    d = mg_ref.shape[1]
    cw = d // ncol
    fo = fo_ref[...]
    ro = ro_ref[...]
    for c in range(ncol):
        cs = slice(c * cw, (c + 1) * cw)
        bd0 = jnp.dot(fo, wb_ref[0, :, cs], preferred_element_type=f32)
        bd1 = jnp.dot(ro, wb_ref[1, :, cs], preferred_element_type=f32)
        ga = jax.nn.sigmoid(g0_ref[:, cs].astype(f32) + bg_ref[0:1, cs])
        gb = jax.nn.sigmoid(g1_ref[:, cs].astype(f32) + bg_ref[1:2, cs])
        mg_ref[:, cs] = (ga * bd0 + gb * bd1).astype(mg_ref.dtype)


def _branch_merge(fox_o, ret_o, proj, b_gate, w_branch):
    t = fox_o.shape[0]
    d = D_MODEL
    tm = min(1024, t)
    gate0 = COL_GATE * HEAD_DIM // d
    return pl.pallas_call(
        functools.partial(_branch_merge_kernel, ncol=4),
        grid=(t // tm,),
        in_specs=[pl.BlockSpec((tm, FOX_WIDTH), lambda i: (i, 0)),
                  pl.BlockSpec((tm, RET_WIDTH), lambda i: (i, 0)),
                  pl.BlockSpec((tm, d), lambda i: (i, gate0)),
                  pl.BlockSpec((tm, d), lambda i: (i, gate0 + 1)),
                  _const_spec((N_BRANCH, d)),
                  _const_spec((N_BRANCH, FOX_WIDTH, d))],
        out_specs=pl.BlockSpec((tm, d), lambda i: (i, 0)),
        out_shape=jax.ShapeDtypeStruct((t, d), bf16),
        compiler_params=_cparams(("parallel",)),
        name="branch_merge",
    )(fox_o, ret_o, proj, proj, b_gate, w_branch)


def _route(logits):
    lane = lax.broadcasted_iota(jnp.int32, logits.shape, 1).astype(f32)
    far = jnp.float32(1e9)
    gmask = lane < N_GROUPS
    gl = jnp.where(gmask, logits, NEG_BIG)
    gmax = jnp.max(gl, axis=-1, keepdims=True)
    gsel = jnp.min(jnp.where(gl == gmax, lane, far), axis=-1, keepdims=True)
    p_group = 1.0 / jnp.sum(jnp.where(gmask, jnp.exp(gl - gmax), 0.0), axis=-1, keepdims=True)
    lo = N_GROUPS + EXPERTS_PER_GROUP * gsel
    emask = (lane >= lo) & (lane < lo + EXPERTS_PER_GROUP)
    el = jnp.where(emask, logits, NEG_BIG)
    m1 = jnp.max(el, axis=-1, keepdims=True)
    i1 = jnp.min(jnp.where(el == m1, lane, far), axis=-1, keepdims=True)
    el2 = jnp.where(lane == i1, NEG_BIG, el)
    m2 = jnp.max(el2, axis=-1, keepdims=True)
    i2 = jnp.min(jnp.where(el2 == m2, lane, far), axis=-1, keepdims=True)
    t = jnp.exp(m2 - m1)
    w1 = p_group / (1.0 + t)
    w2 = p_group * t / (1.0 + t)
    out = jnp.where(lane == 0, i1 - N_GROUPS, 0.0)
    out = jnp.where(lane == 1, i2 - N_GROUPS, out)
    out = jnp.where(lane == 2, w1, out)
    out = jnp.where(lane == 3, w2, out)
    return out


def _outproj_kernel(mg_ref, x_ref, wo_ref, nfw_ref, wr_ref, br_ref, h_ref, hs_ref, route_ref, cnt_ref):
    h = x_ref[...] + jnp.dot(mg_ref[...], wo_ref[...], preferred_element_type=f32)
    h_ref[...] = h
    hn = h * lax.rsqrt(jnp.mean(h * h, axis=-1, keepdims=True) + EPS) * nfw_ref[...]
    _rows_to_slabs(hs_ref, hn, hn.shape[0])
    hi = hn.astype(bf16)
    lo = (hn - hi.astype(f32)).astype(bf16)
    hw = jnp.dot(hi, wr_ref[...], preferred_element_type=f32)
    logits = (hw[:, :LANES] + jnp.dot(lo, wr_ref[:, :LANES], preferred_element_type=f32) + hw[:, LANES:]) + br_ref[...]
    route = _route(logits)
    route_ref[...] = route

    @pl.when(pl.program_id(0) == 0)
    def _():
        cnt_ref[...] = jnp.zeros_like(cnt_ref)

    lane = lax.broadcasted_iota(jnp.int32, route.shape, 1).astype(f32)
    chosen = (lane == route[:, 0:1]).astype(f32) + (lane == route[:, 1:2]).astype(f32)
    cnt_ref[...] += jnp.sum(chosen, axis=0, keepdims=True)


def _outproj_router(merged, x2, w_out, nfw, wr_cat, b_router):
    t, d = x2.shape
    tm = min(512, t)
    return pl.pallas_call(
        _outproj_kernel,
        grid=(t // tm,),
        in_specs=[pl.BlockSpec((tm, d), lambda i: (i, 0)),
                  pl.BlockSpec((tm, d), lambda i: (i, 0)),
                  _const_spec((d, d)),
                  _const_spec((1, d)),
                  _const_spec((d, 2 * LANES)), _const_spec((1, LANES))],
        out_specs=[pl.BlockSpec((tm, d), lambda i: (i, 0)),
                   pl.BlockSpec((tm * PITCH, LANES), lambda i: (i, 0)),
                   pl.BlockSpec((tm, LANES), lambda i: (i, 0)),
                   pl.BlockSpec((8, LANES), lambda i: (0, 0))],
        out_shape=[jax.ShapeDtypeStruct((t, d), f32), jax.ShapeDtypeStruct((t * PITCH, LANES), u32),
                   jax.ShapeDtypeStruct((t, LANES), f32), jax.ShapeDtypeStruct((8, LANES), f32)],
        compiler_params=_cparams(("arbitrary",)),
        name="outproj_router",
    )(merged, x2, w_out, nfw, wr_cat, b_router)


def _plan_kernel(route_ref, cnt_ref, dest_ref, ends_ref, carry_ref, *, bm):
    r = route_ref[...]
    tb = r.shape[0]
    lane = lax.broadcasted_iota(jnp.int32, r.shape, 1).astype(f32)
    o0 = (lane == r[:, 0:1]).astype(f32)
    o1 = (lane == r[:, 1:2]).astype(f32)
    o = o0 + o1

    @pl.when(pl.program_id(0) == 0)
    def _():
        cnt = cnt_ref[...]
        padded = jnp.floor((cnt + (bm - 1)) * (1.0 / bm)) * bm
        lane8 = lax.broadcasted_iota(jnp.int32, cnt.shape, 1)
        ends = padded
        sh = 1
        while sh < LANES:
            ends = ends + jnp.where(lane8 >= sh, pltpu.roll(ends, sh, axis=1), 0.0)
            sh *= 2
        sub = lax.broadcasted_iota(jnp.int32, cnt.shape, 0)
        ends_ref[...] = jnp.where(sub == 0, ends, jnp.where(sub == 1, padded, cnt))
        carry_ref[...] = ends - padded

    row = lax.broadcasted_iota(jnp.int32, (tb, tb), 0)
    col = lax.broadcasted_iota(jnp.int32, (tb, tb), 1)
    lower = jnp.where(col < row, 1.0, 0.0).astype(bf16)
    before = jnp.dot(lower, o.astype(bf16), preferred_element_type=f32) + carry_ref[0:1, :]
    d0 = jnp.sum(before * o0, axis=-1, keepdims=True)
    d1 = jnp.sum(before * o1, axis=-1, keepdims=True)
    dd = jnp.where(lane == 0, d0, jnp.where(lane == 1, d1, 0.0))
    dest_ref[0] = dd.T[:8, :].astype(jnp.int32)
    carry_ref[...] += jnp.sum(o, axis=0, keepdims=True)


def _dispatch_plan(route, counts, bm):
    t = route.shape[0]
    tb = min(1024, t)
    nb = t // tb
    return pl.pallas_call(
        functools.partial(_plan_kernel, bm=bm),
        grid=(nb,),
        in_specs=[pl.BlockSpec((tb, LANES), lambda i: (i, 0)),
                  pl.BlockSpec((8, LANES), lambda i: (0, 0))],
        out_specs=[pl.BlockSpec((1, 8, tb), lambda i: (i, 0, 0)),
                   pl.BlockSpec((8, LANES), lambda i: (0, 0))],
        out_shape=[jax.ShapeDtypeStruct((nb, 8, tb), jnp.int32), jax.ShapeDtypeStruct((8, LANES), f32)],
        scratch_shapes=[pltpu.VMEM((8, LANES), f32)],
        compiler_params=_cparams(("arbitrary",)),
        name="dispatch_plan",
    )(route, counts)


def _dispatch_kernel(d0_ref, d1_ref, fill_ref, end_ref, nu_ref, hs_ref, xs_hbm, zbuf, sem, zsem,
                     *, tb, bm, n_blk, unroll=8):
    i = pl.program_id(0)
    base = i * tb
    blk_rows = bm * PITCH

    def pad_copy(s):
        return pltpu.make_async_copy(zbuf.at[pl.ds(0, PITCH), :], xs_hbm.at[pl.ds(s * PITCH, PITCH), :], zsem)

    def tail_copy(g):
        return pltpu.make_async_copy(zbuf, xs_hbm.at[pl.ds(pl.multiple_of(g * blk_rows, 8), blk_rows), :], zsem)

    @pl.when(i == 0)
    def _():
        zbuf[...] = jnp.zeros(zbuf.shape, u32)
        for op in ("start", "wait"):
            def per_expert(e, c, op=op):
                def one(s, c2):
                    getattr(pad_copy(s), op)()
                    return c2
                return lax.fori_loop(fill_ref[e], end_ref[e], one, c)

            def per_tail(g, c, op=op):
                getattr(tail_copy(g), op)()
                return c

            lax.fori_loop(0, N_EXPERTS, per_expert, 0)
            lax.fori_loop(nu_ref[0], n_blk, per_tail, 0)

    def body(g, c):
        for u in range(unroll):
            r = g * unroll + u
            src = hs_ref.at[pl.ds(r * PITCH, PITCH), :]
            for k, dref in enumerate((d0_ref, d1_ref)):
                d = dref[base + r]
                pltpu.make_async_copy(src, xs_hbm.at[pl.ds(d * PITCH, PITCH), :], sem).start(priority=k)
        return c

    lax.fori_loop(0, tb // unroll, body, 0)
    for _ in range(TOP_K):
        pltpu.make_async_copy(xs_hbm.at[pl.ds(0, tb * PITCH), :], xs_hbm.at[pl.ds(0, tb * PITCH), :], sem).wait()


def _dispatch_rows(dest0, dest1, fill_start, seg_end, n_used, hs, cap):
    t = dest0.shape[0]
    tb = min(1024, t)
    bm = MOE_BLOCK
    grid_spec = pltpu.PrefetchScalarGridSpec(
        num_scalar_prefetch=5,
        grid=(t // tb,),
        in_specs=[pl.BlockSpec((tb * PITCH, LANES), lambda i, *_: (i, 0))],
        out_specs=pl.BlockSpec(memory_space=pl.ANY),
        scratch_shapes=[pltpu.VMEM((bm * PITCH, LANES), u32), pltpu.SemaphoreType.DMA(()),
                        pltpu.SemaphoreType.DMA(())],
    )
    return pl.pallas_call(
        functools.partial(_dispatch_kernel, tb=tb, bm=bm, n_blk=cap // bm),
        grid_spec=grid_spec,
        out_shape=jax.ShapeDtypeStruct((cap * PITCH, LANES), u32),
        compiler_params=_cparams(("arbitrary",)),
        name="dispatch_rows",
    )(dest0, dest1, fill_start, seg_end, n_used, hs)


def _moe_kernel(g0_ref, nb_ref, nu_ref, x_hbm, w1_ref, w3_ref, w2_ref, y_hbm,
                xbuf, ybuf, w1b, w3b, w2b, gsem, ysem, *, bm, n_blk):
    e = pl.program_id(0)
    n_used = nu_ref[0]
    g0 = g0_ref[e]
    nb = nb_ref[e]
    blk_rows = bm * PITCH

    def x_copy(slot, g):
        src = x_hbm.at[pl.ds(pl.multiple_of(g * blk_rows, 8), blk_rows), :]
        return pltpu.make_async_copy(src, xbuf.at[slot], gsem.at[slot])

    def y_copy(slot, g):
        dst = y_hbm.at[pl.ds(pl.multiple_of(g * blk_rows, 8), blk_rows), :]
        return pltpu.make_async_copy(ybuf.at[slot], dst, ysem.at[slot])

    @pl.when(e == 0)
    def _():
        x_copy(0, 0).start()
        for k in range(1, X_BUFS - 1):
            pl.when(n_used > k)(lambda k=k: x_copy(k, k).start())

    @pl.when(nb > 0)
    def _():
        w1b[...] = w1_ref[0].astype(bf16)
        w3b[...] = w3_ref[0].astype(bf16)
        w2b[...] = w2_ref[0].astype(bf16)

    def block(c, carry):
        g = g0 + c
        slot = g % 2
        xslot = g % X_BUFS

        ahead = g + (X_BUFS - 1)

        @pl.when(ahead < n_used)
        def _():
            x_copy(ahead % X_BUFS, ahead).start()

        x_copy(xslot, g).wait()

        hn = _slabs_to_rows(xbuf.at[xslot], bm).astype(bf16)
        a = jnp.dot(hn, w1b[...], preferred_element_type=f32)
        b = jnp.dot(hn, w3b[...], preferred_element_type=f32)
        mid = (a * jax.nn.sigmoid(a) * b).astype(bf16)
        y = jnp.dot(mid, w2b[...], preferred_element_type=f32)

        @pl.when(g >= 2)
        def _():
            y_copy(slot, g - 2).wait()

        _rows_to_slabs(ybuf.at[slot], y, bm)
        y_copy(slot, g).start()
        return carry

    lax.fori_loop(0, nb, block, 0)

    @pl.when(e == pl.num_programs(0) - 1)
    def _():
        @pl.when(n_used >= 2)
        def _():
            y_copy(n_used % 2, n_used - 2).wait()

        y_copy((n_used - 1) % 2, n_used - 1).wait()
        ybuf[0] = jnp.zeros(ybuf.shape[1:], u32)

        def zstart(g, carry):
            y_copy(0, g).start()
            return carry

        def zwait(g, carry):
            y_copy(0, g).wait()
            return carry

        lax.fori_loop(n_used, n_blk, zstart, 0)
        lax.fori_loop(n_used, n_blk, zwait, 0)


def _expert_mlp(blk_start, blk_count, n_used, xs, w1, w3, w2, cap):
    d = D_MODEL
    bm = MOE_BLOCK
    n_blk = cap // bm
    by_expert = lambda shape: pl.BlockSpec(shape, lambda e, g0, nb, nu: (e, 0, 0))
    grid_spec = pltpu.PrefetchScalarGridSpec(
        num_scalar_prefetch=3,
        grid=(N_EXPERTS,),
        in_specs=[pl.BlockSpec(memory_space=pl.ANY),
                  by_expert((1, d, D_EXPERT)), by_expert((1, d, D_EXPERT)), by_expert((1, D_EXPERT, d))],
        out_specs=pl.BlockSpec(memory_space=pl.ANY),
        scratch_shapes=[pltpu.VMEM((X_BUFS, bm * PITCH, LANES), u32), pltpu.VMEM((2, bm * PITCH, LANES), u32),
                        pltpu.VMEM((d, D_EXPERT), bf16), pltpu.VMEM((d, D_EXPERT), bf16),
                        pltpu.VMEM((D_EXPERT, d), bf16),
                        pltpu.SemaphoreType.DMA((X_BUFS,)), pltpu.SemaphoreType.DMA((2,))],
    )
    return pl.pallas_call(
        functools.partial(_moe_kernel, bm=bm, n_blk=n_blk),
        grid_spec=grid_spec,
        out_shape=jax.ShapeDtypeStruct((cap * PITCH, LANES), u32),
        compiler_params=_cparams(("arbitrary",)),
        name="expert_mlp",
    )(blk_start, blk_count, n_used, xs, w1, w3, w2)


def _combine_kernel(d0_ref, d1_ref, h_ref, route_ref, y_hbm, nw_ref, o_ref, ybuf, sem, *, tb):
    i = pl.program_id(0)
    n = pl.num_programs(0)
    slot = i % 2

    def start(blk, s):
        _slab_gather_start(d0_ref, blk * tb, tb, y_hbm, ybuf.at[s, 0], sem.at[s])
        _slab_gather_start(d1_ref, blk * tb, tb, y_hbm, ybuf.at[s, 1], sem.at[s])

    @pl.when(i == 0)
    def _():
        start(0, 0)

    @pl.when(i + 1 < n)
    def _():
        start(i + 1, 1 - slot)

    for k in range(TOP_K):
        _slab_gather_wait(tb, y_hbm, ybuf.at[slot, k], sem.at[slot])
    r = route_ref[...]
    h = h_ref[...] + (r[:, 2:3] * _slabs_to_rows(ybuf.at[slot, 0], tb)
                      + r[:, 3:4] * _slabs_to_rows(ybuf.at[slot, 1], tb))
    o_ref[...] = h * lax.rsqrt(jnp.mean(h * h, axis=-1, keepdims=True) + EPS) * nw_ref[...]


def _combine(dest0, dest1, h, route, ys, norm_w):
    t, d = h.shape
    tb = min(256, t)
    grid_spec = pltpu.PrefetchScalarGridSpec(
        num_scalar_prefetch=2,
        grid=(t // tb,),
        in_specs=[pl.BlockSpec((tb, d), lambda i, a, b: (i, 0)),
                  pl.BlockSpec((tb, LANES), lambda i, a, b: (i, 0)),
                  pl.BlockSpec(memory_space=pl.ANY),
                  pl.BlockSpec((1, d), lambda i, a, b: (0, 0))],
        out_specs=pl.BlockSpec((tb, d), lambda i, a, b: (i, 0)),
        scratch_shapes=[pltpu.VMEM((2, TOP_K, tb * PITCH, LANES), u32), pltpu.SemaphoreType.DMA((2,))],
    )
    return pl.pallas_call(
        functools.partial(_combine_kernel, tb=tb),
        grid_spec=grid_spec,
        out_shape=jax.ShapeDtypeStruct((t, d), f32),
        compiler_params=_cparams(("arbitrary",)),
        name="combine_final_norm",
    )(dest0, dest1, h, route, ys, norm_w)


def kernel(x, positions, norm_mix_w, w_in, fox_b_f, ret_gn_w, w_branch, b_gate, w_out, norm_ffn_w, w_router_group,
           b_router_group, w_router_expert, b_router_expert, w1, w3, w2, norm_final_w):
    b, s, d = x.shape
    assert d == D_MODEL and norm_mix_w.shape[0] == 1
    t = b * s
    x2 = x.reshape(t, d)

    wt = w_in[0].T
    o_f = 3 * FOX_WIDTH
    o_r = o_f + FOX_HEADS
    o_g = o_r + 4 * RET_WIDTH
    tile_rows = ([o_g + k * 1024 for k in range(N_BRANCH * D_MODEL // 1024)]
                 + [k * 1024 for k in range(o_f // 1024)]
                 + [o_r + k * 1024 for k in range(4 * RET_WIDTH // 1024)])
    w_main = _prep_in_weights(wt, tile_rows)
    w_f = jnp.pad(wt[o_f:o_r], ((0, LANES - FOX_HEADS), (0, 0))).astype(bf16)
    bias_f = jnp.pad(fox_b_f[0], (0, LANES - FOX_HEADS)).reshape(1, LANES)
    w_r = jnp.concatenate([w_router_group[0], w_router_expert[0]], axis=1)
    w_r = jnp.pad(w_r, ((0, 0), (0, LANES - w_r.shape[1])))
    wr_hi = w_r.astype(bf16)
    wr_lo = (w_r - wr_hi.astype(f32)).astype(bf16)
    b_r = jnp.pad(jnp.concatenate([b_router_group[0], b_router_expert[0]]), (0, LANES - N_GROUPS - N_EXPERTS))
    b_r = b_r.reshape(1, LANES)

    cos, sin = _rope_tables(positions)
    proj, f_logit = _in_projection(x2, norm_mix_w, w_main, w_f)
    c = _forget_cumsum(f_logit, bias_f, b, s)
    fox_o = _fox_attention(proj, c.reshape(b * FOX_HEADS, 1, s), b, s)
    ret_o = _retention(proj, cos, sin, ret_gn_w, b, s)
    merged = _branch_merge(fox_o, ret_o, proj, b_gate[0], w_branch[0].astype(bf16))
    h, h_slabs, route, counts = _outproj_router(merged, x2, w_out[0].astype(bf16), norm_ffn_w,
                                        jnp.concatenate([wr_hi, wr_lo], axis=1), b_r)

    bm = MOE_BLOCK
    cap = t * TOP_K + N_EXPERTS * bm
    dest_rows, seg = _dispatch_plan(route, counts, bm)
    dest0 = dest_rows[:, 0, :].reshape(t)
    dest1 = dest_rows[:, 1, :].reshape(t)
    pends = seg[0, :N_EXPERTS].astype(jnp.int32)
    padded = seg[1, :N_EXPERTS].astype(jnp.int32)
    counts = seg[2, :N_EXPERTS].astype(jnp.int32)
    pstarts = pends - padded
    n_used = pends[-1:] // bm

    xs = _dispatch_rows(dest0, dest1, pstarts + counts, pends, n_used, h_slabs, cap)
    y = _expert_mlp(pstarts // bm, padded // bm, n_used, xs, w1[0], w3[0], w2[0], cap)
    out = _combine(dest0, dest1, h, route, y, norm_final_w.reshape(1, d))
    return out.reshape(b, s, d)
```

```python
import functools
import math

import numpy as np
import jax
import jax.numpy as jnp
from jax import lax
from jax.experimental import pallas as pl
from jax.experimental.pallas import tpu as pltpu

D_MODEL = 2048
HEAD_DIM = 128
FOX_HEADS = 8
RET_HEADS = 8
FOX_WIDTH = FOX_HEADS * HEAD_DIM
RET_WIDTH = RET_HEADS * HEAD_DIM
N_BRANCH = 2
N_GROUPS = 4
EXPERTS_PER_GROUP = 8
N_EXPERTS = N_GROUPS * EXPERTS_PER_GROUP
TOP_K = 2
D_EXPERT = 512
ROPE_BASE = 10000.0
EPS = 1e-6

LANES = 128
NEG_BIG = -1e30
LOG2E = math.log2(math.e)
MAIN_WIDTH = 3 * FOX_WIDTH + 4 * RET_WIDTH + N_BRANCH * D_MODEL
COL_GATE = 0
COL_FQ = N_BRANCH * D_MODEL // HEAD_DIM
COL_FK = COL_FQ + FOX_HEADS
COL_FV = COL_FK + FOX_HEADS
COL_RQ = COL_FV + FOX_HEADS
COL_RK = COL_RQ + RET_HEADS
COL_RV = COL_RK + RET_HEADS
COL_RG = COL_RV + RET_HEADS

ATTN_BLOCK = 256
FOX_HEADS_PER_STEP = 4
RET_HEADS_PER_STEP = 1
RET_CHUNK = 256
MOE_BLOCK = 256
X_BUFS = 4
PAD_FILL_CHUNKS = (64, 8, 1)
VMEM_LIMIT = 56 * 1024 * 1024

f32 = jnp.float32
bf16 = jnp.bfloat16
_NT = (((1,), (1,)), ((), ()))
SLAB = D_MODEL // (2 * LANES)
PITCH = SLAB + 1
u32 = jnp.uint32
_HI_MASK = np.uint32(0xFFFF0000)


def _rows_to_slabs(ref2, val, n_rows):
    bits = lambda v: lax.bitcast_convert_type(v.astype(bf16).astype(f32), u32)
    for j in range(SLAB):
        lo = val[:, (2 * j) * LANES:(2 * j + 1) * LANES]
        hi = val[:, (2 * j + 1) * LANES:(2 * j + 2) * LANES]
        ref2[pl.ds(j, n_rows, stride=PITCH), :] = (bits(lo) >> 16) | (bits(hi) & _HI_MASK)
    ref2[pl.ds(SLAB, n_rows, stride=PITCH), :] = jnp.zeros((n_rows, LANES), u32)


def _slabs_to_rows(ref2, n_rows):
    chunks = []
    for j in range(SLAB):
        w = ref2[pl.ds(j, n_rows, stride=PITCH), :]
        chunks.append(lax.bitcast_convert_type(w << 16, f32))
        chunks.append(lax.bitcast_convert_type(w & _HI_MASK, f32))
    return jnp.concatenate(chunks, axis=-1)


def _slab_gather_start(idx_ref, base, n_rows, src_hbm, dst, sem, unroll=8):
    def body(g, _):
        for u in range(unroll):
            r = g * unroll + u
            t = idx_ref[base + r]
            pltpu.make_async_copy(src_hbm.at[pl.ds(t * PITCH, SLAB), :], dst.at[pl.ds(r * PITCH, SLAB), :],
                                  sem).start(priority=u % 2)
        return 0
    lax.fori_loop(0, n_rows // unroll, body, 0)


def _slab_gather_wait(n_rows, src_hbm, dst, sem):
    pltpu.make_async_copy(src_hbm.at[pl.ds(0, n_rows * SLAB), :], dst.at[pl.ds(0, n_rows * SLAB), :], sem).wait()


def _cparams(sem, vmem=VMEM_LIMIT):
    return pltpu.CompilerParams(dimension_semantics=sem, vmem_limit_bytes=vmem)


def _const_spec(shape):
    return pl.BlockSpec(shape, lambda i: tuple(0 for _ in shape), pipeline_mode=pl.Buffered(1))


def _rope_kernel(pos_ref, freq_ref, sgn_ref, cos_ref, sin_ref):
    pos = pos_ref[0].astype(f32)
    hs = pos.shape[0] // 2
    half = HEAD_DIM // 2
    lane = lax.broadcasted_iota(jnp.int32, (hs, HEAD_DIM), 1)
    low = lane < half
    ang = jnp.where(low, pos[:hs], pos[hs:]) * freq_ref[...]
    c = jnp.cos(ang)
    s = jnp.sin(ang) * sgn_ref[...]
    c_sw = pltpu.roll(c, half, axis=1)
    s_sw = pltpu.roll(s, half, axis=1)
    cos_ref[0, :hs, :] = jnp.where(low, c, c_sw)
    cos_ref[0, hs:, :] = jnp.where(low, c_sw, c)
    sin_ref[0, :hs, :] = jnp.where(low, s, -s_sw)
    sin_ref[0, hs:, :] = jnp.where(low, -s_sw, s)


def _rope_tables(positions):
    b, s = positions.shape
    half = HEAD_DIM // 2
    inv = (np.float32(ROPE_BASE) ** (-np.arange(half, dtype=np.float32) / np.float32(half))).astype(np.float32)
    freq = jnp.asarray(np.concatenate([inv, inv])[None, :])
    sgn = jnp.asarray(np.concatenate([-np.ones(half, np.float32), np.ones(half, np.float32)])[None, :])
    out = jax.ShapeDtypeStruct((b, s, HEAD_DIM), f32)
    return pl.pallas_call(
        _rope_kernel,
        grid=(b,),
        in_specs=[pl.BlockSpec((1, s, 1), lambda i: (i, 0, 0)),
                  pl.BlockSpec((1, HEAD_DIM), lambda i: (0, 0)),
                  pl.BlockSpec((1, HEAD_DIM), lambda i: (0, 0))],
        out_specs=[pl.BlockSpec((1, s, HEAD_DIM), lambda i: (i, 0, 0)),
                   pl.BlockSpec((1, s, HEAD_DIM), lambda i: (i, 0, 0))],
        out_shape=[out, out],
        compiler_params=_cparams(("parallel",)),
        name="rope_tables",
    )(positions.reshape(b, s, 1), freq, sgn)


_Q_FACTORS = ((COL_FQ * HEAD_DIM // 1024, LOG2E / math.sqrt(HEAD_DIM)),
              (COL_RQ * HEAD_DIM // 1024, 1.0 / math.sqrt(HEAD_DIM)))


def _wprep_kernel(off_ref, w_hbm, o_ref, buf, sem, *, tn):
    j = pl.program_id(0)
    n = pl.num_programs(0)
    slot = j % 2

    def copy(jj, s):
        src = w_hbm.at[pl.ds(pl.multiple_of(off_ref[jj], 8), tn), :]
        return pltpu.make_async_copy(src, buf.at[s], sem.at[s])

    @pl.when(j == 0)
    def _():
        copy(0, 0).start()

    @pl.when(j + 1 < n)
    def _():
        copy(j + 1, 1 - slot).start()

    copy(j, slot).wait()
    factor = jnp.float32(1.0)
    for tile, value in _Q_FACTORS:
        factor = jnp.where(j == tile, jnp.float32(value), factor)
    o_ref[...] = (buf[slot] * factor).astype(o_ref.dtype)


def _prep_in_weights(wt, row_offsets):
    tn = 1024
    d = wt.shape[1]
    n_tiles = len(row_offsets)
    grid_spec = pltpu.PrefetchScalarGridSpec(
        num_scalar_prefetch=1,
        grid=(n_tiles,),
        in_specs=[pl.BlockSpec(memory_space=pl.ANY)],
        out_specs=pl.BlockSpec((tn, d), lambda j, off: (j, 0)),
        scratch_shapes=[pltpu.VMEM((2, tn, d), f32), pltpu.SemaphoreType.DMA((2,))],
    )
    return pl.pallas_call(
        functools.partial(_wprep_kernel, tn=tn),
        grid_spec=grid_spec,
        out_shape=jax.ShapeDtypeStruct((n_tiles * tn, d), bf16),
        compiler_params=_cparams(("arbitrary",)),
        name="prep_in_weights",
    )(jnp.asarray(row_offsets, jnp.int32), wt)


_ROT_TILES = (COL_RQ * HEAD_DIM // 1024, COL_RK * HEAD_DIM // 1024)


def _inproj_kernel(x_ref, nw_ref, w_ref, wf_ref, cos_ref, sin_ref, o_ref, f_ref, xn_ref):
    j = pl.program_id(1)

    @pl.when(j == 0)
    def _():
        x = x_ref[...]
        ms = jnp.mean(x * x, axis=-1, keepdims=True)
        xn = (x * lax.rsqrt(ms + EPS) * nw_ref[...]).astype(bf16)
        xn_ref[...] = xn
        f_ref[...] = lax.dot_general(xn, wf_ref[...], _NT, preferred_element_type=f32)

    rotated = (j == _ROT_TILES[0]) | (j == _ROT_TILES[1])

    @pl.when(jnp.logical_not(rotated))
    def _():
        o_ref[...] = lax.dot_general(xn_ref[...], w_ref[...], _NT, preferred_element_type=f32).astype(o_ref.dtype)

    @pl.when(rotated)
    def _():
        acc = lax.dot_general(xn_ref[...], w_ref[...], _NT, preferred_element_type=f32)
        cos = cos_ref[...]
        sin = sin_ref[...]
        for hh in range(acc.shape[1] // HEAD_DIM):
            lanes = slice(hh * HEAD_DIM, (hh + 1) * HEAD_DIM)
            t = acc[:, lanes]
            o_ref[:, lanes] = (t * cos + pltpu.roll(t, HEAD_DIM // 2, axis=1) * sin).astype(o_ref.dtype)


def _in_projection(x2, norm_w, w_main_t, w_f_t, cos, sin):
    t, d = x2.shape
    n = w_main_t.shape[0]
    tm = min(1024, t)
    tn = 1024
    return pl.pallas_call(
        _inproj_kernel,
        grid=(t // tm, n // tn),
        in_specs=[pl.BlockSpec((tm, d), lambda i, j: (i, 0)),
                  pl.BlockSpec((1, d), lambda i, j: (0, 0)),
                  pl.BlockSpec((tn, d), lambda i, j: (j, 0)),
                  pl.BlockSpec((LANES, d), lambda i, j: (0, 0)),
                  pl.BlockSpec((tm, HEAD_DIM), lambda i, j: (i, 0)),
                  pl.BlockSpec((tm, HEAD_DIM), lambda i, j: (i, 0))],
        out_specs=[pl.BlockSpec((tm, tn), lambda i, j: (i, j)),
                   pl.BlockSpec((tm, LANES), lambda i, j: (i, 0))],
        out_shape=[jax.ShapeDtypeStruct((t, n), bf16), jax.ShapeDtypeStruct((t, LANES), f32)],
        scratch_shapes=[pltpu.VMEM((tm, d), bf16)],
        compiler_params=_cparams(("parallel", "arbitrary")),
        name="in_projection",
    )(x2, norm_w, w_main_t, w_f_t, cos, sin)


def _fcum_kernel(f_ref, b_ref, c_ref):
    z = f_ref[...] + b_ref[...]
    ls = jnp.minimum(z, 0.0) - jnp.log1p(jnp.exp(-jnp.abs(z)))
    c = ls.T[:FOX_HEADS, :]
    s = c.shape[1]
    lane = lax.broadcasted_iota(jnp.int32, c.shape, 1)
    sh = 1
    while sh < s:
        c = c + jnp.where(lane >= sh, pltpu.roll(c, sh, axis=1), 0.0)
        sh *= 2
    c_ref[0] = c * LOG2E


def _forget_cumsum(f_logit, bias, b, s):
    return pl.pallas_call(
        _fcum_kernel,
        grid=(b,),
        in_specs=[pl.BlockSpec((s, LANES), lambda i: (i, 0)),
                  pl.BlockSpec((1, LANES), lambda i: (0, 0))],
        out_specs=pl.BlockSpec((1, FOX_HEADS, s), lambda i: (i, 0, 0)),
        out_shape=jax.ShapeDtypeStruct((b, FOX_HEADS, s), f32),
        compiler_params=_cparams(("parallel",)),
        name="forget_cumsum",
    )(f_logit, bias)


def _fox_kernel(q_ref, k_ref, v_ref, c_ref, o_ref, va_ref, acc_ref, *, tq):
    s_len = q_ref.shape[0]
    nq = s_len // tq
    n_heads = q_ref.shape[1] // HEAD_DIM
    row = lax.broadcasted_iota(jnp.int32, (tq, tq), 0)
    col = lax.broadcasted_iota(jnp.int32, (tq, tq), 1)
    causal = row >= col
    for hh in range(n_heads):
        lanes = slice(hh * HEAD_DIM, (hh + 1) * HEAD_DIM)
        va_ref[hh, :, :HEAD_DIM] = v_ref[:, lanes]
        va_ref[hh, :, HEAD_DIM:] = jnp.ones((s_len, HEAD_DIM), va_ref.dtype)
        panels = []
        row_max = [None] * nq
        for kb in range(nq):
            r0 = kb * tq
            s = lax.dot_general(q_ref[r0:, lanes], k_ref[r0:r0 + tq, lanes], _NT, preferred_element_type=f32)
            s = s - c_ref[hh, :, r0:r0 + tq]
            parts = [jnp.where(causal, s[:tq], NEG_BIG)]
            parts += [s[(j - kb) * tq:(j - kb + 1) * tq] for j in range(kb + 1, nq)]
            panels.append(parts)
            for j, part in zip(range(kb, nq), parts):
                pm = jnp.max(part, axis=-1, keepdims=True)
                row_max[j] = pm if row_max[j] is None else jnp.maximum(row_max[j], pm)
        for kb in range(nq):
            r0 = kb * tq
            p = jnp.concatenate([jnp.exp2(part - row_max[j]).astype(bf16)
                                 for j, part in zip(range(kb, nq), panels[kb])], axis=0)
            contrib = jnp.dot(p, va_ref[hh, r0:r0 + tq, :], preferred_element_type=f32)
            if kb:
                acc_ref[hh, r0:, :] += contrib
            else:
                acc_ref[hh] = contrib
        acc = acc_ref[hh]
        o_ref[:, lanes] = (acc[:, :HEAD_DIM] / acc[:, HEAD_DIM:]).astype(o_ref.dtype)


def _fox_attention(proj, c_blk, b, s):
    tq = min(ATTN_BLOCK, s)
    hp = FOX_HEADS_PER_STEP
    w = hp * HEAD_DIM
    col = lambda c0: pl.BlockSpec((s, w), lambda i, j: (i, c0 // hp + j))
    return pl.pallas_call(
        functools.partial(_fox_kernel, tq=tq),
        grid=(b, FOX_HEADS // hp),
        in_specs=[col(COL_FQ), col(COL_FK), col(COL_FV),
                  pl.BlockSpec((hp, 1, s), lambda i, j: (i * (FOX_HEADS // hp) + j, 0, 0))],
        out_specs=pl.BlockSpec((s, w), lambda i, j: (i, j)),
        out_shape=jax.ShapeDtypeStruct((b * s, FOX_WIDTH), bf16),
        scratch_shapes=[pltpu.VMEM((hp, s, 2 * HEAD_DIM), bf16), pltpu.VMEM((hp, s, 2 * HEAD_DIM), f32)],
        compiler_params=_cparams(("parallel", "parallel")),
        name="fox_attention",
    )(proj, proj, proj, c_blk)


def _ret_kernel(q_ref, k_ref, v_ref, g_ref, dec_ref, xi_ref, zeta_ref, gc_ref, gnw_ref, o_ref, *, chunk):
    s_len = q_ref.shape[0]
    nc = s_len // chunk
    n_heads = q_ref.shape[1] // HEAD_DIM
    for hh in range(n_heads):
        lanes = slice(hh * HEAD_DIM, (hh + 1) * HEAD_DIM)
        state = jnp.zeros((HEAD_DIM, HEAD_DIM), f32)
        for ci in range(nc):
            rows = slice(ci * chunk, (ci + 1) * chunk)
            qb = q_ref[rows, lanes]
            kb = k_ref[rows, lanes]
            v = v_ref[rows, lanes]
            inner = lax.dot_general(qb, kb, _NT, preferred_element_type=f32) * dec_ref[hh]
            o = jnp.dot(inner.astype(bf16), v, preferred_element_type=f32)
            if ci:
                o = o + jnp.dot(qb, state.astype(bf16), preferred_element_type=f32) * xi_ref[hh]
            if ci + 1 < nc:
                kz = (kb.astype(f32) * zeta_ref[hh]).astype(bf16)
                kv = lax.dot_general(kz, v, (((0,), (0,)), ((), ())), preferred_element_type=f32)
                state = state * gc_ref[hh] + kv if ci else kv
            mu = jnp.mean(o, axis=-1, keepdims=True)
            oc = o - mu
            var = jnp.mean(oc * oc, axis=-1, keepdims=True)
            y = oc * lax.rsqrt(var + EPS) * gnw_ref[:, lanes]
            g = g_ref[rows, lanes].astype(f32)
            o_ref[rows, lanes] = (g * jax.nn.sigmoid(g) * y).astype(o_ref.dtype)


def _retention_constants(chunk):
    hh = np.arange(RET_HEADS, dtype=np.float64)
    log_gamma = np.log1p(-(2.0 ** (-5.0 - hh)))
    n = np.arange(chunk, dtype=np.float64)
    diff = n[:, None] - n[None, :]
    decay = np.where(diff[None] >= 0, np.exp(diff[None] * log_gamma[:, None, None]), 0.0)
    xi = np.exp((n[None, :] + 1.0) * log_gamma[:, None])
    zeta = np.exp((chunk - 1.0 - n[None, :]) * log_gamma[:, None])
    g_chunk = np.exp(chunk * log_gamma)
    bc = lambda a: np.broadcast_to(a[:, :, None], (RET_HEADS, chunk, HEAD_DIM))
    gc = np.broadcast_to(g_chunk[:, None, None], (RET_HEADS, 1, HEAD_DIM))
    to = lambda a: jnp.asarray(np.ascontiguousarray(a), dtype=f32)
    return to(decay), to(bc(xi)), to(bc(zeta)), to(gc)


def _retention(proj, gn_w, b, s):
    chunk = min(RET_CHUNK, s)
    decay, xi, zeta, gc = _retention_constants(chunk)
    hp = RET_HEADS_PER_STEP
    w = hp * HEAD_DIM
    head = lambda col: pl.BlockSpec((s, w), lambda i, j: (i, col // hp + j))
    per_head = lambda shape: pl.BlockSpec((hp,) + shape, lambda i, j: (j, 0, 0))
    return pl.pallas_call(
        functools.partial(_ret_kernel, chunk=chunk),
        grid=(b, RET_HEADS // hp),
        in_specs=[head(COL_RQ), head(COL_RK), head(COL_RV), head(COL_RG),
                  per_head((chunk, chunk)), per_head((chunk, HEAD_DIM)), per_head((chunk, HEAD_DIM)),
                  per_head((1, HEAD_DIM)),
                  pl.BlockSpec((1, w), lambda i, j: (0, j))],
        out_specs=pl.BlockSpec((s, w), lambda i, j: (i, j)),
        out_shape=jax.ShapeDtypeStruct((b * s, RET_WIDTH), bf16),
        compiler_params=_cparams(("parallel", "parallel")),
        name="retention",
    )(proj, proj, proj, proj, decay, xi, zeta, gc, gn_w)


def _branch_merge_kernel(fo_ref, ro_ref, g0_ref, g1_ref, bg_ref, wb_ref, mg_ref, *, ncol):
    d = mg_ref.shape[1]
    cw = d // ncol
    fo = fo_ref[...]
    ro = ro_ref[...]
    for c in range(ncol):
        cs = slice(c * cw, (c + 1) * cw)
        bd0 = jnp.dot(fo, wb_ref[0, :, cs], preferred_element_type=f32)
        bd1 = jnp.dot(ro, wb_ref[1, :, cs], preferred_element_type=f32)
        ga = jax.nn.sigmoid(g0_ref[:, cs].astype(f32) + bg_ref[0:1, cs])
        gb = jax.nn.sigmoid(g1_ref[:, cs].astype(f32) + bg_ref[1:2, cs])
        mg_ref[:, cs] = (ga * bd0 + gb * bd1).astype(mg_ref.dtype)


def _branch_merge(fox_o, ret_o, proj, b_gate, w_branch):
    t = fox_o.shape[0]
    d = D_MODEL
    tm = min(1024, t)
    gate0 = COL_GATE * HEAD_DIM // d
    return pl.pallas_call(
        functools.partial(_branch_merge_kernel, ncol=4),
        grid=(t // tm,),
        in_specs=[pl.BlockSpec((tm, FOX_WIDTH), lambda i: (i, 0)),
                  pl.BlockSpec((tm, RET_WIDTH), lambda i: (i, 0)),
                  pl.BlockSpec((tm, d), lambda i: (i, gate0)),
                  pl.BlockSpec((tm, d), lambda i: (i, gate0 + 1)),
                  _const_spec((N_BRANCH, d)),
                  _const_spec((N_BRANCH, FOX_WIDTH, d))],
        out_specs=pl.BlockSpec((tm, d), lambda i: (i, 0)),
        out_shape=jax.ShapeDtypeStruct((t, d), bf16),
        compiler_params=_cparams(("parallel",)),
        name="branch_merge",
    )(fox_o, ret_o, proj, proj, b_gate, w_branch)


def _route(logits):
    lane = lax.broadcasted_iota(jnp.int32, logits.shape, 1).astype(f32)
    far = jnp.float32(1e9)
    gmask = lane < N_GROUPS
    gl = jnp.where(gmask, logits, NEG_BIG)
    gmax = jnp.max(gl, axis=-1, keepdims=True)
    gsel = jnp.min(jnp.where(gl == gmax, lane, far), axis=-1, keepdims=True)
    p_group = 1.0 / jnp.sum(jnp.where(gmask, jnp.exp(gl - gmax), 0.0), axis=-1, keepdims=True)
    lo = N_GROUPS + EXPERTS_PER_GROUP * gsel
    emask = (lane >= lo) & (lane < lo + EXPERTS_PER_GROUP)
    el = jnp.where(emask, logits, NEG_BIG)
    m1 = jnp.max(el, axis=-1, keepdims=True)
    i1 = jnp.min(jnp.where(el == m1, lane, far), axis=-1, keepdims=True)
    el2 = jnp.where(lane == i1, NEG_BIG, el)
    m2 = jnp.max(el2, axis=-1, keepdims=True)
    i2 = jnp.min(jnp.where(el2 == m2, lane, far), axis=-1, keepdims=True)
    t = jnp.exp(m2 - m1)
    w1 = p_group / (1.0 + t)
    w2 = p_group * t / (1.0 + t)
    out = jnp.where(lane == 0, i1 - N_GROUPS, 0.0)
    out = jnp.where(lane == 1, i2 - N_GROUPS, out)
    out = jnp.where(lane == 2, w1, out)
    out = jnp.where(lane == 3, w2, out)
    return out


def _outproj_kernel(mg_ref, x_ref, wo_ref, nfw_ref, wr_ref, br_ref, h_ref, hs_ref, route_ref, cnt_ref):
    h = x_ref[...] + jnp.dot(mg_ref[...], wo_ref[...], preferred_element_type=f32)
    h_ref[...] = h
    hn = h * lax.rsqrt(jnp.mean(h * h, axis=-1, keepdims=True) + EPS) * nfw_ref[...]
    _rows_to_slabs(hs_ref, hn, hn.shape[0])
    hi = hn.astype(bf16)
    lo = (hn - hi.astype(f32)).astype(bf16)
    hw = jnp.dot(hi, wr_ref[...], preferred_element_type=f32)
    logits = (hw[:, :LANES] + jnp.dot(lo, wr_ref[:, :LANES], preferred_element_type=f32) + hw[:, LANES:]) + br_ref[...]
    route = _route(logits)
    route_ref[...] = route

    @pl.when(pl.program_id(0) == 0)
    def _():
        cnt_ref[...] = jnp.zeros_like(cnt_ref)

    lane = lax.broadcasted_iota(jnp.int32, route.shape, 1).astype(f32)
    chosen = (lane == route[:, 0:1]).astype(f32) + (lane == route[:, 1:2]).astype(f32)
    cnt_ref[...] += jnp.sum(chosen, axis=0, keepdims=True)


def _outproj_router(merged, x2, w_out, nfw, wr_cat, b_router):
    t, d = x2.shape
    tm = min(512, t)
    return pl.pallas_call(
        _outproj_kernel,
        grid=(t // tm,),
        in_specs=[pl.BlockSpec((tm, d), lambda i: (i, 0)),
                  pl.BlockSpec((tm, d), lambda i: (i, 0)),
                  _const_spec((d, d)),
                  _const_spec((1, d)),
                  _const_spec((d, 2 * LANES)), _const_spec((1, LANES))],
        out_specs=[pl.BlockSpec((tm, d), lambda i: (i, 0)),
                   pl.BlockSpec((tm * PITCH, LANES), lambda i: (i, 0)),
                   pl.BlockSpec((tm, LANES), lambda i: (i, 0)),
                   pl.BlockSpec((8, LANES), lambda i: (0, 0))],
        out_shape=[jax.ShapeDtypeStruct((t, d), f32), jax.ShapeDtypeStruct((t * PITCH, LANES), u32),
                   jax.ShapeDtypeStruct((t, LANES), f32), jax.ShapeDtypeStruct((8, LANES), f32)],
        compiler_params=_cparams(("arbitrary",)),
        name="outproj_router",
    )(merged, x2, w_out, nfw, wr_cat, b_router)


def _plan_kernel(route_ref, cnt_ref, dest_ref, ends_ref, carry_ref, *, bm):
    r = route_ref[...]
    tb = r.shape[0]
    lane = lax.broadcasted_iota(jnp.int32, r.shape, 1).astype(f32)
    o0 = (lane == r[:, 0:1]).astype(f32)
    o1 = (lane == r[:, 1:2]).astype(f32)
    o = o0 + o1

    @pl.when(pl.program_id(0) == 0)
    def _():
        cnt = cnt_ref[...]
        padded = jnp.floor((cnt + (bm - 1)) * (1.0 / bm)) * bm
        lane8 = lax.broadcasted_iota(jnp.int32, cnt.shape, 1)
        ends = padded
        sh = 1
        while sh < LANES:
            ends = ends + jnp.where(lane8 >= sh, pltpu.roll(ends, sh, axis=1), 0.0)
            sh *= 2
        sub = lax.broadcasted_iota(jnp.int32, cnt.shape, 0)
        ends_ref[...] = jnp.where(sub == 0, ends, jnp.where(sub == 1, padded, cnt))
        carry_ref[...] = ends - padded

    row = lax.broadcasted_iota(jnp.int32, (tb, tb), 0)
    col = lax.broadcasted_iota(jnp.int32, (tb, tb), 1)
    lower = jnp.where(col < row, 1.0, 0.0).astype(bf16)
    before = jnp.dot(lower, o.astype(bf16), preferred_element_type=f32) + carry_ref[0:1, :]
    d0 = jnp.sum(before * o0, axis=-1, keepdims=True)
    d1 = jnp.sum(before * o1, axis=-1, keepdims=True)
    dd = jnp.where(lane == 0, d0, jnp.where(lane == 1, d1, 0.0))
    dest_ref[0] = dd.T[:8, :].astype(jnp.int32)
    carry_ref[...] += jnp.sum(o, axis=0, keepdims=True)


def _dispatch_plan(route, counts, bm):
    t = route.shape[0]
    tb = min(1024, t)
    nb = t // tb
    return pl.pallas_call(
        functools.partial(_plan_kernel, bm=bm),
        grid=(nb,),
        in_specs=[pl.BlockSpec((tb, LANES), lambda i: (i, 0)),
                  pl.BlockSpec((8, LANES), lambda i: (0, 0))],
        out_specs=[pl.BlockSpec((1, 8, tb), lambda i: (i, 0, 0)),
                   pl.BlockSpec((8, LANES), lambda i: (0, 0))],
        out_shape=[jax.ShapeDtypeStruct((nb, 8, tb), jnp.int32), jax.ShapeDtypeStruct((8, LANES), f32)],
        scratch_shapes=[pltpu.VMEM((8, LANES), f32)],
        compiler_params=_cparams(("arbitrary",)),
        name="dispatch_plan",
    )(route, counts)


def _dispatch_kernel(d0_ref, d1_ref, fill_ref, end_ref, nu_ref, hs_ref, xs_hbm, zbuf, sem, zsem,
                     *, tb, bm, n_blk, unroll=8):
    i = pl.program_id(0)
    base = i * tb
    blk_rows = bm * PITCH

    def pad_copy(s, n):
        return pltpu.make_async_copy(zbuf.at[pl.ds(0, n * PITCH), :], xs_hbm.at[pl.ds(s * PITCH, n * PITCH), :], zsem)

    def tail_copy(g):
        return pltpu.make_async_copy(zbuf, xs_hbm.at[pl.ds(pl.multiple_of(g * blk_rows, 8), blk_rows), :], zsem)

    @pl.when(i == 0)
    def _():
        zbuf[...] = jnp.zeros(zbuf.shape, u32)
        for op in ("start", "wait"):
            def per_expert(e, c, op=op):
                s = fill_ref[e]
                for n in PAD_FILL_CHUNKS:
                    count = (end_ref[e] - s) // n

                    def one(k, c2, s=s, n=n):
                        getattr(pad_copy(s + k * n, n), op)()
                        return c2

                    c = lax.fori_loop(0, count, one, c)
                    s = s + count * n
                return c

            def per_tail(g, c, op=op):
                getattr(tail_copy(g), op)()
                return c

            lax.fori_loop(0, N_EXPERTS, per_expert, 0)
            lax.fori_loop(nu_ref[0], n_blk, per_tail, 0)

    def body(g, c):
        for u in range(unroll):
            r = g * unroll + u
            src = hs_ref.at[pl.ds(r * PITCH, PITCH), :]
            for k, dref in enumerate((d0_ref, d1_ref)):
                d = dref[base + r]
                pltpu.make_async_copy(src, xs_hbm.at[pl.ds(d * PITCH, PITCH), :], sem).start(priority=k)
        return c

    lax.fori_loop(0, tb // unroll, body, 0)
    for _ in range(TOP_K):
        pltpu.make_async_copy(xs_hbm.at[pl.ds(0, tb * PITCH), :], xs_hbm.at[pl.ds(0, tb * PITCH), :], sem).wait()


def _dispatch_rows(dest0, dest1, fill_start, seg_end, n_used, hs, cap):
    t = dest0.shape[0]
    tb = min(1024, t)
    bm = MOE_BLOCK
    grid_spec = pltpu.PrefetchScalarGridSpec(
        num_scalar_prefetch=5,
        grid=(t // tb,),
        in_specs=[pl.BlockSpec((tb * PITCH, LANES), lambda i, *_: (i, 0))],
        out_specs=pl.BlockSpec(memory_space=pl.ANY),
        scratch_shapes=[pltpu.VMEM((bm * PITCH, LANES), u32), pltpu.SemaphoreType.DMA(()),
                        pltpu.SemaphoreType.DMA(())],
    )
    return pl.pallas_call(
        functools.partial(_dispatch_kernel, tb=tb, bm=bm, n_blk=cap // bm),
        grid_spec=grid_spec,
        out_shape=jax.ShapeDtypeStruct((cap * PITCH, LANES), u32),
        compiler_params=_cparams(("arbitrary",)),
        name="dispatch_rows",
    )(dest0, dest1, fill_start, seg_end, n_used, hs)


def _moe_kernel(g0_ref, nb_ref, nu_ref, x_hbm, w1_ref, w3_ref, w2_ref, y_hbm,
                xbuf, ybuf, w1b, w3b, w2b, gsem, ysem, *, bm, n_blk):
    e = pl.program_id(0)
    n_used = nu_ref[0]
    g0 = g0_ref[e]
    nb = nb_ref[e]
    blk_rows = bm * PITCH

    def x_copy(slot, g):
        src = x_hbm.at[pl.ds(pl.multiple_of(g * blk_rows, 8), blk_rows), :]
        return pltpu.make_async_copy(src, xbuf.at[slot], gsem.at[slot])

    def y_copy(slot, g):
        dst = y_hbm.at[pl.ds(pl.multiple_of(g * blk_rows, 8), blk_rows), :]
        return pltpu.make_async_copy(ybuf.at[slot], dst, ysem.at[slot])

    @pl.when(e == 0)
    def _():
        x_copy(0, 0).start()
        for k in range(1, X_BUFS - 1):
            pl.when(n_used > k)(lambda k=k: x_copy(k, k).start())

    @pl.when(nb > 0)
    def _():
        w1b[...] = w1_ref[0].astype(bf16)
        w3b[...] = w3_ref[0].astype(bf16)
        w2b[...] = w2_ref[0].astype(bf16)

    def block(c, carry):
        g = g0 + c
        slot = g % 2
        xslot = g % X_BUFS

        ahead = g + (X_BUFS - 1)

        @pl.when(ahead < n_used)
        def _():
            x_copy(ahead % X_BUFS, ahead).start()

        x_copy(xslot, g).wait()

        hn = _slabs_to_rows(xbuf.at[xslot], bm).astype(bf16)
        a = jnp.dot(hn, w1b[...], preferred_element_type=f32)
        b = jnp.dot(hn, w3b[...], preferred_element_type=f32)
        mid = (a * jax.nn.sigmoid(a) * b).astype(bf16)
        y = jnp.dot(mid, w2b[...], preferred_element_type=f32)

        @pl.when(g >= 2)
        def _():
            y_copy(slot, g - 2).wait()

        _rows_to_slabs(ybuf.at[slot], y, bm)
        y_copy(slot, g).start()
        return carry

    lax.fori_loop(0, nb, block, 0)

    @pl.when(e == pl.num_programs(0) - 1)
    def _():
        @pl.when(n_used >= 2)
        def _():
            y_copy(n_used % 2, n_used - 2).wait()

        y_copy((n_used - 1) % 2, n_used - 1).wait()
        ybuf[0] = jnp.zeros(ybuf.shape[1:], u32)

        def zstart(g, carry):
            y_copy(0, g).start()
            return carry

        def zwait(g, carry):
            y_copy(0, g).wait()
            return carry

        lax.fori_loop(n_used, n_blk, zstart, 0)
        lax.fori_loop(n_used, n_blk, zwait, 0)


def _expert_mlp(blk_start, blk_count, n_used, xs, w1, w3, w2, cap):
    d = D_MODEL
    bm = MOE_BLOCK
    n_blk = cap // bm
    by_expert = lambda shape: pl.BlockSpec(shape, lambda e, g0, nb, nu: (e, 0, 0))
    grid_spec = pltpu.PrefetchScalarGridSpec(
        num_scalar_prefetch=3,
        grid=(N_EXPERTS,),
        in_specs=[pl.BlockSpec(memory_space=pl.ANY),
                  by_expert((1, d, D_EXPERT)), by_expert((1, d, D_EXPERT)), by_expert((1, D_EXPERT, d))],
        out_specs=pl.BlockSpec(memory_space=pl.ANY),
        scratch_shapes=[pltpu.VMEM((X_BUFS, bm * PITCH, LANES), u32), pltpu.VMEM((2, bm * PITCH, LANES), u32),
                        pltpu.VMEM((d, D_EXPERT), bf16), pltpu.VMEM((d, D_EXPERT), bf16),
                        pltpu.VMEM((D_EXPERT, d), bf16),
                        pltpu.SemaphoreType.DMA((X_BUFS,)), pltpu.SemaphoreType.DMA((2,))],
    )
    return pl.pallas_call(
        functools.partial(_moe_kernel, bm=bm, n_blk=n_blk),
        grid_spec=grid_spec,
        out_shape=jax.ShapeDtypeStruct((cap * PITCH, LANES), u32),
        compiler_params=_cparams(("arbitrary",)),
        name="expert_mlp",
    )(blk_start, blk_count, n_used, xs, w1, w3, w2)


def _combine_kernel(d0_ref, d1_ref, h_ref, route_ref, y_hbm, nw_ref, o_ref, ybuf, sem, *, tb):
    i = pl.program_id(0)
    n = pl.num_programs(0)
    slot = i % 2

    def start(blk, s):
        _slab_gather_start(d0_ref, blk * tb, tb, y_hbm, ybuf.at[s, 0], sem.at[s])
        _slab_gather_start(d1_ref, blk * tb, tb, y_hbm, ybuf.at[s, 1], sem.at[s])

    @pl.when(i == 0)
    def _():
        start(0, 0)

    @pl.when(i + 1 < n)
    def _():
        start(i + 1, 1 - slot)

    for k in range(TOP_K):
        _slab_gather_wait(tb, y_hbm, ybuf.at[slot, k], sem.at[slot])
    r = route_ref[...]
    h = h_ref[...] + (r[:, 2:3] * _slabs_to_rows(ybuf.at[slot, 0], tb)
                      + r[:, 3:4] * _slabs_to_rows(ybuf.at[slot, 1], tb))
    o_ref[...] = h * lax.rsqrt(jnp.mean(h * h, axis=-1, keepdims=True) + EPS) * nw_ref[...]


def _combine(dest0, dest1, h, route, ys, norm_w):
    t, d = h.shape
    tb = min(256, t)
    grid_spec = pltpu.PrefetchScalarGridSpec(
        num_scalar_prefetch=2,
        grid=(t // tb,),
        in_specs=[pl.BlockSpec((tb, d), lambda i, a, b: (i, 0)),
                  pl.BlockSpec((tb, LANES), lambda i, a, b: (i, 0)),
                  pl.BlockSpec(memory_space=pl.ANY),
                  pl.BlockSpec((1, d), lambda i, a, b: (0, 0))],
        out_specs=pl.BlockSpec((tb, d), lambda i, a, b: (i, 0)),
        scratch_shapes=[pltpu.VMEM((2, TOP_K, tb * PITCH, LANES), u32), pltpu.SemaphoreType.DMA((2,))],
    )
    return pl.pallas_call(
        functools.partial(_combine_kernel, tb=tb),
        grid_spec=grid_spec,
        out_shape=jax.ShapeDtypeStruct((t, d), f32),
        compiler_params=_cparams(("arbitrary",)),
        name="combine_final_norm",
    )(dest0, dest1, h, route, ys, norm_w)


def kernel(x, positions, norm_mix_w, w_in, fox_b_f, ret_gn_w, w_branch, b_gate, w_out, norm_ffn_w, w_router_group,
           b_router_group, w_router_expert, b_router_expert, w1, w3, w2, norm_final_w):
    b, s, d = x.shape
    assert d == D_MODEL and norm_mix_w.shape[0] == 1
    t = b * s
    x2 = x.reshape(t, d)

    wt = w_in[0].T
    o_f = 3 * FOX_WIDTH
    o_r = o_f + FOX_HEADS
    o_g = o_r + 4 * RET_WIDTH
    tile_rows = ([o_g + k * 1024 for k in range(N_BRANCH * D_MODEL // 1024)]
                 + [k * 1024 for k in range(o_f // 1024)]
                 + [o_r + k * 1024 for k in range(4 * RET_WIDTH // 1024)])
    w_main = _prep_in_weights(wt, tile_rows)
    w_f = jnp.pad(wt[o_f:o_r], ((0, LANES - FOX_HEADS), (0, 0))).astype(bf16)
    bias_f = jnp.pad(fox_b_f[0], (0, LANES - FOX_HEADS)).reshape(1, LANES)
    w_r = jnp.concatenate([w_router_group[0], w_router_expert[0]], axis=1)
    w_r = jnp.pad(w_r, ((0, 0), (0, LANES - w_r.shape[1])))
    wr_hi = w_r.astype(bf16)
    wr_lo = (w_r - wr_hi.astype(f32)).astype(bf16)
    b_r = jnp.pad(jnp.concatenate([b_router_group[0], b_router_expert[0]]), (0, LANES - N_GROUPS - N_EXPERTS))
    b_r = b_r.reshape(1, LANES)

    cos, sin = _rope_tables(positions)
    proj, f_logit = _in_projection(x2, norm_mix_w, w_main, w_f, cos.reshape(t, HEAD_DIM), sin.reshape(t, HEAD_DIM))
    c = _forget_cumsum(f_logit, bias_f, b, s)
    fox_o = _fox_attention(proj, c.reshape(b * FOX_HEADS, 1, s), b, s)
    ret_o = _retention(proj, ret_gn_w, b, s)
    merged = _branch_merge(fox_o, ret_o, proj, b_gate[0], w_branch[0].astype(bf16))
    h, h_slabs, route, counts = _outproj_router(merged, x2, w_out[0].astype(bf16), norm_ffn_w,
                                        jnp.concatenate([wr_hi, wr_lo], axis=1), b_r)

    bm = MOE_BLOCK
    cap = t * TOP_K + N_EXPERTS * bm
    dest_rows, seg = _dispatch_plan(route, counts, bm)
    dest0 = dest_rows[:, 0, :].reshape(t)
    dest1 = dest_rows[:, 1, :].reshape(t)
    pends = seg[0, :N_EXPERTS].astype(jnp.int32)
    padded = seg[1, :N_EXPERTS].astype(jnp.int32)
    counts = seg[2, :N_EXPERTS].astype(jnp.int32)
    pstarts = pends - padded
    n_used = pends[-1:] // bm

    xs = _dispatch_rows(dest0, dest1, pstarts + counts, pends, n_used, h_slabs, cap)
    y = _expert_mlp(pstarts // bm, padded // bm, n_used, xs, w1[0], w3[0], w2[0], cap)
    out = _combine(dest0, dest1, h, route, y, norm_final_w.reshape(1, d))
    return out.reshape(b, s, d)
```

```python
import functools
import math

import numpy as np
import jax
import jax.numpy as jnp
from jax import lax
from jax.experimental import pallas as pl
from jax.experimental.pallas import tpu as pltpu

D_MODEL = 2048
HEAD_DIM = 128
FOX_HEADS = 8
RET_HEADS = 8
FOX_WIDTH = FOX_HEADS * HEAD_DIM
RET_WIDTH = RET_HEADS * HEAD_DIM
N_BRANCH = 2
N_GROUPS = 4
EXPERTS_PER_GROUP = 8
N_EXPERTS = N_GROUPS * EXPERTS_PER_GROUP
TOP_K = 2
D_EXPERT = 512
ROPE_BASE = 10000.0
EPS = 1e-6

LANES = 128
NEG_BIG = -1e30
LOG2E = math.log2(math.e)
MAIN_WIDTH = 3 * FOX_WIDTH + 4 * RET_WIDTH + N_BRANCH * D_MODEL
COL_GATE = 0
COL_FQ = N_BRANCH * D_MODEL // HEAD_DIM
COL_FK = COL_FQ + FOX_HEADS
COL_FV = COL_FK + FOX_HEADS
COL_RQ = COL_FV + FOX_HEADS
COL_RK = COL_RQ + RET_HEADS
COL_RV = COL_RK + RET_HEADS
COL_RG = COL_RV + RET_HEADS

ATTN_BLOCK = 256
FOX_HEADS_PER_STEP = 4
RET_HEADS_PER_STEP = 1
RET_CHUNK = 256
MOE_BLOCK = 256
X_BUFS = 4
PAD_FILL_CHUNKS = (64, 8, 1)
VMEM_LIMIT = 56 * 1024 * 1024

f32 = jnp.float32
bf16 = jnp.bfloat16
_NT = (((1,), (1,)), ((), ()))
SLAB = D_MODEL // (2 * LANES)
PITCH = SLAB + 1
u32 = jnp.uint32
_HI_MASK = np.uint32(0xFFFF0000)


def _rows_to_slabs(ref2, val, n_rows):
    bits = lambda v: lax.bitcast_convert_type(v.astype(bf16).astype(f32), u32)
    for j in range(SLAB):
        lo = val[:, (2 * j) * LANES:(2 * j + 1) * LANES]
        hi = val[:, (2 * j + 1) * LANES:(2 * j + 2) * LANES]
        ref2[pl.ds(j, n_rows, stride=PITCH), :] = (bits(lo) >> 16) | (bits(hi) & _HI_MASK)
    ref2[pl.ds(SLAB, n_rows, stride=PITCH), :] = jnp.zeros((n_rows, LANES), u32)


def _slabs_to_rows(ref2, n_rows):
    chunks = []
    for j in range(SLAB):
        w = ref2[pl.ds(j, n_rows, stride=PITCH), :]
        chunks.append(lax.bitcast_convert_type(w << 16, f32))
        chunks.append(lax.bitcast_convert_type(w & _HI_MASK, f32))
    return jnp.concatenate(chunks, axis=-1)


def _slab_gather_start(idx_ref, base, n_rows, src_hbm, dst, sem, unroll=8):
    def body(g, _):
        for u in range(unroll):
            r = g * unroll + u
            t = idx_ref[base + r]
            pltpu.make_async_copy(src_hbm.at[pl.ds(t * PITCH, SLAB), :], dst.at[pl.ds(r * PITCH, SLAB), :],
                                  sem).start(priority=u % 2)
        return 0
    lax.fori_loop(0, n_rows // unroll, body, 0)


def _slab_gather_wait(n_rows, src_hbm, dst, sem):
    pltpu.make_async_copy(src_hbm.at[pl.ds(0, n_rows * SLAB), :], dst.at[pl.ds(0, n_rows * SLAB), :], sem).wait()


def _cparams(sem, vmem=VMEM_LIMIT):
    return pltpu.CompilerParams(dimension_semantics=sem, vmem_limit_bytes=vmem)


def _const_spec(shape):
    return pl.BlockSpec(shape, lambda i: tuple(0 for _ in shape), pipeline_mode=pl.Buffered(1))


def _rope_kernel(pos_ref, freq_ref, sgn_ref, cos_ref, sin_ref):
    pos = pos_ref[0].astype(f32)
    hs = pos.shape[0] // 2
    half = HEAD_DIM // 2
    lane = lax.broadcasted_iota(jnp.int32, (hs, HEAD_DIM), 1)
    low = lane < half
    ang = jnp.where(low, pos[:hs], pos[hs:]) * freq_ref[...]
    c = jnp.cos(ang)
    s = jnp.sin(ang) * sgn_ref[...]
    c_sw = pltpu.roll(c, half, axis=1)
    s_sw = pltpu.roll(s, half, axis=1)
    cos_ref[0, :hs, :] = jnp.where(low, c, c_sw)
    cos_ref[0, hs:, :] = jnp.where(low, c_sw, c)
    sin_ref[0, :hs, :] = jnp.where(low, s, -s_sw)
    sin_ref[0, hs:, :] = jnp.where(low, -s_sw, s)


def _rope_tables(positions):
    b, s = positions.shape
    half = HEAD_DIM // 2
    inv = (np.float32(ROPE_BASE) ** (-np.arange(half, dtype=np.float32) / np.float32(half))).astype(np.float32)
    freq = jnp.asarray(np.concatenate([inv, inv])[None, :])
    sgn = jnp.asarray(np.concatenate([-np.ones(half, np.float32), np.ones(half, np.float32)])[None, :])
    out = jax.ShapeDtypeStruct((b, s, HEAD_DIM), f32)
    return pl.pallas_call(
        _rope_kernel,
        grid=(b,),
        in_specs=[pl.BlockSpec((1, s, 1), lambda i: (i, 0, 0)),
                  pl.BlockSpec((1, HEAD_DIM), lambda i: (0, 0)),
                  pl.BlockSpec((1, HEAD_DIM), lambda i: (0, 0))],
        out_specs=[pl.BlockSpec((1, s, HEAD_DIM), lambda i: (i, 0, 0)),
                   pl.BlockSpec((1, s, HEAD_DIM), lambda i: (i, 0, 0))],
        out_shape=[out, out],
        compiler_params=_cparams(("parallel",)),
        name="rope_tables",
    )(positions.reshape(b, s, 1), freq, sgn)


_Q_FACTORS = ((COL_FQ * HEAD_DIM // 1024, LOG2E / math.sqrt(HEAD_DIM)),
              (COL_RQ * HEAD_DIM // 1024, 1.0 / math.sqrt(HEAD_DIM)))


def _wprep_kernel(off_ref, w_hbm, o_ref, buf, sem, *, tn):
    j = pl.program_id(0)
    n = pl.num_programs(0)
    slot = j % 2

    def copy(jj, s):
        src = w_hbm.at[pl.ds(pl.multiple_of(off_ref[jj], 8), tn), :]
        return pltpu.make_async_copy(src, buf.at[s], sem.at[s])

    @pl.when(j == 0)
    def _():
        copy(0, 0).start()

    @pl.when(j + 1 < n)
    def _():
        copy(j + 1, 1 - slot).start()

    copy(j, slot).wait()
    factor = jnp.float32(1.0)
    for tile, value in _Q_FACTORS:
        factor = jnp.where(j == tile, jnp.float32(value), factor)
    o_ref[...] = (buf[slot] * factor).astype(o_ref.dtype)


def _prep_in_weights(wt, row_offsets):
    tn = 1024
    d = wt.shape[1]
    n_tiles = len(row_offsets)
    grid_spec = pltpu.PrefetchScalarGridSpec(
        num_scalar_prefetch=1,
        grid=(n_tiles,),
        in_specs=[pl.BlockSpec(memory_space=pl.ANY)],
        out_specs=pl.BlockSpec((tn, d), lambda j, off: (j, 0)),
        scratch_shapes=[pltpu.VMEM((2, tn, d), f32), pltpu.SemaphoreType.DMA((2,))],
    )
    return pl.pallas_call(
        functools.partial(_wprep_kernel, tn=tn),
        grid_spec=grid_spec,
        out_shape=jax.ShapeDtypeStruct((n_tiles * tn, d), bf16),
        compiler_params=_cparams(("arbitrary",)),
        name="prep_in_weights",
    )(jnp.asarray(row_offsets, jnp.int32), wt)


_ROT_TILES = (COL_RQ * HEAD_DIM // 1024, COL_RK * HEAD_DIM // 1024)


def _inproj_kernel(x_ref, nw_ref, w_ref, wf_ref, cos_ref, sin_ref, o_ref, f_ref, xn_ref):
    j = pl.program_id(1)

    @pl.when(j == 0)
    def _():
        x = x_ref[...]
        ms = jnp.mean(x * x, axis=-1, keepdims=True)
        xn = (x * lax.rsqrt(ms + EPS) * nw_ref[...]).astype(bf16)
        xn_ref[...] = xn
        f_ref[...] = lax.dot_general(xn, wf_ref[...], _NT, preferred_element_type=f32)

    rotated = (j == _ROT_TILES[0]) | (j == _ROT_TILES[1])

    @pl.when(jnp.logical_not(rotated))
    def _():
        o_ref[...] = lax.dot_general(xn_ref[...], w_ref[...], _NT, preferred_element_type=f32).astype(o_ref.dtype)

    @pl.when(rotated)
    def _():
        acc = lax.dot_general(xn_ref[...], w_ref[...], _NT, preferred_element_type=f32)
        cos = cos_ref[...]
        sin = sin_ref[...]
        for hh in range(acc.shape[1] // HEAD_DIM):
            lanes = slice(hh * HEAD_DIM, (hh + 1) * HEAD_DIM)
            t = acc[:, lanes]
            o_ref[:, lanes] = (t * cos + pltpu.roll(t, HEAD_DIM // 2, axis=1) * sin).astype(o_ref.dtype)


def _in_projection(x2, norm_w, w_main_t, w_f_t, cos, sin):
    t, d = x2.shape
    n = w_main_t.shape[0]
    tm = min(1024, t)
    tn = 1024
    return pl.pallas_call(
        _inproj_kernel,
        grid=(t // tm, n // tn),
        in_specs=[pl.BlockSpec((tm, d), lambda i, j: (i, 0)),
                  pl.BlockSpec((1, d), lambda i, j: (0, 0)),
                  pl.BlockSpec((tn, d), lambda i, j: (j, 0)),
                  pl.BlockSpec((LANES, d), lambda i, j: (0, 0)),
                  pl.BlockSpec((tm, HEAD_DIM), lambda i, j: (i, 0)),
                  pl.BlockSpec((tm, HEAD_DIM), lambda i, j: (i, 0))],
        out_specs=[pl.BlockSpec((tm, tn), lambda i, j: (i, j)),
                   pl.BlockSpec((tm, LANES), lambda i, j: (i, 0))],
        out_shape=[jax.ShapeDtypeStruct((t, n), bf16), jax.ShapeDtypeStruct((t, LANES), f32)],
        scratch_shapes=[pltpu.VMEM((tm, d), bf16)],
        compiler_params=_cparams(("parallel", "arbitrary")),
        name="in_projection",
    )(x2, norm_w, w_main_t, w_f_t, cos, sin)


def _fcum_kernel(f_ref, b_ref, c_ref):
    z = f_ref[...] + b_ref[...]
    ls = jnp.minimum(z, 0.0) - jnp.log1p(jnp.exp(-jnp.abs(z)))
    c = ls.T[:FOX_HEADS, :]
    s = c.shape[1]
    lane = lax.broadcasted_iota(jnp.int32, c.shape, 1)
    sh = 1
    while sh < s:
        c = c + jnp.where(lane >= sh, pltpu.roll(c, sh, axis=1), 0.0)
        sh *= 2
    c_ref[0] = c * LOG2E


def _forget_cumsum(f_logit, bias, b, s):
    return pl.pallas_call(
        _fcum_kernel,
        grid=(b,),
        in_specs=[pl.BlockSpec((s, LANES), lambda i: (i, 0)),
                  pl.BlockSpec((1, LANES), lambda i: (0, 0))],
        out_specs=pl.BlockSpec((1, FOX_HEADS, s), lambda i: (i, 0, 0)),
        out_shape=jax.ShapeDtypeStruct((b, FOX_HEADS, s), f32),
        compiler_params=_cparams(("parallel",)),
        name="forget_cumsum",
    )(f_logit, bias)


def _fox_kernel(q_ref, k_ref, v_ref, c_ref, o_ref, va_ref, acc_ref, *, tq):
    s_len = q_ref.shape[0]
    nq = s_len // tq
    n_heads = q_ref.shape[1] // HEAD_DIM
    row = lax.broadcasted_iota(jnp.int32, (tq, tq), 0)
    col = lax.broadcasted_iota(jnp.int32, (tq, tq), 1)
    causal = row >= col
    def scores(hh):
        lanes = slice(hh * HEAD_DIM, (hh + 1) * HEAD_DIM)
        va_ref[hh, :, :HEAD_DIM] = v_ref[:, lanes]
        va_ref[hh, :, HEAD_DIM:] = jnp.ones((s_len, HEAD_DIM), va_ref.dtype)
        panels = []
        row_max = [None] * nq
        for kb in range(nq):
            r0 = kb * tq
            s = lax.dot_general(q_ref[r0:, lanes], k_ref[r0:r0 + tq, lanes], _NT, preferred_element_type=f32)
            s = s - c_ref[hh, :, r0:r0 + tq]
            parts = [jnp.where(causal, s[:tq], NEG_BIG)]
            parts += [s[(j - kb) * tq:(j - kb + 1) * tq] for j in range(kb + 1, nq)]
            panels.append(parts)
            for j, part in zip(range(kb, nq), parts):
                pm = jnp.max(part, axis=-1, keepdims=True)
                row_max[j] = pm if row_max[j] is None else jnp.maximum(row_max[j], pm)
        return panels, row_max

    def values(hh, panels, row_max):
        lanes = slice(hh * HEAD_DIM, (hh + 1) * HEAD_DIM)
        for kb in range(nq):
            r0 = kb * tq
            p = jnp.concatenate([jnp.exp2(part - row_max[j]).astype(bf16)
                                 for j, part in zip(range(kb, nq), panels[kb])], axis=0)
            contrib = jnp.dot(p, va_ref[hh, r0:r0 + tq, :], preferred_element_type=f32)
            if kb:
                acc_ref[hh, r0:, :] += contrib
            else:
                acc_ref[hh] = contrib
        acc = acc_ref[hh]
        o_ref[:, lanes] = (acc[:, :HEAD_DIM] / acc[:, HEAD_DIM:]).astype(o_ref.dtype)

    pending = None
    for hh in range(n_heads):
        current = (hh,) + scores(hh)
        if pending is not None:
            values(*pending)
        pending = current
    values(*pending)


def _fox_attention(proj, c_blk, b, s):
    tq = min(ATTN_BLOCK, s)
    hp = FOX_HEADS_PER_STEP
    w = hp * HEAD_DIM
    col = lambda c0: pl.BlockSpec((s, w), lambda i, j: (i, c0 // hp + j))
    return pl.pallas_call(
        functools.partial(_fox_kernel, tq=tq),
        grid=(b, FOX_HEADS // hp),
        in_specs=[col(COL_FQ), col(COL_FK), col(COL_FV),
                  pl.BlockSpec((hp, 1, s), lambda i, j: (i * (FOX_HEADS // hp) + j, 0, 0))],
        out_specs=pl.BlockSpec((s, w), lambda i, j: (i, j)),
        out_shape=jax.ShapeDtypeStruct((b * s, FOX_WIDTH), bf16),
        scratch_shapes=[pltpu.VMEM((hp, s, 2 * HEAD_DIM), bf16), pltpu.VMEM((hp, s, 2 * HEAD_DIM), f32)],
        compiler_params=_cparams(("parallel", "parallel")),
        name="fox_attention",
    )(proj, proj, proj, c_blk)


def _ret_kernel(q_ref, k_ref, v_ref, g_ref, dec_ref, xi_ref, zeta_ref, gc_ref, gnw_ref, o_ref, *, chunk):
    s_len = q_ref.shape[0]
    nc = s_len // chunk
    n_heads = q_ref.shape[1] // HEAD_DIM
    for hh in range(n_heads):
        lanes = slice(hh * HEAD_DIM, (hh + 1) * HEAD_DIM)
        state = jnp.zeros((HEAD_DIM, HEAD_DIM), f32)
        for ci in range(nc):
            rows = slice(ci * chunk, (ci + 1) * chunk)
            qb = q_ref[rows, lanes]
            kb = k_ref[rows, lanes]
            v = v_ref[rows, lanes]
            inner = lax.dot_general(qb, kb, _NT, preferred_element_type=f32) * dec_ref[hh]
            o = jnp.dot(inner.astype(bf16), v, preferred_element_type=f32)
            if ci:
                o = o + jnp.dot(qb, state.astype(bf16), preferred_element_type=f32) * xi_ref[hh]
            if ci + 1 < nc:
                kz = (kb.astype(f32) * zeta_ref[hh]).astype(bf16)
                kv = lax.dot_general(kz, v, (((0,), (0,)), ((), ())), preferred_element_type=f32)
                state = state * gc_ref[hh] + kv if ci else kv
            mu = jnp.mean(o, axis=-1, keepdims=True)
            oc = o - mu
            var = jnp.mean(oc * oc, axis=-1, keepdims=True)
            y = oc * lax.rsqrt(var + EPS) * gnw_ref[:, lanes]
            g = g_ref[rows, lanes].astype(f32)
            o_ref[rows, lanes] = (g * jax.nn.sigmoid(g) * y).astype(o_ref.dtype)


def _retention_constants(chunk):
    hh = np.arange(RET_HEADS, dtype=np.float64)
    log_gamma = np.log1p(-(2.0 ** (-5.0 - hh)))
    n = np.arange(chunk, dtype=np.float64)
    diff = n[:, None] - n[None, :]
    decay = np.where(diff[None] >= 0, np.exp(diff[None] * log_gamma[:, None, None]), 0.0)
    xi = np.exp((n[None, :] + 1.0) * log_gamma[:, None])
    zeta = np.exp((chunk - 1.0 - n[None, :]) * log_gamma[:, None])
    g_chunk = np.exp(chunk * log_gamma)
    bc = lambda a: np.broadcast_to(a[:, :, None], (RET_HEADS, chunk, HEAD_DIM))
    gc = np.broadcast_to(g_chunk[:, None, None], (RET_HEADS, 1, HEAD_DIM))
    to = lambda a: jnp.asarray(np.ascontiguousarray(a), dtype=f32)
    return to(decay), to(bc(xi)), to(bc(zeta)), to(gc)


def _retention(proj, gn_w, b, s):
    chunk = min(RET_CHUNK, s)
    decay, xi, zeta, gc = _retention_constants(chunk)
    hp = RET_HEADS_PER_STEP
    w = hp * HEAD_DIM
    head = lambda col: pl.BlockSpec((s, w), lambda i, j: (i, col // hp + j))
    per_head = lambda shape: pl.BlockSpec((hp,) + shape, lambda i, j: (j, 0, 0))
    return pl.pallas_call(
        functools.partial(_ret_kernel, chunk=chunk),
        grid=(b, RET_HEADS // hp),
        in_specs=[head(COL_RQ), head(COL_RK), head(COL_RV), head(COL_RG),
                  per_head((chunk, chunk)), per_head((chunk, HEAD_DIM)), per_head((chunk, HEAD_DIM)),
                  per_head((1, HEAD_DIM)),
                  pl.BlockSpec((1, w), lambda i, j: (0, j))],
        out_specs=pl.BlockSpec((s, w), lambda i, j: (i, j)),
        out_shape=jax.ShapeDtypeStruct((b * s, RET_WIDTH), bf16),
        compiler_params=_cparams(("parallel", "parallel")),
        name="retention",
    )(proj, proj, proj, proj, decay, xi, zeta, gc, gn_w)


def _branch_merge_kernel(fo_ref, ro_ref, g0_ref, g1_ref, bg_ref, wb_ref, mg_ref, *, ncol):
    d = mg_ref.shape[1]
    cw = d // ncol
    fo = fo_ref[...]
    ro = ro_ref[...]
    for c in range(ncol):
        cs = slice(c * cw, (c + 1) * cw)
        bd0 = jnp.dot(fo, wb_ref[0, :, cs], preferred_element_type=f32)
        bd1 = jnp.dot(ro, wb_ref[1, :, cs], preferred_element_type=f32)
        ga = jax.nn.sigmoid(g0_ref[:, cs].astype(f32) + bg_ref[0:1, cs])
        gb = jax.nn.sigmoid(g1_ref[:, cs].astype(f32) + bg_ref[1:2, cs])
        mg_ref[:, cs] = (ga * bd0 + gb * bd1).astype(mg_ref.dtype)


def _branch_merge(fox_o, ret_o, proj, b_gate, w_branch):
    t = fox_o.shape[0]
    d = D_MODEL
    tm = min(1024, t)
    gate0 = COL_GATE * HEAD_DIM // d
    return pl.pallas_call(
        functools.partial(_branch_merge_kernel, ncol=4),
        grid=(t // tm,),
        in_specs=[pl.BlockSpec((tm, FOX_WIDTH), lambda i: (i, 0)),
                  pl.BlockSpec((tm, RET_WIDTH), lambda i: (i, 0)),
                  pl.BlockSpec((tm, d), lambda i: (i, gate0)),
                  pl.BlockSpec((tm, d), lambda i: (i, gate0 + 1)),
                  _const_spec((N_BRANCH, d)),
                  _const_spec((N_BRANCH, FOX_WIDTH, d))],
        out_specs=pl.BlockSpec((tm, d), lambda i: (i, 0)),
        out_shape=jax.ShapeDtypeStruct((t, d), bf16),
        compiler_params=_cparams(("parallel",)),
        name="branch_merge",
    )(fox_o, ret_o, proj, proj, b_gate, w_branch)


def _route(logits):
    lane = lax.broadcasted_iota(jnp.int32, logits.shape, 1).astype(f32)
    far = jnp.float32(1e9)
    gmask = lane < N_GROUPS
    gl = jnp.where(gmask, logits, NEG_BIG)
    gmax = jnp.max(gl, axis=-1, keepdims=True)
    gsel = jnp.min(jnp.where(gl == gmax, lane, far), axis=-1, keepdims=True)
    p_group = 1.0 / jnp.sum(jnp.where(gmask, jnp.exp(gl - gmax), 0.0), axis=-1, keepdims=True)
    lo = N_GROUPS + EXPERTS_PER_GROUP * gsel
    emask = (lane >= lo) & (lane < lo + EXPERTS_PER_GROUP)
    el = jnp.where(emask, logits, NEG_BIG)
    m1 = jnp.max(el, axis=-1, keepdims=True)
    i1 = jnp.min(jnp.where(el == m1, lane, far), axis=-1, keepdims=True)
    el2 = jnp.where(lane == i1, NEG_BIG, el)
    m2 = jnp.max(el2, axis=-1, keepdims=True)
    i2 = jnp.min(jnp.where(el2 == m2, lane, far), axis=-1, keepdims=True)
    t = jnp.exp(m2 - m1)
    w1 = p_group / (1.0 + t)
    w2 = p_group * t / (1.0 + t)
    out = jnp.where(lane == 0, i1 - N_GROUPS, 0.0)
    out = jnp.where(lane == 1, i2 - N_GROUPS, out)
    out = jnp.where(lane == 2, w1, out)
    out = jnp.where(lane == 3, w2, out)
    return out


def _outproj_kernel(mg_ref, x_ref, wo_ref, nfw_ref, wr_ref, br_ref, h_ref, hs_ref, route_ref, cnt_ref):
    h = x_ref[...] + jnp.dot(mg_ref[...], wo_ref[...], preferred_element_type=f32)
    h_ref[...] = h
    hn = h * lax.rsqrt(jnp.mean(h * h, axis=-1, keepdims=True) + EPS) * nfw_ref[...]
    _rows_to_slabs(hs_ref, hn, hn.shape[0])
    hi = hn.astype(bf16)
    lo = (hn - hi.astype(f32)).astype(bf16)
    hw = jnp.dot(hi, wr_ref[...], preferred_element_type=f32)
    logits = (hw[:, :LANES] + jnp.dot(lo, wr_ref[:, :LANES], preferred_element_type=f32) + hw[:, LANES:]) + br_ref[...]
    route = _route(logits)
    route_ref[...] = route

    @pl.when(pl.program_id(0) == 0)
    def _():
        cnt_ref[...] = jnp.zeros_like(cnt_ref)

    lane = lax.broadcasted_iota(jnp.int32, route.shape, 1).astype(f32)
    chosen = (lane == route[:, 0:1]).astype(f32) + (lane == route[:, 1:2]).astype(f32)
    cnt_ref[...] += jnp.sum(chosen, axis=0, keepdims=True)


def _outproj_router(merged, x2, w_out, nfw, wr_cat, b_router):
    t, d = x2.shape
    tm = min(512, t)
    return pl.pallas_call(
        _outproj_kernel,
        grid=(t // tm,),
        in_specs=[pl.BlockSpec((tm, d), lambda i: (i, 0)),
                  pl.BlockSpec((tm, d), lambda i: (i, 0)),
                  _const_spec((d, d)),
                  _const_spec((1, d)),
                  _const_spec((d, 2 * LANES)), _const_spec((1, LANES))],
        out_specs=[pl.BlockSpec((tm, d), lambda i: (i, 0)),
                   pl.BlockSpec((tm * PITCH, LANES), lambda i: (i, 0)),
                   pl.BlockSpec((tm, LANES), lambda i: (i, 0)),
                   pl.BlockSpec((8, LANES), lambda i: (0, 0))],
        out_shape=[jax.ShapeDtypeStruct((t, d), f32), jax.ShapeDtypeStruct((t * PITCH, LANES), u32),
                   jax.ShapeDtypeStruct((t, LANES), f32), jax.ShapeDtypeStruct((8, LANES), f32)],
        compiler_params=_cparams(("arbitrary",)),
        name="outproj_router",
    )(merged, x2, w_out, nfw, wr_cat, b_router)


def _plan_kernel(route_ref, cnt_ref, dest_ref, ends_ref, carry_ref, *, bm):
    r = route_ref[...]
    tb = r.shape[0]
    lane = lax.broadcasted_iota(jnp.int32, r.shape, 1).astype(f32)
    o0 = (lane == r[:, 0:1]).astype(f32)
    o1 = (lane == r[:, 1:2]).astype(f32)
    o = o0 + o1

    @pl.when(pl.program_id(0) == 0)
    def _():
        cnt = cnt_ref[...]
        padded = jnp.floor((cnt + (bm - 1)) * (1.0 / bm)) * bm
        lane8 = lax.broadcasted_iota(jnp.int32, cnt.shape, 1)
        ends = padded
        sh = 1
        while sh < LANES:
            ends = ends + jnp.where(lane8 >= sh, pltpu.roll(ends, sh, axis=1), 0.0)
            sh *= 2
        sub = lax.broadcasted_iota(jnp.int32, cnt.shape, 0)
        ends_ref[...] = jnp.where(sub == 0, ends, jnp.where(sub == 1, padded, cnt))
        carry_ref[...] = ends - padded

    row = lax.broadcasted_iota(jnp.int32, (tb, tb), 0)
    col = lax.broadcasted_iota(jnp.int32, (tb, tb), 1)
    lower = jnp.where(col < row, 1.0, 0.0).astype(bf16)
    before = jnp.dot(lower, o.astype(bf16), preferred_element_type=f32) + carry_ref[0:1, :]
    d0 = jnp.sum(before * o0, axis=-1, keepdims=True)
    d1 = jnp.sum(before * o1, axis=-1, keepdims=True)
    dd = jnp.where(lane == 0, d0, jnp.where(lane == 1, d1, 0.0))
    dest_ref[0] = dd.T[:8, :].astype(jnp.int32)
    carry_ref[...] += jnp.sum(o, axis=0, keepdims=True)


def _dispatch_plan(route, counts, bm):
    t = route.shape[0]
    tb = min(1024, t)
    nb = t // tb
    return pl.pallas_call(
        functools.partial(_plan_kernel, bm=bm),
        grid=(nb,),
        in_specs=[pl.BlockSpec((tb, LANES), lambda i: (i, 0)),
                  pl.BlockSpec((8, LANES), lambda i: (0, 0))],
        out_specs=[pl.BlockSpec((1, 8, tb), lambda i: (i, 0, 0)),
                   pl.BlockSpec((8, LANES), lambda i: (0, 0))],
        out_shape=[jax.ShapeDtypeStruct((nb, 8, tb), jnp.int32), jax.ShapeDtypeStruct((8, LANES), f32)],
        scratch_shapes=[pltpu.VMEM((8, LANES), f32)],
        compiler_params=_cparams(("arbitrary",)),
        name="dispatch_plan",
    )(route, counts)


def _dispatch_kernel(d0_ref, d1_ref, fill_ref, end_ref, nu_ref, hs_ref, xs_hbm, zbuf, sem, zsem,
                     *, tb, bm, n_blk, unroll=8):
    i = pl.program_id(0)
    base = i * tb
    blk_rows = bm * PITCH

    def pad_copy(s, n):
        return pltpu.make_async_copy(zbuf.at[pl.ds(0, n * PITCH), :], xs_hbm.at[pl.ds(s * PITCH, n * PITCH), :], zsem)

    def tail_copy(g):
        return pltpu.make_async_copy(zbuf, xs_hbm.at[pl.ds(pl.multiple_of(g * blk_rows, 8), blk_rows), :], zsem)

    @pl.when(i == 0)
    def _():
        zbuf[...] = jnp.zeros(zbuf.shape, u32)
        for op in ("start", "wait"):
            def per_expert(e, c, op=op):
                s = fill_ref[e]
                for n in PAD_FILL_CHUNKS:
                    count = (end_ref[e] - s) // n

                    def one(k, c2, s=s, n=n):
                        getattr(pad_copy(s + k * n, n), op)()
                        return c2

                    c = lax.fori_loop(0, count, one, c)
                    s = s + count * n
                return c

            def per_tail(g, c, op=op):
                getattr(tail_copy(g), op)()
                return c

            lax.fori_loop(0, N_EXPERTS, per_expert, 0)
            lax.fori_loop(nu_ref[0], n_blk, per_tail, 0)

    def body(g, c):
        for u in range(unroll):
            r = g * unroll + u
            src = hs_ref.at[pl.ds(r * PITCH, PITCH), :]
            for k, dref in enumerate((d0_ref, d1_ref)):
                d = dref[base + r]
                pltpu.make_async_copy(src, xs_hbm.at[pl.ds(d * PITCH, PITCH), :], sem).start(priority=k)
        return c

    lax.fori_loop(0, tb // unroll, body, 0)
    for _ in range(TOP_K):
        pltpu.make_async_copy(xs_hbm.at[pl.ds(0, tb * PITCH), :], xs_hbm.at[pl.ds(0, tb * PITCH), :], sem).wait()


def _dispatch_rows(dest0, dest1, fill_start, seg_end, n_used, hs, cap):
    t = dest0.shape[0]
    tb = min(1024, t)
    bm = MOE_BLOCK
    grid_spec = pltpu.PrefetchScalarGridSpec(
        num_scalar_prefetch=5,
        grid=(t // tb,),
        in_specs=[pl.BlockSpec((tb * PITCH, LANES), lambda i, *_: (i, 0))],
        out_specs=pl.BlockSpec(memory_space=pl.ANY),
        scratch_shapes=[pltpu.VMEM((bm * PITCH, LANES), u32), pltpu.SemaphoreType.DMA(()),
                        pltpu.SemaphoreType.DMA(())],
    )
    return pl.pallas_call(
        functools.partial(_dispatch_kernel, tb=tb, bm=bm, n_blk=cap // bm),
        grid_spec=grid_spec,
        out_shape=jax.ShapeDtypeStruct((cap * PITCH, LANES), u32),
        compiler_params=_cparams(("arbitrary",)),
        name="dispatch_rows",
    )(dest0, dest1, fill_start, seg_end, n_used, hs)


def _moe_kernel(g0_ref, nb_ref, nu_ref, x_hbm, w1_ref, w3_ref, w2_ref, y_hbm,
                xbuf, ybuf, w1b, w3b, w2b, gsem, ysem, *, bm, n_blk):
    e = pl.program_id(0)
    n_used = nu_ref[0]
    g0 = g0_ref[e]
    nb = nb_ref[e]
    blk_rows = bm * PITCH

    def x_copy(slot, g):
        src = x_hbm.at[pl.ds(pl.multiple_of(g * blk_rows, 8), blk_rows), :]
        return pltpu.make_async_copy(src, xbuf.at[slot], gsem.at[slot])

    def y_copy(slot, g):
        dst = y_hbm.at[pl.ds(pl.multiple_of(g * blk_rows, 8), blk_rows), :]
        return pltpu.make_async_copy(ybuf.at[slot], dst, ysem.at[slot])

    @pl.when(e == 0)
    def _():
        x_copy(0, 0).start()
        for k in range(1, X_BUFS - 1):
            pl.when(n_used > k)(lambda k=k: x_copy(k, k).start())

    @pl.when(nb > 0)
    def _():
        w1b[...] = w1_ref[0].astype(bf16)
        w3b[...] = w3_ref[0].astype(bf16)
        w2b[...] = w2_ref[0].astype(bf16)

    def block(c, carry):
        g = g0 + c
        slot = g % 2
        xslot = g % X_BUFS

        ahead = g + (X_BUFS - 1)

        @pl.when(ahead < n_used)
        def _():
            x_copy(ahead % X_BUFS, ahead).start()

        x_copy(xslot, g).wait()

        hn = _slabs_to_rows(xbuf.at[xslot], bm).astype(bf16)
        a = jnp.dot(hn, w1b[...], preferred_element_type=f32)
        b = jnp.dot(hn, w3b[...], preferred_element_type=f32)
        mid = (a * jax.nn.sigmoid(a) * b).astype(bf16)
        y = jnp.dot(mid, w2b[...], preferred_element_type=f32)

        @pl.when(g >= 2)
        def _():
            y_copy(slot, g - 2).wait()

        _rows_to_slabs(ybuf.at[slot], y, bm)
        y_copy(slot, g).start()
        return carry

    lax.fori_loop(0, nb, block, 0)

    @pl.when(e == pl.num_programs(0) - 1)
    def _():
        @pl.when(n_used >= 2)
        def _():
            y_copy(n_used % 2, n_used - 2).wait()

        y_copy((n_used - 1) % 2, n_used - 1).wait()
        ybuf[0] = jnp.zeros(ybuf.shape[1:], u32)

        def zstart(g, carry):
            y_copy(0, g).start()
            return carry

        def zwait(g, carry):
            y_copy(0, g).wait()
            return carry

        lax.fori_loop(n_used, n_blk, zstart, 0)
        lax.fori_loop(n_used, n_blk, zwait, 0)


def _expert_mlp(blk_start, blk_count, n_used, xs, w1, w3, w2, cap):
    d = D_MODEL
    bm = MOE_BLOCK
    n_blk = cap // bm
    by_expert = lambda shape: pl.BlockSpec(shape, lambda e, g0, nb, nu: (e, 0, 0))
    grid_spec = pltpu.PrefetchScalarGridSpec(
        num_scalar_prefetch=3,
        grid=(N_EXPERTS,),
        in_specs=[pl.BlockSpec(memory_space=pl.ANY),
                  by_expert((1, d, D_EXPERT)), by_expert((1, d, D_EXPERT)), by_expert((1, D_EXPERT, d))],
        out_specs=pl.BlockSpec(memory_space=pl.ANY),
        scratch_shapes=[pltpu.VMEM((X_BUFS, bm * PITCH, LANES), u32), pltpu.VMEM((2, bm * PITCH, LANES), u32),
                        pltpu.VMEM((d, D_EXPERT), bf16), pltpu.VMEM((d, D_EXPERT), bf16),
                        pltpu.VMEM((D_EXPERT, d), bf16),
                        pltpu.SemaphoreType.DMA((X_BUFS,)), pltpu.SemaphoreType.DMA((2,))],
    )
    return pl.pallas_call(
        functools.partial(_moe_kernel, bm=bm, n_blk=n_blk),
        grid_spec=grid_spec,
        out_shape=jax.ShapeDtypeStruct((cap * PITCH, LANES), u32),
        compiler_params=_cparams(("arbitrary",)),
        name="expert_mlp",
    )(blk_start, blk_count, n_used, xs, w1, w3, w2)


def _combine_kernel(d0_ref, d1_ref, h_ref, route_ref, y_hbm, nw_ref, o_ref, ybuf, sem, *, tb):
    i = pl.program_id(0)
    n = pl.num_programs(0)
    slot = i % 2

    def start(blk, s):
        _slab_gather_start(d0_ref, blk * tb, tb, y_hbm, ybuf.at[s, 0], sem.at[s])
        _slab_gather_start(d1_ref, blk * tb, tb, y_hbm, ybuf.at[s, 1], sem.at[s])

    @pl.when(i == 0)
    def _():
        start(0, 0)

    @pl.when(i + 1 < n)
    def _():
        start(i + 1, 1 - slot)

    for k in range(TOP_K):
        _slab_gather_wait(tb, y_hbm, ybuf.at[slot, k], sem.at[slot])
    r = route_ref[...]
    h = h_ref[...] + (r[:, 2:3] * _slabs_to_rows(ybuf.at[slot, 0], tb)
                      + r[:, 3:4] * _slabs_to_rows(ybuf.at[slot, 1], tb))
    o_ref[...] = h * lax.rsqrt(jnp.mean(h * h, axis=-1, keepdims=True) + EPS) * nw_ref[...]


def _combine(dest0, dest1, h, route, ys, norm_w):
    t, d = h.shape
    tb = min(256, t)
    grid_spec = pltpu.PrefetchScalarGridSpec(
        num_scalar_prefetch=2,
        grid=(t // tb,),
        in_specs=[pl.BlockSpec((tb, d), lambda i, a, b: (i, 0)),
                  pl.BlockSpec((tb, LANES), lambda i, a, b: (i, 0)),
                  pl.BlockSpec(memory_space=pl.ANY),
                  pl.BlockSpec((1, d), lambda i, a, b: (0, 0))],
        out_specs=pl.BlockSpec((tb, d), lambda i, a, b: (i, 0)),
        scratch_shapes=[pltpu.VMEM((2, TOP_K, tb * PITCH, LANES), u32), pltpu.SemaphoreType.DMA((2,))],
    )
    return pl.pallas_call(
        functools.partial(_combine_kernel, tb=tb),
        grid_spec=grid_spec,
        out_shape=jax.ShapeDtypeStruct((t, d), f32),
        compiler_params=_cparams(("arbitrary",)),
        name="combine_final_norm",
    )(dest0, dest1, h, route, ys, norm_w)


def kernel(x, positions, norm_mix_w, w_in, fox_b_f, ret_gn_w, w_branch, b_gate, w_out, norm_ffn_w, w_router_group,
           b_router_group, w_router_expert, b_router_expert, w1, w3, w2, norm_final_w):
    b, s, d = x.shape
    assert d == D_MODEL and norm_mix_w.shape[0] == 1
    t = b * s
    x2 = x.reshape(t, d)

    wt = w_in[0].T
    o_f = 3 * FOX_WIDTH
    o_r = o_f + FOX_HEADS
    o_g = o_r + 4 * RET_WIDTH
    tile_rows = ([o_g + k * 1024 for k in range(N_BRANCH * D_MODEL // 1024)]
                 + [k * 1024 for k in range(o_f // 1024)]
                 + [o_r + k * 1024 for k in range(4 * RET_WIDTH // 1024)])
    w_main = _prep_in_weights(wt, tile_rows)
    w_f = jnp.pad(wt[o_f:o_r], ((0, LANES - FOX_HEADS), (0, 0))).astype(bf16)
    bias_f = jnp.pad(fox_b_f[0], (0, LANES - FOX_HEADS)).reshape(1, LANES)
    w_r = jnp.concatenate([w_router_group[0], w_router_expert[0]], axis=1)
    w_r = jnp.pad(w_r, ((0, 0), (0, LANES - w_r.shape[1])))
    wr_hi = w_r.astype(bf16)
    wr_lo = (w_r - wr_hi.astype(f32)).astype(bf16)
    b_r = jnp.pad(jnp.concatenate([b_router_group[0], b_router_expert[0]]), (0, LANES - N_GROUPS - N_EXPERTS))
    b_r = b_r.reshape(1, LANES)

    cos, sin = _rope_tables(positions)
    proj, f_logit = _in_projection(x2, norm_mix_w, w_main, w_f, cos.reshape(t, HEAD_DIM), sin.reshape(t, HEAD_DIM))
    c = _forget_cumsum(f_logit, bias_f, b, s)
    fox_o = _fox_attention(proj, c.reshape(b * FOX_HEADS, 1, s), b, s)
    ret_o = _retention(proj, ret_gn_w, b, s)
    merged = _branch_merge(fox_o, ret_o, proj, b_gate[0], w_branch[0].astype(bf16))
    h, h_slabs, route, counts = _outproj_router(merged, x2, w_out[0].astype(bf16), norm_ffn_w,
                                        jnp.concatenate([wr_hi, wr_lo], axis=1), b_r)

    bm = MOE_BLOCK
    cap = t * TOP_K + N_EXPERTS * bm
    dest_rows, seg = _dispatch_plan(route, counts, bm)
    dest0 = dest_rows[:, 0, :].reshape(t)
    dest1 = dest_rows[:, 1, :].reshape(t)
    pends = seg[0, :N_EXPERTS].astype(jnp.int32)
    padded = seg[1, :N_EXPERTS].astype(jnp.int32)
    counts = seg[2, :N_EXPERTS].astype(jnp.int32)
    pstarts = pends - padded
    n_used = pends[-1:] // bm

    xs = _dispatch_rows(dest0, dest1, pstarts + counts, pends, n_used, h_slabs, cap)
    y = _expert_mlp(pstarts // bm, padded // bm, n_used, xs, w1[0], w3[0], w2[0], cap)
    out = _combine(dest0, dest1, h, route, y, norm_final_w.reshape(1, d))
    return out.reshape(b, s, d)
```

```python
import functools
import math

import numpy as np
import jax
import jax.numpy as jnp
from jax import lax
from jax.experimental import pallas as pl
from jax.experimental.pallas import tpu as pltpu

D_MODEL = 2048
HEAD_DIM = 128
FOX_HEADS = 8
RET_HEADS = 8
FOX_WIDTH = FOX_HEADS * HEAD_DIM
RET_WIDTH = RET_HEADS * HEAD_DIM
N_BRANCH = 2
N_GROUPS = 4
EXPERTS_PER_GROUP = 8
N_EXPERTS = N_GROUPS * EXPERTS_PER_GROUP
TOP_K = 2
D_EXPERT = 512
ROPE_BASE = 10000.0
EPS = 1e-6

LANES = 128
NEG_BIG = -1e30
LOG2E = math.log2(math.e)
MAIN_WIDTH = 3 * FOX_WIDTH + 4 * RET_WIDTH + N_BRANCH * D_MODEL
COL_GATE = 0
COL_FQ = N_BRANCH * D_MODEL // HEAD_DIM
COL_FK = COL_FQ + FOX_HEADS
COL_FV = COL_FK + FOX_HEADS
COL_RQ = COL_FV + FOX_HEADS
COL_RK = COL_RQ + RET_HEADS
COL_RV = COL_RK + RET_HEADS
COL_RG = COL_RV + RET_HEADS

ATTN_BLOCK = 256
FOX_HEADS_PER_STEP = 4
RET_HEADS_PER_STEP = 1
RET_CHUNK = 256
MOE_BLOCK = 256
X_BUFS = 5
X_AHEAD = X_BUFS - 2
PAD_FILL_CHUNKS = (64, 8, 1)
VMEM_LIMIT = 56 * 1024 * 1024

f32 = jnp.float32
bf16 = jnp.bfloat16
_NT = (((1,), (1,)), ((), ()))
SLAB = D_MODEL // (2 * LANES)
PITCH = SLAB + 1
u32 = jnp.uint32
_HI_MASK = np.uint32(0xFFFF0000)


def _rows_to_slabs(ref2, val, n_rows):
    bits = lambda v: lax.bitcast_convert_type(v.astype(bf16).astype(f32), u32)
    for j in range(SLAB):
        lo = val[:, (2 * j) * LANES:(2 * j + 1) * LANES]
        hi = val[:, (2 * j + 1) * LANES:(2 * j + 2) * LANES]
        ref2[pl.ds(j, n_rows, stride=PITCH), :] = (bits(lo) >> 16) | (bits(hi) & _HI_MASK)
    ref2[pl.ds(SLAB, n_rows, stride=PITCH), :] = jnp.zeros((n_rows, LANES), u32)


def _slabs_to_rows(ref2, n_rows):
    chunks = []
    for j in range(SLAB):
        w = ref2[pl.ds(j, n_rows, stride=PITCH), :]
        chunks.append(lax.bitcast_convert_type(w << 16, f32))
        chunks.append(lax.bitcast_convert_type(w & _HI_MASK, f32))
    return jnp.concatenate(chunks, axis=-1)


def _slab_gather_start(idx_ref, base, n_rows, src_hbm, dst, sem, unroll=8):
    def body(g, _):
        for u in range(unroll):
            r = g * unroll + u
            t = idx_ref[base + r]
            pltpu.make_async_copy(src_hbm.at[pl.ds(t * PITCH, SLAB), :], dst.at[pl.ds(r * PITCH, SLAB), :],
                                  sem).start(priority=u % 2)
        return 0
    lax.fori_loop(0, n_rows // unroll, body, 0)


def _slab_gather_wait(n_rows, src_hbm, dst, sem):
    pltpu.make_async_copy(src_hbm.at[pl.ds(0, n_rows * SLAB), :], dst.at[pl.ds(0, n_rows * SLAB), :], sem).wait()


def _cparams(sem, vmem=VMEM_LIMIT):
    return pltpu.CompilerParams(dimension_semantics=sem, vmem_limit_bytes=vmem)


def _const_spec(shape):
    return pl.BlockSpec(shape, lambda i: tuple(0 for _ in shape), pipeline_mode=pl.Buffered(1))


def _rope_kernel(pos_ref, freq_ref, sgn_ref, cos_ref, sin_ref):
    pos = pos_ref[0].astype(f32)
    hs = pos.shape[0] // 2
    half = HEAD_DIM // 2
    lane = lax.broadcasted_iota(jnp.int32, (hs, HEAD_DIM), 1)
    low = lane < half
    ang = jnp.where(low, pos[:hs], pos[hs:]) * freq_ref[...]
    c = jnp.cos(ang)
    s = jnp.sin(ang) * sgn_ref[...]
    c_sw = pltpu.roll(c, half, axis=1)
    s_sw = pltpu.roll(s, half, axis=1)
    cos_ref[0, :hs, :] = jnp.where(low, c, c_sw)
    cos_ref[0, hs:, :] = jnp.where(low, c_sw, c)
    sin_ref[0, :hs, :] = jnp.where(low, s, -s_sw)
    sin_ref[0, hs:, :] = jnp.where(low, -s_sw, s)


def _rope_tables(positions):
    b, s = positions.shape
    half = HEAD_DIM // 2
    inv = (np.float32(ROPE_BASE) ** (-np.arange(half, dtype=np.float32) / np.float32(half))).astype(np.float32)
    freq = jnp.asarray(np.concatenate([inv, inv])[None, :])
    sgn = jnp.asarray(np.concatenate([-np.ones(half, np.float32), np.ones(half, np.float32)])[None, :])
    out = jax.ShapeDtypeStruct((b, s, HEAD_DIM), f32)
    return pl.pallas_call(
        _rope_kernel,
        grid=(b,),
        in_specs=[pl.BlockSpec((1, s, 1), lambda i: (i, 0, 0)),
                  pl.BlockSpec((1, HEAD_DIM), lambda i: (0, 0)),
                  pl.BlockSpec((1, HEAD_DIM), lambda i: (0, 0))],
        out_specs=[pl.BlockSpec((1, s, HEAD_DIM), lambda i: (i, 0, 0)),
                   pl.BlockSpec((1, s, HEAD_DIM), lambda i: (i, 0, 0))],
        out_shape=[out, out],
        compiler_params=_cparams(("parallel",)),
        name="rope_tables",
    )(positions.reshape(b, s, 1), freq, sgn)


_Q_FACTORS = ((COL_FQ * HEAD_DIM // 1024, LOG2E / math.sqrt(HEAD_DIM)),
              (COL_RQ * HEAD_DIM // 1024, 1.0 / math.sqrt(HEAD_DIM)))


def _wprep_kernel(off_ref, w_hbm, o_ref, buf, sem, *, tn):
    j = pl.program_id(0)
    n = pl.num_programs(0)
    slot = j % 2

    def copy(jj, s):
        src = w_hbm.at[pl.ds(pl.multiple_of(off_ref[jj], 8), tn), :]
        return pltpu.make_async_copy(src, buf.at[s], sem.at[s])

    @pl.when(j == 0)
    def _():
        copy(0, 0).start()

    @pl.when(j + 1 < n)
    def _():
        copy(j + 1, 1 - slot).start()

    copy(j, slot).wait()
    factor = jnp.float32(1.0)
    for tile, value in _Q_FACTORS:
        factor = jnp.where(j == tile, jnp.float32(value), factor)
    o_ref[...] = (buf[slot] * factor).astype(o_ref.dtype)


def _prep_in_weights(wt, row_offsets):
    tn = 1024
    d = wt.shape[1]
    n_tiles = len(row_offsets)
    grid_spec = pltpu.PrefetchScalarGridSpec(
        num_scalar_prefetch=1,
        grid=(n_tiles,),
        in_specs=[pl.BlockSpec(memory_space=pl.ANY)],
        out_specs=pl.BlockSpec((tn, d), lambda j, off: (j, 0)),
        scratch_shapes=[pltpu.VMEM((2, tn, d), f32), pltpu.SemaphoreType.DMA((2,))],
    )
    return pl.pallas_call(
        functools.partial(_wprep_kernel, tn=tn),
        grid_spec=grid_spec,
        out_shape=jax.ShapeDtypeStruct((n_tiles * tn, d), bf16),
        compiler_params=_cparams(("arbitrary",)),
        name="prep_in_weights",
    )(jnp.asarray(row_offsets, jnp.int32), wt)


_ROT_TILES = (COL_RQ * HEAD_DIM // 1024, COL_RK * HEAD_DIM // 1024)


def _inproj_kernel(x_ref, nw_ref, w_ref, wf_ref, cos_ref, sin_ref, o_ref, f_ref, xn_ref):
    j = pl.program_id(1)

    @pl.when(j == 0)
    def _():
        x = x_ref[...]
        ms = jnp.mean(x * x, axis=-1, keepdims=True)
        xn = (x * lax.rsqrt(ms + EPS) * nw_ref[...]).astype(bf16)
        xn_ref[...] = xn
        f_ref[...] = lax.dot_general(xn, wf_ref[...], _NT, preferred_element_type=f32)

    rotated = (j == _ROT_TILES[0]) | (j == _ROT_TILES[1])

    @pl.when(jnp.logical_not(rotated))
    def _():
        o_ref[...] = lax.dot_general(xn_ref[...], w_ref[...], _NT, preferred_element_type=f32).astype(o_ref.dtype)

    @pl.when(rotated)
    def _():
        acc = lax.dot_general(xn_ref[...], w_ref[...], _NT, preferred_element_type=f32)
        cos = cos_ref[...]
        sin = sin_ref[...]
        for hh in range(acc.shape[1] // HEAD_DIM):
            lanes = slice(hh * HEAD_DIM, (hh + 1) * HEAD_DIM)
            t = acc[:, lanes]
            o_ref[:, lanes] = (t * cos + pltpu.roll(t, HEAD_DIM // 2, axis=1) * sin).astype(o_ref.dtype)


def _in_projection(x2, norm_w, w_main_t, w_f_t, cos, sin):
    t, d = x2.shape
    n = w_main_t.shape[0]
    tm = min(1024, t)
    tn = 1024
    return pl.pallas_call(
        _inproj_kernel,
        grid=(t // tm, n // tn),
        in_specs=[pl.BlockSpec((tm, d), lambda i, j: (i, 0)),
                  pl.BlockSpec((1, d), lambda i, j: (0, 0)),
                  pl.BlockSpec((tn, d), lambda i, j: (j, 0)),
                  pl.BlockSpec((LANES, d), lambda i, j: (0, 0)),
                  pl.BlockSpec((tm, HEAD_DIM), lambda i, j: (i, 0)),
                  pl.BlockSpec((tm, HEAD_DIM), lambda i, j: (i, 0))],
        out_specs=[pl.BlockSpec((tm, tn), lambda i, j: (i, j)),
                   pl.BlockSpec((tm, LANES), lambda i, j: (i, 0))],
        out_shape=[jax.ShapeDtypeStruct((t, n), bf16), jax.ShapeDtypeStruct((t, LANES), f32)],
        scratch_shapes=[pltpu.VMEM((tm, d), bf16)],
        compiler_params=_cparams(("parallel", "arbitrary")),
        name="in_projection",
    )(x2, norm_w, w_main_t, w_f_t, cos, sin)


def _fcum_kernel(f_ref, b_ref, c_ref):
    z = f_ref[...] + b_ref[...]
    ls = jnp.minimum(z, 0.0) - jnp.log1p(jnp.exp(-jnp.abs(z)))
    c = ls.T[:FOX_HEADS, :]
    s = c.shape[1]
    lane = lax.broadcasted_iota(jnp.int32, c.shape, 1)
    sh = 1
    while sh < s:
        c = c + jnp.where(lane >= sh, pltpu.roll(c, sh, axis=1), 0.0)
        sh *= 2
    c_ref[0] = c * LOG2E


def _forget_cumsum(f_logit, bias, b, s):
    return pl.pallas_call(
        _fcum_kernel,
        grid=(b,),
        in_specs=[pl.BlockSpec((s, LANES), lambda i: (i, 0)),
                  pl.BlockSpec((1, LANES), lambda i: (0, 0))],
        out_specs=pl.BlockSpec((1, FOX_HEADS, s), lambda i: (i, 0, 0)),
        out_shape=jax.ShapeDtypeStruct((b, FOX_HEADS, s), f32),
        compiler_params=_cparams(("parallel",)),
        name="forget_cumsum",
    )(f_logit, bias)


def _fox_kernel(q_ref, k_ref, v_ref, c_ref, o_ref, va_ref, acc_ref, *, tq):
    s_len = q_ref.shape[0]
    nq = s_len // tq
    n_heads = q_ref.shape[1] // HEAD_DIM
    row = lax.broadcasted_iota(jnp.int32, (tq, tq), 0)
    col = lax.broadcasted_iota(jnp.int32, (tq, tq), 1)
    causal = row >= col
    def scores(hh):
        lanes = slice(hh * HEAD_DIM, (hh + 1) * HEAD_DIM)
        va_ref[hh, :, :HEAD_DIM] = v_ref[:, lanes]
        va_ref[hh, :, HEAD_DIM:] = jnp.ones((s_len, HEAD_DIM), va_ref.dtype)
        panels = []
        row_max = [None] * nq
        for kb in range(nq):
            r0 = kb * tq
            s = lax.dot_general(q_ref[r0:, lanes], k_ref[r0:r0 + tq, lanes], _NT, preferred_element_type=f32)
            s = s - c_ref[hh, :, r0:r0 + tq]
            parts = [jnp.where(causal, s[:tq], NEG_BIG)]
            parts += [s[(j - kb) * tq:(j - kb + 1) * tq] for j in range(kb + 1, nq)]
            panels.append(parts)
            for j, part in zip(range(kb, nq), parts):
                pm = jnp.max(part, axis=-1, keepdims=True)
                row_max[j] = pm if row_max[j] is None else jnp.maximum(row_max[j], pm)
        return panels, row_max

    def values(hh, panels, row_max):
        lanes = slice(hh * HEAD_DIM, (hh + 1) * HEAD_DIM)
        for kb in range(nq):
            r0 = kb * tq
            p = jnp.concatenate([jnp.exp2(part - row_max[j]).astype(bf16)
                                 for j, part in zip(range(kb, nq), panels[kb])], axis=0)
            contrib = jnp.dot(p, va_ref[hh, r0:r0 + tq, :], preferred_element_type=f32)
            if kb:
                acc_ref[hh, r0:, :] += contrib
            else:
                acc_ref[hh] = contrib
        acc = acc_ref[hh]
        o_ref[:, lanes] = (acc[:, :HEAD_DIM] / acc[:, HEAD_DIM:]).astype(o_ref.dtype)

    pending = None
    for hh in range(n_heads):
        current = (hh,) + scores(hh)
        if pending is not None:
            values(*pending)
        pending = current
    values(*pending)


def _fox_attention(proj, c_blk, b, s):
    tq = min(ATTN_BLOCK, s)
    hp = FOX_HEADS_PER_STEP
    w = hp * HEAD_DIM
    col = lambda c0: pl.BlockSpec((s, w), lambda i, j: (i, c0 // hp + j))
    return pl.pallas_call(
        functools.partial(_fox_kernel, tq=tq),
        grid=(b, FOX_HEADS // hp),
        in_specs=[col(COL_FQ), col(COL_FK), col(COL_FV),
                  pl.BlockSpec((hp, 1, s), lambda i, j: (i * (FOX_HEADS // hp) + j, 0, 0))],
        out_specs=pl.BlockSpec((s, w), lambda i, j: (i, j)),
        out_shape=jax.ShapeDtypeStruct((b * s, FOX_WIDTH), bf16),
        scratch_shapes=[pltpu.VMEM((hp, s, 2 * HEAD_DIM), bf16), pltpu.VMEM((hp, s, 2 * HEAD_DIM), f32)],
        compiler_params=_cparams(("parallel", "parallel")),
        name="fox_attention",
    )(proj, proj, proj, c_blk)


def _ret_kernel(q_ref, k_ref, v_ref, g_ref, dec_ref, xi_ref, zeta_ref, gc_ref, gnw_ref, o_ref, *, chunk):
    s_len = q_ref.shape[0]
    nc = s_len // chunk
    n_heads = q_ref.shape[1] // HEAD_DIM
    for hh in range(n_heads):
        lanes = slice(hh * HEAD_DIM, (hh + 1) * HEAD_DIM)
        state = jnp.zeros((HEAD_DIM, HEAD_DIM), f32)
        for ci in range(nc):
            rows = slice(ci * chunk, (ci + 1) * chunk)
            qb = q_ref[rows, lanes]
            kb = k_ref[rows, lanes]
            v = v_ref[rows, lanes]
            inner = lax.dot_general(qb, kb, _NT, preferred_element_type=f32) * dec_ref[hh]
            o = jnp.dot(inner.astype(bf16), v, preferred_element_type=f32)
            if ci:
                o = o + jnp.dot(qb, state.astype(bf16), preferred_element_type=f32) * xi_ref[hh]
            if ci + 1 < nc:
                kz = (kb.astype(f32) * zeta_ref[hh]).astype(bf16)
                kv = lax.dot_general(kz, v, (((0,), (0,)), ((), ())), preferred_element_type=f32)
                state = state * gc_ref[hh] + kv if ci else kv
            mu = jnp.mean(o, axis=-1, keepdims=True)
            oc = o - mu
            var = jnp.mean(oc * oc, axis=-1, keepdims=True)
            y = oc * lax.rsqrt(var + EPS) * gnw_ref[:, lanes]
            g = g_ref[rows, lanes].astype(f32)
            o_ref[rows, lanes] = (g * jax.nn.sigmoid(g) * y).astype(o_ref.dtype)


def _retention_constants(chunk):
    hh = np.arange(RET_HEADS, dtype=np.float64)
    log_gamma = np.log1p(-(2.0 ** (-5.0 - hh)))
    n = np.arange(chunk, dtype=np.float64)
    diff = n[:, None] - n[None, :]
    decay = np.where(diff[None] >= 0, np.exp(diff[None] * log_gamma[:, None, None]), 0.0)
    xi = np.exp((n[None, :] + 1.0) * log_gamma[:, None])
    zeta = np.exp((chunk - 1.0 - n[None, :]) * log_gamma[:, None])
    g_chunk = np.exp(chunk * log_gamma)
    bc = lambda a: np.broadcast_to(a[:, :, None], (RET_HEADS, chunk, HEAD_DIM))
    gc = np.broadcast_to(g_chunk[:, None, None], (RET_HEADS, 1, HEAD_DIM))
    to = lambda a: jnp.asarray(np.ascontiguousarray(a), dtype=f32)
    return to(decay), to(bc(xi)), to(bc(zeta)), to(gc)


def _retention(proj, gn_w, b, s):
    chunk = min(RET_CHUNK, s)
    decay, xi, zeta, gc = _retention_constants(chunk)
    hp = RET_HEADS_PER_STEP
    w = hp * HEAD_DIM
    head = lambda col: pl.BlockSpec((s, w), lambda i, j: (i, col // hp + j))
    per_head = lambda shape: pl.BlockSpec((hp,) + shape, lambda i, j: (j, 0, 0))
    return pl.pallas_call(
        functools.partial(_ret_kernel, chunk=chunk),
        grid=(b, RET_HEADS // hp),
        in_specs=[head(COL_RQ), head(COL_RK), head(COL_RV), head(COL_RG),
                  per_head((chunk, chunk)), per_head((chunk, HEAD_DIM)), per_head((chunk, HEAD_DIM)),
                  per_head((1, HEAD_DIM)),
                  pl.BlockSpec((1, w), lambda i, j: (0, j))],
        out_specs=pl.BlockSpec((s, w), lambda i, j: (i, j)),
        out_shape=jax.ShapeDtypeStruct((b * s, RET_WIDTH), bf16),
        compiler_params=_cparams(("parallel", "parallel")),
        name="retention",
    )(proj, proj, proj, proj, decay, xi, zeta, gc, gn_w)


def _branch_merge_kernel(fo_ref, ro_ref, g0_ref, g1_ref, bg_ref, wb_ref, mg_ref, *, ncol):
    d = mg_ref.shape[1]
    cw = d // ncol
    fo = fo_ref[...]
    ro = ro_ref[...]
    for c in range(ncol):
        cs = slice(c * cw, (c + 1) * cw)
        bd0 = jnp.dot(fo, wb_ref[0, :, cs], preferred_element_type=f32)
        bd1 = jnp.dot(ro, wb_ref[1, :, cs], preferred_element_type=f32)
        ga = jax.nn.sigmoid(g0_ref[:, cs].astype(f32) + bg_ref[0:1, cs])
        gb = jax.nn.sigmoid(g1_ref[:, cs].astype(f32) + bg_ref[1:2, cs])
        mg_ref[:, cs] = (ga * bd0 + gb * bd1).astype(mg_ref.dtype)


def _branch_merge(fox_o, ret_o, proj, b_gate, w_branch):
    t = fox_o.shape[0]
    d = D_MODEL
    tm = min(1024, t)
    gate0 = COL_GATE * HEAD_DIM // d
    return pl.pallas_call(
        functools.partial(_branch_merge_kernel, ncol=4),
        grid=(t // tm,),
        in_specs=[pl.BlockSpec((tm, FOX_WIDTH), lambda i: (i, 0)),
                  pl.BlockSpec((tm, RET_WIDTH), lambda i: (i, 0)),
                  pl.BlockSpec((tm, d), lambda i: (i, gate0)),
                  pl.BlockSpec((tm, d), lambda i: (i, gate0 + 1)),
                  _const_spec((N_BRANCH, d)),
                  _const_spec((N_BRANCH, FOX_WIDTH, d))],
        out_specs=pl.BlockSpec((tm, d), lambda i: (i, 0)),
        out_shape=jax.ShapeDtypeStruct((t, d), bf16),
        compiler_params=_cparams(("parallel",)),
        name="branch_merge",
    )(fox_o, ret_o, proj, proj, b_gate, w_branch)


def _route(logits):
    lane = lax.broadcasted_iota(jnp.int32, logits.shape, 1).astype(f32)
    far = jnp.float32(1e9)
    gmask = lane < N_GROUPS
    gl = jnp.where(gmask, logits, NEG_BIG)
    gmax = jnp.max(gl, axis=-1, keepdims=True)
    gsel = jnp.min(jnp.where(gl == gmax, lane, far), axis=-1, keepdims=True)
    p_group = 1.0 / jnp.sum(jnp.where(gmask, jnp.exp(gl - gmax), 0.0), axis=-1, keepdims=True)
    lo = N_GROUPS + EXPERTS_PER_GROUP * gsel
    emask = (lane >= lo) & (lane < lo + EXPERTS_PER_GROUP)
    el = jnp.where(emask, logits, NEG_BIG)
    m1 = jnp.max(el, axis=-1, keepdims=True)
    i1 = jnp.min(jnp.where(el == m1, lane, far), axis=-1, keepdims=True)
    el2 = jnp.where(lane == i1, NEG_BIG, el)
    m2 = jnp.max(el2, axis=-1, keepdims=True)
    i2 = jnp.min(jnp.where(el2 == m2, lane, far), axis=-1, keepdims=True)
    t = jnp.exp(m2 - m1)
    w1 = p_group / (1.0 + t)
    w2 = p_group * t / (1.0 + t)
    out = jnp.where(lane == 0, i1 - N_GROUPS, 0.0)
    out = jnp.where(lane == 1, i2 - N_GROUPS, out)
    out = jnp.where(lane == 2, w1, out)
    out = jnp.where(lane == 3, w2, out)
    return out


def _outproj_kernel(mg_ref, x_ref, wo_ref, nfw_ref, wr_ref, br_ref, h_ref, hs_ref, route_ref, cnt_ref):
    h = x_ref[...] + jnp.dot(mg_ref[...], wo_ref[...], preferred_element_type=f32)
    h_ref[...] = h
    hn = h * lax.rsqrt(jnp.mean(h * h, axis=-1, keepdims=True) + EPS) * nfw_ref[...]
    _rows_to_slabs(hs_ref, hn, hn.shape[0])
    hi = hn.astype(bf16)
    lo = (hn - hi.astype(f32)).astype(bf16)
    hw = jnp.dot(hi, wr_ref[...], preferred_element_type=f32)
    logits = (hw[:, :LANES] + jnp.dot(lo, wr_ref[:, :LANES], preferred_element_type=f32) + hw[:, LANES:]) + br_ref[...]
    route = _route(logits)
    route_ref[...] = route

    @pl.when(pl.program_id(0) == 0)
    def _():
        cnt_ref[...] = jnp.zeros_like(cnt_ref)

    lane = lax.broadcasted_iota(jnp.int32, route.shape, 1).astype(f32)
    chosen = (lane == route[:, 0:1]).astype(f32) + (lane == route[:, 1:2]).astype(f32)
    cnt_ref[...] += jnp.sum(chosen, axis=0, keepdims=True)


def _outproj_router(merged, x2, w_out, nfw, wr_cat, b_router):
    t, d = x2.shape
    tm = min(512, t)
    return pl.pallas_call(
        _outproj_kernel,
        grid=(t // tm,),
        in_specs=[pl.BlockSpec((tm, d), lambda i: (i, 0)),
                  pl.BlockSpec((tm, d), lambda i: (i, 0)),
                  _const_spec((d, d)),
                  _const_spec((1, d)),
                  _const_spec((d, 2 * LANES)), _const_spec((1, LANES))],
        out_specs=[pl.BlockSpec((tm, d), lambda i: (i, 0)),
                   pl.BlockSpec((tm * PITCH, LANES), lambda i: (i, 0)),
                   pl.BlockSpec((tm, LANES), lambda i: (i, 0)),
                   pl.BlockSpec((8, LANES), lambda i: (0, 0))],
        out_shape=[jax.ShapeDtypeStruct((t, d), f32), jax.ShapeDtypeStruct((t * PITCH, LANES), u32),
                   jax.ShapeDtypeStruct((t, LANES), f32), jax.ShapeDtypeStruct((8, LANES), f32)],
        compiler_params=_cparams(("arbitrary",)),
        name="outproj_router",
    )(merged, x2, w_out, nfw, wr_cat, b_router)


def _plan_kernel(route_ref, cnt_ref, dest_ref, ends_ref, carry_ref, *, bm):
    r = route_ref[...]
    tb = r.shape[0]
    lane = lax.broadcasted_iota(jnp.int32, r.shape, 1).astype(f32)
    o0 = (lane == r[:, 0:1]).astype(f32)
    o1 = (lane == r[:, 1:2]).astype(f32)
    o = o0 + o1

    @pl.when(pl.program_id(0) == 0)
    def _():
        cnt = cnt_ref[...]
        padded = jnp.floor((cnt + (bm - 1)) * (1.0 / bm)) * bm
        lane8 = lax.broadcasted_iota(jnp.int32, cnt.shape, 1)
        ends = padded
        sh = 1
        while sh < LANES:
            ends = ends + jnp.where(lane8 >= sh, pltpu.roll(ends, sh, axis=1), 0.0)
            sh *= 2
        sub = lax.broadcasted_iota(jnp.int32, cnt.shape, 0)
        ends_ref[...] = jnp.where(sub == 0, ends, jnp.where(sub == 1, padded, cnt))
        carry_ref[...] = ends - padded

    row = lax.broadcasted_iota(jnp.int32, (tb, tb), 0)
    col = lax.broadcasted_iota(jnp.int32, (tb, tb), 1)
    lower = jnp.where(col < row, 1.0, 0.0).astype(bf16)
    before = jnp.dot(lower, o.astype(bf16), preferred_element_type=f32) + carry_ref[0:1, :]
    d0 = jnp.sum(before * o0, axis=-1, keepdims=True)
    d1 = jnp.sum(before * o1, axis=-1, keepdims=True)
    dd = jnp.where(lane == 0, d0, jnp.where(lane == 1, d1, 0.0))
    dest_ref[0] = dd.T[:8, :].astype(jnp.int32)
    carry_ref[...] += jnp.sum(o, axis=0, keepdims=True)


def _dispatch_plan(route, counts, bm):
    t = route.shape[0]
    tb = min(1024, t)
    nb = t // tb
    return pl.pallas_call(
        functools.partial(_plan_kernel, bm=bm),
        grid=(nb,),
        in_specs=[pl.BlockSpec((tb, LANES), lambda i: (i, 0)),
                  pl.BlockSpec((8, LANES), lambda i: (0, 0))],
        out_specs=[pl.BlockSpec((1, 8, tb), lambda i: (i, 0, 0)),
                   pl.BlockSpec((8, LANES), lambda i: (0, 0))],
        out_shape=[jax.ShapeDtypeStruct((nb, 8, tb), jnp.int32), jax.ShapeDtypeStruct((8, LANES), f32)],
        scratch_shapes=[pltpu.VMEM((8, LANES), f32)],
        compiler_params=_cparams(("arbitrary",)),
        name="dispatch_plan",
    )(route, counts)


def _dispatch_kernel(d0_ref, d1_ref, fill_ref, end_ref, nu_ref, hs_ref, xs_hbm, zbuf, sem, zsem,
                     *, tb, bm, n_blk, unroll=8):
    i = pl.program_id(0)
    base = i * tb
    blk_rows = bm * PITCH

    def pad_copy(s, n):
        return pltpu.make_async_copy(zbuf.at[pl.ds(0, n * PITCH), :], xs_hbm.at[pl.ds(s * PITCH, n * PITCH), :], zsem)

    def tail_copy(g):
        return pltpu.make_async_copy(zbuf, xs_hbm.at[pl.ds(pl.multiple_of(g * blk_rows, 8), blk_rows), :], zsem)

    @pl.when(i == 0)
    def _():
        zbuf[...] = jnp.zeros(zbuf.shape, u32)
        for op in ("start", "wait"):
            def per_expert(e, c, op=op):
                s = fill_ref[e]
                for n in PAD_FILL_CHUNKS:
                    count = (end_ref[e] - s) // n

                    def one(k, c2, s=s, n=n):
                        getattr(pad_copy(s + k * n, n), op)()
                        return c2

                    c = lax.fori_loop(0, count, one, c)
                    s = s + count * n
                return c

            def per_tail(g, c, op=op):
                getattr(tail_copy(g), op)()
                return c

            lax.fori_loop(0, N_EXPERTS, per_expert, 0)
            lax.fori_loop(nu_ref[0], n_blk, per_tail, 0)

    def body(g, c):
        for u in range(unroll):
            r = g * unroll + u
            src = hs_ref.at[pl.ds(r * PITCH, PITCH), :]
            for k, dref in enumerate((d0_ref, d1_ref)):
                d = dref[base + r]
                pltpu.make_async_copy(src, xs_hbm.at[pl.ds(d * PITCH, PITCH), :], sem).start(priority=k)
        return c

    lax.fori_loop(0, tb // unroll, body, 0)
    for _ in range(TOP_K):
        pltpu.make_async_copy(xs_hbm.at[pl.ds(0, tb * PITCH), :], xs_hbm.at[pl.ds(0, tb * PITCH), :], sem).wait()


def _dispatch_rows(dest0, dest1, fill_start, seg_end, n_used, hs, cap):
    t = dest0.shape[0]
    tb = min(1024, t)
    bm = MOE_BLOCK
    grid_spec = pltpu.PrefetchScalarGridSpec(
        num_scalar_prefetch=5,
        grid=(t // tb,),
        in_specs=[pl.BlockSpec((tb * PITCH, LANES), lambda i, *_: (i, 0))],
        out_specs=pl.BlockSpec(memory_space=pl.ANY),
        scratch_shapes=[pltpu.VMEM((bm * PITCH, LANES), u32), pltpu.SemaphoreType.DMA(()),
                        pltpu.SemaphoreType.DMA(())],
    )
    return pl.pallas_call(
        functools.partial(_dispatch_kernel, tb=tb, bm=bm, n_blk=cap // bm),
        grid_spec=grid_spec,
        out_shape=jax.ShapeDtypeStruct((cap * PITCH, LANES), u32),
        compiler_params=_cparams(("arbitrary",)),
        name="dispatch_rows",
    )(dest0, dest1, fill_start, seg_end, n_used, hs)


def _moe_kernel(g0_ref, nb_ref, nu_ref, x_hbm, w1_ref, w3_ref, w2_ref, y_hbm,
                xbuf, ybuf, w1b, w3b, w2b, gsem, ysem, *, bm, n_blk):
    e = pl.program_id(0)
    n_used = nu_ref[0]
    g0 = g0_ref[e]
    nb = nb_ref[e]
    blk_rows = bm * PITCH

    def x_copy(slot, g):
        src = x_hbm.at[pl.ds(pl.multiple_of(g * blk_rows, 8), blk_rows), :]
        return pltpu.make_async_copy(src, xbuf.at[slot], gsem.at[slot])

    def y_copy(slot, g):
        dst = y_hbm.at[pl.ds(pl.multiple_of(g * blk_rows, 8), blk_rows), :]
        return pltpu.make_async_copy(ybuf.at[slot], dst, ysem.at[slot])

    @pl.when(e == 0)
    def _():
        x_copy(0, 0).start()
        for k in range(1, X_AHEAD):
            pl.when(n_used > k)(lambda k=k: x_copy(k, k).start())

    @pl.when(nb > 0)
    def _():
        w1b[...] = w1_ref[0].astype(bf16)
        w3b[...] = w3_ref[0].astype(bf16)
        w2b[...] = w2_ref[0].astype(bf16)

    def run_blocks(g_first, count):
        gs = [g_first + k for k in range(count)]
        for g in gs:
            ahead = g + X_AHEAD
            pl.when(ahead < n_used)(lambda ahead=ahead: x_copy(ahead % X_BUFS, ahead).start())
        for g in gs:
            x_copy(g % X_BUFS, g).wait()
        hns = [_slabs_to_rows(xbuf.at[g % X_BUFS], bm).astype(bf16) for g in gs]
        gates = [(jnp.dot(hn, w1b[...], preferred_element_type=f32),
                  jnp.dot(hn, w3b[...], preferred_element_type=f32)) for hn in hns]
        ys = [jnp.dot((a * jax.nn.sigmoid(a) * b).astype(bf16), w2b[...], preferred_element_type=f32)
              for a, b in gates]
        for g in gs:
            pl.when(g >= 2)(lambda g=g: y_copy(g % 2, g - 2).wait())
        for g, y in zip(gs, ys):
            _rows_to_slabs(ybuf.at[g % 2], y, bm)
            y_copy(g % 2, g).start()

    def pair(c, carry):
        run_blocks(g0 + 2 * c, 2)
        return carry

    lax.fori_loop(0, nb // 2, pair, 0)

    @pl.when(nb % 2 == 1)
    def _():
        run_blocks(g0 + nb - 1, 1)

    @pl.when(e == pl.num_programs(0) - 1)
    def _():
        @pl.when(n_used >= 2)
        def _():
            y_copy(n_used % 2, n_used - 2).wait()

        y_copy((n_used - 1) % 2, n_used - 1).wait()
        ybuf[0] = jnp.zeros(ybuf.shape[1:], u32)

        def zstart(g, carry):
            y_copy(0, g).start()
            return carry

        def zwait(g, carry):
            y_copy(0, g).wait()
            return carry

        lax.fori_loop(n_used, n_blk, zstart, 0)
        lax.fori_loop(n_used, n_blk, zwait, 0)


def _expert_mlp(blk_start, blk_count, n_used, xs, w1, w3, w2, cap):
    d = D_MODEL
    bm = MOE_BLOCK
    n_blk = cap // bm
    by_expert = lambda shape: pl.BlockSpec(shape, lambda e, g0, nb, nu: (e, 0, 0))
    grid_spec = pltpu.PrefetchScalarGridSpec(
        num_scalar_prefetch=3,
        grid=(N_EXPERTS,),
        in_specs=[pl.BlockSpec(memory_space=pl.ANY),
                  by_expert((1, d, D_EXPERT)), by_expert((1, d, D_EXPERT)), by_expert((1, D_EXPERT, d))],
        out_specs=pl.BlockSpec(memory_space=pl.ANY),
        scratch_shapes=[pltpu.VMEM((X_BUFS, bm * PITCH, LANES), u32), pltpu.VMEM((2, bm * PITCH, LANES), u32),
                        pltpu.VMEM((d, D_EXPERT), bf16), pltpu.VMEM((d, D_EXPERT), bf16),
                        pltpu.VMEM((D_EXPERT, d), bf16),
                        pltpu.SemaphoreType.DMA((X_BUFS,)), pltpu.SemaphoreType.DMA((2,))],
    )
    return pl.pallas_call(
        functools.partial(_moe_kernel, bm=bm, n_blk=n_blk),
        grid_spec=grid_spec,
        out_shape=jax.ShapeDtypeStruct((cap * PITCH, LANES), u32),
        compiler_params=_cparams(("arbitrary",)),
        name="expert_mlp",
    )(blk_start, blk_count, n_used, xs, w1, w3, w2)


def _combine_kernel(d0_ref, d1_ref, h_ref, route_ref, y_hbm, nw_ref, o_ref, ybuf, sem, *, tb):
    i = pl.program_id(0)
    n = pl.num_programs(0)
    slot = i % 2

    def start(blk, s):
        _slab_gather_start(d0_ref, blk * tb, tb, y_hbm, ybuf.at[s, 0], sem.at[s])
        _slab_gather_start(d1_ref, blk * tb, tb, y_hbm, ybuf.at[s, 1], sem.at[s])

    @pl.when(i == 0)
    def _():
        start(0, 0)

    @pl.when(i + 1 < n)
    def _():
        start(i + 1, 1 - slot)

    for k in range(TOP_K):
        _slab_gather_wait(tb, y_hbm, ybuf.at[slot, k], sem.at[slot])
    r = route_ref[...]
    h = h_ref[...] + (r[:, 2:3] * _slabs_to_rows(ybuf.at[slot, 0], tb)
                      + r[:, 3:4] * _slabs_to_rows(ybuf.at[slot, 1], tb))
    o_ref[...] = h * lax.rsqrt(jnp.mean(h * h, axis=-1, keepdims=True) + EPS) * nw_ref[...]


def _combine(dest0, dest1, h, route, ys, norm_w):
    t, d = h.shape
    tb = min(256, t)
    grid_spec = pltpu.PrefetchScalarGridSpec(
        num_scalar_prefetch=2,
        grid=(t // tb,),
        in_specs=[pl.BlockSpec((tb, d), lambda i, a, b: (i, 0)),
                  pl.BlockSpec((tb, LANES), lambda i, a, b: (i, 0)),
                  pl.BlockSpec(memory_space=pl.ANY),
                  pl.BlockSpec((1, d), lambda i, a, b: (0, 0))],
        out_specs=pl.BlockSpec((tb, d), lambda i, a, b: (i, 0)),
        scratch_shapes=[pltpu.VMEM((2, TOP_K, tb * PITCH, LANES), u32), pltpu.SemaphoreType.DMA((2,))],
    )
    return pl.pallas_call(
        functools.partial(_combine_kernel, tb=tb),
        grid_spec=grid_spec,
        out_shape=jax.ShapeDtypeStruct((t, d), f32),
        compiler_params=_cparams(("arbitrary",)),
        name="combine_final_norm",
    )(dest0, dest1, h, route, ys, norm_w)


def kernel(x, positions, norm_mix_w, w_in, fox_b_f, ret_gn_w, w_branch, b_gate, w_out, norm_ffn_w, w_router_group,
           b_router_group, w_router_expert, b_router_expert, w1, w3, w2, norm_final_w):
    b, s, d = x.shape
    assert d == D_MODEL and norm_mix_w.shape[0] == 1
    t = b * s
    x2 = x.reshape(t, d)

    wt = w_in[0].T
    o_f = 3 * FOX_WIDTH
    o_r = o_f + FOX_HEADS
    o_g = o_r + 4 * RET_WIDTH
    tile_rows = ([o_g + k * 1024 for k in range(N_BRANCH * D_MODEL // 1024)]
                 + [k * 1024 for k in range(o_f // 1024)]
                 + [o_r + k * 1024 for k in range(4 * RET_WIDTH // 1024)])
    w_main = _prep_in_weights(wt, tile_rows)
    w_f = jnp.pad(wt[o_f:o_r], ((0, LANES - FOX_HEADS), (0, 0))).astype(bf16)
    bias_f = jnp.pad(fox_b_f[0], (0, LANES - FOX_HEADS)).reshape(1, LANES)
    w_r = jnp.concatenate([w_router_group[0], w_router_expert[0]], axis=1)
    w_r = jnp.pad(w_r, ((0, 0), (0, LANES - w_r.shape[1])))
    wr_hi = w_r.astype(bf16)
    wr_lo = (w_r - wr_hi.astype(f32)).astype(bf16)
    b_r = jnp.pad(jnp.concatenate([b_router_group[0], b_router_expert[0]]), (0, LANES - N_GROUPS - N_EXPERTS))
    b_r = b_r.reshape(1, LANES)

    cos, sin = _rope_tables(positions)
    proj, f_logit = _in_projection(x2, norm_mix_w, w_main, w_f, cos.reshape(t, HEAD_DIM), sin.reshape(t, HEAD_DIM))
    c = _forget_cumsum(f_logit, bias_f, b, s)
    fox_o = _fox_attention(proj, c.reshape(b * FOX_HEADS, 1, s), b, s)
    ret_o = _retention(proj, ret_gn_w, b, s)
    merged = _branch_merge(fox_o, ret_o, proj, b_gate[0], w_branch[0].astype(bf16))
    h, h_slabs, route, counts = _outproj_router(merged, x2, w_out[0].astype(bf16), norm_ffn_w,
                                        jnp.concatenate([wr_hi, wr_lo], axis=1), b_r)

    bm = MOE_BLOCK
    cap = t * TOP_K + N_EXPERTS * bm
    dest_rows, seg = _dispatch_plan(route, counts, bm)
    dest0 = dest_rows[:, 0, :].reshape(t)
    dest1 = dest_rows[:, 1, :].reshape(t)
    pends = seg[0, :N_EXPERTS].astype(jnp.int32)
    padded = seg[1, :N_EXPERTS].astype(jnp.int32)
    counts = seg[2, :N_EXPERTS].astype(jnp.int32)
    pstarts = pends - padded
    n_used = pends[-1:] // bm

    xs = _dispatch_rows(dest0, dest1, pstarts + counts, pends, n_used, h_slabs, cap)
    y = _expert_mlp(pstarts // bm, padded // bm, n_used, xs, w1[0], w3[0], w2[0], cap)
    out = _combine(dest0, dest1, h, route, y, norm_final_w.reshape(1, d))
    return out.reshape(b, s, d)
```

```python
import functools
import math

import numpy as np
import jax
import jax.numpy as jnp
from jax import lax
from jax.experimental import pallas as pl
from jax.experimental.pallas import tpu as pltpu

D_MODEL = 2048
HEAD_DIM = 128
FOX_HEADS = 8
RET_HEADS = 8
FOX_WIDTH = FOX_HEADS * HEAD_DIM
RET_WIDTH = RET_HEADS * HEAD_DIM
N_BRANCH = 2
N_GROUPS = 4
EXPERTS_PER_GROUP = 8
N_EXPERTS = N_GROUPS * EXPERTS_PER_GROUP
TOP_K = 2
D_EXPERT = 512
ROPE_BASE = 10000.0
EPS = 1e-6

LANES = 128
NEG_BIG = -1e30
LOG2E = math.log2(math.e)
COL_GATE = 0
COL_FQ = N_BRANCH * D_MODEL // HEAD_DIM
COL_FK = COL_FQ + FOX_HEADS
COL_FV = COL_FK + FOX_HEADS
COL_RQ = COL_FV + FOX_HEADS
COL_RK = COL_RQ + RET_HEADS
COL_RV = COL_RK + RET_HEADS
COL_RG = COL_RV + RET_HEADS

ATTN_BLOCK = 256
FOX_HEADS_PER_STEP = 4
RET_HEADS_PER_STEP = 1
RET_CHUNK = 256
MOE_BLOCK = 256
X_BUFS = 5
PAD_FILL_CHUNKS = (64, 8, 1)
PROJ_ROWS, PROJ_COLS = 1024, 1024
MERGE_ROWS = 1024
OUTPROJ_ROWS = 512
PLAN_ROWS = 1024
DISPATCH_ROWS = 2048
COMBINE_ROWS = 256
VMEM_LIMIT = 56 * 1024 * 1024

f32 = jnp.float32
bf16 = jnp.bfloat16
_NT = (((1,), (1,)), ((), ()))
SLAB = D_MODEL // (2 * LANES)
PITCH = SLAB + 1
u32 = jnp.uint32
_HI_MASK = np.uint32(0xFFFF0000)


def _rows_to_slabs(ref2, val, n_rows):
    bits = lambda v: lax.bitcast_convert_type(v.astype(bf16).astype(f32), u32)
    for j in range(SLAB):
        lo = val[:, (2 * j) * LANES:(2 * j + 1) * LANES]
        hi = val[:, (2 * j + 1) * LANES:(2 * j + 2) * LANES]
        ref2[pl.ds(j, n_rows, stride=PITCH), :] = (bits(lo) >> 16) | (bits(hi) & _HI_MASK)
    ref2[pl.ds(SLAB, n_rows, stride=PITCH), :] = jnp.zeros((n_rows, LANES), u32)


def _slabs_to_rows(ref2, n_rows):
    chunks = []
    for j in range(SLAB):
        w = ref2[pl.ds(j, n_rows, stride=PITCH), :]
        chunks.append(lax.bitcast_convert_type(w << 16, f32))
        chunks.append(lax.bitcast_convert_type(w & _HI_MASK, f32))
    return jnp.concatenate(chunks, axis=-1)


def _slab_gather_start(idx_ref, base, n_rows, src_hbm, dst, sem, unroll=8):
    def body(g, _):
        for u in range(unroll):
            r = g * unroll + u
            t = idx_ref[base + r]
            pltpu.make_async_copy(src_hbm.at[pl.ds(t * PITCH, SLAB), :], dst.at[pl.ds(r * PITCH, SLAB), :],
                                  sem).start(priority=u % 2)
        return 0
    lax.fori_loop(0, n_rows // unroll, body, 0)


def _slab_gather_wait(n_rows, src_hbm, dst, sem):
    pltpu.make_async_copy(src_hbm.at[pl.ds(0, n_rows * SLAB), :], dst.at[pl.ds(0, n_rows * SLAB), :], sem).wait()


def _cparams(sem, vmem=VMEM_LIMIT):
    return pltpu.CompilerParams(dimension_semantics=sem, vmem_limit_bytes=vmem)


def _const_spec(shape):
    return pl.BlockSpec(shape, lambda i: tuple(0 for _ in shape), pipeline_mode=pl.Buffered(1))


def _rope_kernel(pos_ref, freq_ref, sgn_ref, cos_ref, sin_ref):
    pos = pos_ref[0].astype(f32)
    hs = pos.shape[0] // 2
    half = HEAD_DIM // 2
    lane = lax.broadcasted_iota(jnp.int32, (hs, HEAD_DIM), 1)
    low = lane < half
    ang = jnp.where(low, pos[:hs], pos[hs:]) * freq_ref[...]
    c = jnp.cos(ang)
    s = jnp.sin(ang) * sgn_ref[...]
    c_sw = pltpu.roll(c, half, axis=1)
    s_sw = pltpu.roll(s, half, axis=1)
    cos_ref[0, :hs, :] = jnp.where(low, c, c_sw)
    cos_ref[0, hs:, :] = jnp.where(low, c_sw, c)
    sin_ref[0, :hs, :] = jnp.where(low, s, -s_sw)
    sin_ref[0, hs:, :] = jnp.where(low, -s_sw, s)


def _rope_tables(positions):
    b, s = positions.shape
    half = HEAD_DIM // 2
    inv = (np.float32(ROPE_BASE) ** (-np.arange(half, dtype=np.float32) / np.float32(half))).astype(np.float32)
    freq = jnp.asarray(np.concatenate([inv, inv])[None, :])
    sgn = jnp.asarray(np.concatenate([-np.ones(half, np.float32), np.ones(half, np.float32)])[None, :])
    out = jax.ShapeDtypeStruct((b, s, HEAD_DIM), f32)
    return pl.pallas_call(
        _rope_kernel,
        grid=(b,),
        in_specs=[pl.BlockSpec((1, s, 1), lambda i: (i, 0, 0)),
                  pl.BlockSpec((1, HEAD_DIM), lambda i: (0, 0)),
                  pl.BlockSpec((1, HEAD_DIM), lambda i: (0, 0))],
        out_specs=[pl.BlockSpec((1, s, HEAD_DIM), lambda i: (i, 0, 0)),
                   pl.BlockSpec((1, s, HEAD_DIM), lambda i: (i, 0, 0))],
        out_shape=[out, out],
        compiler_params=_cparams(("parallel",)),
        name="rope_tables",
    )(positions.reshape(b, s, 1), freq, sgn)


_Q_FACTORS = ((COL_FQ * HEAD_DIM // PROJ_COLS, LOG2E / math.sqrt(HEAD_DIM)),
              (COL_RQ * HEAD_DIM // PROJ_COLS, 1.0 / math.sqrt(HEAD_DIM)))


def _wprep_kernel(off_ref, w_hbm, o_ref, buf, sem, *, tn):
    j = pl.program_id(0)
    n = pl.num_programs(0)
    slot = j % 2

    def copy(jj, s):
        src = w_hbm.at[pl.ds(pl.multiple_of(off_ref[jj], 8), tn), :]
        return pltpu.make_async_copy(src, buf.at[s], sem.at[s])

    @pl.when(j == 0)
    def _():
        copy(0, 0).start()

    @pl.when(j + 1 < n)
    def _():
        copy(j + 1, 1 - slot).start()

    copy(j, slot).wait()
    factor = jnp.float32(1.0)
    for tile, value in _Q_FACTORS:
        factor = jnp.where(j == tile, jnp.float32(value), factor)
    o_ref[...] = (buf[slot] * factor).astype(o_ref.dtype)


def _prep_in_weights(wt, row_offsets):
    tn = PROJ_COLS
    d = wt.shape[1]
    n_tiles = len(row_offsets)
    grid_spec = pltpu.PrefetchScalarGridSpec(
        num_scalar_prefetch=1,
        grid=(n_tiles,),
        in_specs=[pl.BlockSpec(memory_space=pl.ANY)],
        out_specs=pl.BlockSpec((tn, d), lambda j, off: (j, 0)),
        scratch_shapes=[pltpu.VMEM((2, tn, d), f32), pltpu.SemaphoreType.DMA((2,))],
    )
    return pl.pallas_call(
        functools.partial(_wprep_kernel, tn=tn),
        grid_spec=grid_spec,
        out_shape=jax.ShapeDtypeStruct((n_tiles * tn, d), bf16),
        compiler_params=_cparams(("arbitrary",)),
        name="prep_in_weights",
    )(jnp.asarray(row_offsets, jnp.int32), wt)


_ROT_TILES = (COL_RQ * HEAD_DIM // PROJ_COLS, COL_RK * HEAD_DIM // PROJ_COLS)


def _inproj_kernel(x_ref, nw_ref, w_ref, wf_ref, cos_ref, sin_ref, o_ref, f_ref, xn_ref):
    j = pl.program_id(1)

    @pl.when(j == 0)
    def _():
        x = x_ref[...]
        ms = jnp.mean(x * x, axis=-1, keepdims=True)
        xn = (x * lax.rsqrt(ms + EPS) * nw_ref[...]).astype(bf16)
        xn_ref[...] = xn
        f_ref[...] = lax.dot_general(xn, wf_ref[...], _NT, preferred_element_type=f32)

    rotated = (j == _ROT_TILES[0]) | (j == _ROT_TILES[1])

    @pl.when(jnp.logical_not(rotated))
    def _():
        o_ref[...] = lax.dot_general(xn_ref[...], w_ref[...], _NT, preferred_element_type=f32).astype(o_ref.dtype)

    @pl.when(rotated)
    def _():
        acc = lax.dot_general(xn_ref[...], w_ref[...], _NT, preferred_element_type=f32)
        cos = cos_ref[...]
        sin = sin_ref[...]
        for hh in range(acc.shape[1] // HEAD_DIM):
            lanes = slice(hh * HEAD_DIM, (hh + 1) * HEAD_DIM)
            t = acc[:, lanes]
            o_ref[:, lanes] = (t * cos + pltpu.roll(t, HEAD_DIM // 2, axis=1) * sin).astype(o_ref.dtype)


def _in_projection(x2, norm_w, w_main_t, w_f_t, cos, sin):
    t, d = x2.shape
    n = w_main_t.shape[0]
    tm = min(PROJ_ROWS, t)
    tn = PROJ_COLS
    return pl.pallas_call(
        _inproj_kernel,
        grid=(t // tm, n // tn),
        in_specs=[pl.BlockSpec((tm, d), lambda i, j: (i, 0)),
                  pl.BlockSpec((1, d), lambda i, j: (0, 0)),
                  pl.BlockSpec((tn, d), lambda i, j: (j, 0)),
                  pl.BlockSpec((LANES, d), lambda i, j: (0, 0)),
                  pl.BlockSpec((tm, HEAD_DIM), lambda i, j: (i, 0)),
                  pl.BlockSpec((tm, HEAD_DIM), lambda i, j: (i, 0))],
        out_specs=[pl.BlockSpec((tm, tn), lambda i, j: (i, j)),
                   pl.BlockSpec((tm, LANES), lambda i, j: (i, 0))],
        out_shape=[jax.ShapeDtypeStruct((t, n), bf16), jax.ShapeDtypeStruct((t, LANES), f32)],
        scratch_shapes=[pltpu.VMEM((tm, d), bf16)],
        compiler_params=_cparams(("parallel", "arbitrary")),
        name="in_projection",
    )(x2, norm_w, w_main_t, w_f_t, cos, sin)


def _fcum_kernel(f_ref, b_ref, c_ref):
    z = f_ref[...] + b_ref[...]
    ls = jnp.minimum(z, 0.0) - jnp.log1p(jnp.exp(-jnp.abs(z)))
    c = ls.T[:FOX_HEADS, :]
    s = c.shape[1]
    lane = lax.broadcasted_iota(jnp.int32, c.shape, 1)
    sh = 1
    while sh < s:
        c = c + jnp.where(lane >= sh, pltpu.roll(c, sh, axis=1), 0.0)
        sh *= 2
    c_ref[0] = c * LOG2E


def _forget_cumsum(f_logit, bias, b, s):
    return pl.pallas_call(
        _fcum_kernel,
        grid=(b,),
        in_specs=[pl.BlockSpec((s, LANES), lambda i: (i, 0)),
                  pl.BlockSpec((1, LANES), lambda i: (0, 0))],
        out_specs=pl.BlockSpec((1, FOX_HEADS, s), lambda i: (i, 0, 0)),
        out_shape=jax.ShapeDtypeStruct((b, FOX_HEADS, s), f32),
        compiler_params=_cparams(("parallel",)),
        name="forget_cumsum",
    )(f_logit, bias)


def _fox_kernel(q_ref, k_ref, v_ref, c_ref, o_ref, va_ref, acc_ref, *, tq):
    s_len = q_ref.shape[0]
    nq = s_len // tq
    n_heads = q_ref.shape[1] // HEAD_DIM
    row = lax.broadcasted_iota(jnp.int32, (tq, tq), 0)
    col = lax.broadcasted_iota(jnp.int32, (tq, tq), 1)
    causal = row >= col
    def scores(hh):
        lanes = slice(hh * HEAD_DIM, (hh + 1) * HEAD_DIM)
        va_ref[hh, :, :HEAD_DIM] = v_ref[:, lanes]
        va_ref[hh, :, HEAD_DIM:] = jnp.ones((s_len, HEAD_DIM), va_ref.dtype)
        panels = []
        row_max = [None] * nq
        for kb in range(nq):
            r0 = kb * tq
            s = lax.dot_general(q_ref[r0:, lanes], k_ref[r0:r0 + tq, lanes], _NT, preferred_element_type=f32)
            s = s - c_ref[hh, :, r0:r0 + tq]
            parts = [jnp.where(causal, s[:tq], NEG_BIG)]
            parts += [s[(j - kb) * tq:(j - kb + 1) * tq] for j in range(kb + 1, nq)]
            panels.append(parts)
            for j, part in zip(range(kb, nq), parts):
                pm = jnp.max(part, axis=-1, keepdims=True)
                row_max[j] = pm if row_max[j] is None else jnp.maximum(row_max[j], pm)
        return panels, row_max

    def values(hh, panels, row_max):
        lanes = slice(hh * HEAD_DIM, (hh + 1) * HEAD_DIM)
        for kb in range(nq):
            r0 = kb * tq
            p = jnp.concatenate([jnp.exp2(part - row_max[j]).astype(bf16)
                                 for j, part in zip(range(kb, nq), panels[kb])], axis=0)
            contrib = jnp.dot(p, va_ref[hh, r0:r0 + tq, :], preferred_element_type=f32)
            if kb:
                acc_ref[hh, r0:, :] += contrib
            else:
                acc_ref[hh] = contrib
        acc = acc_ref[hh]
        o_ref[:, lanes] = (acc[:, :HEAD_DIM] / acc[:, HEAD_DIM:]).astype(o_ref.dtype)

    pending = None
    for hh in range(n_heads):
        current = (hh,) + scores(hh)
        if pending is not None:
            values(*pending)
        pending = current
    values(*pending)


def _fox_attention(proj, c_blk, b, s):
    tq = min(ATTN_BLOCK, s)
    hp = FOX_HEADS_PER_STEP
    w = hp * HEAD_DIM
    col = lambda c0: pl.BlockSpec((s, w), lambda i, j: (i, c0 // hp + j))
    return pl.pallas_call(
        functools.partial(_fox_kernel, tq=tq),
        grid=(b, FOX_HEADS // hp),
        in_specs=[col(COL_FQ), col(COL_FK), col(COL_FV),
                  pl.BlockSpec((hp, 1, s), lambda i, j: (i * (FOX_HEADS // hp) + j, 0, 0))],
        out_specs=pl.BlockSpec((s, w), lambda i, j: (i, j)),
        out_shape=jax.ShapeDtypeStruct((b * s, FOX_WIDTH), bf16),
        scratch_shapes=[pltpu.VMEM((hp, s, 2 * HEAD_DIM), bf16), pltpu.VMEM((hp, s, 2 * HEAD_DIM), f32)],
        compiler_params=_cparams(("parallel", "parallel")),
        name="fox_attention",
    )(proj, proj, proj, c_blk)


def _ret_kernel(q_ref, k_ref, v_ref, g_ref, dec_ref, xi_ref, zeta_ref, gc_ref, gnw_ref, o_ref, *, chunk):
    s_len = q_ref.shape[0]
    nc = s_len // chunk
    n_heads = q_ref.shape[1] // HEAD_DIM
    for hh in range(n_heads):
        lanes = slice(hh * HEAD_DIM, (hh + 1) * HEAD_DIM)
        state = jnp.zeros((HEAD_DIM, HEAD_DIM), f32)
        for ci in range(nc):
            rows = slice(ci * chunk, (ci + 1) * chunk)
            qb = q_ref[rows, lanes]
            kb = k_ref[rows, lanes]
            v = v_ref[rows, lanes]
            inner = lax.dot_general(qb, kb, _NT, preferred_element_type=f32) * dec_ref[hh]
            o = jnp.dot(inner.astype(bf16), v, preferred_element_type=f32)
            if ci:
                o = o + jnp.dot(qb, state.astype(bf16), preferred_element_type=f32) * xi_ref[hh]
            if ci + 1 < nc:
                kz = (kb.astype(f32) * zeta_ref[hh]).astype(bf16)
                kv = lax.dot_general(kz, v, (((0,), (0,)), ((), ())), preferred_element_type=f32)
                state = state * gc_ref[hh] + kv if ci else kv
            mu = jnp.mean(o, axis=-1, keepdims=True)
            oc = o - mu
            var = jnp.mean(oc * oc, axis=-1, keepdims=True)
            y = oc * lax.rsqrt(var + EPS) * gnw_ref[:, lanes]
            g = g_ref[rows, lanes].astype(f32)
            o_ref[rows, lanes] = (g * jax.nn.sigmoid(g) * y).astype(o_ref.dtype)


def _retention_constants(chunk):
    hh = np.arange(RET_HEADS, dtype=np.float64)
    log_gamma = np.log1p(-(2.0 ** (-5.0 - hh)))
    n = np.arange(chunk, dtype=np.float64)
    diff = n[:, None] - n[None, :]
    decay = np.where(diff[None] >= 0, np.exp(diff[None] * log_gamma[:, None, None]), 0.0)
    xi = np.exp((n[None, :] + 1.0) * log_gamma[:, None])
    zeta = np.exp((chunk - 1.0 - n[None, :]) * log_gamma[:, None])
    g_chunk = np.exp(chunk * log_gamma)
    bc = lambda a: np.broadcast_to(a[:, :, None], (RET_HEADS, chunk, HEAD_DIM))
    gc = np.broadcast_to(g_chunk[:, None, None], (RET_HEADS, 1, HEAD_DIM))
    to = lambda a: jnp.asarray(np.ascontiguousarray(a), dtype=f32)
    return to(decay), to(bc(xi)), to(bc(zeta)), to(gc)


def _retention(proj, gn_w, b, s):
    chunk = min(RET_CHUNK, s)
    decay, xi, zeta, gc = _retention_constants(chunk)
    hp = RET_HEADS_PER_STEP
    w = hp * HEAD_DIM
    head = lambda col: pl.BlockSpec((s, w), lambda i, j: (i, col // hp + j))
    per_head = lambda shape: pl.BlockSpec((hp,) + shape, lambda i, j: (j, 0, 0))
    return pl.pallas_call(
        functools.partial(_ret_kernel, chunk=chunk),
        grid=(b, RET_HEADS // hp),
        in_specs=[head(COL_RQ), head(COL_RK), head(COL_RV), head(COL_RG),
                  per_head((chunk, chunk)), per_head((chunk, HEAD_DIM)), per_head((chunk, HEAD_DIM)),
                  per_head((1, HEAD_DIM)),
                  pl.BlockSpec((1, w), lambda i, j: (0, j))],
        out_specs=pl.BlockSpec((s, w), lambda i, j: (i, j)),
        out_shape=jax.ShapeDtypeStruct((b * s, RET_WIDTH), bf16),
        compiler_params=_cparams(("parallel", "parallel")),
        name="retention",
    )(proj, proj, proj, proj, decay, xi, zeta, gc, gn_w)


def _branch_merge_kernel(fo_ref, ro_ref, g0_ref, g1_ref, bg_ref, wb_ref, mg_ref, *, ncol):
    d = mg_ref.shape[1]
    cw = d // ncol
    fo = fo_ref[...]
    ro = ro_ref[...]
    for c in range(ncol):
        cs = slice(c * cw, (c + 1) * cw)
        bd0 = jnp.dot(fo, wb_ref[0, :, cs], preferred_element_type=f32)
        bd1 = jnp.dot(ro, wb_ref[1, :, cs], preferred_element_type=f32)
        ga = jax.nn.sigmoid(g0_ref[:, cs].astype(f32) + bg_ref[0:1, cs])
        gb = jax.nn.sigmoid(g1_ref[:, cs].astype(f32) + bg_ref[1:2, cs])
        mg_ref[:, cs] = (ga * bd0 + gb * bd1).astype(mg_ref.dtype)


def _branch_merge(fox_o, ret_o, proj, b_gate, w_branch):
    t = fox_o.shape[0]
    d = D_MODEL
    tm = min(MERGE_ROWS, t)
    gate0 = COL_GATE * HEAD_DIM // d
    return pl.pallas_call(
        functools.partial(_branch_merge_kernel, ncol=4),
        grid=(t // tm,),
        in_specs=[pl.BlockSpec((tm, FOX_WIDTH), lambda i: (i, 0)),
                  pl.BlockSpec((tm, RET_WIDTH), lambda i: (i, 0)),
                  pl.BlockSpec((tm, d), lambda i: (i, gate0)),
                  pl.BlockSpec((tm, d), lambda i: (i, gate0 + 1)),
                  _const_spec((N_BRANCH, d)),
                  _const_spec((N_BRANCH, FOX_WIDTH, d))],
        out_specs=pl.BlockSpec((tm, d), lambda i: (i, 0)),
        out_shape=jax.ShapeDtypeStruct((t, d), bf16),
        compiler_params=_cparams(("parallel",)),
        name="branch_merge",
    )(fox_o, ret_o, proj, proj, b_gate, w_branch)


def _route(logits):
    lane = lax.broadcasted_iota(jnp.int32, logits.shape, 1).astype(f32)
    far = jnp.float32(1e9)
    gmask = lane < N_GROUPS
    gl = jnp.where(gmask, logits, NEG_BIG)
    gmax = jnp.max(gl, axis=-1, keepdims=True)
    gsel = jnp.min(jnp.where(gl == gmax, lane, far), axis=-1, keepdims=True)
    p_group = 1.0 / jnp.sum(jnp.where(gmask, jnp.exp(gl - gmax), 0.0), axis=-1, keepdims=True)
    lo = N_GROUPS + EXPERTS_PER_GROUP * gsel
    emask = (lane >= lo) & (lane < lo + EXPERTS_PER_GROUP)
    el = jnp.where(emask, logits, NEG_BIG)
    m1 = jnp.max(el, axis=-1, keepdims=True)
    i1 = jnp.min(jnp.where(el == m1, lane, far), axis=-1, keepdims=True)
    el2 = jnp.where(lane == i1, NEG_BIG, el)
    m2 = jnp.max(el2, axis=-1, keepdims=True)
    i2 = jnp.min(jnp.where(el2 == m2, lane, far), axis=-1, keepdims=True)
    t = jnp.exp(m2 - m1)
    w1 = p_group / (1.0 + t)
    w2 = p_group * t / (1.0 + t)
    out = jnp.where(lane == 0, i1 - N_GROUPS, 0.0)
    out = jnp.where(lane == 1, i2 - N_GROUPS, out)
    out = jnp.where(lane == 2, w1, out)
    out = jnp.where(lane == 3, w2, out)
    return out


def _outproj_kernel(mg_ref, x_ref, wo_ref, nfw_ref, wr_ref, br_ref, h_ref, hs_ref, route_ref, cnt_ref):
    h = x_ref[...] + jnp.dot(mg_ref[...], wo_ref[...], preferred_element_type=f32)
    h_ref[...] = h
    hn = h * lax.rsqrt(jnp.mean(h * h, axis=-1, keepdims=True) + EPS) * nfw_ref[...]
    _rows_to_slabs(hs_ref, hn, hn.shape[0])
    hi = hn.astype(bf16)
    lo = (hn - hi.astype(f32)).astype(bf16)
    hw = jnp.dot(hi, wr_ref[...], preferred_element_type=f32)
    logits = (hw[:, :LANES] + jnp.dot(lo, wr_ref[:, :LANES], preferred_element_type=f32) + hw[:, LANES:]) + br_ref[...]
    route = _route(logits)
    route_ref[...] = route

    @pl.when(pl.program_id(0) == 0)
    def _():
        cnt_ref[...] = jnp.zeros_like(cnt_ref)

    lane = lax.broadcasted_iota(jnp.int32, route.shape, 1).astype(f32)
    chosen = (lane == route[:, 0:1]).astype(f32) + (lane == route[:, 1:2]).astype(f32)
    cnt_ref[...] += jnp.sum(chosen, axis=0, keepdims=True)


def _outproj_router(merged, x2, w_out, nfw, wr_cat, b_router):
    t, d = x2.shape
    tm = min(OUTPROJ_ROWS, t)
    return pl.pallas_call(
        _outproj_kernel,
        grid=(t // tm,),
        in_specs=[pl.BlockSpec((tm, d), lambda i: (i, 0)),
                  pl.BlockSpec((tm, d), lambda i: (i, 0)),
                  _const_spec((d, d)),
                  _const_spec((1, d)),
                  _const_spec((d, 2 * LANES)), _const_spec((1, LANES))],
        out_specs=[pl.BlockSpec((tm, d), lambda i: (i, 0)),
                   pl.BlockSpec((tm * PITCH, LANES), lambda i: (i, 0)),
                   pl.BlockSpec((tm, LANES), lambda i: (i, 0)),
                   pl.BlockSpec((8, LANES), lambda i: (0, 0))],
        out_shape=[jax.ShapeDtypeStruct((t, d), f32), jax.ShapeDtypeStruct((t * PITCH, LANES), u32),
                   jax.ShapeDtypeStruct((t, LANES), f32), jax.ShapeDtypeStruct((8, LANES), f32)],
        compiler_params=_cparams(("arbitrary",)),
        name="outproj_router",
    )(merged, x2, w_out, nfw, wr_cat, b_router)


def _plan_kernel(route_ref, cnt_ref, dest_ref, ends_ref, carry_ref, *, bm):
    r = route_ref[...]
    tb = r.shape[0]
    lane = lax.broadcasted_iota(jnp.int32, r.shape, 1).astype(f32)
    o0 = (lane == r[:, 0:1]).astype(f32)
    o1 = (lane == r[:, 1:2]).astype(f32)
    o = o0 + o1

    @pl.when(pl.program_id(0) == 0)
    def _():
        cnt = cnt_ref[...]
        padded = jnp.floor((cnt + (bm - 1)) * (1.0 / bm)) * bm
        lane8 = lax.broadcasted_iota(jnp.int32, cnt.shape, 1)
        ends = padded
        sh = 1
        while sh < LANES:
            ends = ends + jnp.where(lane8 >= sh, pltpu.roll(ends, sh, axis=1), 0.0)
            sh *= 2
        sub = lax.broadcasted_iota(jnp.int32, cnt.shape, 0)
        ends_ref[...] = jnp.where(sub == 0, ends, jnp.where(sub == 1, padded, cnt))
        carry_ref[...] = ends - padded

    row = lax.broadcasted_iota(jnp.int32, (tb, tb), 0)
    col = lax.broadcasted_iota(jnp.int32, (tb, tb), 1)
    lower = jnp.where(col < row, 1.0, 0.0).astype(bf16)
    before = jnp.dot(lower, o.astype(bf16), preferred_element_type=f32) + carry_ref[0:1, :]
    d0 = jnp.sum(before * o0, axis=-1, keepdims=True)
    d1 = jnp.sum(before * o1, axis=-1, keepdims=True)
    dd = jnp.where(lane == 0, d0, jnp.where(lane == 1, d1, 0.0))
    dest_ref[0] = dd.T[:8, :].astype(jnp.int32)
    carry_ref[...] += jnp.sum(o, axis=0, keepdims=True)


def _dispatch_plan(route, counts, bm):
    t = route.shape[0]
    tb = min(PLAN_ROWS, t)
    nb = t // tb
    return pl.pallas_call(
        functools.partial(_plan_kernel, bm=bm),
        grid=(nb,),
        in_specs=[pl.BlockSpec((tb, LANES), lambda i: (i, 0)),
                  pl.BlockSpec((8, LANES), lambda i: (0, 0))],
        out_specs=[pl.BlockSpec((1, 8, tb), lambda i: (i, 0, 0)),
                   pl.BlockSpec((8, LANES), lambda i: (0, 0))],
        out_shape=[jax.ShapeDtypeStruct((nb, 8, tb), jnp.int32), jax.ShapeDtypeStruct((8, LANES), f32)],
        scratch_shapes=[pltpu.VMEM((8, LANES), f32)],
        compiler_params=_cparams(("arbitrary",)),
        name="dispatch_plan",
    )(route, counts)


def _dispatch_kernel(d0_ref, d1_ref, fill_ref, end_ref, nu_ref, hs_ref, xs_hbm, zbuf, sem, zsem,
                     *, tb, bm, n_blk, unroll=8):
    i = pl.program_id(0)
    base = i * tb
    blk_rows = bm * PITCH

    def pad_copy(s, n):
        return pltpu.make_async_copy(zbuf.at[pl.ds(0, n * PITCH), :], xs_hbm.at[pl.ds(s * PITCH, n * PITCH), :], zsem)

    def tail_copy(g):
        return pltpu.make_async_copy(zbuf, xs_hbm.at[pl.ds(pl.multiple_of(g * blk_rows, 8), blk_rows), :], zsem)

    @pl.when(i == 0)
    def _():
        zbuf[...] = jnp.zeros(zbuf.shape, u32)
        for op in ("start", "wait"):
            def per_expert(e, c, op=op):
                s = fill_ref[e]
                for n in PAD_FILL_CHUNKS:
                    count = (end_ref[e] - s) // n

                    def one(k, c2, s=s, n=n):
                        getattr(pad_copy(s + k * n, n), op)()
                        return c2

                    c = lax.fori_loop(0, count, one, c)
                    s = s + count * n
                return c

            def per_tail(g, c, op=op):
                getattr(tail_copy(g), op)()
                return c

            lax.fori_loop(0, N_EXPERTS, per_expert, 0)
            lax.fori_loop(nu_ref[0], n_blk, per_tail, 0)

    def body(g, c):
        for u in range(unroll):
            r = g * unroll + u
            src = hs_ref.at[pl.ds(r * PITCH, PITCH), :]
            for k, dref in enumerate((d0_ref, d1_ref)):
                d = dref[base + r]
                pltpu.make_async_copy(src, xs_hbm.at[pl.ds(d * PITCH, PITCH), :], sem).start(priority=k)
        return c

    lax.fori_loop(0, tb // unroll, body, 0)
    for _ in range(TOP_K):
        pltpu.make_async_copy(xs_hbm.at[pl.ds(0, tb * PITCH), :], xs_hbm.at[pl.ds(0, tb * PITCH), :], sem).wait()


def _dispatch_rows(dest0, dest1, fill_start, seg_end, n_used, hs, cap):
    t = dest0.shape[0]
    tb = min(DISPATCH_ROWS, t)
    bm = MOE_BLOCK
    grid_spec = pltpu.PrefetchScalarGridSpec(
        num_scalar_prefetch=5,
        grid=(t // tb,),
        in_specs=[pl.BlockSpec((tb * PITCH, LANES), lambda i, *_: (i, 0))],
        out_specs=pl.BlockSpec(memory_space=pl.ANY),
        scratch_shapes=[pltpu.VMEM((bm * PITCH, LANES), u32), pltpu.SemaphoreType.DMA(()),
                        pltpu.SemaphoreType.DMA(())],
    )
    return pl.pallas_call(
        functools.partial(_dispatch_kernel, tb=tb, bm=bm, n_blk=cap // bm),
        grid_spec=grid_spec,
        out_shape=jax.ShapeDtypeStruct((cap * PITCH, LANES), u32),
        compiler_params=_cparams(("arbitrary",)),
        name="dispatch_rows",
    )(dest0, dest1, fill_start, seg_end, n_used, hs)


def _moe_kernel(g0_ref, nb_ref, nu_ref, x_hbm, w1_ref, w3_ref, w2_ref, y_hbm,
                xbuf, ybuf, w1b, w3b, w2b, gsem, ysem, *, bm, n_blk):
    e = pl.program_id(0)
    n_used = nu_ref[0]
    g0 = g0_ref[e]
    nb = nb_ref[e]
    blk_rows = bm * PITCH

    def x_copy(slot, g):
        src = x_hbm.at[pl.ds(pl.multiple_of(g * blk_rows, 8), blk_rows), :]
        return pltpu.make_async_copy(src, xbuf.at[slot], gsem.at[slot])

    def y_copy(slot, g):
        dst = y_hbm.at[pl.ds(pl.multiple_of(g * blk_rows, 8), blk_rows), :]
        return pltpu.make_async_copy(ybuf.at[slot], dst, ysem.at[slot])

    @pl.when(e == 0)
    def _():
        x_copy(0, 0).start()
        for k in range(1, X_BUFS - 1):
            pl.when(n_used > k)(lambda k=k: x_copy(k, k).start())

    @pl.when(nb > 0)
    def _():
        w1b[...] = w1_ref[0].astype(bf16)
        w3b[...] = w3_ref[0].astype(bf16)
        w2b[...] = w2_ref[0].astype(bf16)

    def block(c, carry):
        g = g0 + c
        slot = g % 2
        xslot = g % X_BUFS

        ahead = g + (X_BUFS - 1)

        @pl.when(ahead < n_used)
        def _():
            x_copy(ahead % X_BUFS, ahead).start()

        x_copy(xslot, g).wait()

        hn = _slabs_to_rows(xbuf.at[xslot], bm).astype(bf16)
        a = jnp.dot(hn, w1b[...], preferred_element_type=f32)
        b = jnp.dot(hn, w3b[...], preferred_element_type=f32)
        mid = (a * jax.nn.sigmoid(a) * b).astype(bf16)
        y = jnp.dot(mid, w2b[...], preferred_element_type=f32)

        @pl.when(g >= 2)
        def _():
            y_copy(slot, g - 2).wait()

        _rows_to_slabs(ybuf.at[slot], y, bm)
        y_copy(slot, g).start()
        return carry

    lax.fori_loop(0, nb, block, 0)

    @pl.when(e == pl.num_programs(0) - 1)
    def _():
        @pl.when(n_used >= 2)
        def _():
            y_copy(n_used % 2, n_used - 2).wait()

        y_copy((n_used - 1) % 2, n_used - 1).wait()
        ybuf[0] = jnp.zeros(ybuf.shape[1:], u32)

        def zstart(g, carry):
            y_copy(0, g).start()
            return carry

        def zwait(g, carry):
            y_copy(0, g).wait()
            return carry

        lax.fori_loop(n_used, n_blk, zstart, 0)
        lax.fori_loop(n_used, n_blk, zwait, 0)


def _expert_mlp(blk_start, blk_count, n_used, xs, w1, w3, w2, cap):
    d = D_MODEL
    bm = MOE_BLOCK
    n_blk = cap // bm
    by_expert = lambda shape: pl.BlockSpec(shape, lambda e, g0, nb, nu: (e, 0, 0))
    grid_spec = pltpu.PrefetchScalarGridSpec(
        num_scalar_prefetch=3,
        grid=(N_EXPERTS,),
        in_specs=[pl.BlockSpec(memory_space=pl.ANY),
                  by_expert((1, d, D_EXPERT)), by_expert((1, d, D_EXPERT)), by_expert((1, D_EXPERT, d))],
        out_specs=pl.BlockSpec(memory_space=pl.ANY),
        scratch_shapes=[pltpu.VMEM((X_BUFS, bm * PITCH, LANES), u32), pltpu.VMEM((2, bm * PITCH, LANES), u32),
                        pltpu.VMEM((d, D_EXPERT), bf16), pltpu.VMEM((d, D_EXPERT), bf16),
                        pltpu.VMEM((D_EXPERT, d), bf16),
                        pltpu.SemaphoreType.DMA((X_BUFS,)), pltpu.SemaphoreType.DMA((2,))],
    )
    return pl.pallas_call(
        functools.partial(_moe_kernel, bm=bm, n_blk=n_blk),
        grid_spec=grid_spec,
        out_shape=jax.ShapeDtypeStruct((cap * PITCH, LANES), u32),
        compiler_params=_cparams(("arbitrary",)),
        name="expert_mlp",
    )(blk_start, blk_count, n_used, xs, w1, w3, w2)


def _combine_kernel(d0_ref, d1_ref, h_ref, route_ref, y_hbm, nw_ref, o_ref, ybuf, sem, *, tb):
    i = pl.program_id(0)
    n = pl.num_programs(0)
    slot = i % 2

    def start(blk, s):
        _slab_gather_start(d0_ref, blk * tb, tb, y_hbm, ybuf.at[s, 0], sem.at[s])
        _slab_gather_start(d1_ref, blk * tb, tb, y_hbm, ybuf.at[s, 1], sem.at[s])

    @pl.when(i == 0)
    def _():
        start(0, 0)

    @pl.when(i + 1 < n)
    def _():
        start(i + 1, 1 - slot)

    for k in range(TOP_K):
        _slab_gather_wait(tb, y_hbm, ybuf.at[slot, k], sem.at[slot])
    r = route_ref[...]
    h = h_ref[...] + (r[:, 2:3] * _slabs_to_rows(ybuf.at[slot, 0], tb)
                      + r[:, 3:4] * _slabs_to_rows(ybuf.at[slot, 1], tb))
    o_ref[...] = h * lax.rsqrt(jnp.mean(h * h, axis=-1, keepdims=True) + EPS) * nw_ref[...]


def _combine(dest0, dest1, h, route, ys, norm_w):
    t, d = h.shape
    tb = min(COMBINE_ROWS, t)
    grid_spec = pltpu.PrefetchScalarGridSpec(
        num_scalar_prefetch=2,
        grid=(t // tb,),
        in_specs=[pl.BlockSpec((tb, d), lambda i, a, b: (i, 0)),
                  pl.BlockSpec((tb, LANES), lambda i, a, b: (i, 0)),
                  pl.BlockSpec(memory_space=pl.ANY),
                  pl.BlockSpec((1, d), lambda i, a, b: (0, 0))],
        out_specs=pl.BlockSpec((tb, d), lambda i, a, b: (i, 0)),
        scratch_shapes=[pltpu.VMEM((2, TOP_K, tb * PITCH, LANES), u32), pltpu.SemaphoreType.DMA((2,))],
    )
    return pl.pallas_call(
        functools.partial(_combine_kernel, tb=tb),
        grid_spec=grid_spec,
        out_shape=jax.ShapeDtypeStruct((t, d), f32),
        compiler_params=_cparams(("arbitrary",)),
        name="combine_final_norm",
    )(dest0, dest1, h, route, ys, norm_w)


def kernel(x, positions, norm_mix_w, w_in, fox_b_f, ret_gn_w, w_branch, b_gate, w_out, norm_ffn_w, w_router_group,
           b_router_group, w_router_expert, b_router_expert, w1, w3, w2, norm_final_w):
    b, s, d = x.shape
    assert d == D_MODEL and norm_mix_w.shape[0] == 1
    t = b * s
    x2 = x.reshape(t, d)

    wt = w_in[0].T
    o_f = 3 * FOX_WIDTH
    o_r = o_f + FOX_HEADS
    o_g = o_r + 4 * RET_WIDTH
    tn = PROJ_COLS
    tile_rows = ([o_g + k * tn for k in range(N_BRANCH * D_MODEL // tn)]
                 + [k * tn for k in range(o_f // tn)]
                 + [o_r + k * tn for k in range(4 * RET_WIDTH // tn)])
    w_main = _prep_in_weights(wt, tile_rows)
    w_f = jnp.pad(wt[o_f:o_r], ((0, LANES - FOX_HEADS), (0, 0))).astype(bf16)
    bias_f = jnp.pad(fox_b_f[0], (0, LANES - FOX_HEADS)).reshape(1, LANES)
    w_r = jnp.concatenate([w_router_group[0], w_router_expert[0]], axis=1)
    w_r = jnp.pad(w_r, ((0, 0), (0, LANES - w_r.shape[1])))
    wr_hi = w_r.astype(bf16)
    wr_lo = (w_r - wr_hi.astype(f32)).astype(bf16)
    b_r = jnp.pad(jnp.concatenate([b_router_group[0], b_router_expert[0]]), (0, LANES - N_GROUPS - N_EXPERTS))
    b_r = b_r.reshape(1, LANES)

    cos, sin = _rope_tables(positions)
    proj, f_logit = _in_projection(x2, norm_mix_w, w_main, w_f, cos.reshape(t, HEAD_DIM), sin.reshape(t, HEAD_DIM))
    c = _forget_cumsum(f_logit, bias_f, b, s)
    fox_o = _fox_attention(proj, c.reshape(b * FOX_HEADS, 1, s), b, s)
    ret_o = _retention(proj, ret_gn_w, b, s)
    merged = _branch_merge(fox_o, ret_o, proj, b_gate[0], w_branch[0].astype(bf16))
    h, h_slabs, route, counts = _outproj_router(merged, x2, w_out[0].astype(bf16), norm_ffn_w,
                                        jnp.concatenate([wr_hi, wr_lo], axis=1), b_r)

    bm = MOE_BLOCK
    cap = t * TOP_K + N_EXPERTS * bm
    dest_rows, seg = _dispatch_plan(route, counts, bm)
    dest0 = dest_rows[:, 0, :].reshape(t)
    dest1 = dest_rows[:, 1, :].reshape(t)
    pends = seg[0, :N_EXPERTS].astype(jnp.int32)
    padded = seg[1, :N_EXPERTS].astype(jnp.int32)
    counts = seg[2, :N_EXPERTS].astype(jnp.int32)
    pstarts = pends - padded
    n_used = pends[-1:] // bm

    xs = _dispatch_rows(dest0, dest1, pstarts + counts, pends, n_used, h_slabs, cap)
    y = _expert_mlp(pstarts // bm, padded // bm, n_used, xs, w1[0], w3[0], w2[0], cap)
    out = _combine(dest0, dest1, h, route, y, norm_final_w.reshape(1, d))
    return out.reshape(b, s, d)
```

```python
import functools
import math

import numpy as np
import jax
import jax.numpy as jnp
from jax import lax
from jax.experimental import pallas as pl
from jax.experimental.pallas import tpu as pltpu

D_MODEL = 2048
HEAD_DIM = 128
FOX_HEADS = 8
RET_HEADS = 8
FOX_WIDTH = FOX_HEADS * HEAD_DIM
RET_WIDTH = RET_HEADS * HEAD_DIM
N_BRANCH = 2
N_GROUPS = 4
EXPERTS_PER_GROUP = 8
N_EXPERTS = N_GROUPS * EXPERTS_PER_GROUP
TOP_K = 2
D_EXPERT = 512
ROPE_BASE = 10000.0
EPS = 1e-6

LANES = 128
NEG_BIG = -1e30
LOG2E = math.log2(math.e)
COL_GATE = 0
COL_FQ = N_BRANCH * D_MODEL // HEAD_DIM
COL_FK = COL_FQ + FOX_HEADS
COL_FV = COL_FK + FOX_HEADS
COL_RQ = COL_FV + FOX_HEADS
COL_RK = COL_RQ + RET_HEADS
COL_RV = COL_RK + RET_HEADS
COL_RG = COL_RV + RET_HEADS

ATTN_BLOCK = 256
FOX_HEADS_PER_STEP = 4
RET_HEADS_PER_STEP = 4
RET_CHUNK = 256
MOE_BLOCK = 256
X_BUFS = 5
PAD_FILL_CHUNKS = (64, 8, 1)
PROJ_ROWS, PROJ_COLS = 1024, 1024
MERGE_ROWS = 1024
OUTPROJ_ROWS = 512
PLAN_ROWS = 1024
DISPATCH_ROWS = 2048
COMBINE_ROWS = 256
VMEM_LIMIT = 56 * 1024 * 1024

f32 = jnp.float32
bf16 = jnp.bfloat16
_NT = (((1,), (1,)), ((), ()))
SLAB = D_MODEL // (2 * LANES)
PITCH = SLAB + 1
u32 = jnp.uint32
_HI_MASK = np.uint32(0xFFFF0000)


def _rows_to_slabs(ref2, val, n_rows):
    bits = lambda v: lax.bitcast_convert_type(v.astype(bf16).astype(f32), u32)
    for j in range(SLAB):
        lo = val[:, (2 * j) * LANES:(2 * j + 1) * LANES]
        hi = val[:, (2 * j + 1) * LANES:(2 * j + 2) * LANES]
        ref2[pl.ds(j, n_rows, stride=PITCH), :] = (bits(lo) >> 16) | (bits(hi) & _HI_MASK)
    ref2[pl.ds(SLAB, n_rows, stride=PITCH), :] = jnp.zeros((n_rows, LANES), u32)


def _slabs_to_rows(ref2, n_rows):
    chunks = []
    for j in range(SLAB):
        w = ref2[pl.ds(j, n_rows, stride=PITCH), :]
        chunks.append(lax.bitcast_convert_type(w << 16, f32))
        chunks.append(lax.bitcast_convert_type(w & _HI_MASK, f32))
    return jnp.concatenate(chunks, axis=-1)


def _slab_gather_start(idx_ref, base, n_rows, src_hbm, dst, sem, unroll=8):
    def body(g, _):
        for u in range(unroll):
            r = g * unroll + u
            t = idx_ref[base + r]
            pltpu.make_async_copy(src_hbm.at[pl.ds(t * PITCH, SLAB), :], dst.at[pl.ds(r * PITCH, SLAB), :],
                                  sem).start(priority=u % 2)
        return 0
    lax.fori_loop(0, n_rows // unroll, body, 0)


def _slab_gather_wait(n_rows, src_hbm, dst, sem):
    pltpu.make_async_copy(src_hbm.at[pl.ds(0, n_rows * SLAB), :], dst.at[pl.ds(0, n_rows * SLAB), :], sem).wait()


def _cparams(sem, vmem=VMEM_LIMIT):
    return pltpu.CompilerParams(dimension_semantics=sem, vmem_limit_bytes=vmem)


def _const_spec(shape):
    return pl.BlockSpec(shape, lambda i: tuple(0 for _ in shape), pipeline_mode=pl.Buffered(1))


def _rope_kernel(pos_ref, freq_ref, sgn_ref, cos_ref, sin_ref):
    pos = pos_ref[0].astype(f32)
    hs = pos.shape[0] // 2
    half = HEAD_DIM // 2
    lane = lax.broadcasted_iota(jnp.int32, (hs, HEAD_DIM), 1)
    low = lane < half
    ang = jnp.where(low, pos[:hs], pos[hs:]) * freq_ref[...]
    c = jnp.cos(ang)
    s = jnp.sin(ang) * sgn_ref[...]
    c_sw = pltpu.roll(c, half, axis=1)
    s_sw = pltpu.roll(s, half, axis=1)
    cos_ref[0, :hs, :] = jnp.where(low, c, c_sw)
    cos_ref[0, hs:, :] = jnp.where(low, c_sw, c)
    sin_ref[0, :hs, :] = jnp.where(low, s, -s_sw)
    sin_ref[0, hs:, :] = jnp.where(low, -s_sw, s)


def _rope_tables(positions):
    b, s = positions.shape
    half = HEAD_DIM // 2
    inv = (np.float32(ROPE_BASE) ** (-np.arange(half, dtype=np.float32) / np.float32(half))).astype(np.float32)
    freq = jnp.asarray(np.concatenate([inv, inv])[None, :])
    sgn = jnp.asarray(np.concatenate([-np.ones(half, np.float32), np.ones(half, np.float32)])[None, :])
    out = jax.ShapeDtypeStruct((b, s, HEAD_DIM), f32)
    return pl.pallas_call(
        _rope_kernel,
        grid=(b,),
        in_specs=[pl.BlockSpec((1, s, 1), lambda i: (i, 0, 0)),
                  pl.BlockSpec((1, HEAD_DIM), lambda i: (0, 0)),
                  pl.BlockSpec((1, HEAD_DIM), lambda i: (0, 0))],
        out_specs=[pl.BlockSpec((1, s, HEAD_DIM), lambda i: (i, 0, 0)),
                   pl.BlockSpec((1, s, HEAD_DIM), lambda i: (i, 0, 0))],
        out_shape=[out, out],
        compiler_params=_cparams(("parallel",)),
        name="rope_tables",
    )(positions.reshape(b, s, 1), freq, sgn)


_Q_FACTORS = ((COL_FQ * HEAD_DIM // PROJ_COLS, LOG2E / math.sqrt(HEAD_DIM)),
              (COL_RQ * HEAD_DIM // PROJ_COLS, 1.0 / math.sqrt(HEAD_DIM)))


def _wprep_kernel(off_ref, w_hbm, o_ref, buf, sem, *, tn):
    j = pl.program_id(0)
    n = pl.num_programs(0)
    slot = j % 2

    def copy(jj, s):
        src = w_hbm.at[pl.ds(pl.multiple_of(off_ref[jj], 8), tn), :]
        return pltpu.make_async_copy(src, buf.at[s], sem.at[s])

    @pl.when(j == 0)
    def _():
        copy(0, 0).start()

    @pl.when(j + 1 < n)
    def _():
        copy(j + 1, 1 - slot).start()

    copy(j, slot).wait()
    factor = jnp.float32(1.0)
    for tile, value in _Q_FACTORS:
        factor = jnp.where(j == tile, jnp.float32(value), factor)
    o_ref[...] = (buf[slot] * factor).astype(o_ref.dtype)


def _prep_in_weights(wt, row_offsets):
    tn = PROJ_COLS
    d = wt.shape[1]
    n_tiles = len(row_offsets)
    grid_spec = pltpu.PrefetchScalarGridSpec(
        num_scalar_prefetch=1,
        grid=(n_tiles,),
        in_specs=[pl.BlockSpec(memory_space=pl.ANY)],
        out_specs=pl.BlockSpec((tn, d), lambda j, off: (j, 0)),
        scratch_shapes=[pltpu.VMEM((2, tn, d), f32), pltpu.SemaphoreType.DMA((2,))],
    )
    return pl.pallas_call(
        functools.partial(_wprep_kernel, tn=tn),
        grid_spec=grid_spec,
        out_shape=jax.ShapeDtypeStruct((n_tiles * tn, d), bf16),
        compiler_params=_cparams(("arbitrary",)),
        name="prep_in_weights",
    )(jnp.asarray(row_offsets, jnp.int32), wt)


_ROT_TILES = (COL_RQ * HEAD_DIM // PROJ_COLS, COL_RK * HEAD_DIM // PROJ_COLS)


def _inproj_kernel(x_ref, nw_ref, w_ref, wf_ref, cos_ref, sin_ref, o_ref, f_ref, xn_ref):
    j = pl.program_id(1)

    @pl.when(j == 0)
    def _():
        x = x_ref[...]
        ms = jnp.mean(x * x, axis=-1, keepdims=True)
        xn = (x * lax.rsqrt(ms + EPS) * nw_ref[...]).astype(bf16)
        xn_ref[...] = xn
        f_ref[...] = lax.dot_general(xn, wf_ref[...], _NT, preferred_element_type=f32)

    rotated = (j == _ROT_TILES[0]) | (j == _ROT_TILES[1])

    @pl.when(jnp.logical_not(rotated))
    def _():
        o_ref[...] = lax.dot_general(xn_ref[...], w_ref[...], _NT, preferred_element_type=f32).astype(o_ref.dtype)

    @pl.when(rotated)
    def _():
        acc = lax.dot_general(xn_ref[...], w_ref[...], _NT, preferred_element_type=f32)
        cos = cos_ref[...]
        sin = sin_ref[...]
        for hh in range(acc.shape[1] // HEAD_DIM):
            lanes = slice(hh * HEAD_DIM, (hh + 1) * HEAD_DIM)
            t = acc[:, lanes]
            o_ref[:, lanes] = (t * cos + pltpu.roll(t, HEAD_DIM // 2, axis=1) * sin).astype(o_ref.dtype)


def _in_projection(x2, norm_w, w_main_t, w_f_t, cos, sin):
    t, d = x2.shape
    n = w_main_t.shape[0]
    tm = min(PROJ_ROWS, t)
    tn = PROJ_COLS
    return pl.pallas_call(
        _inproj_kernel,
        grid=(t // tm, n // tn),
        in_specs=[pl.BlockSpec((tm, d), lambda i, j: (i, 0)),
                  pl.BlockSpec((1, d), lambda i, j: (0, 0)),
                  pl.BlockSpec((tn, d), lambda i, j: (j, 0)),
                  pl.BlockSpec((LANES, d), lambda i, j: (0, 0)),
                  pl.BlockSpec((tm, HEAD_DIM), lambda i, j: (i, 0)),
                  pl.BlockSpec((tm, HEAD_DIM), lambda i, j: (i, 0))],
        out_specs=[pl.BlockSpec((tm, tn), lambda i, j: (i, j)),
                   pl.BlockSpec((tm, LANES), lambda i, j: (i, 0))],
        out_shape=[jax.ShapeDtypeStruct((t, n), bf16), jax.ShapeDtypeStruct((t, LANES), f32)],
        scratch_shapes=[pltpu.VMEM((tm, d), bf16)],
        compiler_params=_cparams(("parallel", "arbitrary")),
        name="in_projection",
    )(x2, norm_w, w_main_t, w_f_t, cos, sin)


def _fcum_kernel(f_ref, b_ref, c_ref):
    z = f_ref[...] + b_ref[...]
    ls = jnp.minimum(z, 0.0) - jnp.log1p(jnp.exp(-jnp.abs(z)))
    c = ls.T[:FOX_HEADS, :]
    s = c.shape[1]
    lane = lax.broadcasted_iota(jnp.int32, c.shape, 1)
    sh = 1
    while sh < s:
        c = c + jnp.where(lane >= sh, pltpu.roll(c, sh, axis=1), 0.0)
        sh *= 2
    c_ref[0] = c * LOG2E


def _forget_cumsum(f_logit, bias, b, s):
    return pl.pallas_call(
        _fcum_kernel,
        grid=(b,),
        in_specs=[pl.BlockSpec((s, LANES), lambda i: (i, 0)),
                  pl.BlockSpec((1, LANES), lambda i: (0, 0))],
        out_specs=pl.BlockSpec((1, FOX_HEADS, s), lambda i: (i, 0, 0)),
        out_shape=jax.ShapeDtypeStruct((b, FOX_HEADS, s), f32),
        compiler_params=_cparams(("parallel",)),
        name="forget_cumsum",
    )(f_logit, bias)


def _fox_kernel(q_ref, k_ref, v_ref, c_ref, o_ref, va_ref, acc_ref, *, tq):
    s_len = q_ref.shape[0]
    nq = s_len // tq
    n_heads = q_ref.shape[1] // HEAD_DIM
    row = lax.broadcasted_iota(jnp.int32, (tq, tq), 0)
    col = lax.broadcasted_iota(jnp.int32, (tq, tq), 1)
    causal = row >= col
    def scores(hh):
        lanes = slice(hh * HEAD_DIM, (hh + 1) * HEAD_DIM)
        va_ref[hh, :, :HEAD_DIM] = v_ref[:, lanes]
        va_ref[hh, :, HEAD_DIM:] = jnp.ones((s_len, HEAD_DIM), va_ref.dtype)
        panels = []
        row_max = [None] * nq
        for kb in range(nq):
            r0 = kb * tq
            s = lax.dot_general(q_ref[r0:, lanes], k_ref[r0:r0 + tq, lanes], _NT, preferred_element_type=f32)
            s = s - c_ref[hh, :, r0:r0 + tq]
            parts = [jnp.where(causal, s[:tq], NEG_BIG)]
            parts += [s[(j - kb) * tq:(j - kb + 1) * tq] for j in range(kb + 1, nq)]
            panels.append(parts)
            for j, part in zip(range(kb, nq), parts):
                pm = jnp.max(part, axis=-1, keepdims=True)
                row_max[j] = pm if row_max[j] is None else jnp.maximum(row_max[j], pm)
        return panels, row_max

    def values(hh, panels, row_max):
        lanes = slice(hh * HEAD_DIM, (hh + 1) * HEAD_DIM)
        for kb in range(nq):
            r0 = kb * tq
            p = jnp.concatenate([jnp.exp2(part - row_max[j]).astype(bf16)
                                 for j, part in zip(range(kb, nq), panels[kb])], axis=0)
            contrib = jnp.dot(p, va_ref[hh, r0:r0 + tq, :], preferred_element_type=f32)
            if kb:
                acc_ref[hh, r0:, :] += contrib
            else:
                acc_ref[hh] = contrib
        acc = acc_ref[hh]
        o_ref[:, lanes] = (acc[:, :HEAD_DIM] / acc[:, HEAD_DIM:]).astype(o_ref.dtype)

    pending = None
    for hh in range(n_heads):
        current = (hh,) + scores(hh)
        if pending is not None:
            values(*pending)
        pending = current
    values(*pending)


def _fox_attention(proj, c_blk, b, s):
    tq = min(ATTN_BLOCK, s)
    hp = FOX_HEADS_PER_STEP
    w = hp * HEAD_DIM
    col = lambda c0: pl.BlockSpec((s, w), lambda i, j: (i, c0 // hp + j))
    return pl.pallas_call(
        functools.partial(_fox_kernel, tq=tq),
        grid=(b, FOX_HEADS // hp),
        in_specs=[col(COL_FQ), col(COL_FK), col(COL_FV),
                  pl.BlockSpec((hp, 1, s), lambda i, j: (i * (FOX_HEADS // hp) + j, 0, 0))],
        out_specs=pl.BlockSpec((s, w), lambda i, j: (i, j)),
        out_shape=jax.ShapeDtypeStruct((b * s, FOX_WIDTH), bf16),
        scratch_shapes=[pltpu.VMEM((hp, s, 2 * HEAD_DIM), bf16), pltpu.VMEM((hp, s, 2 * HEAD_DIM), f32)],
        compiler_params=_cparams(("parallel", "parallel")),
        name="fox_attention",
    )(proj, proj, proj, c_blk)


def _ret_kernel(q_ref, k_ref, v_ref, g_ref, dec_ref, xi_ref, zeta_ref, gc_ref, gnw_ref, o_ref, *, chunk):
    s_len = q_ref.shape[0]
    nc = s_len // chunk
    n_heads = q_ref.shape[1] // HEAD_DIM
    states = [None] * n_heads
    for ci in range(nc):
        rows = slice(ci * chunk, (ci + 1) * chunk)
        for hh in range(n_heads):
            lanes = slice(hh * HEAD_DIM, (hh + 1) * HEAD_DIM)
            state = states[hh]
            qb = q_ref[rows, lanes]
            kb = k_ref[rows, lanes]
            v = v_ref[rows, lanes]
            inner = lax.dot_general(qb, kb, _NT, preferred_element_type=f32) * dec_ref[hh]
            o = jnp.dot(inner.astype(bf16), v, preferred_element_type=f32)
            if ci:
                o = o + jnp.dot(qb, state.astype(bf16), preferred_element_type=f32) * xi_ref[hh]
            if ci + 1 < nc:
                kz = (kb.astype(f32) * zeta_ref[hh]).astype(bf16)
                kv = lax.dot_general(kz, v, (((0,), (0,)), ((), ())), preferred_element_type=f32)
                states[hh] = state * gc_ref[hh] + kv if ci else kv
            mu = jnp.mean(o, axis=-1, keepdims=True)
            oc = o - mu
            var = jnp.mean(oc * oc, axis=-1, keepdims=True)
            y = oc * lax.rsqrt(var + EPS) * gnw_ref[:, lanes]
            g = g_ref[rows, lanes].astype(f32)
            o_ref[rows, lanes] = (g * jax.nn.sigmoid(g) * y).astype(o_ref.dtype)


def _retention_constants(chunk):
    hh = np.arange(RET_HEADS, dtype=np.float64)
    log_gamma = np.log1p(-(2.0 ** (-5.0 - hh)))
    n = np.arange(chunk, dtype=np.float64)
    diff = n[:, None] - n[None, :]
    decay = np.where(diff[None] >= 0, np.exp(diff[None] * log_gamma[:, None, None]), 0.0)
    xi = np.exp((n[None, :] + 1.0) * log_gamma[:, None])
    zeta = np.exp((chunk - 1.0 - n[None, :]) * log_gamma[:, None])
    g_chunk = np.exp(chunk * log_gamma)
    bc = lambda a: np.broadcast_to(a[:, :, None], (RET_HEADS, chunk, HEAD_DIM))
    gc = np.broadcast_to(g_chunk[:, None, None], (RET_HEADS, 1, HEAD_DIM))
    to = lambda a: jnp.asarray(np.ascontiguousarray(a), dtype=f32)
    return to(decay), to(bc(xi)), to(bc(zeta)), to(gc)


def _retention(proj, gn_w, b, s):
    chunk = min(RET_CHUNK, s)
    decay, xi, zeta, gc = _retention_constants(chunk)
    hp = RET_HEADS_PER_STEP
    w = hp * HEAD_DIM
    head = lambda col: pl.BlockSpec((s, w), lambda i, j: (i, col // hp + j))
    per_head = lambda shape: pl.BlockSpec((hp,) + shape, lambda i, j: (j, 0, 0))
    return pl.pallas_call(
        functools.partial(_ret_kernel, chunk=chunk),
        grid=(b, RET_HEADS // hp),
        in_specs=[head(COL_RQ), head(COL_RK), head(COL_RV), head(COL_RG),
                  per_head((chunk, chunk)), per_head((chunk, HEAD_DIM)), per_head((chunk, HEAD_DIM)),
                  per_head((1, HEAD_DIM)),
                  pl.BlockSpec((1, w), lambda i, j: (0, j))],
        out_specs=pl.BlockSpec((s, w), lambda i, j: (i, j)),
        out_shape=jax.ShapeDtypeStruct((b * s, RET_WIDTH), bf16),
        compiler_params=_cparams(("parallel", "parallel")),
        name="retention",
    )(proj, proj, proj, proj, decay, xi, zeta, gc, gn_w)


def _branch_merge_kernel(fo_ref, ro_ref, g0_ref, g1_ref, bg_ref, wb_ref, mg_ref, *, ncol):
    d = mg_ref.shape[1]
    cw = d // ncol
    fo = fo_ref[...]
    ro = ro_ref[...]
    for c in range(ncol):
        cs = slice(c * cw, (c + 1) * cw)
        bd0 = jnp.dot(fo, wb_ref[0, :, cs], preferred_element_type=f32)
        bd1 = jnp.dot(ro, wb_ref[1, :, cs], preferred_element_type=f32)
        ga = jax.nn.sigmoid(g0_ref[:, cs].astype(f32) + bg_ref[0:1, cs])
        gb = jax.nn.sigmoid(g1_ref[:, cs].astype(f32) + bg_ref[1:2, cs])
        mg_ref[:, cs] = (ga * bd0 + gb * bd1).astype(mg_ref.dtype)


def _branch_merge(fox_o, ret_o, proj, b_gate, w_branch):
    t = fox_o.shape[0]
    d = D_MODEL
    tm = min(MERGE_ROWS, t)
    gate0 = COL_GATE * HEAD_DIM // d
    return pl.pallas_call(
        functools.partial(_branch_merge_kernel, ncol=4),
        grid=(t // tm,),
        in_specs=[pl.BlockSpec((tm, FOX_WIDTH), lambda i: (i, 0)),
                  pl.BlockSpec((tm, RET_WIDTH), lambda i: (i, 0)),
                  pl.BlockSpec((tm, d), lambda i: (i, gate0)),
                  pl.BlockSpec((tm, d), lambda i: (i, gate0 + 1)),
                  _const_spec((N_BRANCH, d)),
                  _const_spec((N_BRANCH, FOX_WIDTH, d))],
        out_specs=pl.BlockSpec((tm, d), lambda i: (i, 0)),
        out_shape=jax.ShapeDtypeStruct((t, d), bf16),
        compiler_params=_cparams(("parallel",)),
        name="branch_merge",
    )(fox_o, ret_o, proj, proj, b_gate, w_branch)


def _route(logits):
    lane = lax.broadcasted_iota(jnp.int32, logits.shape, 1).astype(f32)
    far = jnp.float32(1e9)
    gmask = lane < N_GROUPS
    gl = jnp.where(gmask, logits, NEG_BIG)
    gmax = jnp.max(gl, axis=-1, keepdims=True)
    gsel = jnp.min(jnp.where(gl == gmax, lane, far), axis=-1, keepdims=True)
    p_group = 1.0 / jnp.sum(jnp.where(gmask, jnp.exp(gl - gmax), 0.0), axis=-1, keepdims=True)
    lo = N_GROUPS + EXPERTS_PER_GROUP * gsel
    emask = (lane >= lo) & (lane < lo + EXPERTS_PER_GROUP)
    el = jnp.where(emask, logits, NEG_BIG)
    m1 = jnp.max(el, axis=-1, keepdims=True)
    i1 = jnp.min(jnp.where(el == m1, lane, far), axis=-1, keepdims=True)
    el2 = jnp.where(lane == i1, NEG_BIG, el)
    m2 = jnp.max(el2, axis=-1, keepdims=True)
    i2 = jnp.min(jnp.where(el2 == m2, lane, far), axis=-1, keepdims=True)
    t = jnp.exp(m2 - m1)
    w1 = p_group / (1.0 + t)
    w2 = p_group * t / (1.0 + t)
    out = jnp.where(lane == 0, i1 - N_GROUPS, 0.0)
    out = jnp.where(lane == 1, i2 - N_GROUPS, out)
    out = jnp.where(lane == 2, w1, out)
    out = jnp.where(lane == 3, w2, out)
    return out


def _outproj_kernel(mg_ref, x_ref, wo_ref, nfw_ref, wr_ref, br_ref, h_ref, hs_ref, route_ref, cnt_ref):
    h = x_ref[...] + jnp.dot(mg_ref[...], wo_ref[...], preferred_element_type=f32)
    h_ref[...] = h
    hn = h * lax.rsqrt(jnp.mean(h * h, axis=-1, keepdims=True) + EPS) * nfw_ref[...]
    _rows_to_slabs(hs_ref, hn, hn.shape[0])
    hi = hn.astype(bf16)
    lo = (hn - hi.astype(f32)).astype(bf16)
    hw = jnp.dot(hi, wr_ref[...], preferred_element_type=f32)
    logits = (hw[:, :LANES] + jnp.dot(lo, wr_ref[:, :LANES], preferred_element_type=f32) + hw[:, LANES:]) + br_ref[...]
    route = _route(logits)
    route_ref[...] = route

    @pl.when(pl.program_id(0) == 0)
    def _():
        cnt_ref[...] = jnp.zeros_like(cnt_ref)

    lane = lax.broadcasted_iota(jnp.int32, route.shape, 1).astype(f32)
    chosen = (lane == route[:, 0:1]).astype(f32) + (lane == route[:, 1:2]).astype(f32)
    cnt_ref[...] += jnp.sum(chosen, axis=0, keepdims=True)


def _outproj_router(merged, x2, w_out, nfw, wr_cat, b_router):
    t, d = x2.shape
    tm = min(OUTPROJ_ROWS, t)
    return pl.pallas_call(
        _outproj_kernel,
        grid=(t // tm,),
        in_specs=[pl.BlockSpec((tm, d), lambda i: (i, 0)),
                  pl.BlockSpec((tm, d), lambda i: (i, 0)),
                  _const_spec((d, d)),
                  _const_spec((1, d)),
                  _const_spec((d, 2 * LANES)), _const_spec((1, LANES))],
        out_specs=[pl.BlockSpec((tm, d), lambda i: (i, 0)),
                   pl.BlockSpec((tm * PITCH, LANES), lambda i: (i, 0)),
                   pl.BlockSpec((tm, LANES), lambda i: (i, 0)),
                   pl.BlockSpec((8, LANES), lambda i: (0, 0))],
        out_shape=[jax.ShapeDtypeStruct((t, d), f32), jax.ShapeDtypeStruct((t * PITCH, LANES), u32),
                   jax.ShapeDtypeStruct((t, LANES), f32), jax.ShapeDtypeStruct((8, LANES), f32)],
        compiler_params=_cparams(("arbitrary",)),
        name="outproj_router",
    )(merged, x2, w_out, nfw, wr_cat, b_router)


def _plan_kernel(route_ref, cnt_ref, dest_ref, ends_ref, carry_ref, *, bm):
    r = route_ref[...]
    tb = r.shape[0]
    lane = lax.broadcasted_iota(jnp.int32, r.shape, 1).astype(f32)
    o0 = (lane == r[:, 0:1]).astype(f32)
    o1 = (lane == r[:, 1:2]).astype(f32)
    o = o0 + o1

    @pl.when(pl.program_id(0) == 0)
    def _():
        cnt = cnt_ref[...]
        padded = jnp.floor((cnt + (bm - 1)) * (1.0 / bm)) * bm
        lane8 = lax.broadcasted_iota(jnp.int32, cnt.shape, 1)
        ends = padded
        sh = 1
        while sh < LANES:
            ends = ends + jnp.where(lane8 >= sh, pltpu.roll(ends, sh, axis=1), 0.0)
            sh *= 2
        sub = lax.broadcasted_iota(jnp.int32, cnt.shape, 0)
        ends_ref[...] = jnp.where(sub == 0, ends, jnp.where(sub == 1, padded, cnt))
        carry_ref[...] = ends - padded

    row = lax.broadcasted_iota(jnp.int32, (tb, tb), 0)
    col = lax.broadcasted_iota(jnp.int32, (tb, tb), 1)
    lower = jnp.where(col < row, 1.0, 0.0).astype(bf16)
    before = jnp.dot(lower, o.astype(bf16), preferred_element_type=f32) + carry_ref[0:1, :]
    d0 = jnp.sum(before * o0, axis=-1, keepdims=True)
    d1 = jnp.sum(before * o1, axis=-1, keepdims=True)
    dd = jnp.where(lane == 0, d0, jnp.where(lane == 1, d1, 0.0))
    dest_ref[0] = dd.T[:8, :].astype(jnp.int32)
    carry_ref[...] += jnp.sum(o, axis=0, keepdims=True)


def _dispatch_plan(route, counts, bm):
    t = route.shape[0]
    tb = min(PLAN_ROWS, t)
    nb = t // tb
    return pl.pallas_call(
        functools.partial(_plan_kernel, bm=bm),
        grid=(nb,),
        in_specs=[pl.BlockSpec((tb, LANES), lambda i: (i, 0)),
                  pl.BlockSpec((8, LANES), lambda i: (0, 0))],
        out_specs=[pl.BlockSpec((1, 8, tb), lambda i: (i, 0, 0)),
                   pl.BlockSpec((8, LANES), lambda i: (0, 0))],
        out_shape=[jax.ShapeDtypeStruct((nb, 8, tb), jnp.int32), jax.ShapeDtypeStruct((8, LANES), f32)],
        scratch_shapes=[pltpu.VMEM((8, LANES), f32)],
        compiler_params=_cparams(("arbitrary",)),
        name="dispatch_plan",
    )(route, counts)


def _dispatch_kernel(d0_ref, d1_ref, fill_ref, end_ref, nu_ref, hs_ref, xs_hbm, zbuf, sem, zsem,
                     *, tb, bm, n_blk, unroll=8):
    i = pl.program_id(0)
    base = i * tb
    blk_rows = bm * PITCH

    def pad_copy(s, n):
        return pltpu.make_async_copy(zbuf.at[pl.ds(0, n * PITCH), :], xs_hbm.at[pl.ds(s * PITCH, n * PITCH), :], zsem)

    def tail_copy(g):
        return pltpu.make_async_copy(zbuf, xs_hbm.at[pl.ds(pl.multiple_of(g * blk_rows, 8), blk_rows), :], zsem)

    @pl.when(i == 0)
    def _():
        zbuf[...] = jnp.zeros(zbuf.shape, u32)
        for op in ("start", "wait"):
            def per_expert(e, c, op=op):
                s = fill_ref[e]
                for n in PAD_FILL_CHUNKS:
                    count = (end_ref[e] - s) // n

                    def one(k, c2, s=s, n=n):
                        getattr(pad_copy(s + k * n, n), op)()
                        return c2

                    c = lax.fori_loop(0, count, one, c)
                    s = s + count * n
                return c

            def per_tail(g, c, op=op):
                getattr(tail_copy(g), op)()
                return c

            lax.fori_loop(0, N_EXPERTS, per_expert, 0)
            lax.fori_loop(nu_ref[0], n_blk, per_tail, 0)

    def body(g, c):
        for u in range(unroll):
            r = g * unroll + u
            src = hs_ref.at[pl.ds(r * PITCH, PITCH), :]
            for k, dref in enumerate((d0_ref, d1_ref)):
                d = dref[base + r]
                pltpu.make_async_copy(src, xs_hbm.at[pl.ds(d * PITCH, PITCH), :], sem).start(priority=k)
        return c

    lax.fori_loop(0, tb // unroll, body, 0)
    for _ in range(TOP_K):
        pltpu.make_async_copy(xs_hbm.at[pl.ds(0, tb * PITCH), :], xs_hbm.at[pl.ds(0, tb * PITCH), :], sem).wait()


def _dispatch_rows(dest0, dest1, fill_start, seg_end, n_used, hs, cap):
    t = dest0.shape[0]
    tb = min(DISPATCH_ROWS, t)
    bm = MOE_BLOCK
    grid_spec = pltpu.PrefetchScalarGridSpec(
        num_scalar_prefetch=5,
        grid=(t // tb,),
        in_specs=[pl.BlockSpec((tb * PITCH, LANES), lambda i, *_: (i, 0))],
        out_specs=pl.BlockSpec(memory_space=pl.ANY),
        scratch_shapes=[pltpu.VMEM((bm * PITCH, LANES), u32), pltpu.SemaphoreType.DMA(()),
                        pltpu.SemaphoreType.DMA(())],
    )
    return pl.pallas_call(
        functools.partial(_dispatch_kernel, tb=tb, bm=bm, n_blk=cap // bm),
        grid_spec=grid_spec,
        out_shape=jax.ShapeDtypeStruct((cap * PITCH, LANES), u32),
        compiler_params=_cparams(("arbitrary",)),
        name="dispatch_rows",
    )(dest0, dest1, fill_start, seg_end, n_used, hs)


def _moe_kernel(g0_ref, nb_ref, nu_ref, x_hbm, w1_ref, w3_ref, w2_ref, y_hbm,
                xbuf, ybuf, w1b, w3b, w2b, gsem, ysem, *, bm, n_blk):
    e = pl.program_id(0)
    n_used = nu_ref[0]
    g0 = g0_ref[e]
    nb = nb_ref[e]
    blk_rows = bm * PITCH

    def x_copy(slot, g):
        src = x_hbm.at[pl.ds(pl.multiple_of(g * blk_rows, 8), blk_rows), :]
        return pltpu.make_async_copy(src, xbuf.at[slot], gsem.at[slot])

    def y_copy(slot, g):
        dst = y_hbm.at[pl.ds(pl.multiple_of(g * blk_rows, 8), blk_rows), :]
        return pltpu.make_async_copy(ybuf.at[slot], dst, ysem.at[slot])

    @pl.when(e == 0)
    def _():
        x_copy(0, 0).start()
        for k in range(1, X_BUFS - 1):
            pl.when(n_used > k)(lambda k=k: x_copy(k, k).start())

    @pl.when(nb > 0)
    def _():
        w1b[...] = w1_ref[0].astype(bf16)
        w3b[...] = w3_ref[0].astype(bf16)
        w2b[...] = w2_ref[0].astype(bf16)

    def block(c, carry):
        g = g0 + c
        slot = g % 2
        xslot = g % X_BUFS

        ahead = g + (X_BUFS - 1)

        @pl.when(ahead < n_used)
        def _():
            x_copy(ahead % X_BUFS, ahead).start()

        x_copy(xslot, g).wait()

        hn = _slabs_to_rows(xbuf.at[xslot], bm).astype(bf16)
        a = jnp.dot(hn, w1b[...], preferred_element_type=f32)
        b = jnp.dot(hn, w3b[...], preferred_element_type=f32)
        mid = (a * jax.nn.sigmoid(a) * b).astype(bf16)
        y = jnp.dot(mid, w2b[...], preferred_element_type=f32)

        @pl.when(g >= 2)
        def _():
            y_copy(slot, g - 2).wait()

        _rows_to_slabs(ybuf.at[slot], y, bm)
        y_copy(slot, g).start()
        return carry

    lax.fori_loop(0, nb, block, 0)

    @pl.when(e == pl.num_programs(0) - 1)
    def _():
        @pl.when(n_used >= 2)
        def _():
            y_copy(n_used % 2, n_used - 2).wait()

        y_copy((n_used - 1) % 2, n_used - 1).wait()
        ybuf[0] = jnp.zeros(ybuf.shape[1:], u32)

        def zstart(g, carry):
            y_copy(0, g).start()
            return carry

        def zwait(g, carry):
            y_copy(0, g).wait()
            return carry

        lax.fori_loop(n_used, n_blk, zstart, 0)
        lax.fori_loop(n_used, n_blk, zwait, 0)


def _expert_mlp(blk_start, blk_count, n_used, xs, w1, w3, w2, cap):
    d = D_MODEL
    bm = MOE_BLOCK
    n_blk = cap // bm
    by_expert = lambda shape: pl.BlockSpec(shape, lambda e, g0, nb, nu: (e, 0, 0))
    grid_spec = pltpu.PrefetchScalarGridSpec(
        num_scalar_prefetch=3,
        grid=(N_EXPERTS,),
        in_specs=[pl.BlockSpec(memory_space=pl.ANY),
                  by_expert((1, d, D_EXPERT)), by_expert((1, d, D_EXPERT)), by_expert((1, D_EXPERT, d))],
        out_specs=pl.BlockSpec(memory_space=pl.ANY),
        scratch_shapes=[pltpu.VMEM((X_BUFS, bm * PITCH, LANES), u32), pltpu.VMEM((2, bm * PITCH, LANES), u32),
                        pltpu.VMEM((d, D_EXPERT), bf16), pltpu.VMEM((d, D_EXPERT), bf16),
                        pltpu.VMEM((D_EXPERT, d), bf16),
                        pltpu.SemaphoreType.DMA((X_BUFS,)), pltpu.SemaphoreType.DMA((2,))],
    )
    return pl.pallas_call(
        functools.partial(_moe_kernel, bm=bm, n_blk=n_blk),
        grid_spec=grid_spec,
        out_shape=jax.ShapeDtypeStruct((cap * PITCH, LANES), u32),
        compiler_params=_cparams(("arbitrary",)),
        name="expert_mlp",
    )(blk_start, blk_count, n_used, xs, w1, w3, w2)


def _combine_kernel(d0_ref, d1_ref, h_ref, route_ref, y_hbm, nw_ref, o_ref, ybuf, sem, *, tb):
    i = pl.program_id(0)
    n = pl.num_programs(0)
    slot = i % 2

    def start(blk, s):
        _slab_gather_start(d0_ref, blk * tb, tb, y_hbm, ybuf.at[s, 0], sem.at[s])
        _slab_gather_start(d1_ref, blk * tb, tb, y_hbm, ybuf.at[s, 1], sem.at[s])

    @pl.when(i == 0)
    def _():
        start(0, 0)

    @pl.when(i + 1 < n)
    def _():
        start(i + 1, 1 - slot)

    for k in range(TOP_K):
        _slab_gather_wait(tb, y_hbm, ybuf.at[slot, k], sem.at[slot])
    r = route_ref[...]
    h = h_ref[...] + (r[:, 2:3] * _slabs_to_rows(ybuf.at[slot, 0], tb)
                      + r[:, 3:4] * _slabs_to_rows(ybuf.at[slot, 1], tb))
    o_ref[...] = h * lax.rsqrt(jnp.mean(h * h, axis=-1, keepdims=True) + EPS) * nw_ref[...]


def _combine(dest0, dest1, h, route, ys, norm_w):
    t, d = h.shape
    tb = min(COMBINE_ROWS, t)
    grid_spec = pltpu.PrefetchScalarGridSpec(
        num_scalar_prefetch=2,
        grid=(t // tb,),
        in_specs=[pl.BlockSpec((tb, d), lambda i, a, b: (i, 0)),
                  pl.BlockSpec((tb, LANES), lambda i, a, b: (i, 0)),
                  pl.BlockSpec(memory_space=pl.ANY),
                  pl.BlockSpec((1, d), lambda i, a, b: (0, 0))],
        out_specs=pl.BlockSpec((tb, d), lambda i, a, b: (i, 0)),
        scratch_shapes=[pltpu.VMEM((2, TOP_K, tb * PITCH, LANES), u32), pltpu.SemaphoreType.DMA((2,))],
    )
    return pl.pallas_call(
        functools.partial(_combine_kernel, tb=tb),
        grid_spec=grid_spec,
        out_shape=jax.ShapeDtypeStruct((t, d), f32),
        compiler_params=_cparams(("arbitrary",)),
        name="combine_final_norm",
    )(dest0, dest1, h, route, ys, norm_w)


def kernel(x, positions, norm_mix_w, w_in, fox_b_f, ret_gn_w, w_branch, b_gate, w_out, norm_ffn_w, w_router_group,
           b_router_group, w_router_expert, b_router_expert, w1, w3, w2, norm_final_w):
    b, s, d = x.shape
    assert d == D_MODEL and norm_mix_w.shape[0] == 1
    t = b * s
    x2 = x.reshape(t, d)

    wt = w_in[0].T
    o_f = 3 * FOX_WIDTH
    o_r = o_f + FOX_HEADS
    o_g = o_r + 4 * RET_WIDTH
    tn = PROJ_COLS
    tile_rows = ([o_g + k * tn for k in range(N_BRANCH * D_MODEL // tn)]
                 + [k * tn for k in range(o_f // tn)]
                 + [o_r + k * tn for k in range(4 * RET_WIDTH // tn)])
    w_main = _prep_in_weights(wt, tile_rows)
    w_f = jnp.pad(wt[o_f:o_r], ((0, LANES - FOX_HEADS), (0, 0))).astype(bf16)
    bias_f = jnp.pad(fox_b_f[0], (0, LANES - FOX_HEADS)).reshape(1, LANES)
    w_r = jnp.concatenate([w_router_group[0], w_router_expert[0]], axis=1)
    w_r = jnp.pad(w_r, ((0, 0), (0, LANES - w_r.shape[1])))
    wr_hi = w_r.astype(bf16)
    wr_lo = (w_r - wr_hi.astype(f32)).astype(bf16)
    b_r = jnp.pad(jnp.concatenate([b_router_group[0], b_router_expert[0]]), (0, LANES - N_GROUPS - N_EXPERTS))
    b_r = b_r.reshape(1, LANES)

    cos, sin = _rope_tables(positions)
    proj, f_logit = _in_projection(x2, norm_mix_w, w_main, w_f, cos.reshape(t, HEAD_DIM), sin.reshape(t, HEAD_DIM))
    c = _forget_cumsum(f_logit, bias_f, b, s)
    fox_o = _fox_attention(proj, c.reshape(b * FOX_HEADS, 1, s), b, s)
    ret_o = _retention(proj, ret_gn_w, b, s)
    merged = _branch_merge(fox_o, ret_o, proj, b_gate[0], w_branch[0].astype(bf16))
    h, h_slabs, route, counts = _outproj_router(merged, x2, w_out[0].astype(bf16), norm_ffn_w,
                                        jnp.concatenate([wr_hi, wr_lo], axis=1), b_r)

    bm = MOE_BLOCK
    cap = t * TOP_K + N_EXPERTS * bm
    dest_rows, seg = _dispatch_plan(route, counts, bm)
    dest0 = dest_rows[:, 0, :].reshape(t)
    dest1 = dest_rows[:, 1, :].reshape(t)
    pends = seg[0, :N_EXPERTS].astype(jnp.int32)
    padded = seg[1, :N_EXPERTS].astype(jnp.int32)
    counts = seg[2, :N_EXPERTS].astype(jnp.int32)
    pstarts = pends - padded
    n_used = pends[-1:] // bm

    xs = _dispatch_rows(dest0, dest1, pstarts + counts, pends, n_used, h_slabs, cap)
    y = _expert_mlp(pstarts // bm, padded // bm, n_used, xs, w1[0], w3[0], w2[0], cap)
    out = _combine(dest0, dest1, h, route, y, norm_final_w.reshape(1, d))
    return out.reshape(b, s, d)
```

```python
import functools
import math

import numpy as np
import jax
import jax.numpy as jnp
from jax import lax
from jax.experimental import pallas as pl
from jax.experimental.pallas import tpu as pltpu

D_MODEL = 2048
HEAD_DIM = 128
FOX_HEADS = 8
RET_HEADS = 8
FOX_WIDTH = FOX_HEADS * HEAD_DIM
RET_WIDTH = RET_HEADS * HEAD_DIM
N_BRANCH = 2
N_GROUPS = 4
EXPERTS_PER_GROUP = 8
N_EXPERTS = N_GROUPS * EXPERTS_PER_GROUP
TOP_K = 2
D_EXPERT = 512
ROPE_BASE = 10000.0
EPS = 1e-6

LANES = 128
NEG_BIG = -1e30
LOG2E = math.log2(math.e)
COL_GATE = 0
COL_FQ = N_BRANCH * D_MODEL // HEAD_DIM
COL_FK = COL_FQ + FOX_HEADS
COL_FV = COL_FK + FOX_HEADS
COL_RQ = COL_FV + FOX_HEADS
COL_RK = COL_RQ + RET_HEADS
COL_RV = COL_RK + RET_HEADS
COL_RG = COL_RV + RET_HEADS

ATTN_BLOCK = 256
FOX_HEADS_PER_STEP = 4
RET_HEADS_PER_STEP = 4
RET_CHUNK = 256
MOE_BLOCK = 256
X_BUFS = 5
PAD_FILL_CHUNKS = (64, 8, 1)
PROJ_ROWS, PROJ_COLS = 1024, 1024
MERGE_ROWS = 1024
OUTPROJ_ROWS = 512
PLAN_ROWS = 1024
DISPATCH_ROWS = 2048
COMBINE_ROWS = 256
VMEM_LIMIT = 56 * 1024 * 1024

f32 = jnp.float32
bf16 = jnp.bfloat16
_NT = (((1,), (1,)), ((), ()))
SLAB = D_MODEL // (2 * LANES)
PITCH = SLAB + 1
u32 = jnp.uint32
_HI_MASK = np.uint32(0xFFFF0000)


def _rows_to_slabs(ref2, val, n_rows):
    bits = lambda v: lax.bitcast_convert_type(v.astype(bf16).astype(f32), u32)
    for j in range(SLAB):
        lo = val[:, (2 * j) * LANES:(2 * j + 1) * LANES]
        hi = val[:, (2 * j + 1) * LANES:(2 * j + 2) * LANES]
        ref2[pl.ds(j, n_rows, stride=PITCH), :] = (bits(lo) >> 16) | (bits(hi) & _HI_MASK)
    ref2[pl.ds(SLAB, n_rows, stride=PITCH), :] = jnp.zeros((n_rows, LANES), u32)


def _slabs_to_rows(ref2, n_rows):
    chunks = []
    for j in range(SLAB):
        w = ref2[pl.ds(j, n_rows, stride=PITCH), :]
        chunks.append(lax.bitcast_convert_type(w << 16, f32))
        chunks.append(lax.bitcast_convert_type(w & _HI_MASK, f32))
    return jnp.concatenate(chunks, axis=-1)


def _slab_gather_start(idx_ref, base, n_rows, src_hbm, dst, sem, unroll=8):
    def body(g, _):
        for u in range(unroll):
            r = g * unroll + u
            t = idx_ref[base + r]
            pltpu.make_async_copy(src_hbm.at[pl.ds(t * PITCH, SLAB), :], dst.at[pl.ds(r * PITCH, SLAB), :],
                                  sem).start(priority=u % 2)
        return 0
    lax.fori_loop(0, n_rows // unroll, body, 0)


def _slab_gather_wait(n_rows, src_hbm, dst, sem):
    pltpu.make_async_copy(src_hbm.at[pl.ds(0, n_rows * SLAB), :], dst.at[pl.ds(0, n_rows * SLAB), :], sem).wait()


def _cparams(sem, vmem=VMEM_LIMIT):
    return pltpu.CompilerParams(dimension_semantics=sem, vmem_limit_bytes=vmem)


def _const_spec(shape):
    return pl.BlockSpec(shape, lambda i: tuple(0 for _ in shape), pipeline_mode=pl.Buffered(1))


def _rope_kernel(pos_ref, freq_ref, sgn_ref, cos_ref, sin_ref):
    pos = pos_ref[0].astype(f32)
    hs = pos.shape[0] // 2
    half = HEAD_DIM // 2
    lane = lax.broadcasted_iota(jnp.int32, (hs, HEAD_DIM), 1)
    low = lane < half
    ang = jnp.where(low, pos[:hs], pos[hs:]) * freq_ref[...]
    c = jnp.cos(ang)
    s = jnp.sin(ang) * sgn_ref[...]
    c_sw = pltpu.roll(c, half, axis=1)
    s_sw = pltpu.roll(s, half, axis=1)
    cos_ref[0, :hs, :] = jnp.where(low, c, c_sw)
    cos_ref[0, hs:, :] = jnp.where(low, c_sw, c)
    sin_ref[0, :hs, :] = jnp.where(low, s, -s_sw)
    sin_ref[0, hs:, :] = jnp.where(low, -s_sw, s)


def _rope_tables(positions):
    b, s = positions.shape
    half = HEAD_DIM // 2
    inv = (np.float32(ROPE_BASE) ** (-np.arange(half, dtype=np.float32) / np.float32(half))).astype(np.float32)
    freq = jnp.asarray(np.concatenate([inv, inv])[None, :])
    sgn = jnp.asarray(np.concatenate([-np.ones(half, np.float32), np.ones(half, np.float32)])[None, :])
    out = jax.ShapeDtypeStruct((b, s, HEAD_DIM), f32)
    return pl.pallas_call(
        _rope_kernel,
        grid=(b,),
        in_specs=[pl.BlockSpec((1, s, 1), lambda i: (i, 0, 0)),
                  pl.BlockSpec((1, HEAD_DIM), lambda i: (0, 0)),
                  pl.BlockSpec((1, HEAD_DIM), lambda i: (0, 0))],
        out_specs=[pl.BlockSpec((1, s, HEAD_DIM), lambda i: (i, 0, 0)),
                   pl.BlockSpec((1, s, HEAD_DIM), lambda i: (i, 0, 0))],
        out_shape=[out, out],
        compiler_params=_cparams(("parallel",)),
        name="rope_tables",
    )(positions.reshape(b, s, 1), freq, sgn)


_Q_FACTORS = ((COL_FQ * HEAD_DIM // PROJ_COLS, LOG2E / math.sqrt(HEAD_DIM)),
              (COL_RQ * HEAD_DIM // PROJ_COLS, 1.0 / math.sqrt(HEAD_DIM)))


def _wprep_kernel(off_ref, w_hbm, o_ref, buf, sem, *, tn):
    j = pl.program_id(0)
    n = pl.num_programs(0)
    slot = j % 2

    def copy(jj, s):
        src = w_hbm.at[pl.ds(pl.multiple_of(off_ref[jj], 8), tn), :]
        return pltpu.make_async_copy(src, buf.at[s], sem.at[s])

    @pl.when(j == 0)
    def _():
        copy(0, 0).start()

    @pl.when(j + 1 < n)
    def _():
        copy(j + 1, 1 - slot).start()

    copy(j, slot).wait()
    factor = jnp.float32(1.0)
    for tile, value in _Q_FACTORS:
        factor = jnp.where(j == tile, jnp.float32(value), factor)
    o_ref[...] = (buf[slot] * factor).astype(o_ref.dtype)


def _prep_in_weights(wt, row_offsets):
    tn = PROJ_COLS
    d = wt.shape[1]
    n_tiles = len(row_offsets)
    grid_spec = pltpu.PrefetchScalarGridSpec(
        num_scalar_prefetch=1,
        grid=(n_tiles,),
        in_specs=[pl.BlockSpec(memory_space=pl.ANY)],
        out_specs=pl.BlockSpec((tn, d), lambda j, off: (j, 0)),
        scratch_shapes=[pltpu.VMEM((2, tn, d), f32), pltpu.SemaphoreType.DMA((2,))],
    )
    return pl.pallas_call(
        functools.partial(_wprep_kernel, tn=tn),
        grid_spec=grid_spec,
        out_shape=jax.ShapeDtypeStruct((n_tiles * tn, d), bf16),
        compiler_params=_cparams(("arbitrary",)),
        name="prep_in_weights",
    )(jnp.asarray(row_offsets, jnp.int32), wt)


_ROT_TILES = (COL_RQ * HEAD_DIM // PROJ_COLS, COL_RK * HEAD_DIM // PROJ_COLS)


def _inproj_kernel(x_ref, nw_ref, w_ref, wf_ref, cos_ref, sin_ref, o_ref, f_ref, xn_ref):
    j = pl.program_id(1)

    @pl.when(j == 0)
    def _():
        x = x_ref[...]
        ms = jnp.mean(x * x, axis=-1, keepdims=True)
        xn = (x * lax.rsqrt(ms + EPS) * nw_ref[...]).astype(bf16)
        xn_ref[...] = xn
        f_ref[...] = lax.dot_general(xn, wf_ref[...], _NT, preferred_element_type=f32)

    rotated = (j == _ROT_TILES[0]) | (j == _ROT_TILES[1])

    @pl.when(jnp.logical_not(rotated))
    def _():
        o_ref[...] = lax.dot_general(xn_ref[...], w_ref[...], _NT, preferred_element_type=f32).astype(o_ref.dtype)

    @pl.when(rotated)
    def _():
        acc = lax.dot_general(xn_ref[...], w_ref[...], _NT, preferred_element_type=f32)
        cos = cos_ref[...]
        sin = sin_ref[...]
        for hh in range(acc.shape[1] // HEAD_DIM):
            lanes = slice(hh * HEAD_DIM, (hh + 1) * HEAD_DIM)
            t = acc[:, lanes]
            o_ref[:, lanes] = (t * cos + pltpu.roll(t, HEAD_DIM // 2, axis=1) * sin).astype(o_ref.dtype)


def _in_projection(x2, norm_w, w_main_t, w_f_t, cos, sin):
    t, d = x2.shape
    n = w_main_t.shape[0]
    tm = min(PROJ_ROWS, t)
    tn = PROJ_COLS
    return pl.pallas_call(
        _inproj_kernel,
        grid=(t // tm, n // tn),
        in_specs=[pl.BlockSpec((tm, d), lambda i, j: (i, 0)),
                  pl.BlockSpec((1, d), lambda i, j: (0, 0)),
                  pl.BlockSpec((tn, d), lambda i, j: (j, 0)),
                  pl.BlockSpec((LANES, d), lambda i, j: (0, 0)),
                  pl.BlockSpec((tm, HEAD_DIM), lambda i, j: (i, 0)),
                  pl.BlockSpec((tm, HEAD_DIM), lambda i, j: (i, 0))],
        out_specs=[pl.BlockSpec((tm, tn), lambda i, j: (i, j)),
                   pl.BlockSpec((tm, LANES), lambda i, j: (i, 0))],
        out_shape=[jax.ShapeDtypeStruct((t, n), bf16), jax.ShapeDtypeStruct((t, LANES), f32)],
        scratch_shapes=[pltpu.VMEM((tm, d), bf16)],
        compiler_params=_cparams(("parallel", "arbitrary")),
        name="in_projection",
    )(x2, norm_w, w_main_t, w_f_t, cos, sin)


def _fcum_kernel(f_ref, b_ref, c_ref):
    z = f_ref[...] + b_ref[...]
    ls = jnp.minimum(z, 0.0) - jnp.log1p(jnp.exp(-jnp.abs(z)))
    c = ls.T[:FOX_HEADS, :]
    s = c.shape[1]
    lane = lax.broadcasted_iota(jnp.int32, c.shape, 1)
    sh = 1
    while sh < s:
        c = c + jnp.where(lane >= sh, pltpu.roll(c, sh, axis=1), 0.0)
        sh *= 2
    c_ref[0] = c * LOG2E


def _forget_cumsum(f_logit, bias, b, s):
    return pl.pallas_call(
        _fcum_kernel,
        grid=(b,),
        in_specs=[pl.BlockSpec((s, LANES), lambda i: (i, 0)),
                  pl.BlockSpec((1, LANES), lambda i: (0, 0))],
        out_specs=pl.BlockSpec((1, FOX_HEADS, s), lambda i: (i, 0, 0)),
        out_shape=jax.ShapeDtypeStruct((b, FOX_HEADS, s), f32),
        compiler_params=_cparams(("parallel",)),
        name="forget_cumsum",
    )(f_logit, bias)


def _fox_kernel(q_ref, k_ref, v_ref, c_ref, o_ref, va_ref, acc_ref, *, tq):
    s_len = q_ref.shape[0]
    nq = s_len // tq
    n_heads = q_ref.shape[1] // HEAD_DIM
    row = lax.broadcasted_iota(jnp.int32, (tq, tq), 0)
    col = lax.broadcasted_iota(jnp.int32, (tq, tq), 1)
    causal = row >= col
    def scores(hh):
        lanes = slice(hh * HEAD_DIM, (hh + 1) * HEAD_DIM)
        va_ref[hh, :, :HEAD_DIM] = v_ref[:, lanes]
        va_ref[hh, :, HEAD_DIM:] = jnp.ones((s_len, HEAD_DIM), va_ref.dtype)
        panels = []
        row_max = [None] * nq
        for kb in range(nq):
            r0 = kb * tq
            s = lax.dot_general(q_ref[r0:, lanes], k_ref[r0:r0 + tq, lanes], _NT, preferred_element_type=f32)
            s = s - c_ref[hh, :, r0:r0 + tq]
            parts = [jnp.where(causal, s[:tq], NEG_BIG)]
            parts += [s[(j - kb) * tq:(j - kb + 1) * tq] for j in range(kb + 1, nq)]
            panels.append(parts)
            for j, part in zip(range(kb, nq), parts):
                pm = jnp.max(part, axis=-1, keepdims=True)
                row_max[j] = pm if row_max[j] is None else jnp.maximum(row_max[j], pm)
        return panels, row_max

    def values(hh, panels, row_max):
        lanes = slice(hh * HEAD_DIM, (hh + 1) * HEAD_DIM)
        for kb in range(nq):
            r0 = kb * tq
            p = jnp.concatenate([jnp.exp2(part - row_max[j]).astype(bf16)
                                 for j, part in zip(range(kb, nq), panels[kb])], axis=0)
            contrib = jnp.dot(p, va_ref[hh, r0:r0 + tq, :], preferred_element_type=f32)
            if kb:
                acc_ref[hh, r0:, :] += contrib
            else:
                acc_ref[hh] = contrib
        acc = acc_ref[hh]
        o_ref[:, lanes] = (acc[:, :HEAD_DIM] / acc[:, HEAD_DIM:]).astype(o_ref.dtype)

    pending = None
    for hh in range(n_heads):
        current = (hh,) + scores(hh)
        if pending is not None:
            values(*pending)
        pending = current
    values(*pending)


def _fox_attention(proj, c_blk, b, s):
    tq = min(ATTN_BLOCK, s)
    hp = FOX_HEADS_PER_STEP
    w = hp * HEAD_DIM
    col = lambda c0: pl.BlockSpec((s, w), lambda i, j: (i, c0 // hp + j))
    return pl.pallas_call(
        functools.partial(_fox_kernel, tq=tq),
        grid=(b, FOX_HEADS // hp),
        in_specs=[col(COL_FQ), col(COL_FK), col(COL_FV),
                  pl.BlockSpec((hp, 1, s), lambda i, j: (i * (FOX_HEADS // hp) + j, 0, 0))],
        out_specs=pl.BlockSpec((s, w), lambda i, j: (i, j)),
        out_shape=jax.ShapeDtypeStruct((b * s, FOX_WIDTH), bf16),
        scratch_shapes=[pltpu.VMEM((hp, s, 2 * HEAD_DIM), bf16), pltpu.VMEM((hp, s, 2 * HEAD_DIM), f32)],
        compiler_params=_cparams(("parallel", "parallel")),
        name="fox_attention",
    )(proj, proj, proj, c_blk)


def _ret_kernel(q_ref, k_ref, v_ref, g_ref, dec_ref, xi_ref, zeta_ref, gc_ref, gnw_ref, o_ref, *, chunk):
    s_len = q_ref.shape[0]
    nc = s_len // chunk
    n_heads = q_ref.shape[1] // HEAD_DIM
    states = [None] * n_heads
    for ci in range(nc):
        rows = slice(ci * chunk, (ci + 1) * chunk)
        for hh in range(n_heads):
            lanes = slice(hh * HEAD_DIM, (hh + 1) * HEAD_DIM)
            state = states[hh]
            qb = q_ref[rows, lanes]
            kb = k_ref[rows, lanes]
            v = v_ref[rows, lanes]
            inner = lax.dot_general(qb, kb, _NT, preferred_element_type=f32) * dec_ref[hh]
            o = jnp.dot(inner.astype(bf16), v, preferred_element_type=f32)
            if ci:
                o = o + jnp.dot(qb, state.astype(bf16), preferred_element_type=f32) * xi_ref[hh]
            if ci + 1 < nc:
                kz = (kb.astype(f32) * zeta_ref[hh]).astype(bf16)
                kv = lax.dot_general(kz, v, (((0,), (0,)), ((), ())), preferred_element_type=f32)
                states[hh] = state * gc_ref[hh] + kv if ci else kv
            mu = jnp.mean(o, axis=-1, keepdims=True)
            oc = o - mu
            var = jnp.mean(oc * oc, axis=-1, keepdims=True)
            y = oc * lax.rsqrt(var + EPS) * gnw_ref[:, lanes]
            g = g_ref[rows, lanes].astype(f32)
            o_ref[rows, lanes] = (g * jax.nn.sigmoid(g) * y).astype(o_ref.dtype)


def _retention_constants(chunk):
    hh = np.arange(RET_HEADS, dtype=np.float64)
    log_gamma = np.log1p(-(2.0 ** (-5.0 - hh)))
    n = np.arange(chunk, dtype=np.float64)
    diff = n[:, None] - n[None, :]
    decay = np.where(diff[None] >= 0, np.exp(diff[None] * log_gamma[:, None, None]), 0.0)
    xi = np.exp((n[None, :] + 1.0) * log_gamma[:, None])
    zeta = np.exp((chunk - 1.0 - n[None, :]) * log_gamma[:, None])
    g_chunk = np.exp(chunk * log_gamma)
    bc = lambda a: np.broadcast_to(a[:, :, None], (RET_HEADS, chunk, HEAD_DIM))
    gc = np.broadcast_to(g_chunk[:, None, None], (RET_HEADS, 1, HEAD_DIM))
    to = lambda a: jnp.asarray(np.ascontiguousarray(a), dtype=f32)
    return to(decay), to(bc(xi)), to(bc(zeta)), to(gc)


def _retention(proj, gn_w, b, s):
    chunk = min(RET_CHUNK, s)
    decay, xi, zeta, gc = _retention_constants(chunk)
    hp = RET_HEADS_PER_STEP
    w = hp * HEAD_DIM
    head = lambda col: pl.BlockSpec((s, w), lambda i, j: (i, col // hp + j))
    per_head = lambda shape: pl.BlockSpec((hp,) + shape, lambda i, j: (j, 0, 0))
    return pl.pallas_call(
        functools.partial(_ret_kernel, chunk=chunk),
        grid=(b, RET_HEADS // hp),
        in_specs=[head(COL_RQ), head(COL_RK), head(COL_RV), head(COL_RG),
                  per_head((chunk, chunk)), per_head((chunk, HEAD_DIM)), per_head((chunk, HEAD_DIM)),
                  per_head((1, HEAD_DIM)),
                  pl.BlockSpec((1, w), lambda i, j: (0, j))],
        out_specs=pl.BlockSpec((s, w), lambda i, j: (i, j)),
        out_shape=jax.ShapeDtypeStruct((b * s, RET_WIDTH), bf16),
        compiler_params=_cparams(("parallel", "parallel")),
        name="retention",
    )(proj, proj, proj, proj, decay, xi, zeta, gc, gn_w)


def _branch_merge_kernel(fo_ref, ro_ref, g0_ref, g1_ref, bg_ref, wb_ref, mg_ref, *, ncol):
    d = mg_ref.shape[1]
    cw = d // ncol
    fo = fo_ref[...]
    ro = ro_ref[...]
    for c in range(ncol):
        cs = slice(c * cw, (c + 1) * cw)
        bd0 = jnp.dot(fo, wb_ref[0, :, cs], preferred_element_type=f32)
        bd1 = jnp.dot(ro, wb_ref[1, :, cs], preferred_element_type=f32)
        ga = jax.nn.sigmoid(g0_ref[:, cs].astype(f32) + bg_ref[0:1, cs])
        gb = jax.nn.sigmoid(g1_ref[:, cs].astype(f32) + bg_ref[1:2, cs])
        mg_ref[:, cs] = (ga * bd0 + gb * bd1).astype(mg_ref.dtype)


def _branch_merge(fox_o, ret_o, proj, b_gate, w_branch):
    t = fox_o.shape[0]
    d = D_MODEL
    tm = min(MERGE_ROWS, t)
    gate0 = COL_GATE * HEAD_DIM // d
    return pl.pallas_call(
        functools.partial(_branch_merge_kernel, ncol=4),
        grid=(t // tm,),
        in_specs=[pl.BlockSpec((tm, FOX_WIDTH), lambda i: (i, 0)),
                  pl.BlockSpec((tm, RET_WIDTH), lambda i: (i, 0)),
                  pl.BlockSpec((tm, d), lambda i: (i, gate0)),
                  pl.BlockSpec((tm, d), lambda i: (i, gate0 + 1)),
                  _const_spec((N_BRANCH, d)),
                  _const_spec((N_BRANCH, FOX_WIDTH, d))],
        out_specs=pl.BlockSpec((tm, d), lambda i: (i, 0)),
        out_shape=jax.ShapeDtypeStruct((t, d), bf16),
        compiler_params=_cparams(("parallel",)),
        name="branch_merge",
    )(fox_o, ret_o, proj, proj, b_gate, w_branch)


def _route(logits):
    lane = lax.broadcasted_iota(jnp.int32, logits.shape, 1).astype(f32)
    far = jnp.float32(1e9)
    gmask = lane < N_GROUPS
    gl = jnp.where(gmask, logits, NEG_BIG)
    gmax = jnp.max(gl, axis=-1, keepdims=True)
    gsel = jnp.min(jnp.where(gl == gmax, lane, far), axis=-1, keepdims=True)
    p_group = 1.0 / jnp.sum(jnp.where(gmask, jnp.exp(gl - gmax), 0.0), axis=-1, keepdims=True)
    lo = N_GROUPS + EXPERTS_PER_GROUP * gsel
    emask = (lane >= lo) & (lane < lo + EXPERTS_PER_GROUP)
    el = jnp.where(emask, logits, NEG_BIG)
    m1 = jnp.max(el, axis=-1, keepdims=True)
    i1 = jnp.min(jnp.where(el == m1, lane, far), axis=-1, keepdims=True)
    el2 = jnp.where(lane == i1, NEG_BIG, el)
    m2 = jnp.max(el2, axis=-1, keepdims=True)
    i2 = jnp.min(jnp.where(el2 == m2, lane, far), axis=-1, keepdims=True)
    t = jnp.exp(m2 - m1)
    w1 = p_group / (1.0 + t)
    w2 = p_group * t / (1.0 + t)
    out = jnp.where(lane == 0, i1 - N_GROUPS, 0.0)
    out = jnp.where(lane == 1, i2 - N_GROUPS, out)
    out = jnp.where(lane == 2, w1, out)
    out = jnp.where(lane == 3, w2, out)
    return out


def _outproj_kernel(mg_ref, x_ref, wo_ref, nfw_ref, wr_ref, br_ref, h_ref, hs_ref, route_ref, cnt_ref):
    h = x_ref[...] + jnp.dot(mg_ref[...], wo_ref[...], preferred_element_type=f32)
    h_ref[...] = h
    hn = h * lax.rsqrt(jnp.mean(h * h, axis=-1, keepdims=True) + EPS) * nfw_ref[...]
    _rows_to_slabs(hs_ref, hn, hn.shape[0])
    hi = hn.astype(bf16)
    lo = (hn - hi.astype(f32)).astype(bf16)
    hw = jnp.dot(hi, wr_ref[...], preferred_element_type=f32)
    logits = (hw[:, :LANES] + jnp.dot(lo, wr_ref[:, :LANES], preferred_element_type=f32) + hw[:, LANES:]) + br_ref[...]
    route = _route(logits)
    route_ref[...] = route

    @pl.when(pl.program_id(0) == 0)
    def _():
        cnt_ref[...] = jnp.zeros_like(cnt_ref)

    lane = lax.broadcasted_iota(jnp.int32, route.shape, 1).astype(f32)
    chosen = (lane == route[:, 0:1]).astype(f32) + (lane == route[:, 1:2]).astype(f32)
    cnt_ref[...] += jnp.sum(chosen, axis=0, keepdims=True)


def _outproj_router(merged, x2, w_out, nfw, wr_cat, b_router):
    t, d = x2.shape
    tm = min(OUTPROJ_ROWS, t)
    return pl.pallas_call(
        _outproj_kernel,
        grid=(t // tm,),
        in_specs=[pl.BlockSpec((tm, d), lambda i: (i, 0)),
                  pl.BlockSpec((tm, d), lambda i: (i, 0)),
                  _const_spec((d, d)),
                  _const_spec((1, d)),
                  _const_spec((d, 2 * LANES)), _const_spec((1, LANES))],
        out_specs=[pl.BlockSpec((tm, d), lambda i: (i, 0)),
                   pl.BlockSpec((tm * PITCH, LANES), lambda i: (i, 0)),
                   pl.BlockSpec((tm, LANES), lambda i: (i, 0)),
                   pl.BlockSpec((8, LANES), lambda i: (0, 0))],
        out_shape=[jax.ShapeDtypeStruct((t, d), f32), jax.ShapeDtypeStruct((t * PITCH, LANES), u32),
                   jax.ShapeDtypeStruct((t, LANES), f32), jax.ShapeDtypeStruct((8, LANES), f32)],
        compiler_params=_cparams(("arbitrary",)),
        name="outproj_router",
    )(merged, x2, w_out, nfw, wr_cat, b_router)


def _plan_kernel(route_ref, cnt_ref, dest_ref, ends_ref, carry_ref, *, bm):
    r = route_ref[...]
    tb = r.shape[0]
    lane = lax.broadcasted_iota(jnp.int32, r.shape, 1).astype(f32)
    o0 = (lane == r[:, 0:1]).astype(f32)
    o1 = (lane == r[:, 1:2]).astype(f32)
    o = o0 + o1

    @pl.when(pl.program_id(0) == 0)
    def _():
        cnt = cnt_ref[...]
        padded = jnp.floor((cnt + (bm - 1)) * (1.0 / bm)) * bm
        lane8 = lax.broadcasted_iota(jnp.int32, cnt.shape, 1)
        ends = padded
        sh = 1
        while sh < LANES:
            ends = ends + jnp.where(lane8 >= sh, pltpu.roll(ends, sh, axis=1), 0.0)
            sh *= 2
        sub = lax.broadcasted_iota(jnp.int32, cnt.shape, 0)
        ends_ref[...] = jnp.where(sub == 0, ends, jnp.where(sub == 1, padded, cnt))
        carry_ref[...] = ends - padded

    row = lax.broadcasted_iota(jnp.int32, (tb, tb), 0)
    col = lax.broadcasted_iota(jnp.int32, (tb, tb), 1)
    lower = jnp.where(col < row, 1.0, 0.0).astype(bf16)
    before = jnp.dot(lower, o.astype(bf16), preferred_element_type=f32) + carry_ref[0:1, :]
    d0 = jnp.sum(before * o0, axis=-1, keepdims=True)
    d1 = jnp.sum(before * o1, axis=-1, keepdims=True)
    dd = jnp.where(lane == 0, d0, jnp.where(lane == 1, d1, 0.0))
    dest_ref[0] = dd.T[:8, :].astype(jnp.int32)
    carry_ref[...] += jnp.sum(o, axis=0, keepdims=True)


def _dispatch_plan(route, counts, bm):
    t = route.shape[0]
    tb = min(PLAN_ROWS, t)
    nb = t // tb
    return pl.pallas_call(
        functools.partial(_plan_kernel, bm=bm),
        grid=(nb,),
        in_specs=[pl.BlockSpec((tb, LANES), lambda i: (i, 0)),
                  pl.BlockSpec((8, LANES), lambda i: (0, 0))],
        out_specs=[pl.BlockSpec((1, 8, tb), lambda i: (i, 0, 0)),
                   pl.BlockSpec((8, LANES), lambda i: (0, 0))],
        out_shape=[jax.ShapeDtypeStruct((nb, 8, tb), jnp.int32), jax.ShapeDtypeStruct((8, LANES), f32)],
        scratch_shapes=[pltpu.VMEM((8, LANES), f32)],
        compiler_params=_cparams(("arbitrary",)),
        name="dispatch_plan",
    )(route, counts)


def _dispatch_kernel(d0_ref, d1_ref, fill_ref, end_ref, nu_ref, hs_ref, xs_hbm, zbuf, sem, zsem,
                     *, tb, bm, n_blk, unroll=8):
    i = pl.program_id(0)
    base = i * tb
    blk_rows = bm * PITCH

    def pad_copy(s, n):
        return pltpu.make_async_copy(zbuf.at[pl.ds(0, n * PITCH), :], xs_hbm.at[pl.ds(s * PITCH, n * PITCH), :], zsem)

    def tail_copy(g):
        return pltpu.make_async_copy(zbuf, xs_hbm.at[pl.ds(pl.multiple_of(g * blk_rows, 8), blk_rows), :], zsem)

    @pl.when(i == 0)
    def _():
        zbuf[...] = jnp.zeros(zbuf.shape, u32)
        for op in ("start", "wait"):
            def per_expert(e, c, op=op):
                s = fill_ref[e]
                for n in PAD_FILL_CHUNKS:
                    count = (end_ref[e] - s) // n

                    def one(k, c2, s=s, n=n):
                        getattr(pad_copy(s + k * n, n), op)()
                        return c2

                    c = lax.fori_loop(0, count, one, c)
                    s = s + count * n
                return c

            def per_tail(g, c, op=op):
                getattr(tail_copy(g), op)()
                return c

            lax.fori_loop(0, N_EXPERTS, per_expert, 0)
            lax.fori_loop(nu_ref[0], n_blk, per_tail, 0)

    def body(g, c):
        for u in range(unroll):
            r = g * unroll + u
            src = hs_ref.at[pl.ds(r * PITCH, PITCH), :]
            for k, dref in enumerate((d0_ref, d1_ref)):
                d = dref[base + r]
                pltpu.make_async_copy(src, xs_hbm.at[pl.ds(d * PITCH, PITCH), :], sem).start(priority=k)
        return c

    lax.fori_loop(0, tb // unroll, body, 0)
    for _ in range(TOP_K):
        pltpu.make_async_copy(xs_hbm.at[pl.ds(0, tb * PITCH), :], xs_hbm.at[pl.ds(0, tb * PITCH), :], sem).wait()


def _dispatch_rows(dest0, dest1, fill_start, seg_end, n_used, hs, cap):
    t = dest0.shape[0]
    tb = min(DISPATCH_ROWS, t)
    bm = MOE_BLOCK
    grid_spec = pltpu.PrefetchScalarGridSpec(
        num_scalar_prefetch=5,
        grid=(t // tb,),
        in_specs=[pl.BlockSpec((tb * PITCH, LANES), lambda i, *_: (i, 0))],
        out_specs=pl.BlockSpec(memory_space=pl.ANY),
        scratch_shapes=[pltpu.VMEM((bm * PITCH, LANES), u32), pltpu.SemaphoreType.DMA(()),
                        pltpu.SemaphoreType.DMA(())],
    )
    return pl.pallas_call(
        functools.partial(_dispatch_kernel, tb=tb, bm=bm, n_blk=cap // bm),
        grid_spec=grid_spec,
        out_shape=jax.ShapeDtypeStruct((cap * PITCH, LANES), u32),
        compiler_params=_cparams(("arbitrary",)),
        name="dispatch_rows",
    )(dest0, dest1, fill_start, seg_end, n_used, hs)


def _moe_kernel(g0_ref, nb_ref, nu_ref, x_hbm, w1_ref, w3_ref, w2_ref, y_hbm,
                xbuf, ybuf, w1b, w3b, w2b, gsem, ysem, *, bm, n_blk):
    e = pl.program_id(0)
    n_used = nu_ref[0]
    g0 = g0_ref[e]
    nb = nb_ref[e]
    blk_rows = bm * PITCH

    def x_copy(slot, g):
        src = x_hbm.at[pl.ds(pl.multiple_of(g * blk_rows, 8), blk_rows), :]
        return pltpu.make_async_copy(src, xbuf.at[slot], gsem.at[slot])

    def y_copy(slot, g):
        dst = y_hbm.at[pl.ds(pl.multiple_of(g * blk_rows, 8), blk_rows), :]
        return pltpu.make_async_copy(ybuf.at[slot], dst, ysem.at[slot])

    @pl.when(e == 0)
    def _():
        x_copy(0, 0).start()
        for k in range(1, X_BUFS - 1):
            pl.when(n_used > k)(lambda k=k: x_copy(k, k).start())

    def run_block(g, cast_weights):
        slot = g % 2
        xslot = g % X_BUFS
        ahead = g + (X_BUFS - 1)

        @pl.when(ahead < n_used)
        def _():
            x_copy(ahead % X_BUFS, ahead).start()

        x_copy(xslot, g).wait()

        if cast_weights:
            w1b[...] = w1_ref[0].astype(bf16)
        hn = _slabs_to_rows(xbuf.at[xslot], bm).astype(bf16)
        a = jnp.dot(hn, w1b[...], preferred_element_type=f32)
        if cast_weights:
            w3b[...] = w3_ref[0].astype(bf16)
        b = jnp.dot(hn, w3b[...], preferred_element_type=f32)
        if cast_weights:
            w2b[...] = w2_ref[0].astype(bf16)
        mid = (a * jax.nn.sigmoid(a) * b).astype(bf16)
        y = jnp.dot(mid, w2b[...], preferred_element_type=f32)

        @pl.when(g >= 2)
        def _():
            y_copy(slot, g - 2).wait()

        _rows_to_slabs(ybuf.at[slot], y, bm)
        y_copy(slot, g).start()

    @pl.when(nb > 0)
    def _():
        run_block(g0, True)

    def block(c, carry):
        run_block(g0 + c, False)
        return carry

    lax.fori_loop(1, nb, block, 0)

    @pl.when(e == pl.num_programs(0) - 1)
    def _():
        @pl.when(n_used >= 2)
        def _():
            y_copy(n_used % 2, n_used - 2).wait()

        y_copy((n_used - 1) % 2, n_used - 1).wait()
        ybuf[0] = jnp.zeros(ybuf.shape[1:], u32)

        def zstart(g, carry):
            y_copy(0, g).start()
            return carry

        def zwait(g, carry):
            y_copy(0, g).wait()
            return carry

        lax.fori_loop(n_used, n_blk, zstart, 0)
        lax.fori_loop(n_used, n_blk, zwait, 0)


def _expert_mlp(blk_start, blk_count, n_used, xs, w1, w3, w2, cap):
    d = D_MODEL
    bm = MOE_BLOCK
    n_blk = cap // bm
    by_expert = lambda shape: pl.BlockSpec(shape, lambda e, g0, nb, nu: (e, 0, 0))
    grid_spec = pltpu.PrefetchScalarGridSpec(
        num_scalar_prefetch=3,
        grid=(N_EXPERTS,),
        in_specs=[pl.BlockSpec(memory_space=pl.ANY),
                  by_expert((1, d, D_EXPERT)), by_expert((1, d, D_EXPERT)), by_expert((1, D_EXPERT, d))],
        out_specs=pl.BlockSpec(memory_space=pl.ANY),
        scratch_shapes=[pltpu.VMEM((X_BUFS, bm * PITCH, LANES), u32), pltpu.VMEM((2, bm * PITCH, LANES), u32),
                        pltpu.VMEM((d, D_EXPERT), bf16), pltpu.VMEM((d, D_EXPERT), bf16),
                        pltpu.VMEM((D_EXPERT, d), bf16),
                        pltpu.SemaphoreType.DMA((X_BUFS,)), pltpu.SemaphoreType.DMA((2,))],
    )
    return pl.pallas_call(
        functools.partial(_moe_kernel, bm=bm, n_blk=n_blk),
        grid_spec=grid_spec,
        out_shape=jax.ShapeDtypeStruct((cap * PITCH, LANES), u32),
        compiler_params=_cparams(("arbitrary",)),
        name="expert_mlp",
    )(blk_start, blk_count, n_used, xs, w1, w3, w2)


def _combine_kernel(d0_ref, d1_ref, h_ref, route_ref, y_hbm, nw_ref, o_ref, ybuf, sem, *, tb):
    i = pl.program_id(0)
    n = pl.num_programs(0)
    slot = i % 2

    def start(blk, s):
        _slab_gather_start(d0_ref, blk * tb, tb, y_hbm, ybuf.at[s, 0], sem.at[s])
        _slab_gather_start(d1_ref, blk * tb, tb, y_hbm, ybuf.at[s, 1], sem.at[s])

    @pl.when(i == 0)
    def _():
        start(0, 0)

    @pl.when(i + 1 < n)
    def _():
        start(i + 1, 1 - slot)

    for k in range(TOP_K):
        _slab_gather_wait(tb, y_hbm, ybuf.at[slot, k], sem.at[slot])
    r = route_ref[...]
    h = h_ref[...] + (r[:, 2:3] * _slabs_to_rows(ybuf.at[slot, 0], tb)
                      + r[:, 3:4] * _slabs_to_rows(ybuf.at[slot, 1], tb))
    o_ref[...] = h * lax.rsqrt(jnp.mean(h * h, axis=-1, keepdims=True) + EPS) * nw_ref[...]


def _combine(dest0, dest1, h, route, ys, norm_w):
    t, d = h.shape
    tb = min(COMBINE_ROWS, t)
    grid_spec = pltpu.PrefetchScalarGridSpec(
        num_scalar_prefetch=2,
        grid=(t // tb,),
        in_specs=[pl.BlockSpec((tb, d), lambda i, a, b: (i, 0)),
                  pl.BlockSpec((tb, LANES), lambda i, a, b: (i, 0)),
                  pl.BlockSpec(memory_space=pl.ANY),
                  pl.BlockSpec((1, d), lambda i, a, b: (0, 0))],
        out_specs=pl.BlockSpec((tb, d), lambda i, a, b: (i, 0)),
        scratch_shapes=[pltpu.VMEM((2, TOP_K, tb * PITCH, LANES), u32), pltpu.SemaphoreType.DMA((2,))],
    )
    return pl.pallas_call(
        functools.partial(_combine_kernel, tb=tb),
        grid_spec=grid_spec,
        out_shape=jax.ShapeDtypeStruct((t, d), f32),
        compiler_params=_cparams(("arbitrary",)),
        name="combine_final_norm",
    )(dest0, dest1, h, route, ys, norm_w)


def kernel(x, positions, norm_mix_w, w_in, fox_b_f, ret_gn_w, w_branch, b_gate, w_out, norm_ffn_w, w_router_group,
           b_router_group, w_router_expert, b_router_expert, w1, w3, w2, norm_final_w):
    b, s, d = x.shape
    assert d == D_MODEL and norm_mix_w.shape[0] == 1
    t = b * s
    x2 = x.reshape(t, d)

    wt = w_in[0].T
    o_f = 3 * FOX_WIDTH
    o_r = o_f + FOX_HEADS
    o_g = o_r + 4 * RET_WIDTH
    tn = PROJ_COLS
    tile_rows = ([o_g + k * tn for k in range(N_BRANCH * D_MODEL // tn)]
                 + [k * tn for k in range(o_f // tn)]
                 + [o_r + k * tn for k in range(4 * RET_WIDTH // tn)])
    w_main = _prep_in_weights(wt, tile_rows)
    w_f = jnp.pad(wt[o_f:o_r], ((0, LANES - FOX_HEADS), (0, 0))).astype(bf16)
    bias_f = jnp.pad(fox_b_f[0], (0, LANES - FOX_HEADS)).reshape(1, LANES)
    w_r = jnp.concatenate([w_router_group[0], w_router_expert[0]], axis=1)
    w_r = jnp.pad(w_r, ((0, 0), (0, LANES - w_r.shape[1])))
    wr_hi = w_r.astype(bf16)
    wr_lo = (w_r - wr_hi.astype(f32)).astype(bf16)
    b_r = jnp.pad(jnp.concatenate([b_router_group[0], b_router_expert[0]]), (0, LANES - N_GROUPS - N_EXPERTS))
    b_r = b_r.reshape(1, LANES)

    cos, sin = _rope_tables(positions)
    proj, f_logit = _in_projection(x2, norm_mix_w, w_main, w_f, cos.reshape(t, HEAD_DIM), sin.reshape(t, HEAD_DIM))
    c = _forget_cumsum(f_logit, bias_f, b, s)
    fox_o = _fox_attention(proj, c.reshape(b * FOX_HEADS, 1, s), b, s)
    ret_o = _retention(proj, ret_gn_w, b, s)
    merged = _branch_merge(fox_o, ret_o, proj, b_gate[0], w_branch[0].astype(bf16))
    h, h_slabs, route, counts = _outproj_router(merged, x2, w_out[0].astype(bf16), norm_ffn_w,
                                        jnp.concatenate([wr_hi, wr_lo], axis=1), b_r)

    bm = MOE_BLOCK
    cap = t * TOP_K + N_EXPERTS * bm
    dest_rows, seg = _dispatch_plan(route, counts, bm)
    dest0 = dest_rows[:, 0, :].reshape(t)
    dest1 = dest_rows[:, 1, :].reshape(t)
    pends = seg[0, :N_EXPERTS].astype(jnp.int32)
    padded = seg[1, :N_EXPERTS].astype(jnp.int32)
    counts = seg[2, :N_EXPERTS].astype(jnp.int32)
    pstarts = pends - padded
    n_used = pends[-1:] // bm

    xs = _dispatch_rows(dest0, dest1, pstarts + counts, pends, n_used, h_slabs, cap)
    y = _expert_mlp(pstarts // bm, padded // bm, n_used, xs, w1[0], w3[0], w2[0], cap)
    out = _combine(dest0, dest1, h, route, y, norm_final_w.reshape(1, d))
    return out.reshape(b, s, d)
```

```python
import functools
import math

import numpy as np
import jax
import jax.numpy as jnp
from jax import lax
from jax.experimental import pallas as pl
from jax.experimental.pallas import tpu as pltpu

D_MODEL = 2048
HEAD_DIM = 128
FOX_HEADS = 8
RET_HEADS = 8
FOX_WIDTH = FOX_HEADS * HEAD_DIM
RET_WIDTH = RET_HEADS * HEAD_DIM
N_BRANCH = 2
N_GROUPS = 4
EXPERTS_PER_GROUP = 8
N_EXPERTS = N_GROUPS * EXPERTS_PER_GROUP
TOP_K = 2
D_EXPERT = 512
ROPE_BASE = 10000.0
EPS = 1e-6

LANES = 128
NEG_BIG = -1e30
LOG2E = math.log2(math.e)
COL_GATE = 0
COL_FQ = N_BRANCH * D_MODEL // HEAD_DIM
COL_FK = COL_FQ + FOX_HEADS
COL_FV = COL_FK + FOX_HEADS
COL_RQ = COL_FV + FOX_HEADS
COL_RK = COL_RQ + RET_HEADS
COL_RV = COL_RK + RET_HEADS
COL_RG = COL_RV + RET_HEADS

ATTN_BLOCK = 256
FOX_HEADS_PER_STEP = 4
RET_HEADS_PER_STEP = 4
RET_CHUNK = 256
MOE_BLOCK = 256
X_BUFS = 4
PAD_FILL_CHUNKS = (64, 8, 1)
PROJ_ROWS, PROJ_COLS = 1024, 1024
MERGE_ROWS = 1024
OUTPROJ_ROWS = 512
PLAN_ROWS = 1024
DISPATCH_ROWS = 2048
COMBINE_ROWS = 256
VMEM_LIMIT = 56 * 1024 * 1024

f32 = jnp.float32
bf16 = jnp.bfloat16
_NT = (((1,), (1,)), ((), ()))
SLAB = D_MODEL // (2 * LANES)
PITCH = SLAB + 1
u32 = jnp.uint32
_HI_MASK = np.uint32(0xFFFF0000)


def _rows_to_slabs(ref2, val, n_rows):
    bits = lambda v: lax.bitcast_convert_type(v.astype(bf16).astype(f32), u32)
    for j in range(SLAB):
        lo = val[:, (2 * j) * LANES:(2 * j + 1) * LANES]
        hi = val[:, (2 * j + 1) * LANES:(2 * j + 2) * LANES]
        ref2[pl.ds(j, n_rows, stride=PITCH), :] = (bits(lo) >> 16) | (bits(hi) & _HI_MASK)
    ref2[pl.ds(SLAB, n_rows, stride=PITCH), :] = jnp.zeros((n_rows, LANES), u32)


def _slabs_to_rows(ref2, n_rows):
    chunks = []
    for j in range(SLAB):
        w = ref2[pl.ds(j, n_rows, stride=PITCH), :]
        chunks.append(lax.bitcast_convert_type(w << 16, f32))
        chunks.append(lax.bitcast_convert_type(w & _HI_MASK, f32))
    return jnp.concatenate(chunks, axis=-1)


def _slab_gather_start(idx_ref, base, n_rows, src_hbm, dst, sem, unroll=8):
    def body(g, _):
        for u in range(unroll):
            r = g * unroll + u
            t = idx_ref[base + r]
            pltpu.make_async_copy(src_hbm.at[pl.ds(t * PITCH, SLAB), :], dst.at[pl.ds(r * PITCH, SLAB), :],
                                  sem).start(priority=u % 2)
        return 0
    lax.fori_loop(0, n_rows // unroll, body, 0)


def _slab_gather_wait(n_rows, src_hbm, dst, sem):
    pltpu.make_async_copy(src_hbm.at[pl.ds(0, n_rows * SLAB), :], dst.at[pl.ds(0, n_rows * SLAB), :], sem).wait()


def _cparams(sem, vmem=VMEM_LIMIT):
    return pltpu.CompilerParams(dimension_semantics=sem, vmem_limit_bytes=vmem)


def _const_spec(shape):
    return pl.BlockSpec(shape, lambda i: tuple(0 for _ in shape), pipeline_mode=pl.Buffered(1))


def _rope_kernel(pos_ref, freq_ref, sgn_ref, cos_ref, sin_ref):
    pos = pos_ref[0].astype(f32)
    hs = pos.shape[0] // 2
    half = HEAD_DIM // 2
    lane = lax.broadcasted_iota(jnp.int32, (hs, HEAD_DIM), 1)
    low = lane < half
    ang = jnp.where(low, pos[:hs], pos[hs:]) * freq_ref[...]
    c = jnp.cos(ang)
    s = jnp.sin(ang) * sgn_ref[...]
    c_sw = pltpu.roll(c, half, axis=1)
    s_sw = pltpu.roll(s, half, axis=1)
    cos_ref[0, :hs, :] = jnp.where(low, c, c_sw)
    cos_ref[0, hs:, :] = jnp.where(low, c_sw, c)
    sin_ref[0, :hs, :] = jnp.where(low, s, -s_sw)
    sin_ref[0, hs:, :] = jnp.where(low, -s_sw, s)


def _rope_tables(positions):
    b, s = positions.shape
    half = HEAD_DIM // 2
    inv = (np.float32(ROPE_BASE) ** (-np.arange(half, dtype=np.float32) / np.float32(half))).astype(np.float32)
    freq = jnp.asarray(np.concatenate([inv, inv])[None, :])
    sgn = jnp.asarray(np.concatenate([-np.ones(half, np.float32), np.ones(half, np.float32)])[None, :])
    out = jax.ShapeDtypeStruct((b, s, HEAD_DIM), f32)
    return pl.pallas_call(
        _rope_kernel,
        grid=(b,),
        in_specs=[pl.BlockSpec((1, s, 1), lambda i: (i, 0, 0)),
                  pl.BlockSpec((1, HEAD_DIM), lambda i: (0, 0)),
                  pl.BlockSpec((1, HEAD_DIM), lambda i: (0, 0))],
        out_specs=[pl.BlockSpec((1, s, HEAD_DIM), lambda i: (i, 0, 0)),
                   pl.BlockSpec((1, s, HEAD_DIM), lambda i: (i, 0, 0))],
        out_shape=[out, out],
        compiler_params=_cparams(("parallel",)),
        name="rope_tables",
    )(positions.reshape(b, s, 1), freq, sgn)


_Q_FACTORS = ((COL_FQ * HEAD_DIM // PROJ_COLS, LOG2E / math.sqrt(HEAD_DIM)),
              (COL_RQ * HEAD_DIM // PROJ_COLS, 1.0 / math.sqrt(HEAD_DIM)))


def _wprep_kernel(off_ref, w_hbm, o_ref, buf, sem, *, tn):
    j = pl.program_id(0)
    n = pl.num_programs(0)
    slot = j % 2

    def copy(jj, s):
        src = w_hbm.at[pl.ds(pl.multiple_of(off_ref[jj], 8), tn), :]
        return pltpu.make_async_copy(src, buf.at[s], sem.at[s])

    @pl.when(j == 0)
    def _():
        copy(0, 0).start()

    @pl.when(j + 1 < n)
    def _():
        copy(j + 1, 1 - slot).start()

    copy(j, slot).wait()
    factor = jnp.float32(1.0)
    for tile, value in _Q_FACTORS:
        factor = jnp.where(j == tile, jnp.float32(value), factor)
    o_ref[...] = (buf[slot] * factor).astype(o_ref.dtype)


def _prep_in_weights(wt, row_offsets):
    tn = PROJ_COLS
    d = wt.shape[1]
    n_tiles = len(row_offsets)
    grid_spec = pltpu.PrefetchScalarGridSpec(
        num_scalar_prefetch=1,
        grid=(n_tiles,),
        in_specs=[pl.BlockSpec(memory_space=pl.ANY)],
        out_specs=pl.BlockSpec((tn, d), lambda j, off: (j, 0)),
        scratch_shapes=[pltpu.VMEM((2, tn, d), f32), pltpu.SemaphoreType.DMA((2,))],
    )
    return pl.pallas_call(
        functools.partial(_wprep_kernel, tn=tn),
        grid_spec=grid_spec,
        out_shape=jax.ShapeDtypeStruct((n_tiles * tn, d), bf16),
        compiler_params=_cparams(("arbitrary",)),
        name="prep_in_weights",
    )(jnp.asarray(row_offsets, jnp.int32), wt)


_ROT_TILES = (COL_RQ * HEAD_DIM // PROJ_COLS, COL_RK * HEAD_DIM // PROJ_COLS)


def _inproj_kernel(x_ref, nw_ref, w_ref, wf_ref, cos_ref, sin_ref, o_ref, f_ref, xn_ref):
    j = pl.program_id(1)

    @pl.when(j == 0)
    def _():
        x = x_ref[...]
        ms = jnp.mean(x * x, axis=-1, keepdims=True)
        xn = (x * lax.rsqrt(ms + EPS) * nw_ref[...]).astype(bf16)
        xn_ref[...] = xn
        f_ref[...] = lax.dot_general(xn, wf_ref[...], _NT, preferred_element_type=f32)

    rotated = (j == _ROT_TILES[0]) | (j == _ROT_TILES[1])

    @pl.when(jnp.logical_not(rotated))
    def _():
        o_ref[...] = lax.dot_general(xn_ref[...], w_ref[...], _NT, preferred_element_type=f32).astype(o_ref.dtype)

    @pl.when(rotated)
    def _():
        acc = lax.dot_general(xn_ref[...], w_ref[...], _NT, preferred_element_type=f32)
        cos = cos_ref[...]
        sin = sin_ref[...]
        for hh in range(acc.shape[1] // HEAD_DIM):
            lanes = slice(hh * HEAD_DIM, (hh + 1) * HEAD_DIM)
            t = acc[:, lanes]
            o_ref[:, lanes] = (t * cos + pltpu.roll(t, HEAD_DIM // 2, axis=1) * sin).astype(o_ref.dtype)


def _in_projection(x2, norm_w, w_main_t, w_f_t, cos, sin):
    t, d = x2.shape
    n = w_main_t.shape[0]
    tm = min(PROJ_ROWS, t)
    tn = PROJ_COLS
    return pl.pallas_call(
        _inproj_kernel,
        grid=(t // tm, n // tn),
        in_specs=[pl.BlockSpec((tm, d), lambda i, j: (i, 0)),
                  pl.BlockSpec((1, d), lambda i, j: (0, 0)),
                  pl.BlockSpec((tn, d), lambda i, j: (j, 0)),
                  pl.BlockSpec((LANES, d), lambda i, j: (0, 0)),
                  pl.BlockSpec((tm, HEAD_DIM), lambda i, j: (i, 0)),
                  pl.BlockSpec((tm, HEAD_DIM), lambda i, j: (i, 0))],
        out_specs=[pl.BlockSpec((tm, tn), lambda i, j: (i, j)),
                   pl.BlockSpec((tm, LANES), lambda i, j: (i, 0))],
        out_shape=[jax.ShapeDtypeStruct((t, n), bf16), jax.ShapeDtypeStruct((t, LANES), f32)],
        scratch_shapes=[pltpu.VMEM((tm, d), bf16)],
        compiler_params=_cparams(("parallel", "arbitrary")),
        name="in_projection",
    )(x2, norm_w, w_main_t, w_f_t, cos, sin)


def _fcum_kernel(f_ref, b_ref, c_ref):
    z = f_ref[...] + b_ref[...]
    ls = jnp.minimum(z, 0.0) - jnp.log1p(jnp.exp(-jnp.abs(z)))
    c = ls.T[:FOX_HEADS, :]
    s = c.shape[1]
    lane = lax.broadcasted_iota(jnp.int32, c.shape, 1)
    sh = 1
    while sh < s:
        c = c + jnp.where(lane >= sh, pltpu.roll(c, sh, axis=1), 0.0)
        sh *= 2
    c_ref[0] = c * LOG2E


def _forget_cumsum(f_logit, bias, b, s):
    return pl.pallas_call(
        _fcum_kernel,
        grid=(b,),
        in_specs=[pl.BlockSpec((s, LANES), lambda i: (i, 0)),
                  pl.BlockSpec((1, LANES), lambda i: (0, 0))],
        out_specs=pl.BlockSpec((1, FOX_HEADS, s), lambda i: (i, 0, 0)),
        out_shape=jax.ShapeDtypeStruct((b, FOX_HEADS, s), f32),
        compiler_params=_cparams(("parallel",)),
        name="forget_cumsum",
    )(f_logit, bias)


def _fox_kernel(q_ref, k_ref, v_ref, c_ref, o_ref, va_ref, acc_ref, *, tq):
    s_len = q_ref.shape[0]
    nq = s_len // tq
    n_heads = q_ref.shape[1] // HEAD_DIM
    row = lax.broadcasted_iota(jnp.int32, (tq, tq), 0)
    col = lax.broadcasted_iota(jnp.int32, (tq, tq), 1)
    causal = row >= col
    def scores(hh):
        lanes = slice(hh * HEAD_DIM, (hh + 1) * HEAD_DIM)
        va_ref[hh, :, :HEAD_DIM] = v_ref[:, lanes]
        va_ref[hh, :, HEAD_DIM:] = jnp.ones((s_len, HEAD_DIM), va_ref.dtype)
        panels = []
        row_max = [None] * nq
        for kb in range(nq):
            r0 = kb * tq
            s = lax.dot_general(q_ref[r0:, lanes], k_ref[r0:r0 + tq, lanes], _NT, preferred_element_type=f32)
            s = s - c_ref[hh, :, r0:r0 + tq]
            parts = [jnp.where(causal, s[:tq], NEG_BIG)]
            parts += [s[(j - kb) * tq:(j - kb + 1) * tq] for j in range(kb + 1, nq)]
            panels.append(parts)
            for j, part in zip(range(kb, nq), parts):
                pm = jnp.max(part, axis=-1, keepdims=True)
                row_max[j] = pm if row_max[j] is None else jnp.maximum(row_max[j], pm)
        return panels, row_max

    def values(hh, panels, row_max):
        lanes = slice(hh * HEAD_DIM, (hh + 1) * HEAD_DIM)
        for kb in range(nq):
            r0 = kb * tq
            p = jnp.concatenate([jnp.exp2(part - row_max[j]).astype(bf16)
                                 for j, part in zip(range(kb, nq), panels[kb])], axis=0)
            contrib = jnp.dot(p, va_ref[hh, r0:r0 + tq, :], preferred_element_type=f32)
            if kb:
                acc_ref[hh, r0:, :] += contrib
            else:
                acc_ref[hh] = contrib
        acc = acc_ref[hh]
        o_ref[:, lanes] = (acc[:, :HEAD_DIM] / acc[:, HEAD_DIM:]).astype(o_ref.dtype)

    pending = None
    for hh in range(n_heads):
        current = (hh,) + scores(hh)
        if pending is not None:
            values(*pending)
        pending = current
    values(*pending)


def _fox_attention(proj, c_blk, b, s):
    tq = min(ATTN_BLOCK, s)
    hp = FOX_HEADS_PER_STEP
    w = hp * HEAD_DIM
    col = lambda c0: pl.BlockSpec((s, w), lambda i, j: (i, c0 // hp + j))
    return pl.pallas_call(
        functools.partial(_fox_kernel, tq=tq),
        grid=(b, FOX_HEADS // hp),
        in_specs=[col(COL_FQ), col(COL_FK), col(COL_FV),
                  pl.BlockSpec((hp, 1, s), lambda i, j: (i * (FOX_HEADS // hp) + j, 0, 0))],
        out_specs=pl.BlockSpec((s, w), lambda i, j: (i, j)),
        out_shape=jax.ShapeDtypeStruct((b * s, FOX_WIDTH), bf16),
        scratch_shapes=[pltpu.VMEM((hp, s, 2 * HEAD_DIM), bf16), pltpu.VMEM((hp, s, 2 * HEAD_DIM), f32)],
        compiler_params=_cparams(("parallel", "parallel")),
        name="fox_attention",
    )(proj, proj, proj, c_blk)


def _ret_kernel(q_ref, k_ref, v_ref, g_ref, dec_ref, xi_ref, zeta_ref, gc_ref, gnw_ref, o_ref, *, chunk):
    s_len = q_ref.shape[0]
    nc = s_len // chunk
    n_heads = q_ref.shape[1] // HEAD_DIM
    states = [None] * n_heads
    for ci in range(nc):
        rows = slice(ci * chunk, (ci + 1) * chunk)
        for hh in range(n_heads):
            lanes = slice(hh * HEAD_DIM, (hh + 1) * HEAD_DIM)
            state = states[hh]
            qb = q_ref[rows, lanes]
            kb = k_ref[rows, lanes]
            v = v_ref[rows, lanes]
            inner = lax.dot_general(qb, kb, _NT, preferred_element_type=f32) * dec_ref[hh]
            o = jnp.dot(inner.astype(bf16), v, preferred_element_type=f32)
            if ci:
                o = o + jnp.dot(qb, state.astype(bf16), preferred_element_type=f32) * xi_ref[hh]
            if ci + 1 < nc:
                kz = (kb.astype(f32) * zeta_ref[hh]).astype(bf16)
                kv = lax.dot_general(kz, v, (((0,), (0,)), ((), ())), preferred_element_type=f32)
                states[hh] = state * gc_ref[hh] + kv if ci else kv
            mu = jnp.mean(o, axis=-1, keepdims=True)
            oc = o - mu
            var = jnp.mean(oc * oc, axis=-1, keepdims=True)
            y = oc * lax.rsqrt(var + EPS) * gnw_ref[:, lanes]
            g = g_ref[rows, lanes].astype(f32)
            o_ref[rows, lanes] = (g * jax.nn.sigmoid(g) * y).astype(o_ref.dtype)


def _retention_constants(chunk):
    hh = np.arange(RET_HEADS, dtype=np.float64)
    log_gamma = np.log1p(-(2.0 ** (-5.0 - hh)))
    n = np.arange(chunk, dtype=np.float64)
    diff = n[:, None] - n[None, :]
    decay = np.where(diff[None] >= 0, np.exp(diff[None] * log_gamma[:, None, None]), 0.0)
    xi = np.exp((n[None, :] + 1.0) * log_gamma[:, None])
    zeta = np.exp((chunk - 1.0 - n[None, :]) * log_gamma[:, None])
    g_chunk = np.exp(chunk * log_gamma)
    bc = lambda a: np.broadcast_to(a[:, :, None], (RET_HEADS, chunk, HEAD_DIM))
    gc = np.broadcast_to(g_chunk[:, None, None], (RET_HEADS, 1, HEAD_DIM))
    to = lambda a: jnp.asarray(np.ascontiguousarray(a), dtype=f32)
    return to(decay), to(bc(xi)), to(bc(zeta)), to(gc)


def _retention(proj, gn_w, b, s):
    chunk = min(RET_CHUNK, s)
    decay, xi, zeta, gc = _retention_constants(chunk)
    hp = RET_HEADS_PER_STEP
    w = hp * HEAD_DIM
    head = lambda col: pl.BlockSpec((s, w), lambda i, j: (i, col // hp + j))
    per_head = lambda shape: pl.BlockSpec((hp,) + shape, lambda i, j: (j, 0, 0))
    return pl.pallas_call(
        functools.partial(_ret_kernel, chunk=chunk),
        grid=(b, RET_HEADS // hp),
        in_specs=[head(COL_RQ), head(COL_RK), head(COL_RV), head(COL_RG),
                  per_head((chunk, chunk)), per_head((chunk, HEAD_DIM)), per_head((chunk, HEAD_DIM)),
                  per_head((1, HEAD_DIM)),
                  pl.BlockSpec((1, w), lambda i, j: (0, j))],
        out_specs=pl.BlockSpec((s, w), lambda i, j: (i, j)),
        out_shape=jax.ShapeDtypeStruct((b * s, RET_WIDTH), bf16),
        compiler_params=_cparams(("parallel", "parallel")),
        name="retention",
    )(proj, proj, proj, proj, decay, xi, zeta, gc, gn_w)


def _branch_merge_kernel(fo_ref, ro_ref, g0_ref, g1_ref, bg_ref, wb_ref, mg_ref, *, ncol):
    d = mg_ref.shape[1]
    cw = d // ncol
    fo = fo_ref[...]
    ro = ro_ref[...]
    for c in range(ncol):
        cs = slice(c * cw, (c + 1) * cw)
        bd0 = jnp.dot(fo, wb_ref[0, :, cs], preferred_element_type=f32)
        bd1 = jnp.dot(ro, wb_ref[1, :, cs], preferred_element_type=f32)
        ga = jax.nn.sigmoid(g0_ref[:, cs].astype(f32) + bg_ref[0:1, cs])
        gb = jax.nn.sigmoid(g1_ref[:, cs].astype(f32) + bg_ref[1:2, cs])
        mg_ref[:, cs] = (ga * bd0 + gb * bd1).astype(mg_ref.dtype)


def _branch_merge(fox_o, ret_o, proj, b_gate, w_branch):
    t = fox_o.shape[0]
    d = D_MODEL
    tm = min(MERGE_ROWS, t)
    gate0 = COL_GATE * HEAD_DIM // d
    return pl.pallas_call(
        functools.partial(_branch_merge_kernel, ncol=4),
        grid=(t // tm,),
        in_specs=[pl.BlockSpec((tm, FOX_WIDTH), lambda i: (i, 0)),
                  pl.BlockSpec((tm, RET_WIDTH), lambda i: (i, 0)),
                  pl.BlockSpec((tm, d), lambda i: (i, gate0)),
                  pl.BlockSpec((tm, d), lambda i: (i, gate0 + 1)),
                  _const_spec((N_BRANCH, d)),
                  _const_spec((N_BRANCH, FOX_WIDTH, d))],
        out_specs=pl.BlockSpec((tm, d), lambda i: (i, 0)),
        out_shape=jax.ShapeDtypeStruct((t, d), bf16),
        compiler_params=_cparams(("parallel",)),
        name="branch_merge",
    )(fox_o, ret_o, proj, proj, b_gate, w_branch)


def _route(logits):
    lane = lax.broadcasted_iota(jnp.int32, logits.shape, 1).astype(f32)
    far = jnp.float32(1e9)
    gmask = lane < N_GROUPS
    gl = jnp.where(gmask, logits, NEG_BIG)
    gmax = jnp.max(gl, axis=-1, keepdims=True)
    gsel = jnp.min(jnp.where(gl == gmax, lane, far), axis=-1, keepdims=True)
    p_group = 1.0 / jnp.sum(jnp.where(gmask, jnp.exp(gl - gmax), 0.0), axis=-1, keepdims=True)
    lo = N_GROUPS + EXPERTS_PER_GROUP * gsel
    emask = (lane >= lo) & (lane < lo + EXPERTS_PER_GROUP)
    el = jnp.where(emask, logits, NEG_BIG)
    m1 = jnp.max(el, axis=-1, keepdims=True)
    i1 = jnp.min(jnp.where(el == m1, lane, far), axis=-1, keepdims=True)
    el2 = jnp.where(lane == i1, NEG_BIG, el)
    m2 = jnp.max(el2, axis=-1, keepdims=True)
    i2 = jnp.min(jnp.where(el2 == m2, lane, far), axis=-1, keepdims=True)
    t = jnp.exp(m2 - m1)
    w1 = p_group / (1.0 + t)
    w2 = p_group * t / (1.0 + t)
    out = jnp.where(lane == 0, i1 - N_GROUPS, 0.0)
    out = jnp.where(lane == 1, i2 - N_GROUPS, out)
    out = jnp.where(lane == 2, w1, out)
    out = jnp.where(lane == 3, w2, out)
    return out


def _outproj_kernel(mg_ref, x_ref, wo_ref, nfw_ref, wr_ref, br_ref, h_ref, hs_ref, route_ref, cnt_ref):
    h = x_ref[...] + jnp.dot(mg_ref[...], wo_ref[...], preferred_element_type=f32)
    h_ref[...] = h
    hn = h * lax.rsqrt(jnp.mean(h * h, axis=-1, keepdims=True) + EPS) * nfw_ref[...]
    _rows_to_slabs(hs_ref, hn, hn.shape[0])
    hi = hn.astype(bf16)
    lo = (hn - hi.astype(f32)).astype(bf16)
    hw = jnp.dot(hi, wr_ref[...], preferred_element_type=f32)
    logits = (hw[:, :LANES] + jnp.dot(lo, wr_ref[:, :LANES], preferred_element_type=f32) + hw[:, LANES:]) + br_ref[...]
    route = _route(logits)
    route_ref[...] = route

    @pl.when(pl.program_id(0) == 0)
    def _():
        cnt_ref[...] = jnp.zeros_like(cnt_ref)

    lane = lax.broadcasted_iota(jnp.int32, route.shape, 1).astype(f32)
    chosen = (lane == route[:, 0:1]).astype(f32) + (lane == route[:, 1:2]).astype(f32)
    cnt_ref[...] += jnp.sum(chosen, axis=0, keepdims=True)


def _outproj_router(merged, x2, w_out, nfw, wr_cat, b_router):
    t, d = x2.shape
    tm = min(OUTPROJ_ROWS, t)
    return pl.pallas_call(
        _outproj_kernel,
        grid=(t // tm,),
        in_specs=[pl.BlockSpec((tm, d), lambda i: (i, 0)),
                  pl.BlockSpec((tm, d), lambda i: (i, 0)),
                  _const_spec((d, d)),
                  _const_spec((1, d)),
                  _const_spec((d, 2 * LANES)), _const_spec((1, LANES))],
        out_specs=[pl.BlockSpec((tm, d), lambda i: (i, 0)),
                   pl.BlockSpec((tm * PITCH, LANES), lambda i: (i, 0)),
                   pl.BlockSpec((tm, LANES), lambda i: (i, 0)),
                   pl.BlockSpec((8, LANES), lambda i: (0, 0))],
        out_shape=[jax.ShapeDtypeStruct((t, d), f32), jax.ShapeDtypeStruct((t * PITCH, LANES), u32),
                   jax.ShapeDtypeStruct((t, LANES), f32), jax.ShapeDtypeStruct((8, LANES), f32)],
        compiler_params=_cparams(("arbitrary",)),
        name="outproj_router",
    )(merged, x2, w_out, nfw, wr_cat, b_router)


def _plan_kernel(route_ref, cnt_ref, dest_ref, ends_ref, carry_ref, *, bm):
    r = route_ref[...]
    tb = r.shape[0]
    lane = lax.broadcasted_iota(jnp.int32, r.shape, 1).astype(f32)
    o0 = (lane == r[:, 0:1]).astype(f32)
    o1 = (lane == r[:, 1:2]).astype(f32)
    o = o0 + o1

    @pl.when(pl.program_id(0) == 0)
    def _():
        cnt = cnt_ref[...]
        padded = jnp.floor((cnt + (bm - 1)) * (1.0 / bm)) * bm
        lane8 = lax.broadcasted_iota(jnp.int32, cnt.shape, 1)
        ends = padded
        sh = 1
        while sh < LANES:
            ends = ends + jnp.where(lane8 >= sh, pltpu.roll(ends, sh, axis=1), 0.0)
            sh *= 2
        sub = lax.broadcasted_iota(jnp.int32, cnt.shape, 0)
        ends_ref[...] = jnp.where(sub == 0, ends, jnp.where(sub == 1, padded, cnt))
        carry_ref[...] = ends - padded

    row = lax.broadcasted_iota(jnp.int32, (tb, tb), 0)
    col = lax.broadcasted_iota(jnp.int32, (tb, tb), 1)
    lower = jnp.where(col < row, 1.0, 0.0).astype(bf16)
    before = jnp.dot(lower, o.astype(bf16), preferred_element_type=f32) + carry_ref[0:1, :]
    d0 = jnp.sum(before * o0, axis=-1, keepdims=True)
    d1 = jnp.sum(before * o1, axis=-1, keepdims=True)
    dd = jnp.where(lane == 0, d0, jnp.where(lane == 1, d1, 0.0))
    dest_ref[0] = dd.T[:8, :].astype(jnp.int32)
    carry_ref[...] += jnp.sum(o, axis=0, keepdims=True)


def _dispatch_plan(route, counts, bm):
    t = route.shape[0]
    tb = min(PLAN_ROWS, t)
    nb = t // tb
    return pl.pallas_call(
        functools.partial(_plan_kernel, bm=bm),
        grid=(nb,),
        in_specs=[pl.BlockSpec((tb, LANES), lambda i: (i, 0)),
                  pl.BlockSpec((8, LANES), lambda i: (0, 0))],
        out_specs=[pl.BlockSpec((1, 8, tb), lambda i: (i, 0, 0)),
                   pl.BlockSpec((8, LANES), lambda i: (0, 0))],
        out_shape=[jax.ShapeDtypeStruct((nb, 8, tb), jnp.int32), jax.ShapeDtypeStruct((8, LANES), f32)],
        scratch_shapes=[pltpu.VMEM((8, LANES), f32)],
        compiler_params=_cparams(("arbitrary",)),
        name="dispatch_plan",
    )(route, counts)


def _dispatch_kernel(d0_ref, d1_ref, fill_ref, end_ref, nu_ref, hs_ref, xs_hbm, zbuf, sem, zsem,
                     *, tb, bm, n_blk, unroll=8):
    i = pl.program_id(0)
    base = i * tb
    blk_rows = bm * PITCH

    def pad_copy(s, n):
        return pltpu.make_async_copy(zbuf.at[pl.ds(0, n * PITCH), :], xs_hbm.at[pl.ds(s * PITCH, n * PITCH), :], zsem)

    def tail_copy(g):
        return pltpu.make_async_copy(zbuf, xs_hbm.at[pl.ds(pl.multiple_of(g * blk_rows, 8), blk_rows), :], zsem)

    @pl.when(i == 0)
    def _():
        zbuf[...] = jnp.zeros(zbuf.shape, u32)
        for op in ("start", "wait"):
            def per_expert(e, c, op=op):
                s = fill_ref[e]
                for n in PAD_FILL_CHUNKS:
                    count = (end_ref[e] - s) // n

                    def one(k, c2, s=s, n=n):
                        getattr(pad_copy(s + k * n, n), op)()
                        return c2

                    c = lax.fori_loop(0, count, one, c)
                    s = s + count * n
                return c

            def per_tail(g, c, op=op):
                getattr(tail_copy(g), op)()
                return c

            lax.fori_loop(0, N_EXPERTS, per_expert, 0)
            lax.fori_loop(nu_ref[0], n_blk, per_tail, 0)

    def body(g, c):
        for u in range(unroll):
            r = g * unroll + u
            src = hs_ref.at[pl.ds(r * PITCH, PITCH), :]
            for k, dref in enumerate((d0_ref, d1_ref)):
                d = dref[base + r]
                pltpu.make_async_copy(src, xs_hbm.at[pl.ds(d * PITCH, PITCH), :], sem).start(priority=k)
        return c

    lax.fori_loop(0, tb // unroll, body, 0)
    for _ in range(TOP_K):
        pltpu.make_async_copy(xs_hbm.at[pl.ds(0, tb * PITCH), :], xs_hbm.at[pl.ds(0, tb * PITCH), :], sem).wait()


def _dispatch_rows(dest0, dest1, fill_start, seg_end, n_used, hs, cap):
    t = dest0.shape[0]
    tb = min(DISPATCH_ROWS, t)
    bm = MOE_BLOCK
    grid_spec = pltpu.PrefetchScalarGridSpec(
        num_scalar_prefetch=5,
        grid=(t // tb,),
        in_specs=[pl.BlockSpec((tb * PITCH, LANES), lambda i, *_: (i, 0))],
        out_specs=pl.BlockSpec(memory_space=pl.ANY),
        scratch_shapes=[pltpu.VMEM((bm * PITCH, LANES), u32), pltpu.SemaphoreType.DMA(()),
                        pltpu.SemaphoreType.DMA(())],
    )
    return pl.pallas_call(
        functools.partial(_dispatch_kernel, tb=tb, bm=bm, n_blk=cap // bm),
        grid_spec=grid_spec,
        out_shape=jax.ShapeDtypeStruct((cap * PITCH, LANES), u32),
        compiler_params=_cparams(("arbitrary",)),
        name="dispatch_rows",
    )(dest0, dest1, fill_start, seg_end, n_used, hs)


def _moe_kernel(g0_ref, nb_ref, nu_ref, x_hbm, w1_ref, w3_ref, w2_ref, y_hbm,
                xbuf, ybuf, w1b, w3b, w2b, gsem, ysem, *, bm, n_blk):
    e = pl.program_id(0)
    n_used = nu_ref[0]
    g0 = g0_ref[e]
    nb = nb_ref[e]
    blk_rows = bm * PITCH

    def x_copy(slot, g):
        src = x_hbm.at[pl.ds(pl.multiple_of(g * blk_rows, 8), blk_rows), :]
        return pltpu.make_async_copy(src, xbuf.at[slot], gsem.at[slot])

    def y_copy(slot, g):
        dst = y_hbm.at[pl.ds(pl.multiple_of(g * blk_rows, 8), blk_rows), :]
        return pltpu.make_async_copy(ybuf.at[slot], dst, ysem.at[slot])

    @pl.when(e == 0)
    def _():
        x_copy(0, 0).start()
        for k in range(1, X_BUFS - 1):
            pl.when(n_used > k)(lambda k=k: x_copy(k, k).start())

    def run_block(g, cast_weights):
        slot = g % 2
        xslot = g % X_BUFS
        ahead = g + (X_BUFS - 1)

        @pl.when(ahead < n_used)
        def _():
            x_copy(ahead % X_BUFS, ahead).start()

        x_copy(xslot, g).wait()

        if cast_weights:
            w1b[...] = w1_ref[0].astype(bf16)
        hn = _slabs_to_rows(xbuf.at[xslot], bm).astype(bf16)
        a = jnp.dot(hn, w1b[...], preferred_element_type=f32)
        if cast_weights:
            w3b[...] = w3_ref[0].astype(bf16)
        b = jnp.dot(hn, w3b[...], preferred_element_type=f32)
        if cast_weights:
            w2b[...] = w2_ref[0].astype(bf16)
        mid = (a * jax.nn.sigmoid(a) * b).astype(bf16)
        y = jnp.dot(mid, w2b[...], preferred_element_type=f32)

        @pl.when(g >= 2)
        def _():
            y_copy(slot, g - 2).wait()

        _rows_to_slabs(ybuf.at[slot], y, bm)
        y_copy(slot, g).start()

    @pl.when(nb > 0)
    def _():
        run_block(g0, True)

    def block(c, carry):
        run_block(g0 + c, False)
        return carry

    lax.fori_loop(1, nb, block, 0)

    @pl.when(e == pl.num_programs(0) - 1)
    def _():
        @pl.when(n_used >= 2)
        def _():
            y_copy(n_used % 2, n_used - 2).wait()

        y_copy((n_used - 1) % 2, n_used - 1).wait()
        ybuf[0] = jnp.zeros(ybuf.shape[1:], u32)

        def zstart(g, carry):
            y_copy(0, g).start()
            return carry

        def zwait(g, carry):
            y_copy(0, g).wait()
            return carry

        lax.fori_loop(n_used, n_blk, zstart, 0)
        lax.fori_loop(n_used, n_blk, zwait, 0)


def _expert_mlp(blk_start, blk_count, n_used, xs, w1, w3, w2, cap):
    d = D_MODEL
    bm = MOE_BLOCK
    n_blk = cap // bm
    by_expert = lambda shape: pl.BlockSpec(shape, lambda e, g0, nb, nu: (e, 0, 0))
    grid_spec = pltpu.PrefetchScalarGridSpec(
        num_scalar_prefetch=3,
        grid=(N_EXPERTS,),
        in_specs=[pl.BlockSpec(memory_space=pl.ANY),
                  by_expert((1, d, D_EXPERT)), by_expert((1, d, D_EXPERT)), by_expert((1, D_EXPERT, d))],
        out_specs=pl.BlockSpec(memory_space=pl.ANY),
        scratch_shapes=[pltpu.VMEM((X_BUFS, bm * PITCH, LANES), u32), pltpu.VMEM((2, bm * PITCH, LANES), u32),
                        pltpu.VMEM((d, D_EXPERT), bf16), pltpu.VMEM((d, D_EXPERT), bf16),
                        pltpu.VMEM((D_EXPERT, d), bf16),
                        pltpu.SemaphoreType.DMA((X_BUFS,)), pltpu.SemaphoreType.DMA((2,))],
    )
    return pl.pallas_call(
        functools.partial(_moe_kernel, bm=bm, n_blk=n_blk),
        grid_spec=grid_spec,
        out_shape=jax.ShapeDtypeStruct((cap * PITCH, LANES), u32),
        compiler_params=_cparams(("arbitrary",)),
        name="expert_mlp",
    )(blk_start, blk_count, n_used, xs, w1, w3, w2)


def _combine_kernel(d0_ref, d1_ref, h_ref, route_ref, y_hbm, nw_ref, o_ref, ybuf, sem, *, tb):
    i = pl.program_id(0)
    n = pl.num_programs(0)
    slot = i % 2

    def start(blk, s):
        _slab_gather_start(d0_ref, blk * tb, tb, y_hbm, ybuf.at[s, 0], sem.at[s])
        _slab_gather_start(d1_ref, blk * tb, tb, y_hbm, ybuf.at[s, 1], sem.at[s])

    @pl.when(i == 0)
    def _():
        start(0, 0)

    @pl.when(i + 1 < n)
    def _():
        start(i + 1, 1 - slot)

    for k in range(TOP_K):
        _slab_gather_wait(tb, y_hbm, ybuf.at[slot, k], sem.at[slot])
    r = route_ref[...]
    h = h_ref[...] + (r[:, 2:3] * _slabs_to_rows(ybuf.at[slot, 0], tb)
                      + r[:, 3:4] * _slabs_to_rows(ybuf.at[slot, 1], tb))
    o_ref[...] = h * lax.rsqrt(jnp.mean(h * h, axis=-1, keepdims=True) + EPS) * nw_ref[...]


def _combine(dest0, dest1, h, route, ys, norm_w):
    t, d = h.shape
    tb = min(COMBINE_ROWS, t)
    grid_spec = pltpu.PrefetchScalarGridSpec(
        num_scalar_prefetch=2,
        grid=(t // tb,),
        in_specs=[pl.BlockSpec((tb, d), lambda i, a, b: (i, 0)),
                  pl.BlockSpec((tb, LANES), lambda i, a, b: (i, 0)),
                  pl.BlockSpec(memory_space=pl.ANY),
                  pl.BlockSpec((1, d), lambda i, a, b: (0, 0))],
        out_specs=pl.BlockSpec((tb, d), lambda i, a, b: (i, 0)),
        scratch_shapes=[pltpu.VMEM((2, TOP_K, tb * PITCH, LANES), u32), pltpu.SemaphoreType.DMA((2,))],
    )
    return pl.pallas_call(
        functools.partial(_combine_kernel, tb=tb),
        grid_spec=grid_spec,
        out_shape=jax.ShapeDtypeStruct((t, d), f32),
        compiler_params=_cparams(("arbitrary",)),
        name="combine_final_norm",
    )(dest0, dest1, h, route, ys, norm_w)


def kernel(x, positions, norm_mix_w, w_in, fox_b_f, ret_gn_w, w_branch, b_gate, w_out, norm_ffn_w, w_router_group,
           b_router_group, w_router_expert, b_router_expert, w1, w3, w2, norm_final_w):
    b, s, d = x.shape
    assert d == D_MODEL and norm_mix_w.shape[0] == 1
    t = b * s
    x2 = x.reshape(t, d)

    wt = w_in[0].T
    o_f = 3 * FOX_WIDTH
    o_r = o_f + FOX_HEADS
    o_g = o_r + 4 * RET_WIDTH
    tn = PROJ_COLS
    tile_rows = ([o_g + k * tn for k in range(N_BRANCH * D_MODEL // tn)]
                 + [k * tn for k in range(o_f // tn)]
                 + [o_r + k * tn for k in range(4 * RET_WIDTH // tn)])
    w_main = _prep_in_weights(wt, tile_rows)
    w_f = jnp.pad(wt[o_f:o_r], ((0, LANES - FOX_HEADS), (0, 0))).astype(bf16)
    bias_f = jnp.pad(fox_b_f[0], (0, LANES - FOX_HEADS)).reshape(1, LANES)
    w_r = jnp.concatenate([w_router_group[0], w_router_expert[0]], axis=1)
    w_r = jnp.pad(w_r, ((0, 0), (0, LANES - w_r.shape[1])))
    wr_hi = w_r.astype(bf16)
    wr_lo = (w_r - wr_hi.astype(f32)).astype(bf16)
    b_r = jnp.pad(jnp.concatenate([b_router_group[0], b_router_expert[0]]), (0, LANES - N_GROUPS - N_EXPERTS))
    b_r = b_r.reshape(1, LANES)

    cos, sin = _rope_tables(positions)
    proj, f_logit = _in_projection(x2, norm_mix_w, w_main, w_f, cos.reshape(t, HEAD_DIM), sin.reshape(t, HEAD_DIM))
    c = _forget_cumsum(f_logit, bias_f, b, s)
    fox_o = _fox_attention(proj, c.reshape(b * FOX_HEADS, 1, s), b, s)
    ret_o = _retention(proj, ret_gn_w, b, s)
    merged = _branch_merge(fox_o, ret_o, proj, b_gate[0], w_branch[0].astype(bf16))
    h, h_slabs, route, counts = _outproj_router(merged, x2, w_out[0].astype(bf16), norm_ffn_w,
                                        jnp.concatenate([wr_hi, wr_lo], axis=1), b_r)

    bm = MOE_BLOCK
    cap = t * TOP_K + N_EXPERTS * bm
    dest_rows, seg = _dispatch_plan(route, counts, bm)
    dest0 = dest_rows[:, 0, :].reshape(t)
    dest1 = dest_rows[:, 1, :].reshape(t)
    pends = seg[0, :N_EXPERTS].astype(jnp.int32)
    padded = seg[1, :N_EXPERTS].astype(jnp.int32)
    counts = seg[2, :N_EXPERTS].astype(jnp.int32)
    pstarts = pends - padded
    n_used = pends[-1:] // bm

    xs = _dispatch_rows(dest0, dest1, pstarts + counts, pends, n_used, h_slabs, cap)
    y = _expert_mlp(pstarts // bm, padded // bm, n_used, xs, w1[0], w3[0], w2[0], cap)
    out = _combine(dest0, dest1, h, route, y, norm_final_w.reshape(1, d))
    return out.reshape(b, s, d)
```

```python
import functools
import math

import numpy as np
import jax
import jax.numpy as jnp
from jax import lax
from jax.experimental import pallas as pl
from jax.experimental.pallas import tpu as pltpu

D_MODEL = 2048
HEAD_DIM = 128
FOX_HEADS = 8
RET_HEADS = 8
FOX_WIDTH = FOX_HEADS * HEAD_DIM
RET_WIDTH = RET_HEADS * HEAD_DIM
N_BRANCH = 2
N_GROUPS = 4
EXPERTS_PER_GROUP = 8
N_EXPERTS = N_GROUPS * EXPERTS_PER_GROUP
TOP_K = 2
D_EXPERT = 512
ROPE_BASE = 10000.0
EPS = 1e-6

LANES = 128
NEG_BIG = -1e30
LOG2E = math.log2(math.e)
COL_GATE = 0
COL_FQ = N_BRANCH * D_MODEL // HEAD_DIM
COL_FK = COL_FQ + FOX_HEADS
COL_FV = COL_FK + FOX_HEADS
COL_RQ = COL_FV + FOX_HEADS
COL_RK = COL_RQ + RET_HEADS
COL_RV = COL_RK + RET_HEADS
COL_RG = COL_RV + RET_HEADS

ATTN_BLOCK = 256
FOX_HEADS_PER_STEP = 4
RET_HEADS_PER_STEP = 4
RET_CHUNK = 256
MOE_BLOCK = 256
X_BUFS = 4
PAD_FILL_CHUNKS = (64, 8, 1)
PROJ_ROWS, PROJ_COLS = 1024, 1024
MERGE_ROWS = 1024
OUTPROJ_ROWS = 512
PLAN_ROWS = 1024
DISPATCH_ROWS = 2048
COMBINE_ROWS = 128
VMEM_LIMIT = 56 * 1024 * 1024

f32 = jnp.float32
bf16 = jnp.bfloat16
_NT = (((1,), (1,)), ((), ()))
SLAB = D_MODEL // (2 * LANES)
PITCH = SLAB + 1
u32 = jnp.uint32
_HI_MASK = np.uint32(0xFFFF0000)


def _rows_to_slabs(ref2, val, n_rows):
    bits = lambda v: lax.bitcast_convert_type(v.astype(bf16).astype(f32), u32)
    for j in range(SLAB):
        lo = val[:, (2 * j) * LANES:(2 * j + 1) * LANES]
        hi = val[:, (2 * j + 1) * LANES:(2 * j + 2) * LANES]
        ref2[pl.ds(j, n_rows, stride=PITCH), :] = (bits(lo) >> 16) | (bits(hi) & _HI_MASK)
    ref2[pl.ds(SLAB, n_rows, stride=PITCH), :] = jnp.zeros((n_rows, LANES), u32)


def _slabs_to_rows(ref2, n_rows):
    chunks = []
    for j in range(SLAB):
        w = ref2[pl.ds(j, n_rows, stride=PITCH), :]
        chunks.append(lax.bitcast_convert_type(w << 16, f32))
        chunks.append(lax.bitcast_convert_type(w & _HI_MASK, f32))
    return jnp.concatenate(chunks, axis=-1)


def _slab_gather_start(idx_ref, base, n_rows, src_hbm, dst, sem, unroll=8):
    def body(g, _):
        for u in range(unroll):
            r = g * unroll + u
            t = idx_ref[base + r]
            pltpu.make_async_copy(src_hbm.at[pl.ds(t * PITCH, SLAB), :], dst.at[pl.ds(r * PITCH, SLAB), :],
                                  sem).start(priority=u % 2)
        return 0
    lax.fori_loop(0, n_rows // unroll, body, 0)


def _slab_gather_wait(n_rows, src_hbm, dst, sem):
    pltpu.make_async_copy(src_hbm.at[pl.ds(0, n_rows * SLAB), :], dst.at[pl.ds(0, n_rows * SLAB), :], sem).wait()


def _cparams(sem, vmem=VMEM_LIMIT):
    return pltpu.CompilerParams(dimension_semantics=sem, vmem_limit_bytes=vmem)


def _const_spec(shape):
    return pl.BlockSpec(shape, lambda i: tuple(0 for _ in shape), pipeline_mode=pl.Buffered(1))


def _rope_kernel(pos_ref, freq_ref, sgn_ref, cos_ref, sin_ref):
    pos = pos_ref[0].astype(f32)
    hs = pos.shape[0] // 2
    half = HEAD_DIM // 2
    lane = lax.broadcasted_iota(jnp.int32, (hs, HEAD_DIM), 1)
    low = lane < half
    ang = jnp.where(low, pos[:hs], pos[hs:]) * freq_ref[...]
    c = jnp.cos(ang)
    s = jnp.sin(ang) * sgn_ref[...]
    c_sw = pltpu.roll(c, half, axis=1)
    s_sw = pltpu.roll(s, half, axis=1)
    cos_ref[0, :hs, :] = jnp.where(low, c, c_sw)
    cos_ref[0, hs:, :] = jnp.where(low, c_sw, c)
    sin_ref[0, :hs, :] = jnp.where(low, s, -s_sw)
    sin_ref[0, hs:, :] = jnp.where(low, -s_sw, s)


def _rope_tables(positions):
    b, s = positions.shape
    half = HEAD_DIM // 2
    inv = (np.float32(ROPE_BASE) ** (-np.arange(half, dtype=np.float32) / np.float32(half))).astype(np.float32)
    freq = jnp.asarray(np.concatenate([inv, inv])[None, :])
    sgn = jnp.asarray(np.concatenate([-np.ones(half, np.float32), np.ones(half, np.float32)])[None, :])
    out = jax.ShapeDtypeStruct((b, s, HEAD_DIM), f32)
    return pl.pallas_call(
        _rope_kernel,
        grid=(b,),
        in_specs=[pl.BlockSpec((1, s, 1), lambda i: (i, 0, 0)),
                  pl.BlockSpec((1, HEAD_DIM), lambda i: (0, 0)),
                  pl.BlockSpec((1, HEAD_DIM), lambda i: (0, 0))],
        out_specs=[pl.BlockSpec((1, s, HEAD_DIM), lambda i: (i, 0, 0)),
                   pl.BlockSpec((1, s, HEAD_DIM), lambda i: (i, 0, 0))],
        out_shape=[out, out],
        compiler_params=_cparams(("parallel",)),
        name="rope_tables",
    )(positions.reshape(b, s, 1), freq, sgn)


_Q_FACTORS = ((COL_FQ * HEAD_DIM // PROJ_COLS, LOG2E / math.sqrt(HEAD_DIM)),
              (COL_RQ * HEAD_DIM // PROJ_COLS, 1.0 / math.sqrt(HEAD_DIM)))


def _wprep_kernel(off_ref, w_hbm, o_ref, buf, sem, *, tn):
    j = pl.program_id(0)
    n = pl.num_programs(0)
    slot = j % 2

    def copy(jj, s):
        src = w_hbm.at[pl.ds(pl.multiple_of(off_ref[jj], 8), tn), :]
        return pltpu.make_async_copy(src, buf.at[s], sem.at[s])

    @pl.when(j == 0)
    def _():
        copy(0, 0).start()

    @pl.when(j + 1 < n)
    def _():
        copy(j + 1, 1 - slot).start()

    copy(j, slot).wait()
    factor = jnp.float32(1.0)
    for tile, value in _Q_FACTORS:
        factor = jnp.where(j == tile, jnp.float32(value), factor)
    o_ref[...] = (buf[slot] * factor).astype(o_ref.dtype)


def _prep_in_weights(wt, row_offsets):
    tn = PROJ_COLS
    d = wt.shape[1]
    n_tiles = len(row_offsets)
    grid_spec = pltpu.PrefetchScalarGridSpec(
        num_scalar_prefetch=1,
        grid=(n_tiles,),
        in_specs=[pl.BlockSpec(memory_space=pl.ANY)],
        out_specs=pl.BlockSpec((tn, d), lambda j, off: (j, 0)),
        scratch_shapes=[pltpu.VMEM((2, tn, d), f32), pltpu.SemaphoreType.DMA((2,))],
    )
    return pl.pallas_call(
        functools.partial(_wprep_kernel, tn=tn),
        grid_spec=grid_spec,
        out_shape=jax.ShapeDtypeStruct((n_tiles * tn, d), bf16),
        compiler_params=_cparams(("arbitrary",)),
        name="prep_in_weights",
    )(jnp.asarray(row_offsets, jnp.int32), wt)


_ROT_TILES = (COL_RQ * HEAD_DIM // PROJ_COLS, COL_RK * HEAD_DIM // PROJ_COLS)


def _inproj_kernel(x_ref, nw_ref, w_ref, wf_ref, cos_ref, sin_ref, o_ref, f_ref, xn_ref):
    j = pl.program_id(1)

    @pl.when(j == 0)
    def _():
        x = x_ref[...]
        ms = jnp.mean(x * x, axis=-1, keepdims=True)
        xn = (x * lax.rsqrt(ms + EPS) * nw_ref[...]).astype(bf16)
        xn_ref[...] = xn
        f_ref[...] = lax.dot_general(xn, wf_ref[...], _NT, preferred_element_type=f32)

    rotated = (j == _ROT_TILES[0]) | (j == _ROT_TILES[1])

    @pl.when(jnp.logical_not(rotated))
    def _():
        o_ref[...] = lax.dot_general(xn_ref[...], w_ref[...], _NT, preferred_element_type=f32).astype(o_ref.dtype)

    @pl.when(rotated)
    def _():
        acc = lax.dot_general(xn_ref[...], w_ref[...], _NT, preferred_element_type=f32)
        cos = cos_ref[...]
        sin = sin_ref[...]
        for hh in range(acc.shape[1] // HEAD_DIM):
            lanes = slice(hh * HEAD_DIM, (hh + 1) * HEAD_DIM)
            t = acc[:, lanes]
            o_ref[:, lanes] = (t * cos + pltpu.roll(t, HEAD_DIM // 2, axis=1) * sin).astype(o_ref.dtype)


def _in_projection(x2, norm_w, w_main_t, w_f_t, cos, sin):
    t, d = x2.shape
    n = w_main_t.shape[0]
    tm = min(PROJ_ROWS, t)
    tn = PROJ_COLS
    return pl.pallas_call(
        _inproj_kernel,
        grid=(t // tm, n // tn),
        in_specs=[pl.BlockSpec((tm, d), lambda i, j: (i, 0)),
                  pl.BlockSpec((1, d), lambda i, j: (0, 0)),
                  pl.BlockSpec((tn, d), lambda i, j: (j, 0)),
                  pl.BlockSpec((LANES, d), lambda i, j: (0, 0)),
                  pl.BlockSpec((tm, HEAD_DIM), lambda i, j: (i, 0)),
                  pl.BlockSpec((tm, HEAD_DIM), lambda i, j: (i, 0))],
        out_specs=[pl.BlockSpec((tm, tn), lambda i, j: (i, j)),
                   pl.BlockSpec((tm, LANES), lambda i, j: (i, 0))],
        out_shape=[jax.ShapeDtypeStruct((t, n), bf16), jax.ShapeDtypeStruct((t, LANES), f32)],
        scratch_shapes=[pltpu.VMEM((tm, d), bf16)],
        compiler_params=_cparams(("parallel", "arbitrary")),
        name="in_projection",
    )(x2, norm_w, w_main_t, w_f_t, cos, sin)


def _fcum_kernel(f_ref, b_ref, c_ref):
    z = f_ref[...] + b_ref[...]
    ls = jnp.minimum(z, 0.0) - jnp.log1p(jnp.exp(-jnp.abs(z)))
    c = ls.T[:FOX_HEADS, :]
    s = c.shape[1]
    lane = lax.broadcasted_iota(jnp.int32, c.shape, 1)
    sh = 1
    while sh < s:
        c = c + jnp.where(lane >= sh, pltpu.roll(c, sh, axis=1), 0.0)
        sh *= 2
    c_ref[0] = c * LOG2E


def _forget_cumsum(f_logit, bias, b, s):
    return pl.pallas_call(
        _fcum_kernel,
        grid=(b,),
        in_specs=[pl.BlockSpec((s, LANES), lambda i: (i, 0)),
                  pl.BlockSpec((1, LANES), lambda i: (0, 0))],
        out_specs=pl.BlockSpec((1, FOX_HEADS, s), lambda i: (i, 0, 0)),
        out_shape=jax.ShapeDtypeStruct((b, FOX_HEADS, s), f32),
        compiler_params=_cparams(("parallel",)),
        name="forget_cumsum",
    )(f_logit, bias)


def _fox_kernel(q_ref, k_ref, v_ref, c_ref, o_ref, va_ref, acc_ref, *, tq):
    s_len = q_ref.shape[0]
    nq = s_len // tq
    n_heads = q_ref.shape[1] // HEAD_DIM
    row = lax.broadcasted_iota(jnp.int32, (tq, tq), 0)
    col = lax.broadcasted_iota(jnp.int32, (tq, tq), 1)
    causal = row >= col
    def scores(hh):
        lanes = slice(hh * HEAD_DIM, (hh + 1) * HEAD_DIM)
        va_ref[hh, :, :HEAD_DIM] = v_ref[:, lanes]
        va_ref[hh, :, HEAD_DIM:] = jnp.ones((s_len, HEAD_DIM), va_ref.dtype)
        panels = []
        row_max = [None] * nq
        for kb in range(nq):
            r0 = kb * tq
            s = lax.dot_general(q_ref[r0:, lanes], k_ref[r0:r0 + tq, lanes], _NT, preferred_element_type=f32)
            s = s - c_ref[hh, :, r0:r0 + tq]
            parts = [jnp.where(causal, s[:tq], NEG_BIG)]
            parts += [s[(j - kb) * tq:(j - kb + 1) * tq] for j in range(kb + 1, nq)]
            panels.append(parts)
            for j, part in zip(range(kb, nq), parts):
                pm = jnp.max(part, axis=-1, keepdims=True)
                row_max[j] = pm if row_max[j] is None else jnp.maximum(row_max[j], pm)
        return panels, row_max

    def values(hh, panels, row_max):
        lanes = slice(hh * HEAD_DIM, (hh + 1) * HEAD_DIM)
        for kb in range(nq):
            r0 = kb * tq
            p = jnp.concatenate([jnp.exp2(part - row_max[j]).astype(bf16)
                                 for j, part in zip(range(kb, nq), panels[kb])], axis=0)
            contrib = jnp.dot(p, va_ref[hh, r0:r0 + tq, :], preferred_element_type=f32)
            if kb:
                acc_ref[hh, r0:, :] += contrib
            else:
                acc_ref[hh] = contrib
        acc = acc_ref[hh]
        o_ref[:, lanes] = (acc[:, :HEAD_DIM] / acc[:, HEAD_DIM:]).astype(o_ref.dtype)

    pending = None
    for hh in range(n_heads):
        current = (hh,) + scores(hh)
        if pending is not None:
            values(*pending)
        pending = current
    values(*pending)


def _fox_attention(proj, c_blk, b, s):
    tq = min(ATTN_BLOCK, s)
    hp = FOX_HEADS_PER_STEP
    w = hp * HEAD_DIM
    col = lambda c0: pl.BlockSpec((s, w), lambda i, j: (i, c0 // hp + j))
    return pl.pallas_call(
        functools.partial(_fox_kernel, tq=tq),
        grid=(b, FOX_HEADS // hp),
        in_specs=[col(COL_FQ), col(COL_FK), col(COL_FV),
                  pl.BlockSpec((hp, 1, s), lambda i, j: (i * (FOX_HEADS // hp) + j, 0, 0))],
        out_specs=pl.BlockSpec((s, w), lambda i, j: (i, j)),
        out_shape=jax.ShapeDtypeStruct((b * s, FOX_WIDTH), bf16),
        scratch_shapes=[pltpu.VMEM((hp, s, 2 * HEAD_DIM), bf16), pltpu.VMEM((hp, s, 2 * HEAD_DIM), f32)],
        compiler_params=_cparams(("parallel", "parallel")),
        name="fox_attention",
    )(proj, proj, proj, c_blk)


def _ret_kernel(q_ref, k_ref, v_ref, g_ref, dec_ref, xi_ref, zeta_ref, gc_ref, gnw_ref, o_ref, *, chunk):
    s_len = q_ref.shape[0]
    nc = s_len // chunk
    n_heads = q_ref.shape[1] // HEAD_DIM
    states = [None] * n_heads
    for ci in range(nc):
        rows = slice(ci * chunk, (ci + 1) * chunk)
        for hh in range(n_heads):
            lanes = slice(hh * HEAD_DIM, (hh + 1) * HEAD_DIM)
            state = states[hh]
            qb = q_ref[rows, lanes]
            kb = k_ref[rows, lanes]
            v = v_ref[rows, lanes]
            inner = lax.dot_general(qb, kb, _NT, preferred_element_type=f32) * dec_ref[hh]
            o = jnp.dot(inner.astype(bf16), v, preferred_element_type=f32)
            if ci:
                o = o + jnp.dot(qb, state.astype(bf16), preferred_element_type=f32) * xi_ref[hh]
            if ci + 1 < nc:
                kz = (kb.astype(f32) * zeta_ref[hh]).astype(bf16)
                kv = lax.dot_general(kz, v, (((0,), (0,)), ((), ())), preferred_element_type=f32)
                states[hh] = state * gc_ref[hh] + kv if ci else kv
            mu = jnp.mean(o, axis=-1, keepdims=True)
            oc = o - mu
            var = jnp.mean(oc * oc, axis=-1, keepdims=True)
            y = oc * lax.rsqrt(var + EPS) * gnw_ref[:, lanes]
            g = g_ref[rows, lanes].astype(f32)
            o_ref[rows, lanes] = (g * jax.nn.sigmoid(g) * y).astype(o_ref.dtype)


def _retention_constants(chunk):
    hh = np.arange(RET_HEADS, dtype=np.float64)
    log_gamma = np.log1p(-(2.0 ** (-5.0 - hh)))
    n = np.arange(chunk, dtype=np.float64)
    diff = n[:, None] - n[None, :]
    decay = np.where(diff[None] >= 0, np.exp(diff[None] * log_gamma[:, None, None]), 0.0)
    xi = np.exp((n[None, :] + 1.0) * log_gamma[:, None])
    zeta = np.exp((chunk - 1.0 - n[None, :]) * log_gamma[:, None])
    g_chunk = np.exp(chunk * log_gamma)
    bc = lambda a: np.broadcast_to(a[:, :, None], (RET_HEADS, chunk, HEAD_DIM))
    gc = np.broadcast_to(g_chunk[:, None, None], (RET_HEADS, 1, HEAD_DIM))
    to = lambda a: jnp.asarray(np.ascontiguousarray(a), dtype=f32)
    return to(decay), to(bc(xi)), to(bc(zeta)), to(gc)


def _retention(proj, gn_w, b, s):
    chunk = min(RET_CHUNK, s)
    decay, xi, zeta, gc = _retention_constants(chunk)
    hp = RET_HEADS_PER_STEP
    w = hp * HEAD_DIM
    head = lambda col: pl.BlockSpec((s, w), lambda i, j: (i, col // hp + j))
    per_head = lambda shape: pl.BlockSpec((hp,) + shape, lambda i, j: (j, 0, 0))
    return pl.pallas_call(
        functools.partial(_ret_kernel, chunk=chunk),
        grid=(b, RET_HEADS // hp),
        in_specs=[head(COL_RQ), head(COL_RK), head(COL_RV), head(COL_RG),
                  per_head((chunk, chunk)), per_head((chunk, HEAD_DIM)), per_head((chunk, HEAD_DIM)),
                  per_head((1, HEAD_DIM)),
                  pl.BlockSpec((1, w), lambda i, j: (0, j))],
        out_specs=pl.BlockSpec((s, w), lambda i, j: (i, j)),
        out_shape=jax.ShapeDtypeStruct((b * s, RET_WIDTH), bf16),
        compiler_params=_cparams(("parallel", "parallel")),
        name="retention",
    )(proj, proj, proj, proj, decay, xi, zeta, gc, gn_w)


def _branch_merge_kernel(fo_ref, ro_ref, g0_ref, g1_ref, bg_ref, wb_ref, mg_ref, *, ncol):
    d = mg_ref.shape[1]
    cw = d // ncol
    fo = fo_ref[...]
    ro = ro_ref[...]
    for c in range(ncol):
        cs = slice(c * cw, (c + 1) * cw)
        bd0 = jnp.dot(fo, wb_ref[0, :, cs], preferred_element_type=f32)
        bd1 = jnp.dot(ro, wb_ref[1, :, cs], preferred_element_type=f32)
        ga = jax.nn.sigmoid(g0_ref[:, cs].astype(f32) + bg_ref[0:1, cs])
        gb = jax.nn.sigmoid(g1_ref[:, cs].astype(f32) + bg_ref[1:2, cs])
        mg_ref[:, cs] = (ga * bd0 + gb * bd1).astype(mg_ref.dtype)


def _branch_merge(fox_o, ret_o, proj, b_gate, w_branch):
    t = fox_o.shape[0]
    d = D_MODEL
    tm = min(MERGE_ROWS, t)
    gate0 = COL_GATE * HEAD_DIM // d
    return pl.pallas_call(
        functools.partial(_branch_merge_kernel, ncol=4),
        grid=(t // tm,),
        in_specs=[pl.BlockSpec((tm, FOX_WIDTH), lambda i: (i, 0)),
                  pl.BlockSpec((tm, RET_WIDTH), lambda i: (i, 0)),
                  pl.BlockSpec((tm, d), lambda i: (i, gate0)),
                  pl.BlockSpec((tm, d), lambda i: (i, gate0 + 1)),
                  _const_spec((N_BRANCH, d)),
                  _const_spec((N_BRANCH, FOX_WIDTH, d))],
        out_specs=pl.BlockSpec((tm, d), lambda i: (i, 0)),
        out_shape=jax.ShapeDtypeStruct((t, d), bf16),
        compiler_params=_cparams(("parallel",)),
        name="branch_merge",
    )(fox_o, ret_o, proj, proj, b_gate, w_branch)


def _route(logits):
    lane = lax.broadcasted_iota(jnp.int32, logits.shape, 1).astype(f32)
    far = jnp.float32(1e9)
    gmask = lane < N_GROUPS
    gl = jnp.where(gmask, logits, NEG_BIG)
    gmax = jnp.max(gl, axis=-1, keepdims=True)
    gsel = jnp.min(jnp.where(gl == gmax, lane, far), axis=-1, keepdims=True)
    p_group = 1.0 / jnp.sum(jnp.where(gmask, jnp.exp(gl - gmax), 0.0), axis=-1, keepdims=True)
    lo = N_GROUPS + EXPERTS_PER_GROUP * gsel
    emask = (lane >= lo) & (lane < lo + EXPERTS_PER_GROUP)
    el = jnp.where(emask, logits, NEG_BIG)
    m1 = jnp.max(el, axis=-1, keepdims=True)
    i1 = jnp.min(jnp.where(el == m1, lane, far), axis=-1, keepdims=True)
    el2 = jnp.where(lane == i1, NEG_BIG, el)
    m2 = jnp.max(el2, axis=-1, keepdims=True)
    i2 = jnp.min(jnp.where(el2 == m2, lane, far), axis=-1, keepdims=True)
    t = jnp.exp(m2 - m1)
    w1 = p_group / (1.0 + t)
    w2 = p_group * t / (1.0 + t)
    out = jnp.where(lane == 0, i1 - N_GROUPS, 0.0)
    out = jnp.where(lane == 1, i2 - N_GROUPS, out)
    out = jnp.where(lane == 2, w1, out)
    out = jnp.where(lane == 3, w2, out)
    return out


def _outproj_kernel(mg_ref, x_ref, wo_ref, nfw_ref, wr_ref, br_ref, h_ref, hs_ref, route_ref, cnt_ref):
    h = x_ref[...] + jnp.dot(mg_ref[...], wo_ref[...], preferred_element_type=f32)
    h_ref[...] = h
    hn = h * lax.rsqrt(jnp.mean(h * h, axis=-1, keepdims=True) + EPS) * nfw_ref[...]
    _rows_to_slabs(hs_ref, hn, hn.shape[0])
    hi = hn.astype(bf16)
    lo = (hn - hi.astype(f32)).astype(bf16)
    hw = jnp.dot(hi, wr_ref[...], preferred_element_type=f32)
    logits = (hw[:, :LANES] + jnp.dot(lo, wr_ref[:, :LANES], preferred_element_type=f32) + hw[:, LANES:]) + br_ref[...]
    route = _route(logits)
    route_ref[...] = route

    @pl.when(pl.program_id(0) == 0)
    def _():
        cnt_ref[...] = jnp.zeros_like(cnt_ref)

    lane = lax.broadcasted_iota(jnp.int32, route.shape, 1).astype(f32)
    chosen = (lane == route[:, 0:1]).astype(f32) + (lane == route[:, 1:2]).astype(f32)
    cnt_ref[...] += jnp.sum(chosen, axis=0, keepdims=True)


def _outproj_router(merged, x2, w_out, nfw, wr_cat, b_router):
    t, d = x2.shape
    tm = min(OUTPROJ_ROWS, t)
    return pl.pallas_call(
        _outproj_kernel,
        grid=(t // tm,),
        in_specs=[pl.BlockSpec((tm, d), lambda i: (i, 0)),
                  pl.BlockSpec((tm, d), lambda i: (i, 0)),
                  _const_spec((d, d)),
                  _const_spec((1, d)),
                  _const_spec((d, 2 * LANES)), _const_spec((1, LANES))],
        out_specs=[pl.BlockSpec((tm, d), lambda i: (i, 0)),
                   pl.BlockSpec((tm * PITCH, LANES), lambda i: (i, 0)),
                   pl.BlockSpec((tm, LANES), lambda i: (i, 0)),
                   pl.BlockSpec((8, LANES), lambda i: (0, 0))],
        out_shape=[jax.ShapeDtypeStruct((t, d), f32), jax.ShapeDtypeStruct((t * PITCH, LANES), u32),
                   jax.ShapeDtypeStruct((t, LANES), f32), jax.ShapeDtypeStruct((8, LANES), f32)],
        compiler_params=_cparams(("arbitrary",)),
        name="outproj_router",
    )(merged, x2, w_out, nfw, wr_cat, b_router)


def _plan_kernel(route_ref, cnt_ref, dest_ref, ends_ref, carry_ref, *, bm):
    r = route_ref[...]
    tb = r.shape[0]
    lane = lax.broadcasted_iota(jnp.int32, r.shape, 1).astype(f32)
    o0 = (lane == r[:, 0:1]).astype(f32)
    o1 = (lane == r[:, 1:2]).astype(f32)
    o = o0 + o1

    @pl.when(pl.program_id(0) == 0)
    def _():
        cnt = cnt_ref[...]
        padded = jnp.floor((cnt + (bm - 1)) * (1.0 / bm)) * bm
        lane8 = lax.broadcasted_iota(jnp.int32, cnt.shape, 1)
        ends = padded
        sh = 1
        while sh < LANES:
            ends = ends + jnp.where(lane8 >= sh, pltpu.roll(ends, sh, axis=1), 0.0)
            sh *= 2
        sub = lax.broadcasted_iota(jnp.int32, cnt.shape, 0)
        ends_ref[...] = jnp.where(sub == 0, ends, jnp.where(sub == 1, padded, cnt))
        carry_ref[...] = ends - padded

    row = lax.broadcasted_iota(jnp.int32, (tb, tb), 0)
    col = lax.broadcasted_iota(jnp.int32, (tb, tb), 1)
    lower = jnp.where(col < row, 1.0, 0.0).astype(bf16)
    before = jnp.dot(lower, o.astype(bf16), preferred_element_type=f32) + carry_ref[0:1, :]
    d0 = jnp.sum(before * o0, axis=-1, keepdims=True)
    d1 = jnp.sum(before * o1, axis=-1, keepdims=True)
    dd = jnp.where(lane == 0, d0, jnp.where(lane == 1, d1, 0.0))
    dest_ref[0] = dd.T[:8, :].astype(jnp.int32)
    carry_ref[...] += jnp.sum(o, axis=0, keepdims=True)


def _dispatch_plan(route, counts, bm):
    t = route.shape[0]
    tb = min(PLAN_ROWS, t)
    nb = t // tb
    return pl.pallas_call(
        functools.partial(_plan_kernel, bm=bm),
        grid=(nb,),
        in_specs=[pl.BlockSpec((tb, LANES), lambda i: (i, 0)),
                  pl.BlockSpec((8, LANES), lambda i: (0, 0))],
        out_specs=[pl.BlockSpec((1, 8, tb), lambda i: (i, 0, 0)),
                   pl.BlockSpec((8, LANES), lambda i: (0, 0))],
        out_shape=[jax.ShapeDtypeStruct((nb, 8, tb), jnp.int32), jax.ShapeDtypeStruct((8, LANES), f32)],
        scratch_shapes=[pltpu.VMEM((8, LANES), f32)],
        compiler_params=_cparams(("arbitrary",)),
        name="dispatch_plan",
    )(route, counts)


def _dispatch_kernel(d0_ref, d1_ref, fill_ref, end_ref, nu_ref, hs_ref, xs_hbm, zbuf, sem, zsem,
                     *, tb, bm, n_blk, unroll=8):
    i = pl.program_id(0)
    base = i * tb
    blk_rows = bm * PITCH

    def pad_copy(s, n):
        return pltpu.make_async_copy(zbuf.at[pl.ds(0, n * PITCH), :], xs_hbm.at[pl.ds(s * PITCH, n * PITCH), :], zsem)

    def tail_copy(g):
        return pltpu.make_async_copy(zbuf, xs_hbm.at[pl.ds(pl.multiple_of(g * blk_rows, 8), blk_rows), :], zsem)

    @pl.when(i == 0)
    def _():
        zbuf[...] = jnp.zeros(zbuf.shape, u32)
        for op in ("start", "wait"):
            def per_expert(e, c, op=op):
                s = fill_ref[e]
                for n in PAD_FILL_CHUNKS:
                    count = (end_ref[e] - s) // n

                    def one(k, c2, s=s, n=n):
                        getattr(pad_copy(s + k * n, n), op)()
                        return c2

                    c = lax.fori_loop(0, count, one, c)
                    s = s + count * n
                return c

            def per_tail(g, c, op=op):
                getattr(tail_copy(g), op)()
                return c

            lax.fori_loop(0, N_EXPERTS, per_expert, 0)
            lax.fori_loop(nu_ref[0], n_blk, per_tail, 0)

    def body(g, c):
        for u in range(unroll):
            r = g * unroll + u
            src = hs_ref.at[pl.ds(r * PITCH, PITCH), :]
            for k, dref in enumerate((d0_ref, d1_ref)):
                d = dref[base + r]
                pltpu.make_async_copy(src, xs_hbm.at[pl.ds(d * PITCH, PITCH), :], sem).start(priority=k)
        return c

    lax.fori_loop(0, tb // unroll, body, 0)
    for _ in range(TOP_K):
        pltpu.make_async_copy(xs_hbm.at[pl.ds(0, tb * PITCH), :], xs_hbm.at[pl.ds(0, tb * PITCH), :], sem).wait()


def _dispatch_rows(dest0, dest1, fill_start, seg_end, n_used, hs, cap):
    t = dest0.shape[0]
    tb = min(DISPATCH_ROWS, t)
    bm = MOE_BLOCK
    grid_spec = pltpu.PrefetchScalarGridSpec(
        num_scalar_prefetch=5,
        grid=(t // tb,),
        in_specs=[pl.BlockSpec((tb * PITCH, LANES), lambda i, *_: (i, 0))],
        out_specs=pl.BlockSpec(memory_space=pl.ANY),
        scratch_shapes=[pltpu.VMEM((bm * PITCH, LANES), u32), pltpu.SemaphoreType.DMA(()),
                        pltpu.SemaphoreType.DMA(())],
    )
    return pl.pallas_call(
        functools.partial(_dispatch_kernel, tb=tb, bm=bm, n_blk=cap // bm),
        grid_spec=grid_spec,
        out_shape=jax.ShapeDtypeStruct((cap * PITCH, LANES), u32),
        compiler_params=_cparams(("arbitrary",)),
        name="dispatch_rows",
    )(dest0, dest1, fill_start, seg_end, n_used, hs)


def _moe_kernel(g0_ref, nb_ref, nu_ref, x_hbm, w1_ref, w3_ref, w2_ref, y_hbm,
                xbuf, ybuf, w1b, w3b, w2b, gsem, ysem, *, bm, n_blk):
    e = pl.program_id(0)
    n_used = nu_ref[0]
    g0 = g0_ref[e]
    nb = nb_ref[e]
    blk_rows = bm * PITCH

    def x_copy(slot, g):
        src = x_hbm.at[pl.ds(pl.multiple_of(g * blk_rows, 8), blk_rows), :]
        return pltpu.make_async_copy(src, xbuf.at[slot], gsem.at[slot])

    def y_copy(slot, g):
        dst = y_hbm.at[pl.ds(pl.multiple_of(g * blk_rows, 8), blk_rows), :]
        return pltpu.make_async_copy(ybuf.at[slot], dst, ysem.at[slot])

    @pl.when(e == 0)
    def _():
        x_copy(0, 0).start()
        for k in range(1, X_BUFS - 1):
            pl.when(n_used > k)(lambda k=k: x_copy(k, k).start())

    def run_block(g, cast_weights):
        slot = g % 2
        xslot = g % X_BUFS
        ahead = g + (X_BUFS - 1)

        @pl.when(ahead < n_used)
        def _():
            x_copy(ahead % X_BUFS, ahead).start()

        x_copy(xslot, g).wait()

        if cast_weights:
            w1b[...] = w1_ref[0].astype(bf16)
        hn = _slabs_to_rows(xbuf.at[xslot], bm).astype(bf16)
        a = jnp.dot(hn, w1b[...], preferred_element_type=f32)
        if cast_weights:
            w3b[...] = w3_ref[0].astype(bf16)
        b = jnp.dot(hn, w3b[...], preferred_element_type=f32)
        if cast_weights:
            w2b[...] = w2_ref[0].astype(bf16)
        mid = (a * jax.nn.sigmoid(a) * b).astype(bf16)
        y = jnp.dot(mid, w2b[...], preferred_element_type=f32)

        @pl.when(g >= 2)
        def _():
            y_copy(slot, g - 2).wait()

        _rows_to_slabs(ybuf.at[slot], y, bm)
        y_copy(slot, g).start()

    @pl.when(nb > 0)
    def _():
        run_block(g0, True)

    def block(c, carry):
        run_block(g0 + c, False)
        return carry

    lax.fori_loop(1, nb, block, 0)

    @pl.when(e == pl.num_programs(0) - 1)
    def _():
        @pl.when(n_used >= 2)
        def _():
            y_copy(n_used % 2, n_used - 2).wait()

        y_copy((n_used - 1) % 2, n_used - 1).wait()
        ybuf[0] = jnp.zeros(ybuf.shape[1:], u32)

        def zstart(g, carry):
            y_copy(0, g).start()
            return carry

        def zwait(g, carry):
            y_copy(0, g).wait()
            return carry

        lax.fori_loop(n_used, n_blk, zstart, 0)
        lax.fori_loop(n_used, n_blk, zwait, 0)


def _expert_mlp(blk_start, blk_count, n_used, xs, w1, w3, w2, cap):
    d = D_MODEL
    bm = MOE_BLOCK
    n_blk = cap // bm
    by_expert = lambda shape: pl.BlockSpec(shape, lambda e, g0, nb, nu: (e, 0, 0))
    grid_spec = pltpu.PrefetchScalarGridSpec(
        num_scalar_prefetch=3,
        grid=(N_EXPERTS,),
        in_specs=[pl.BlockSpec(memory_space=pl.ANY),
                  by_expert((1, d, D_EXPERT)), by_expert((1, d, D_EXPERT)), by_expert((1, D_EXPERT, d))],
        out_specs=pl.BlockSpec(memory_space=pl.ANY),
        scratch_shapes=[pltpu.VMEM((X_BUFS, bm * PITCH, LANES), u32), pltpu.VMEM((2, bm * PITCH, LANES), u32),
                        pltpu.VMEM((d, D_EXPERT), bf16), pltpu.VMEM((d, D_EXPERT), bf16),
                        pltpu.VMEM((D_EXPERT, d), bf16),
                        pltpu.SemaphoreType.DMA((X_BUFS,)), pltpu.SemaphoreType.DMA((2,))],
    )
    return pl.pallas_call(
        functools.partial(_moe_kernel, bm=bm, n_blk=n_blk),
        grid_spec=grid_spec,
        out_shape=jax.ShapeDtypeStruct((cap * PITCH, LANES), u32),
        compiler_params=_cparams(("arbitrary",)),
        name="expert_mlp",
    )(blk_start, blk_count, n_used, xs, w1, w3, w2)


def _combine_kernel(d0_ref, d1_ref, h_ref, route_ref, y_hbm, nw_ref, o_ref, ybuf, sem, *, tb):
    i = pl.program_id(0)
    n = pl.num_programs(0)
    slot = i % 2

    def start(blk, s):
        _slab_gather_start(d0_ref, blk * tb, tb, y_hbm, ybuf.at[s, 0], sem.at[s])
        _slab_gather_start(d1_ref, blk * tb, tb, y_hbm, ybuf.at[s, 1], sem.at[s])

    @pl.when(i == 0)
    def _():
        start(0, 0)

    @pl.when(i + 1 < n)
    def _():
        start(i + 1, 1 - slot)

    for k in range(TOP_K):
        _slab_gather_wait(tb, y_hbm, ybuf.at[slot, k], sem.at[slot])
    r = route_ref[...]
    h = h_ref[...] + (r[:, 2:3] * _slabs_to_rows(ybuf.at[slot, 0], tb)
                      + r[:, 3:4] * _slabs_to_rows(ybuf.at[slot, 1], tb))
    o_ref[...] = h * lax.rsqrt(jnp.mean(h * h, axis=-1, keepdims=True) + EPS) * nw_ref[...]


def _combine(dest0, dest1, h, route, ys, norm_w):
    t, d = h.shape
    tb = min(COMBINE_ROWS, t)
    grid_spec = pltpu.PrefetchScalarGridSpec(
        num_scalar_prefetch=2,
        grid=(t // tb,),
        in_specs=[pl.BlockSpec((tb, d), lambda i, a, b: (i, 0)),
                  pl.BlockSpec((tb, LANES), lambda i, a, b: (i, 0)),
                  pl.BlockSpec(memory_space=pl.ANY),
                  pl.BlockSpec((1, d), lambda i, a, b: (0, 0))],
        out_specs=pl.BlockSpec((tb, d), lambda i, a, b: (i, 0)),
        scratch_shapes=[pltpu.VMEM((2, TOP_K, tb * PITCH, LANES), u32), pltpu.SemaphoreType.DMA((2,))],
    )
    return pl.pallas_call(
        functools.partial(_combine_kernel, tb=tb),
        grid_spec=grid_spec,
        out_shape=jax.ShapeDtypeStruct((t, d), f32),
        compiler_params=_cparams(("arbitrary",)),
        name="combine_final_norm",
    )(dest0, dest1, h, route, ys, norm_w)


def kernel(x, positions, norm_mix_w, w_in, fox_b_f, ret_gn_w, w_branch, b_gate, w_out, norm_ffn_w, w_router_group,
           b_router_group, w_router_expert, b_router_expert, w1, w3, w2, norm_final_w):
    b, s, d = x.shape
    assert d == D_MODEL and norm_mix_w.shape[0] == 1
    t = b * s
    x2 = x.reshape(t, d)

    wt = w_in[0].T
    o_f = 3 * FOX_WIDTH
    o_r = o_f + FOX_HEADS
    o_g = o_r + 4 * RET_WIDTH
    tn = PROJ_COLS
    tile_rows = ([o_g + k * tn for k in range(N_BRANCH * D_MODEL // tn)]
                 + [k * tn for k in range(o_f // tn)]
                 + [o_r + k * tn for k in range(4 * RET_WIDTH // tn)])
    w_main = _prep_in_weights(wt, tile_rows)
    w_f = jnp.pad(wt[o_f:o_r], ((0, LANES - FOX_HEADS), (0, 0))).astype(bf16)
    bias_f = jnp.pad(fox_b_f[0], (0, LANES - FOX_HEADS)).reshape(1, LANES)
    w_r = jnp.concatenate([w_router_group[0], w_router_expert[0]], axis=1)
    w_r = jnp.pad(w_r, ((0, 0), (0, LANES - w_r.shape[1])))
    wr_hi = w_r.astype(bf16)
    wr_lo = (w_r - wr_hi.astype(f32)).astype(bf16)
    b_r = jnp.pad(jnp.concatenate([b_router_group[0], b_router_expert[0]]), (0, LANES - N_GROUPS - N_EXPERTS))
    b_r = b_r.reshape(1, LANES)

    cos, sin = _rope_tables(positions)
    proj, f_logit = _in_projection(x2, norm_mix_w, w_main, w_f, cos.reshape(t, HEAD_DIM), sin.reshape(t, HEAD_DIM))
    c = _forget_cumsum(f_logit, bias_f, b, s)
    fox_o = _fox_attention(proj, c.reshape(b * FOX_HEADS, 1, s), b, s)
    ret_o = _retention(proj, ret_gn_w, b, s)
    merged = _branch_merge(fox_o, ret_o, proj, b_gate[0], w_branch[0].astype(bf16))
    h, h_slabs, route, counts = _outproj_router(merged, x2, w_out[0].astype(bf16), norm_ffn_w,
                                        jnp.concatenate([wr_hi, wr_lo], axis=1), b_r)

    bm = MOE_BLOCK
    cap = t * TOP_K + N_EXPERTS * bm
    dest_rows, seg = _dispatch_plan(route, counts, bm)
    dest0 = dest_rows[:, 0, :].reshape(t)
    dest1 = dest_rows[:, 1, :].reshape(t)
    pends = seg[0, :N_EXPERTS].astype(jnp.int32)
    padded = seg[1, :N_EXPERTS].astype(jnp.int32)
    counts = seg[2, :N_EXPERTS].astype(jnp.int32)
    pstarts = pends - padded
    n_used = pends[-1:] // bm

    xs = _dispatch_rows(dest0, dest1, pstarts + counts, pends, n_used, h_slabs, cap)
    y = _expert_mlp(pstarts // bm, padded // bm, n_used, xs, w1[0], w3[0], w2[0], cap)
    out = _combine(dest0, dest1, h, route, y, norm_final_w.reshape(1, d))
    return out.reshape(b, s, d)
```

```python
import functools
import math

import numpy as np
import jax
import jax.numpy as jnp
from jax import lax
from jax.experimental import pallas as pl
from jax.experimental.pallas import tpu as pltpu

D_MODEL = 2048
HEAD_DIM = 128
FOX_HEADS = 8
RET_HEADS = 8
FOX_WIDTH = FOX_HEADS * HEAD_DIM
RET_WIDTH = RET_HEADS * HEAD_DIM
N_BRANCH = 2
N_GROUPS = 4
EXPERTS_PER_GROUP = 8
N_EXPERTS = N_GROUPS * EXPERTS_PER_GROUP
TOP_K = 2
D_EXPERT = 512
ROPE_BASE = 10000.0
EPS = 1e-6

LANES = 128
NEG_BIG = -1e30
LOG2E = math.log2(math.e)
COL_GATE = 0
COL_FQ = N_BRANCH * D_MODEL // HEAD_DIM
COL_FK = COL_FQ + FOX_HEADS
COL_FV = COL_FK + FOX_HEADS
COL_RQ = COL_FV + FOX_HEADS
COL_RK = COL_RQ + RET_HEADS
COL_RV = COL_RK + RET_HEADS
COL_RG = COL_RV + RET_HEADS

ATTN_BLOCK = 256
FOX_HEADS_PER_STEP = 4
RET_HEADS_PER_STEP = 4
RET_CHUNK = 256
MOE_BLOCK = 256
X_BUFS = 4
PAD_FILL_CHUNKS = (64, 8, 1)
PROJ_ROWS, PROJ_COLS = 1024, 1024
PROJ_TILE = 2048
MERGE_ROWS = 1024
OUTPROJ_ROWS = 512
PLAN_ROWS = 1024
DISPATCH_ROWS = 2048
COMBINE_ROWS = 256
VMEM_LIMIT = 56 * 1024 * 1024

f32 = jnp.float32
bf16 = jnp.bfloat16
_NT = (((1,), (1,)), ((), ()))
SLAB = D_MODEL // (2 * LANES)
PITCH = SLAB + 1
u32 = jnp.uint32
_HI_MASK = np.uint32(0xFFFF0000)


def _rows_to_slabs(ref2, val, n_rows):
    bits = lambda v: lax.bitcast_convert_type(v.astype(bf16).astype(f32), u32)
    for j in range(SLAB):
        lo = val[:, (2 * j) * LANES:(2 * j + 1) * LANES]
        hi = val[:, (2 * j + 1) * LANES:(2 * j + 2) * LANES]
        ref2[pl.ds(j, n_rows, stride=PITCH), :] = (bits(lo) >> 16) | (bits(hi) & _HI_MASK)
    ref2[pl.ds(SLAB, n_rows, stride=PITCH), :] = jnp.zeros((n_rows, LANES), u32)


def _slabs_to_rows(ref2, n_rows):
    chunks = []
    for j in range(SLAB):
        w = ref2[pl.ds(j, n_rows, stride=PITCH), :]
        chunks.append(lax.bitcast_convert_type(w << 16, f32))
        chunks.append(lax.bitcast_convert_type(w & _HI_MASK, f32))
    return jnp.concatenate(chunks, axis=-1)


def _slab_gather_start(idx_ref, base, n_rows, src_hbm, dst, sem, unroll=8):
    def body(g, _):
        for u in range(unroll):
            r = g * unroll + u
            t = idx_ref[base + r]
            pltpu.make_async_copy(src_hbm.at[pl.ds(t * PITCH, SLAB), :], dst.at[pl.ds(r * PITCH, SLAB), :],
                                  sem).start(priority=u % 2)
        return 0
    lax.fori_loop(0, n_rows // unroll, body, 0)


def _slab_gather_wait(n_rows, src_hbm, dst, sem):
    pltpu.make_async_copy(src_hbm.at[pl.ds(0, n_rows * SLAB), :], dst.at[pl.ds(0, n_rows * SLAB), :], sem).wait()


def _cparams(sem, vmem=VMEM_LIMIT):
    return pltpu.CompilerParams(dimension_semantics=sem, vmem_limit_bytes=vmem)


def _const_spec(shape):
    return pl.BlockSpec(shape, lambda i: tuple(0 for _ in shape), pipeline_mode=pl.Buffered(1))


def _rope_kernel(pos_ref, freq_ref, sgn_ref, cos_ref, sin_ref):
    pos = pos_ref[0].astype(f32)
    hs = pos.shape[0] // 2
    half = HEAD_DIM // 2
    lane = lax.broadcasted_iota(jnp.int32, (hs, HEAD_DIM), 1)
    low = lane < half
    ang = jnp.where(low, pos[:hs], pos[hs:]) * freq_ref[...]
    c = jnp.cos(ang)
    s = jnp.sin(ang) * sgn_ref[...]
    c_sw = pltpu.roll(c, half, axis=1)
    s_sw = pltpu.roll(s, half, axis=1)
    cos_ref[0, :hs, :] = jnp.where(low, c, c_sw)
    cos_ref[0, hs:, :] = jnp.where(low, c_sw, c)
    sin_ref[0, :hs, :] = jnp.where(low, s, -s_sw)
    sin_ref[0, hs:, :] = jnp.where(low, -s_sw, s)


def _rope_tables(positions):
    b, s = positions.shape
    half = HEAD_DIM // 2
    inv = (np.float32(ROPE_BASE) ** (-np.arange(half, dtype=np.float32) / np.float32(half))).astype(np.float32)
    freq = jnp.asarray(np.concatenate([inv, inv])[None, :])
    sgn = jnp.asarray(np.concatenate([-np.ones(half, np.float32), np.ones(half, np.float32)])[None, :])
    out = jax.ShapeDtypeStruct((b, s, HEAD_DIM), f32)
    return pl.pallas_call(
        _rope_kernel,
        grid=(b,),
        in_specs=[pl.BlockSpec((1, s, 1), lambda i: (i, 0, 0)),
                  pl.BlockSpec((1, HEAD_DIM), lambda i: (0, 0)),
                  pl.BlockSpec((1, HEAD_DIM), lambda i: (0, 0))],
        out_specs=[pl.BlockSpec((1, s, HEAD_DIM), lambda i: (i, 0, 0)),
                   pl.BlockSpec((1, s, HEAD_DIM), lambda i: (i, 0, 0))],
        out_shape=[out, out],
        compiler_params=_cparams(("parallel",)),
        name="rope_tables",
    )(positions.reshape(b, s, 1), freq, sgn)


_Q_FACTORS = ((COL_FQ * HEAD_DIM // PROJ_COLS, LOG2E / math.sqrt(HEAD_DIM)),
              (COL_RQ * HEAD_DIM // PROJ_COLS, 1.0 / math.sqrt(HEAD_DIM)))


def _wprep_kernel(off_ref, w_hbm, o_ref, buf, sem, *, tn):
    j = pl.program_id(0)
    n = pl.num_programs(0)
    slot = j % 2

    def copy(jj, s):
        src = w_hbm.at[pl.ds(pl.multiple_of(off_ref[jj], 8), tn), :]
        return pltpu.make_async_copy(src, buf.at[s], sem.at[s])

    @pl.when(j == 0)
    def _():
        copy(0, 0).start()

    @pl.when(j + 1 < n)
    def _():
        copy(j + 1, 1 - slot).start()

    copy(j, slot).wait()
    factor = jnp.float32(1.0)
    for tile, value in _Q_FACTORS:
        factor = jnp.where(j == tile, jnp.float32(value), factor)
    o_ref[...] = (buf[slot] * factor).astype(o_ref.dtype)


def _prep_in_weights(wt, row_offsets):
    tn = PROJ_COLS
    d = wt.shape[1]
    n_tiles = len(row_offsets)
    grid_spec = pltpu.PrefetchScalarGridSpec(
        num_scalar_prefetch=1,
        grid=(n_tiles,),
        in_specs=[pl.BlockSpec(memory_space=pl.ANY)],
        out_specs=pl.BlockSpec((tn, d), lambda j, off: (j, 0)),
        scratch_shapes=[pltpu.VMEM((2, tn, d), f32), pltpu.SemaphoreType.DMA((2,))],
    )
    return pl.pallas_call(
        functools.partial(_wprep_kernel, tn=tn),
        grid_spec=grid_spec,
        out_shape=jax.ShapeDtypeStruct((n_tiles * tn, d), bf16),
        compiler_params=_cparams(("arbitrary",)),
        name="prep_in_weights",
    )(jnp.asarray(row_offsets, jnp.int32), wt)


_ROT_COLS = (COL_RQ * HEAD_DIM, (COL_RK + RET_HEADS) * HEAD_DIM)


def _inproj_kernel(x_ref, nw_ref, wf_ref, cos_ref, sin_ref, w_hbm, f_ref, o_hbm, xn_ref, wbuf, obuf, wsem, osem,
                   *, n_cols):
    i = pl.program_id(0)
    n_steps = pl.num_programs(0)
    tm = x_ref.shape[0]
    tn = wbuf.shape[1]
    n_tiles = pl.cdiv(n_cols, tn)
    assert n_tiles >= 2 and n_tiles % 2 == 0
    width = lambda j: min(tn, n_cols - j * tn)

    def w_copy(j):
        return pltpu.make_async_copy(w_hbm.at[pl.ds(j * tn, width(j)), :], wbuf.at[j % 2, pl.ds(0, width(j)), :],
                                     wsem.at[j % 2])

    def o_copy(step, j):
        dst = o_hbm.at[pl.ds(pl.multiple_of(step * tm, tm), tm), pl.ds(j * tn, width(j))]
        return pltpu.make_async_copy(obuf.at[j % 2, :, pl.ds(0, width(j))], dst, osem.at[j % 2])

    @pl.when(i == 0)
    def _():
        w_copy(0).start()

    x = x_ref[...]
    ms = jnp.mean(x * x, axis=-1, keepdims=True)
    xn = (x * lax.rsqrt(ms + EPS) * nw_ref[...]).astype(bf16)
    xn_ref[...] = xn
    f_ref[...] = lax.dot_general(xn, wf_ref[...], _NT, preferred_element_type=f32)

    @pl.when(i > 0)
    def _():
        o_copy(i - 1, n_tiles - 2).wait()
        o_copy(i - 1, n_tiles - 1).wait()

    for j in range(n_tiles):
        w_copy(j).wait()
        w_copy((j + 1) % n_tiles).start()
        if j >= 2:
            o_copy(i, j - 2).wait()
        wj = width(j)
        acc = lax.dot_general(xn_ref[...], wbuf[j % 2, :wj, :], _NT, preferred_element_type=f32)
        lo = max(_ROT_COLS[0] - j * tn, 0)
        hi = min(_ROT_COLS[1] - j * tn, wj)
        if lo >= hi:
            lo = hi = 0
        if lo:
            obuf[j % 2, :, :lo] = acc[:, :lo].astype(obuf.dtype)
        for c0 in range(lo, hi, HEAD_DIM):
            t = acc[:, c0:c0 + HEAD_DIM]
            rot = t * cos_ref[...] + pltpu.roll(t, HEAD_DIM // 2, axis=1) * sin_ref[...]
            obuf[j % 2, :, c0:c0 + HEAD_DIM] = rot.astype(obuf.dtype)
        if hi < wj:
            obuf[j % 2, :, hi:wj] = acc[:, hi:].astype(obuf.dtype)
        o_copy(i, j).start()

    @pl.when(i == n_steps - 1)
    def _():
        w_copy(0).wait()
        o_copy(i, n_tiles - 2).wait()
        o_copy(i, n_tiles - 1).wait()


def _in_projection(x2, norm_w, w_main_t, w_f_t, cos, sin):
    t, d = x2.shape
    n = w_main_t.shape[0]
    tm = min(PROJ_ROWS, t)
    tn = PROJ_TILE
    return pl.pallas_call(
        functools.partial(_inproj_kernel, n_cols=n),
        grid=(t // tm,),
        in_specs=[pl.BlockSpec((tm, d), lambda i: (i, 0)),
                  pl.BlockSpec((1, d), lambda i: (0, 0)),
                  pl.BlockSpec((LANES, d), lambda i: (0, 0)),
                  pl.BlockSpec((tm, HEAD_DIM), lambda i: (i, 0)),
                  pl.BlockSpec((tm, HEAD_DIM), lambda i: (i, 0)),
                  pl.BlockSpec(memory_space=pl.ANY)],
        out_specs=[pl.BlockSpec((tm, LANES), lambda i: (i, 0)),
                   pl.BlockSpec(memory_space=pl.ANY)],
        out_shape=[jax.ShapeDtypeStruct((t, LANES), f32), jax.ShapeDtypeStruct((t, n), bf16)],
        scratch_shapes=[pltpu.VMEM((tm, d), bf16), pltpu.VMEM((2, tn, d), bf16), pltpu.VMEM((2, tm, tn), bf16),
                        pltpu.SemaphoreType.DMA((2,)), pltpu.SemaphoreType.DMA((2,))],
        compiler_params=_cparams(("arbitrary",)),
        name="in_projection",
    )(x2, norm_w, w_f_t, cos, sin, w_main_t)


def _fcum_kernel(f_ref, b_ref, c_ref):
    z = f_ref[...] + b_ref[...]
    ls = jnp.minimum(z, 0.0) - jnp.log1p(jnp.exp(-jnp.abs(z)))
    c = ls.T[:FOX_HEADS, :]
    s = c.shape[1]
    lane = lax.broadcasted_iota(jnp.int32, c.shape, 1)
    sh = 1
    while sh < s:
        c = c + jnp.where(lane >= sh, pltpu.roll(c, sh, axis=1), 0.0)
        sh *= 2
    c_ref[0] = c * LOG2E


def _forget_cumsum(f_logit, bias, b, s):
    return pl.pallas_call(
        _fcum_kernel,
        grid=(b,),
        in_specs=[pl.BlockSpec((s, LANES), lambda i: (i, 0)),
                  pl.BlockSpec((1, LANES), lambda i: (0, 0))],
        out_specs=pl.BlockSpec((1, FOX_HEADS, s), lambda i: (i, 0, 0)),
        out_shape=jax.ShapeDtypeStruct((b, FOX_HEADS, s), f32),
        compiler_params=_cparams(("parallel",)),
        name="forget_cumsum",
    )(f_logit, bias)


def _fox_kernel(q_ref, k_ref, v_ref, c_ref, o_ref, va_ref, acc_ref, *, tq):
    s_len = q_ref.shape[0]
    nq = s_len // tq
    n_heads = q_ref.shape[1] // HEAD_DIM
    row = lax.broadcasted_iota(jnp.int32, (tq, tq), 0)
    col = lax.broadcasted_iota(jnp.int32, (tq, tq), 1)
    causal = row >= col
    def scores(hh):
        lanes = slice(hh * HEAD_DIM, (hh + 1) * HEAD_DIM)
        va_ref[hh, :, :HEAD_DIM] = v_ref[:, lanes]
        va_ref[hh, :, HEAD_DIM:] = jnp.ones((s_len, HEAD_DIM), va_ref.dtype)
        panels = []
        row_max = [None] * nq
        for kb in range(nq):
            r0 = kb * tq
            s = lax.dot_general(q_ref[r0:, lanes], k_ref[r0:r0 + tq, lanes], _NT, preferred_element_type=f32)
            s = s - c_ref[hh, :, r0:r0 + tq]
            parts = [jnp.where(causal, s[:tq], NEG_BIG)]
            parts += [s[(j - kb) * tq:(j - kb + 1) * tq] for j in range(kb + 1, nq)]
            panels.append(parts)
            for j, part in zip(range(kb, nq), parts):
                pm = jnp.max(part, axis=-1, keepdims=True)
                row_max[j] = pm if row_max[j] is None else jnp.maximum(row_max[j], pm)
        return panels, row_max

    def values(hh, panels, row_max):
        lanes = slice(hh * HEAD_DIM, (hh + 1) * HEAD_DIM)
        for kb in range(nq):
            r0 = kb * tq
            p = jnp.concatenate([jnp.exp2(part - row_max[j]).astype(bf16)
                                 for j, part in zip(range(kb, nq), panels[kb])], axis=0)
            contrib = jnp.dot(p, va_ref[hh, r0:r0 + tq, :], preferred_element_type=f32)
            if kb:
                acc_ref[hh, r0:, :] += contrib
            else:
                acc_ref[hh] = contrib
        acc = acc_ref[hh]
        o_ref[:, lanes] = (acc[:, :HEAD_DIM] / acc[:, HEAD_DIM:]).astype(o_ref.dtype)

    pending = None
    for hh in range(n_heads):
        current = (hh,) + scores(hh)
        if pending is not None:
            values(*pending)
        pending = current
    values(*pending)


def _fox_attention(proj, c_blk, b, s):
    tq = min(ATTN_BLOCK, s)
    hp = FOX_HEADS_PER_STEP
    w = hp * HEAD_DIM
    col = lambda c0: pl.BlockSpec((s, w), lambda i, j: (i, c0 // hp + j))
    return pl.pallas_call(
        functools.partial(_fox_kernel, tq=tq),
        grid=(b, FOX_HEADS // hp),
        in_specs=[col(COL_FQ), col(COL_FK), col(COL_FV),
                  pl.BlockSpec((hp, 1, s), lambda i, j: (i * (FOX_HEADS // hp) + j, 0, 0))],
        out_specs=pl.BlockSpec((s, w), lambda i, j: (i, j)),
        out_shape=jax.ShapeDtypeStruct((b * s, FOX_WIDTH), bf16),
        scratch_shapes=[pltpu.VMEM((hp, s, 2 * HEAD_DIM), bf16), pltpu.VMEM((hp, s, 2 * HEAD_DIM), f32)],
        compiler_params=_cparams(("parallel", "parallel")),
        name="fox_attention",
    )(proj, proj, proj, c_blk)


def _ret_kernel(q_ref, k_ref, v_ref, g_ref, dec_ref, xi_ref, zeta_ref, gc_ref, gnw_ref, o_ref, *, chunk):
    s_len = q_ref.shape[0]
    nc = s_len // chunk
    n_heads = q_ref.shape[1] // HEAD_DIM
    states = [None] * n_heads
    for ci in range(nc):
        rows = slice(ci * chunk, (ci + 1) * chunk)
        for hh in range(n_heads):
            lanes = slice(hh * HEAD_DIM, (hh + 1) * HEAD_DIM)
            state = states[hh]
            qb = q_ref[rows, lanes]
            kb = k_ref[rows, lanes]
            v = v_ref[rows, lanes]
            inner = lax.dot_general(qb, kb, _NT, preferred_element_type=f32) * dec_ref[hh]
            o = jnp.dot(inner.astype(bf16), v, preferred_element_type=f32)
            if ci:
                o = o + jnp.dot(qb, state.astype(bf16), preferred_element_type=f32) * xi_ref[hh]
            if ci + 1 < nc:
                kz = (kb.astype(f32) * zeta_ref[hh]).astype(bf16)
                kv = lax.dot_general(kz, v, (((0,), (0,)), ((), ())), preferred_element_type=f32)
                states[hh] = state * gc_ref[hh] + kv if ci else kv
            mu = jnp.mean(o, axis=-1, keepdims=True)
            oc = o - mu
            var = jnp.mean(oc * oc, axis=-1, keepdims=True)
            y = oc * lax.rsqrt(var + EPS) * gnw_ref[:, lanes]
            g = g_ref[rows, lanes].astype(f32)
            o_ref[rows, lanes] = (g * jax.nn.sigmoid(g) * y).astype(o_ref.dtype)


def _retention_constants(chunk):
    hh = np.arange(RET_HEADS, dtype=np.float64)
    log_gamma = np.log1p(-(2.0 ** (-5.0 - hh)))
    n = np.arange(chunk, dtype=np.float64)
    diff = n[:, None] - n[None, :]
    decay = np.where(diff[None] >= 0, np.exp(diff[None] * log_gamma[:, None, None]), 0.0)
    xi = np.exp((n[None, :] + 1.0) * log_gamma[:, None])
    zeta = np.exp((chunk - 1.0 - n[None, :]) * log_gamma[:, None])
    g_chunk = np.exp(chunk * log_gamma)
    bc = lambda a: np.broadcast_to(a[:, :, None], (RET_HEADS, chunk, HEAD_DIM))
    gc = np.broadcast_to(g_chunk[:, None, None], (RET_HEADS, 1, HEAD_DIM))
    to = lambda a: jnp.asarray(np.ascontiguousarray(a), dtype=f32)
    return to(decay), to(bc(xi)), to(bc(zeta)), to(gc)


def _retention(proj, gn_w, b, s):
    chunk = min(RET_CHUNK, s)
    decay, xi, zeta, gc = _retention_constants(chunk)
    hp = RET_HEADS_PER_STEP
    w = hp * HEAD_DIM
    head = lambda col: pl.BlockSpec((s, w), lambda i, j: (i, col // hp + j))
    per_head = lambda shape: pl.BlockSpec((hp,) + shape, lambda i, j: (j, 0, 0))
    return pl.pallas_call(
        functools.partial(_ret_kernel, chunk=chunk),
        grid=(b, RET_HEADS // hp),
        in_specs=[head(COL_RQ), head(COL_RK), head(COL_RV), head(COL_RG),
                  per_head((chunk, chunk)), per_head((chunk, HEAD_DIM)), per_head((chunk, HEAD_DIM)),
                  per_head((1, HEAD_DIM)),
                  pl.BlockSpec((1, w), lambda i, j: (0, j))],
        out_specs=pl.BlockSpec((s, w), lambda i, j: (i, j)),
        out_shape=jax.ShapeDtypeStruct((b * s, RET_WIDTH), bf16),
        compiler_params=_cparams(("parallel", "parallel")),
        name="retention",
    )(proj, proj, proj, proj, decay, xi, zeta, gc, gn_w)


def _branch_merge_kernel(fo_ref, ro_ref, g0_ref, g1_ref, bg_ref, wb_ref, mg_ref, *, ncol):
    d = mg_ref.shape[1]
    cw = d // ncol
    fo = fo_ref[...]
    ro = ro_ref[...]
    for c in range(ncol):
        cs = slice(c * cw, (c + 1) * cw)
        bd0 = jnp.dot(fo, wb_ref[0, :, cs], preferred_element_type=f32)
        bd1 = jnp.dot(ro, wb_ref[1, :, cs], preferred_element_type=f32)
        ga = jax.nn.sigmoid(g0_ref[:, cs].astype(f32) + bg_ref[0:1, cs])
        gb = jax.nn.sigmoid(g1_ref[:, cs].astype(f32) + bg_ref[1:2, cs])
        mg_ref[:, cs] = (ga * bd0 + gb * bd1).astype(mg_ref.dtype)


def _branch_merge(fox_o, ret_o, proj, b_gate, w_branch):
    t = fox_o.shape[0]
    d = D_MODEL
    tm = min(MERGE_ROWS, t)
    gate0 = COL_GATE * HEAD_DIM // d
    return pl.pallas_call(
        functools.partial(_branch_merge_kernel, ncol=4),
        grid=(t // tm,),
        in_specs=[pl.BlockSpec((tm, FOX_WIDTH), lambda i: (i, 0)),
                  pl.BlockSpec((tm, RET_WIDTH), lambda i: (i, 0)),
                  pl.BlockSpec((tm, d), lambda i: (i, gate0)),
                  pl.BlockSpec((tm, d), lambda i: (i, gate0 + 1)),
                  _const_spec((N_BRANCH, d)),
                  _const_spec((N_BRANCH, FOX_WIDTH, d))],
        out_specs=pl.BlockSpec((tm, d), lambda i: (i, 0)),
        out_shape=jax.ShapeDtypeStruct((t, d), bf16),
        compiler_params=_cparams(("parallel",)),
        name="branch_merge",
    )(fox_o, ret_o, proj, proj, b_gate, w_branch)


def _route(logits):
    lane = lax.broadcasted_iota(jnp.int32, logits.shape, 1).astype(f32)
    far = jnp.float32(1e9)
    gmask = lane < N_GROUPS
    gl = jnp.where(gmask, logits, NEG_BIG)
    gmax = jnp.max(gl, axis=-1, keepdims=True)
    gsel = jnp.min(jnp.where(gl == gmax, lane, far), axis=-1, keepdims=True)
    p_group = 1.0 / jnp.sum(jnp.where(gmask, jnp.exp(gl - gmax), 0.0), axis=-1, keepdims=True)
    lo = N_GROUPS + EXPERTS_PER_GROUP * gsel
    emask = (lane >= lo) & (lane < lo + EXPERTS_PER_GROUP)
    el = jnp.where(emask, logits, NEG_BIG)
    m1 = jnp.max(el, axis=-1, keepdims=True)
    i1 = jnp.min(jnp.where(el == m1, lane, far), axis=-1, keepdims=True)
    el2 = jnp.where(lane == i1, NEG_BIG, el)
    m2 = jnp.max(el2, axis=-1, keepdims=True)
    i2 = jnp.min(jnp.where(el2 == m2, lane, far), axis=-1, keepdims=True)
    t = jnp.exp(m2 - m1)
    w1 = p_group / (1.0 + t)
    w2 = p_group * t / (1.0 + t)
    out = jnp.where(lane == 0, i1 - N_GROUPS, 0.0)
    out = jnp.where(lane == 1, i2 - N_GROUPS, out)
    out = jnp.where(lane == 2, w1, out)
    out = jnp.where(lane == 3, w2, out)
    return out


def _outproj_kernel(mg_ref, x_ref, wo_ref, nfw_ref, wr_ref, br_ref, h_ref, hs_ref, route_ref, cnt_ref):
    h = x_ref[...] + jnp.dot(mg_ref[...], wo_ref[...], preferred_element_type=f32)
    h_ref[...] = h
    hn = h * lax.rsqrt(jnp.mean(h * h, axis=-1, keepdims=True) + EPS) * nfw_ref[...]
    _rows_to_slabs(hs_ref, hn, hn.shape[0])
    hi = hn.astype(bf16)
    lo = (hn - hi.astype(f32)).astype(bf16)
    hw = jnp.dot(hi, wr_ref[...], preferred_element_type=f32)
    logits = (hw[:, :LANES] + jnp.dot(lo, wr_ref[:, :LANES], preferred_element_type=f32) + hw[:, LANES:]) + br_ref[...]
    route = _route(logits)
    route_ref[...] = route

    @pl.when(pl.program_id(0) == 0)
    def _():
        cnt_ref[...] = jnp.zeros_like(cnt_ref)

    lane = lax.broadcasted_iota(jnp.int32, route.shape, 1).astype(f32)
    chosen = (lane == route[:, 0:1]).astype(f32) + (lane == route[:, 1:2]).astype(f32)
    cnt_ref[...] += jnp.sum(chosen, axis=0, keepdims=True)


def _outproj_router(merged, x2, w_out, nfw, wr_cat, b_router):
    t, d = x2.shape
    tm = min(OUTPROJ_ROWS, t)
    return pl.pallas_call(
        _outproj_kernel,
        grid=(t // tm,),
        in_specs=[pl.BlockSpec((tm, d), lambda i: (i, 0)),
                  pl.BlockSpec((tm, d), lambda i: (i, 0)),
                  _const_spec((d, d)),
                  _const_spec((1, d)),
                  _const_spec((d, 2 * LANES)), _const_spec((1, LANES))],
        out_specs=[pl.BlockSpec((tm, d), lambda i: (i, 0)),
                   pl.BlockSpec((tm * PITCH, LANES), lambda i: (i, 0)),
                   pl.BlockSpec((tm, LANES), lambda i: (i, 0)),
                   pl.BlockSpec((8, LANES), lambda i: (0, 0))],
        out_shape=[jax.ShapeDtypeStruct((t, d), f32), jax.ShapeDtypeStruct((t * PITCH, LANES), u32),
                   jax.ShapeDtypeStruct((t, LANES), f32), jax.ShapeDtypeStruct((8, LANES), f32)],
        compiler_params=_cparams(("arbitrary",)),
        name="outproj_router",
    )(merged, x2, w_out, nfw, wr_cat, b_router)


def _plan_kernel(route_ref, cnt_ref, dest_ref, ends_ref, carry_ref, *, bm):
    r = route_ref[...]
    tb = r.shape[0]
    lane = lax.broadcasted_iota(jnp.int32, r.shape, 1).astype(f32)
    o0 = (lane == r[:, 0:1]).astype(f32)
    o1 = (lane == r[:, 1:2]).astype(f32)
    o = o0 + o1

    @pl.when(pl.program_id(0) == 0)
    def _():
        cnt = cnt_ref[...]
        padded = jnp.floor((cnt + (bm - 1)) * (1.0 / bm)) * bm
        lane8 = lax.broadcasted_iota(jnp.int32, cnt.shape, 1)
        ends = padded
        sh = 1
        while sh < LANES:
            ends = ends + jnp.where(lane8 >= sh, pltpu.roll(ends, sh, axis=1), 0.0)
            sh *= 2
        sub = lax.broadcasted_iota(jnp.int32, cnt.shape, 0)
        ends_ref[...] = jnp.where(sub == 0, ends, jnp.where(sub == 1, padded, cnt))
        carry_ref[...] = ends - padded

    row = lax.broadcasted_iota(jnp.int32, (tb, tb), 0)
    col = lax.broadcasted_iota(jnp.int32, (tb, tb), 1)
    lower = jnp.where(col < row, 1.0, 0.0).astype(bf16)
    before = jnp.dot(lower, o.astype(bf16), preferred_element_type=f32) + carry_ref[0:1, :]
    d0 = jnp.sum(before * o0, axis=-1, keepdims=True)
    d1 = jnp.sum(before * o1, axis=-1, keepdims=True)
    dd = jnp.where(lane == 0, d0, jnp.where(lane == 1, d1, 0.0))
    dest_ref[0] = dd.T[:8, :].astype(jnp.int32)
    carry_ref[...] += jnp.sum(o, axis=0, keepdims=True)


def _dispatch_plan(route, counts, bm):
    t = route.shape[0]
    tb = min(PLAN_ROWS, t)
    nb = t // tb
    return pl.pallas_call(
        functools.partial(_plan_kernel, bm=bm),
        grid=(nb,),
        in_specs=[pl.BlockSpec((tb, LANES), lambda i: (i, 0)),
                  pl.BlockSpec((8, LANES), lambda i: (0, 0))],
        out_specs=[pl.BlockSpec((1, 8, tb), lambda i: (i, 0, 0)),
                   pl.BlockSpec((8, LANES), lambda i: (0, 0))],
        out_shape=[jax.ShapeDtypeStruct((nb, 8, tb), jnp.int32), jax.ShapeDtypeStruct((8, LANES), f32)],
        scratch_shapes=[pltpu.VMEM((8, LANES), f32)],
        compiler_params=_cparams(("arbitrary",)),
        name="dispatch_plan",
    )(route, counts)


def _dispatch_kernel(d0_ref, d1_ref, fill_ref, end_ref, nu_ref, hs_ref, xs_hbm, zbuf, sem, zsem,
                     *, tb, bm, n_blk, unroll=8):
    i = pl.program_id(0)
    base = i * tb
    blk_rows = bm * PITCH

    def pad_copy(s, n):
        return pltpu.make_async_copy(zbuf.at[pl.ds(0, n * PITCH), :], xs_hbm.at[pl.ds(s * PITCH, n * PITCH), :], zsem)

    def tail_copy(g):
        return pltpu.make_async_copy(zbuf, xs_hbm.at[pl.ds(pl.multiple_of(g * blk_rows, 8), blk_rows), :], zsem)

    @pl.when(i == 0)
    def _():
        zbuf[...] = jnp.zeros(zbuf.shape, u32)
        for op in ("start", "wait"):
            def per_expert(e, c, op=op):
                s = fill_ref[e]
                for n in PAD_FILL_CHUNKS:
                    count = (end_ref[e] - s) // n

                    def one(k, c2, s=s, n=n):
                        getattr(pad_copy(s + k * n, n), op)()
                        return c2

                    c = lax.fori_loop(0, count, one, c)
                    s = s + count * n
                return c

            def per_tail(g, c, op=op):
                getattr(tail_copy(g), op)()
                return c

            lax.fori_loop(0, N_EXPERTS, per_expert, 0)
            lax.fori_loop(nu_ref[0], n_blk, per_tail, 0)

    def body(g, c):
        for u in range(unroll):
            r = g * unroll + u
            src = hs_ref.at[pl.ds(r * PITCH, PITCH), :]
            for k, dref in enumerate((d0_ref, d1_ref)):
                d = dref[base + r]
                pltpu.make_async_copy(src, xs_hbm.at[pl.ds(d * PITCH, PITCH), :], sem).start(priority=k)
        return c

    lax.fori_loop(0, tb // unroll, body, 0)
    for _ in range(TOP_K):
        pltpu.make_async_copy(xs_hbm.at[pl.ds(0, tb * PITCH), :], xs_hbm.at[pl.ds(0, tb * PITCH), :], sem).wait()


def _dispatch_rows(dest0, dest1, fill_start, seg_end, n_used, hs, cap):
    t = dest0.shape[0]
    tb = min(DISPATCH_ROWS, t)
    bm = MOE_BLOCK
    grid_spec = pltpu.PrefetchScalarGridSpec(
        num_scalar_prefetch=5,
        grid=(t // tb,),
        in_specs=[pl.BlockSpec((tb * PITCH, LANES), lambda i, *_: (i, 0))],
        out_specs=pl.BlockSpec(memory_space=pl.ANY),
        scratch_shapes=[pltpu.VMEM((bm * PITCH, LANES), u32), pltpu.SemaphoreType.DMA(()),
                        pltpu.SemaphoreType.DMA(())],
    )
    return pl.pallas_call(
        functools.partial(_dispatch_kernel, tb=tb, bm=bm, n_blk=cap // bm),
        grid_spec=grid_spec,
        out_shape=jax.ShapeDtypeStruct((cap * PITCH, LANES), u32),
        compiler_params=_cparams(("arbitrary",)),
        name="dispatch_rows",
    )(dest0, dest1, fill_start, seg_end, n_used, hs)


def _moe_kernel(g0_ref, nb_ref, nu_ref, x_hbm, w1_ref, w3_ref, w2_ref, y_hbm,
                xbuf, ybuf, w1b, w3b, w2b, gsem, ysem, *, bm, n_blk):
    e = pl.program_id(0)
    n_used = nu_ref[0]
    g0 = g0_ref[e]
    nb = nb_ref[e]
    blk_rows = bm * PITCH

    def x_copy(slot, g):
        src = x_hbm.at[pl.ds(pl.multiple_of(g * blk_rows, 8), blk_rows), :]
        return pltpu.make_async_copy(src, xbuf.at[slot], gsem.at[slot])

    def y_copy(slot, g):
        dst = y_hbm.at[pl.ds(pl.multiple_of(g * blk_rows, 8), blk_rows), :]
        return pltpu.make_async_copy(ybuf.at[slot], dst, ysem.at[slot])

    @pl.when(e == 0)
    def _():
        x_copy(0, 0).start()
        for k in range(1, X_BUFS - 1):
            pl.when(n_used > k)(lambda k=k: x_copy(k, k).start())

    def run_block(g, cast_weights):
        slot = g % 2
        xslot = g % X_BUFS
        ahead = g + (X_BUFS - 1)

        @pl.when(ahead < n_used)
        def _():
            x_copy(ahead % X_BUFS, ahead).start()

        x_copy(xslot, g).wait()

        if cast_weights:
            w1b[...] = w1_ref[0].astype(bf16)
        hn = _slabs_to_rows(xbuf.at[xslot], bm).astype(bf16)
        a = jnp.dot(hn, w1b[...], preferred_element_type=f32)
        if cast_weights:
            w3b[...] = w3_ref[0].astype(bf16)
        b = jnp.dot(hn, w3b[...], preferred_element_type=f32)
        if cast_weights:
            w2b[...] = w2_ref[0].astype(bf16)
        mid = (a * jax.nn.sigmoid(a) * b).astype(bf16)
        y = jnp.dot(mid, w2b[...], preferred_element_type=f32)

        @pl.when(g >= 2)
        def _():
            y_copy(slot, g - 2).wait()

        _rows_to_slabs(ybuf.at[slot], y, bm)
        y_copy(slot, g).start()

    @pl.when(nb > 0)
    def _():
        run_block(g0, True)

    def block(c, carry):
        run_block(g0 + c, False)
        return carry

    lax.fori_loop(1, nb, block, 0)

    @pl.when(e == pl.num_programs(0) - 1)
    def _():
        @pl.when(n_used >= 2)
        def _():
            y_copy(n_used % 2, n_used - 2).wait()

        y_copy((n_used - 1) % 2, n_used - 1).wait()
        ybuf[0] = jnp.zeros(ybuf.shape[1:], u32)

        def zstart(g, carry):
            y_copy(0, g).start()
            return carry

        def zwait(g, carry):
            y_copy(0, g).wait()
            return carry

        lax.fori_loop(n_used, n_blk, zstart, 0)
        lax.fori_loop(n_used, n_blk, zwait, 0)


def _expert_mlp(blk_start, blk_count, n_used, xs, w1, w3, w2, cap):
    d = D_MODEL
    bm = MOE_BLOCK
    n_blk = cap // bm
    by_expert = lambda shape: pl.BlockSpec(shape, lambda e, g0, nb, nu: (e, 0, 0))
    grid_spec = pltpu.PrefetchScalarGridSpec(
        num_scalar_prefetch=3,
        grid=(N_EXPERTS,),
        in_specs=[pl.BlockSpec(memory_space=pl.ANY),
                  by_expert((1, d, D_EXPERT)), by_expert((1, d, D_EXPERT)), by_expert((1, D_EXPERT, d))],
        out_specs=pl.BlockSpec(memory_space=pl.ANY),
        scratch_shapes=[pltpu.VMEM((X_BUFS, bm * PITCH, LANES), u32), pltpu.VMEM((2, bm * PITCH, LANES), u32),
                        pltpu.VMEM((d, D_EXPERT), bf16), pltpu.VMEM((d, D_EXPERT), bf16),
                        pltpu.VMEM((D_EXPERT, d), bf16),
                        pltpu.SemaphoreType.DMA((X_BUFS,)), pltpu.SemaphoreType.DMA((2,))],
    )
    return pl.pallas_call(
        functools.partial(_moe_kernel, bm=bm, n_blk=n_blk),
        grid_spec=grid_spec,
        out_shape=jax.ShapeDtypeStruct((cap * PITCH, LANES), u32),
        compiler_params=_cparams(("arbitrary",)),
        name="expert_mlp",
    )(blk_start, blk_count, n_used, xs, w1, w3, w2)


def _combine_kernel(d0_ref, d1_ref, h_ref, route_ref, y_hbm, nw_ref, o_ref, ybuf, sem, *, tb):
    i = pl.program_id(0)
    n = pl.num_programs(0)
    slot = i % 2

    def start(blk, s):
        _slab_gather_start(d0_ref, blk * tb, tb, y_hbm, ybuf.at[s, 0], sem.at[s])
        _slab_gather_start(d1_ref, blk * tb, tb, y_hbm, ybuf.at[s, 1], sem.at[s])

    @pl.when(i == 0)
    def _():
        start(0, 0)

    @pl.when(i + 1 < n)
    def _():
        start(i + 1, 1 - slot)

    for k in range(TOP_K):
        _slab_gather_wait(tb, y_hbm, ybuf.at[slot, k], sem.at[slot])
    r = route_ref[...]
    h = h_ref[...] + (r[:, 2:3] * _slabs_to_rows(ybuf.at[slot, 0], tb)
                      + r[:, 3:4] * _slabs_to_rows(ybuf.at[slot, 1], tb))
    o_ref[...] = h * lax.rsqrt(jnp.mean(h * h, axis=-1, keepdims=True) + EPS) * nw_ref[...]


def _combine(dest0, dest1, h, route, ys, norm_w):
    t, d = h.shape
    tb = min(COMBINE_ROWS, t)
    grid_spec = pltpu.PrefetchScalarGridSpec(
        num_scalar_prefetch=2,
        grid=(t // tb,),
        in_specs=[pl.BlockSpec((tb, d), lambda i, a, b: (i, 0)),
                  pl.BlockSpec((tb, LANES), lambda i, a, b: (i, 0)),
                  pl.BlockSpec(memory_space=pl.ANY),
                  pl.BlockSpec((1, d), lambda i, a, b: (0, 0))],
        out_specs=pl.BlockSpec((tb, d), lambda i, a, b: (i, 0)),
        scratch_shapes=[pltpu.VMEM((2, TOP_K, tb * PITCH, LANES), u32), pltpu.SemaphoreType.DMA((2,))],
    )
    return pl.pallas_call(
        functools.partial(_combine_kernel, tb=tb),
        grid_spec=grid_spec,
        out_shape=jax.ShapeDtypeStruct((t, d), f32),
        compiler_params=_cparams(("arbitrary",)),
        name="combine_final_norm",
    )(dest0, dest1, h, route, ys, norm_w)


def kernel(x, positions, norm_mix_w, w_in, fox_b_f, ret_gn_w, w_branch, b_gate, w_out, norm_ffn_w, w_router_group,
           b_router_group, w_router_expert, b_router_expert, w1, w3, w2, norm_final_w):
    b, s, d = x.shape
    assert d == D_MODEL and norm_mix_w.shape[0] == 1
    t = b * s
    x2 = x.reshape(t, d)

    wt = w_in[0].T
    o_f = 3 * FOX_WIDTH
    o_r = o_f + FOX_HEADS
    o_g = o_r + 4 * RET_WIDTH
    tn = PROJ_COLS
    tile_rows = ([o_g + k * tn for k in range(N_BRANCH * D_MODEL // tn)]
                 + [k * tn for k in range(o_f // tn)]
                 + [o_r + k * tn for k in range(4 * RET_WIDTH // tn)])
    w_main = _prep_in_weights(wt, tile_rows)
    w_f = jnp.pad(wt[o_f:o_r], ((0, LANES - FOX_HEADS), (0, 0))).astype(bf16)
    bias_f = jnp.pad(fox_b_f[0], (0, LANES - FOX_HEADS)).reshape(1, LANES)
    w_r = jnp.concatenate([w_router_group[0], w_router_expert[0]], axis=1)
    w_r = jnp.pad(w_r, ((0, 0), (0, LANES - w_r.shape[1])))
    wr_hi = w_r.astype(bf16)
    wr_lo = (w_r - wr_hi.astype(f32)).astype(bf16)
    b_r = jnp.pad(jnp.concatenate([b_router_group[0], b_router_expert[0]]), (0, LANES - N_GROUPS - N_EXPERTS))
    b_r = b_r.reshape(1, LANES)

    cos, sin = _rope_tables(positions)
    f_logit, proj = _in_projection(x2, norm_mix_w, w_main, w_f, cos.reshape(t, HEAD_DIM), sin.reshape(t, HEAD_DIM))
    c = _forget_cumsum(f_logit, bias_f, b, s)
    fox_o = _fox_attention(proj, c.reshape(b * FOX_HEADS, 1, s), b, s)
    ret_o = _retention(proj, ret_gn_w, b, s)
    merged = _branch_merge(fox_o, ret_o, proj, b_gate[0], w_branch[0].astype(bf16))
    h, h_slabs, route, counts = _outproj_router(merged, x2, w_out[0].astype(bf16), norm_ffn_w,
                                        jnp.concatenate([wr_hi, wr_lo], axis=1), b_r)

    bm = MOE_BLOCK
    cap = t * TOP_K + N_EXPERTS * bm
    dest_rows, seg = _dispatch_plan(route, counts, bm)
    dest0 = dest_rows[:, 0, :].reshape(t)
    dest1 = dest_rows[:, 1, :].reshape(t)
    pends = seg[0, :N_EXPERTS].astype(jnp.int32)
    padded = seg[1, :N_EXPERTS].astype(jnp.int32)
    counts = seg[2, :N_EXPERTS].astype(jnp.int32)
    pstarts = pends - padded
    n_used = pends[-1:] // bm

    xs = _dispatch_rows(dest0, dest1, pstarts + counts, pends, n_used, h_slabs, cap)
    y = _expert_mlp(pstarts // bm, padded // bm, n_used, xs, w1[0], w3[0], w2[0], cap)
    out = _combine(dest0, dest1, h, route, y, norm_final_w.reshape(1, d))
    return out.reshape(b, s, d)
```

```python
import functools
import math

import numpy as np
import jax
import jax.numpy as jnp
from jax import lax
from jax.experimental import pallas as pl
from jax.experimental.pallas import tpu as pltpu

D_MODEL = 2048
HEAD_DIM = 128
FOX_HEADS = 8
RET_HEADS = 8
FOX_WIDTH = FOX_HEADS * HEAD_DIM
RET_WIDTH = RET_HEADS * HEAD_DIM
N_BRANCH = 2
N_GROUPS = 4
EXPERTS_PER_GROUP = 8
N_EXPERTS = N_GROUPS * EXPERTS_PER_GROUP
TOP_K = 2
D_EXPERT = 512
ROPE_BASE = 10000.0
EPS = 1e-6

LANES = 128
NEG_BIG = -1e30
LOG2E = math.log2(math.e)
COL_GATE = 0
COL_FQ = N_BRANCH * D_MODEL // HEAD_DIM
COL_FK = COL_FQ + FOX_HEADS
COL_FV = COL_FK + FOX_HEADS
COL_RQ = COL_FV + FOX_HEADS
COL_RK = COL_RQ + RET_HEADS
COL_RV = COL_RK + RET_HEADS
COL_RG = COL_RV + RET_HEADS

ATTN_BLOCK = 256
FOX_HEADS_PER_STEP = 4
RET_HEADS_PER_STEP = 4
RET_CHUNK = 256
MOE_BLOCK = 256
X_BUFS = 4
PAD_FILL_CHUNKS = (64, 8, 1)
PROJ_ROWS, PROJ_COLS = 1024, 1024
PROJ_TILE = 2048
PROJ_W_SPLIT = 4
MERGE_ROWS = 1024
OUTPROJ_ROWS = 512
PLAN_ROWS = 1024
DISPATCH_ROWS = 2048
COMBINE_ROWS = 256
VMEM_LIMIT = 56 * 1024 * 1024

f32 = jnp.float32
bf16 = jnp.bfloat16
_NT = (((1,), (1,)), ((), ()))
SLAB = D_MODEL // (2 * LANES)
PITCH = SLAB + 1
u32 = jnp.uint32
_HI_MASK = np.uint32(0xFFFF0000)


def _rows_to_slabs(ref2, val, n_rows):
    bits = lambda v: lax.bitcast_convert_type(v.astype(bf16).astype(f32), u32)
    for j in range(SLAB):
        lo = val[:, (2 * j) * LANES:(2 * j + 1) * LANES]
        hi = val[:, (2 * j + 1) * LANES:(2 * j + 2) * LANES]
        ref2[pl.ds(j, n_rows, stride=PITCH), :] = (bits(lo) >> 16) | (bits(hi) & _HI_MASK)
    ref2[pl.ds(SLAB, n_rows, stride=PITCH), :] = jnp.zeros((n_rows, LANES), u32)


def _slabs_to_rows(ref2, n_rows):
    chunks = []
    for j in range(SLAB):
        w = ref2[pl.ds(j, n_rows, stride=PITCH), :]
        chunks.append(lax.bitcast_convert_type(w << 16, f32))
        chunks.append(lax.bitcast_convert_type(w & _HI_MASK, f32))
    return jnp.concatenate(chunks, axis=-1)


def _slab_gather_start(idx_ref, base, n_rows, src_hbm, dst, sem, unroll=8):
    def body(g, _):
        for u in range(unroll):
            r = g * unroll + u
            t = idx_ref[base + r]
            pltpu.make_async_copy(src_hbm.at[pl.ds(t * PITCH, SLAB), :], dst.at[pl.ds(r * PITCH, SLAB), :],
                                  sem).start(priority=u % 2)
        return 0
    lax.fori_loop(0, n_rows // unroll, body, 0)


def _slab_gather_wait(n_rows, src_hbm, dst, sem):
    pltpu.make_async_copy(src_hbm.at[pl.ds(0, n_rows * SLAB), :], dst.at[pl.ds(0, n_rows * SLAB), :], sem).wait()


def _cparams(sem, vmem=VMEM_LIMIT):
    return pltpu.CompilerParams(dimension_semantics=sem, vmem_limit_bytes=vmem)


def _const_spec(shape):
    return pl.BlockSpec(shape, lambda i: tuple(0 for _ in shape), pipeline_mode=pl.Buffered(1))


def _rope_kernel(pos_ref, freq_ref, sgn_ref, cos_ref, sin_ref):
    pos = pos_ref[0].astype(f32)
    hs = pos.shape[0] // 2
    half = HEAD_DIM // 2
    lane = lax.broadcasted_iota(jnp.int32, (hs, HEAD_DIM), 1)
    low = lane < half
    ang = jnp.where(low, pos[:hs], pos[hs:]) * freq_ref[...]
    c = jnp.cos(ang)
    s = jnp.sin(ang) * sgn_ref[...]
    c_sw = pltpu.roll(c, half, axis=1)
    s_sw = pltpu.roll(s, half, axis=1)
    cos_ref[0, :hs, :] = jnp.where(low, c, c_sw)
    cos_ref[0, hs:, :] = jnp.where(low, c_sw, c)
    sin_ref[0, :hs, :] = jnp.where(low, s, -s_sw)
    sin_ref[0, hs:, :] = jnp.where(low, -s_sw, s)


def _rope_tables(positions):
    b, s = positions.shape
    half = HEAD_DIM // 2
    inv = (np.float32(ROPE_BASE) ** (-np.arange(half, dtype=np.float32) / np.float32(half))).astype(np.float32)
    freq = jnp.asarray(np.concatenate([inv, inv])[None, :])
    sgn = jnp.asarray(np.concatenate([-np.ones(half, np.float32), np.ones(half, np.float32)])[None, :])
    out = jax.ShapeDtypeStruct((b, s, HEAD_DIM), f32)
    return pl.pallas_call(
        _rope_kernel,
        grid=(b,),
        in_specs=[pl.BlockSpec((1, s, 1), lambda i: (i, 0, 0)),
                  pl.BlockSpec((1, HEAD_DIM), lambda i: (0, 0)),
                  pl.BlockSpec((1, HEAD_DIM), lambda i: (0, 0))],
        out_specs=[pl.BlockSpec((1, s, HEAD_DIM), lambda i: (i, 0, 0)),
                   pl.BlockSpec((1, s, HEAD_DIM), lambda i: (i, 0, 0))],
        out_shape=[out, out],
        compiler_params=_cparams(("parallel",)),
        name="rope_tables",
    )(positions.reshape(b, s, 1), freq, sgn)


_Q_FACTORS = ((COL_FQ * HEAD_DIM // PROJ_COLS, LOG2E / math.sqrt(HEAD_DIM)),
              (COL_RQ * HEAD_DIM // PROJ_COLS, 1.0 / math.sqrt(HEAD_DIM)))


def _wprep_kernel(off_ref, w_hbm, o_ref, buf, sem, *, tn):
    j = pl.program_id(0)
    n = pl.num_programs(0)
    slot = j % 2

    def copy(jj, s):
        src = w_hbm.at[pl.ds(pl.multiple_of(off_ref[jj], 8), tn), :]
        return pltpu.make_async_copy(src, buf.at[s], sem.at[s])

    @pl.when(j == 0)
    def _():
        copy(0, 0).start()

    @pl.when(j + 1 < n)
    def _():
        copy(j + 1, 1 - slot).start()

    copy(j, slot).wait()
    factor = jnp.float32(1.0)
    for tile, value in _Q_FACTORS:
        factor = jnp.where(j == tile, jnp.float32(value), factor)
    o_ref[...] = (buf[slot] * factor).astype(o_ref.dtype)


def _prep_in_weights(wt, row_offsets):
    tn = PROJ_COLS
    d = wt.shape[1]
    n_tiles = len(row_offsets)
    grid_spec = pltpu.PrefetchScalarGridSpec(
        num_scalar_prefetch=1,
        grid=(n_tiles,),
        in_specs=[pl.BlockSpec(memory_space=pl.ANY)],
        out_specs=pl.BlockSpec((tn, d), lambda j, off: (j, 0)),
        scratch_shapes=[pltpu.VMEM((2, tn, d), f32), pltpu.SemaphoreType.DMA((2,))],
    )
    return pl.pallas_call(
        functools.partial(_wprep_kernel, tn=tn),
        grid_spec=grid_spec,
        out_shape=jax.ShapeDtypeStruct((n_tiles * tn, d), bf16),
        compiler_params=_cparams(("arbitrary",)),
        name="prep_in_weights",
    )(jnp.asarray(row_offsets, jnp.int32), wt)


_ROT_COLS = (COL_RQ * HEAD_DIM, (COL_RK + RET_HEADS) * HEAD_DIM)


def _inproj_kernel(x_ref, nw_ref, wf_ref, cos_ref, sin_ref, w_hbm, f_ref, o_hbm, xn_ref, wbuf, obuf, wsem, osem,
                   *, n_cols):
    i = pl.program_id(0)
    n_steps = pl.num_programs(0)
    tm = x_ref.shape[0]
    tn = wbuf.shape[1]
    n_tiles = pl.cdiv(n_cols, tn)
    assert n_tiles >= 2 and n_tiles % 2 == 0
    width = lambda j: min(tn, n_cols - j * tn)

    class w_copy:
        def __init__(self, j):
            rows = width(j) // PROJ_W_SPLIT
            self.parts = [pltpu.make_async_copy(w_hbm.at[pl.ds(j * tn + k * rows, rows), :],
                                                wbuf.at[j % 2, pl.ds(k * rows, rows), :], wsem.at[j % 2])
                          for k in range(PROJ_W_SPLIT)]

        def start(self):
            for k, part in enumerate(self.parts):
                part.start(priority=k % 2)

        def wait(self):
            for part in self.parts:
                part.wait()

    def o_copy(step, j):
        dst = o_hbm.at[pl.ds(pl.multiple_of(step * tm, tm), tm), pl.ds(j * tn, width(j))]
        return pltpu.make_async_copy(obuf.at[j % 2, :, pl.ds(0, width(j))], dst, osem.at[j % 2])

    @pl.when(i == 0)
    def _():
        w_copy(0).start()

    x = x_ref[...]
    ms = jnp.mean(x * x, axis=-1, keepdims=True)
    xn = (x * lax.rsqrt(ms + EPS) * nw_ref[...]).astype(bf16)
    xn_ref[...] = xn
    f_ref[...] = lax.dot_general(xn, wf_ref[...], _NT, preferred_element_type=f32)

    @pl.when(i > 0)
    def _():
        o_copy(i - 1, n_tiles - 2).wait()
        o_copy(i - 1, n_tiles - 1).wait()

    for j in range(n_tiles):
        w_copy(j).wait()
        w_copy((j + 1) % n_tiles).start()
        if j >= 2:
            o_copy(i, j - 2).wait()
        wj = width(j)
        acc = lax.dot_general(xn_ref[...], wbuf[j % 2, :wj, :], _NT, preferred_element_type=f32)
        lo = max(_ROT_COLS[0] - j * tn, 0)
        hi = min(_ROT_COLS[1] - j * tn, wj)
        if lo >= hi:
            lo = hi = 0
        if lo:
            obuf[j % 2, :, :lo] = acc[:, :lo].astype(obuf.dtype)
        for c0 in range(lo, hi, HEAD_DIM):
            t = acc[:, c0:c0 + HEAD_DIM]
            rot = t * cos_ref[...] + pltpu.roll(t, HEAD_DIM // 2, axis=1) * sin_ref[...]
            obuf[j % 2, :, c0:c0 + HEAD_DIM] = rot.astype(obuf.dtype)
        if hi < wj:
            obuf[j % 2, :, hi:wj] = acc[:, hi:].astype(obuf.dtype)
        o_copy(i, j).start()

    @pl.when(i == n_steps - 1)
    def _():
        w_copy(0).wait()
        o_copy(i, n_tiles - 2).wait()
        o_copy(i, n_tiles - 1).wait()


def _in_projection(x2, norm_w, w_main_t, w_f_t, cos, sin):
    t, d = x2.shape
    n = w_main_t.shape[0]
    tm = min(PROJ_ROWS, t)
    tn = PROJ_TILE
    return pl.pallas_call(
        functools.partial(_inproj_kernel, n_cols=n),
        grid=(t // tm,),
        in_specs=[pl.BlockSpec((tm, d), lambda i: (i, 0)),
                  pl.BlockSpec((1, d), lambda i: (0, 0)),
                  pl.BlockSpec((LANES, d), lambda i: (0, 0)),
                  pl.BlockSpec((tm, HEAD_DIM), lambda i: (i, 0)),
                  pl.BlockSpec((tm, HEAD_DIM), lambda i: (i, 0)),
                  pl.BlockSpec(memory_space=pl.ANY)],
        out_specs=[pl.BlockSpec((tm, LANES), lambda i: (i, 0)),
                   pl.BlockSpec(memory_space=pl.ANY)],
        out_shape=[jax.ShapeDtypeStruct((t, LANES), f32), jax.ShapeDtypeStruct((t, n), bf16)],
        scratch_shapes=[pltpu.VMEM((tm, d), bf16), pltpu.VMEM((2, tn, d), bf16), pltpu.VMEM((2, tm, tn), bf16),
                        pltpu.SemaphoreType.DMA((2,)), pltpu.SemaphoreType.DMA((2,))],
        compiler_params=_cparams(("arbitrary",)),
        name="in_projection",
    )(x2, norm_w, w_f_t, cos, sin, w_main_t)


def _fcum_kernel(f_ref, b_ref, c_ref):
    z = f_ref[...] + b_ref[...]
    ls = jnp.minimum(z, 0.0) - jnp.log1p(jnp.exp(-jnp.abs(z)))
    c = ls.T[:FOX_HEADS, :]
    s = c.shape[1]
    lane = lax.broadcasted_iota(jnp.int32, c.shape, 1)
    sh = 1
    while sh < s:
        c = c + jnp.where(lane >= sh, pltpu.roll(c, sh, axis=1), 0.0)
        sh *= 2
    c_ref[0] = c * LOG2E


def _forget_cumsum(f_logit, bias, b, s):
    return pl.pallas_call(
        _fcum_kernel,
        grid=(b,),
        in_specs=[pl.BlockSpec((s, LANES), lambda i: (i, 0)),
                  pl.BlockSpec((1, LANES), lambda i: (0, 0))],
        out_specs=pl.BlockSpec((1, FOX_HEADS, s), lambda i: (i, 0, 0)),
        out_shape=jax.ShapeDtypeStruct((b, FOX_HEADS, s), f32),
        compiler_params=_cparams(("parallel",)),
        name="forget_cumsum",
    )(f_logit, bias)


def _fox_kernel(q_ref, k_ref, v_ref, c_ref, o_ref, va_ref, acc_ref, *, tq):
    s_len = q_ref.shape[0]
    nq = s_len // tq
    n_heads = q_ref.shape[1] // HEAD_DIM
    row = lax.broadcasted_iota(jnp.int32, (tq, tq), 0)
    col = lax.broadcasted_iota(jnp.int32, (tq, tq), 1)
    causal = row >= col
    def scores(hh):
        lanes = slice(hh * HEAD_DIM, (hh + 1) * HEAD_DIM)
        va_ref[hh, :, :HEAD_DIM] = v_ref[:, lanes]
        va_ref[hh, :, HEAD_DIM:] = jnp.ones((s_len, HEAD_DIM), va_ref.dtype)
        panels = []
        row_max = [None] * nq
        for kb in range(nq):
            r0 = kb * tq
            s = lax.dot_general(q_ref[r0:, lanes], k_ref[r0:r0 + tq, lanes], _NT, preferred_element_type=f32)
            s = s - c_ref[hh, :, r0:r0 + tq]
            parts = [jnp.where(causal, s[:tq], NEG_BIG)]
            parts += [s[(j - kb) * tq:(j - kb + 1) * tq] for j in range(kb + 1, nq)]
            panels.append(parts)
            for j, part in zip(range(kb, nq), parts):
                pm = jnp.max(part, axis=-1, keepdims=True)
                row_max[j] = pm if row_max[j] is None else jnp.maximum(row_max[j], pm)
        return panels, row_max

    def values(hh, panels, row_max):
        lanes = slice(hh * HEAD_DIM, (hh + 1) * HEAD_DIM)
        for kb in range(nq):
            r0 = kb * tq
            p = jnp.concatenate([jnp.exp2(part - row_max[j]).astype(bf16)
                                 for j, part in zip(range(kb, nq), panels[kb])], axis=0)
            contrib = jnp.dot(p, va_ref[hh, r0:r0 + tq, :], preferred_element_type=f32)
            if kb:
                acc_ref[hh, r0:, :] += contrib
            else:
                acc_ref[hh] = contrib
        acc = acc_ref[hh]
        o_ref[:, lanes] = (acc[:, :HEAD_DIM] / acc[:, HEAD_DIM:]).astype(o_ref.dtype)

    pending = None
    for hh in range(n_heads):
        current = (hh,) + scores(hh)
        if pending is not None:
            values(*pending)
        pending = current
    values(*pending)


def _fox_attention(proj, c_blk, b, s):
    tq = min(ATTN_BLOCK, s)
    hp = FOX_HEADS_PER_STEP
    w = hp * HEAD_DIM
    col = lambda c0: pl.BlockSpec((s, w), lambda i, j: (i, c0 // hp + j))
    return pl.pallas_call(
        functools.partial(_fox_kernel, tq=tq),
        grid=(b, FOX_HEADS // hp),
        in_specs=[col(COL_FQ), col(COL_FK), col(COL_FV),
                  pl.BlockSpec((hp, 1, s), lambda i, j: (i * (FOX_HEADS // hp) + j, 0, 0))],
        out_specs=pl.BlockSpec((s, w), lambda i, j: (i, j)),
        out_shape=jax.ShapeDtypeStruct((b * s, FOX_WIDTH), bf16),
        scratch_shapes=[pltpu.VMEM((hp, s, 2 * HEAD_DIM), bf16), pltpu.VMEM((hp, s, 2 * HEAD_DIM), f32)],
        compiler_params=_cparams(("parallel", "parallel")),
        name="fox_attention",
    )(proj, proj, proj, c_blk)


def _ret_kernel(q_ref, k_ref, v_ref, g_ref, dec_ref, xi_ref, zeta_ref, gc_ref, gnw_ref, o_ref, *, chunk):
    s_len = q_ref.shape[0]
    nc = s_len // chunk
    n_heads = q_ref.shape[1] // HEAD_DIM
    states = [None] * n_heads
    for ci in range(nc):
        rows = slice(ci * chunk, (ci + 1) * chunk)
        for hh in range(n_heads):
            lanes = slice(hh * HEAD_DIM, (hh + 1) * HEAD_DIM)
            state = states[hh]
            qb = q_ref[rows, lanes]
            kb = k_ref[rows, lanes]
            v = v_ref[rows, lanes]
            inner = lax.dot_general(qb, kb, _NT, preferred_element_type=f32) * dec_ref[hh]
            o = jnp.dot(inner.astype(bf16), v, preferred_element_type=f32)
            if ci:
                o = o + jnp.dot(qb, state.astype(bf16), preferred_element_type=f32) * xi_ref[hh]
            if ci + 1 < nc:
                kz = (kb.astype(f32) * zeta_ref[hh]).astype(bf16)
                kv = lax.dot_general(kz, v, (((0,), (0,)), ((), ())), preferred_element_type=f32)
                states[hh] = state * gc_ref[hh] + kv if ci else kv
            mu = jnp.mean(o, axis=-1, keepdims=True)
            oc = o - mu
            var = jnp.mean(oc * oc, axis=-1, keepdims=True)
            y = oc * lax.rsqrt(var + EPS) * gnw_ref[:, lanes]
            g = g_ref[rows, lanes].astype(f32)
            o_ref[rows, lanes] = (g * jax.nn.sigmoid(g) * y).astype(o_ref.dtype)


def _retention_constants(chunk):
    hh = np.arange(RET_HEADS, dtype=np.float64)
    log_gamma = np.log1p(-(2.0 ** (-5.0 - hh)))
    n = np.arange(chunk, dtype=np.float64)
    diff = n[:, None] - n[None, :]
    decay = np.where(diff[None] >= 0, np.exp(diff[None] * log_gamma[:, None, None]), 0.0)
    xi = np.exp((n[None, :] + 1.0) * log_gamma[:, None])
    zeta = np.exp((chunk - 1.0 - n[None, :]) * log_gamma[:, None])
    g_chunk = np.exp(chunk * log_gamma)
    bc = lambda a: np.broadcast_to(a[:, :, None], (RET_HEADS, chunk, HEAD_DIM))
    gc = np.broadcast_to(g_chunk[:, None, None], (RET_HEADS, 1, HEAD_DIM))
    to = lambda a: jnp.asarray(np.ascontiguousarray(a), dtype=f32)
    return to(decay), to(bc(xi)), to(bc(zeta)), to(gc)


def _retention(proj, gn_w, b, s):
    chunk = min(RET_CHUNK, s)
    decay, xi, zeta, gc = _retention_constants(chunk)
    hp = RET_HEADS_PER_STEP
    w = hp * HEAD_DIM
    head = lambda col: pl.BlockSpec((s, w), lambda i, j: (i, col // hp + j))
    per_head = lambda shape: pl.BlockSpec((hp,) + shape, lambda i, j: (j, 0, 0))
    return pl.pallas_call(
        functools.partial(_ret_kernel, chunk=chunk),
        grid=(b, RET_HEADS // hp),
        in_specs=[head(COL_RQ), head(COL_RK), head(COL_RV), head(COL_RG),
                  per_head((chunk, chunk)), per_head((chunk, HEAD_DIM)), per_head((chunk, HEAD_DIM)),
                  per_head((1, HEAD_DIM)),
                  pl.BlockSpec((1, w), lambda i, j: (0, j))],
        out_specs=pl.BlockSpec((s, w), lambda i, j: (i, j)),
        out_shape=jax.ShapeDtypeStruct((b * s, RET_WIDTH), bf16),
        compiler_params=_cparams(("parallel", "parallel")),
        name="retention",
    )(proj, proj, proj, proj, decay, xi, zeta, gc, gn_w)


def _branch_merge_kernel(fo_ref, ro_ref, g0_ref, g1_ref, bg_ref, wb_ref, mg_ref, *, ncol):
    d = mg_ref.shape[1]
    cw = d // ncol
    fo = fo_ref[...]
    ro = ro_ref[...]
    for c in range(ncol):
        cs = slice(c * cw, (c + 1) * cw)
        bd0 = jnp.dot(fo, wb_ref[0, :, cs], preferred_element_type=f32)
        bd1 = jnp.dot(ro, wb_ref[1, :, cs], preferred_element_type=f32)
        ga = jax.nn.sigmoid(g0_ref[:, cs].astype(f32) + bg_ref[0:1, cs])
        gb = jax.nn.sigmoid(g1_ref[:, cs].astype(f32) + bg_ref[1:2, cs])
        mg_ref[:, cs] = (ga * bd0 + gb * bd1).astype(mg_ref.dtype)


def _branch_merge(fox_o, ret_o, proj, b_gate, w_branch):
    t = fox_o.shape[0]
    d = D_MODEL
    tm = min(MERGE_ROWS, t)
    gate0 = COL_GATE * HEAD_DIM // d
    return pl.pallas_call(
        functools.partial(_branch_merge_kernel, ncol=4),
        grid=(t // tm,),
        in_specs=[pl.BlockSpec((tm, FOX_WIDTH), lambda i: (i, 0)),
                  pl.BlockSpec((tm, RET_WIDTH), lambda i: (i, 0)),
                  pl.BlockSpec((tm, d), lambda i: (i, gate0)),
                  pl.BlockSpec((tm, d), lambda i: (i, gate0 + 1)),
                  _const_spec((N_BRANCH, d)),
                  _const_spec((N_BRANCH, FOX_WIDTH, d))],
        out_specs=pl.BlockSpec((tm, d), lambda i: (i, 0)),
        out_shape=jax.ShapeDtypeStruct((t, d), bf16),
        compiler_params=_cparams(("parallel",)),
        name="branch_merge",
    )(fox_o, ret_o, proj, proj, b_gate, w_branch)


def _route(logits):
    lane = lax.broadcasted_iota(jnp.int32, logits.shape, 1).astype(f32)
    far = jnp.float32(1e9)
    gmask = lane < N_GROUPS
    gl = jnp.where(gmask, logits, NEG_BIG)
    gmax = jnp.max(gl, axis=-1, keepdims=True)
    gsel = jnp.min(jnp.where(gl == gmax, lane, far), axis=-1, keepdims=True)
    p_group = 1.0 / jnp.sum(jnp.where(gmask, jnp.exp(gl - gmax), 0.0), axis=-1, keepdims=True)
    lo = N_GROUPS + EXPERTS_PER_GROUP * gsel
    emask = (lane >= lo) & (lane < lo + EXPERTS_PER_GROUP)
    el = jnp.where(emask, logits, NEG_BIG)
    m1 = jnp.max(el, axis=-1, keepdims=True)
    i1 = jnp.min(jnp.where(el == m1, lane, far), axis=-1, keepdims=True)
    el2 = jnp.where(lane == i1, NEG_BIG, el)
    m2 = jnp.max(el2, axis=-1, keepdims=True)
    i2 = jnp.min(jnp.where(el2 == m2, lane, far), axis=-1, keepdims=True)
    t = jnp.exp(m2 - m1)
    w1 = p_group / (1.0 + t)
    w2 = p_group * t / (1.0 + t)
    out = jnp.where(lane == 0, i1 - N_GROUPS, 0.0)
    out = jnp.where(lane == 1, i2 - N_GROUPS, out)
    out = jnp.where(lane == 2, w1, out)
    out = jnp.where(lane == 3, w2, out)
    return out


def _outproj_kernel(mg_ref, x_ref, wo_ref, nfw_ref, wr_ref, br_ref, h_ref, hs_ref, route_ref, cnt_ref):
    h = x_ref[...] + jnp.dot(mg_ref[...], wo_ref[...], preferred_element_type=f32)
    h_ref[...] = h
    hn = h * lax.rsqrt(jnp.mean(h * h, axis=-1, keepdims=True) + EPS) * nfw_ref[...]
    _rows_to_slabs(hs_ref, hn, hn.shape[0])
    hi = hn.astype(bf16)
    lo = (hn - hi.astype(f32)).astype(bf16)
    hw = jnp.dot(hi, wr_ref[...], preferred_element_type=f32)
    logits = (hw[:, :LANES] + jnp.dot(lo, wr_ref[:, :LANES], preferred_element_type=f32) + hw[:, LANES:]) + br_ref[...]
    route = _route(logits)
    route_ref[...] = route

    @pl.when(pl.program_id(0) == 0)
    def _():
        cnt_ref[...] = jnp.zeros_like(cnt_ref)

    lane = lax.broadcasted_iota(jnp.int32, route.shape, 1).astype(f32)
    chosen = (lane == route[:, 0:1]).astype(f32) + (lane == route[:, 1:2]).astype(f32)
    cnt_ref[...] += jnp.sum(chosen, axis=0, keepdims=True)


def _outproj_router(merged, x2, w_out, nfw, wr_cat, b_router):
    t, d = x2.shape
    tm = min(OUTPROJ_ROWS, t)
    return pl.pallas_call(
        _outproj_kernel,
        grid=(t // tm,),
        in_specs=[pl.BlockSpec((tm, d), lambda i: (i, 0)),
                  pl.BlockSpec((tm, d), lambda i: (i, 0)),
                  _const_spec((d, d)),
                  _const_spec((1, d)),
                  _const_spec((d, 2 * LANES)), _const_spec((1, LANES))],
        out_specs=[pl.BlockSpec((tm, d), lambda i: (i, 0)),
                   pl.BlockSpec((tm * PITCH, LANES), lambda i: (i, 0)),
                   pl.BlockSpec((tm, LANES), lambda i: (i, 0)),
                   pl.BlockSpec((8, LANES), lambda i: (0, 0))],
        out_shape=[jax.ShapeDtypeStruct((t, d), f32), jax.ShapeDtypeStruct((t * PITCH, LANES), u32),
                   jax.ShapeDtypeStruct((t, LANES), f32), jax.ShapeDtypeStruct((8, LANES), f32)],
        compiler_params=_cparams(("arbitrary",)),
        name="outproj_router",
    )(merged, x2, w_out, nfw, wr_cat, b_router)


def _plan_kernel(route_ref, cnt_ref, dest_ref, ends_ref, carry_ref, *, bm):
    r = route_ref[...]
    tb = r.shape[0]
    lane = lax.broadcasted_iota(jnp.int32, r.shape, 1).astype(f32)
    o0 = (lane == r[:, 0:1]).astype(f32)
    o1 = (lane == r[:, 1:2]).astype(f32)
    o = o0 + o1

    @pl.when(pl.program_id(0) == 0)
    def _():
        cnt = cnt_ref[...]
        padded = jnp.floor((cnt + (bm - 1)) * (1.0 / bm)) * bm
        lane8 = lax.broadcasted_iota(jnp.int32, cnt.shape, 1)
        ends = padded
        sh = 1
        while sh < LANES:
            ends = ends + jnp.where(lane8 >= sh, pltpu.roll(ends, sh, axis=1), 0.0)
            sh *= 2
        sub = lax.broadcasted_iota(jnp.int32, cnt.shape, 0)
        ends_ref[...] = jnp.where(sub == 0, ends, jnp.where(sub == 1, padded, cnt))
        carry_ref[...] = ends - padded

    row = lax.broadcasted_iota(jnp.int32, (tb, tb), 0)
    col = lax.broadcasted_iota(jnp.int32, (tb, tb), 1)
    lower = jnp.where(col < row, 1.0, 0.0).astype(bf16)
    before = jnp.dot(lower, o.astype(bf16), preferred_element_type=f32) + carry_ref[0:1, :]
    d0 = jnp.sum(before * o0, axis=-1, keepdims=True)
    d1 = jnp.sum(before * o1, axis=-1, keepdims=True)
    dd = jnp.where(lane == 0, d0, jnp.where(lane == 1, d1, 0.0))
    dest_ref[0] = dd.T[:8, :].astype(jnp.int32)
    carry_ref[...] += jnp.sum(o, axis=0, keepdims=True)


def _dispatch_plan(route, counts, bm):
    t = route.shape[0]
    tb = min(PLAN_ROWS, t)
    nb = t // tb
    return pl.pallas_call(
        functools.partial(_plan_kernel, bm=bm),
        grid=(nb,),
        in_specs=[pl.BlockSpec((tb, LANES), lambda i: (i, 0)),
                  pl.BlockSpec((8, LANES), lambda i: (0, 0))],
        out_specs=[pl.BlockSpec((1, 8, tb), lambda i: (i, 0, 0)),
                   pl.BlockSpec((8, LANES), lambda i: (0, 0))],
        out_shape=[jax.ShapeDtypeStruct((nb, 8, tb), jnp.int32), jax.ShapeDtypeStruct((8, LANES), f32)],
        scratch_shapes=[pltpu.VMEM((8, LANES), f32)],
        compiler_params=_cparams(("arbitrary",)),
        name="dispatch_plan",
    )(route, counts)


def _dispatch_kernel(d0_ref, d1_ref, fill_ref, end_ref, nu_ref, hs_ref, xs_hbm, zbuf, sem, zsem,
                     *, tb, bm, n_blk, unroll=8):
    i = pl.program_id(0)
    base = i * tb
    blk_rows = bm * PITCH

    def pad_copy(s, n):
        return pltpu.make_async_copy(zbuf.at[pl.ds(0, n * PITCH), :], xs_hbm.at[pl.ds(s * PITCH, n * PITCH), :], zsem)

    def tail_copy(g):
        return pltpu.make_async_copy(zbuf, xs_hbm.at[pl.ds(pl.multiple_of(g * blk_rows, 8), blk_rows), :], zsem)

    @pl.when(i == 0)
    def _():
        zbuf[...] = jnp.zeros(zbuf.shape, u32)
        for op in ("start", "wait"):
            def per_expert(e, c, op=op):
                s = fill_ref[e]
                for n in PAD_FILL_CHUNKS:
                    count = (end_ref[e] - s) // n

                    def one(k, c2, s=s, n=n):
                        getattr(pad_copy(s + k * n, n), op)()
                        return c2

                    c = lax.fori_loop(0, count, one, c)
                    s = s + count * n
                return c

            def per_tail(g, c, op=op):
                getattr(tail_copy(g), op)()
                return c

            lax.fori_loop(0, N_EXPERTS, per_expert, 0)
            lax.fori_loop(nu_ref[0], n_blk, per_tail, 0)

    def body(g, c):
        for u in range(unroll):
            r = g * unroll + u
            src = hs_ref.at[pl.ds(r * PITCH, PITCH), :]
            for k, dref in enumerate((d0_ref, d1_ref)):
                d = dref[base + r]
                pltpu.make_async_copy(src, xs_hbm.at[pl.ds(d * PITCH, PITCH), :], sem).start(priority=k)
        return c

    lax.fori_loop(0, tb // unroll, body, 0)
    for _ in range(TOP_K):
        pltpu.make_async_copy(xs_hbm.at[pl.ds(0, tb * PITCH), :], xs_hbm.at[pl.ds(0, tb * PITCH), :], sem).wait()


def _dispatch_rows(dest0, dest1, fill_start, seg_end, n_used, hs, cap):
    t = dest0.shape[0]
    tb = min(DISPATCH_ROWS, t)
    bm = MOE_BLOCK
    grid_spec = pltpu.PrefetchScalarGridSpec(
        num_scalar_prefetch=5,
        grid=(t // tb,),
        in_specs=[pl.BlockSpec((tb * PITCH, LANES), lambda i, *_: (i, 0))],
        out_specs=pl.BlockSpec(memory_space=pl.ANY),
        scratch_shapes=[pltpu.VMEM((bm * PITCH, LANES), u32), pltpu.SemaphoreType.DMA(()),
                        pltpu.SemaphoreType.DMA(())],
    )
    return pl.pallas_call(
        functools.partial(_dispatch_kernel, tb=tb, bm=bm, n_blk=cap // bm),
        grid_spec=grid_spec,
        out_shape=jax.ShapeDtypeStruct((cap * PITCH, LANES), u32),
        compiler_params=_cparams(("arbitrary",)),
        name="dispatch_rows",
    )(dest0, dest1, fill_start, seg_end, n_used, hs)


def _moe_kernel(g0_ref, nb_ref, nu_ref, x_hbm, w1_ref, w3_ref, w2_ref, y_hbm,
                xbuf, ybuf, w1b, w3b, w2b, gsem, ysem, *, bm, n_blk):
    e = pl.program_id(0)
    n_used = nu_ref[0]
    g0 = g0_ref[e]
    nb = nb_ref[e]
    blk_rows = bm * PITCH

    def x_copy(slot, g):
        src = x_hbm.at[pl.ds(pl.multiple_of(g * blk_rows, 8), blk_rows), :]
        return pltpu.make_async_copy(src, xbuf.at[slot], gsem.at[slot])

    def y_copy(slot, g):
        dst = y_hbm.at[pl.ds(pl.multiple_of(g * blk_rows, 8), blk_rows), :]
        return pltpu.make_async_copy(ybuf.at[slot], dst, ysem.at[slot])

    @pl.when(e == 0)
    def _():
        x_copy(0, 0).start()
        for k in range(1, X_BUFS - 1):
            pl.when(n_used > k)(lambda k=k: x_copy(k, k).start())

    def run_block(g, cast_weights):
        slot = g % 2
        xslot = g % X_BUFS
        ahead = g + (X_BUFS - 1)

        @pl.when(ahead < n_used)
        def _():
            x_copy(ahead % X_BUFS, ahead).start()

        x_copy(xslot, g).wait()

        if cast_weights:
            w1b[...] = w1_ref[0].astype(bf16)
        hn = _slabs_to_rows(xbuf.at[xslot], bm).astype(bf16)
        a = jnp.dot(hn, w1b[...], preferred_element_type=f32)
        if cast_weights:
            w3b[...] = w3_ref[0].astype(bf16)
        b = jnp.dot(hn, w3b[...], preferred_element_type=f32)
        if cast_weights:
            w2b[...] = w2_ref[0].astype(bf16)
        mid = (a * jax.nn.sigmoid(a) * b).astype(bf16)
        y = jnp.dot(mid, w2b[...], preferred_element_type=f32)

        @pl.when(g >= 2)
        def _():
            y_copy(slot, g - 2).wait()

        _rows_to_slabs(ybuf.at[slot], y, bm)
        y_copy(slot, g).start()

    @pl.when(nb > 0)
    def _():
        run_block(g0, True)

    def block(c, carry):
        run_block(g0 + c, False)
        return carry

    lax.fori_loop(1, nb, block, 0)

    @pl.when(e == pl.num_programs(0) - 1)
    def _():
        @pl.when(n_used >= 2)
        def _():
            y_copy(n_used % 2, n_used - 2).wait()

        y_copy((n_used - 1) % 2, n_used - 1).wait()
        ybuf[0] = jnp.zeros(ybuf.shape[1:], u32)

        def zstart(g, carry):
            y_copy(0, g).start()
            return carry

        def zwait(g, carry):
            y_copy(0, g).wait()
            return carry

        lax.fori_loop(n_used, n_blk, zstart, 0)
        lax.fori_loop(n_used, n_blk, zwait, 0)


def _expert_mlp(blk_start, blk_count, n_used, xs, w1, w3, w2, cap):
    d = D_MODEL
    bm = MOE_BLOCK
    n_blk = cap // bm
    by_expert = lambda shape: pl.BlockSpec(shape, lambda e, g0, nb, nu: (e, 0, 0))
    grid_spec = pltpu.PrefetchScalarGridSpec(
        num_scalar_prefetch=3,
        grid=(N_EXPERTS,),
        in_specs=[pl.BlockSpec(memory_space=pl.ANY),
                  by_expert((1, d, D_EXPERT)), by_expert((1, d, D_EXPERT)), by_expert((1, D_EXPERT, d))],
        out_specs=pl.BlockSpec(memory_space=pl.ANY),
        scratch_shapes=[pltpu.VMEM((X_BUFS, bm * PITCH, LANES), u32), pltpu.VMEM((2, bm * PITCH, LANES), u32),
                        pltpu.VMEM((d, D_EXPERT), bf16), pltpu.VMEM((d, D_EXPERT), bf16),
                        pltpu.VMEM((D_EXPERT, d), bf16),
                        pltpu.SemaphoreType.DMA((X_BUFS,)), pltpu.SemaphoreType.DMA((2,))],
    )
    return pl.pallas_call(
        functools.partial(_moe_kernel, bm=bm, n_blk=n_blk),
        grid_spec=grid_spec,
        out_shape=jax.ShapeDtypeStruct((cap * PITCH, LANES), u32),
        compiler_params=_cparams(("arbitrary",)),
        name="expert_mlp",
    )(blk_start, blk_count, n_used, xs, w1, w3, w2)


def _combine_kernel(d0_ref, d1_ref, h_ref, route_ref, y_hbm, nw_ref, o_ref, ybuf, sem, *, tb):
    i = pl.program_id(0)
    n = pl.num_programs(0)
    slot = i % 2

    def start(blk, s):
        _slab_gather_start(d0_ref, blk * tb, tb, y_hbm, ybuf.at[s, 0], sem.at[s])
        _slab_gather_start(d1_ref, blk * tb, tb, y_hbm, ybuf.at[s, 1], sem.at[s])

    @pl.when(i == 0)
    def _():
        start(0, 0)

    @pl.when(i + 1 < n)
    def _():
        start(i + 1, 1 - slot)

    for k in range(TOP_K):
        _slab_gather_wait(tb, y_hbm, ybuf.at[slot, k], sem.at[slot])
    r = route_ref[...]
    h = h_ref[...] + (r[:, 2:3] * _slabs_to_rows(ybuf.at[slot, 0], tb)
                      + r[:, 3:4] * _slabs_to_rows(ybuf.at[slot, 1], tb))
    o_ref[...] = h * lax.rsqrt(jnp.mean(h * h, axis=-1, keepdims=True) + EPS) * nw_ref[...]


def _combine(dest0, dest1, h, route, ys, norm_w):
    t, d = h.shape
    tb = min(COMBINE_ROWS, t)
    grid_spec = pltpu.PrefetchScalarGridSpec(
        num_scalar_prefetch=2,
        grid=(t // tb,),
        in_specs=[pl.BlockSpec((tb, d), lambda i, a, b: (i, 0)),
                  pl.BlockSpec((tb, LANES), lambda i, a, b: (i, 0)),
                  pl.BlockSpec(memory_space=pl.ANY),
                  pl.BlockSpec((1, d), lambda i, a, b: (0, 0))],
        out_specs=pl.BlockSpec((tb, d), lambda i, a, b: (i, 0)),
        scratch_shapes=[pltpu.VMEM((2, TOP_K, tb * PITCH, LANES), u32), pltpu.SemaphoreType.DMA((2,))],
    )
    return pl.pallas_call(
        functools.partial(_combine_kernel, tb=tb),
        grid_spec=grid_spec,
        out_shape=jax.ShapeDtypeStruct((t, d), f32),
        compiler_params=_cparams(("arbitrary",)),
        name="combine_final_norm",
    )(dest0, dest1, h, route, ys, norm_w)


def kernel(x, positions, norm_mix_w, w_in, fox_b_f, ret_gn_w, w_branch, b_gate, w_out, norm_ffn_w, w_router_group,
           b_router_group, w_router_expert, b_router_expert, w1, w3, w2, norm_final_w):
    b, s, d = x.shape
    assert d == D_MODEL and norm_mix_w.shape[0] == 1
    t = b * s
    x2 = x.reshape(t, d)

    wt = w_in[0].T
    o_f = 3 * FOX_WIDTH
    o_r = o_f + FOX_HEADS
    o_g = o_r + 4 * RET_WIDTH
    tn = PROJ_COLS
    tile_rows = ([o_g + k * tn for k in range(N_BRANCH * D_MODEL // tn)]
                 + [k * tn for k in range(o_f // tn)]
                 + [o_r + k * tn for k in range(4 * RET_WIDTH // tn)])
    w_main = _prep_in_weights(wt, tile_rows)
    w_f = jnp.pad(wt[o_f:o_r], ((0, LANES - FOX_HEADS), (0, 0))).astype(bf16)
    bias_f = jnp.pad(fox_b_f[0], (0, LANES - FOX_HEADS)).reshape(1, LANES)
    w_r = jnp.concatenate([w_router_group[0], w_router_expert[0]], axis=1)
    w_r = jnp.pad(w_r, ((0, 0), (0, LANES - w_r.shape[1])))
    wr_hi = w_r.astype(bf16)
    wr_lo = (w_r - wr_hi.astype(f32)).astype(bf16)
    b_r = jnp.pad(jnp.concatenate([b_router_group[0], b_router_expert[0]]), (0, LANES - N_GROUPS - N_EXPERTS))
    b_r = b_r.reshape(1, LANES)

    cos, sin = _rope_tables(positions)
    f_logit, proj = _in_projection(x2, norm_mix_w, w_main, w_f, cos.reshape(t, HEAD_DIM), sin.reshape(t, HEAD_DIM))
    c = _forget_cumsum(f_logit, bias_f, b, s)
    fox_o = _fox_attention(proj, c.reshape(b * FOX_HEADS, 1, s), b, s)
    ret_o = _retention(proj, ret_gn_w, b, s)
    merged = _branch_merge(fox_o, ret_o, proj, b_gate[0], w_branch[0].astype(bf16))
    h, h_slabs, route, counts = _outproj_router(merged, x2, w_out[0].astype(bf16), norm_ffn_w,
                                        jnp.concatenate([wr_hi, wr_lo], axis=1), b_r)

    bm = MOE_BLOCK
    cap = t * TOP_K + N_EXPERTS * bm
    dest_rows, seg = _dispatch_plan(route, counts, bm)
    dest0 = dest_rows[:, 0, :].reshape(t)
    dest1 = dest_rows[:, 1, :].reshape(t)
    pends = seg[0, :N_EXPERTS].astype(jnp.int32)
    padded = seg[1, :N_EXPERTS].astype(jnp.int32)
    counts = seg[2, :N_EXPERTS].astype(jnp.int32)
    pstarts = pends - padded
    n_used = pends[-1:] // bm

    xs = _dispatch_rows(dest0, dest1, pstarts + counts, pends, n_used, h_slabs, cap)
    y = _expert_mlp(pstarts // bm, padded // bm, n_used, xs, w1[0], w3[0], w2[0], cap)
    out = _combine(dest0, dest1, h, route, y, norm_final_w.reshape(1, d))
    return out.reshape(b, s, d)
```

```python
import functools
import math

import numpy as np
import jax
import jax.numpy as jnp
from jax import lax
from jax.experimental import pallas as pl
from jax.experimental.pallas import tpu as pltpu

D_MODEL = 2048
HEAD_DIM = 128
FOX_HEADS = 8
RET_HEADS = 8
FOX_WIDTH = FOX_HEADS * HEAD_DIM
RET_WIDTH = RET_HEADS * HEAD_DIM
N_BRANCH = 2
N_GROUPS = 4
EXPERTS_PER_GROUP = 8
N_EXPERTS = N_GROUPS * EXPERTS_PER_GROUP
TOP_K = 2
D_EXPERT = 512
ROPE_BASE = 10000.0
EPS = 1e-6

LANES = 128
NEG_BIG = -1e30
LOG2E = math.log2(math.e)
COL_GATE = 0
COL_FQ = N_BRANCH * D_MODEL // HEAD_DIM
COL_FK = COL_FQ + FOX_HEADS
COL_FV = COL_FK + FOX_HEADS
COL_RQ = COL_FV + FOX_HEADS
COL_RK = COL_RQ + RET_HEADS
COL_RV = COL_RK + RET_HEADS
COL_RG = COL_RV + RET_HEADS

ATTN_BLOCK = 256
FOX_HEADS_PER_STEP = 4
RET_HEADS_PER_STEP = 4
RET_CHUNK = 256
MOE_BLOCK = 256
X_BUFS = 4
PAD_FILL_CHUNKS = (64, 8, 1)
PROJ_ROWS, PROJ_COLS = 1024, 1024
MERGE_ROWS = 1024
OUTPROJ_ROWS = 512
PLAN_ROWS = 1024
DISPATCH_ROWS = 2048
COMBINE_ROWS = 256
VMEM_LIMIT = 56 * 1024 * 1024

f32 = jnp.float32
bf16 = jnp.bfloat16
_NT = (((1,), (1,)), ((), ()))
SLAB = D_MODEL // (2 * LANES)
PITCH = SLAB + 1
u32 = jnp.uint32
_HI_MASK = np.uint32(0xFFFF0000)


def _rows_to_slabs(ref2, val, n_rows):
    bits = lambda v: lax.bitcast_convert_type(v.astype(bf16).astype(f32), u32)
    for j in range(SLAB):
        lo = val[:, (2 * j) * LANES:(2 * j + 1) * LANES]
        hi = val[:, (2 * j + 1) * LANES:(2 * j + 2) * LANES]
        ref2[pl.ds(j, n_rows, stride=PITCH), :] = (bits(lo) >> 16) | (bits(hi) & _HI_MASK)
    ref2[pl.ds(SLAB, n_rows, stride=PITCH), :] = jnp.zeros((n_rows, LANES), u32)


def _slabs_to_rows(ref2, n_rows):
    chunks = []
    for j in range(SLAB):
        w = ref2[pl.ds(j, n_rows, stride=PITCH), :]
        chunks.append(lax.bitcast_convert_type(w << 16, f32))
        chunks.append(lax.bitcast_convert_type(w & _HI_MASK, f32))
    return jnp.concatenate(chunks, axis=-1)


def _slab_gather_start(idx_ref, base, n_rows, src_hbm, dst, sem, unroll=8):
    def body(g, _):
        for u in range(unroll):
            r = g * unroll + u
            t = idx_ref[base + r]
            pltpu.make_async_copy(src_hbm.at[pl.ds(t * PITCH, SLAB), :], dst.at[pl.ds(r * PITCH, SLAB), :],
                                  sem).start(priority=u % 2)
        return 0
    lax.fori_loop(0, n_rows // unroll, body, 0)


def _slab_gather_wait(n_rows, src_hbm, dst, sem):
    pltpu.make_async_copy(src_hbm.at[pl.ds(0, n_rows * SLAB), :], dst.at[pl.ds(0, n_rows * SLAB), :], sem).wait()


def _cparams(sem, vmem=VMEM_LIMIT):
    return pltpu.CompilerParams(dimension_semantics=sem, vmem_limit_bytes=vmem)


def _const_spec(shape):
    return pl.BlockSpec(shape, lambda i: tuple(0 for _ in shape), pipeline_mode=pl.Buffered(1))


def _rope_kernel(pos_ref, freq_ref, sgn_ref, cos_ref, sin_ref):
    pos = pos_ref[0].astype(f32)
    hs = pos.shape[0] // 2
    half = HEAD_DIM // 2
    lane = lax.broadcasted_iota(jnp.int32, (hs, HEAD_DIM), 1)
    low = lane < half
    ang = jnp.where(low, pos[:hs], pos[hs:]) * freq_ref[...]
    c = jnp.cos(ang)
    s = jnp.sin(ang) * sgn_ref[...]
    c_sw = pltpu.roll(c, half, axis=1)
    s_sw = pltpu.roll(s, half, axis=1)
    cos_ref[0, :hs, :] = jnp.where(low, c, c_sw)
    cos_ref[0, hs:, :] = jnp.where(low, c_sw, c)
    sin_ref[0, :hs, :] = jnp.where(low, s, -s_sw)
    sin_ref[0, hs:, :] = jnp.where(low, -s_sw, s)


def _rope_tables(positions):
    b, s = positions.shape
    half = HEAD_DIM // 2
    inv = (np.float32(ROPE_BASE) ** (-np.arange(half, dtype=np.float32) / np.float32(half))).astype(np.float32)
    freq = jnp.asarray(np.concatenate([inv, inv])[None, :])
    sgn = jnp.asarray(np.concatenate([-np.ones(half, np.float32), np.ones(half, np.float32)])[None, :])
    out = jax.ShapeDtypeStruct((b, s, HEAD_DIM), f32)
    return pl.pallas_call(
        _rope_kernel,
        grid=(b,),
        in_specs=[pl.BlockSpec((1, s, 1), lambda i: (i, 0, 0)),
                  pl.BlockSpec((1, HEAD_DIM), lambda i: (0, 0)),
                  pl.BlockSpec((1, HEAD_DIM), lambda i: (0, 0))],
        out_specs=[pl.BlockSpec((1, s, HEAD_DIM), lambda i: (i, 0, 0)),
                   pl.BlockSpec((1, s, HEAD_DIM), lambda i: (i, 0, 0))],
        out_shape=[out, out],
        compiler_params=_cparams(("parallel",)),
        name="rope_tables",
    )(positions.reshape(b, s, 1), freq, sgn)


_Q_FACTORS = ((COL_FQ * HEAD_DIM // PROJ_COLS, LOG2E / math.sqrt(HEAD_DIM)),
              (COL_RQ * HEAD_DIM // PROJ_COLS, 1.0 / math.sqrt(HEAD_DIM)))


def _wprep_kernel(off_ref, w_hbm, o_ref, buf, sem, *, tn):
    j = pl.program_id(0)
    n = pl.num_programs(0)
    slot = j % 2

    def copy(jj, s):
        src = w_hbm.at[pl.ds(pl.multiple_of(off_ref[jj], 8), tn), :]
        return pltpu.make_async_copy(src, buf.at[s], sem.at[s])

    @pl.when(j == 0)
    def _():
        copy(0, 0).start()

    @pl.when(j + 1 < n)
    def _():
        copy(j + 1, 1 - slot).start()

    copy(j, slot).wait()
    factor = jnp.float32(1.0)
    for tile, value in _Q_FACTORS:
        factor = jnp.where(j == tile, jnp.float32(value), factor)
    o_ref[...] = (buf[slot] * factor).astype(o_ref.dtype)


def _prep_in_weights(wt, row_offsets):
    tn = PROJ_COLS
    d = wt.shape[1]
    n_tiles = len(row_offsets)
    grid_spec = pltpu.PrefetchScalarGridSpec(
        num_scalar_prefetch=1,
        grid=(n_tiles,),
        in_specs=[pl.BlockSpec(memory_space=pl.ANY)],
        out_specs=pl.BlockSpec((tn, d), lambda j, off: (j, 0)),
        scratch_shapes=[pltpu.VMEM((2, tn, d), f32), pltpu.SemaphoreType.DMA((2,))],
    )
    return pl.pallas_call(
        functools.partial(_wprep_kernel, tn=tn),
        grid_spec=grid_spec,
        out_shape=jax.ShapeDtypeStruct((n_tiles * tn, d), bf16),
        compiler_params=_cparams(("arbitrary",)),
        name="prep_in_weights",
    )(jnp.asarray(row_offsets, jnp.int32), wt)


_ROT_TILES = (COL_RQ * HEAD_DIM // PROJ_COLS, COL_RK * HEAD_DIM // PROJ_COLS)


def _inproj_kernel(x_ref, nw_ref, w_ref, wf_ref, cos_ref, sin_ref, o_ref, f_ref, xn_ref):
    j = pl.program_id(1)

    @pl.when(j == 0)
    def _():
        x = x_ref[...]
        ms = jnp.mean(x * x, axis=-1, keepdims=True)
        xn = (x * lax.rsqrt(ms + EPS) * nw_ref[...]).astype(bf16)
        xn_ref[...] = xn
        f_ref[...] = lax.dot_general(xn, wf_ref[...], _NT, preferred_element_type=f32)

    rotated = (j == _ROT_TILES[0]) | (j == _ROT_TILES[1])

    @pl.when(jnp.logical_not(rotated))
    def _():
        o_ref[...] = lax.dot_general(xn_ref[...], w_ref[...], _NT, preferred_element_type=f32).astype(o_ref.dtype)

    @pl.when(rotated)
    def _():
        acc = lax.dot_general(xn_ref[...], w_ref[...], _NT, preferred_element_type=f32)
        cos = cos_ref[...]
        sin = sin_ref[...]
        for hh in range(acc.shape[1] // HEAD_DIM):
            lanes = slice(hh * HEAD_DIM, (hh + 1) * HEAD_DIM)
            t = acc[:, lanes]
            o_ref[:, lanes] = (t * cos + pltpu.roll(t, HEAD_DIM // 2, axis=1) * sin).astype(o_ref.dtype)


def _in_projection(x2, norm_w, w_main_t, w_f_t, cos, sin):
    t, d = x2.shape
    n = w_main_t.shape[0]
    tm = min(PROJ_ROWS, t)
    tn = PROJ_COLS
    return pl.pallas_call(
        _inproj_kernel,
        grid=(t // tm, n // tn),
        in_specs=[pl.BlockSpec((tm, d), lambda i, j: (i, 0)),
                  pl.BlockSpec((1, d), lambda i, j: (0, 0)),
                  pl.BlockSpec((tn, d), lambda i, j: (j, 0)),
                  pl.BlockSpec((LANES, d), lambda i, j: (0, 0)),
                  pl.BlockSpec((tm, HEAD_DIM), lambda i, j: (i, 0)),
                  pl.BlockSpec((tm, HEAD_DIM), lambda i, j: (i, 0))],
        out_specs=[pl.BlockSpec((tm, tn), lambda i, j: (i, j)),
                   pl.BlockSpec((tm, LANES), lambda i, j: (i, 0))],
        out_shape=[jax.ShapeDtypeStruct((t, n), bf16), jax.ShapeDtypeStruct((t, LANES), f32)],
        scratch_shapes=[pltpu.VMEM((tm, d), bf16)],
        compiler_params=_cparams(("parallel", "arbitrary")),
        name="in_projection",
    )(x2, norm_w, w_main_t, w_f_t, cos, sin)


def _fcum_kernel(f_ref, b_ref, c_ref):
    z = f_ref[...] + b_ref[...]
    ls = jnp.minimum(z, 0.0) - jnp.log1p(jnp.exp(-jnp.abs(z)))
    c = ls.T[:FOX_HEADS, :]
    s = c.shape[1]
    lane = lax.broadcasted_iota(jnp.int32, c.shape, 1)
    sh = 1
    while sh < s:
        c = c + jnp.where(lane >= sh, pltpu.roll(c, sh, axis=1), 0.0)
        sh *= 2
    c_ref[0] = c * LOG2E


def _forget_cumsum(f_logit, bias, b, s):
    return pl.pallas_call(
        _fcum_kernel,
        grid=(b,),
        in_specs=[pl.BlockSpec((s, LANES), lambda i: (i, 0)),
                  pl.BlockSpec((1, LANES), lambda i: (0, 0))],
        out_specs=pl.BlockSpec((1, FOX_HEADS, s), lambda i: (i, 0, 0)),
        out_shape=jax.ShapeDtypeStruct((b, FOX_HEADS, s), f32),
        compiler_params=_cparams(("parallel",)),
        name="forget_cumsum",
    )(f_logit, bias)


def _fox_kernel(q_ref, k_ref, v_ref, c_ref, o_ref, va_ref, acc_ref, *, tq):
    s_len = q_ref.shape[0]
    nq = s_len // tq
    n_heads = q_ref.shape[1] // HEAD_DIM
    row = lax.broadcasted_iota(jnp.int32, (tq, tq), 0)
    col = lax.broadcasted_iota(jnp.int32, (tq, tq), 1)
    causal = row >= col
    def scores(hh):
        lanes = slice(hh * HEAD_DIM, (hh + 1) * HEAD_DIM)
        va_ref[hh, :, :HEAD_DIM] = v_ref[:, lanes]
        va_ref[hh, :, HEAD_DIM:] = jnp.ones((s_len, HEAD_DIM), va_ref.dtype)
        panels = []
        row_max = [None] * nq
        for kb in range(nq):
            r0 = kb * tq
            s = lax.dot_general(q_ref[r0:, lanes], k_ref[r0:r0 + tq, lanes], _NT, preferred_element_type=f32)
            s = s - c_ref[hh, :, r0:r0 + tq]
            parts = [jnp.where(causal, s[:tq], NEG_BIG)]
            parts += [s[(j - kb) * tq:(j - kb + 1) * tq] for j in range(kb + 1, nq)]
            panels.append(parts)
            for j, part in zip(range(kb, nq), parts):
                pm = jnp.max(part, axis=-1, keepdims=True)
                row_max[j] = pm if row_max[j] is None else jnp.maximum(row_max[j], pm)
        return panels, row_max

    def values(hh, panels, row_max):
        lanes = slice(hh * HEAD_DIM, (hh + 1) * HEAD_DIM)
        for kb in range(nq):
            r0 = kb * tq
            p = jnp.concatenate([jnp.exp2(part - row_max[j]).astype(bf16)
                                 for j, part in zip(range(kb, nq), panels[kb])], axis=0)
            contrib = jnp.dot(p, va_ref[hh, r0:r0 + tq, :], preferred_element_type=f32)
            if kb:
                acc_ref[hh, r0:, :] += contrib
            else:
                acc_ref[hh] = contrib
        acc = acc_ref[hh]
        o_ref[:, lanes] = (acc[:, :HEAD_DIM] / acc[:, HEAD_DIM:]).astype(o_ref.dtype)

    pending = None
    for hh in range(n_heads):
        current = (hh,) + scores(hh)
        if pending is not None:
            values(*pending)
        pending = current
    values(*pending)


def _fox_attention(proj, c_blk, b, s):
    tq = min(ATTN_BLOCK, s)
    hp = FOX_HEADS_PER_STEP
    w = hp * HEAD_DIM
    col = lambda c0: pl.BlockSpec((s, w), lambda i, j: (i, c0 // hp + j))
    return pl.pallas_call(
        functools.partial(_fox_kernel, tq=tq),
        grid=(b, FOX_HEADS // hp),
        in_specs=[col(COL_FQ), col(COL_FK), col(COL_FV),
                  pl.BlockSpec((hp, 1, s), lambda i, j: (i * (FOX_HEADS // hp) + j, 0, 0))],
        out_specs=pl.BlockSpec((s, w), lambda i, j: (i, j)),
        out_shape=jax.ShapeDtypeStruct((b * s, FOX_WIDTH), bf16),
        scratch_shapes=[pltpu.VMEM((hp, s, 2 * HEAD_DIM), bf16), pltpu.VMEM((hp, s, 2 * HEAD_DIM), f32)],
        compiler_params=_cparams(("parallel", "parallel")),
        name="fox_attention",
    )(proj, proj, proj, c_blk)


def _ret_kernel(q_ref, k_ref, v_ref, g_ref, dec_ref, xi_ref, zeta_ref, gc_ref, gnw_ref, o_ref, *, chunk):
    s_len = q_ref.shape[0]
    nc = s_len // chunk
    n_heads = q_ref.shape[1] // HEAD_DIM
    states = [None] * n_heads
    for ci in range(nc):
        rows = slice(ci * chunk, (ci + 1) * chunk)
        for hh in range(n_heads):
            lanes = slice(hh * HEAD_DIM, (hh + 1) * HEAD_DIM)
            state = states[hh]
            qb = q_ref[rows, lanes]
            kb = k_ref[rows, lanes]
            v = v_ref[rows, lanes]
            inner = lax.dot_general(qb, kb, _NT, preferred_element_type=f32) * dec_ref[hh]
            o = jnp.dot(inner.astype(bf16), v, preferred_element_type=f32)
            if ci:
                o = o + jnp.dot(qb, state.astype(bf16), preferred_element_type=f32) * xi_ref[hh]
            if ci + 1 < nc:
                kz = (kb.astype(f32) * zeta_ref[hh]).astype(bf16)
                kv = lax.dot_general(kz, v, (((0,), (0,)), ((), ())), preferred_element_type=f32)
                states[hh] = state * gc_ref[hh] + kv if ci else kv
            mu = jnp.mean(o, axis=-1, keepdims=True)
            oc = o - mu
            var = jnp.mean(oc * oc, axis=-1, keepdims=True)
            y = oc * lax.rsqrt(var + EPS) * gnw_ref[:, lanes]
            g = g_ref[rows, lanes].astype(f32)
            o_ref[rows, lanes] = (g * jax.nn.sigmoid(g) * y).astype(o_ref.dtype)


def _retention_constants(chunk):
    hh = np.arange(RET_HEADS, dtype=np.float64)
    log_gamma = np.log1p(-(2.0 ** (-5.0 - hh)))
    n = np.arange(chunk, dtype=np.float64)
    diff = n[:, None] - n[None, :]
    decay = np.where(diff[None] >= 0, np.exp(diff[None] * log_gamma[:, None, None]), 0.0)
    xi = np.exp((n[None, :] + 1.0) * log_gamma[:, None])
    zeta = np.exp((chunk - 1.0 - n[None, :]) * log_gamma[:, None])
    g_chunk = np.exp(chunk * log_gamma)
    bc = lambda a: np.broadcast_to(a[:, :, None], (RET_HEADS, chunk, HEAD_DIM))
    gc = np.broadcast_to(g_chunk[:, None, None], (RET_HEADS, 1, HEAD_DIM))
    to = lambda a: jnp.asarray(np.ascontiguousarray(a), dtype=f32)
    return to(decay), to(bc(xi)), to(bc(zeta)), to(gc)


def _retention(proj, gn_w, b, s):
    chunk = min(RET_CHUNK, s)
    decay, xi, zeta, gc = _retention_constants(chunk)
    hp = RET_HEADS_PER_STEP
    w = hp * HEAD_DIM
    head = lambda col: pl.BlockSpec((s, w), lambda i, j: (i, col // hp + j))
    per_head = lambda shape: pl.BlockSpec((hp,) + shape, lambda i, j: (j, 0, 0))
    return pl.pallas_call(
        functools.partial(_ret_kernel, chunk=chunk),
        grid=(b, RET_HEADS // hp),
        in_specs=[head(COL_RQ), head(COL_RK), head(COL_RV), head(COL_RG),
                  per_head((chunk, chunk)), per_head((chunk, HEAD_DIM)), per_head((chunk, HEAD_DIM)),
                  per_head((1, HEAD_DIM)),
                  pl.BlockSpec((1, w), lambda i, j: (0, j))],
        out_specs=pl.BlockSpec((s, w), lambda i, j: (i, j)),
        out_shape=jax.ShapeDtypeStruct((b * s, RET_WIDTH), bf16),
        compiler_params=_cparams(("parallel", "parallel")),
        name="retention",
    )(proj, proj, proj, proj, decay, xi, zeta, gc, gn_w)


def _branch_merge_kernel(fo_ref, ro_ref, g0_ref, g1_ref, bg_ref, wb_ref, mg_ref, *, ncol):
    d = mg_ref.shape[1]
    cw = d // ncol
    fo = fo_ref[...]
    ro = ro_ref[...]
    for c in range(ncol):
        cs = slice(c * cw, (c + 1) * cw)
        bd0 = jnp.dot(fo, wb_ref[0, :, cs], preferred_element_type=f32)
        bd1 = jnp.dot(ro, wb_ref[1, :, cs], preferred_element_type=f32)
        ga = jax.nn.sigmoid(g0_ref[:, cs].astype(f32) + bg_ref[0:1, cs])
        gb = jax.nn.sigmoid(g1_ref[:, cs].astype(f32) + bg_ref[1:2, cs])
        mg_ref[:, cs] = (ga * bd0 + gb * bd1).astype(mg_ref.dtype)


def _branch_merge(fox_o, ret_o, proj, b_gate, w_branch):
    t = fox_o.shape[0]
    d = D_MODEL
    tm = min(MERGE_ROWS, t)
    gate0 = COL_GATE * HEAD_DIM // d
    return pl.pallas_call(
        functools.partial(_branch_merge_kernel, ncol=4),
        grid=(t // tm,),
        in_specs=[pl.BlockSpec((tm, FOX_WIDTH), lambda i: (i, 0)),
                  pl.BlockSpec((tm, RET_WIDTH), lambda i: (i, 0)),
                  pl.BlockSpec((tm, d), lambda i: (i, gate0)),
                  pl.BlockSpec((tm, d), lambda i: (i, gate0 + 1)),
                  _const_spec((N_BRANCH, d)),
                  _const_spec((N_BRANCH, FOX_WIDTH, d))],
        out_specs=pl.BlockSpec((tm, d), lambda i: (i, 0)),
        out_shape=jax.ShapeDtypeStruct((t, d), bf16),
        compiler_params=_cparams(("parallel",)),
        name="branch_merge",
    )(fox_o, ret_o, proj, proj, b_gate, w_branch)


def _route(logits):
    lane = lax.broadcasted_iota(jnp.int32, logits.shape, 1).astype(f32)
    far = jnp.float32(1e9)
    gmask = lane < N_GROUPS
    gl = jnp.where(gmask, logits, NEG_BIG)
    gmax = jnp.max(gl, axis=-1, keepdims=True)
    gsel = jnp.min(jnp.where(gl == gmax, lane, far), axis=-1, keepdims=True)
    p_group = 1.0 / jnp.sum(jnp.where(gmask, jnp.exp(gl - gmax), 0.0), axis=-1, keepdims=True)
    lo = N_GROUPS + EXPERTS_PER_GROUP * gsel
    emask = (lane >= lo) & (lane < lo + EXPERTS_PER_GROUP)
    el = jnp.where(emask, logits, NEG_BIG)
    m1 = jnp.max(el, axis=-1, keepdims=True)
    i1 = jnp.min(jnp.where(el == m1, lane, far), axis=-1, keepdims=True)
    el2 = jnp.where(lane == i1, NEG_BIG, el)
    m2 = jnp.max(el2, axis=-1, keepdims=True)
    i2 = jnp.min(jnp.where(el2 == m2, lane, far), axis=-1, keepdims=True)
    t = jnp.exp(m2 - m1)
    w1 = p_group / (1.0 + t)
    w2 = p_group * t / (1.0 + t)
    out = jnp.where(lane == 0, i1 - N_GROUPS, 0.0)
    out = jnp.where(lane == 1, i2 - N_GROUPS, out)
    out = jnp.where(lane == 2, w1, out)
    out = jnp.where(lane == 3, w2, out)
    return out


def _outproj_kernel(mg_ref, x_ref, wo_ref, nfw_ref, wr_ref, br_ref, h_ref, hs_ref, route_ref, cnt_ref):
    h = x_ref[...] + jnp.dot(mg_ref[...], wo_ref[...], preferred_element_type=f32)
    h_ref[...] = h
    hn = h * lax.rsqrt(jnp.mean(h * h, axis=-1, keepdims=True) + EPS) * nfw_ref[...]
    _rows_to_slabs(hs_ref, hn, hn.shape[0])
    logits = jnp.dot(hn.astype(bf16), wr_ref[...], preferred_element_type=f32) + br_ref[...]
    route = _route(logits)
    route_ref[...] = route

    @pl.when(pl.program_id(0) == 0)
    def _():
        cnt_ref[...] = jnp.zeros_like(cnt_ref)

    lane = lax.broadcasted_iota(jnp.int32, route.shape, 1).astype(f32)
    chosen = (lane == route[:, 0:1]).astype(f32) + (lane == route[:, 1:2]).astype(f32)
    cnt_ref[...] += jnp.sum(chosen, axis=0, keepdims=True)


def _outproj_router(merged, x2, w_out, nfw, w_router, b_router):
    t, d = x2.shape
    tm = min(OUTPROJ_ROWS, t)
    return pl.pallas_call(
        _outproj_kernel,
        grid=(t // tm,),
        in_specs=[pl.BlockSpec((tm, d), lambda i: (i, 0)),
                  pl.BlockSpec((tm, d), lambda i: (i, 0)),
                  _const_spec((d, d)),
                  _const_spec((1, d)),
                  _const_spec((d, LANES)), _const_spec((1, LANES))],
        out_specs=[pl.BlockSpec((tm, d), lambda i: (i, 0)),
                   pl.BlockSpec((tm * PITCH, LANES), lambda i: (i, 0)),
                   pl.BlockSpec((tm, LANES), lambda i: (i, 0)),
                   pl.BlockSpec((8, LANES), lambda i: (0, 0))],
        out_shape=[jax.ShapeDtypeStruct((t, d), f32), jax.ShapeDtypeStruct((t * PITCH, LANES), u32),
                   jax.ShapeDtypeStruct((t, LANES), f32), jax.ShapeDtypeStruct((8, LANES), f32)],
        compiler_params=_cparams(("arbitrary",)),
        name="outproj_router",
    )(merged, x2, w_out, nfw, w_router, b_router)


def _plan_kernel(route_ref, cnt_ref, dest_ref, ends_ref, carry_ref, *, bm):
    r = route_ref[...]
    tb = r.shape[0]
    lane = lax.broadcasted_iota(jnp.int32, r.shape, 1).astype(f32)
    o0 = (lane == r[:, 0:1]).astype(f32)
    o1 = (lane == r[:, 1:2]).astype(f32)
    o = o0 + o1

    @pl.when(pl.program_id(0) == 0)
    def _():
        cnt = cnt_ref[...]
        padded = jnp.floor((cnt + (bm - 1)) * (1.0 / bm)) * bm
        lane8 = lax.broadcasted_iota(jnp.int32, cnt.shape, 1)
        ends = padded
        sh = 1
        while sh < LANES:
            ends = ends + jnp.where(lane8 >= sh, pltpu.roll(ends, sh, axis=1), 0.0)
            sh *= 2
        sub = lax.broadcasted_iota(jnp.int32, cnt.shape, 0)
        ends_ref[...] = jnp.where(sub == 0, ends, jnp.where(sub == 1, padded, cnt))
        carry_ref[...] = ends - padded

    row = lax.broadcasted_iota(jnp.int32, (tb, tb), 0)
    col = lax.broadcasted_iota(jnp.int32, (tb, tb), 1)
    lower = jnp.where(col < row, 1.0, 0.0).astype(bf16)
    before = jnp.dot(lower, o.astype(bf16), preferred_element_type=f32) + carry_ref[0:1, :]
    d0 = jnp.sum(before * o0, axis=-1, keepdims=True)
    d1 = jnp.sum(before * o1, axis=-1, keepdims=True)
    dd = jnp.where(lane == 0, d0, jnp.where(lane == 1, d1, 0.0))
    dest_ref[0] = dd.T[:8, :].astype(jnp.int32)
    carry_ref[...] += jnp.sum(o, axis=0, keepdims=True)


def _dispatch_plan(route, counts, bm):
    t = route.shape[0]
    tb = min(PLAN_ROWS, t)
    nb = t // tb
    return pl.pallas_call(
        functools.partial(_plan_kernel, bm=bm),
        grid=(nb,),
        in_specs=[pl.BlockSpec((tb, LANES), lambda i: (i, 0)),
                  pl.BlockSpec((8, LANES), lambda i: (0, 0))],
        out_specs=[pl.BlockSpec((1, 8, tb), lambda i: (i, 0, 0)),
                   pl.BlockSpec((8, LANES), lambda i: (0, 0))],
        out_shape=[jax.ShapeDtypeStruct((nb, 8, tb), jnp.int32), jax.ShapeDtypeStruct((8, LANES), f32)],
        scratch_shapes=[pltpu.VMEM((8, LANES), f32)],
        compiler_params=_cparams(("arbitrary",)),
        name="dispatch_plan",
    )(route, counts)


def _dispatch_kernel(d0_ref, d1_ref, fill_ref, end_ref, nu_ref, hs_ref, xs_hbm, zbuf, sem, zsem,
                     *, tb, bm, n_blk, unroll=8):
    i = pl.program_id(0)
    base = i * tb
    blk_rows = bm * PITCH

    def pad_copy(s, n):
        return pltpu.make_async_copy(zbuf.at[pl.ds(0, n * PITCH), :], xs_hbm.at[pl.ds(s * PITCH, n * PITCH), :], zsem)

    def tail_copy(g):
        return pltpu.make_async_copy(zbuf, xs_hbm.at[pl.ds(pl.multiple_of(g * blk_rows, 8), blk_rows), :], zsem)

    @pl.when(i == 0)
    def _():
        zbuf[...] = jnp.zeros(zbuf.shape, u32)
        for op in ("start", "wait"):
            def per_expert(e, c, op=op):
                s = fill_ref[e]
                for n in PAD_FILL_CHUNKS:
                    count = (end_ref[e] - s) // n

                    def one(k, c2, s=s, n=n):
                        getattr(pad_copy(s + k * n, n), op)()
                        return c2

                    c = lax.fori_loop(0, count, one, c)
                    s = s + count * n
                return c

            def per_tail(g, c, op=op):
                getattr(tail_copy(g), op)()
                return c

            lax.fori_loop(0, N_EXPERTS, per_expert, 0)
            lax.fori_loop(nu_ref[0], n_blk, per_tail, 0)

    def body(g, c):
        for u in range(unroll):
            r = g * unroll + u
            src = hs_ref.at[pl.ds(r * PITCH, PITCH), :]
            for k, dref in enumerate((d0_ref, d1_ref)):
                d = dref[base + r]
                pltpu.make_async_copy(src, xs_hbm.at[pl.ds(d * PITCH, PITCH), :], sem).start(priority=k)
        return c

    lax.fori_loop(0, tb // unroll, body, 0)
    for _ in range(TOP_K):
        pltpu.make_async_copy(xs_hbm.at[pl.ds(0, tb * PITCH), :], xs_hbm.at[pl.ds(0, tb * PITCH), :], sem).wait()


def _dispatch_rows(dest0, dest1, fill_start, seg_end, n_used, hs, cap):
    t = dest0.shape[0]
    tb = min(DISPATCH_ROWS, t)
    bm = MOE_BLOCK
    grid_spec = pltpu.PrefetchScalarGridSpec(
        num_scalar_prefetch=5,
        grid=(t // tb,),
        in_specs=[pl.BlockSpec((tb * PITCH, LANES), lambda i, *_: (i, 0))],
        out_specs=pl.BlockSpec(memory_space=pl.ANY),
        scratch_shapes=[pltpu.VMEM((bm * PITCH, LANES), u32), pltpu.SemaphoreType.DMA(()),
                        pltpu.SemaphoreType.DMA(())],
    )
    return pl.pallas_call(
        functools.partial(_dispatch_kernel, tb=tb, bm=bm, n_blk=cap // bm),
        grid_spec=grid_spec,
        out_shape=jax.ShapeDtypeStruct((cap * PITCH, LANES), u32),
        compiler_params=_cparams(("arbitrary",)),
        name="dispatch_rows",
    )(dest0, dest1, fill_start, seg_end, n_used, hs)


def _moe_kernel(g0_ref, nb_ref, nu_ref, x_hbm, w1_ref, w3_ref, w2_ref, y_hbm,
                xbuf, ybuf, w1b, w3b, w2b, gsem, ysem, *, bm, n_blk):
    e = pl.program_id(0)
    n_used = nu_ref[0]
    g0 = g0_ref[e]
    nb = nb_ref[e]
    blk_rows = bm * PITCH

    def x_copy(slot, g):
        src = x_hbm.at[pl.ds(pl.multiple_of(g * blk_rows, 8), blk_rows), :]
        return pltpu.make_async_copy(src, xbuf.at[slot], gsem.at[slot])

    def y_copy(slot, g):
        dst = y_hbm.at[pl.ds(pl.multiple_of(g * blk_rows, 8), blk_rows), :]
        return pltpu.make_async_copy(ybuf.at[slot], dst, ysem.at[slot])

    @pl.when(e == 0)
    def _():
        x_copy(0, 0).start()
        for k in range(1, X_BUFS - 1):
            pl.when(n_used > k)(lambda k=k: x_copy(k, k).start())

    def run_block(g, cast_weights):
        slot = g % 2
        xslot = g % X_BUFS
        ahead = g + (X_BUFS - 1)

        @pl.when(ahead < n_used)
        def _():
            x_copy(ahead % X_BUFS, ahead).start()

        x_copy(xslot, g).wait()

        if cast_weights:
            w1b[...] = w1_ref[0].astype(bf16)
        hn = _slabs_to_rows(xbuf.at[xslot], bm).astype(bf16)
        a = jnp.dot(hn, w1b[...], preferred_element_type=f32)
        if cast_weights:
            w3b[...] = w3_ref[0].astype(bf16)
        b = jnp.dot(hn, w3b[...], preferred_element_type=f32)
        if cast_weights:
            w2b[...] = w2_ref[0].astype(bf16)
        mid = (a * jax.nn.sigmoid(a) * b).astype(bf16)
        y = jnp.dot(mid, w2b[...], preferred_element_type=f32)

        @pl.when(g >= 2)
        def _():
            y_copy(slot, g - 2).wait()

        _rows_to_slabs(ybuf.at[slot], y, bm)
        y_copy(slot, g).start()

    @pl.when(nb > 0)
    def _():
        run_block(g0, True)

    def block(c, carry):
        run_block(g0 + c, False)
        return carry

    lax.fori_loop(1, nb, block, 0)

    @pl.when(e == pl.num_programs(0) - 1)
    def _():
        @pl.when(n_used >= 2)
        def _():
            y_copy(n_used % 2, n_used - 2).wait()

        y_copy((n_used - 1) % 2, n_used - 1).wait()
        ybuf[0] = jnp.zeros(ybuf.shape[1:], u32)

        def zstart(g, carry):
            y_copy(0, g).start()
            return carry

        def zwait(g, carry):
            y_copy(0, g).wait()
            return carry

        lax.fori_loop(n_used, n_blk, zstart, 0)
        lax.fori_loop(n_used, n_blk, zwait, 0)


def _expert_mlp(blk_start, blk_count, n_used, xs, w1, w3, w2, cap):
    d = D_MODEL
    bm = MOE_BLOCK
    n_blk = cap // bm
    by_expert = lambda shape: pl.BlockSpec(shape, lambda e, g0, nb, nu: (e, 0, 0))
    grid_spec = pltpu.PrefetchScalarGridSpec(
        num_scalar_prefetch=3,
        grid=(N_EXPERTS,),
        in_specs=[pl.BlockSpec(memory_space=pl.ANY),
                  by_expert((1, d, D_EXPERT)), by_expert((1, d, D_EXPERT)), by_expert((1, D_EXPERT, d))],
        out_specs=pl.BlockSpec(memory_space=pl.ANY),
        scratch_shapes=[pltpu.VMEM((X_BUFS, bm * PITCH, LANES), u32), pltpu.VMEM((2, bm * PITCH, LANES), u32),
                        pltpu.VMEM((d, D_EXPERT), bf16), pltpu.VMEM((d, D_EXPERT), bf16),
                        pltpu.VMEM((D_EXPERT, d), bf16),
                        pltpu.SemaphoreType.DMA((X_BUFS,)), pltpu.SemaphoreType.DMA((2,))],
    )
    return pl.pallas_call(
        functools.partial(_moe_kernel, bm=bm, n_blk=n_blk),
        grid_spec=grid_spec,
        out_shape=jax.ShapeDtypeStruct((cap * PITCH, LANES), u32),
        compiler_params=_cparams(("arbitrary",)),
        name="expert_mlp",
    )(blk_start, blk_count, n_used, xs, w1, w3, w2)


def _combine_kernel(d0_ref, d1_ref, h_ref, route_ref, y_hbm, nw_ref, o_ref, ybuf, sem, *, tb):
    i = pl.program_id(0)
    n = pl.num_programs(0)
    slot = i % 2

    def start(blk, s):
        _slab_gather_start(d0_ref, blk * tb, tb, y_hbm, ybuf.at[s, 0], sem.at[s])
        _slab_gather_start(d1_ref, blk * tb, tb, y_hbm, ybuf.at[s, 1], sem.at[s])

    @pl.when(i == 0)
    def _():
        start(0, 0)

    @pl.when(i + 1 < n)
    def _():
        start(i + 1, 1 - slot)

    for k in range(TOP_K):
        _slab_gather_wait(tb, y_hbm, ybuf.at[slot, k], sem.at[slot])
    r = route_ref[...]
    h = h_ref[...] + (r[:, 2:3] * _slabs_to_rows(ybuf.at[slot, 0], tb)
                      + r[:, 3:4] * _slabs_to_rows(ybuf.at[slot, 1], tb))
    o_ref[...] = h * lax.rsqrt(jnp.mean(h * h, axis=-1, keepdims=True) + EPS) * nw_ref[...]


def _combine(dest0, dest1, h, route, ys, norm_w):
    t, d = h.shape
    tb = min(COMBINE_ROWS, t)
    grid_spec = pltpu.PrefetchScalarGridSpec(
        num_scalar_prefetch=2,
        grid=(t // tb,),
        in_specs=[pl.BlockSpec((tb, d), lambda i, a, b: (i, 0)),
                  pl.BlockSpec((tb, LANES), lambda i, a, b: (i, 0)),
                  pl.BlockSpec(memory_space=pl.ANY),
                  pl.BlockSpec((1, d), lambda i, a, b: (0, 0))],
        out_specs=pl.BlockSpec((tb, d), lambda i, a, b: (i, 0)),
        scratch_shapes=[pltpu.VMEM((2, TOP_K, tb * PITCH, LANES), u32), pltpu.SemaphoreType.DMA((2,))],
    )
    return pl.pallas_call(
        functools.partial(_combine_kernel, tb=tb),
        grid_spec=grid_spec,
        out_shape=jax.ShapeDtypeStruct((t, d), f32),
        compiler_params=_cparams(("arbitrary",)),
        name="combine_final_norm",
    )(dest0, dest1, h, route, ys, norm_w)


def kernel(x, positions, norm_mix_w, w_in, fox_b_f, ret_gn_w, w_branch, b_gate, w_out, norm_ffn_w, w_router_group,
           b_router_group, w_router_expert, b_router_expert, w1, w3, w2, norm_final_w):
    b, s, d = x.shape
    assert d == D_MODEL and norm_mix_w.shape[0] == 1
    t = b * s
    x2 = x.reshape(t, d)

    wt = w_in[0].T
    o_f = 3 * FOX_WIDTH
    o_r = o_f + FOX_HEADS
    o_g = o_r + 4 * RET_WIDTH
    tn = PROJ_COLS
    tile_rows = ([o_g + k * tn for k in range(N_BRANCH * D_MODEL // tn)]
                 + [k * tn for k in range(o_f // tn)]
                 + [o_r + k * tn for k in range(4 * RET_WIDTH // tn)])
    w_main = _prep_in_weights(wt, tile_rows)
    w_f = jnp.pad(wt[o_f:o_r], ((0, LANES - FOX_HEADS), (0, 0))).astype(bf16)
    bias_f = jnp.pad(fox_b_f[0], (0, LANES - FOX_HEADS)).reshape(1, LANES)
    w_r = jnp.concatenate([w_router_group[0], w_router_expert[0]], axis=1)
    w_r = jnp.pad(w_r, ((0, 0), (0, LANES - w_r.shape[1])))
    b_r = jnp.pad(jnp.concatenate([b_router_group[0], b_router_expert[0]]), (0, LANES - N_GROUPS - N_EXPERTS))
    b_r = b_r.reshape(1, LANES)

    cos, sin = _rope_tables(positions)
    proj, f_logit = _in_projection(x2, norm_mix_w, w_main, w_f, cos.reshape(t, HEAD_DIM), sin.reshape(t, HEAD_DIM))
    c = _forget_cumsum(f_logit, bias_f, b, s)
    fox_o = _fox_attention(proj, c.reshape(b * FOX_HEADS, 1, s), b, s)
    ret_o = _retention(proj, ret_gn_w, b, s)
    merged = _branch_merge(fox_o, ret_o, proj, b_gate[0], w_branch[0].astype(bf16))
    h, h_slabs, route, counts = _outproj_router(merged, x2, w_out[0].astype(bf16), norm_ffn_w, w_r.astype(bf16), b_r)

    bm = MOE_BLOCK
    cap = t * TOP_K + N_EXPERTS * bm
    dest_rows, seg = _dispatch_plan(route, counts, bm)
    dest0 = dest_rows[:, 0, :].reshape(t)
    dest1 = dest_rows[:, 1, :].reshape(t)
    pends = seg[0, :N_EXPERTS].astype(jnp.int32)
    padded = seg[1, :N_EXPERTS].astype(jnp.int32)
    counts = seg[2, :N_EXPERTS].astype(jnp.int32)
    pstarts = pends - padded
    n_used = pends[-1:] // bm

    xs = _dispatch_rows(dest0, dest1, pstarts + counts, pends, n_used, h_slabs, cap)
    y = _expert_mlp(pstarts // bm, padded // bm, n_used, xs, w1[0], w3[0], w2[0], cap)
    out = _combine(dest0, dest1, h, route, y, norm_final_w.reshape(1, d))
    return out.reshape(b, s, d)
```

```python
import functools
import math

import numpy as np
import jax
import jax.numpy as jnp
from jax import lax
from jax.experimental import pallas as pl
from jax.experimental.pallas import tpu as pltpu

D_MODEL = 2048
HEAD_DIM = 128
FOX_HEADS = 8
RET_HEADS = 8
FOX_WIDTH = FOX_HEADS * HEAD_DIM
RET_WIDTH = RET_HEADS * HEAD_DIM
N_BRANCH = 2
N_GROUPS = 4
EXPERTS_PER_GROUP = 8
N_EXPERTS = N_GROUPS * EXPERTS_PER_GROUP
TOP_K = 2
D_EXPERT = 512
ROPE_BASE = 10000.0
EPS = 1e-6

LANES = 128
NEG_BIG = -1e30
LOG2E = math.log2(math.e)
COL_GATE = 0
COL_FQ = N_BRANCH * D_MODEL // HEAD_DIM
COL_FK = COL_FQ + FOX_HEADS
COL_FV = COL_FK + FOX_HEADS
COL_RQ = COL_FV + FOX_HEADS
COL_RK = COL_RQ + RET_HEADS
COL_RV = COL_RK + RET_HEADS
COL_RG = COL_RV + RET_HEADS

ATTN_BLOCK = 256
FOX_HEADS_PER_STEP = 4
RET_HEADS_PER_STEP = 4
RET_CHUNK = 256
MOE_BLOCK = 256
X_BUFS = 4
PAD_FILL_CHUNKS = (64, 8, 1)
PROJ_ROWS, PROJ_COLS = 1024, 1024
MERGE_ROWS = 1024
OUTPROJ_ROWS = 512
PLAN_ROWS = 1024
DISPATCH_ROWS = 2048
COMBINE_ROWS = 256
VMEM_LIMIT = 56 * 1024 * 1024

f32 = jnp.float32
bf16 = jnp.bfloat16
_NT = (((1,), (1,)), ((), ()))
SLAB = D_MODEL // (2 * LANES)
PITCH = SLAB + 1
u32 = jnp.uint32
_HI_MASK = np.uint32(0xFFFF0000)


def _rows_to_slabs(ref2, val, n_rows):
    bits = lambda v: lax.bitcast_convert_type(v.astype(bf16).astype(f32), u32)
    for j in range(SLAB):
        lo = val[:, (2 * j) * LANES:(2 * j + 1) * LANES]
        hi = val[:, (2 * j + 1) * LANES:(2 * j + 2) * LANES]
        ref2[pl.ds(j, n_rows, stride=PITCH), :] = (bits(lo) >> 16) | (bits(hi) & _HI_MASK)
    ref2[pl.ds(SLAB, n_rows, stride=PITCH), :] = jnp.zeros((n_rows, LANES), u32)


def _slabs_to_rows(ref2, n_rows):
    chunks = []
    for j in range(SLAB):
        w = ref2[pl.ds(j, n_rows, stride=PITCH), :]
        chunks.append(lax.bitcast_convert_type(w << 16, f32))
        chunks.append(lax.bitcast_convert_type(w & _HI_MASK, f32))
    return jnp.concatenate(chunks, axis=-1)


def _slab_gather_start(idx_ref, base, n_rows, src_hbm, dst, sem, unroll=8):
    def body(g, _):
        for u in range(unroll):
            r = g * unroll + u
            t = idx_ref[base + r]
            pltpu.make_async_copy(src_hbm.at[pl.ds(t * PITCH, SLAB), :], dst.at[pl.ds(r * PITCH, SLAB), :],
                                  sem).start(priority=u % 2)
        return 0
    lax.fori_loop(0, n_rows // unroll, body, 0)


def _slab_gather_wait(n_rows, src_hbm, dst, sem):
    pltpu.make_async_copy(src_hbm.at[pl.ds(0, n_rows * SLAB), :], dst.at[pl.ds(0, n_rows * SLAB), :], sem).wait()


def _cparams(sem, vmem=VMEM_LIMIT):
    return pltpu.CompilerParams(dimension_semantics=sem, vmem_limit_bytes=vmem)


def _const_spec(shape):
    return pl.BlockSpec(shape, lambda i: tuple(0 for _ in shape), pipeline_mode=pl.Buffered(1))


def _rope_kernel(pos_ref, freq_ref, sgn_ref, cos_ref, sin_ref):
    pos = pos_ref[0].astype(f32)
    hs = pos.shape[0] // 2
    half = HEAD_DIM // 2
    lane = lax.broadcasted_iota(jnp.int32, (hs, HEAD_DIM), 1)
    low = lane < half
    ang = jnp.where(low, pos[:hs], pos[hs:]) * freq_ref[...]
    c = jnp.cos(ang)
    s = jnp.sin(ang) * sgn_ref[...]
    c_sw = pltpu.roll(c, half, axis=1)
    s_sw = pltpu.roll(s, half, axis=1)
    cos_ref[0, :hs, :] = jnp.where(low, c, c_sw)
    cos_ref[0, hs:, :] = jnp.where(low, c_sw, c)
    sin_ref[0, :hs, :] = jnp.where(low, s, -s_sw)
    sin_ref[0, hs:, :] = jnp.where(low, -s_sw, s)


def _rope_tables(positions):
    b, s = positions.shape
    half = HEAD_DIM // 2
    inv = (np.float32(ROPE_BASE) ** (-np.arange(half, dtype=np.float32) / np.float32(half))).astype(np.float32)
    freq = jnp.asarray(np.concatenate([inv, inv])[None, :])
    sgn = jnp.asarray(np.concatenate([-np.ones(half, np.float32), np.ones(half, np.float32)])[None, :])
    out = jax.ShapeDtypeStruct((b, s, HEAD_DIM), f32)
    return pl.pallas_call(
        _rope_kernel,
        grid=(b,),
        in_specs=[pl.BlockSpec((1, s, 1), lambda i: (i, 0, 0)),
                  pl.BlockSpec((1, HEAD_DIM), lambda i: (0, 0)),
                  pl.BlockSpec((1, HEAD_DIM), lambda i: (0, 0))],
        out_specs=[pl.BlockSpec((1, s, HEAD_DIM), lambda i: (i, 0, 0)),
                   pl.BlockSpec((1, s, HEAD_DIM), lambda i: (i, 0, 0))],
        out_shape=[out, out],
        compiler_params=_cparams(("parallel",)),
        name="rope_tables",
    )(positions.reshape(b, s, 1), freq, sgn)


_Q_FACTORS = ((COL_FQ * HEAD_DIM // PROJ_COLS, LOG2E / math.sqrt(HEAD_DIM)),
              (COL_RQ * HEAD_DIM // PROJ_COLS, 1.0 / math.sqrt(HEAD_DIM)))


def _wprep_kernel(off_ref, w_hbm, o_ref, buf, sem, *, tn):
    j = pl.program_id(0)
    n = pl.num_programs(0)
    slot = j % 2

    def copy(jj, s):
        src = w_hbm.at[pl.ds(pl.multiple_of(off_ref[jj], 8), tn), :]
        return pltpu.make_async_copy(src, buf.at[s], sem.at[s])

    @pl.when(j == 0)
    def _():
        copy(0, 0).start()

    @pl.when(j + 1 < n)
    def _():
        copy(j + 1, 1 - slot).start()

    copy(j, slot).wait()
    factor = jnp.float32(1.0)
    for tile, value in _Q_FACTORS:
        factor = jnp.where(j == tile, jnp.float32(value), factor)
    o_ref[...] = (buf[slot] * factor).astype(o_ref.dtype)


def _prep_in_weights(wt, row_offsets):
    tn = PROJ_COLS
    d = wt.shape[1]
    n_tiles = len(row_offsets)
    grid_spec = pltpu.PrefetchScalarGridSpec(
        num_scalar_prefetch=1,
        grid=(n_tiles,),
        in_specs=[pl.BlockSpec(memory_space=pl.ANY)],
        out_specs=pl.BlockSpec((tn, d), lambda j, off: (j, 0)),
        scratch_shapes=[pltpu.VMEM((2, tn, d), f32), pltpu.SemaphoreType.DMA((2,))],
    )
    return pl.pallas_call(
        functools.partial(_wprep_kernel, tn=tn),
        grid_spec=grid_spec,
        out_shape=jax.ShapeDtypeStruct((n_tiles * tn, d), bf16),
        compiler_params=_cparams(("arbitrary",)),
        name="prep_in_weights",
    )(jnp.asarray(row_offsets, jnp.int32), wt)


_ROT_TILES = (COL_RQ * HEAD_DIM // PROJ_COLS, COL_RK * HEAD_DIM // PROJ_COLS)


def _inproj_kernel(x_hbm, nw_ref, w_ref, wf_ref, cos_ref, sin_ref, o_ref, f_ref, xn_ref, xbuf, xsem):
    i = pl.program_id(0)
    j = pl.program_id(1)
    tm = xbuf.shape[0]

    def x_copy(blk):
        src = x_hbm.at[pl.ds(pl.multiple_of(blk * tm, tm), tm), :]
        return pltpu.make_async_copy(src, xbuf, xsem.at[0])

    @pl.when((j == 0) & (i == 0))
    def _():
        x_copy(0).start()

    @pl.when(j == 0)
    def _():
        x_copy(i).wait()
        x = xbuf[...]
        ms = jnp.mean(x * x, axis=-1, keepdims=True)
        xn = (x * lax.rsqrt(ms + EPS) * nw_ref[...]).astype(bf16)
        xn_ref[...] = xn
        f_ref[...] = lax.dot_general(xn, wf_ref[...], _NT, preferred_element_type=f32)

    @pl.when((j == 0) & (i + 1 < pl.num_programs(0)))
    def _():
        x_copy(i + 1).start()

    rotated = (j == _ROT_TILES[0]) | (j == _ROT_TILES[1])

    @pl.when(jnp.logical_not(rotated))
    def _():
        o_ref[...] = lax.dot_general(xn_ref[...], w_ref[...], _NT, preferred_element_type=f32).astype(o_ref.dtype)

    @pl.when(rotated)
    def _():
        acc = lax.dot_general(xn_ref[...], w_ref[...], _NT, preferred_element_type=f32)
        cos = cos_ref[...]
        sin = sin_ref[...]
        for hh in range(acc.shape[1] // HEAD_DIM):
            lanes = slice(hh * HEAD_DIM, (hh + 1) * HEAD_DIM)
            t = acc[:, lanes]
            o_ref[:, lanes] = (t * cos + pltpu.roll(t, HEAD_DIM // 2, axis=1) * sin).astype(o_ref.dtype)


def _in_projection(x2, norm_w, w_main_t, w_f_t, cos, sin):
    t, d = x2.shape
    n = w_main_t.shape[0]
    tm = min(PROJ_ROWS, t)
    tn = PROJ_COLS
    return pl.pallas_call(
        _inproj_kernel,
        grid=(t // tm, n // tn),
        in_specs=[pl.BlockSpec(memory_space=pl.ANY),
                  pl.BlockSpec((1, d), lambda i, j: (0, 0)),
                  pl.BlockSpec((tn, d), lambda i, j: (j, 0)),
                  pl.BlockSpec((LANES, d), lambda i, j: (0, 0)),
                  pl.BlockSpec((tm, HEAD_DIM), lambda i, j: (i, 0)),
                  pl.BlockSpec((tm, HEAD_DIM), lambda i, j: (i, 0))],
        out_specs=[pl.BlockSpec((tm, tn), lambda i, j: (i, j)),
                   pl.BlockSpec((tm, LANES), lambda i, j: (i, 0))],
        out_shape=[jax.ShapeDtypeStruct((t, n), bf16), jax.ShapeDtypeStruct((t, LANES), f32)],
        scratch_shapes=[pltpu.VMEM((tm, d), bf16), pltpu.VMEM((tm, d), f32), pltpu.SemaphoreType.DMA((1,))],
        compiler_params=_cparams(("arbitrary", "arbitrary")),
        name="in_projection",
    )(x2, norm_w, w_main_t, w_f_t, cos, sin)


def _fcum_kernel(f_ref, b_ref, c_ref):
    z = f_ref[...] + b_ref[...]
    ls = jnp.minimum(z, 0.0) - jnp.log1p(jnp.exp(-jnp.abs(z)))
    c = ls.T[:FOX_HEADS, :]
    s = c.shape[1]
    lane = lax.broadcasted_iota(jnp.int32, c.shape, 1)
    sh = 1
    while sh < s:
        c = c + jnp.where(lane >= sh, pltpu.roll(c, sh, axis=1), 0.0)
        sh *= 2
    c_ref[0] = c * LOG2E


def _forget_cumsum(f_logit, bias, b, s):
    return pl.pallas_call(
        _fcum_kernel,
        grid=(b,),
        in_specs=[pl.BlockSpec((s, LANES), lambda i: (i, 0)),
                  pl.BlockSpec((1, LANES), lambda i: (0, 0))],
        out_specs=pl.BlockSpec((1, FOX_HEADS, s), lambda i: (i, 0, 0)),
        out_shape=jax.ShapeDtypeStruct((b, FOX_HEADS, s), f32),
        compiler_params=_cparams(("parallel",)),
        name="forget_cumsum",
    )(f_logit, bias)


def _fox_kernel(q_ref, k_ref, v_ref, c_ref, o_ref, va_ref, acc_ref, *, tq):
    s_len = q_ref.shape[0]
    nq = s_len // tq
    n_heads = q_ref.shape[1] // HEAD_DIM
    row = lax.broadcasted_iota(jnp.int32, (tq, tq), 0)
    col = lax.broadcasted_iota(jnp.int32, (tq, tq), 1)
    causal = row >= col
    def scores(hh):
        lanes = slice(hh * HEAD_DIM, (hh + 1) * HEAD_DIM)
        va_ref[hh, :, :HEAD_DIM] = v_ref[:, lanes]
        va_ref[hh, :, HEAD_DIM:] = jnp.ones((s_len, HEAD_DIM), va_ref.dtype)
        panels = []
        row_max = [None] * nq
        for kb in range(nq):
            r0 = kb * tq
            s = lax.dot_general(q_ref[r0:, lanes], k_ref[r0:r0 + tq, lanes], _NT, preferred_element_type=f32)
            s = s - c_ref[hh, :, r0:r0 + tq]
            parts = [jnp.where(causal, s[:tq], NEG_BIG)]
            parts += [s[(j - kb) * tq:(j - kb + 1) * tq] for j in range(kb + 1, nq)]
            panels.append(parts)
            for j, part in zip(range(kb, nq), parts):
                pm = jnp.max(part, axis=-1, keepdims=True)
                row_max[j] = pm if row_max[j] is None else jnp.maximum(row_max[j], pm)
        return panels, row_max

    def values(hh, panels, row_max):
        lanes = slice(hh * HEAD_DIM, (hh + 1) * HEAD_DIM)
        for kb in range(nq):
            r0 = kb * tq
            p = jnp.concatenate([jnp.exp2(part - row_max[j]).astype(bf16)
                                 for j, part in zip(range(kb, nq), panels[kb])], axis=0)
            contrib = jnp.dot(p, va_ref[hh, r0:r0 + tq, :], preferred_element_type=f32)
            if kb:
                acc_ref[hh, r0:, :] += contrib
            else:
                acc_ref[hh] = contrib
        acc = acc_ref[hh]
        o_ref[:, lanes] = (acc[:, :HEAD_DIM] / acc[:, HEAD_DIM:]).astype(o_ref.dtype)

    pending = None
    for hh in range(n_heads):
        current = (hh,) + scores(hh)
        if pending is not None:
            values(*pending)
        pending = current
    values(*pending)


def _fox_attention(proj, c_blk, b, s):
    tq = min(ATTN_BLOCK, s)
    hp = FOX_HEADS_PER_STEP
    w = hp * HEAD_DIM
    col = lambda c0: pl.BlockSpec((s, w), lambda i, j: (i, c0 // hp + j))
    return pl.pallas_call(
        functools.partial(_fox_kernel, tq=tq),
        grid=(b, FOX_HEADS // hp),
        in_specs=[col(COL_FQ), col(COL_FK), col(COL_FV),
                  pl.BlockSpec((hp, 1, s), lambda i, j: (i * (FOX_HEADS // hp) + j, 0, 0))],
        out_specs=pl.BlockSpec((s, w), lambda i, j: (i, j)),
        out_shape=jax.ShapeDtypeStruct((b * s, FOX_WIDTH), bf16),
        scratch_shapes=[pltpu.VMEM((hp, s, 2 * HEAD_DIM), bf16), pltpu.VMEM((hp, s, 2 * HEAD_DIM), f32)],
        compiler_params=_cparams(("parallel", "parallel")),
        name="fox_attention",
    )(proj, proj, proj, c_blk)


def _ret_kernel(q_ref, k_ref, v_ref, g_ref, dec_ref, xi_ref, zeta_ref, gc_ref, gnw_ref, o_ref, *, chunk):
    s_len = q_ref.shape[0]
    nc = s_len // chunk
    n_heads = q_ref.shape[1] // HEAD_DIM
    states = [None] * n_heads
    for ci in range(nc):
        rows = slice(ci * chunk, (ci + 1) * chunk)
        for hh in range(n_heads):
            lanes = slice(hh * HEAD_DIM, (hh + 1) * HEAD_DIM)
            state = states[hh]
            qb = q_ref[rows, lanes]
            kb = k_ref[rows, lanes]
            v = v_ref[rows, lanes]
            inner = lax.dot_general(qb, kb, _NT, preferred_element_type=f32) * dec_ref[hh]
            o = jnp.dot(inner.astype(bf16), v, preferred_element_type=f32)
            if ci:
                o = o + jnp.dot(qb, state.astype(bf16), preferred_element_type=f32) * xi_ref[hh]
            if ci + 1 < nc:
                kz = (kb.astype(f32) * zeta_ref[hh]).astype(bf16)
                kv = lax.dot_general(kz, v, (((0,), (0,)), ((), ())), preferred_element_type=f32)
                states[hh] = state * gc_ref[hh] + kv if ci else kv
            mu = jnp.mean(o, axis=-1, keepdims=True)
            oc = o - mu
            var = jnp.mean(oc * oc, axis=-1, keepdims=True)
            y = oc * lax.rsqrt(var + EPS) * gnw_ref[:, lanes]
            g = g_ref[rows, lanes].astype(f32)
            o_ref[rows, lanes] = (g * jax.nn.sigmoid(g) * y).astype(o_ref.dtype)


def _retention_constants(chunk):
    hh = np.arange(RET_HEADS, dtype=np.float64)
    log_gamma = np.log1p(-(2.0 ** (-5.0 - hh)))
    n = np.arange(chunk, dtype=np.float64)
    diff = n[:, None] - n[None, :]
    decay = np.where(diff[None] >= 0, np.exp(diff[None] * log_gamma[:, None, None]), 0.0)
    xi = np.exp((n[None, :] + 1.0) * log_gamma[:, None])
    zeta = np.exp((chunk - 1.0 - n[None, :]) * log_gamma[:, None])
    g_chunk = np.exp(chunk * log_gamma)
    bc = lambda a: np.broadcast_to(a[:, :, None], (RET_HEADS, chunk, HEAD_DIM))
    gc = np.broadcast_to(g_chunk[:, None, None], (RET_HEADS, 1, HEAD_DIM))
    to = lambda a: jnp.asarray(np.ascontiguousarray(a), dtype=f32)
    return to(decay), to(bc(xi)), to(bc(zeta)), to(gc)


def _retention(proj, gn_w, b, s):
    chunk = min(RET_CHUNK, s)
    decay, xi, zeta, gc = _retention_constants(chunk)
    hp = RET_HEADS_PER_STEP
    w = hp * HEAD_DIM
    head = lambda col: pl.BlockSpec((s, w), lambda i, j: (i, col // hp + j))
    per_head = lambda shape: pl.BlockSpec((hp,) + shape, lambda i, j: (j, 0, 0))
    return pl.pallas_call(
        functools.partial(_ret_kernel, chunk=chunk),
        grid=(b, RET_HEADS // hp),
        in_specs=[head(COL_RQ), head(COL_RK), head(COL_RV), head(COL_RG),
                  per_head((chunk, chunk)), per_head((chunk, HEAD_DIM)), per_head((chunk, HEAD_DIM)),
                  per_head((1, HEAD_DIM)),
                  pl.BlockSpec((1, w), lambda i, j: (0, j))],
        out_specs=pl.BlockSpec((s, w), lambda i, j: (i, j)),
        out_shape=jax.ShapeDtypeStruct((b * s, RET_WIDTH), bf16),
        compiler_params=_cparams(("parallel", "parallel")),
        name="retention",
    )(proj, proj, proj, proj, decay, xi, zeta, gc, gn_w)


def _branch_merge_kernel(fo_ref, ro_ref, g0_ref, g1_ref, bg_ref, wb_ref, mg_ref, *, ncol):
    d = mg_ref.shape[1]
    cw = d // ncol
    fo = fo_ref[...]
    ro = ro_ref[...]
    for c in range(ncol):
        cs = slice(c * cw, (c + 1) * cw)
        bd0 = jnp.dot(fo, wb_ref[0, :, cs], preferred_element_type=f32)
        bd1 = jnp.dot(ro, wb_ref[1, :, cs], preferred_element_type=f32)
        ga = jax.nn.sigmoid(g0_ref[:, cs].astype(f32) + bg_ref[0:1, cs])
        gb = jax.nn.sigmoid(g1_ref[:, cs].astype(f32) + bg_ref[1:2, cs])
        mg_ref[:, cs] = (ga * bd0 + gb * bd1).astype(mg_ref.dtype)


def _branch_merge(fox_o, ret_o, proj, b_gate, w_branch):
    t = fox_o.shape[0]
    d = D_MODEL
    tm = min(MERGE_ROWS, t)
    gate0 = COL_GATE * HEAD_DIM // d
    return pl.pallas_call(
        functools.partial(_branch_merge_kernel, ncol=4),
        grid=(t // tm,),
        in_specs=[pl.BlockSpec((tm, FOX_WIDTH), lambda i: (i, 0)),
                  pl.BlockSpec((tm, RET_WIDTH), lambda i: (i, 0)),
                  pl.BlockSpec((tm, d), lambda i: (i, gate0)),
                  pl.BlockSpec((tm, d), lambda i: (i, gate0 + 1)),
                  _const_spec((N_BRANCH, d)),
                  _const_spec((N_BRANCH, FOX_WIDTH, d))],
        out_specs=pl.BlockSpec((tm, d), lambda i: (i, 0)),
        out_shape=jax.ShapeDtypeStruct((t, d), bf16),
        compiler_params=_cparams(("parallel",)),
        name="branch_merge",
    )(fox_o, ret_o, proj, proj, b_gate, w_branch)


def _route(logits):
    lane = lax.broadcasted_iota(jnp.int32, logits.shape, 1).astype(f32)
    far = jnp.float32(1e9)
    gmask = lane < N_GROUPS
    gl = jnp.where(gmask, logits, NEG_BIG)
    gmax = jnp.max(gl, axis=-1, keepdims=True)
    gsel = jnp.min(jnp.where(gl == gmax, lane, far), axis=-1, keepdims=True)
    p_group = 1.0 / jnp.sum(jnp.where(gmask, jnp.exp(gl - gmax), 0.0), axis=-1, keepdims=True)
    lo = N_GROUPS + EXPERTS_PER_GROUP * gsel
    emask = (lane >= lo) & (lane < lo + EXPERTS_PER_GROUP)
    el = jnp.where(emask, logits, NEG_BIG)
    m1 = jnp.max(el, axis=-1, keepdims=True)
    i1 = jnp.min(jnp.where(el == m1, lane, far), axis=-1, keepdims=True)
    el2 = jnp.where(lane == i1, NEG_BIG, el)
    m2 = jnp.max(el2, axis=-1, keepdims=True)
    i2 = jnp.min(jnp.where(el2 == m2, lane, far), axis=-1, keepdims=True)
    t = jnp.exp(m2 - m1)
    w1 = p_group / (1.0 + t)
    w2 = p_group * t / (1.0 + t)
    out = jnp.where(lane == 0, i1 - N_GROUPS, 0.0)
    out = jnp.where(lane == 1, i2 - N_GROUPS, out)
    out = jnp.where(lane == 2, w1, out)
    out = jnp.where(lane == 3, w2, out)
    return out


def _outproj_kernel(mg_ref, x_ref, wo_ref, nfw_ref, wr_ref, br_ref, h_ref, hs_ref, route_ref, cnt_ref):
    h = x_ref[...] + jnp.dot(mg_ref[...], wo_ref[...], preferred_element_type=f32)
    h_ref[...] = h
    hn = h * lax.rsqrt(jnp.mean(h * h, axis=-1, keepdims=True) + EPS) * nfw_ref[...]
    _rows_to_slabs(hs_ref, hn, hn.shape[0])
    hi = hn.astype(bf16)
    lo = (hn - hi.astype(f32)).astype(bf16)
    hw = jnp.dot(hi, wr_ref[...], preferred_element_type=f32)
    logits = (hw[:, :LANES] + jnp.dot(lo, wr_ref[:, :LANES], preferred_element_type=f32) + hw[:, LANES:]) + br_ref[...]
    route = _route(logits)
    route_ref[...] = route

    @pl.when(pl.program_id(0) == 0)
    def _():
        cnt_ref[...] = jnp.zeros_like(cnt_ref)

    lane = lax.broadcasted_iota(jnp.int32, route.shape, 1).astype(f32)
    chosen = (lane == route[:, 0:1]).astype(f32) + (lane == route[:, 1:2]).astype(f32)
    cnt_ref[...] += jnp.sum(chosen, axis=0, keepdims=True)


def _outproj_router(merged, x2, w_out, nfw, wr_cat, b_router):
    t, d = x2.shape
    tm = min(OUTPROJ_ROWS, t)
    return pl.pallas_call(
        _outproj_kernel,
        grid=(t // tm,),
        in_specs=[pl.BlockSpec((tm, d), lambda i: (i, 0)),
                  pl.BlockSpec((tm, d), lambda i: (i, 0)),
                  _const_spec((d, d)),
                  _const_spec((1, d)),
                  _const_spec((d, 2 * LANES)), _const_spec((1, LANES))],
        out_specs=[pl.BlockSpec((tm, d), lambda i: (i, 0)),
                   pl.BlockSpec((tm * PITCH, LANES), lambda i: (i, 0)),
                   pl.BlockSpec((tm, LANES), lambda i: (i, 0)),
                   pl.BlockSpec((8, LANES), lambda i: (0, 0))],
        out_shape=[jax.ShapeDtypeStruct((t, d), f32), jax.ShapeDtypeStruct((t * PITCH, LANES), u32),
                   jax.ShapeDtypeStruct((t, LANES), f32), jax.ShapeDtypeStruct((8, LANES), f32)],
        compiler_params=_cparams(("arbitrary",)),
        name="outproj_router",
    )(merged, x2, w_out, nfw, wr_cat, b_router)


def _plan_kernel(route_ref, cnt_ref, dest_ref, ends_ref, carry_ref, *, bm):
    r = route_ref[...]
    tb = r.shape[0]
    lane = lax.broadcasted_iota(jnp.int32, r.shape, 1).astype(f32)
    o0 = (lane == r[:, 0:1]).astype(f32)
    o1 = (lane == r[:, 1:2]).astype(f32)
    o = o0 + o1

    @pl.when(pl.program_id(0) == 0)
    def _():
        cnt = cnt_ref[...]
        padded = jnp.floor((cnt + (bm - 1)) * (1.0 / bm)) * bm
        lane8 = lax.broadcasted_iota(jnp.int32, cnt.shape, 1)
        ends = padded
        sh = 1
        while sh < LANES:
            ends = ends + jnp.where(lane8 >= sh, pltpu.roll(ends, sh, axis=1), 0.0)
            sh *= 2
        sub = lax.broadcasted_iota(jnp.int32, cnt.shape, 0)
        ends_ref[...] = jnp.where(sub == 0, ends, jnp.where(sub == 1, padded, cnt))
        carry_ref[...] = ends - padded

    row = lax.broadcasted_iota(jnp.int32, (tb, tb), 0)
    col = lax.broadcasted_iota(jnp.int32, (tb, tb), 1)
    lower = jnp.where(col < row, 1.0, 0.0).astype(bf16)
    before = jnp.dot(lower, o.astype(bf16), preferred_element_type=f32) + carry_ref[0:1, :]
    d0 = jnp.sum(before * o0, axis=-1, keepdims=True)
    d1 = jnp.sum(before * o1, axis=-1, keepdims=True)
    dd = jnp.where(lane == 0, d0, jnp.where(lane == 1, d1, 0.0))
    dest_ref[0] = dd.T[:8, :].astype(jnp.int32)
    carry_ref[...] += jnp.sum(o, axis=0, keepdims=True)


def _dispatch_plan(route, counts, bm):
    t = route.shape[0]
    tb = min(PLAN_ROWS, t)
    nb = t // tb
    return pl.pallas_call(
        functools.partial(_plan_kernel, bm=bm),
        grid=(nb,),
        in_specs=[pl.BlockSpec((tb, LANES), lambda i: (i, 0)),
                  pl.BlockSpec((8, LANES), lambda i: (0, 0))],
        out_specs=[pl.BlockSpec((1, 8, tb), lambda i: (i, 0, 0)),
                   pl.BlockSpec((8, LANES), lambda i: (0, 0))],
        out_shape=[jax.ShapeDtypeStruct((nb, 8, tb), jnp.int32), jax.ShapeDtypeStruct((8, LANES), f32)],
        scratch_shapes=[pltpu.VMEM((8, LANES), f32)],
        compiler_params=_cparams(("arbitrary",)),
        name="dispatch_plan",
    )(route, counts)


def _dispatch_kernel(d0_ref, d1_ref, fill_ref, end_ref, nu_ref, hs_ref, xs_hbm, zbuf, sem, zsem,
                     *, tb, bm, n_blk, unroll=8):
    i = pl.program_id(0)
    base = i * tb
    blk_rows = bm * PITCH

    def pad_copy(s, n):
        return pltpu.make_async_copy(zbuf.at[pl.ds(0, n * PITCH), :], xs_hbm.at[pl.ds(s * PITCH, n * PITCH), :], zsem)

    def tail_copy(g):
        return pltpu.make_async_copy(zbuf, xs_hbm.at[pl.ds(pl.multiple_of(g * blk_rows, 8), blk_rows), :], zsem)

    @pl.when(i == 0)
    def _():
        zbuf[...] = jnp.zeros(zbuf.shape, u32)
        for op in ("start", "wait"):
            def per_expert(e, c, op=op):
                s = fill_ref[e]
                for n in PAD_FILL_CHUNKS:
                    count = (end_ref[e] - s) // n

                    def one(k, c2, s=s, n=n):
                        getattr(pad_copy(s + k * n, n), op)()
                        return c2

                    c = lax.fori_loop(0, count, one, c)
                    s = s + count * n
                return c

            def per_tail(g, c, op=op):
                getattr(tail_copy(g), op)()
                return c

            lax.fori_loop(0, N_EXPERTS, per_expert, 0)
            lax.fori_loop(nu_ref[0], n_blk, per_tail, 0)

    def body(g, c):
        for u in range(unroll):
            r = g * unroll + u
            src = hs_ref.at[pl.ds(r * PITCH, PITCH), :]
            for k, dref in enumerate((d0_ref, d1_ref)):
                d = dref[base + r]
                pltpu.make_async_copy(src, xs_hbm.at[pl.ds(d * PITCH, PITCH), :], sem).start(priority=k)
        return c

    lax.fori_loop(0, tb // unroll, body, 0)
    for _ in range(TOP_K):
        pltpu.make_async_copy(xs_hbm.at[pl.ds(0, tb * PITCH), :], xs_hbm.at[pl.ds(0, tb * PITCH), :], sem).wait()


def _dispatch_rows(dest0, dest1, fill_start, seg_end, n_used, hs, cap):
    t = dest0.shape[0]
    tb = min(DISPATCH_ROWS, t)
    bm = MOE_BLOCK
    grid_spec = pltpu.PrefetchScalarGridSpec(
        num_scalar_prefetch=5,
        grid=(t // tb,),
        in_specs=[pl.BlockSpec((tb * PITCH, LANES), lambda i, *_: (i, 0))],
        out_specs=pl.BlockSpec(memory_space=pl.ANY),
        scratch_shapes=[pltpu.VMEM((bm * PITCH, LANES), u32), pltpu.SemaphoreType.DMA(()),
                        pltpu.SemaphoreType.DMA(())],
    )
    return pl.pallas_call(
        functools.partial(_dispatch_kernel, tb=tb, bm=bm, n_blk=cap // bm),
        grid_spec=grid_spec,
        out_shape=jax.ShapeDtypeStruct((cap * PITCH, LANES), u32),
        compiler_params=_cparams(("arbitrary",)),
        name="dispatch_rows",
    )(dest0, dest1, fill_start, seg_end, n_used, hs)


def _moe_kernel(g0_ref, nb_ref, nu_ref, x_hbm, w1_ref, w3_ref, w2_ref, y_hbm,
                xbuf, ybuf, w1b, w3b, w2b, gsem, ysem, *, bm, n_blk):
    e = pl.program_id(0)
    n_used = nu_ref[0]
    g0 = g0_ref[e]
    nb = nb_ref[e]
    blk_rows = bm * PITCH

    def x_copy(slot, g):
        src = x_hbm.at[pl.ds(pl.multiple_of(g * blk_rows, 8), blk_rows), :]
        return pltpu.make_async_copy(src, xbuf.at[slot], gsem.at[slot])

    def y_copy(slot, g):
        dst = y_hbm.at[pl.ds(pl.multiple_of(g * blk_rows, 8), blk_rows), :]
        return pltpu.make_async_copy(ybuf.at[slot], dst, ysem.at[slot])

    @pl.when(e == 0)
    def _():
        x_copy(0, 0).start()
        for k in range(1, X_BUFS - 1):
            pl.when(n_used > k)(lambda k=k: x_copy(k, k).start())

    def run_block(g, cast_weights):
        slot = g % 2
        xslot = g % X_BUFS
        ahead = g + (X_BUFS - 1)

        @pl.when(ahead < n_used)
        def _():
            x_copy(ahead % X_BUFS, ahead).start()

        x_copy(xslot, g).wait()

        if cast_weights:
            w1b[...] = w1_ref[0].astype(bf16)
        hn = _slabs_to_rows(xbuf.at[xslot], bm).astype(bf16)
        a = jnp.dot(hn, w1b[...], preferred_element_type=f32)
        if cast_weights:
            w3b[...] = w3_ref[0].astype(bf16)
        b = jnp.dot(hn, w3b[...], preferred_element_type=f32)
        if cast_weights:
            w2b[...] = w2_ref[0].astype(bf16)
        mid = (a * jax.nn.sigmoid(a) * b).astype(bf16)
        y = jnp.dot(mid, w2b[...], preferred_element_type=f32)

        @pl.when(g >= 2)
        def _():
            y_copy(slot, g - 2).wait()

        _rows_to_slabs(ybuf.at[slot], y, bm)
        y_copy(slot, g).start()

    @pl.when(nb > 0)
    def _():
        run_block(g0, True)

    def block(c, carry):
        run_block(g0 + c, False)
        return carry

    lax.fori_loop(1, nb, block, 0)

    @pl.when(e == pl.num_programs(0) - 1)
    def _():
        @pl.when(n_used >= 2)
        def _():
            y_copy(n_used % 2, n_used - 2).wait()

        y_copy((n_used - 1) % 2, n_used - 1).wait()
        ybuf[0] = jnp.zeros(ybuf.shape[1:], u32)

        def zstart(g, carry):
            y_copy(0, g).start()
            return carry

        def zwait(g, carry):
            y_copy(0, g).wait()
            return carry

        lax.fori_loop(n_used, n_blk, zstart, 0)
        lax.fori_loop(n_used, n_blk, zwait, 0)


def _expert_mlp(blk_start, blk_count, n_used, xs, w1, w3, w2, cap):
    d = D_MODEL
    bm = MOE_BLOCK
    n_blk = cap // bm
    by_expert = lambda shape: pl.BlockSpec(shape, lambda e, g0, nb, nu: (e, 0, 0))
    grid_spec = pltpu.PrefetchScalarGridSpec(
        num_scalar_prefetch=3,
        grid=(N_EXPERTS,),
        in_specs=[pl.BlockSpec(memory_space=pl.ANY),
                  by_expert((1, d, D_EXPERT)), by_expert((1, d, D_EXPERT)), by_expert((1, D_EXPERT, d))],
        out_specs=pl.BlockSpec(memory_space=pl.ANY),
        scratch_shapes=[pltpu.VMEM((X_BUFS, bm * PITCH, LANES), u32), pltpu.VMEM((2, bm * PITCH, LANES), u32),
                        pltpu.VMEM((d, D_EXPERT), bf16), pltpu.VMEM((d, D_EXPERT), bf16),
                        pltpu.VMEM((D_EXPERT, d), bf16),
                        pltpu.SemaphoreType.DMA((X_BUFS,)), pltpu.SemaphoreType.DMA((2,))],
    )
    return pl.pallas_call(
        functools.partial(_moe_kernel, bm=bm, n_blk=n_blk),
        grid_spec=grid_spec,
        out_shape=jax.ShapeDtypeStruct((cap * PITCH, LANES), u32),
        compiler_params=_cparams(("arbitrary",)),
        name="expert_mlp",
    )(blk_start, blk_count, n_used, xs, w1, w3, w2)


def _combine_kernel(d0_ref, d1_ref, h_ref, route_ref, y_hbm, nw_ref, o_ref, ybuf, sem, *, tb):
    i = pl.program_id(0)
    n = pl.num_programs(0)
    slot = i % 2

    def start(blk, s):
        _slab_gather_start(d0_ref, blk * tb, tb, y_hbm, ybuf.at[s, 0], sem.at[s])
        _slab_gather_start(d1_ref, blk * tb, tb, y_hbm, ybuf.at[s, 1], sem.at[s])

    @pl.when(i == 0)
    def _():
        start(0, 0)

    @pl.when(i + 1 < n)
    def _():
        start(i + 1, 1 - slot)

    for k in range(TOP_K):
        _slab_gather_wait(tb, y_hbm, ybuf.at[slot, k], sem.at[slot])
    r = route_ref[...]
    h = h_ref[...] + (r[:, 2:3] * _slabs_to_rows(ybuf.at[slot, 0], tb)
                      + r[:, 3:4] * _slabs_to_rows(ybuf.at[slot, 1], tb))
    o_ref[...] = h * lax.rsqrt(jnp.mean(h * h, axis=-1, keepdims=True) + EPS) * nw_ref[...]


def _combine(dest0, dest1, h, route, ys, norm_w):
    t, d = h.shape
    tb = min(COMBINE_ROWS, t)
    grid_spec = pltpu.PrefetchScalarGridSpec(
        num_scalar_prefetch=2,
        grid=(t // tb,),
        in_specs=[pl.BlockSpec((tb, d), lambda i, a, b: (i, 0)),
                  pl.BlockSpec((tb, LANES), lambda i, a, b: (i, 0)),
                  pl.BlockSpec(memory_space=pl.ANY),
                  pl.BlockSpec((1, d), lambda i, a, b: (0, 0))],
        out_specs=pl.BlockSpec((tb, d), lambda i, a, b: (i, 0)),
        scratch_shapes=[pltpu.VMEM((2, TOP_K, tb * PITCH, LANES), u32), pltpu.SemaphoreType.DMA((2,))],
    )
    return pl.pallas_call(
        functools.partial(_combine_kernel, tb=tb),
        grid_spec=grid_spec,
        out_shape=jax.ShapeDtypeStruct((t, d), f32),
        compiler_params=_cparams(("arbitrary",)),
        name="combine_final_norm",
    )(dest0, dest1, h, route, ys, norm_w)


def kernel(x, positions, norm_mix_w, w_in, fox_b_f, ret_gn_w, w_branch, b_gate, w_out, norm_ffn_w, w_router_group,
           b_router_group, w_router_expert, b_router_expert, w1, w3, w2, norm_final_w):
    b, s, d = x.shape
    assert d == D_MODEL and norm_mix_w.shape[0] == 1
    t = b * s
    x2 = x.reshape(t, d)

    wt = w_in[0].T
    o_f = 3 * FOX_WIDTH
    o_r = o_f + FOX_HEADS
    o_g = o_r + 4 * RET_WIDTH
    tn = PROJ_COLS
    tile_rows = ([o_g + k * tn for k in range(N_BRANCH * D_MODEL // tn)]
                 + [k * tn for k in range(o_f // tn)]
                 + [o_r + k * tn for k in range(4 * RET_WIDTH // tn)])
    w_main = _prep_in_weights(wt, tile_rows)
    w_f = jnp.pad(wt[o_f:o_r], ((0, LANES - FOX_HEADS), (0, 0))).astype(bf16)
    bias_f = jnp.pad(fox_b_f[0], (0, LANES - FOX_HEADS)).reshape(1, LANES)
    w_r = jnp.concatenate([w_router_group[0], w_router_expert[0]], axis=1)
    w_r = jnp.pad(w_r, ((0, 0), (0, LANES - w_r.shape[1])))
    wr_hi = w_r.astype(bf16)
    wr_lo = (w_r - wr_hi.astype(f32)).astype(bf16)
    b_r = jnp.pad(jnp.concatenate([b_router_group[0], b_router_expert[0]]), (0, LANES - N_GROUPS - N_EXPERTS))
    b_r = b_r.reshape(1, LANES)

    cos, sin = _rope_tables(positions)
    proj, f_logit = _in_projection(x2, norm_mix_w, w_main, w_f, cos.reshape(t, HEAD_DIM), sin.reshape(t, HEAD_DIM))
    c = _forget_cumsum(f_logit, bias_f, b, s)
    fox_o = _fox_attention(proj, c.reshape(b * FOX_HEADS, 1, s), b, s)
    ret_o = _retention(proj, ret_gn_w, b, s)
    merged = _branch_merge(fox_o, ret_o, proj, b_gate[0], w_branch[0].astype(bf16))
    h, h_slabs, route, counts = _outproj_router(merged, x2, w_out[0].astype(bf16), norm_ffn_w,
                                        jnp.concatenate([wr_hi, wr_lo], axis=1), b_r)

    bm = MOE_BLOCK
    cap = t * TOP_K + N_EXPERTS * bm
    dest_rows, seg = _dispatch_plan(route, counts, bm)
    dest0 = dest_rows[:, 0, :].reshape(t)
    dest1 = dest_rows[:, 1, :].reshape(t)
    pends = seg[0, :N_EXPERTS].astype(jnp.int32)
    padded = seg[1, :N_EXPERTS].astype(jnp.int32)
    counts = seg[2, :N_EXPERTS].astype(jnp.int32)
    pstarts = pends - padded
    n_used = pends[-1:] // bm

    xs = _dispatch_rows(dest0, dest1, pstarts + counts, pends, n_used, h_slabs, cap)
    y = _expert_mlp(pstarts // bm, padded // bm, n_used, xs, w1[0], w3[0], w2[0], cap)
    out = _combine(dest0, dest1, h, route, y, norm_final_w.reshape(1, d))
    return out.reshape(b, s, d)
```
